```python
import math
import jax
import jax.numpy as jnp
from jax import lax
import numpy as np

D_MODEL = 1024
BATCH = 32
SEQ = 2048
DEPTH = 2

GRID_W = 64
CTX_LEN = 256
N_MIXERS = 2
N_A_LAYERS = (DEPTH + N_MIXERS - 1) // N_MIXERS
N_B_LAYERS = DEPTH // N_MIXERS
N_DIR = 2
N_MOD = 6
NORM_EPS = 1e-6

S5_GROUP = 16
S5_GROUPS = D_MODEL // S5_GROUP
S5_STATE = 64
S5_DT_MIN = 1e-3
S5_DT_MAX = 1e-1
S5_LAM_RE_MAX = -1e-4

HGRN_HEAD_K = 128
HGRN_HEADS = D_MODEL // HGRN_HEAD_K
HGRN_HEAD_V = D_MODEL // HGRN_HEADS
HGRN_CHUNK = 32
HGRN_N_PROJ = 5

D_FF = ((8 * D_MODEL // 3 + 127) // 128) * 128
CONV_WIDTH = 3

kernel_name = 'hybrid_s5_hgrn2_prefix_flow_block'


def rms_norm(x, w):
    x32 = x.astype(jnp.float32)
    y = x32 * lax.rsqrt(jnp.mean(x32 * x32, axis=-1, keepdims=True) + NORM_EPS)
    return (y * w.astype(jnp.float32)).astype(x.dtype)


def modulate(h, shift, scale):
    return h * (1.0 + scale[:, None, :]) + shift[:, None, :]


def _dwconv(u, w, b, axis):
    n = u.shape[axis]
    half = CONV_WIDTH // 2
    pad = [(0, 0)] * u.ndim
    pad[axis] = (half, half)
    up = jnp.pad(u, pad)
    out = b
    for j in range(CONV_WIDTH):
        out = out + lax.slice_in_dim(up, j, j + n, axis=axis) * w[j]
    return out


def conv_ffn(h, w_up, conv_w, conv_b, w_down, rows):
    u = h @ w_up
    bn, length, f2 = u.shape
    if rows is None:
        u = _dwconv(u, conv_w, conv_b, axis=1)
    else:
        u = _dwconv(u.reshape(bn, rows, GRID_W, f2), conv_w, conv_b, axis=2).reshape(bn, length, f2)
    a, g = jnp.split(u, 2, axis=-1)
    return (jax.nn.silu(a) * g) @ w_down


def _cplx_combine(e1, e2):
    a1r, a1i, b1r, b1i = e1
    a2r, a2i, b2r, b2i = e2
    return (a2r * a1r - a2i * a1i,
            a2r * a1i + a2i * a1r,
            a2r * b1r - a2i * b1i + b2r,
            a2r * b1i + a2i * b1r + b2i)


def _s5_discretise(lam_re, lam_im, log_step, b_re, b_im):
    lr = jnp.minimum(lam_re.astype(jnp.float32), S5_LAM_RE_MAX)
    li = lam_im.astype(jnp.float32)
    dt = jnp.exp(log_step.astype(jnp.float32))[:, None]
    mag = jnp.exp(lr * dt)
    abar_r = mag * jnp.cos(li * dt)
    abar_i = mag * jnp.sin(li * dt)
    den = lr * lr + li * li
    nr = abar_r - 1.0
    coef_r = (nr * lr + abar_i * li) / den
    coef_i = (abar_i * lr - nr * li) / den
    br = b_re.astype(jnp.float32)
    bi = b_im.astype(jnp.float32)
    bbar_r = coef_r[..., None] * br - coef_i[..., None] * bi
    bbar_i = coef_r[..., None] * bi + coef_i[..., None] * br
    return abar_r, abar_i, bbar_r, bbar_i


def _s5_scan(u, s0_r, s0_i, abar_r, abar_i, bbar_r, bbar_i, reverse):
    length = u.shape[1]
    bu_r = jnp.einsum('blgh,gph->blgp', u, bbar_r)
    bu_i = jnp.einsum('blgh,gph->blgp', u, bbar_i)
    a_r = jnp.broadcast_to(abar_r[None, None], (1, length) + abar_r.shape)
    a_i = jnp.broadcast_to(abar_i[None, None], (1, length) + abar_i.shape)
    acr, aci, sr, si = lax.associative_scan(_cplx_combine, (a_r, a_i, bu_r, bu_i), reverse=reverse, axis=1)
    s0r = s0_r[:, None]
    s0i = s0_i[:, None]
    sr = sr + acr * s0r - aci * s0i
    si = si + acr * s0i + aci * s0r
    end = 0 if reverse else length - 1
    return sr, si, sr[:, end], si[:, end]


def _s5_readout(sr, si, c_re, c_im):
    return (jnp.einsum('blgp,ghp->blgh', sr, c_re.astype(jnp.float32))
            - jnp.einsum('blgp,ghp->blgh', si, c_im.astype(jnp.float32)))


def s5_mixer(h_ctx, h_lat, w_in, lam_re, lam_im, log_step, b_re, b_im, c_re, c_im, d_skip, w_glu, w_out, need_ctx_out):
    dtype = h_lat.dtype
    bn = h_lat.shape[0]

    def inputs(h):
        return (h @ w_in).astype(jnp.float32).reshape(h.shape[0], h.shape[1], S5_GROUPS, S5_GROUP)

    def glu_out(y):
        z = jax.nn.gelu(y.reshape(y.shape[0], y.shape[1], D_MODEL)).astype(dtype)
        z = z * jax.nn.sigmoid(z @ w_glu)
        return z @ w_out

    u_ctx = inputs(h_ctx)
    u_lat = inputs(h_lat)
    dsk = d_skip.astype(jnp.float32).reshape(S5_GROUPS, S5_GROUP)
    zeros = jnp.zeros((bn, S5_GROUPS, S5_STATE), jnp.float32)
    y_lat = dsk * u_lat
    y_ctx = dsk * u_ctx if need_ctx_out else None
    for d in range(N_DIR):
        rev = d == 1
        disc = _s5_discretise(lam_re[d], lam_im[d], log_step[d], b_re[d], b_im[d])
        sr, si, fr, fi = _s5_scan(u_ctx, zeros, zeros, *disc, reverse=rev)
        if need_ctx_out:
            y_ctx = y_ctx + _s5_readout(sr, si, c_re[d], c_im[d])
        sr, si, _, _ = _s5_scan(u_lat, fr, fi, *disc, reverse=rev)
        y_lat = y_lat + _s5_readout(sr, si, c_re[d], c_im[d])
    out_ctx = glu_out(y_ctx) if need_ctx_out else None
    return out_ctx, glu_out(y_lat)


def _hgrn2_scan(q, k, v, logf, s0, reverse):
    if reverse:
        q, k, v, logf = (jnp.flip(t, axis=1) for t in (q, k, v, logf))
    bn, length, hh, _ = q.shape
    n_chunks = length // HGRN_CHUNK

    def chunks(t):
        return jnp.moveaxis(t.reshape(bn, n_chunks, HGRN_CHUNK, hh, t.shape[-1]), 1, 0)

    tri = jnp.tril(jnp.ones((HGRN_CHUNK, HGRN_CHUNK), dtype=bool))

    def step(state, xs):
        qc, kc, vc, fc = xs
        b = jnp.cumsum(fc, axis=1)
        b_end = b[:, -1]
        q_dec = qc * jnp.exp(b)
        att = jnp.einsum('bthk,bshk->bhts', q_dec, kc * jnp.exp(-b))
        att = jnp.where(tri, att, 0.0)
        o = jnp.einsum('bhts,bshv->bthv', att, vc) + jnp.einsum('bthk,bhkv->bthv', q_dec, state)
        k_end = kc * jnp.exp(b_end[:, None] - b)
        state = jnp.exp(b_end)[..., None] * state + jnp.einsum('bshk,bshv->bhkv', k_end, vc)
        return state, o

    s_fin, o = lax.scan(step, s0, (chunks(q), chunks(k), chunks(v), chunks(logf)))
    o = jnp.moveaxis(o, 0, 1).reshape(bn, length, hh, v.shape[-1])
    if reverse:
        o = jnp.flip(o, axis=1)
    return o, s_fin


def hgrn2_mixer(h_ctx, h_lat, w_in, lb, gnorm_w, w_out, need_ctx_out):
    dtype = h_lat.dtype
    bn = h_lat.shape[0]

    def project(h):
        z = (h @ w_in).astype(jnp.float32)
        q, i, f_fwd, f_bwd, g = jnp.split(z, HGRN_N_PROJ, axis=-1)
        heads = lambda t: t.reshape(h.shape[0], h.shape[1], HGRN_HEADS, -1)
        return heads(q), heads(i), (heads(f_fwd), heads(f_bwd)), heads(g)

    def gates(fraw, lb_d):
        lbh = lb_d.reshape(HGRN_HEADS, HGRN_HEAD_K)
        logf = jnp.logaddexp(jnp.log(lbh), jnp.log1p(-lbh) + jax.nn.log_sigmoid(fraw))
        return logf, (1.0 - lbh) * jax.nn.sigmoid(-fraw)

    def readout(o, g):
        o = o * lax.rsqrt(jnp.mean(o * o, axis=-1, keepdims=True) + NORM_EPS) * gnorm_w.astype(jnp.float32)
        o = o * jax.nn.sigmoid(g)
        return o.reshape(o.shape[0], o.shape[1], D_MODEL).astype(dtype) @ w_out

    q_c, v_c, f_c, g_c = project(h_ctx)
    q_l, v_l, f_l, g_l = project(h_lat)
    zeros = jnp.zeros((bn, HGRN_HEADS, HGRN_HEAD_K, HGRN_HEAD_V), jnp.float32)
    o_ctx = None
    o_lat = None
    for d in range(N_DIR):
        rev = d == 1
        logf_c, k_c = gates(f_c[d], lb[d])
        logf_l, k_l = gates(f_l[d], lb[d])
        oc, s_ctx = _hgrn2_scan(q_c, k_c, v_c, logf_c, zeros, rev)
        ol, _ = _hgrn2_scan(q_l, k_l, v_l, logf_l, s_ctx, rev)
        o_lat = ol if o_lat is None else o_lat + ol
        if need_ctx_out:
            o_ctx = oc if o_ctx is None else o_ctx + oc
    out_ctx = readout(o_ctx, g_c) if need_ctx_out else None
    return out_ctx, readout(o_lat, g_l)


def _fwd_setup_inputs(seed: int = 0) -> dict:
    key = jax.random.key(seed)
    ks = jax.random.split(key, 28)
    f32 = jnp.float32
    D = D_MODEL

    def nrm(k, shape, s):
        return s * jax.random.normal(k, shape, f32)

    s5_shape = (N_A_LAYERS, N_DIR, S5_GROUPS, S5_STATE)
    lam_im0 = jnp.pi * jnp.arange(S5_STATE, dtype=f32)
    return {
        'x': nrm(ks[0], (BATCH, SEQ, D), 1.0),
        'c': nrm(ks[1], (BATCH, D), 1.0),
        'ctx': nrm(ks[2], (BATCH, CTX_LEN, D), 1.0),
        'c_ctx': nrm(ks[3], (D,), 1.0),
        'w_mod': nrm(ks[4], (DEPTH, D, N_MOD * D), 0.02),
        'b_mod': nrm(ks[5], (DEPTH, N_MOD * D), 0.02),
        'norm1_w': 1.0 + nrm(ks[6], (DEPTH, D), 0.01),
        'norm2_w': 1.0 + nrm(ks[7], (DEPTH, D), 0.01),
        'final_norm_w': 1.0 + nrm(ks[8], (D,), 0.01),
        's5_w_in': nrm(ks[9], (N_A_LAYERS, D, D), D ** -0.5),
        's5_lam_re': -0.5 + nrm(ks[10], s5_shape, 0.01),
        's5_lam_im': lam_im0 + nrm(ks[11], s5_shape, 0.01),
        's5_log_step': jax.random.uniform(ks[12], (N_A_LAYERS, N_DIR, S5_GROUPS), f32, minval=math.log(S5_DT_MIN), maxval=math.log(S5_DT_MAX)),
        's5_b_re': nrm(ks[13], s5_shape + (S5_GROUP,), (0.5 / S5_GROUP) ** 0.5),
        's5_b_im': nrm(ks[14], s5_shape + (S5_GROUP,), (0.5 / S5_GROUP) ** 0.5),
        's5_c_re': nrm(ks[15], (N_A_LAYERS, N_DIR, S5_GROUPS, S5_GROUP, S5_STATE), (1.0 / S5_STATE) ** 0.5),
        's5_c_im': nrm(ks[16], (N_A_LAYERS, N_DIR, S5_GROUPS, S5_GROUP, S5_STATE), (1.0 / S5_STATE) ** 0.5),
        's5_d': nrm(ks[17], (N_A_LAYERS, D), 1.0),
        's5_w_glu': nrm(ks[18], (N_A_LAYERS, D, D), D ** -0.5),
        's5_w_out': nrm(ks[19], (N_A_LAYERS, D, D), D ** -0.5),
        'hg_w_in': nrm(ks[20], (N_B_LAYERS, D, HGRN_N_PROJ * D), D ** -0.5),
        'hg_lower_bounds': nrm(ks[21], (N_DIR, DEPTH, D), 0.1),
        'hg_gnorm_w': 1.0 + nrm(ks[22], (N_B_LAYERS, HGRN_HEAD_V), 0.01),
        'hg_w_out': nrm(ks[23], (N_B_LAYERS, D, D), D ** -0.5),
        'ffn_w_up': nrm(ks[24], (DEPTH, D, 2 * D_FF), D ** -0.5),
        'ffn_conv_w': nrm(ks[25], (DEPTH, CONV_WIDTH, 2 * D_FF), 0.5),
        'ffn_conv_b': nrm(ks[26], (DEPTH, 2 * D_FF), 0.02),
        'ffn_w_down': nrm(ks[27], (DEPTH, D_FF, D), D_FF ** -0.5),
    }


def _fwd_reference(x, c, ctx, c_ctx, w_mod, b_mod, norm1_w, norm2_w, final_norm_w,
              s5_w_in, s5_lam_re, s5_lam_im, s5_log_step, s5_b_re, s5_b_im, s5_c_re, s5_c_im, s5_d, s5_w_glu, s5_w_out,
              hg_w_in, hg_lower_bounds, hg_gnorm_w, hg_w_out,
              ffn_w_up, ffn_conv_w, ffn_conv_b, ffn_w_down):
    rows = x.shape[1] // GRID_W
    lb_all = jax.nn.softmax(hg_lower_bounds.astype(jnp.float32), axis=1)
    lb_all = jnp.cumsum(lb_all, axis=1) - lb_all[:, :1]
    sc_lat = jax.nn.silu(c)
    sc_ctx = jax.nn.silu(c_ctx)[None, :]
    for layer in range(DEPTH):
        last = layer == DEPTH - 1
        m_lat = jnp.split(sc_lat @ w_mod[layer] + b_mod[layer], N_MOD, axis=-1)
        m_ctx = jnp.split(sc_ctx @ w_mod[layer] + b_mod[layer], N_MOD, axis=-1)
        h_lat = modulate(rms_norm(x, norm1_w[layer]), m_lat[0], m_lat[1])
        h_ctx = modulate(rms_norm(ctx, norm1_w[layer]), m_ctx[0], m_ctx[1])
        j = layer // N_MIXERS
        if layer % N_MIXERS == 0:
            y_ctx, y_lat = s5_mixer(h_ctx, h_lat, s5_w_in[j], s5_lam_re[j], s5_lam_im[j], s5_log_step[j],
                                    s5_b_re[j], s5_b_im[j], s5_c_re[j], s5_c_im[j], s5_d[j],
                                    s5_w_glu[j], s5_w_out[j], not last)
        else:
            y_ctx, y_lat = hgrn2_mixer(h_ctx, h_lat, hg_w_in[j], lb_all[:, layer], hg_gnorm_w[j],
                                       hg_w_out[j], not last)
        x = x + m_lat[2][:, None, :] * y_lat
        f_lat = conv_ffn(modulate(rms_norm(x, norm2_w[layer]), m_lat[3], m_lat[4]),
                         ffn_w_up[layer], ffn_conv_w[layer], ffn_conv_b[layer], ffn_w_down[layer], rows)
        x = x + m_lat[5][:, None, :] * f_lat
        if not last:
            ctx = ctx + m_ctx[2][:, None, :] * y_ctx
            f_ctx = conv_ffn(modulate(rms_norm(ctx, norm2_w[layer]), m_ctx[3], m_ctx[4]),
                             ffn_w_up[layer], ffn_conv_w[layer], ffn_conv_b[layer], ffn_w_down[layer], None)
            ctx = ctx + m_ctx[5][:, None, :] * f_ctx
    return rms_norm(x, final_norm_w)


import jax as _jax
import jax.numpy as _jnp

TWIN_FORMAT = 'train_step'
FWD_PARAMS = ['x', 'c', 'ctx', 'c_ctx', 'w_mod', 'b_mod', 'norm1_w', 'norm2_w', 'final_norm_w', 's5_w_in', 's5_lam_re', 's5_lam_im', 's5_log_step', 's5_b_re', 's5_b_im', 's5_c_re', 's5_c_im', 's5_d', 's5_w_glu', 's5_w_out', 'hg_w_in', 'hg_lower_bounds', 'hg_gnorm_w', 'hg_w_out', 'ffn_w_up', 'ffn_conv_w', 'ffn_conv_b', 'ffn_w_down']
TWIN_WEIGHTS = ['c_ctx', 'w_mod', 'b_mod', 'norm1_w', 'norm2_w', 'final_norm_w', 's5_w_in', 's5_lam_re', 's5_lam_im', 's5_log_step', 's5_b_re', 's5_b_im', 's5_c_re', 's5_c_im', 's5_d', 's5_w_glu', 's5_w_out', 'hg_w_in', 'hg_lower_bounds', 'hg_gnorm_w', 'hg_w_out', 'ffn_w_up', 'ffn_conv_w', 'ffn_conv_b', 'ffn_w_down']
TWIN_DIFF_INPUT = 'x'
TWIN_INPUTS = ['x', 'c', 'ctx', 'c_ctx', 'w_mod', 'b_mod', 'norm1_w', 'norm2_w', 'final_norm_w', 's5_w_in', 's5_lam_re', 's5_lam_im', 's5_log_step', 's5_b_re', 's5_b_im', 's5_c_re', 's5_c_im', 's5_d', 's5_w_glu', 's5_w_out', 'hg_w_in', 'hg_lower_bounds', 'hg_gnorm_w', 'hg_w_out', 'ffn_w_up', 'ffn_conv_w', 'ffn_conv_b', 'ffn_w_down', 'loss_target', 'm_c_ctx', 'm_w_mod', 'm_b_mod', 'm_norm1_w', 'm_norm2_w', 'm_final_norm_w', 'm_s5_w_in', 'm_s5_lam_re', 'm_s5_lam_im', 'm_s5_log_step', 'm_s5_b_re', 'm_s5_b_im', 'm_s5_c_re', 'm_s5_c_im', 'm_s5_d', 'm_s5_w_glu', 'm_s5_w_out', 'm_hg_w_in', 'm_hg_lower_bounds', 'm_hg_gnorm_w', 'm_hg_w_out', 'm_ffn_w_up', 'm_ffn_conv_w', 'm_ffn_conv_b', 'm_ffn_w_down', 'v_c_ctx', 'v_w_mod', 'v_b_mod', 'v_norm1_w', 'v_norm2_w', 'v_final_norm_w', 'v_s5_w_in', 'v_s5_lam_re', 'v_s5_lam_im', 'v_s5_log_step', 'v_s5_b_re', 'v_s5_b_im', 'v_s5_c_re', 'v_s5_c_im', 'v_s5_d', 'v_s5_w_glu', 'v_s5_w_out', 'v_hg_w_in', 'v_hg_lower_bounds', 'v_hg_gnorm_w', 'v_hg_w_out', 'v_ffn_w_up', 'v_ffn_conv_w', 'v_ffn_conv_b', 'v_ffn_w_down']
TWIN_OUTPUTS = ['loss', 'grad_x', 'grad_c_ctx', 'grad_w_mod', 'grad_b_mod', 'grad_norm1_w', 'grad_norm2_w', 'grad_final_norm_w', 'grad_s5_w_in', 'grad_s5_lam_re', 'grad_s5_lam_im', 'grad_s5_log_step', 'grad_s5_b_re', 'grad_s5_b_im', 'grad_s5_c_re', 'grad_s5_c_im', 'grad_s5_d', 'grad_s5_w_glu', 'grad_s5_w_out', 'grad_hg_w_in', 'grad_hg_lower_bounds', 'grad_hg_gnorm_w', 'grad_hg_w_out', 'grad_ffn_w_up', 'grad_ffn_conv_w', 'grad_ffn_conv_b', 'grad_ffn_w_down', 'delta_c_ctx', 'delta_w_mod', 'delta_b_mod', 'delta_norm1_w', 'delta_norm2_w', 'delta_final_norm_w', 'delta_s5_w_in', 'delta_s5_lam_re', 'delta_s5_lam_im', 'delta_s5_log_step', 'delta_s5_b_re', 'delta_s5_b_im', 'delta_s5_c_re', 'delta_s5_c_im', 'delta_s5_d', 'delta_s5_w_glu', 'delta_s5_w_out', 'delta_hg_w_in', 'delta_hg_lower_bounds', 'delta_hg_gnorm_w', 'delta_hg_w_out', 'delta_ffn_w_up', 'delta_ffn_conv_w', 'delta_ffn_conv_b', 'delta_ffn_w_down', 'new_m_c_ctx', 'new_m_w_mod', 'new_m_b_mod', 'new_m_norm1_w', 'new_m_norm2_w', 'new_m_final_norm_w', 'new_m_s5_w_in', 'new_m_s5_lam_re', 'new_m_s5_lam_im', 'new_m_s5_log_step', 'new_m_s5_b_re', 'new_m_s5_b_im', 'new_m_s5_c_re', 'new_m_s5_c_im', 'new_m_s5_d', 'new_m_s5_w_glu', 'new_m_s5_w_out', 'new_m_hg_w_in', 'new_m_hg_lower_bounds', 'new_m_hg_gnorm_w', 'new_m_hg_w_out', 'new_m_ffn_w_up', 'new_m_ffn_conv_w', 'new_m_ffn_conv_b', 'new_m_ffn_w_down', 'new_v_c_ctx', 'new_v_w_mod', 'new_v_b_mod', 'new_v_norm1_w', 'new_v_norm2_w', 'new_v_final_norm_w', 'new_v_s5_w_in', 'new_v_s5_lam_re', 'new_v_s5_lam_im', 'new_v_s5_log_step', 'new_v_s5_b_re', 'new_v_s5_b_im', 'new_v_s5_c_re', 'new_v_s5_c_im', 'new_v_s5_d', 'new_v_s5_w_glu', 'new_v_s5_w_out', 'new_v_hg_w_in', 'new_v_hg_lower_bounds', 'new_v_hg_gnorm_w', 'new_v_hg_w_out', 'new_v_ffn_w_up', 'new_v_ffn_conv_w', 'new_v_ffn_conv_b', 'new_v_ffn_w_down']
TWIN_LEAF_KINDS = {'loss': 'loss', 'grad_x': 'grad_x', 'grad_c_ctx': 'grad_w', 'grad_w_mod': 'grad_w', 'grad_b_mod': 'grad_w', 'grad_norm1_w': 'grad_w', 'grad_norm2_w': 'grad_w', 'grad_final_norm_w': 'grad_w', 'grad_s5_w_in': 'grad_w', 'grad_s5_lam_re': 'grad_w', 'grad_s5_lam_im': 'grad_w', 'grad_s5_log_step': 'grad_w', 'grad_s5_b_re': 'grad_w', 'grad_s5_b_im': 'grad_w', 'grad_s5_c_re': 'grad_w', 'grad_s5_c_im': 'grad_w', 'grad_s5_d': 'grad_w', 'grad_s5_w_glu': 'grad_w', 'grad_s5_w_out': 'grad_w', 'grad_hg_w_in': 'grad_w', 'grad_hg_lower_bounds': 'grad_w', 'grad_hg_gnorm_w': 'grad_w', 'grad_hg_w_out': 'grad_w', 'grad_ffn_w_up': 'grad_w', 'grad_ffn_conv_w': 'grad_w', 'grad_ffn_conv_b': 'grad_w', 'grad_ffn_w_down': 'grad_w', 'delta_c_ctx': 'delta_w', 'delta_w_mod': 'delta_w', 'delta_b_mod': 'delta_w', 'delta_norm1_w': 'delta_w', 'delta_norm2_w': 'delta_w', 'delta_final_norm_w': 'delta_w', 'delta_s5_w_in': 'delta_w', 'delta_s5_lam_re': 'delta_w', 'delta_s5_lam_im': 'delta_w', 'delta_s5_log_step': 'delta_w', 'delta_s5_b_re': 'delta_w', 'delta_s5_b_im': 'delta_w', 'delta_s5_c_re': 'delta_w', 'delta_s5_c_im': 'delta_w', 'delta_s5_d': 'delta_w', 'delta_s5_w_glu': 'delta_w', 'delta_s5_w_out': 'delta_w', 'delta_hg_w_in': 'delta_w', 'delta_hg_lower_bounds': 'delta_w', 'delta_hg_gnorm_w': 'delta_w', 'delta_hg_w_out': 'delta_w', 'delta_ffn_w_up': 'delta_w', 'delta_ffn_conv_w': 'delta_w', 'delta_ffn_conv_b': 'delta_w', 'delta_ffn_w_down': 'delta_w', 'new_m_c_ctx': 'new_m', 'new_m_w_mod': 'new_m', 'new_m_b_mod': 'new_m', 'new_m_norm1_w': 'new_m', 'new_m_norm2_w': 'new_m', 'new_m_final_norm_w': 'new_m', 'new_m_s5_w_in': 'new_m', 'new_m_s5_lam_re': 'new_m', 'new_m_s5_lam_im': 'new_m', 'new_m_s5_log_step': 'new_m', 'new_m_s5_b_re': 'new_m', 'new_m_s5_b_im': 'new_m', 'new_m_s5_c_re': 'new_m', 'new_m_s5_c_im': 'new_m', 'new_m_s5_d': 'new_m', 'new_m_s5_w_glu': 'new_m', 'new_m_s5_w_out': 'new_m', 'new_m_hg_w_in': 'new_m', 'new_m_hg_lower_bounds': 'new_m', 'new_m_hg_gnorm_w': 'new_m', 'new_m_hg_w_out': 'new_m', 'new_m_ffn_w_up': 'new_m', 'new_m_ffn_conv_w': 'new_m', 'new_m_ffn_conv_b': 'new_m', 'new_m_ffn_w_down': 'new_m', 'new_v_c_ctx': 'new_v', 'new_v_w_mod': 'new_v', 'new_v_b_mod': 'new_v', 'new_v_norm1_w': 'new_v', 'new_v_norm2_w': 'new_v', 'new_v_final_norm_w': 'new_v', 'new_v_s5_w_in': 'new_v', 'new_v_s5_lam_re': 'new_v', 'new_v_s5_lam_im': 'new_v', 'new_v_s5_log_step': 'new_v', 'new_v_s5_b_re': 'new_v', 'new_v_s5_b_im': 'new_v', 'new_v_s5_c_re': 'new_v', 'new_v_s5_c_im': 'new_v', 'new_v_s5_d': 'new_v', 'new_v_s5_w_glu': 'new_v', 'new_v_s5_w_out': 'new_v', 'new_v_hg_w_in': 'new_v', 'new_v_hg_lower_bounds': 'new_v', 'new_v_hg_gnorm_w': 'new_v', 'new_v_hg_w_out': 'new_v', 'new_v_ffn_w_up': 'new_v', 'new_v_ffn_conv_w': 'new_v', 'new_v_ffn_conv_b': 'new_v', 'new_v_ffn_w_down': 'new_v'}


def _forward(args):
    return _fwd_reference(*[args[k] for k in FWD_PARAMS])


def _output_shape():
    out = _jax.eval_shape(lambda: _forward(_fwd_setup_inputs(0)))
    return out.shape, out.dtype

N_MICROBATCH = 1
ADAM_LR = 0.001
ADAM_B1 = 0.9
ADAM_B2 = 0.999
ADAM_EPS = 1e-08
ADAM_WD = 0.01
ADAM_STEP = 10
PER_EXAMPLE_BATCH_AXIS = {'x': 0, 'c': 0, 'ctx': 0, 'loss_target': 0}
SHARED_INPUTS = []
_WEIGHT_DTYPES = {'c_ctx': _jnp.float32, 'w_mod': _jnp.float32, 'b_mod': _jnp.float32, 'norm1_w': _jnp.float32, 'norm2_w': _jnp.float32, 'final_norm_w': _jnp.float32, 's5_w_in': _jnp.float32, 's5_lam_re': _jnp.float32, 's5_lam_im': _jnp.float32, 's5_log_step': _jnp.float32, 's5_b_re': _jnp.float32, 's5_b_im': _jnp.float32, 's5_c_re': _jnp.float32, 's5_c_im': _jnp.float32, 's5_d': _jnp.float32, 's5_w_glu': _jnp.float32, 's5_w_out': _jnp.float32, 'hg_w_in': _jnp.float32, 'hg_lower_bounds': _jnp.float32, 'hg_gnorm_w': _jnp.float32, 'hg_w_out': _jnp.float32, 'ffn_w_up': _jnp.float32, 'ffn_conv_w': _jnp.float32, 'ffn_conv_b': _jnp.float32, 'ffn_w_down': _jnp.float32}
MOMENT_SCALE = {'c_ctx': 3.515733e-03, 'w_mod': 6.173879e-02, 'b_mod': 1.018899e-01, 'norm1_w': 8.197667e-02, 'norm2_w': 7.763938e-02, 'final_norm_w': 6.404253e+01, 's5_w_in': 4.770562e-02, 's5_lam_re': 4.928936e-03, 's5_lam_im': 5.442293e-03, 's5_log_step': 1.766036e+00, 's5_b_re': 2.836411e-03, 's5_b_im': 3.058190e-03, 's5_c_re': 4.097842e-03, 's5_c_im': 4.118233e-03, 's5_d': 4.965563e-02, 's5_w_glu': 1.454534e-02, 's5_w_out': 4.404927e-02, 'hg_w_in': 4.998301e-02, 'hg_lower_bounds': 2.036584e-02, 'hg_gnorm_w': 1.369444e-01, 'hg_w_out': 5.195235e-02, 'ffn_w_up': 3.560964e-02, 'ffn_conv_w': 4.132832e-02, 'ffn_conv_b': 3.561375e-02, 'ffn_w_down': 5.811871e-02}


def _to_microbatches(a, axis):
    t = _jnp.moveaxis(a, axis, 0)
    t = t.reshape((N_MICROBATCH, t.shape[0] // N_MICROBATCH) + t.shape[1:])
    return _jnp.moveaxis(t, 1, axis + 1)


def setup_inputs(seed: int = 0) -> dict:
    inp = _fwd_setup_inputs(seed)
    key = _jax.random.fold_in(_jax.random.key(seed), 7919)
    shape, _ = _output_shape()
    out = dict(inp)
    out["loss_target"] = _jax.random.normal(_jax.random.fold_in(key, 0), shape, _jnp.float32)
    for i, name in enumerate(TWIN_WEIGHTS):
        w = inp[name].astype(_jnp.float32)
        if MOMENT_SCALE is None:
            s = _jnp.sqrt(_jnp.mean(_jnp.square(w)) + 1e-30)
        else:
            s = MOMENT_SCALE[name]
        km, kv = _jax.random.split(_jax.random.fold_in(key, i + 1))
        out[name] = w
        out["m_" + name] = s * _jax.random.normal(km, w.shape, _jnp.float32)
        out["v_" + name] = (s * s) * _jax.random.uniform(kv, w.shape, _jnp.float32, 0.5, 1.5)
    if N_MICROBATCH > 1:
        for name, axis in PER_EXAMPLE_BATCH_AXIS.items():
            out[name] = _to_microbatches(out[name], axis)
    return {'x': out['x'], 'c': out['c'], 'ctx': out['ctx'], 'c_ctx': out['c_ctx'], 'w_mod': out['w_mod'], 'b_mod': out['b_mod'], 'norm1_w': out['norm1_w'], 'norm2_w': out['norm2_w'], 'final_norm_w': out['final_norm_w'], 's5_w_in': out['s5_w_in'], 's5_lam_re': out['s5_lam_re'], 's5_lam_im': out['s5_lam_im'], 's5_log_step': out['s5_log_step'], 's5_b_re': out['s5_b_re'], 's5_b_im': out['s5_b_im'], 's5_c_re': out['s5_c_re'], 's5_c_im': out['s5_c_im'], 's5_d': out['s5_d'], 's5_w_glu': out['s5_w_glu'], 's5_w_out': out['s5_w_out'], 'hg_w_in': out['hg_w_in'], 'hg_lower_bounds': out['hg_lower_bounds'], 'hg_gnorm_w': out['hg_gnorm_w'], 'hg_w_out': out['hg_w_out'], 'ffn_w_up': out['ffn_w_up'], 'ffn_conv_w': out['ffn_conv_w'], 'ffn_conv_b': out['ffn_conv_b'], 'ffn_w_down': out['ffn_w_down'], 'loss_target': out['loss_target'], 'm_c_ctx': out['m_c_ctx'], 'm_w_mod': out['m_w_mod'], 'm_b_mod': out['m_b_mod'], 'm_norm1_w': out['m_norm1_w'], 'm_norm2_w': out['m_norm2_w'], 'm_final_norm_w': out['m_final_norm_w'], 'm_s5_w_in': out['m_s5_w_in'], 'm_s5_lam_re': out['m_s5_lam_re'], 'm_s5_lam_im': out['m_s5_lam_im'], 'm_s5_log_step': out['m_s5_log_step'], 'm_s5_b_re': out['m_s5_b_re'], 'm_s5_b_im': out['m_s5_b_im'], 'm_s5_c_re': out['m_s5_c_re'], 'm_s5_c_im': out['m_s5_c_im'], 'm_s5_d': out['m_s5_d'], 'm_s5_w_glu': out['m_s5_w_glu'], 'm_s5_w_out': out['m_s5_w_out'], 'm_hg_w_in': out['m_hg_w_in'], 'm_hg_lower_bounds': out['m_hg_lower_bounds'], 'm_hg_gnorm_w': out['m_hg_gnorm_w'], 'm_hg_w_out': out['m_hg_w_out'], 'm_ffn_w_up': out['m_ffn_w_up'], 'm_ffn_conv_w': out['m_ffn_conv_w'], 'm_ffn_conv_b': out['m_ffn_conv_b'], 'm_ffn_w_down': out['m_ffn_w_down'], 'v_c_ctx': out['v_c_ctx'], 'v_w_mod': out['v_w_mod'], 'v_b_mod': out['v_b_mod'], 'v_norm1_w': out['v_norm1_w'], 'v_norm2_w': out['v_norm2_w'], 'v_final_norm_w': out['v_final_norm_w'], 'v_s5_w_in': out['v_s5_w_in'], 'v_s5_lam_re': out['v_s5_lam_re'], 'v_s5_lam_im': out['v_s5_lam_im'], 'v_s5_log_step': out['v_s5_log_step'], 'v_s5_b_re': out['v_s5_b_re'], 'v_s5_b_im': out['v_s5_b_im'], 'v_s5_c_re': out['v_s5_c_re'], 'v_s5_c_im': out['v_s5_c_im'], 'v_s5_d': out['v_s5_d'], 'v_s5_w_glu': out['v_s5_w_glu'], 'v_s5_w_out': out['v_s5_w_out'], 'v_hg_w_in': out['v_hg_w_in'], 'v_hg_lower_bounds': out['v_hg_lower_bounds'], 'v_hg_gnorm_w': out['v_hg_gnorm_w'], 'v_hg_w_out': out['v_hg_w_out'], 'v_ffn_w_up': out['v_ffn_w_up'], 'v_ffn_conv_w': out['v_ffn_conv_w'], 'v_ffn_conv_b': out['v_ffn_conv_b'], 'v_ffn_w_down': out['v_ffn_w_down']}


def _loss(weights, diff, rest, loss_target):
    with _jax.named_scope("forward"):
        args = {**rest, TWIN_DIFF_INPUT: diff, **{k: w.astype(_WEIGHT_DTYPES[k]) for k, w in weights.items()}}
        y = _forward(args)
    with _jax.named_scope("loss_head"):
        err = _jnp.square(y.astype(_jnp.float32) - loss_target)
        return 0.5 * _jnp.sum(_jnp.mean(err, axis=-1)) if err.ndim else 0.5 * err


def _adamw(w, g, m, v):
    m = ADAM_B1 * m + (1.0 - ADAM_B1) * g
    v = ADAM_B2 * v + (1.0 - ADAM_B2) * _jnp.square(g)
    m_hat = m / (1.0 - ADAM_B1 ** ADAM_STEP)
    v_hat = v / (1.0 - ADAM_B2 ** ADAM_STEP)
    delta = -ADAM_LR * (m_hat / (_jnp.sqrt(v_hat) + ADAM_EPS) + ADAM_WD * w)
    return delta, m, v


def reference(x, c, ctx, c_ctx, w_mod, b_mod, norm1_w, norm2_w, final_norm_w, s5_w_in, s5_lam_re, s5_lam_im, s5_log_step, s5_b_re, s5_b_im, s5_c_re, s5_c_im, s5_d, s5_w_glu, s5_w_out, hg_w_in, hg_lower_bounds, hg_gnorm_w, hg_w_out, ffn_w_up, ffn_conv_w, ffn_conv_b, ffn_w_down, loss_target, m_c_ctx, m_w_mod, m_b_mod, m_norm1_w, m_norm2_w, m_final_norm_w, m_s5_w_in, m_s5_lam_re, m_s5_lam_im, m_s5_log_step, m_s5_b_re, m_s5_b_im, m_s5_c_re, m_s5_c_im, m_s5_d, m_s5_w_glu, m_s5_w_out, m_hg_w_in, m_hg_lower_bounds, m_hg_gnorm_w, m_hg_w_out, m_ffn_w_up, m_ffn_conv_w, m_ffn_conv_b, m_ffn_w_down, v_c_ctx, v_w_mod, v_b_mod, v_norm1_w, v_norm2_w, v_final_norm_w, v_s5_w_in, v_s5_lam_re, v_s5_lam_im, v_s5_log_step, v_s5_b_re, v_s5_b_im, v_s5_c_re, v_s5_c_im, v_s5_d, v_s5_w_glu, v_s5_w_out, v_hg_w_in, v_hg_lower_bounds, v_hg_gnorm_w, v_hg_w_out, v_ffn_w_up, v_ffn_conv_w, v_ffn_conv_b, v_ffn_w_down):
    given = dict(x=x, c=c, ctx=ctx, c_ctx=c_ctx, w_mod=w_mod, b_mod=b_mod, norm1_w=norm1_w, norm2_w=norm2_w, final_norm_w=final_norm_w, s5_w_in=s5_w_in, s5_lam_re=s5_lam_re, s5_lam_im=s5_lam_im, s5_log_step=s5_log_step, s5_b_re=s5_b_re, s5_b_im=s5_b_im, s5_c_re=s5_c_re, s5_c_im=s5_c_im, s5_d=s5_d, s5_w_glu=s5_w_glu, s5_w_out=s5_w_out, hg_w_in=hg_w_in, hg_lower_bounds=hg_lower_bounds, hg_gnorm_w=hg_gnorm_w, hg_w_out=hg_w_out, ffn_w_up=ffn_w_up, ffn_conv_w=ffn_conv_w, ffn_conv_b=ffn_conv_b, ffn_w_down=ffn_w_down, loss_target=loss_target, m_c_ctx=m_c_ctx, m_w_mod=m_w_mod, m_b_mod=m_b_mod, m_norm1_w=m_norm1_w, m_norm2_w=m_norm2_w, m_final_norm_w=m_final_norm_w, m_s5_w_in=m_s5_w_in, m_s5_lam_re=m_s5_lam_re, m_s5_lam_im=m_s5_lam_im, m_s5_log_step=m_s5_log_step, m_s5_b_re=m_s5_b_re, m_s5_b_im=m_s5_b_im, m_s5_c_re=m_s5_c_re, m_s5_c_im=m_s5_c_im, m_s5_d=m_s5_d, m_s5_w_glu=m_s5_w_glu, m_s5_w_out=m_s5_w_out, m_hg_w_in=m_hg_w_in, m_hg_lower_bounds=m_hg_lower_bounds, m_hg_gnorm_w=m_hg_gnorm_w, m_hg_w_out=m_hg_w_out, m_ffn_w_up=m_ffn_w_up, m_ffn_conv_w=m_ffn_conv_w, m_ffn_conv_b=m_ffn_conv_b, m_ffn_w_down=m_ffn_w_down, v_c_ctx=v_c_ctx, v_w_mod=v_w_mod, v_b_mod=v_b_mod, v_norm1_w=v_norm1_w, v_norm2_w=v_norm2_w, v_final_norm_w=v_final_norm_w, v_s5_w_in=v_s5_w_in, v_s5_lam_re=v_s5_lam_re, v_s5_lam_im=v_s5_lam_im, v_s5_log_step=v_s5_log_step, v_s5_b_re=v_s5_b_re, v_s5_b_im=v_s5_b_im, v_s5_c_re=v_s5_c_re, v_s5_c_im=v_s5_c_im, v_s5_d=v_s5_d, v_s5_w_glu=v_s5_w_glu, v_s5_w_out=v_s5_w_out, v_hg_w_in=v_hg_w_in, v_hg_lower_bounds=v_hg_lower_bounds, v_hg_gnorm_w=v_hg_gnorm_w, v_hg_w_out=v_hg_w_out, v_ffn_w_up=v_ffn_w_up, v_ffn_conv_w=v_ffn_conv_w, v_ffn_conv_b=v_ffn_conv_b, v_ffn_w_down=v_ffn_w_down)
    weights = {n: given[n] for n in TWIN_WEIGHTS}
    shared = {n: given[n] for n in SHARED_INPUTS}
    per_example = {n: given[n] for n in ['x', 'c', 'ctx']}
    grad_fn = _jax.value_and_grad(_loss, argnums=(0, 1))

    def one_microbatch(ex, loss_target):
        ex = dict(ex)
        diff = ex.pop(TWIN_DIFF_INPUT)
        return grad_fn(weights, diff, {**shared, **ex}, loss_target)

    if N_MICROBATCH == 1:
        loss, (grad_w, grad_x) = one_microbatch(per_example, given["loss_target"])
    else:
        def body(carry, xs):
            loss_sum, grad_sum = carry
            l_k, (gw_k, gx_k) = one_microbatch(xs[0], xs[1])
            with _jax.named_scope("update"):
                return (loss_sum + l_k, _jax.tree.map(_jnp.add, grad_sum, gw_k)), gx_k

        init = (_jnp.zeros((), _jnp.float32), _jax.tree.map(_jnp.zeros_like, weights))
        (loss, grad_w), grad_x = _jax.lax.scan(body, init, (per_example, given["loss_target"]))
    with _jax.named_scope("update"):
        delta_w, new_m, new_v = {}, {}, {}
        for n in TWIN_WEIGHTS:
            delta_w[n], new_m[n], new_v[n] = _adamw(weights[n], grad_w[n], given["m_" + n], given["v_" + n])
    return (loss, grad_x, *[grad_w[n] for n in TWIN_WEIGHTS], *[delta_w[n] for n in TWIN_WEIGHTS],
            *[new_m[n] for n in TWIN_WEIGHTS], *[new_v[n] for n in TWIN_WEIGHTS])
```

```python
import functools
import math

import jax
import jax.numpy as jnp
from jax import lax
from jax.experimental import pallas as pl
from jax.experimental.pallas import tpu as pltpu

F32 = jnp.float32
BF = jnp.bfloat16
MXU = jnp.bfloat16

NORM_EPS = 1e-6
GRID_W = 64
N_MOD = 6
S5_GROUP = 16
S5_STATE = 64
S5_LAM_RE_MAX = -1e-4
S5_KIN = 256
S5_KST = S5_KIN // S5_GROUP * S5_STATE
HEAD = 128
CHUNK_ROWS = 128
N_PROJ = 5
NB = 4
ADAM_LR, ADAM_B1, ADAM_B2, ADAM_EPS, ADAM_WD, ADAM_STEP = 0.001, 0.9, 0.999, 1e-08, 0.01, 10
VMEM_LIMIT = 56 * 1024 * 1024
MESH = pl.DeviceIdType.MESH


def _tile(n, cap):
    if n <= cap:
        return n
    best = None
    for t in range(128, cap + 1, 128):
        if n % t == 0:
            best = t
    assert best is not None, (n, cap)
    return best


def _row_tile(r, width=1024):
    cap = max(8, (512 * 1024) // max(width, 1))
    return next((t for t in (512, 256, 128, 64, 32, 16, 8) if t <= cap and r % t == 0), r)


def _cp(sem):
    return pltpu.CompilerParams(dimension_semantics=sem, vmem_limit_bytes=VMEM_LIMIT)


def _dot(a, b, ca=1, cb=0):
    return lax.dot_general(a.astype(MXU), b.astype(MXU), (((ca,), (cb,)), ((), ())), preferred_element_type=F32)


def _dot3(m, x):
    hi = x.astype(MXU)
    r1 = x - hi.astype(F32)
    mid = r1.astype(MXU)
    lo = (r1 - mid.astype(F32)).astype(MXU)
    return _dot(m, hi) + _dot(m, mid) + _dot(m, lo)


def mm(a, b, *, ta=False, tb=False, out_dtype=F32, name):
    (kd, m) = a.shape if ta else a.shape[::-1]
    (n, kd2) = b.shape if tb else b.shape[::-1]
    assert kd == kd2, (a.shape, b.shape, ta, tb)
    tm, tn, tk = _tile(m, 1024), _tile(n, 1536), _tile(kd, 1024)
    nk = kd // tk

    def body(a_ref, b_ref, o_ref, acc_ref):
        k = pl.program_id(2)

        @pl.when(k == 0)
        def _():
            acc_ref[...] = jnp.zeros_like(acc_ref)

        acc_ref[...] += _dot(a_ref[...], b_ref[...], 0 if ta else 1, 1 if tb else 0)

        @pl.when(k == nk - 1)
        def _():
            o_ref[...] = acc_ref[...].astype(out_dtype)

    a_spec = pl.BlockSpec((tk, tm), lambda i, j, k: (k, i)) if ta else pl.BlockSpec((tm, tk), lambda i, j, k: (i, k))
    b_spec = pl.BlockSpec((tn, tk), lambda i, j, k: (j, k)) if tb else pl.BlockSpec((tk, tn), lambda i, j, k: (k, j))
    return pl.pallas_call(
        body, name=name, grid=(m // tm, n // tn, nk), in_specs=[a_spec, b_spec],
        out_specs=pl.BlockSpec((tm, tn), lambda i, j, k: (i, j)), out_shape=jax.ShapeDtypeStruct((m, n), out_dtype),
        scratch_shapes=[pltpu.VMEM((tm, tn), F32)], compiler_params=_cp(("parallel", "parallel", "arbitrary")))(a, b)


def blockdiag_tn(a, b, wa, wb, *, scale=1.0, name):
    rows = a.shape[0]
    kb = a.shape[1] // wa
    tr = _tile(rows, 1024)
    nr = rows // tr

    def body(a_ref, b_ref, o_ref):
        i = pl.program_id(1)

        @pl.when(i == 0)
        def _():
            o_ref[...] = jnp.zeros_like(o_ref)

        o_ref[0] += scale * _dot(a_ref[...], b_ref[...], 0, 0)

    return pl.pallas_call(
        body, name=name, grid=(kb, nr),
        in_specs=[pl.BlockSpec((tr, wa), lambda k, i: (i, k)), pl.BlockSpec((tr, wb), lambda k, i: (i, k))],
        out_specs=pl.BlockSpec((1, wa, wb), lambda k, i: (k, 0, 0)), out_shape=jax.ShapeDtypeStruct((kb, wa, wb), F32),
        compiler_params=_cp(("parallel", "arbitrary")))(a, b)


def _pat(v, p, op):
    tm, d = v.shape
    return op(v.reshape(tm // 8, 8, d), p[None]).reshape(tm, d)


def _norm_mod(x, nw, shift, scale):
    y = x * lax.rsqrt(jnp.mean(x * x, axis=-1, keepdims=True) + NORM_EPS) * nw
    return _pat(_pat(y, 1.0 + scale, jnp.multiply), shift, jnp.add)


def _mt_spec(d, nct):
    return pl.BlockSpec((8, N_MOD * d), lambda i: (jnp.where(i < nct, 0, 1), 0))


def _acc_spec(d, nct):
    return pl.BlockSpec((8, d), lambda i: (jnp.where(i < nct, 0, 1), 0))


def _rows(cfg):
    tm = min(512, cfg["rc"])
    return tm, cfg["rc"] // tm


def node_fwd(cfg, xp, y, mtg, gi, nw, mtn, si, *, name):
    r, d = xp.shape
    tm, nct = _rows(cfg)
    row = pl.BlockSpec((tm, d), lambda i: (i, 0))
    vec = pl.BlockSpec((1, d), lambda i: (0, 0))

    def body(*refs):
        if y is None:
            xp_ref, nw_ref, mtn_ref, h_ref = refs
            x = xp_ref[...]
        else:
            xp_ref, y_ref, mtg_ref, nw_ref, mtn_ref, xn_ref, h_ref = refs
            x = xp_ref[...] + _pat(y_ref[...], mtg_ref[:, gi * d:(gi + 1) * d], jnp.multiply)
            xn_ref[...] = x
        h_ref[...] = _norm_mod(x, nw_ref[...], mtn_ref[:, si * d:(si + 1) * d], mtn_ref[:, (si + 1) * d:(si + 2) * d]).astype(MXU)

    h_shape = jax.ShapeDtypeStruct((r, d), MXU)
    if y is None:
        h = pl.pallas_call(body, name=name, grid=(r // tm,), in_specs=[row, vec, _mt_spec(d, nct)], out_specs=row,
                           out_shape=h_shape, compiler_params=_cp(("parallel",)))(xp, nw, mtn)
        return xp, h
    return pl.pallas_call(body, name=name, grid=(r // tm,), in_specs=[row, row, _mt_spec(d, nct), vec, _mt_spec(d, nct)],
                          out_specs=(row, row), out_shape=(jax.ShapeDtypeStruct((r, d), F32), h_shape),
                          compiler_params=_cp(("parallel",)))(xp, y, mtg, nw, mtn)


def node_bwd(cfg, dxres, dh, xn, y, mtg, gi, nw, mtn, si, *, name):
    r, d = xn.shape
    tm, nct = _rows(cfg)
    row = pl.BlockSpec((tm, d), lambda i: (i, 0))
    vec = pl.BlockSpec((1, d), lambda i: (0, 0))
    has_y = y is not None

    def body(*refs):
        if has_y:
            dxres_ref, dh_ref, xn_ref, y_ref, mtg_ref, nw_ref, mtn_ref, dxn_ref, dy_ref, dnw_ref, dsh_ref, dsc_ref, dg_ref = refs
        else:
            dxres_ref, dh_ref, xn_ref, nw_ref, mtn_ref, dxn_ref, dnw_ref, dsh_ref, dsc_ref = refs
        i = pl.program_id(0)
        _, vjp = jax.vjp(_norm_mod, xn_ref[...], nw_ref[...], mtn_ref[:, si * d:(si + 1) * d], mtn_ref[:, (si + 1) * d:(si + 2) * d])
        dx, dnw, dsh, dsc = vjp(dh_ref[...])
        dx = dx + dxres_ref[...]
        dxn_ref[...] = dx

        @pl.when(i == 0)
        def _():
            dnw_ref[...] = jnp.zeros_like(dnw_ref)

        @pl.when((i == 0) | (i == nct))
        def _():
            dsh_ref[...] = jnp.zeros_like(dsh_ref)
            dsc_ref[...] = jnp.zeros_like(dsc_ref)
            if has_y:
                dg_ref[...] = jnp.zeros_like(dg_ref)

        dnw_ref[...] += dnw
        dsh_ref[...] += dsh
        dsc_ref[...] += dsc
        if has_y:
            dy_ref[...] = _pat(dx, mtg_ref[:, gi * d:(gi + 1) * d], jnp.multiply).astype(MXU)
            dg_ref[...] += jnp.sum((dx * y_ref[...]).reshape(tm // 8, 8, d), axis=0)

    acc = jax.ShapeDtypeStruct((16, d), F32)
    xs = jax.ShapeDtypeStruct((r, d), F32)
    if has_y:
        return pl.pallas_call(
            body, name=name, grid=(r // tm,), in_specs=[row, row, row, row, _mt_spec(d, nct), vec, _mt_spec(d, nct)],
            out_specs=(row, row, vec, _acc_spec(d, nct), _acc_spec(d, nct), _acc_spec(d, nct)),
            out_shape=(xs, jax.ShapeDtypeStruct((r, d), MXU), jax.ShapeDtypeStruct((1, d), F32), acc, acc, acc),
            compiler_params=_cp(("arbitrary",)))(dxres, dh, xn, y, mtg, nw, mtn)
    dxn, dnw, dsh, dsc = pl.pallas_call(
        body, name=name, grid=(r // tm,), in_specs=[row, row, row, vec, _mt_spec(d, nct)],
        out_specs=(row, vec, _acc_spec(d, nct), _acc_spec(d, nct)),
        out_shape=(xs, jax.ShapeDtypeStruct((1, d), F32), acc, acc), compiler_params=_cp(("arbitrary",)))(dxres, dh, xn, nw, mtn)
    return dxn, None, dnw, dsh, dsc, None


def final_node(cfg, xp, y, mtg, gi, fnw, tgt, *, name):
    r, d = xp.shape
    tm, nct = _rows(cfg)
    row = pl.BlockSpec((tm, d), lambda i: (i, 0))
    vec = pl.BlockSpec((1, d), lambda i: (0, 0))

    def norm(x, w):
        return x * lax.rsqrt(jnp.mean(x * x, axis=-1, keepdims=True) + NORM_EPS) * w

    def body(xp_ref, y_ref, mtg_ref, fnw_ref, tgt_ref, loss_ref, dx_ref, dy_ref, dg_ref, dfnw_ref):
        i = pl.program_id(0)
        g = mtg_ref[:, gi * d:(gi + 1) * d]
        x = xp_ref[...] + _pat(y_ref[...], g, jnp.multiply)
        out, vjp = jax.vjp(norm, x, fnw_ref[...])
        lat = i >= nct
        err = jnp.where(lat, out - tgt_ref[...], 0.0)
        dx, dfnw = vjp(err * (1.0 / d))

        @pl.when(i == 0)
        def _():
            loss_ref[...] = jnp.zeros_like(loss_ref)
            dfnw_ref[...] = jnp.zeros_like(dfnw_ref)

        @pl.when((i == 0) | (i == nct))
        def _():
            dg_ref[...] = jnp.zeros_like(dg_ref)

        loss_ref[...] += jnp.full(loss_ref.shape, 0.5 / d * jnp.sum(err * err), F32)
        dfnw_ref[...] += dfnw
        dx_ref[...] = dx
        dy_ref[...] = _pat(dx, g, jnp.multiply).astype(MXU)
        dg_ref[...] += jnp.sum((dx * y_ref[...]).reshape(tm // 8, 8, d), axis=0)

    return pl.pallas_call(
        body, name=name, grid=(r // tm,),
        in_specs=[row, row, _mt_spec(d, nct), vec, pl.BlockSpec((tm, d), lambda i: (jnp.maximum(i - nct, 0), 0))],
        out_specs=(pl.BlockSpec((8, 128), lambda i: (0, 0)), row, row, _acc_spec(d, nct), vec),
        out_shape=(jax.ShapeDtypeStruct((8, 128), F32), jax.ShapeDtypeStruct((r, d), F32), jax.ShapeDtypeStruct((r, d), MXU),
                   jax.ShapeDtypeStruct((16, d), F32), jax.ShapeDtypeStruct((1, d), F32)),
        compiler_params=_cp(("arbitrary",)))(xp, y, mtg, fnw, tgt)


def _silu(x):
    return x * jax.nn.sigmoid(x)


def mod_fwd(c16, w, b, *, name):
    d, n = w.shape
    tn = _tile(n, 1536)

    def body(c_ref, w_ref, b_ref, o_ref, s_ref):
        s = _silu(c_ref[...])
        s_ref[...] = s.astype(MXU)
        o_ref[...] = _dot(s, w_ref[...]) + b_ref[...]

    return pl.pallas_call(
        body, name=name, grid=(n // tn,),
        in_specs=[pl.BlockSpec((16, d), lambda j: (0, 0)), pl.BlockSpec((d, tn), lambda j: (0, j)), pl.BlockSpec((1, tn), lambda j: (0, j))],
        out_specs=(pl.BlockSpec((16, tn), lambda j: (0, j)), pl.BlockSpec((16, d), lambda j: (0, 0))),
        out_shape=(jax.ShapeDtypeStruct((16, n), F32), jax.ShapeDtypeStruct((16, d), MXU)),
        compiler_params=_cp(("arbitrary",)))(c16, w, b)


def colsum(x, *, name):
    def body(x_ref, o_ref):
        o_ref[...] = jnp.sum(x_ref[...], axis=0, keepdims=True)

    return pl.pallas_call(body, name=name, out_shape=jax.ShapeDtypeStruct((1, x.shape[1]), F32))(x)


def cctx_grad(c16, ds_list, *, name):
    def body(c_ref, *refs):
        o_ref = refs[-1]
        ds = refs[0][...]
        for r_ in refs[1:-1]:
            ds = ds + r_[...]
        _, vjp = jax.vjp(_silu, c_ref[...])
        (dc,) = vjp(ds)
        o_ref[...] = jnp.sum(dc[0:8], axis=0, keepdims=True)

    return pl.pallas_call(body, name=name, out_shape=jax.ShapeDtypeStruct((1, c16.shape[1]), F32))(c16, *ds_list)


def _s5_disc(lam_re, lam_im, log_step, b_re, b_im):
    lr = jnp.minimum(lam_re, S5_LAM_RE_MAX)
    li = lam_im
    dt = jnp.exp(log_step)
    mag = jnp.exp(lr * dt)
    abar_r = mag * jnp.cos(li * dt)
    abar_i = mag * jnp.sin(li * dt)
    den = lr * lr + li * li
    nr = abar_r - 1.0
    coef_r = (nr * lr + abar_i * li) / den
    coef_i = (abar_i * lr - nr * li) / den
    bbar_r = coef_r[:, None, :] * b_re - coef_i[:, None, :] * b_im
    bbar_i = coef_r[:, None, :] * b_im + coef_i[:, None, :] * b_re
    return abar_r, abar_i, bbar_r, bbar_i


def s5_disc_fwd(lam_re, lam_im, log_step, b_re, b_im, *, name):
    def body(lr, li, ls, br, bi, ar_o, ai_o, br_o, bi_o):
        ar_o[...], ai_o[...], br_o[...], bi_o[...] = _s5_disc(lr[...], li[...], ls[...], br[...], bi[...])

    s2, s3 = jax.ShapeDtypeStruct(lam_re.shape, F32), jax.ShapeDtypeStruct(b_re.shape, F32)
    return pl.pallas_call(body, name=name, out_shape=(s2, s2, s3, s3))(lam_re, lam_im, log_step, b_re, b_im)


def s5_disc_bwd(lam_re, lam_im, log_step, b_re, b_im, d_ar, d_ai, d_br, d_bi, *, name):
    def body(lr, li, ls, br, bi, dar, dai, dbr, dbi, o_lr, o_li, o_ls, o_br, o_bi):
        _, vjp = jax.vjp(_s5_disc, lr[...], li[...], ls[...], br[...], bi[...])
        o_lr[...], o_li[...], o_ls[...], o_br[...], o_bi[...] = vjp((dar[...], dai[...], dbr[...], dbi[...]))

    s2, s3 = jax.ShapeDtypeStruct(lam_re.shape, F32), jax.ShapeDtypeStruct(b_re.shape, F32)
    return pl.pallas_call(body, name=name, out_shape=(s2, s2, jax.ShapeDtypeStruct(log_step.shape, F32), s3, s3))(
        lam_re, lam_im, log_step, b_re, b_im, d_ar, d_ai, d_br, d_bi)


S5_LANES = 512


def _chunk_order(k, ncc, nch, rev):
    if not rev:
        return k
    return jnp.where(k < ncc, ncc - 1 - k, nch - 1 - (k - ncc))


def _cmul(ar, ai, xr, xi):
    return ar * xr - ai * xi, ar * xi + ai * xr


def s5_scan_fwd(cfg, u, a_re, a_im, bre, bim, cre, cim, *, rev, name):
    r, d = u.shape
    ns = a_re.shape[1]
    kb = d // S5_KIN
    tcr = 128
    n8 = tcr // 8
    nch, ncc = r // tcr, cfg["rc"] // tcr
    lw = min(S5_LANES, ns)

    def body(u_ref, ar_ref, ai_ref, bre_ref, bim_ref, cre_ref, cim_ref, sre_ref, sim_ref, ere_ref, eim_ref, y_ref, st_re, st_im):
        k = pl.program_id(0)

        @pl.when(k == 0)
        def _():
            st_re[...] = jnp.zeros_like(st_re)
            st_im[...] = jnp.zeros_like(st_im)

        ub = u_ref[...].astype(MXU)
        for j in range(kb):
            uj = ub[:, j * S5_KIN:(j + 1) * S5_KIN]
            sre_ref[:, :, j * S5_KST:(j + 1) * S5_KST] = _dot(uj, bre_ref[j]).reshape(n8, 8, S5_KST)
            sim_ref[:, :, j * S5_KST:(j + 1) * S5_KST] = _dot(uj, bim_ref[j]).reshape(n8, 8, S5_KST)
        ere_ref[0] = st_re[...]
        eim_ref[0] = st_im[...]
        first = lax.broadcasted_iota(jnp.int32, (8, lw), 0) < 4
        if rev:
            first = jnp.logical_not(first)
        for c in range(ns // lw):
            sl = slice(c * lw, (c + 1) * lw)
            ar = jnp.broadcast_to(ar_ref[:, sl], (8, lw))
            ai = jnp.broadcast_to(ai_ref[:, sl], (8, lw))

            def step(i, carry, sl=sl, ar=ar, ai=ai):
                sr, si = carry
                ii = n8 - 1 - i if rev else i
                tr, ti = sre_ref[ii, :, sl], sim_ref[ii, :, sl]
                pr, pi = _cmul(ar, ai, sr, si)
                t1r, t1i = pr + tr, pi + ti
                s1r = jnp.where(first, t1r, pltpu.roll(t1r, 4, 0))
                s1i = jnp.where(first, t1i, pltpu.roll(t1i, 4, 0))
                pr, pi = _cmul(ar, ai, s1r, s1i)
                t2r, t2i = pr + tr, pi + ti
                sre_ref[ii, :, sl] = jnp.where(first, t1r, t2r)
                sim_ref[ii, :, sl] = jnp.where(first, t1i, t2i)
                return jnp.where(first, pltpu.roll(t2r, 4, 0), t2r), jnp.where(first, pltpu.roll(t2i, 4, 0), t2i)

            sr, si = lax.fori_loop(0, n8, step, (st_re[:, sl], st_im[:, sl]))
            st_re[:, sl] = sr
            st_im[:, sl] = si
        for j in range(kb):
            sr = sre_ref[:, :, j * S5_KST:(j + 1) * S5_KST].reshape(tcr, S5_KST)
            si = sim_ref[:, :, j * S5_KST:(j + 1) * S5_KST].reshape(tcr, S5_KST)
            y_ref[:, j * S5_KIN:(j + 1) * S5_KIN] = _dot(sr, cre_ref[j]) - _dot(si, cim_ref[j])

    cidx = functools.partial(_chunk_order, ncc=ncc, nch=nch, rev=rev)
    full = lambda a: pl.BlockSpec(a.shape, lambda k: (0,) * a.ndim)
    st = pl.BlockSpec((n8, 8, ns), lambda k: (cidx(k), 0, 0))
    en = pl.BlockSpec((1, 8, ns), lambda k: (cidx(k), 0, 0))
    return pl.pallas_call(
        body, name=name, grid=(nch,),
        in_specs=[pl.BlockSpec((tcr, d), lambda k: (cidx(k), 0)), full(a_re), full(a_im), full(bre), full(bim), full(cre), full(cim)],
        out_specs=(st, st, en, en, pl.BlockSpec((tcr, d), lambda k: (cidx(k), 0))),
        out_shape=(jax.ShapeDtypeStruct((r // 8, 8, ns), F32),) * 2 + (jax.ShapeDtypeStruct((nch, 8, ns), F32),) * 2
        + (jax.ShapeDtypeStruct((r, d), F32),),
        scratch_shapes=[pltpu.VMEM((8, ns), F32), pltpu.VMEM((8, ns), F32)], compiler_params=_cp(("arbitrary",)))(
            u, a_re, a_im, bre, bim, cre, cim)


def s5_scan_bwd(cfg, dyb, sre, sim, ere, eim, a_re, a_im, bre, bim, cre, cim, du_in, *, rev, name):
    r, d = dyb.shape
    ns = a_re.shape[1]
    kb = d // S5_KIN
    tcr = 128
    n8 = tcr // 8
    nch, ncc = r // tcr, cfg["rc"] // tcr
    lw = min(S5_LANES, ns)

    def body(dy_ref, sre_ref, sim_ref, ere_ref, eim_ref, ar_ref, ai_ref, bre_ref, bim_ref, cre_ref, cim_ref, duin_ref,
             du_ref, gre_ref, gim_ref, dar_ref, dai_ref, g_re, g_im, gc_re, gc_im):
        k = pl.program_id(0)

        @pl.when(k == 0)
        def _():
            gc_re[...] = jnp.zeros_like(gc_re)
            gc_im[...] = jnp.zeros_like(gc_im)
            dar_ref[...] = jnp.zeros_like(dar_ref)
            dai_ref[...] = jnp.zeros_like(dai_ref)

        dy = dy_ref[...]
        for j in range(kb):
            dyj = dy[:, j * S5_KIN:(j + 1) * S5_KIN]
            g_re[:, :, j * S5_KST:(j + 1) * S5_KST] = _dot(dyj, cre_ref[j], 1, 1).reshape(n8, 8, S5_KST)
            g_im[:, :, j * S5_KST:(j + 1) * S5_KST] = -_dot(dyj, cim_ref[j], 1, 1).reshape(n8, 8, S5_KST)
        first = lax.broadcasted_iota(jnp.int32, (8, lw), 0) < 4
        if rev:
            first = jnp.logical_not(first)
        for c in range(ns // lw):
            sl = slice(c * lw, (c + 1) * lw)
            ar = jnp.broadcast_to(ar_ref[:, sl], (8, lw))
            nai = -jnp.broadcast_to(ai_ref[:, sl], (8, lw))

            def step(i, carry, sl=sl, ar=ar, nai=nai):
                gr, gi, accr, acci = carry
                ii = i if rev else n8 - 1 - i
                tr, ti = g_re[ii, :, sl], g_im[ii, :, sl]
                pr, pi = _cmul(ar, nai, gr, gi)
                t2r, t2i = pr + tr, pi + ti
                g2r = jnp.where(first, pltpu.roll(t2r, 4, 0), t2r)
                g2i = jnp.where(first, pltpu.roll(t2i, 4, 0), t2i)
                pr, pi = _cmul(ar, nai, g2r, g2i)
                t1r, t1i = pr + tr, pi + ti
                outr, outi = jnp.where(first, t1r, t2r), jnp.where(first, t1i, t2i)
                g_re[ii, :, sl] = outr
                g_im[ii, :, sl] = outi
                pv = jnp.clip(ii + 1 if rev else ii - 1, 0, n8 - 1)
                at_entry = (ii == n8 - 1) if rev else (ii == 0)
                pvr = jnp.where(at_entry, ere_ref[0, :, sl], sre_ref[pv, :, sl])
                pvi = jnp.where(at_entry, eim_ref[0, :, sl], sim_ref[pv, :, sl])
                spr = pltpu.roll(jnp.where(first, sre_ref[ii, :, sl], pvr), 4, 0)
                spi = pltpu.roll(jnp.where(first, sim_ref[ii, :, sl], pvi), 4, 0)
                accr = accr + outr * spr + outi * spi
                acci = acci + outi * spr - outr * spi
                return jnp.where(first, t1r, pltpu.roll(t1r, 4, 0)), jnp.where(first, t1i, pltpu.roll(t1i, 4, 0)), accr, acci

            gr, gi, accr, acci = lax.fori_loop(0, n8, step, (gc_re[:, sl], gc_im[:, sl], dar_ref[:, sl], dai_ref[:, sl]))
            gc_re[:, sl] = gr
            gc_im[:, sl] = gi
            dar_ref[:, sl] = accr
            dai_ref[:, sl] = acci
        for j in range(kb):
            gr = g_re[:, :, j * S5_KST:(j + 1) * S5_KST].reshape(tcr, S5_KST)
            gi = g_im[:, :, j * S5_KST:(j + 1) * S5_KST].reshape(tcr, S5_KST)
            gre_ref[:, j * S5_KST:(j + 1) * S5_KST] = gr.astype(MXU)
            gim_ref[:, j * S5_KST:(j + 1) * S5_KST] = gi.astype(MXU)
            du_ref[:, j * S5_KIN:(j + 1) * S5_KIN] = (duin_ref[:, j * S5_KIN:(j + 1) * S5_KIN]
                                                     + _dot(gr, bre_ref[j], 1, 1) + _dot(gi, bim_ref[j], 1, 1))

    def cidx(k):
        return _chunk_order(nch - 1 - k, ncc, nch, rev)

    full = lambda a: pl.BlockSpec(a.shape, lambda k: (0,) * a.ndim)
    st = pl.BlockSpec((n8, 8, ns), lambda k: (cidx(k), 0, 0))
    en = pl.BlockSpec((1, 8, ns), lambda k: (cidx(k), 0, 0))
    rowd = pl.BlockSpec((tcr, d), lambda k: (cidx(k), 0))
    rown = pl.BlockSpec((tcr, ns), lambda k: (cidx(k), 0))
    acc = pl.BlockSpec((8, ns), lambda k: (0, 0))
    return pl.pallas_call(
        body, name=name, grid=(nch,),
        in_specs=[rowd, st, st, en, en, full(a_re), full(a_im), full(bre), full(bim), full(cre), full(cim), rowd],
        out_specs=(rowd, rown, rown, acc, acc),
        out_shape=(jax.ShapeDtypeStruct((r, d), F32), jax.ShapeDtypeStruct((r, ns), MXU), jax.ShapeDtypeStruct((r, ns), MXU),
                   jax.ShapeDtypeStruct((8, ns), F32), jax.ShapeDtypeStruct((8, ns), F32)),
        scratch_shapes=[pltpu.VMEM((n8, 8, ns), F32), pltpu.VMEM((n8, 8, ns), F32), pltpu.VMEM((8, ns), F32), pltpu.VMEM((8, ns), F32)],
        compiler_params=_cp(("arbitrary",)))(dyb, sre, sim, ere, eim, a_re, a_im, bre, bim, cre, cim, du_in)


def rowmap(fn, rows_in, vecs_in, outs, accs=(), *, name):
    r = rows_in[0].shape[0]
    tm = _row_tile(r, max(a.shape[1] for a in rows_in))
    nr, nv, no = len(rows_in), len(vecs_in), len(outs)

    def body(*refs):
        ins = [x[...] for x in refs[:nr + nv]]
        res = fn(*ins)
        if not isinstance(res, (tuple, list)):
            res = (res,)
        out_refs = refs[nr + nv:]
        for o_ref, v in zip(out_refs[:no], res[:no]):
            o_ref[...] = v.astype(o_ref.dtype)
        if accs:
            @pl.when(pl.program_id(0) == 0)
            def _():
                for a_ref in out_refs[no:]:
                    a_ref[...] = jnp.zeros_like(a_ref)
            for a_ref, v in zip(out_refs[no:], res[no:]):
                a_ref[...] += v

    in_specs = [pl.BlockSpec((tm, a.shape[1]), lambda i: (i, 0)) for a in rows_in]
    in_specs += [pl.BlockSpec(v.shape, lambda i, n=v.ndim: (0,) * n) for v in vecs_in]
    out_specs = [pl.BlockSpec((tm, w), lambda i: (i, 0)) for w, _ in outs] + [pl.BlockSpec(s, lambda i, n=len(s): (0,) * n) for s in accs]
    out_shape = [jax.ShapeDtypeStruct((r, w), dt) for w, dt in outs] + [jax.ShapeDtypeStruct(s, F32) for s in accs]
    res = pl.pallas_call(body, name=name, grid=(r // tm,), in_specs=in_specs, out_specs=tuple(out_specs), out_shape=tuple(out_shape),
                         compiler_params=_cp(("arbitrary",) if accs else ("parallel",)))(*rows_in, *vecs_in)
    return res


def _gelu(x):
    return jax.nn.gelu(x, approximate=True)


def _hg_lower_bound(e0, e1):
    m = jnp.maximum(e0, e1)
    a, b = jnp.exp(e0 - m), jnp.exp(e1 - m)
    return b / (a + b)


def _hg_gates(x, lb):
    logf = jnp.log(lb + (1.0 - lb) * jax.nn.sigmoid(x))
    return logf, (1.0 - lb) * jax.nn.sigmoid(-x)


def _hg_masks(rev):
    n = CHUNK_ROWS
    rr = lax.broadcasted_iota(jnp.int32, (n, n), 0)
    ss = lax.broadcasted_iota(jnp.int32, (n, n), 1)
    same = (rr % NB) == (ss % NB)
    causal = same & ((ss >= rr) if rev else (ss <= rr))
    anti = same & ((ss <= rr) if rev else (ss >= rr))
    end0 = 0 if rev else n - NB
    pick_end = ss == (end0 + rr % NB)
    return same, causal, anti, pick_end, end0


def _hg_expand(x):
    ex = lax.broadcasted_iota(jnp.int32, x.shape, 0) % NB
    return jnp.concatenate([jnp.where(ex == b, x, 0.0) for b in range(NB)], axis=1)


def _hg_fold(xe):
    kk = xe.shape[1] // NB
    ex = lax.broadcasted_iota(jnp.int32, (xe.shape[0], kk), 0) % NB
    out = jnp.zeros((xe.shape[0], kk), F32)
    for b in range(NB):
        out = out + jnp.where(ex == b, xe[:, b * kk:(b + 1) * kk], 0.0)
    return out


def _hg_chunk(q, v, x, lb, stk, rev):
    same, causal, anti, pick_end, end0 = _hg_masks(rev)
    logf, kk = _hg_gates(x, lb)
    b = _dot3(causal.astype(MXU), logf)
    bend_t = _dot3(pick_end.astype(MXU), b)
    bend_flat = jnp.concatenate([b[end0 + i:end0 + i + 1] for i in range(NB)], axis=1)
    eb = jnp.exp(b)
    enb = jnp.exp(-b)
    ee = jnp.exp(bend_t - b)
    qd, kd, ke = q * eb, kk * enb, kk * ee
    att = jnp.where(causal, _dot(qd, kd, 1, 1), 0.0)
    decay = jnp.exp(bend_flat)
    return dict(same=same, causal=causal, anti=anti, logf=logf, kk=kk, b=b, eb=eb, enb=enb, ee=ee, qd=qd, kd=kd, ke=ke, att=att,
                decay=decay, qde=_hg_expand(qd), kee=_hg_expand(ke))


def _hg_chunk_order(cfg, r):
    nch, ncc = r // CHUNK_ROWS, cfg["rc"] // CHUNK_ROWS
    return nch, ncc


def hg_scan_fwd(cfg, z, lb, *, d_dir, name):
    r = z.shape[0]
    d = z.shape[1] // N_PROJ
    nh = d // HEAD
    rev = d_dir == 1
    nch, ncc = _hg_chunk_order(cfg, r)
    n = CHUNK_ROWS

    def body(q_ref, v_ref, x_ref, lb_ref, o_ref, sin_ref, stk):
        k = pl.program_id(1)

        @pl.when(k == 0)
        def _():
            stk[...] = jnp.zeros_like(stk)

        s0 = stk[...]
        sin_ref[0, 0] = s0
        v = v_ref[...]
        c = _hg_chunk(q_ref[...], v, x_ref[...], lb_ref[...], s0, rev)
        o_ref[...] = _dot(c["att"], v) + _dot(c["qde"], s0, 1, 1)
        stk[...] = s0 * c["decay"] + _dot(v, c["kee"], 0, 0)

    def cidx(k):
        return _chunk_order(k, ncc, nch, rev)

    blk = lambda p: pl.BlockSpec((n, HEAD), lambda h, k: (cidx(k), p * nh + h))
    return pl.pallas_call(
        body, name=name, grid=(nh, nch),
        in_specs=[blk(0), blk(1), blk(2 + d_dir), pl.BlockSpec((1, HEAD), lambda h, k: (0, h))],
        out_specs=(pl.BlockSpec((n, HEAD), lambda h, k: (cidx(k), h)), pl.BlockSpec((1, 1, HEAD, NB * HEAD), lambda h, k: (cidx(k), h, 0, 0))),
        out_shape=(jax.ShapeDtypeStruct((r, d), F32), jax.ShapeDtypeStruct((nch, nh, HEAD, NB * HEAD), F32)),
        scratch_shapes=[pltpu.VMEM((HEAD, NB * HEAD), F32)], compiler_params=_cp(("parallel", "arbitrary")))(z, z, z, lb)


def hg_scan_bwd(cfg, do, z, lb, sin, dq_in, dv_in, *, d_dir, name):
    r = z.shape[0]
    d = z.shape[1] // N_PROJ
    nh = d // HEAD
    rev = d_dir == 1
    nch, ncc = _hg_chunk_order(cfg, r)
    n = CHUNK_ROWS
    has_in = dq_in is not None

    def body(*refs):
        if has_in:
            do_ref, q_ref, v_ref, x_ref, lb_ref, sin_ref, dqi_ref, dvi_ref, dq_ref, dv_ref, dx_ref, dlb_ref, dstk = refs
        else:
            do_ref, q_ref, v_ref, x_ref, lb_ref, sin_ref, dq_ref, dv_ref, dx_ref, dlb_ref, dstk = refs
        k = pl.program_id(1)

        @pl.when(k == 0)
        def _():
            dstk[...] = jnp.zeros_like(dstk)
            dlb_ref[...] = jnp.zeros_like(dlb_ref)

        do, q, v, x, lb, s0, ds1 = do_ref[...], q_ref[...], v_ref[...], x_ref[...], lb_ref[...], sin_ref[0, 0], dstk[...]
        c = _hg_chunk(q, v, x, lb, s0, rev)
        datt = jnp.where(c["causal"], _dot(do, v, 1, 1), 0.0)
        dv = _dot(c["att"], do, 0, 0) + _dot(c["kee"], ds1, 1, 1)
        dqd = _dot(datt, c["kd"]) + _hg_fold(_dot(do, s0))
        dkd = _dot(datt, c["qd"], 0, 0)
        dke = _hg_fold(_dot(v, ds1))
        dbend_flat = jnp.sum(ds1 * s0, axis=0, keepdims=True) * c["decay"]
        dstk[...] = _dot(do, c["qde"], 0, 0) + ds1 * c["decay"]
        dq = dqd * c["eb"]
        dk = dkd * c["enb"] + dke * c["ee"]
        db = dqd * c["qd"] - dkd * c["kd"] - dke * c["ke"]
        ex = lax.broadcasted_iota(jnp.int32, (n, HEAD), 0) % NB
        dbend_rows = jnp.zeros((n, HEAD), F32)
        for b in range(NB):
            dbend_rows = dbend_rows + jnp.where(ex == b, dbend_flat[:, b * HEAD:(b + 1) * HEAD], 0.0)
        dlogf = _dot3(c["anti"].astype(MXU), db) + _dot3(c["same"].astype(MXU), dke * c["ke"]) + dbend_rows
        _, vjp = jax.vjp(_hg_gates, x, lb)
        dx, dlb = vjp((dlogf, dk))
        if has_in:
            dq = dq + dqi_ref[...]
            dv = dv + dvi_ref[...]
        dq_ref[...] = dq
        dv_ref[...] = dv
        dx_ref[...] = dx
        dlb_ref[...] += dlb

    def cidx(k):
        return _chunk_order(nch - 1 - k, ncc, nch, rev)

    blk = lambda p: pl.BlockSpec((n, HEAD), lambda h, k: (cidx(k), p * nh + h))
    oblk = pl.BlockSpec((n, HEAD), lambda h, k: (cidx(k), h))
    vec = pl.BlockSpec((1, HEAD), lambda h, k: (0, h))
    in_specs = [oblk, blk(0), blk(1), blk(2 + d_dir), vec, pl.BlockSpec((1, 1, HEAD, NB * HEAD), lambda h, k: (cidx(k), h, 0, 0))]
    args = [do, z, z, z, lb, sin]
    if has_in:
        in_specs += [oblk, oblk]
        args += [dq_in, dv_in]
    rd = jax.ShapeDtypeStruct((r, d), F32)
    return pl.pallas_call(
        body, name=name, grid=(nh, nch), in_specs=in_specs, out_specs=(oblk, oblk, oblk, vec),
        out_shape=(rd, rd, rd, jax.ShapeDtypeStruct((1, d), F32)),
        scratch_shapes=[pltpu.VMEM((HEAD, NB * HEAD), F32)], compiler_params=_cp(("parallel", "arbitrary")))(*args)


def _hg_read(o, g, gw):
    on = o * lax.rsqrt(jnp.mean(o * o, axis=-1, keepdims=True) + NORM_EPS) * gw
    return on * jax.nn.sigmoid(g)


def hg_read_fwd(of, ob, z, gw, *, name):
    r, d = of.shape
    nh = d // HEAD
    tm = _row_tile(r)

    def body(of_ref, ob_ref, g_ref, gw_ref, o_ref):
        o_ref[...] = _hg_read(of_ref[...] + ob_ref[...], g_ref[...], gw_ref[...]).astype(MXU)

    blk = pl.BlockSpec((tm, HEAD), lambda i, h: (i, h))
    return pl.pallas_call(
        body, name=name, grid=(r // tm, nh),
        in_specs=[blk, blk, pl.BlockSpec((tm, HEAD), lambda i, h: (i, (N_PROJ - 1) * nh + h)), pl.BlockSpec((1, HEAD), lambda i, h: (0, 0))],
        out_specs=blk, out_shape=jax.ShapeDtypeStruct((r, d), MXU), compiler_params=_cp(("parallel", "parallel")))(of, ob, z, gw)


def hg_read_bwd(don, of, ob, z, gw, *, name):
    r, d = of.shape
    nh = d // HEAD
    tm = _row_tile(r)

    def body(don_ref, of_ref, ob_ref, g_ref, gw_ref, do_ref, dg_ref, dgw_ref):
        @pl.when((pl.program_id(0) == 0) & (pl.program_id(1) == 0))
        def _():
            dgw_ref[...] = jnp.zeros_like(dgw_ref)

        _, vjp = jax.vjp(_hg_read, of_ref[...] + ob_ref[...], g_ref[...], gw_ref[...])
        do_ref[...], dg_ref[...], dgw = vjp(don_ref[...])
        dgw_ref[...] += dgw

    blk = pl.BlockSpec((tm, HEAD), lambda i, h: (i, h))
    vec = pl.BlockSpec((1, HEAD), lambda i, h: (0, 0))
    rd = jax.ShapeDtypeStruct((r, d), F32)
    return pl.pallas_call(
        body, name=name, grid=(r // tm, nh),
        in_specs=[blk, blk, blk, pl.BlockSpec((tm, HEAD), lambda i, h: (i, (N_PROJ - 1) * nh + h)), vec],
        out_specs=(blk, blk, vec), out_shape=(rd, rd, jax.ShapeDtypeStruct((1, HEAD), F32)),
        compiler_params=_cp(("arbitrary", "arbitrary")))(don, of, ob, z, gw)


FFN_COLS = 256


def _seg_masks(cfg, tr, i):
    t = lax.broadcasted_iota(jnp.int32, (tr, FFN_COLS), 0) // NB
    ctx_steps = cfg["rc"] // NB
    pos = jnp.where(i == 0, t % ctx_steps, t % GRID_W)
    last = jnp.where(i == 0, ctx_steps - 1, GRID_W - 1)
    return pos == 0, pos == last


def _prev(x, start):
    return jnp.where(start, 0.0, pltpu.roll(x, NB, 0))


def _next(x, end):
    return jnp.where(end, 0.0, pltpu.roll(x, x.shape[0] - NB, 0))


def _conv3(u, w, b, start, end):
    return ((b + _prev(u, start) * w[0:1]) + u * w[1:2]) + _next(u, end) * w[2:3]


def ffn_mid_fwd(cfg, u, cw, cb, *, name):
    r, f2 = u.shape
    f = f2 // 2
    tr = cfg["rc"]
    nf = f // FFN_COLS

    def body(ua_ref, ug_ref, wa_ref, wg_ref, ba_ref, bg_ref, o_ref):
        start, end = _seg_masks(cfg, tr, pl.program_id(0))
        a = _conv3(ua_ref[...], wa_ref[...], ba_ref[...], start, end)
        g = _conv3(ug_ref[...], wg_ref[...], bg_ref[...], start, end)
        o_ref[...] = (_silu(a) * g).astype(MXU)

    ca = lambda rows: pl.BlockSpec((rows, FFN_COLS), lambda i, j: (i if rows == tr else 0, j))
    cg = lambda rows: pl.BlockSpec((rows, FFN_COLS), lambda i, j: (i if rows == tr else 0, j + nf))
    return pl.pallas_call(
        body, name=name, grid=(r // tr, nf), in_specs=[ca(tr), cg(tr), ca(3), cg(3), ca(1), cg(1)], out_specs=ca(tr),
        out_shape=jax.ShapeDtypeStruct((r, f), MXU), compiler_params=_cp(("parallel", "parallel")))(u, u, cw, cw, cb, cb)


def ffn_mid_bwd(cfg, dact, u, cw, cb, *, name):
    r, f2 = u.shape
    f = f2 // 2
    tr = cfg["rc"]
    nf = f // FFN_COLS

    def body(da_ref, us_ref, up_ref, ws_ref, wp_ref, bs_ref, bp_ref, du_ref, dcw_ref, dcb_ref):
        i = pl.program_id(1)
        is_a = pl.program_id(0) < nf
        start, end = _seg_masks(cfg, tr, i)
        us, ws = us_ref[...], ws_ref[...]
        cs = _conv3(us, ws, bs_ref[...], start, end)
        cp = _conv3(up_ref[...], wp_ref[...], bp_ref[...], start, end)
        dact_v = da_ref[...]
        sg = jax.nn.sigmoid(cs)
        d_if_a = dact_v * cp * (sg * (1.0 + cs * (1.0 - sg)))
        d_if_g = dact_v * _silu(cp)
        dc = jnp.where(is_a, d_if_a, d_if_g)
        du_ref[...] = (ws[1:2] * dc + ws[0:1] * _next(dc, end) + ws[2:3] * _prev(dc, start)).astype(MXU)

        @pl.when(i == 0)
        def _():
            dcw_ref[...] = jnp.zeros_like(dcw_ref)
            dcb_ref[...] = jnp.zeros_like(dcb_ref)

        dcw_ref[...] += jnp.concatenate([jnp.sum(dc * _prev(us, start), axis=0, keepdims=True), jnp.sum(dc * us, axis=0, keepdims=True),
                                         jnp.sum(dc * _next(us, end), axis=0, keepdims=True)], axis=0)
        dcb_ref[...] += jnp.sum(dc, axis=0, keepdims=True)

    cs_ = lambda rows: pl.BlockSpec((rows, FFN_COLS), lambda j, i: (i if rows == tr else 0, j))
    cp_ = lambda rows: pl.BlockSpec((rows, FFN_COLS), lambda j, i: (i if rows == tr else 0, (j + nf) % (2 * nf)))
    return pl.pallas_call(
        body, name=name, grid=(2 * nf, r // tr),
        in_specs=[pl.BlockSpec((tr, FFN_COLS), lambda j, i: (i, j % nf)), cs_(tr), cp_(tr), cs_(3), cp_(3), cs_(1), cp_(1)],
        out_specs=(cs_(tr), cs_(3), cs_(1)),
        out_shape=(jax.ShapeDtypeStruct((r, f2), MXU), jax.ShapeDtypeStruct((3, f2), F32), jax.ShapeDtypeStruct((1, f2), F32)),
        compiler_params=_cp(("parallel", "arbitrary")))(dact, u, u, cw, cw, cb, cb)


def hg_lb_fwd(e0, e1, *, name):
    def body(a, b, o):
        o[...] = _hg_lower_bound(a[...], b[...])

    return pl.pallas_call(body, name=name, out_shape=jax.ShapeDtypeStruct(e0.shape, F32))(e0, e1)


def hg_lb_bwd(e0, e1, dlb, *, name):
    def body(a, b, g, oa, ob):
        _, vjp = jax.vjp(_hg_lower_bound, a[...], b[...])
        oa[...], ob[...] = vjp(g[...])

    s = jax.ShapeDtypeStruct(e0.shape, F32)
    return pl.pallas_call(body, name=name, out_shape=(s, s))(e0, e1, dlb)


def _adamw(w, g, m, v):
    m = ADAM_B1 * m + (1.0 - ADAM_B1) * g
    v = ADAM_B2 * v + (1.0 - ADAM_B2) * jnp.square(g)
    m_hat = m / (1.0 - ADAM_B1 ** ADAM_STEP)
    v_hat = v / (1.0 - ADAM_B2 ** ADAM_STEP)
    delta = -ADAM_LR * (m_hat / (jnp.sqrt(v_hat) + ADAM_EPS) + ADAM_WD * w)
    return delta, m, v


def _as2d(a):
    if a.ndim >= 2 and a.shape[-1] % 128 == 0:
        return a.reshape(-1, a.shape[-1])
    return a.reshape(-1, 128) if a.size % 128 == 0 else a.reshape(1, -1)


def adamw(w, g, m, v, *, name):
    w2 = _as2d(w)
    outs = rowmap(_adamw, [w2, _as2d(g), _as2d(m), _as2d(v)], [], [(w2.shape[1], F32)] * 3, name=name)
    return tuple(o.reshape(w.shape) for o in outs)


HBM_SPEC = pl.BlockSpec(memory_space=pltpu.HBM)


def _place():
    mx, my, mc = lax.axis_index("x"), lax.axis_index("y"), lax.axis_index("c")
    others = [(1 - mx, my), (mx, 1 - my), (1 - mx, 1 - my)]
    return mx, my, mc, others


def chip_allgather(x, *, name):
    def body(x_ref, o_ref, send_sems, recv_sems, local_sem):
        mx, my, mc, others = _place()
        me = 2 * mx + my
        mine = pltpu.make_async_copy(x_ref, o_ref.at[me], local_sem)
        mine.start()
        sends = [pltpu.make_async_remote_copy(src_ref=x_ref, dst_ref=o_ref.at[me], send_sem=send_sems.at[j], recv_sem=recv_sems.at[j],
                                              device_id=(px, py, mc), device_id_type=MESH) for j, (px, py) in enumerate(others)]
        for cp in sends:
            cp.start()
        for j, (px, py) in enumerate(others):
            pltpu.make_async_remote_copy(src_ref=x_ref, dst_ref=o_ref.at[2 * px + py], send_sem=send_sems.at[j], recv_sem=recv_sems.at[j],
                                         device_id=(px, py, mc), device_id_type=MESH).wait_recv()
        for cp in sends:
            cp.wait_send()
        mine.wait()

    return pl.pallas_call(
        body, name=name, out_shape=jax.ShapeDtypeStruct((4,) + x.shape, x.dtype), in_specs=[HBM_SPEC], out_specs=HBM_SPEC,
        scratch_shapes=[pltpu.SemaphoreType.DMA((3,)), pltpu.SemaphoreType.DMA((3,)), pltpu.SemaphoreType.DMA])(x)


def chip_exchange(h, *, name):
    def body(h_ref, o_ref, send_sems, recv_sems, local_sem):
        mx, my, mc, others = _place()
        me = 2 * mx + my
        mine = pltpu.make_async_copy(h_ref.at[me], o_ref.at[me], local_sem)
        mine.start()
        sends = [pltpu.make_async_remote_copy(src_ref=h_ref.at[2 * px + py], dst_ref=o_ref.at[me], send_sem=send_sems.at[j],
                                              recv_sem=recv_sems.at[j], device_id=(px, py, mc), device_id_type=MESH)
                 for j, (px, py) in enumerate(others)]
        for cp in sends:
            cp.start()
        for j, (px, py) in enumerate(others):
            pltpu.make_async_remote_copy(src_ref=h_ref.at[me], dst_ref=o_ref.at[2 * px + py], send_sem=send_sems.at[j],
                                         recv_sem=recv_sems.at[j], device_id=(px, py, mc), device_id_type=MESH).wait_recv()
        for cp in sends:
            cp.wait_send()
        mine.wait()

    return pl.pallas_call(
        body, name=name, out_shape=jax.ShapeDtypeStruct(h.shape, h.dtype), in_specs=[HBM_SPEC], out_specs=HBM_SPEC,
        scratch_shapes=[pltpu.SemaphoreType.DMA((3,)), pltpu.SemaphoreType.DMA((3,)), pltpu.SemaphoreType.DMA])(h)


def pair_swap_half(g, *, name):
    def body(g_ref, o_ref, send_sem, recv_sem):
        mx, my, mc, _ = _place()
        cp = pltpu.make_async_remote_copy(src_ref=g_ref.at[1 - mc], dst_ref=o_ref, send_sem=send_sem, recv_sem=recv_sem,
                                          device_id=(mx, my, 1 - mc), device_id_type=MESH)
        cp.start()
        cp.wait()

    return pl.pallas_call(
        body, name=name, out_shape=jax.ShapeDtypeStruct(g.shape[1:], g.dtype), in_specs=[HBM_SPEC], out_specs=HBM_SPEC,
        scratch_shapes=[pltpu.SemaphoreType.DMA, pltpu.SemaphoreType.DMA])(g)


def pair_allgather(s, *, name):
    def body(s_ref, o_ref, send_sem, recv_sem, local_sem):
        mx, my, mc, _ = _place()
        mine = pltpu.make_async_copy(s_ref, o_ref.at[mc], local_sem)
        mine.start()
        cp = pltpu.make_async_remote_copy(src_ref=s_ref, dst_ref=o_ref.at[mc], send_sem=send_sem, recv_sem=recv_sem,
                                          device_id=(mx, my, 1 - mc), device_id_type=MESH)
        cp.start()
        pltpu.make_async_remote_copy(src_ref=s_ref, dst_ref=o_ref.at[1 - mc], send_sem=send_sem, recv_sem=recv_sem,
                                     device_id=(mx, my, 1 - mc), device_id_type=MESH).wait_recv()
        cp.wait_send()
        mine.wait()

    return pl.pallas_call(
        body, name=name, out_shape=jax.ShapeDtypeStruct((2,) + s.shape, s.dtype), in_specs=[HBM_SPEC], out_specs=HBM_SPEC,
        scratch_shapes=[pltpu.SemaphoreType.DMA, pltpu.SemaphoreType.DMA, pltpu.SemaphoreType.DMA])(s)


def add_own_half(g, t, core, *, name):
    _, nk, rows, w = g.shape
    tm = _row_tile(rows, w)

    def body(c_ref, g_ref, t_ref, o_ref):
        o_ref[...] = g_ref[0] + t_ref[...]

    return pl.pallas_call(
        body, name=name, out_shape=jax.ShapeDtypeStruct(t.shape, F32),
        grid_spec=pltpu.PrefetchScalarGridSpec(
            num_scalar_prefetch=1, grid=(nk, rows // tm),
            in_specs=[pl.BlockSpec((1, 1, tm, w), lambda k, i, c: (c[0], k, i, 0)), pl.BlockSpec((1, tm, w), lambda k, i, c: (k, i, 0))],
            out_specs=pl.BlockSpec((1, tm, w), lambda k, i, c: (k, i, 0))),
        compiler_params=_cp(("parallel", "parallel")))(core, g, t)


def sum_chips(x, *, name):
    _, rows, w = x.shape
    tm = _row_tile(rows, w)

    def body(x_ref, o_ref):
        o_ref[...] = ((x_ref[0] + x_ref[1]) + x_ref[2]) + x_ref[3]

    return pl.pallas_call(
        body, name=name, grid=(rows // tm,), in_specs=[pl.BlockSpec((4, tm, w), lambda i: (0, i, 0))],
        out_specs=pl.BlockSpec((tm, w), lambda i: (i, 0)), out_shape=jax.ShapeDtypeStruct((rows, w), F32),
        compiler_params=_cp(("parallel",)))(x)


WEIGHTS = ['c_ctx', 'w_mod', 'b_mod', 'norm1_w', 'norm2_w', 'final_norm_w', 's5_w_in', 's5_lam_re', 's5_lam_im', 's5_log_step', 's5_b_re', 's5_b_im', 's5_c_re', 's5_c_im', 's5_d', 's5_w_glu', 's5_w_out', 'hg_w_in', 'hg_lower_bounds', 'hg_gnorm_w', 'hg_w_out', 'ffn_w_up', 'ffn_conv_w', 'ffn_conv_b', 'ffn_w_down']
INPUTS = ['x', 'c', 'ctx', 'c_ctx', 'w_mod', 'b_mod', 'norm1_w', 'norm2_w', 'final_norm_w', 's5_w_in', 's5_lam_re', 's5_lam_im', 's5_log_step', 's5_b_re', 's5_b_im', 's5_c_re', 's5_c_im', 's5_d', 's5_w_glu', 's5_w_out', 'hg_w_in', 'hg_lower_bounds', 'hg_gnorm_w', 'hg_w_out', 'ffn_w_up', 'ffn_conv_w', 'ffn_conv_b', 'ffn_w_down', 'loss_target', 'm_c_ctx', 'm_w_mod', 'm_b_mod', 'm_norm1_w', 'm_norm2_w', 'm_final_norm_w', 'm_s5_w_in', 'm_s5_lam_re', 'm_s5_lam_im', 'm_s5_log_step', 'm_s5_b_re', 'm_s5_b_im', 'm_s5_c_re', 'm_s5_c_im', 'm_s5_d', 'm_s5_w_glu', 'm_s5_w_out', 'm_hg_w_in', 'm_hg_lower_bounds', 'm_hg_gnorm_w', 'm_hg_w_out', 'm_ffn_w_up', 'm_ffn_conv_w', 'm_ffn_conv_b', 'm_ffn_w_down', 'v_c_ctx', 'v_w_mod', 'v_b_mod', 'v_norm1_w', 'v_norm2_w', 'v_final_norm_w', 'v_s5_w_in', 'v_s5_lam_re', 'v_s5_lam_im', 'v_s5_log_step', 'v_s5_b_re', 'v_s5_b_im', 'v_s5_c_re', 'v_s5_c_im', 'v_s5_d', 'v_s5_w_glu', 'v_s5_w_out', 'v_hg_w_in', 'v_hg_lower_bounds', 'v_hg_gnorm_w', 'v_hg_w_out', 'v_ffn_w_up', 'v_ffn_conv_w', 'v_ffn_conv_b', 'v_ffn_w_down']
SHARD_AXIS = {"w_mod": 2, "s5_w_in": 1, "s5_w_glu": 1, "s5_w_out": 1, "hg_w_in": 2, "hg_lower_bounds": 2, "hg_w_out": 1,
              "ffn_w_up": 2, "ffn_conv_w": 2, "ffn_w_down": 1}
GATHER_F32 = ("hg_lower_bounds", "ffn_conv_w")
PACK_W = 1024


def _pack(parts, mult):
    flat = jnp.concatenate([p.reshape(-1) for p in parts])
    pad = (-flat.shape[0]) % mult
    return jnp.pad(flat, (0, pad)) if pad else flat


def _gather_weights(a, names, dtype, tag):
    pack = _pack([a[n].astype(dtype) for n in names], 16 * PACK_W).reshape(-1, PACK_W)
    got = chip_allgather(pack, name="allgather_" + tag).reshape(4, -1)
    out, off = {}, 0
    for n in names:
        ls = a[n].shape
        size = math.prod(ls)
        ax = SHARD_AXIS[n]
        blk = got[:, off:off + size].reshape((4,) + ls)
        out[n] = jnp.moveaxis(blk, 0, ax).reshape(ls[:ax] + (4 * ls[ax],) + ls[ax + 1:])
        off += size
    return out


def _reduce_grads(a, grads, core):
    sharded = [n for n in WEIGHTS if n in SHARD_AXIS]
    small = [n for n in WEIGHTS if n not in SHARD_AXIS]
    small_flat = _pack([grads[n] for n in small], 4 * 128)
    nsm = small_flat.shape[0] // 4
    rows = []
    for k in range(4):
        parts = []
        for n in sharded:
            ls, ax = a[n].shape, SHARD_AXIS[n]
            parts.append(lax.slice_in_dim(grads[n], k * ls[ax], (k + 1) * ls[ax], axis=ax))
        parts.append(small_flat[k * nsm:(k + 1) * nsm])
        rows.append(_pack(parts, 2 * 8 * PACK_W))
    p = rows[0].shape[0]
    g = jnp.stack(rows).reshape(4, 2, p // 2).transpose(1, 0, 2).reshape(2, 4, p // 2 // PACK_W, PACK_W)
    t = pair_swap_half(g, name="grad_pair_swap")
    h = add_own_half(g, t, core, name="grad_pair_add")
    e = chip_exchange(h, name="grad_chip_exchange")
    s = sum_chips(e, name="grad_chip_sum")
    red = pair_allgather(s, name="grad_pair_gather").reshape(-1)
    out, off = {}, 0
    for n in sharded:
        ls = a[n].shape
        out[n] = red[off:off + math.prod(ls)].reshape(ls)
        off += math.prod(ls)
    sm = chip_allgather(_pack([red[off:off + nsm]], 8 * 128).reshape(-1, 128), name="allgather_small_grads").reshape(4, -1)[:, :nsm].reshape(-1)
    off = 0
    for n in small:
        out[n] = sm[off:off + math.prod(a[n].shape)].reshape(a[n].shape)
        off += math.prod(a[n].shape)
    return out


def _blockdiag_b(bb, kb):
    gl = S5_KIN // S5_GROUP
    x = bb.reshape(kb, gl, S5_GROUP, S5_STATE)
    return (x[:, :, :, None, :] * jnp.eye(gl, dtype=bb.dtype)[None, :, None, :, None]).reshape(kb, S5_KIN, S5_KST)


def _blockdiag_c(cc, kb):
    gl = S5_KIN // S5_GROUP
    x = cc.reshape(kb, gl, S5_GROUP, S5_STATE).transpose(0, 1, 3, 2)
    return (x[:, :, :, None, :] * jnp.eye(gl, dtype=cc.dtype)[None, :, None, :, None]).reshape(kb, S5_KST, S5_KIN)


def _diag_b(m, kb):
    gl = S5_KIN // S5_GROUP
    x = m.reshape(kb, gl, S5_GROUP, gl, S5_STATE)
    return jnp.stack([x[:, i, :, i, :] for i in range(gl)], axis=1).reshape(kb * gl, S5_GROUP, S5_STATE)


def _diag_c(m, kb):
    gl = S5_KIN // S5_GROUP
    x = m.reshape(kb, gl, S5_STATE, gl, S5_GROUP)
    return jnp.stack([x[:, i, :, i, :] for i in range(gl)], axis=1).transpose(0, 1, 3, 2).reshape(kb * gl, S5_GROUP, S5_STATE)


def kernel(x, c, ctx, c_ctx, w_mod, b_mod, norm1_w, norm2_w, final_norm_w, s5_w_in, s5_lam_re, s5_lam_im, s5_log_step, s5_b_re, s5_b_im, s5_c_re, s5_c_im, s5_d, s5_w_glu, s5_w_out, hg_w_in, hg_lower_bounds, hg_gnorm_w, hg_w_out, ffn_w_up, ffn_conv_w, ffn_conv_b, ffn_w_down, loss_target, m_c_ctx, m_w_mod, m_b_mod, m_norm1_w, m_norm2_w, m_final_norm_w, m_s5_w_in, m_s5_lam_re, m_s5_lam_im, m_s5_log_step, m_s5_b_re, m_s5_b_im, m_s5_c_re, m_s5_c_im, m_s5_d, m_s5_w_glu, m_s5_w_out, m_hg_w_in, m_hg_lower_bounds, m_hg_gnorm_w, m_hg_w_out, m_ffn_w_up, m_ffn_conv_w, m_ffn_conv_b, m_ffn_w_down, v_c_ctx, v_w_mod, v_b_mod, v_norm1_w, v_norm2_w, v_final_norm_w, v_s5_w_in, v_s5_lam_re, v_s5_lam_im, v_s5_log_step, v_s5_b_re, v_s5_b_im, v_s5_c_re, v_s5_c_im, v_s5_d, v_s5_w_glu, v_s5_w_out, v_hg_w_in, v_hg_lower_bounds, v_hg_gnorm_w, v_hg_w_out, v_ffn_w_up, v_ffn_conv_w, v_ffn_conv_b, v_ffn_w_down):
    a = dict(zip(INPUTS, (x, c, ctx, c_ctx, w_mod, b_mod, norm1_w, norm2_w, final_norm_w, s5_w_in, s5_lam_re, s5_lam_im, s5_log_step, s5_b_re, s5_b_im, s5_c_re, s5_c_im, s5_d, s5_w_glu, s5_w_out, hg_w_in, hg_lower_bounds, hg_gnorm_w, hg_w_out, ffn_w_up, ffn_conv_w, ffn_conv_b, ffn_w_down, loss_target, m_c_ctx, m_w_mod, m_b_mod, m_norm1_w, m_norm2_w, m_final_norm_w, m_s5_w_in, m_s5_lam_re, m_s5_lam_im, m_s5_log_step, m_s5_b_re, m_s5_b_im, m_s5_c_re, m_s5_c_im, m_s5_d, m_s5_w_glu, m_s5_w_out, m_hg_w_in, m_hg_lower_bounds, m_hg_gnorm_w, m_hg_w_out, m_ffn_w_up, m_ffn_conv_w, m_ffn_conv_b, m_ffn_w_down, v_c_ctx, v_w_mod, v_b_mod, v_norm1_w, v_norm2_w, v_final_norm_w, v_s5_w_in, v_s5_lam_re, v_s5_lam_im, v_s5_log_step, v_s5_b_re, v_s5_b_im, v_s5_c_re, v_s5_c_im, v_s5_d, v_s5_w_glu, v_s5_w_out, v_hg_w_in, v_hg_lower_bounds, v_hg_gnorm_w, v_hg_w_out, v_ffn_w_up, v_ffn_conv_w, v_ffn_conv_b, v_ffn_w_down)))
    nb, seq, d = x.shape
    assert nb == NB
    rc = nb * ctx.shape[1]
    cfg = {"rc": rc}
    f = a["ffn_w_down"].shape[1] * 4
    core = lax.axis_index("c").astype(jnp.int32).reshape(1)

    w = {n: a[n] for n in WEIGHTS if n not in SHARD_AXIS}
    w.update(_gather_weights(a, [n for n in WEIGHTS if n in SHARD_AXIS and n not in GATHER_F32], MXU, "matmul_weights"))
    w.update(_gather_weights(a, list(GATHER_F32), F32, "f32_weights"))

    tmaj = lambda t: t.transpose(1, 0, 2).reshape(-1, t.shape[-1])
    x0 = jnp.concatenate([tmaj(ctx), tmaj(x)], axis=0)
    tgt = tmaj(a["loss_target"])
    c16 = jnp.concatenate([jnp.broadcast_to(c_ctx[None], (8, d)), c, c], axis=0)
    mt, scb = [], None
    for l in range(2):
        m_, scb = mod_fwd(c16, w["w_mod"][l], w["b_mod"][l][None], name=f"mod_fwd{l}")
        mt.append(m_)
    n1, n2 = w["norm1_w"], w["norm2_w"]

    def ffn_fwd(l, h):
        u = mm(h, w["ffn_w_up"][l], name=f"ffn_up{l}")
        act = ffn_mid_fwd(cfg, u, w["ffn_conv_w"][l], w["ffn_conv_b"][l][None], name=f"ffn_mid{l}")
        return u, act, mm(act, w["ffn_w_down"][l], name=f"ffn_down{l}")

    def ffn_bwd(l, dfo, u, act, h):
        dact = mm(dfo, w["ffn_w_down"][l], tb=True, name=f"ffn_down_dx{l}")
        dwd = mm(act, dfo, ta=True, name=f"ffn_down_dw{l}")
        du, dcw, dcb = ffn_mid_bwd(cfg, dact, u, w["ffn_conv_w"][l], w["ffn_conv_b"][l][None], name=f"ffn_mid_bwd{l}")
        dh = mm(du, w["ffn_w_up"][l], tb=True, name=f"ffn_up_dx{l}")
        dwu = mm(h, du, ta=True, name=f"ffn_up_dw{l}")
        return dh, dwu, dcw, dcb[0], dwd

    g_, p_ = d // S5_GROUP, S5_STATE
    ns, kb = g_ * p_, d // S5_KIN
    s5p = (w["s5_lam_re"][0].reshape(2 * g_, p_), w["s5_lam_im"][0].reshape(2 * g_, p_), w["s5_log_step"][0].reshape(2 * g_, 1),
           w["s5_b_re"][0].transpose(0, 1, 3, 2).reshape(2 * g_, S5_GROUP, p_), w["s5_b_im"][0].transpose(0, 1, 3, 2).reshape(2 * g_, S5_GROUP, p_))
    ar, ai, bbr, bbi = s5_disc_fwd(*s5p, name="s5_disc")
    dsk = w["s5_d"]
    _, h1 = node_fwd(cfg, x0, None, None, 0, n1[0:1], mt[0], 0, name="node0a")
    u0 = mm(h1, w["s5_w_in"][0], name="s5_in")
    s5s, ys = [], []
    for dd in range(2):
        sl = slice(dd * g_, (dd + 1) * g_)
        prm = (ar[sl].reshape(1, ns), ai[sl].reshape(1, ns), _blockdiag_b(bbr[sl], kb).astype(MXU), _blockdiag_b(bbi[sl], kb).astype(MXU),
               _blockdiag_c(w["s5_c_re"][0, dd], kb).astype(MXU), _blockdiag_c(w["s5_c_im"][0, dd], kb).astype(MXU))
        sre, sim, ere, eim, y_ = s5_scan_fwd(cfg, u0, *prm, rev=dd == 1, name=f"s5_scan{dd}")
        s5s.append((sre, sim, ere, eim) + prm)
        ys.append(y_)

    def glu_a(u, y0, y1, ds):
        yp = (ds * u + y0) + y1
        return yp, _gelu(yp)

    ypre, zgb = rowmap(glu_a, [u0, ys[0], ys[1]], [dsk], [(d, F32), (d, MXU)], name="s5_glu_a")
    tg = mm(zgb, w["s5_w_glu"][0], name="s5_glu")
    (z2,) = rowmap(lambda yp, t: _gelu(yp) * jax.nn.sigmoid(t), [ypre, tg], [], [(d, MXU)], name="s5_glu_b")
    y1a = mm(z2, w["s5_w_out"][0], name="s5_out")
    x1a, h2a = node_fwd(cfg, x0, y1a, mt[0], 2, n2[0:1], mt[0], 3, name="node0b")
    ufa, acta, foa = ffn_fwd(0, h2a)

    x2a, h1b = node_fwd(cfg, x1a, foa, mt[0], 5, n1[1:2], mt[1], 0, name="node1a")
    z = mm(h1b, w["hg_w_in"][0], name="hg_in")
    e0, e1 = w["hg_lower_bounds"][:, 0, :], w["hg_lower_bounds"][:, 1, :]
    lb = hg_lb_fwd(e0, e1, name="hg_lb")
    gw = w["hg_gnorm_w"]
    o0, sin0 = hg_scan_fwd(cfg, z, lb[0:1], d_dir=0, name="hg_scan0")
    o1, sin1 = hg_scan_fwd(cfg, z, lb[1:2], d_dir=1, name="hg_scan1")
    onb = hg_read_fwd(o0, o1, z, gw, name="hg_read")
    y1b = mm(onb, w["hg_w_out"][0], name="hg_out")
    x1b, h2b = node_fwd(cfg, x2a, y1b, mt[1], 2, n2[1:2], mt[1], 3, name="node1b")
    ufb, actb, fob = ffn_fwd(1, h2b)
    loss_p, dx2b, dfob, dg2_1, dfnw = final_node(cfg, x1b, fob, mt[1], 5, w["final_norm_w"][None], tgt, name="final_node")

    gr = {}
    dh2b, dwu1, dcw1, dcb1, dwd1 = ffn_bwd(1, dfob, ufb, actb, h2b)
    dx1b, dy1b, dn2_1, dsh2_1, dsc2_1, dg1_1 = node_bwd(cfg, dx2b, dh2b, x1b, y1b, mt[1], 2, n2[1:2], mt[1], 3, name="node1b_bwd")
    don = mm(dy1b, w["hg_w_out"][0], tb=True, name="hg_out_dx")
    gr["hg_w_out"] = mm(onb, dy1b, ta=True, name="hg_out_dw")[None]
    do_, dgate_, dgw = hg_read_bwd(don, o0, o1, z, gw, name="hg_read_bwd")
    dq, dv, dxf, dlb0 = hg_scan_bwd(cfg, do_, z, lb[0:1], sin0, None, None, d_dir=0, name="hg_scan_bwd0")
    dq, dv, dxb, dlb1 = hg_scan_bwd(cfg, do_, z, lb[1:2], sin1, dq, dv, d_dir=1, name="hg_scan_bwd1")
    dz = jnp.concatenate([t_.astype(MXU) for t_ in (dq, dv, dxf, dxb, dgate_)], axis=1)
    dh1b = mm(dz, w["hg_w_in"][0], tb=True, name="hg_in_dx")
    gr["hg_w_in"] = mm(h1b, dz, ta=True, name="hg_in_dw")[None]
    de0, de1 = hg_lb_bwd(e0, e1, jnp.concatenate([dlb0, dlb1], axis=0), name="hg_lb_bwd")
    gr["hg_lower_bounds"] = jnp.stack([de0, de1], axis=1)
    gr["hg_gnorm_w"] = dgw
    dx2a, dfoa, dn1_1, dsh1_1, dsc1_1, dg2_0 = node_bwd(cfg, dx1b, dh1b, x2a, foa, mt[0], 5, n1[1:2], mt[1], 0, name="node1a_bwd")

    dh2a, dwu0, dcw0, dcb0, dwd0 = ffn_bwd(0, dfoa, ufa, acta, h2a)
    dx1a, dy1a, dn2_0, dsh2_0, dsc2_0, dg1_0 = node_bwd(cfg, dx2a, dh2a, x1a, y1a, mt[0], 2, n2[0:1], mt[0], 3, name="node0b_bwd")
    dz2 = mm(dy1a, w["s5_w_out"][0], tb=True, name="s5_out_dx")
    gr["s5_w_out"] = mm(z2, dy1a, ta=True, name="s5_out_dw")[None]

    def glu_b_bwd(dz2_, yp, t):
        zg, sg = _gelu(yp), jax.nn.sigmoid(t)
        return dz2_ * zg * sg * (1.0 - sg), dz2_ * sg

    dtg, dzg_dir = rowmap(glu_b_bwd, [dz2, ypre, tg], [], [(d, MXU), (d, F32)], name="s5_glu_b_bwd")
    dzg_mm = mm(dtg, w["s5_w_glu"][0], tb=True, name="s5_glu_dx")
    gr["s5_w_glu"] = mm(zgb, dtg, ta=True, name="s5_glu_dw")[None]

    def glu_a_bwd(dzd, dzm, yp, u, ds):
        _, vjp = jax.vjp(_gelu, yp)
        (dy,) = vjp(dzd + dzm)
        return dy, dy * ds, jnp.sum(dy * u, axis=0, keepdims=True)

    dyb, du, ddsk = rowmap(glu_a_bwd, [dzg_dir, dzg_mm, ypre, u0], [dsk], [(d, MXU), (d, F32)], [(1, d)], name="s5_glu_a_bwd")
    gr["s5_d"] = ddsk
    dar, dai, dbr, dbi, dcr, dci = [], [], [], [], [], []
    for dd in range(2):
        sre, sim, ere, eim, a_re, a_im, bre, bim, cre, cim = s5s[dd]
        du, gre, gim, da_r, da_i = s5_scan_bwd(cfg, dyb, sre, sim, ere, eim, a_re, a_im, bre, bim, cre, cim, du, rev=dd == 1,
                                               name=f"s5_scan_bwd{dd}")
        dar.append(colsum(da_r, name=f"s5_da_re{dd}").reshape(g_, p_))
        dai.append(colsum(da_i, name=f"s5_da_im{dd}").reshape(g_, p_))
        dbr.append(_diag_b(blockdiag_tn(u0, gre, S5_KIN, S5_KST, name=f"s5_db_re{dd}"), kb))
        dbi.append(_diag_b(blockdiag_tn(u0, gim, S5_KIN, S5_KST, name=f"s5_db_im{dd}"), kb))
        dcr.append(_diag_c(blockdiag_tn(sre.reshape(-1, ns), dyb, S5_KST, S5_KIN, name=f"s5_dc_re{dd}"), kb))
        dci.append(_diag_c(blockdiag_tn(sim.reshape(-1, ns), dyb, S5_KST, S5_KIN, scale=-1.0, name=f"s5_dc_im{dd}"), kb))
    cat = lambda l_: jnp.concatenate(l_, axis=0)
    dlr, dli, dls, dbre, dbim = s5_disc_bwd(*s5p, cat(dar), cat(dai), cat(dbr), cat(dbi), name="s5_disc_bwd")
    gr["s5_lam_re"], gr["s5_lam_im"] = dlr.reshape(1, 2, g_, p_), dli.reshape(1, 2, g_, p_)
    gr["s5_log_step"] = dls.reshape(1, 2, g_)
    gr["s5_b_re"] = dbre.reshape(1, 2, g_, S5_GROUP, p_).transpose(0, 1, 2, 4, 3)
    gr["s5_b_im"] = dbim.reshape(1, 2, g_, S5_GROUP, p_).transpose(0, 1, 2, 4, 3)
    gr["s5_c_re"], gr["s5_c_im"] = jnp.stack(dcr)[None], jnp.stack(dci)[None]
    dh1 = mm(du, w["s5_w_in"][0], tb=True, name="s5_in_dx")
    gr["s5_w_in"] = mm(h1, du, ta=True, name="s5_in_dw")[None]
    dx0, _, dn1_0, dsh1_0, dsc1_0, _ = node_bwd(cfg, dx1a, dh1, x0, None, None, 0, n1[0:1], mt[0], 0, name="node0a_bwd")

    dmt = [jnp.concatenate([dsh1_0, dsc1_0, dg1_0, dsh2_0, dsc2_0, dg2_0], axis=1),
           jnp.concatenate([dsh1_1, dsc1_1, dg1_1, dsh2_1, dsc2_1, dg2_1], axis=1)]
    gr["w_mod"] = jnp.stack([mm(scb, dmt[l], ta=True, name=f"mod_dw{l}") for l in range(2)])
    gr["b_mod"] = jnp.concatenate([colsum(dmt[l], name=f"mod_db{l}") for l in range(2)], axis=0)
    dsc16 = [mm(dmt[l], w["w_mod"][l], tb=True, name=f"mod_dx{l}") for l in range(2)]
    gr["c_ctx"] = cctx_grad(c16, dsc16, name="c_ctx_grad")[0]
    gr["norm1_w"] = jnp.concatenate([dn1_0, dn1_1], axis=0)
    gr["norm2_w"] = jnp.concatenate([dn2_0, dn2_1], axis=0)
    gr["final_norm_w"] = dfnw[0]
    gr["ffn_w_up"], gr["ffn_conv_w"] = jnp.stack([dwu0, dwu1]), jnp.stack([dcw0, dcw1])
    gr["ffn_conv_b"], gr["ffn_w_down"] = jnp.stack([dcb0, dcb1]), jnp.stack([dwd0, dwd1])

    red = _reduce_grads(a, gr, core)
    loss = lax.psum(loss_p[0, 0], ("x", "y", "c"))
    grad_x = dx0[rc:].reshape(seq, nb, d).transpose(1, 0, 2)
    upd = {n: adamw(a[n], red[n], a["m_" + n], a["v_" + n], name="adamw_" + n) for n in WEIGHTS}
    return (loss, grad_x, *[red[n] for n in WEIGHTS], *[upd[n][0] for n in WEIGHTS], *[upd[n][1] for n in WEIGHTS],
            *[upd[n][2] for n in WEIGHTS])
```

```python
import functools
import math

import jax
import jax.numpy as jnp
from jax import lax
from jax.experimental import pallas as pl
from jax.experimental.pallas import tpu as pltpu

F32 = jnp.float32
BF = jnp.bfloat16
MXU = jnp.bfloat16

NORM_EPS = 1e-6
GRID_W = 64
N_MOD = 6
S5_GROUP = 16
S5_STATE = 64
S5_LAM_RE_MAX = -1e-4
S5_KIN = 256
S5_KST = S5_KIN // S5_GROUP * S5_STATE
HEAD = 128
CHUNK_ROWS = 128
N_PROJ = 5
NB = 4
ADAM_LR, ADAM_B1, ADAM_B2, ADAM_EPS, ADAM_WD, ADAM_STEP = 0.001, 0.9, 0.999, 1e-08, 0.01, 10
VMEM_LIMIT = 56 * 1024 * 1024
MESH = pl.DeviceIdType.MESH


def _tile(n, cap):
    if n <= cap:
        return n
    best = None
    for t in range(128, cap + 1, 128):
        if n % t == 0:
            best = t
    assert best is not None, (n, cap)
    return best


def _row_tile(r, width=1024):
    cap = max(8, (512 * 1024) // max(width, 1))
    return next((t for t in (512, 256, 128, 64, 32, 16, 8) if t <= cap and r % t == 0), r)


def _cp(sem):
    return pltpu.CompilerParams(dimension_semantics=sem, vmem_limit_bytes=VMEM_LIMIT)


def _dot(a, b, ca=1, cb=0):
    return lax.dot_general(a.astype(MXU), b.astype(MXU), (((ca,), (cb,)), ((), ())), preferred_element_type=F32)


def _dot3(m, x):
    hi = x.astype(MXU)
    r1 = x - hi.astype(F32)
    mid = r1.astype(MXU)
    lo = (r1 - mid.astype(F32)).astype(MXU)
    return _dot(m, hi) + _dot(m, mid) + _dot(m, lo)


def mm(a, b, *, ta=False, tb=False, out_dtype=F32, name):
    (kd, m) = a.shape if ta else a.shape[::-1]
    (n, kd2) = b.shape if tb else b.shape[::-1]
    assert kd == kd2, (a.shape, b.shape, ta, tb)
    tm, tn, tk = _tile(m, 1024), _tile(n, 1536), _tile(kd, 1024)
    nk = kd // tk

    def body(a_ref, b_ref, o_ref, acc_ref):
        k = pl.program_id(2)

        @pl.when(k == 0)
        def _():
            acc_ref[...] = jnp.zeros_like(acc_ref)

        acc_ref[...] += _dot(a_ref[...], b_ref[...], 0 if ta else 1, 1 if tb else 0)

        @pl.when(k == nk - 1)
        def _():
            o_ref[...] = acc_ref[...].astype(out_dtype)

    a_spec = pl.BlockSpec((tk, tm), lambda i, j, k: (k, i)) if ta else pl.BlockSpec((tm, tk), lambda i, j, k: (i, k))
    b_spec = pl.BlockSpec((tn, tk), lambda i, j, k: (j, k)) if tb else pl.BlockSpec((tk, tn), lambda i, j, k: (k, j))
    return pl.pallas_call(
        body, name=name, grid=(m // tm, n // tn, nk), in_specs=[a_spec, b_spec],
        out_specs=pl.BlockSpec((tm, tn), lambda i, j, k: (i, j)), out_shape=jax.ShapeDtypeStruct((m, n), out_dtype),
        scratch_shapes=[pltpu.VMEM((tm, tn), F32)], compiler_params=_cp(("parallel", "parallel", "arbitrary")))(a, b)


def blockdiag_tn(a, b, wa, wb, *, scale=1.0, name):
    rows = a.shape[0]
    kb = a.shape[1] // wa
    tr = _tile(rows, 1024)
    nr = rows // tr

    def body(a_ref, b_ref, o_ref):
        i = pl.program_id(1)

        @pl.when(i == 0)
        def _():
            o_ref[...] = jnp.zeros_like(o_ref)

        o_ref[0] += scale * _dot(a_ref[...], b_ref[...], 0, 0)

    return pl.pallas_call(
        body, name=name, grid=(kb, nr),
        in_specs=[pl.BlockSpec((tr, wa), lambda k, i: (i, k)), pl.BlockSpec((tr, wb), lambda k, i: (i, k))],
        out_specs=pl.BlockSpec((1, wa, wb), lambda k, i: (k, 0, 0)), out_shape=jax.ShapeDtypeStruct((kb, wa, wb), F32),
        compiler_params=_cp(("parallel", "arbitrary")))(a, b)


def _pat(v, p, op):
    tm, d = v.shape
    return op(v.reshape(tm // 8, 8, d), p[None]).reshape(tm, d)


def _norm_mod(x, nw, shift, scale):
    y = x * lax.rsqrt(jnp.mean(x * x, axis=-1, keepdims=True) + NORM_EPS) * nw
    return _pat(_pat(y, 1.0 + scale, jnp.multiply), shift, jnp.add)


def _mt_spec(d, nct):
    return pl.BlockSpec((8, N_MOD * d), lambda i: (jnp.where(i < nct, 0, 1), 0))


def _acc_spec(d, nct):
    return pl.BlockSpec((8, d), lambda i: (jnp.where(i < nct, 0, 1), 0))


def _rows(cfg):
    tm = min(512, cfg["rc"])
    return tm, cfg["rc"] // tm


def node_fwd(cfg, xp, y, mtg, gi, nw, mtn, si, *, name):
    r, d = xp.shape
    tm, nct = _rows(cfg)
    row = pl.BlockSpec((tm, d), lambda i: (i, 0))
    vec = pl.BlockSpec((1, d), lambda i: (0, 0))

    def body(*refs):
        if y is None:
            xp_ref, nw_ref, mtn_ref, h_ref = refs
            x = xp_ref[...]
        else:
            xp_ref, y_ref, mtg_ref, nw_ref, mtn_ref, xn_ref, h_ref = refs
            x = xp_ref[...] + _pat(y_ref[...], mtg_ref[:, gi * d:(gi + 1) * d], jnp.multiply)
            xn_ref[...] = x
        h_ref[...] = _norm_mod(x, nw_ref[...], mtn_ref[:, si * d:(si + 1) * d], mtn_ref[:, (si + 1) * d:(si + 2) * d]).astype(MXU)

    h_shape = jax.ShapeDtypeStruct((r, d), MXU)
    if y is None:
        h = pl.pallas_call(body, name=name, grid=(r // tm,), in_specs=[row, vec, _mt_spec(d, nct)], out_specs=row,
                           out_shape=h_shape, compiler_params=_cp(("parallel",)))(xp, nw, mtn)
        return xp, h
    return pl.pallas_call(body, name=name, grid=(r // tm,), in_specs=[row, row, _mt_spec(d, nct), vec, _mt_spec(d, nct)],
                          out_specs=(row, row), out_shape=(jax.ShapeDtypeStruct((r, d), F32), h_shape),
                          compiler_params=_cp(("parallel",)))(xp, y, mtg, nw, mtn)


def node_bwd(cfg, dxres, dh, xn, y, mtg, gi, nw, mtn, si, *, name):
    r, d = xn.shape
    tm, nct = _rows(cfg)
    row = pl.BlockSpec((tm, d), lambda i: (i, 0))
    vec = pl.BlockSpec((1, d), lambda i: (0, 0))
    has_y = y is not None

    def body(*refs):
        if has_y:
            dxres_ref, dh_ref, xn_ref, y_ref, mtg_ref, nw_ref, mtn_ref, dxn_ref, dy_ref, dnw_ref, dsh_ref, dsc_ref, dg_ref = refs
        else:
            dxres_ref, dh_ref, xn_ref, nw_ref, mtn_ref, dxn_ref, dnw_ref, dsh_ref, dsc_ref = refs
        i = pl.program_id(0)
        _, vjp = jax.vjp(_norm_mod, xn_ref[...], nw_ref[...], mtn_ref[:, si * d:(si + 1) * d], mtn_ref[:, (si + 1) * d:(si + 2) * d])
        dx, dnw, dsh, dsc = vjp(dh_ref[...])
        dx = dx + dxres_ref[...]
        dxn_ref[...] = dx

        @pl.when(i == 0)
        def _():
            dnw_ref[...] = jnp.zeros_like(dnw_ref)

        @pl.when((i == 0) | (i == nct))
        def _():
            dsh_ref[...] = jnp.zeros_like(dsh_ref)
            dsc_ref[...] = jnp.zeros_like(dsc_ref)
            if has_y:
                dg_ref[...] = jnp.zeros_like(dg_ref)

        dnw_ref[...] += dnw
        dsh_ref[...] += dsh
        dsc_ref[...] += dsc
        if has_y:
            dy_ref[...] = _pat(dx, mtg_ref[:, gi * d:(gi + 1) * d], jnp.multiply).astype(MXU)
            dg_ref[...] += jnp.sum((dx * y_ref[...]).reshape(tm // 8, 8, d), axis=0)

    acc = jax.ShapeDtypeStruct((16, d), F32)
    xs = jax.ShapeDtypeStruct((r, d), F32)
    if has_y:
        return pl.pallas_call(
            body, name=name, grid=(r // tm,), in_specs=[row, row, row, row, _mt_spec(d, nct), vec, _mt_spec(d, nct)],
            out_specs=(row, row, vec, _acc_spec(d, nct), _acc_spec(d, nct), _acc_spec(d, nct)),
            out_shape=(xs, jax.ShapeDtypeStruct((r, d), MXU), jax.ShapeDtypeStruct((1, d), F32), acc, acc, acc),
            compiler_params=_cp(("arbitrary",)))(dxres, dh, xn, y, mtg, nw, mtn)
    dxn, dnw, dsh, dsc = pl.pallas_call(
        body, name=name, grid=(r // tm,), in_specs=[row, row, row, vec, _mt_spec(d, nct)],
        out_specs=(row, vec, _acc_spec(d, nct), _acc_spec(d, nct)),
        out_shape=(xs, jax.ShapeDtypeStruct((1, d), F32), acc, acc), compiler_params=_cp(("arbitrary",)))(dxres, dh, xn, nw, mtn)
    return dxn, None, dnw, dsh, dsc, None


def final_node(cfg, xp, y, mtg, gi, fnw, tgt, *, name):
    r, d = xp.shape
    tm, nct = _rows(cfg)
    row = pl.BlockSpec((tm, d), lambda i: (i, 0))
    vec = pl.BlockSpec((1, d), lambda i: (0, 0))

    def norm(x, w):
        return x * lax.rsqrt(jnp.mean(x * x, axis=-1, keepdims=True) + NORM_EPS) * w

    def body(xp_ref, y_ref, mtg_ref, fnw_ref, tgt_ref, loss_ref, dx_ref, dy_ref, dg_ref, dfnw_ref):
        i = pl.program_id(0)
        g = mtg_ref[:, gi * d:(gi + 1) * d]
        x = xp_ref[...] + _pat(y_ref[...], g, jnp.multiply)
        out, vjp = jax.vjp(norm, x, fnw_ref[...])
        lat = i >= nct
        err = jnp.where(lat, out - tgt_ref[...], 0.0)
        dx, dfnw = vjp(err * (1.0 / d))

        @pl.when(i == 0)
        def _():
            loss_ref[...] = jnp.zeros_like(loss_ref)
            dfnw_ref[...] = jnp.zeros_like(dfnw_ref)

        @pl.when((i == 0) | (i == nct))
        def _():
            dg_ref[...] = jnp.zeros_like(dg_ref)

        loss_ref[...] += jnp.full(loss_ref.shape, 0.5 / d * jnp.sum(err * err), F32)
        dfnw_ref[...] += dfnw
        dx_ref[...] = dx
        dy_ref[...] = _pat(dx, g, jnp.multiply).astype(MXU)
        dg_ref[...] += jnp.sum((dx * y_ref[...]).reshape(tm // 8, 8, d), axis=0)

    return pl.pallas_call(
        body, name=name, grid=(r // tm,),
        in_specs=[row, row, _mt_spec(d, nct), vec, pl.BlockSpec((tm, d), lambda i: (jnp.maximum(i - nct, 0), 0))],
        out_specs=(pl.BlockSpec((8, 128), lambda i: (0, 0)), row, row, _acc_spec(d, nct), vec),
        out_shape=(jax.ShapeDtypeStruct((8, 128), F32), jax.ShapeDtypeStruct((r, d), F32), jax.ShapeDtypeStruct((r, d), MXU),
                   jax.ShapeDtypeStruct((16, d), F32), jax.ShapeDtypeStruct((1, d), F32)),
        compiler_params=_cp(("arbitrary",)))(xp, y, mtg, fnw, tgt)


def _silu(x):
    return x * jax.nn.sigmoid(x)


def mod_fwd(c16, w, b, *, name):
    d, n = w.shape
    tn = _tile(n, 1536)

    def body(c_ref, w_ref, b_ref, o_ref, s_ref):
        s = _silu(c_ref[...])
        s_ref[...] = s.astype(MXU)
        o_ref[...] = _dot(s, w_ref[...]) + b_ref[...]

    return pl.pallas_call(
        body, name=name, grid=(n // tn,),
        in_specs=[pl.BlockSpec((16, d), lambda j: (0, 0)), pl.BlockSpec((d, tn), lambda j: (0, j)), pl.BlockSpec((1, tn), lambda j: (0, j))],
        out_specs=(pl.BlockSpec((16, tn), lambda j: (0, j)), pl.BlockSpec((16, d), lambda j: (0, 0))),
        out_shape=(jax.ShapeDtypeStruct((16, n), F32), jax.ShapeDtypeStruct((16, d), MXU)),
        compiler_params=_cp(("arbitrary",)))(c16, w, b)


def colsum(x, *, name):
    def body(x_ref, o_ref):
        o_ref[...] = jnp.sum(x_ref[...], axis=0, keepdims=True)

    return pl.pallas_call(body, name=name, out_shape=jax.ShapeDtypeStruct((1, x.shape[1]), F32))(x)


def cctx_grad(c16, ds_list, *, name):
    def body(c_ref, *refs):
        o_ref = refs[-1]
        ds = refs[0][...]
        for r_ in refs[1:-1]:
            ds = ds + r_[...]
        _, vjp = jax.vjp(_silu, c_ref[...])
        (dc,) = vjp(ds)
        o_ref[...] = jnp.sum(dc[0:8], axis=0, keepdims=True)

    return pl.pallas_call(body, name=name, out_shape=jax.ShapeDtypeStruct((1, c16.shape[1]), F32))(c16, *ds_list)


def _s5_disc(lam_re, lam_im, log_step, b_re, b_im):
    lr = jnp.minimum(lam_re, S5_LAM_RE_MAX)
    li = lam_im
    dt = jnp.exp(log_step)
    mag = jnp.exp(lr * dt)
    abar_r = mag * jnp.cos(li * dt)
    abar_i = mag * jnp.sin(li * dt)
    den = lr * lr + li * li
    nr = abar_r - 1.0
    coef_r = (nr * lr + abar_i * li) / den
    coef_i = (abar_i * lr - nr * li) / den
    bbar_r = coef_r[:, None, :] * b_re - coef_i[:, None, :] * b_im
    bbar_i = coef_r[:, None, :] * b_im + coef_i[:, None, :] * b_re
    return abar_r, abar_i, bbar_r, bbar_i


def s5_disc_fwd(lam_re, lam_im, log_step, b_re, b_im, *, name):
    def body(lr, li, ls, br, bi, ar_o, ai_o, br_o, bi_o):
        ar_o[...], ai_o[...], br_o[...], bi_o[...] = _s5_disc(lr[...], li[...], ls[...], br[...], bi[...])

    s2, s3 = jax.ShapeDtypeStruct(lam_re.shape, F32), jax.ShapeDtypeStruct(b_re.shape, F32)
    return pl.pallas_call(body, name=name, out_shape=(s2, s2, s3, s3))(lam_re, lam_im, log_step, b_re, b_im)


def s5_disc_bwd(lam_re, lam_im, log_step, b_re, b_im, d_ar, d_ai, d_br, d_bi, *, name):
    def body(lr, li, ls, br, bi, dar, dai, dbr, dbi, o_lr, o_li, o_ls, o_br, o_bi):
        _, vjp = jax.vjp(_s5_disc, lr[...], li[...], ls[...], br[...], bi[...])
        o_lr[...], o_li[...], o_ls[...], o_br[...], o_bi[...] = vjp((dar[...], dai[...], dbr[...], dbi[...]))

    s2, s3 = jax.ShapeDtypeStruct(lam_re.shape, F32), jax.ShapeDtypeStruct(b_re.shape, F32)
    return pl.pallas_call(body, name=name, out_shape=(s2, s2, jax.ShapeDtypeStruct(log_step.shape, F32), s3, s3))(
        lam_re, lam_im, log_step, b_re, b_im, d_ar, d_ai, d_br, d_bi)


S5_LANES = 512


def _chunk_order(k, ncc, nch, rev):
    if not rev:
        return k
    return jnp.where(k < ncc, ncc - 1 - k, nch - 1 - (k - ncc))


def _cmul(ar, ai, xr, xi):
    return ar * xr - ai * xi, ar * xi + ai * xr


def s5_scan_fwd(cfg, u, a_re, a_im, bre, bim, cre, cim, *, rev, name):
    r, d = u.shape
    ns = a_re.shape[1]
    kb = d // S5_KIN
    tcr = 128
    n8 = tcr // 8
    nch, ncc = r // tcr, cfg["rc"] // tcr
    lw = min(S5_LANES, ns)

    def body(u_ref, ar_ref, ai_ref, bre_ref, bim_ref, cre_ref, cim_ref, sre_ref, sim_ref, ere_ref, eim_ref, y_ref, st_re, st_im):
        k = pl.program_id(0)

        @pl.when(k == 0)
        def _():
            st_re[...] = jnp.zeros_like(st_re)
            st_im[...] = jnp.zeros_like(st_im)

        ub = u_ref[...].astype(MXU)
        for j in range(kb):
            uj = ub[:, j * S5_KIN:(j + 1) * S5_KIN]
            sre_ref[:, :, j * S5_KST:(j + 1) * S5_KST] = _dot(uj, bre_ref[j]).reshape(n8, 8, S5_KST)
            sim_ref[:, :, j * S5_KST:(j + 1) * S5_KST] = _dot(uj, bim_ref[j]).reshape(n8, 8, S5_KST)
        ere_ref[0] = st_re[...]
        eim_ref[0] = st_im[...]
        first = lax.broadcasted_iota(jnp.int32, (8, lw), 0) < 4
        if rev:
            first = jnp.logical_not(first)
        for c in range(ns // lw):
            sl = slice(c * lw, (c + 1) * lw)
            ar = jnp.broadcast_to(ar_ref[:, sl], (8, lw))
            ai = jnp.broadcast_to(ai_ref[:, sl], (8, lw))

            def step(i, carry, sl=sl, ar=ar, ai=ai):
                sr, si = carry
                ii = n8 - 1 - i if rev else i
                tr, ti = sre_ref[ii, :, sl], sim_ref[ii, :, sl]
                pr, pi = _cmul(ar, ai, sr, si)
                t1r, t1i = pr + tr, pi + ti
                s1r = jnp.where(first, t1r, pltpu.roll(t1r, 4, 0))
                s1i = jnp.where(first, t1i, pltpu.roll(t1i, 4, 0))
                pr, pi = _cmul(ar, ai, s1r, s1i)
                t2r, t2i = pr + tr, pi + ti
                sre_ref[ii, :, sl] = jnp.where(first, t1r, t2r)
                sim_ref[ii, :, sl] = jnp.where(first, t1i, t2i)
                return jnp.where(first, pltpu.roll(t2r, 4, 0), t2r), jnp.where(first, pltpu.roll(t2i, 4, 0), t2i)

            sr, si = lax.fori_loop(0, n8, step, (st_re[:, sl], st_im[:, sl]))
            st_re[:, sl] = sr
            st_im[:, sl] = si
        for j in range(kb):
            sr = sre_ref[:, :, j * S5_KST:(j + 1) * S5_KST].reshape(tcr, S5_KST)
            si = sim_ref[:, :, j * S5_KST:(j + 1) * S5_KST].reshape(tcr, S5_KST)
            y_ref[:, j * S5_KIN:(j + 1) * S5_KIN] = _dot(sr, cre_ref[j]) - _dot(si, cim_ref[j])

    cidx = functools.partial(_chunk_order, ncc=ncc, nch=nch, rev=rev)
    full = lambda a: pl.BlockSpec(a.shape, lambda k: (0,) * a.ndim)
    st = pl.BlockSpec((n8, 8, ns), lambda k: (cidx(k), 0, 0))
    en = pl.BlockSpec((1, 8, ns), lambda k: (cidx(k), 0, 0))
    return pl.pallas_call(
        body, name=name, grid=(nch,),
        in_specs=[pl.BlockSpec((tcr, d), lambda k: (cidx(k), 0)), full(a_re), full(a_im), full(bre), full(bim), full(cre), full(cim)],
        out_specs=(st, st, en, en, pl.BlockSpec((tcr, d), lambda k: (cidx(k), 0))),
        out_shape=(jax.ShapeDtypeStruct((r // 8, 8, ns), F32),) * 2 + (jax.ShapeDtypeStruct((nch, 8, ns), F32),) * 2
        + (jax.ShapeDtypeStruct((r, d), F32),),
        scratch_shapes=[pltpu.VMEM((8, ns), F32), pltpu.VMEM((8, ns), F32)], compiler_params=_cp(("arbitrary",)))(
            u, a_re, a_im, bre, bim, cre, cim)


def s5_scan_bwd(cfg, dyb, sre, sim, ere, eim, a_re, a_im, bre, bim, cre, cim, du_in, *, rev, name):
    r, d = dyb.shape
    ns = a_re.shape[1]
    kb = d // S5_KIN
    tcr = 128
    n8 = tcr // 8
    nch, ncc = r // tcr, cfg["rc"] // tcr
    lw = min(S5_LANES, ns)

    def body(dy_ref, sre_ref, sim_ref, ere_ref, eim_ref, ar_ref, ai_ref, bre_ref, bim_ref, cre_ref, cim_ref, duin_ref,
             du_ref, gre_ref, gim_ref, dar_ref, dai_ref, g_re, g_im, gc_re, gc_im):
        k = pl.program_id(0)

        @pl.when(k == 0)
        def _():
            gc_re[...] = jnp.zeros_like(gc_re)
            gc_im[...] = jnp.zeros_like(gc_im)
            dar_ref[...] = jnp.zeros_like(dar_ref)
            dai_ref[...] = jnp.zeros_like(dai_ref)

        dy = dy_ref[...]
        for j in range(kb):
            dyj = dy[:, j * S5_KIN:(j + 1) * S5_KIN]
            g_re[:, :, j * S5_KST:(j + 1) * S5_KST] = _dot(dyj, cre_ref[j], 1, 1).reshape(n8, 8, S5_KST)
            g_im[:, :, j * S5_KST:(j + 1) * S5_KST] = -_dot(dyj, cim_ref[j], 1, 1).reshape(n8, 8, S5_KST)
        first = lax.broadcasted_iota(jnp.int32, (8, lw), 0) < 4
        if rev:
            first = jnp.logical_not(first)
        for c in range(ns // lw):
            sl = slice(c * lw, (c + 1) * lw)
            ar = jnp.broadcast_to(ar_ref[:, sl], (8, lw))
            nai = -jnp.broadcast_to(ai_ref[:, sl], (8, lw))

            def step(i, carry, sl=sl, ar=ar, nai=nai):
                gr, gi, accr, acci = carry
                ii = i if rev else n8 - 1 - i
                tr, ti = g_re[ii, :, sl], g_im[ii, :, sl]
                pr, pi = _cmul(ar, nai, gr, gi)
                t2r, t2i = pr + tr, pi + ti
                g2r = jnp.where(first, pltpu.roll(t2r, 4, 0), t2r)
                g2i = jnp.where(first, pltpu.roll(t2i, 4, 0), t2i)
                pr, pi = _cmul(ar, nai, g2r, g2i)
                t1r, t1i = pr + tr, pi + ti
                outr, outi = jnp.where(first, t1r, t2r), jnp.where(first, t1i, t2i)
                g_re[ii, :, sl] = outr
                g_im[ii, :, sl] = outi
                pv = jnp.clip(ii + 1 if rev else ii - 1, 0, n8 - 1)
                at_entry = (ii == n8 - 1) if rev else (ii == 0)
                pvr = jnp.where(at_entry, ere_ref[0, :, sl], sre_ref[pv, :, sl])
                pvi = jnp.where(at_entry, eim_ref[0, :, sl], sim_ref[pv, :, sl])
                spr = pltpu.roll(jnp.where(first, sre_ref[ii, :, sl], pvr), 4, 0)
                spi = pltpu.roll(jnp.where(first, sim_ref[ii, :, sl], pvi), 4, 0)
                accr = accr + outr * spr + outi * spi
                acci = acci + outi * spr - outr * spi
                return jnp.where(first, t1r, pltpu.roll(t1r, 4, 0)), jnp.where(first, t1i, pltpu.roll(t1i, 4, 0)), accr, acci

            gr, gi, accr, acci = lax.fori_loop(0, n8, step, (gc_re[:, sl], gc_im[:, sl], dar_ref[:, sl], dai_ref[:, sl]))
            gc_re[:, sl] = gr
            gc_im[:, sl] = gi
            dar_ref[:, sl] = accr
            dai_ref[:, sl] = acci
        for j in range(kb):
            gr = g_re[:, :, j * S5_KST:(j + 1) * S5_KST].reshape(tcr, S5_KST)
            gi = g_im[:, :, j * S5_KST:(j + 1) * S5_KST].reshape(tcr, S5_KST)
            gre_ref[:, j * S5_KST:(j + 1) * S5_KST] = gr.astype(MXU)
            gim_ref[:, j * S5_KST:(j + 1) * S5_KST] = gi.astype(MXU)
            du_ref[:, j * S5_KIN:(j + 1) * S5_KIN] = (duin_ref[:, j * S5_KIN:(j + 1) * S5_KIN]
                                                     + _dot(gr, bre_ref[j], 1, 1) + _dot(gi, bim_ref[j], 1, 1))

    def cidx(k):
        return _chunk_order(nch - 1 - k, ncc, nch, rev)

    full = lambda a: pl.BlockSpec(a.shape, lambda k: (0,) * a.ndim)
    st = pl.BlockSpec((n8, 8, ns), lambda k: (cidx(k), 0, 0))
    en = pl.BlockSpec((1, 8, ns), lambda k: (cidx(k), 0, 0))
    rowd = pl.BlockSpec((tcr, d), lambda k: (cidx(k), 0))
    rown = pl.BlockSpec((tcr, ns), lambda k: (cidx(k), 0))
    acc = pl.BlockSpec((8, ns), lambda k: (0, 0))
    return pl.pallas_call(
        body, name=name, grid=(nch,),
        in_specs=[rowd, st, st, en, en, full(a_re), full(a_im), full(bre), full(bim), full(cre), full(cim), rowd],
        out_specs=(rowd, rown, rown, acc, acc),
        out_shape=(jax.ShapeDtypeStruct((r, d), F32), jax.ShapeDtypeStruct((r, ns), MXU), jax.ShapeDtypeStruct((r, ns), MXU),
                   jax.ShapeDtypeStruct((8, ns), F32), jax.ShapeDtypeStruct((8, ns), F32)),
        scratch_shapes=[pltpu.VMEM((n8, 8, ns), F32), pltpu.VMEM((n8, 8, ns), F32), pltpu.VMEM((8, ns), F32), pltpu.VMEM((8, ns), F32)],
        compiler_params=_cp(("arbitrary",)))(dyb, sre, sim, ere, eim, a_re, a_im, bre, bim, cre, cim, du_in)


def rowmap(fn, rows_in, vecs_in, outs, accs=(), *, name):
    r = rows_in[0].shape[0]
    tm = _row_tile(r, max(a.shape[1] for a in rows_in))
    nr, nv, no = len(rows_in), len(vecs_in), len(outs)

    def body(*refs):
        ins = [x[...] for x in refs[:nr + nv]]
        res = fn(*ins)
        if not isinstance(res, (tuple, list)):
            res = (res,)
        out_refs = refs[nr + nv:]
        for o_ref, v in zip(out_refs[:no], res[:no]):
            o_ref[...] = v.astype(o_ref.dtype)
        if accs:
            @pl.when(pl.program_id(0) == 0)
            def _():
                for a_ref in out_refs[no:]:
                    a_ref[...] = jnp.zeros_like(a_ref)
            for a_ref, v in zip(out_refs[no:], res[no:]):
                a_ref[...] += v

    in_specs = [pl.BlockSpec((tm, a.shape[1]), lambda i: (i, 0)) for a in rows_in]
    in_specs += [pl.BlockSpec(v.shape, lambda i, n=v.ndim: (0,) * n) for v in vecs_in]
    out_specs = [pl.BlockSpec((tm, w), lambda i: (i, 0)) for w, _ in outs] + [pl.BlockSpec(s, lambda i, n=len(s): (0,) * n) for s in accs]
    out_shape = [jax.ShapeDtypeStruct((r, w), dt) for w, dt in outs] + [jax.ShapeDtypeStruct(s, F32) for s in accs]
    res = pl.pallas_call(body, name=name, grid=(r // tm,), in_specs=in_specs, out_specs=tuple(out_specs), out_shape=tuple(out_shape),
                         compiler_params=_cp(("arbitrary",) if accs else ("parallel",)))(*rows_in, *vecs_in)
    return res


def _gelu(x):
    return jax.nn.gelu(x, approximate=True)


def _hg_lower_bound(e0, e1):
    m = jnp.maximum(e0, e1)
    a, b = jnp.exp(e0 - m), jnp.exp(e1 - m)
    return b / (a + b)


def _hg_gates(x, lb):
    logf = jnp.log(lb + (1.0 - lb) * jax.nn.sigmoid(x))
    return logf, (1.0 - lb) * jax.nn.sigmoid(-x)


def _hg_masks(rev):
    n = CHUNK_ROWS
    rr = lax.broadcasted_iota(jnp.int32, (n, n), 0)
    ss = lax.broadcasted_iota(jnp.int32, (n, n), 1)
    same = (rr % NB) == (ss % NB)
    causal = same & ((ss >= rr) if rev else (ss <= rr))
    anti = same & ((ss <= rr) if rev else (ss >= rr))
    end0 = 0 if rev else n - NB
    pick_end = ss == (end0 + rr % NB)
    return same, causal, anti, pick_end, end0


def _hg_expand(x):
    ex = lax.broadcasted_iota(jnp.int32, x.shape, 0) % NB
    return jnp.concatenate([jnp.where(ex == b, x, 0.0) for b in range(NB)], axis=1)


def _hg_fold(xe):
    kk = xe.shape[1] // NB
    ex = lax.broadcasted_iota(jnp.int32, (xe.shape[0], kk), 0) % NB
    out = jnp.zeros((xe.shape[0], kk), F32)
    for b in range(NB):
        out = out + jnp.where(ex == b, xe[:, b * kk:(b + 1) * kk], 0.0)
    return out


def _hg_chunk(q, v, x, lb, stk, rev):
    same, causal, anti, pick_end, end0 = _hg_masks(rev)
    logf, kk = _hg_gates(x, lb)
    b = _dot3(causal.astype(MXU), logf)
    bend_t = _dot3(pick_end.astype(MXU), b)
    bend_flat = jnp.concatenate([b[end0 + i:end0 + i + 1] for i in range(NB)], axis=1)
    eb = jnp.exp(b)
    enb = jnp.exp(-b)
    ee = jnp.exp(bend_t - b)
    qd, kd, ke = q * eb, kk * enb, kk * ee
    att = jnp.where(causal, _dot(qd, kd, 1, 1), 0.0)
    decay = jnp.exp(bend_flat)
    return dict(same=same, causal=causal, anti=anti, logf=logf, kk=kk, b=b, eb=eb, enb=enb, ee=ee, qd=qd, kd=kd, ke=ke, att=att,
                decay=decay, qde=_hg_expand(qd), kee=_hg_expand(ke))


def _hg_chunk_order(cfg, r):
    nch, ncc = r // CHUNK_ROWS, cfg["rc"] // CHUNK_ROWS
    return nch, ncc


def hg_scan_fwd(cfg, z, lb, *, d_dir, name):
    r = z.shape[0]
    d = z.shape[1] // N_PROJ
    nh = d // HEAD
    rev = d_dir == 1
    nch, ncc = _hg_chunk_order(cfg, r)
    n = CHUNK_ROWS

    def body(q_ref, v_ref, x_ref, lb_ref, o_ref, sin_ref, stk):
        k = pl.program_id(1)

        @pl.when(k == 0)
        def _():
            stk[...] = jnp.zeros_like(stk)

        s0 = stk[...]
        sin_ref[0, 0] = s0
        v = v_ref[...]
        c = _hg_chunk(q_ref[...], v, x_ref[...], lb_ref[...], s0, rev)
        o_ref[...] = _dot(c["att"], v) + _dot(c["qde"], s0, 1, 1)
        stk[...] = s0 * c["decay"] + _dot(v, c["kee"], 0, 0)

    def cidx(k):
        return _chunk_order(k, ncc, nch, rev)

    blk = lambda p: pl.BlockSpec((n, HEAD), lambda h, k: (cidx(k), p * nh + h))
    return pl.pallas_call(
        body, name=name, grid=(nh, nch),
        in_specs=[blk(0), blk(1), blk(2 + d_dir), pl.BlockSpec((1, HEAD), lambda h, k: (0, h))],
        out_specs=(pl.BlockSpec((n, HEAD), lambda h, k: (cidx(k), h)), pl.BlockSpec((1, 1, HEAD, NB * HEAD), lambda h, k: (cidx(k), h, 0, 0))),
        out_shape=(jax.ShapeDtypeStruct((r, d), F32), jax.ShapeDtypeStruct((nch, nh, HEAD, NB * HEAD), F32)),
        scratch_shapes=[pltpu.VMEM((HEAD, NB * HEAD), F32)], compiler_params=_cp(("parallel", "arbitrary")))(z, z, z, lb)


def hg_scan_bwd(cfg, do, z, lb, sin, dq_in, dv_in, *, d_dir, name):
    r = z.shape[0]
    d = z.shape[1] // N_PROJ
    nh = d // HEAD
    rev = d_dir == 1
    nch, ncc = _hg_chunk_order(cfg, r)
    n = CHUNK_ROWS
    has_in = dq_in is not None

    def body(*refs):
        if has_in:
            do_ref, q_ref, v_ref, x_ref, lb_ref, sin_ref, dqi_ref, dvi_ref, dq_ref, dv_ref, dx_ref, dlb_ref, dstk = refs
        else:
            do_ref, q_ref, v_ref, x_ref, lb_ref, sin_ref, dq_ref, dv_ref, dx_ref, dlb_ref, dstk = refs
        k = pl.program_id(1)

        @pl.when(k == 0)
        def _():
            dstk[...] = jnp.zeros_like(dstk)
            dlb_ref[...] = jnp.zeros_like(dlb_ref)

        do, q, v, x, lb, s0, ds1 = do_ref[...], q_ref[...], v_ref[...], x_ref[...], lb_ref[...], sin_ref[0, 0], dstk[...]
        c = _hg_chunk(q, v, x, lb, s0, rev)
        datt = jnp.where(c["causal"], _dot(do, v, 1, 1), 0.0)
        dv = _dot(c["att"], do, 0, 0) + _dot(c["kee"], ds1, 1, 1)
        dqd = _dot(datt, c["kd"]) + _hg_fold(_dot(do, s0))
        dkd = _dot(datt, c["qd"], 0, 0)
        dke = _hg_fold(_dot(v, ds1))
        dbend_flat = jnp.sum(ds1 * s0, axis=0, keepdims=True) * c["decay"]
        dstk[...] = _dot(do, c["qde"], 0, 0) + ds1 * c["decay"]
        dq = dqd * c["eb"]
        dk = dkd * c["enb"] + dke * c["ee"]
        db = dqd * c["qd"] - dkd * c["kd"] - dke * c["ke"]
        ex = lax.broadcasted_iota(jnp.int32, (n, HEAD), 0) % NB
        dbend_rows = jnp.zeros((n, HEAD), F32)
        for b in range(NB):
            dbend_rows = dbend_rows + jnp.where(ex == b, dbend_flat[:, b * HEAD:(b + 1) * HEAD], 0.0)
        dlogf = _dot3(c["anti"].astype(MXU), db) + _dot3(c["same"].astype(MXU), dke * c["ke"]) + dbend_rows
        _, vjp = jax.vjp(_hg_gates, x, lb)
        dx, dlb = vjp((dlogf, dk))
        if has_in:
            dq = dq + dqi_ref[...]
            dv = dv + dvi_ref[...]
        dq_ref[...] = dq
        dv_ref[...] = dv
        dx_ref[...] = dx
        dlb_ref[...] += dlb

    def cidx(k):
        return _chunk_order(nch - 1 - k, ncc, nch, rev)

    blk = lambda p: pl.BlockSpec((n, HEAD), lambda h, k: (cidx(k), p * nh + h))
    oblk = pl.BlockSpec((n, HEAD), lambda h, k: (cidx(k), h))
    vec = pl.BlockSpec((1, HEAD), lambda h, k: (0, h))
    in_specs = [oblk, blk(0), blk(1), blk(2 + d_dir), vec, pl.BlockSpec((1, 1, HEAD, NB * HEAD), lambda h, k: (cidx(k), h, 0, 0))]
    args = [do, z, z, z, lb, sin]
    if has_in:
        in_specs += [oblk, oblk]
        args += [dq_in, dv_in]
    rd = jax.ShapeDtypeStruct((r, d), F32)
    return pl.pallas_call(
        body, name=name, grid=(nh, nch), in_specs=in_specs, out_specs=(oblk, oblk, oblk, vec),
        out_shape=(rd, rd, rd, jax.ShapeDtypeStruct((1, d), F32)),
        scratch_shapes=[pltpu.VMEM((HEAD, NB * HEAD), F32)], compiler_params=_cp(("parallel", "arbitrary")))(*args)


def _hg_read(o, g, gw):
    on = o * lax.rsqrt(jnp.mean(o * o, axis=-1, keepdims=True) + NORM_EPS) * gw
    return on * jax.nn.sigmoid(g)


def hg_read_fwd(of, ob, z, gw, *, name):
    r, d = of.shape
    nh = d // HEAD
    tm = _row_tile(r)

    def body(of_ref, ob_ref, g_ref, gw_ref, o_ref):
        o_ref[...] = _hg_read(of_ref[...] + ob_ref[...], g_ref[...], gw_ref[...]).astype(MXU)

    blk = pl.BlockSpec((tm, HEAD), lambda i, h: (i, h))
    return pl.pallas_call(
        body, name=name, grid=(r // tm, nh),
        in_specs=[blk, blk, pl.BlockSpec((tm, HEAD), lambda i, h: (i, (N_PROJ - 1) * nh + h)), pl.BlockSpec((1, HEAD), lambda i, h: (0, 0))],
        out_specs=blk, out_shape=jax.ShapeDtypeStruct((r, d), MXU), compiler_params=_cp(("parallel", "parallel")))(of, ob, z, gw)


def hg_read_bwd(don, of, ob, z, gw, *, name):
    r, d = of.shape
    nh = d // HEAD
    tm = _row_tile(r)

    def body(don_ref, of_ref, ob_ref, g_ref, gw_ref, do_ref, dg_ref, dgw_ref):
        @pl.when((pl.program_id(0) == 0) & (pl.program_id(1) == 0))
        def _():
            dgw_ref[...] = jnp.zeros_like(dgw_ref)

        _, vjp = jax.vjp(_hg_read, of_ref[...] + ob_ref[...], g_ref[...], gw_ref[...])
        do_ref[...], dg_ref[...], dgw = vjp(don_ref[...])
        dgw_ref[...] += dgw

    blk = pl.BlockSpec((tm, HEAD), lambda i, h: (i, h))
    vec = pl.BlockSpec((1, HEAD), lambda i, h: (0, 0))
    rd = jax.ShapeDtypeStruct((r, d), F32)
    return pl.pallas_call(
        body, name=name, grid=(r // tm, nh),
        in_specs=[blk, blk, blk, pl.BlockSpec((tm, HEAD), lambda i, h: (i, (N_PROJ - 1) * nh + h)), vec],
        out_specs=(blk, blk, vec), out_shape=(rd, rd, jax.ShapeDtypeStruct((1, HEAD), F32)),
        compiler_params=_cp(("arbitrary", "arbitrary")))(don, of, ob, z, gw)


FFN_COLS = 256


def _seg_masks(cfg, tr, i):
    t = lax.broadcasted_iota(jnp.int32, (tr, FFN_COLS), 0) // NB
    ctx_steps = cfg["rc"] // NB
    pos = jnp.where(i == 0, t % ctx_steps, t % GRID_W)
    last = jnp.where(i == 0, ctx_steps - 1, GRID_W - 1)
    return pos == 0, pos == last


def _prev(x, start):
    return jnp.where(start, 0.0, pltpu.roll(x, NB, 0))


def _next(x, end):
    return jnp.where(end, 0.0, pltpu.roll(x, x.shape[0] - NB, 0))


def _conv3(u, w, b, start, end):
    return ((b + _prev(u, start) * w[0:1]) + u * w[1:2]) + _next(u, end) * w[2:3]


def ffn_mid_fwd(cfg, u, cw, cb, *, name):
    r, f2 = u.shape
    f = f2 // 2
    tr = cfg["rc"]
    nf = f // FFN_COLS

    def body(ua_ref, ug_ref, wa_ref, wg_ref, ba_ref, bg_ref, o_ref):
        start, end = _seg_masks(cfg, tr, pl.program_id(0))
        a = _conv3(ua_ref[...], wa_ref[...], ba_ref[...], start, end)
        g = _conv3(ug_ref[...], wg_ref[...], bg_ref[...], start, end)
        o_ref[...] = (_silu(a) * g).astype(MXU)

    ca = lambda rows: pl.BlockSpec((rows, FFN_COLS), lambda i, j: (i if rows == tr else 0, j))
    cg = lambda rows: pl.BlockSpec((rows, FFN_COLS), lambda i, j: (i if rows == tr else 0, j + nf))
    return pl.pallas_call(
        body, name=name, grid=(r // tr, nf), in_specs=[ca(tr), cg(tr), ca(3), cg(3), ca(1), cg(1)], out_specs=ca(tr),
        out_shape=jax.ShapeDtypeStruct((r, f), MXU), compiler_params=_cp(("parallel", "parallel")))(u, u, cw, cw, cb, cb)


def ffn_mid_bwd(cfg, dact, u, cw, cb, *, name):
    r, f2 = u.shape
    f = f2 // 2
    tr = cfg["rc"]
    nf = f // FFN_COLS

    def body(da_ref, us_ref, up_ref, ws_ref, wp_ref, bs_ref, bp_ref, du_ref, dcw_ref, dcb_ref):
        i = pl.program_id(1)
        is_a = pl.program_id(0) < nf
        start, end = _seg_masks(cfg, tr, i)
        us, ws = us_ref[...], ws_ref[...]
        cs = _conv3(us, ws, bs_ref[...], start, end)
        cp = _conv3(up_ref[...], wp_ref[...], bp_ref[...], start, end)
        dact_v = da_ref[...]
        sg = jax.nn.sigmoid(cs)
        d_if_a = dact_v * cp * (sg * (1.0 + cs * (1.0 - sg)))
        d_if_g = dact_v * _silu(cp)
        dc = jnp.where(is_a, d_if_a, d_if_g)
        du_ref[...] = (ws[1:2] * dc + ws[0:1] * _next(dc, end) + ws[2:3] * _prev(dc, start)).astype(MXU)

        @pl.when(i == 0)
        def _():
            dcw_ref[...] = jnp.zeros_like(dcw_ref)
            dcb_ref[...] = jnp.zeros_like(dcb_ref)

        dcw_ref[...] += jnp.concatenate([jnp.sum(dc * _prev(us, start), axis=0, keepdims=True), jnp.sum(dc * us, axis=0, keepdims=True),
                                         jnp.sum(dc * _next(us, end), axis=0, keepdims=True)], axis=0)
        dcb_ref[...] += jnp.sum(dc, axis=0, keepdims=True)

    cs_ = lambda rows: pl.BlockSpec((rows, FFN_COLS), lambda j, i: (i if rows == tr else 0, j))
    cp_ = lambda rows: pl.BlockSpec((rows, FFN_COLS), lambda j, i: (i if rows == tr else 0, (j + nf) % (2 * nf)))
    return pl.pallas_call(
        body, name=name, grid=(2 * nf, r // tr),
        in_specs=[pl.BlockSpec((tr, FFN_COLS), lambda j, i: (i, j % nf)), cs_(tr), cp_(tr), cs_(3), cp_(3), cs_(1), cp_(1)],
        out_specs=(cs_(tr), cs_(3), cs_(1)),
        out_shape=(jax.ShapeDtypeStruct((r, f2), MXU), jax.ShapeDtypeStruct((3, f2), F32), jax.ShapeDtypeStruct((1, f2), F32)),
        compiler_params=_cp(("parallel", "arbitrary")))(dact, u, u, cw, cw, cb, cb)


def hg_lb_fwd(e0, e1, *, name):
    def body(a, b, o):
        o[...] = _hg_lower_bound(a[...], b[...])

    return pl.pallas_call(body, name=name, out_shape=jax.ShapeDtypeStruct(e0.shape, F32))(e0, e1)


def hg_lb_bwd(e0, e1, dlb, *, name):
    def body(a, b, g, oa, ob):
        _, vjp = jax.vjp(_hg_lower_bound, a[...], b[...])
        oa[...], ob[...] = vjp(g[...])

    s = jax.ShapeDtypeStruct(e0.shape, F32)
    return pl.pallas_call(body, name=name, out_shape=(s, s))(e0, e1, dlb)


def _adamw(w, g, m, v):
    m = ADAM_B1 * m + (1.0 - ADAM_B1) * g
    v = ADAM_B2 * v + (1.0 - ADAM_B2) * jnp.square(g)
    m_hat = m / (1.0 - ADAM_B1 ** ADAM_STEP)
    v_hat = v / (1.0 - ADAM_B2 ** ADAM_STEP)
    delta = -ADAM_LR * (m_hat / (jnp.sqrt(v_hat) + ADAM_EPS) + ADAM_WD * w)
    return delta, m, v


def _as2d(a):
    if a.ndim >= 2 and a.shape[-1] % 128 == 0:
        return a.reshape(-1, a.shape[-1])
    return a.reshape(-1, 128) if a.size % 128 == 0 else a.reshape(1, -1)


def adamw(w, g, m, v, *, name):
    w2 = _as2d(w)
    outs = rowmap(_adamw, [w2, _as2d(g), _as2d(m), _as2d(v)], [], [(w2.shape[1], F32)] * 3, name=name)
    return tuple(o.reshape(w.shape) for o in outs)


HBM_SPEC = pl.BlockSpec(memory_space=pltpu.HBM)


def _place():
    mx, my, mc = lax.axis_index("x"), lax.axis_index("y"), lax.axis_index("c")
    others = [(1 - mx, my), (mx, 1 - my), (1 - mx, 1 - my)]
    return mx, my, mc, others


def chip_allgather(x, *, name):
    def body(x_ref, o_ref, send_sems, recv_sems, local_sem):
        mx, my, mc, others = _place()
        me = 2 * mx + my
        mine = pltpu.make_async_copy(x_ref, o_ref.at[me], local_sem)
        mine.start()
        sends = [pltpu.make_async_remote_copy(src_ref=x_ref, dst_ref=o_ref.at[me], send_sem=send_sems.at[j], recv_sem=recv_sems.at[j],
                                              device_id=(px, py, mc), device_id_type=MESH) for j, (px, py) in enumerate(others)]
        for cp in sends:
            cp.start()
        for j, (px, py) in enumerate(others):
            pltpu.make_async_remote_copy(src_ref=x_ref, dst_ref=o_ref.at[2 * px + py], send_sem=send_sems.at[j], recv_sem=recv_sems.at[j],
                                         device_id=(px, py, mc), device_id_type=MESH).wait_recv()
        for cp in sends:
            cp.wait_send()
        mine.wait()

    return pl.pallas_call(
        body, name=name, out_shape=jax.ShapeDtypeStruct((4,) + x.shape, x.dtype), in_specs=[HBM_SPEC], out_specs=HBM_SPEC,
        scratch_shapes=[pltpu.SemaphoreType.DMA((3,)), pltpu.SemaphoreType.DMA((3,)), pltpu.SemaphoreType.DMA])(x)


def _win(ref, axis, start, size):
    idx = [slice(None)] * len(ref.shape)
    idx[axis] = pl.ds(start, size)
    return ref.at[tuple(idx)]


def _half_axis(shape, ax):
    if shape[0] == 2:
        return 0
    return 2 if ax == 1 else 1


def _cut(shape, axis, parts):
    return shape[:axis] + (shape[axis] // parts,) + shape[axis + 1:]


def _hbm_call(body, arrays, out_shapes, sems, name):
    n_in = len(arrays)
    return pl.pallas_call(body, name=name, out_shape=tuple(out_shapes), in_specs=[HBM_SPEC] * n_in, out_specs=tuple([HBM_SPEC] * len(out_shapes)),
                          scratch_shapes=sems)(*arrays)


def gather_shards(shards, axes, *, name):
    n = len(shards)

    def body(*refs):
        ins, outs = refs[:n], refs[n:2 * n]
        send_sems, recv_sems, local_sems = refs[2 * n:]
        mx, my, mc, others = _place()
        me = 2 * mx + my
        waits = []
        for i in range(n):
            sz = shards[i].shape[axes[i]]
            mine = _win(outs[i], axes[i], me * sz, sz)
            cp = pltpu.make_async_copy(ins[i], mine, local_sems.at[i])
            cp.start()
            waits.append(cp.wait)
            for j, (px, py) in enumerate(others):
                rc = pltpu.make_async_remote_copy(src_ref=ins[i], dst_ref=mine, send_sem=send_sems.at[i, j], recv_sem=recv_sems.at[i, j],
                                                  device_id=(px, py, mc), device_id_type=MESH)
                rc.start()
                waits.append(rc.wait_send)
        for i in range(n):
            sz = shards[i].shape[axes[i]]
            for j, (px, py) in enumerate(others):
                pltpu.make_async_remote_copy(src_ref=ins[i], dst_ref=_win(outs[i], axes[i], (2 * px + py) * sz, sz), send_sem=send_sems.at[i, j],
                                             recv_sem=recv_sems.at[i, j], device_id=(px, py, mc), device_id_type=MESH).wait_recv()
        for w_ in waits:
            w_()

    outs = [jax.ShapeDtypeStruct(s.shape[:ax] + (4 * s.shape[ax],) + s.shape[ax + 1:], s.dtype) for s, ax in zip(shards, axes)]
    return _hbm_call(body, shards, outs, [pltpu.SemaphoreType.DMA((n, 3)), pltpu.SemaphoreType.DMA((n, 3)), pltpu.SemaphoreType.DMA((n,))], name)


def pair_swap_halves(arrays, haxes, *, name):
    n = len(arrays)

    def body(*refs):
        ins, outs = refs[:n], refs[n:2 * n]
        send_sems, recv_sems = refs[2 * n:]
        mx, my, mc, _ = _place()
        cps = []
        for i in range(n):
            hs = arrays[i].shape[haxes[i]] // 2
            cp = pltpu.make_async_remote_copy(src_ref=_win(ins[i], haxes[i], (1 - mc) * hs, hs), dst_ref=outs[i], send_sem=send_sems.at[i],
                                              recv_sem=recv_sems.at[i], device_id=(mx, my, 1 - mc), device_id_type=MESH)
            cp.start()
            cps.append(cp)
        for cp in cps:
            cp.wait()

    outs = [jax.ShapeDtypeStruct(_cut(a_.shape, h_, 2), a_.dtype) for a_, h_ in zip(arrays, haxes)]
    return _hbm_call(body, arrays, outs, [pltpu.SemaphoreType.DMA((n,)), pltpu.SemaphoreType.DMA((n,))], name)


def add_own_half(g, t, hax, core, *, out_dtype, name):
    l, r, c = t.shape
    tr = _row_tile(r, c)
    per_half = (l, r // tr, 1)[hax]

    def imap(li, ri, cref):
        idx = [li, ri, 0]
        idx[hax] = idx[hax] + cref[0] * per_half
        return tuple(idx)

    def body(c_ref, g_ref, t_ref, o_ref):
        o_ref[...] = (g_ref[...] + t_ref[...]).astype(out_dtype)

    return pl.pallas_call(
        body, name=name, out_shape=jax.ShapeDtypeStruct(t.shape, out_dtype),
        grid_spec=pltpu.PrefetchScalarGridSpec(
            num_scalar_prefetch=1, grid=(l, r // tr),
            in_specs=[pl.BlockSpec((1, tr, c), imap), pl.BlockSpec((1, tr, c), lambda li, ri, cref: (li, ri, 0))],
            out_specs=pl.BlockSpec((1, tr, c), lambda li, ri, cref: (li, ri, 0))),
        compiler_params=_cp(("parallel", "parallel")))(core, g, t)


def exchange_blocks(arrays, axes, *, name):
    n = len(arrays)

    def body(*refs):
        ins, outs = refs[:n], refs[n:2 * n]
        send_sems, recv_sems, local_sems = refs[2 * n:]
        mx, my, mc, others = _place()
        me = 2 * mx + my
        waits = []
        for i in range(n):
            sz = arrays[i].shape[axes[i]] // 4
            cp = pltpu.make_async_copy(_win(ins[i], axes[i], me * sz, sz), outs[i].at[me], local_sems.at[i])
            cp.start()
            waits.append(cp.wait)
            for j, (px, py) in enumerate(others):
                rc = pltpu.make_async_remote_copy(src_ref=_win(ins[i], axes[i], (2 * px + py) * sz, sz), dst_ref=outs[i].at[me],
                                                  send_sem=send_sems.at[i, j], recv_sem=recv_sems.at[i, j], device_id=(px, py, mc),
                                                  device_id_type=MESH)
                rc.start()
                waits.append(rc.wait_send)
        for i in range(n):
            sz = arrays[i].shape[axes[i]] // 4
            for j, (px, py) in enumerate(others):
                pltpu.make_async_remote_copy(src_ref=_win(ins[i], axes[i], me * sz, sz), dst_ref=outs[i].at[2 * px + py],
                                             send_sem=send_sems.at[i, j], recv_sem=recv_sems.at[i, j], device_id=(px, py, mc),
                                             device_id_type=MESH).wait_recv()
        for w_ in waits:
            w_()

    outs = [jax.ShapeDtypeStruct((4,) + _cut(a_.shape, ax, 4), a_.dtype) for a_, ax in zip(arrays, axes)]
    return _hbm_call(body, arrays, outs, [pltpu.SemaphoreType.DMA((n, 3)), pltpu.SemaphoreType.DMA((n, 3)), pltpu.SemaphoreType.DMA((n,))], name)


def sum_blocks(e, *, name):
    _, l, r, c = e.shape
    tr = _row_tile(r, c)

    def body(e_ref, o_ref):
        v = e_ref[...].astype(F32)
        o_ref[...] = ((v[0] + v[1]) + v[2]) + v[3]

    return pl.pallas_call(
        body, name=name, grid=(l, r // tr), in_specs=[pl.BlockSpec((4, 1, tr, c), lambda li, ri: (0, li, ri, 0))],
        out_specs=pl.BlockSpec((1, tr, c), lambda li, ri: (li, ri, 0)), out_shape=jax.ShapeDtypeStruct((l, r, c), F32),
        compiler_params=_cp(("parallel", "parallel")))(e)


def pair_gather_halves(arrays, haxes, *, name):
    n = len(arrays)

    def body(*refs):
        ins, outs = refs[:n], refs[n:2 * n]
        send_sems, recv_sems, local_sems = refs[2 * n:]
        mx, my, mc, _ = _place()
        waits = []
        for i in range(n):
            hs = arrays[i].shape[haxes[i]]
            mine = _win(outs[i], haxes[i], mc * hs, hs)
            cp = pltpu.make_async_copy(ins[i], mine, local_sems.at[i])
            cp.start()
            rc = pltpu.make_async_remote_copy(src_ref=ins[i], dst_ref=mine, send_sem=send_sems.at[i], recv_sem=recv_sems.at[i],
                                              device_id=(mx, my, 1 - mc), device_id_type=MESH)
            rc.start()
            waits += [cp.wait, rc.wait_send]
        for i in range(n):
            hs = arrays[i].shape[haxes[i]]
            pltpu.make_async_remote_copy(src_ref=ins[i], dst_ref=_win(outs[i], haxes[i], (1 - mc) * hs, hs), send_sem=send_sems.at[i],
                                         recv_sem=recv_sems.at[i], device_id=(mx, my, 1 - mc), device_id_type=MESH).wait_recv()
        for w_ in waits:
            w_()

    outs = [jax.ShapeDtypeStruct(a_.shape[:h_] + (2 * a_.shape[h_],) + a_.shape[h_ + 1:], a_.dtype) for a_, h_ in zip(arrays, haxes)]
    return _hbm_call(body, arrays, outs, [pltpu.SemaphoreType.DMA((n,)), pltpu.SemaphoreType.DMA((n,)), pltpu.SemaphoreType.DMA((n,))], name)


WEIGHTS = ['c_ctx', 'w_mod', 'b_mod', 'norm1_w', 'norm2_w', 'final_norm_w', 's5_w_in', 's5_lam_re', 's5_lam_im', 's5_log_step', 's5_b_re', 's5_b_im', 's5_c_re', 's5_c_im', 's5_d', 's5_w_glu', 's5_w_out', 'hg_w_in', 'hg_lower_bounds', 'hg_gnorm_w', 'hg_w_out', 'ffn_w_up', 'ffn_conv_w', 'ffn_conv_b', 'ffn_w_down']
INPUTS = ['x', 'c', 'ctx', 'c_ctx', 'w_mod', 'b_mod', 'norm1_w', 'norm2_w', 'final_norm_w', 's5_w_in', 's5_lam_re', 's5_lam_im', 's5_log_step', 's5_b_re', 's5_b_im', 's5_c_re', 's5_c_im', 's5_d', 's5_w_glu', 's5_w_out', 'hg_w_in', 'hg_lower_bounds', 'hg_gnorm_w', 'hg_w_out', 'ffn_w_up', 'ffn_conv_w', 'ffn_conv_b', 'ffn_w_down', 'loss_target', 'm_c_ctx', 'm_w_mod', 'm_b_mod', 'm_norm1_w', 'm_norm2_w', 'm_final_norm_w', 'm_s5_w_in', 'm_s5_lam_re', 'm_s5_lam_im', 'm_s5_log_step', 'm_s5_b_re', 'm_s5_b_im', 'm_s5_c_re', 'm_s5_c_im', 'm_s5_d', 'm_s5_w_glu', 'm_s5_w_out', 'm_hg_w_in', 'm_hg_lower_bounds', 'm_hg_gnorm_w', 'm_hg_w_out', 'm_ffn_w_up', 'm_ffn_conv_w', 'm_ffn_conv_b', 'm_ffn_w_down', 'v_c_ctx', 'v_w_mod', 'v_b_mod', 'v_norm1_w', 'v_norm2_w', 'v_final_norm_w', 'v_s5_w_in', 'v_s5_lam_re', 'v_s5_lam_im', 'v_s5_log_step', 'v_s5_b_re', 'v_s5_b_im', 'v_s5_c_re', 'v_s5_c_im', 'v_s5_d', 'v_s5_w_glu', 'v_s5_w_out', 'v_hg_w_in', 'v_hg_lower_bounds', 'v_hg_gnorm_w', 'v_hg_w_out', 'v_ffn_w_up', 'v_ffn_conv_w', 'v_ffn_conv_b', 'v_ffn_w_down']
SHARD_AXIS = {"w_mod": 2, "s5_w_in": 1, "s5_w_glu": 1, "s5_w_out": 1, "hg_w_in": 2, "hg_lower_bounds": 2, "hg_w_out": 1,
              "ffn_w_up": 2, "ffn_conv_w": 2, "ffn_w_down": 1}
GATHER_F32 = ("hg_lower_bounds", "ffn_conv_w")
PACK_W = 1024
GRAD_WIRE = F32


def _gather_weights(a, names):
    shards = [a[n].astype(F32 if n in GATHER_F32 else MXU) for n in names]
    return dict(zip(names, gather_shards(shards, [SHARD_AXIS[n] for n in names], name="allgather_weights")))


def _reduce_grads(a, grads, core):
    sharded = [n for n in WEIGHTS if n in SHARD_AXIS]
    small = [n for n in WEIGHTS if n not in SHARD_AXIS]
    flat = jnp.concatenate([grads[n].reshape(-1) for n in small])
    pad = (-flat.shape[0]) % (64 * PACK_W)
    small_pack = jnp.pad(flat, (0, pad)).reshape(1, -1, PACK_W)
    arrays = [grads[n] for n in sharded] + [small_pack]
    axes = [SHARD_AXIS[n] for n in sharded] + [1]
    haxes = [_half_axis(g_.shape, ax) for g_, ax in zip(arrays, axes)]
    tags = sharded + ["small"]
    t = pair_swap_halves(arrays, haxes, name="grad_pair_swap")
    h = [add_own_half(g_, t_, hx, core, out_dtype=GRAD_WIRE, name="grad_pair_add_" + tg) for g_, t_, hx, tg in zip(arrays, t, haxes, tags)]
    e = exchange_blocks(h, axes, name="grad_chip_exchange")
    s = [sum_blocks(e_, name="grad_chip_sum_" + tg) for e_, tg in zip(e, tags)]
    red = pair_gather_halves(s, haxes, name="grad_pair_gather")
    out = dict(zip(sharded, red[:-1]))
    sm = chip_allgather(red[-1][0], name="allgather_small_grads").reshape(-1)
    off = 0
    for n in small:
        out[n] = sm[off:off + math.prod(a[n].shape)].reshape(a[n].shape)
        off += math.prod(a[n].shape)
    return out


def _blockdiag_b(bb, kb):
    gl = S5_KIN // S5_GROUP
    x = bb.reshape(kb, gl, S5_GROUP, S5_STATE)
    return (x[:, :, :, None, :] * jnp.eye(gl, dtype=bb.dtype)[None, :, None, :, None]).reshape(kb, S5_KIN, S5_KST)


def _blockdiag_c(cc, kb):
    gl = S5_KIN // S5_GROUP
    x = cc.reshape(kb, gl, S5_GROUP, S5_STATE).transpose(0, 1, 3, 2)
    return (x[:, :, :, None, :] * jnp.eye(gl, dtype=cc.dtype)[None, :, None, :, None]).reshape(kb, S5_KST, S5_KIN)


def _diag_b(m, kb):
    gl = S5_KIN // S5_GROUP
    x = m.reshape(kb, gl, S5_GROUP, gl, S5_STATE)
    return jnp.stack([x[:, i, :, i, :] for i in range(gl)], axis=1).reshape(kb * gl, S5_GROUP, S5_STATE)


def _diag_c(m, kb):
    gl = S5_KIN // S5_GROUP
    x = m.reshape(kb, gl, S5_STATE, gl, S5_GROUP)
    return jnp.stack([x[:, i, :, i, :] for i in range(gl)], axis=1).transpose(0, 1, 3, 2).reshape(kb * gl, S5_GROUP, S5_STATE)


def kernel(x, c, ctx, c_ctx, w_mod, b_mod, norm1_w, norm2_w, final_norm_w, s5_w_in, s5_lam_re, s5_lam_im, s5_log_step, s5_b_re, s5_b_im, s5_c_re, s5_c_im, s5_d, s5_w_glu, s5_w_out, hg_w_in, hg_lower_bounds, hg_gnorm_w, hg_w_out, ffn_w_up, ffn_conv_w, ffn_conv_b, ffn_w_down, loss_target, m_c_ctx, m_w_mod, m_b_mod, m_norm1_w, m_norm2_w, m_final_norm_w, m_s5_w_in, m_s5_lam_re, m_s5_lam_im, m_s5_log_step, m_s5_b_re, m_s5_b_im, m_s5_c_re, m_s5_c_im, m_s5_d, m_s5_w_glu, m_s5_w_out, m_hg_w_in, m_hg_lower_bounds, m_hg_gnorm_w, m_hg_w_out, m_ffn_w_up, m_ffn_conv_w, m_ffn_conv_b, m_ffn_w_down, v_c_ctx, v_w_mod, v_b_mod, v_norm1_w, v_norm2_w, v_final_norm_w, v_s5_w_in, v_s5_lam_re, v_s5_lam_im, v_s5_log_step, v_s5_b_re, v_s5_b_im, v_s5_c_re, v_s5_c_im, v_s5_d, v_s5_w_glu, v_s5_w_out, v_hg_w_in, v_hg_lower_bounds, v_hg_gnorm_w, v_hg_w_out, v_ffn_w_up, v_ffn_conv_w, v_ffn_conv_b, v_ffn_w_down):
    a = dict(zip(INPUTS, (x, c, ctx, c_ctx, w_mod, b_mod, norm1_w, norm2_w, final_norm_w, s5_w_in, s5_lam_re, s5_lam_im, s5_log_step, s5_b_re, s5_b_im, s5_c_re, s5_c_im, s5_d, s5_w_glu, s5_w_out, hg_w_in, hg_lower_bounds, hg_gnorm_w, hg_w_out, ffn_w_up, ffn_conv_w, ffn_conv_b, ffn_w_down, loss_target, m_c_ctx, m_w_mod, m_b_mod, m_norm1_w, m_norm2_w, m_final_norm_w, m_s5_w_in, m_s5_lam_re, m_s5_lam_im, m_s5_log_step, m_s5_b_re, m_s5_b_im, m_s5_c_re, m_s5_c_im, m_s5_d, m_s5_w_glu, m_s5_w_out, m_hg_w_in, m_hg_lower_bounds, m_hg_gnorm_w, m_hg_w_out, m_ffn_w_up, m_ffn_conv_w, m_ffn_conv_b, m_ffn_w_down, v_c_ctx, v_w_mod, v_b_mod, v_norm1_w, v_norm2_w, v_final_norm_w, v_s5_w_in, v_s5_lam_re, v_s5_lam_im, v_s5_log_step, v_s5_b_re, v_s5_b_im, v_s5_c_re, v_s5_c_im, v_s5_d, v_s5_w_glu, v_s5_w_out, v_hg_w_in, v_hg_lower_bounds, v_hg_gnorm_w, v_hg_w_out, v_ffn_w_up, v_ffn_conv_w, v_ffn_conv_b, v_ffn_w_down)))
    nb, seq, d = x.shape
    assert nb == NB
    rc = nb * ctx.shape[1]
    cfg = {"rc": rc}
    f = a["ffn_w_down"].shape[1] * 4
    core = lax.axis_index("c").astype(jnp.int32).reshape(1)

    w = {n: a[n] for n in WEIGHTS if n not in SHARD_AXIS}
    w.update(_gather_weights(a, [n for n in WEIGHTS if n in SHARD_AXIS]))

    tmaj = lambda t: t.transpose(1, 0, 2).reshape(-1, t.shape[-1])
    x0 = jnp.concatenate([tmaj(ctx), tmaj(x)], axis=0)
    tgt = tmaj(a["loss_target"])
    c16 = jnp.concatenate([jnp.broadcast_to(c_ctx[None], (8, d)), c, c], axis=0)
    mt, scb = [], None
    for l in range(2):
        m_, scb = mod_fwd(c16, w["w_mod"][l], w["b_mod"][l][None], name=f"mod_fwd{l}")
        mt.append(m_)
    n1, n2 = w["norm1_w"], w["norm2_w"]

    def ffn_fwd(l, h):
        u = mm(h, w["ffn_w_up"][l], name=f"ffn_up{l}")
        act = ffn_mid_fwd(cfg, u, w["ffn_conv_w"][l], w["ffn_conv_b"][l][None], name=f"ffn_mid{l}")
        return u, act, mm(act, w["ffn_w_down"][l], name=f"ffn_down{l}")

    def ffn_bwd(l, dfo, u, act, h):
        dact = mm(dfo, w["ffn_w_down"][l], tb=True, name=f"ffn_down_dx{l}")
        dwd = mm(act, dfo, ta=True, name=f"ffn_down_dw{l}")
        du, dcw, dcb = ffn_mid_bwd(cfg, dact, u, w["ffn_conv_w"][l], w["ffn_conv_b"][l][None], name=f"ffn_mid_bwd{l}")
        dh = mm(du, w["ffn_w_up"][l], tb=True, name=f"ffn_up_dx{l}")
        dwu = mm(h, du, ta=True, name=f"ffn_up_dw{l}")
        return dh, dwu, dcw, dcb[0], dwd

    g_, p_ = d // S5_GROUP, S5_STATE
    ns, kb = g_ * p_, d // S5_KIN
    s5p = (w["s5_lam_re"][0].reshape(2 * g_, p_), w["s5_lam_im"][0].reshape(2 * g_, p_), w["s5_log_step"][0].reshape(2 * g_, 1),
           w["s5_b_re"][0].transpose(0, 1, 3, 2).reshape(2 * g_, S5_GROUP, p_), w["s5_b_im"][0].transpose(0, 1, 3, 2).reshape(2 * g_, S5_GROUP, p_))
    ar, ai, bbr, bbi = s5_disc_fwd(*s5p, name="s5_disc")
    dsk = w["s5_d"]
    _, h1 = node_fwd(cfg, x0, None, None, 0, n1[0:1], mt[0], 0, name="node0a")
    u0 = mm(h1, w["s5_w_in"][0], name="s5_in")
    s5s, ys = [], []
    for dd in range(2):
        sl = slice(dd * g_, (dd + 1) * g_)
        prm = (ar[sl].reshape(1, ns), ai[sl].reshape(1, ns), _blockdiag_b(bbr[sl], kb).astype(MXU), _blockdiag_b(bbi[sl], kb).astype(MXU),
               _blockdiag_c(w["s5_c_re"][0, dd], kb).astype(MXU), _blockdiag_c(w["s5_c_im"][0, dd], kb).astype(MXU))
        sre, sim, ere, eim, y_ = s5_scan_fwd(cfg, u0, *prm, rev=dd == 1, name=f"s5_scan{dd}")
        s5s.append((sre, sim, ere, eim) + prm)
        ys.append(y_)

    def glu_a(u, y0, y1, ds):
        yp = (ds * u + y0) + y1
        return yp, _gelu(yp)

    ypre, zgb = rowmap(glu_a, [u0, ys[0], ys[1]], [dsk], [(d, F32), (d, MXU)], name="s5_glu_a")
    tg = mm(zgb, w["s5_w_glu"][0], name="s5_glu")
    (z2,) = rowmap(lambda yp, t: _gelu(yp) * jax.nn.sigmoid(t), [ypre, tg], [], [(d, MXU)], name="s5_glu_b")
    y1a = mm(z2, w["s5_w_out"][0], name="s5_out")
    x1a, h2a = node_fwd(cfg, x0, y1a, mt[0], 2, n2[0:1], mt[0], 3, name="node0b")
    ufa, acta, foa = ffn_fwd(0, h2a)

    x2a, h1b = node_fwd(cfg, x1a, foa, mt[0], 5, n1[1:2], mt[1], 0, name="node1a")
    z = mm(h1b, w["hg_w_in"][0], name="hg_in")
    e0, e1 = w["hg_lower_bounds"][:, 0, :], w["hg_lower_bounds"][:, 1, :]
    lb = hg_lb_fwd(e0, e1, name="hg_lb")
    gw = w["hg_gnorm_w"]
    o0, sin0 = hg_scan_fwd(cfg, z, lb[0:1], d_dir=0, name="hg_scan0")
    o1, sin1 = hg_scan_fwd(cfg, z, lb[1:2], d_dir=1, name="hg_scan1")
    onb = hg_read_fwd(o0, o1, z, gw, name="hg_read")
    y1b = mm(onb, w["hg_w_out"][0], name="hg_out")
    x1b, h2b = node_fwd(cfg, x2a, y1b, mt[1], 2, n2[1:2], mt[1], 3, name="node1b")
    ufb, actb, fob = ffn_fwd(1, h2b)
    loss_p, dx2b, dfob, dg2_1, dfnw = final_node(cfg, x1b, fob, mt[1], 5, w["final_norm_w"][None], tgt, name="final_node")

    gr = {}
    dh2b, dwu1, dcw1, dcb1, dwd1 = ffn_bwd(1, dfob, ufb, actb, h2b)
    dx1b, dy1b, dn2_1, dsh2_1, dsc2_1, dg1_1 = node_bwd(cfg, dx2b, dh2b, x1b, y1b, mt[1], 2, n2[1:2], mt[1], 3, name="node1b_bwd")
    don = mm(dy1b, w["hg_w_out"][0], tb=True, name="hg_out_dx")
    gr["hg_w_out"] = mm(onb, dy1b, ta=True, name="hg_out_dw")[None]
    do_, dgate_, dgw = hg_read_bwd(don, o0, o1, z, gw, name="hg_read_bwd")
    dq, dv, dxf, dlb0 = hg_scan_bwd(cfg, do_, z, lb[0:1], sin0, None, None, d_dir=0, name="hg_scan_bwd0")
    dq, dv, dxb, dlb1 = hg_scan_bwd(cfg, do_, z, lb[1:2], sin1, dq, dv, d_dir=1, name="hg_scan_bwd1")
    dz = jnp.concatenate([t_.astype(MXU) for t_ in (dq, dv, dxf, dxb, dgate_)], axis=1)
    dh1b = mm(dz, w["hg_w_in"][0], tb=True, name="hg_in_dx")
    gr["hg_w_in"] = mm(h1b, dz, ta=True, name="hg_in_dw")[None]
    de0, de1 = hg_lb_bwd(e0, e1, jnp.concatenate([dlb0, dlb1], axis=0), name="hg_lb_bwd")
    gr["hg_lower_bounds"] = jnp.stack([de0, de1], axis=1)
    gr["hg_gnorm_w"] = dgw
    dx2a, dfoa, dn1_1, dsh1_1, dsc1_1, dg2_0 = node_bwd(cfg, dx1b, dh1b, x2a, foa, mt[0], 5, n1[1:2], mt[1], 0, name="node1a_bwd")

    dh2a, dwu0, dcw0, dcb0, dwd0 = ffn_bwd(0, dfoa, ufa, acta, h2a)
    dx1a, dy1a, dn2_0, dsh2_0, dsc2_0, dg1_0 = node_bwd(cfg, dx2a, dh2a, x1a, y1a, mt[0], 2, n2[0:1], mt[0], 3, name="node0b_bwd")
    dz2 = mm(dy1a, w["s5_w_out"][0], tb=True, name="s5_out_dx")
    gr["s5_w_out"] = mm(z2, dy1a, ta=True, name="s5_out_dw")[None]

    def glu_b_bwd(dz2_, yp, t):
        zg, sg = _gelu(yp), jax.nn.sigmoid(t)
        return dz2_ * zg * sg * (1.0 - sg), dz2_ * sg

    dtg, dzg_dir = rowmap(glu_b_bwd, [dz2, ypre, tg], [], [(d, MXU), (d, F32)], name="s5_glu_b_bwd")
    dzg_mm = mm(dtg, w["s5_w_glu"][0], tb=True, name="s5_glu_dx")
    gr["s5_w_glu"] = mm(zgb, dtg, ta=True, name="s5_glu_dw")[None]

    def glu_a_bwd(dzd, dzm, yp, u, ds):
        _, vjp = jax.vjp(_gelu, yp)
        (dy,) = vjp(dzd + dzm)
        return dy, dy * ds, jnp.sum(dy * u, axis=0, keepdims=True)

    dyb, du, ddsk = rowmap(glu_a_bwd, [dzg_dir, dzg_mm, ypre, u0], [dsk], [(d, MXU), (d, F32)], [(1, d)], name="s5_glu_a_bwd")
    gr["s5_d"] = ddsk
    dar, dai, dbr, dbi, dcr, dci = [], [], [], [], [], []
    for dd in range(2):
        sre, sim, ere, eim, a_re, a_im, bre, bim, cre, cim = s5s[dd]
        du, gre, gim, da_r, da_i = s5_scan_bwd(cfg, dyb, sre, sim, ere, eim, a_re, a_im, bre, bim, cre, cim, du, rev=dd == 1,
                                               name=f"s5_scan_bwd{dd}")
        dar.append(colsum(da_r, name=f"s5_da_re{dd}").reshape(g_, p_))
        dai.append(colsum(da_i, name=f"s5_da_im{dd}").reshape(g_, p_))
        dbr.append(_diag_b(blockdiag_tn(u0, gre, S5_KIN, S5_KST, name=f"s5_db_re{dd}"), kb))
        dbi.append(_diag_b(blockdiag_tn(u0, gim, S5_KIN, S5_KST, name=f"s5_db_im{dd}"), kb))
        dcr.append(_diag_c(blockdiag_tn(sre.reshape(-1, ns), dyb, S5_KST, S5_KIN, name=f"s5_dc_re{dd}"), kb))
        dci.append(_diag_c(blockdiag_tn(sim.reshape(-1, ns), dyb, S5_KST, S5_KIN, scale=-1.0, name=f"s5_dc_im{dd}"), kb))
    cat = lambda l_: jnp.concatenate(l_, axis=0)
    dlr, dli, dls, dbre, dbim = s5_disc_bwd(*s5p, cat(dar), cat(dai), cat(dbr), cat(dbi), name="s5_disc_bwd")
    gr["s5_lam_re"], gr["s5_lam_im"] = dlr.reshape(1, 2, g_, p_), dli.reshape(1, 2, g_, p_)
    gr["s5_log_step"] = dls.reshape(1, 2, g_)
    gr["s5_b_re"] = dbre.reshape(1, 2, g_, S5_GROUP, p_).transpose(0, 1, 2, 4, 3)
    gr["s5_b_im"] = dbim.reshape(1, 2, g_, S5_GROUP, p_).transpose(0, 1, 2, 4, 3)
    gr["s5_c_re"], gr["s5_c_im"] = jnp.stack(dcr)[None], jnp.stack(dci)[None]
    dh1 = mm(du, w["s5_w_in"][0], tb=True, name="s5_in_dx")
    gr["s5_w_in"] = mm(h1, du, ta=True, name="s5_in_dw")[None]
    dx0, _, dn1_0, dsh1_0, dsc1_0, _ = node_bwd(cfg, dx1a, dh1, x0, None, None, 0, n1[0:1], mt[0], 0, name="node0a_bwd")

    dmt = [jnp.concatenate([dsh1_0, dsc1_0, dg1_0, dsh2_0, dsc2_0, dg2_0], axis=1),
           jnp.concatenate([dsh1_1, dsc1_1, dg1_1, dsh2_1, dsc2_1, dg2_1], axis=1)]
    gr["w_mod"] = jnp.stack([mm(scb, dmt[l], ta=True, name=f"mod_dw{l}") for l in range(2)])
    gr["b_mod"] = jnp.concatenate([colsum(dmt[l], name=f"mod_db{l}") for l in range(2)], axis=0)
    dsc16 = [mm(dmt[l], w["w_mod"][l], tb=True, name=f"mod_dx{l}") for l in range(2)]
    gr["c_ctx"] = cctx_grad(c16, dsc16, name="c_ctx_grad")[0]
    gr["norm1_w"] = jnp.concatenate([dn1_0, dn1_1], axis=0)
    gr["norm2_w"] = jnp.concatenate([dn2_0, dn2_1], axis=0)
    gr["final_norm_w"] = dfnw[0]
    gr["ffn_w_up"], gr["ffn_conv_w"] = jnp.stack([dwu0, dwu1]), jnp.stack([dcw0, dcw1])
    gr["ffn_conv_b"], gr["ffn_w_down"] = jnp.stack([dcb0, dcb1]), jnp.stack([dwd0, dwd1])

    red = _reduce_grads(a, gr, core)
    loss = lax.psum(loss_p[0, 0], ("x", "y", "c"))
    grad_x = dx0[rc:].reshape(seq, nb, d).transpose(1, 0, 2)
    upd = {n: adamw(a[n], red[n], a["m_" + n], a["v_" + n], name="adamw_" + n) for n in WEIGHTS}
    return (loss, grad_x, *[red[n] for n in WEIGHTS], *[upd[n][0] for n in WEIGHTS], *[upd[n][1] for n in WEIGHTS],
            *[upd[n][2] for n in WEIGHTS])
```

```python
import functools
import math

import jax
import jax.numpy as jnp
from jax import lax
from jax.experimental import pallas as pl
from jax.experimental.pallas import tpu as pltpu

F32 = jnp.float32
BF = jnp.bfloat16
MXU = jnp.bfloat16

NORM_EPS = 1e-6
GRID_W = 64
N_MOD = 6
S5_GROUP = 16
S5_STATE = 64
S5_LAM_RE_MAX = -1e-4
S5_KIN = 256
S5_KST = S5_KIN // S5_GROUP * S5_STATE
HEAD = 128
CHUNK_ROWS = 128
N_PROJ = 5
NB = 4
ADAM_LR, ADAM_B1, ADAM_B2, ADAM_EPS, ADAM_WD, ADAM_STEP = 0.001, 0.9, 0.999, 1e-08, 0.01, 10
VMEM_LIMIT = 56 * 1024 * 1024
MESH = pl.DeviceIdType.MESH


def _tile(n, cap):
    if n <= cap:
        return n
    best = None
    for t in range(128, cap + 1, 128):
        if n % t == 0:
            best = t
    assert best is not None, (n, cap)
    return best


def _row_tile(r, width=1024):
    cap = max(8, (512 * 1024) // max(width, 1))
    return next((t for t in (512, 256, 128, 64, 32, 16, 8) if t <= cap and r % t == 0), r)


def _cp(sem):
    return pltpu.CompilerParams(dimension_semantics=sem, vmem_limit_bytes=VMEM_LIMIT)


def _dot(a, b, ca=1, cb=0):
    return lax.dot_general(a.astype(MXU), b.astype(MXU), (((ca,), (cb,)), ((), ())), preferred_element_type=F32)


def _dot3(m, x):
    hi = x.astype(MXU)
    r1 = x - hi.astype(F32)
    mid = r1.astype(MXU)
    lo = (r1 - mid.astype(F32)).astype(MXU)
    return _dot(m, hi) + _dot(m, mid) + _dot(m, lo)


def mm(a, b, *, ta=False, tb=False, out_dtype=F32, name):
    (kd, m) = a.shape if ta else a.shape[::-1]
    (n, kd2) = b.shape if tb else b.shape[::-1]
    assert kd == kd2, (a.shape, b.shape, ta, tb)
    tm, tn, tk = _tile(m, 1024), _tile(n, 1536), _tile(kd, 1024)
    nk = kd // tk

    def body(a_ref, b_ref, o_ref, acc_ref):
        k = pl.program_id(2)

        @pl.when(k == 0)
        def _():
            acc_ref[...] = jnp.zeros_like(acc_ref)

        acc_ref[...] += _dot(a_ref[...], b_ref[...], 0 if ta else 1, 1 if tb else 0)

        @pl.when(k == nk - 1)
        def _():
            o_ref[...] = acc_ref[...].astype(out_dtype)

    a_spec = pl.BlockSpec((tk, tm), lambda i, j, k: (k, i)) if ta else pl.BlockSpec((tm, tk), lambda i, j, k: (i, k))
    b_spec = pl.BlockSpec((tn, tk), lambda i, j, k: (j, k)) if tb else pl.BlockSpec((tk, tn), lambda i, j, k: (k, j))
    return pl.pallas_call(
        body, name=name, grid=(m // tm, n // tn, nk), in_specs=[a_spec, b_spec],
        out_specs=pl.BlockSpec((tm, tn), lambda i, j, k: (i, j)), out_shape=jax.ShapeDtypeStruct((m, n), out_dtype),
        scratch_shapes=[pltpu.VMEM((tm, tn), F32)], compiler_params=_cp(("parallel", "parallel", "arbitrary")))(a, b)


def blockdiag_tn(a, b, wa, wb, *, scale=1.0, name):
    rows = a.shape[0]
    kb = a.shape[1] // wa
    tr = _tile(rows, 1024)
    nr = rows // tr

    def body(a_ref, b_ref, o_ref):
        i = pl.program_id(1)

        @pl.when(i == 0)
        def _():
            o_ref[...] = jnp.zeros_like(o_ref)

        o_ref[0] += scale * _dot(a_ref[...], b_ref[...], 0, 0)

    return pl.pallas_call(
        body, name=name, grid=(kb, nr),
        in_specs=[pl.BlockSpec((tr, wa), lambda k, i: (i, k)), pl.BlockSpec((tr, wb), lambda k, i: (i, k))],
        out_specs=pl.BlockSpec((1, wa, wb), lambda k, i: (k, 0, 0)), out_shape=jax.ShapeDtypeStruct((kb, wa, wb), F32),
        compiler_params=_cp(("parallel", "arbitrary")))(a, b)


def _pat(v, p, op):
    tm, d = v.shape
    return op(v.reshape(tm // 8, 8, d), p[None]).reshape(tm, d)


def _norm_mod(x, nw, shift, scale):
    y = x * lax.rsqrt(jnp.mean(x * x, axis=-1, keepdims=True) + NORM_EPS) * nw
    return _pat(_pat(y, 1.0 + scale, jnp.multiply), shift, jnp.add)


def _mt_spec(d, nct):
    return pl.BlockSpec((8, N_MOD * d), lambda i: (jnp.where(i < nct, 0, 1), 0))


def _acc_spec(d, nct):
    return pl.BlockSpec((8, d), lambda i: (jnp.where(i < nct, 0, 1), 0))


def _rows(cfg):
    tm = min(512, cfg["rc"])
    return tm, cfg["rc"] // tm


def node_fwd(cfg, xp, y, mtg, gi, nw, mtn, si, *, name):
    r, d = xp.shape
    tm, nct = _rows(cfg)
    row = pl.BlockSpec((tm, d), lambda i: (i, 0))
    vec = pl.BlockSpec((1, d), lambda i: (0, 0))

    def body(*refs):
        if y is None:
            xp_ref, nw_ref, mtn_ref, h_ref = refs
            x = xp_ref[...]
        else:
            xp_ref, y_ref, mtg_ref, nw_ref, mtn_ref, xn_ref, h_ref = refs
            x = xp_ref[...] + _pat(y_ref[...], mtg_ref[:, gi * d:(gi + 1) * d], jnp.multiply)
            xn_ref[...] = x
        h_ref[...] = _norm_mod(x, nw_ref[...], mtn_ref[:, si * d:(si + 1) * d], mtn_ref[:, (si + 1) * d:(si + 2) * d]).astype(MXU)

    h_shape = jax.ShapeDtypeStruct((r, d), MXU)
    if y is None:
        h = pl.pallas_call(body, name=name, grid=(r // tm,), in_specs=[row, vec, _mt_spec(d, nct)], out_specs=row,
                           out_shape=h_shape, compiler_params=_cp(("parallel",)))(xp, nw, mtn)
        return xp, h
    return pl.pallas_call(body, name=name, grid=(r // tm,), in_specs=[row, row, _mt_spec(d, nct), vec, _mt_spec(d, nct)],
                          out_specs=(row, row), out_shape=(jax.ShapeDtypeStruct((r, d), F32), h_shape),
                          compiler_params=_cp(("parallel",)))(xp, y, mtg, nw, mtn)


def node_bwd(cfg, dxres, dh, xn, y, mtg, gi, nw, mtn, si, *, name):
    r, d = xn.shape
    tm, nct = _rows(cfg)
    row = pl.BlockSpec((tm, d), lambda i: (i, 0))
    vec = pl.BlockSpec((1, d), lambda i: (0, 0))
    has_y = y is not None

    def body(*refs):
        if has_y:
            dxres_ref, dh_ref, xn_ref, y_ref, mtg_ref, nw_ref, mtn_ref, dxn_ref, dy_ref, dnw_ref, dsh_ref, dsc_ref, dg_ref = refs
        else:
            dxres_ref, dh_ref, xn_ref, nw_ref, mtn_ref, dxn_ref, dnw_ref, dsh_ref, dsc_ref = refs
        i = pl.program_id(0)
        _, vjp = jax.vjp(_norm_mod, xn_ref[...], nw_ref[...], mtn_ref[:, si * d:(si + 1) * d], mtn_ref[:, (si + 1) * d:(si + 2) * d])
        dx, dnw, dsh, dsc = vjp(dh_ref[...])
        dx = dx + dxres_ref[...]
        dxn_ref[...] = dx

        @pl.when(i == 0)
        def _():
            dnw_ref[...] = jnp.zeros_like(dnw_ref)

        @pl.when((i == 0) | (i == nct))
        def _():
            dsh_ref[...] = jnp.zeros_like(dsh_ref)
            dsc_ref[...] = jnp.zeros_like(dsc_ref)
            if has_y:
                dg_ref[...] = jnp.zeros_like(dg_ref)

        dnw_ref[...] += dnw
        dsh_ref[...] += dsh
        dsc_ref[...] += dsc
        if has_y:
            dy_ref[...] = _pat(dx, mtg_ref[:, gi * d:(gi + 1) * d], jnp.multiply).astype(MXU)
            dg_ref[...] += jnp.sum((dx * y_ref[...]).reshape(tm // 8, 8, d), axis=0)

    acc = jax.ShapeDtypeStruct((16, d), F32)
    xs = jax.ShapeDtypeStruct((r, d), F32)
    if has_y:
        return pl.pallas_call(
            body, name=name, grid=(r // tm,), in_specs=[row, row, row, row, _mt_spec(d, nct), vec, _mt_spec(d, nct)],
            out_specs=(row, row, vec, _acc_spec(d, nct), _acc_spec(d, nct), _acc_spec(d, nct)),
            out_shape=(xs, jax.ShapeDtypeStruct((r, d), MXU), jax.ShapeDtypeStruct((1, d), F32), acc, acc, acc),
            compiler_params=_cp(("arbitrary",)))(dxres, dh, xn, y, mtg, nw, mtn)
    dxn, dnw, dsh, dsc = pl.pallas_call(
        body, name=name, grid=(r // tm,), in_specs=[row, row, row, vec, _mt_spec(d, nct)],
        out_specs=(row, vec, _acc_spec(d, nct), _acc_spec(d, nct)),
        out_shape=(xs, jax.ShapeDtypeStruct((1, d), F32), acc, acc), compiler_params=_cp(("arbitrary",)))(dxres, dh, xn, nw, mtn)
    return dxn, None, dnw, dsh, dsc, None


def final_node(cfg, xp, y, mtg, gi, fnw, tgt, *, name):
    r, d = xp.shape
    tm, nct = _rows(cfg)
    row = pl.BlockSpec((tm, d), lambda i: (i, 0))
    vec = pl.BlockSpec((1, d), lambda i: (0, 0))

    def norm(x, w):
        return x * lax.rsqrt(jnp.mean(x * x, axis=-1, keepdims=True) + NORM_EPS) * w

    def body(xp_ref, y_ref, mtg_ref, fnw_ref, tgt_ref, loss_ref, dx_ref, dy_ref, dg_ref, dfnw_ref):
        i = pl.program_id(0)
        g = mtg_ref[:, gi * d:(gi + 1) * d]
        x = xp_ref[...] + _pat(y_ref[...], g, jnp.multiply)
        out, vjp = jax.vjp(norm, x, fnw_ref[...])
        lat = i >= nct
        err = jnp.where(lat, out - tgt_ref[...], 0.0)
        dx, dfnw = vjp(err * (1.0 / d))

        @pl.when(i == 0)
        def _():
            loss_ref[...] = jnp.zeros_like(loss_ref)
            dfnw_ref[...] = jnp.zeros_like(dfnw_ref)

        @pl.when((i == 0) | (i == nct))
        def _():
            dg_ref[...] = jnp.zeros_like(dg_ref)

        loss_ref[...] += jnp.full(loss_ref.shape, 0.5 / d * jnp.sum(err * err), F32)
        dfnw_ref[...] += dfnw
        dx_ref[...] = dx
        dy_ref[...] = _pat(dx, g, jnp.multiply).astype(MXU)
        dg_ref[...] += jnp.sum((dx * y_ref[...]).reshape(tm // 8, 8, d), axis=0)

    return pl.pallas_call(
        body, name=name, grid=(r // tm,),
        in_specs=[row, row, _mt_spec(d, nct), vec, pl.BlockSpec((tm, d), lambda i: (jnp.maximum(i - nct, 0), 0))],
        out_specs=(pl.BlockSpec((8, 128), lambda i: (0, 0)), row, row, _acc_spec(d, nct), vec),
        out_shape=(jax.ShapeDtypeStruct((8, 128), F32), jax.ShapeDtypeStruct((r, d), F32), jax.ShapeDtypeStruct((r, d), MXU),
                   jax.ShapeDtypeStruct((16, d), F32), jax.ShapeDtypeStruct((1, d), F32)),
        compiler_params=_cp(("arbitrary",)))(xp, y, mtg, fnw, tgt)


def _silu(x):
    return x * jax.nn.sigmoid(x)


def mod_fwd(c16, w, b, *, name):
    d, n = w.shape
    tn = _tile(n, 1536)

    def body(c_ref, w_ref, b_ref, o_ref, s_ref):
        s = _silu(c_ref[...])
        s_ref[...] = s.astype(MXU)
        o_ref[...] = _dot(s, w_ref[...]) + b_ref[...]

    return pl.pallas_call(
        body, name=name, grid=(n // tn,),
        in_specs=[pl.BlockSpec((16, d), lambda j: (0, 0)), pl.BlockSpec((d, tn), lambda j: (0, j)), pl.BlockSpec((1, tn), lambda j: (0, j))],
        out_specs=(pl.BlockSpec((16, tn), lambda j: (0, j)), pl.BlockSpec((16, d), lambda j: (0, 0))),
        out_shape=(jax.ShapeDtypeStruct((16, n), F32), jax.ShapeDtypeStruct((16, d), MXU)),
        compiler_params=_cp(("arbitrary",)))(c16, w, b)


def colsum(x, *, name):
    def body(x_ref, o_ref):
        o_ref[...] = jnp.sum(x_ref[...], axis=0, keepdims=True)

    return pl.pallas_call(body, name=name, out_shape=jax.ShapeDtypeStruct((1, x.shape[1]), F32))(x)


def cctx_grad(c16, ds_list, *, name):
    def body(c_ref, *refs):
        o_ref = refs[-1]
        ds = refs[0][...]
        for r_ in refs[1:-1]:
            ds = ds + r_[...]
        _, vjp = jax.vjp(_silu, c_ref[...])
        (dc,) = vjp(ds)
        o_ref[...] = jnp.sum(dc[0:8], axis=0, keepdims=True)

    return pl.pallas_call(body, name=name, out_shape=jax.ShapeDtypeStruct((1, c16.shape[1]), F32))(c16, *ds_list)


def _s5_disc(lam_re, lam_im, log_step, b_re, b_im):
    lr = jnp.minimum(lam_re, S5_LAM_RE_MAX)
    li = lam_im
    dt = jnp.exp(log_step)
    mag = jnp.exp(lr * dt)
    abar_r = mag * jnp.cos(li * dt)
    abar_i = mag * jnp.sin(li * dt)
    den = lr * lr + li * li
    nr = abar_r - 1.0
    coef_r = (nr * lr + abar_i * li) / den
    coef_i = (abar_i * lr - nr * li) / den
    bbar_r = coef_r[:, None, :] * b_re - coef_i[:, None, :] * b_im
    bbar_i = coef_r[:, None, :] * b_im + coef_i[:, None, :] * b_re
    return abar_r, abar_i, bbar_r, bbar_i


def s5_disc_fwd(lam_re, lam_im, log_step, b_re, b_im, *, name):
    def body(lr, li, ls, br, bi, ar_o, ai_o, br_o, bi_o):
        ar_o[...], ai_o[...], br_o[...], bi_o[...] = _s5_disc(lr[...], li[...], ls[...], br[...], bi[...])

    s2, s3 = jax.ShapeDtypeStruct(lam_re.shape, F32), jax.ShapeDtypeStruct(b_re.shape, F32)
    return pl.pallas_call(body, name=name, out_shape=(s2, s2, s3, s3))(lam_re, lam_im, log_step, b_re, b_im)


def s5_disc_bwd(lam_re, lam_im, log_step, b_re, b_im, d_ar, d_ai, d_br, d_bi, *, name):
    def body(lr, li, ls, br, bi, dar, dai, dbr, dbi, o_lr, o_li, o_ls, o_br, o_bi):
        _, vjp = jax.vjp(_s5_disc, lr[...], li[...], ls[...], br[...], bi[...])
        o_lr[...], o_li[...], o_ls[...], o_br[...], o_bi[...] = vjp((dar[...], dai[...], dbr[...], dbi[...]))

    s2, s3 = jax.ShapeDtypeStruct(lam_re.shape, F32), jax.ShapeDtypeStruct(b_re.shape, F32)
    return pl.pallas_call(body, name=name, out_shape=(s2, s2, jax.ShapeDtypeStruct(log_step.shape, F32), s3, s3))(
        lam_re, lam_im, log_step, b_re, b_im, d_ar, d_ai, d_br, d_bi)


S5_LANES = 512


def _chunk_order(k, ncc, nch, rev):
    if not rev:
        return k
    return jnp.where(k < ncc, ncc - 1 - k, nch - 1 - (k - ncc))


def _cmul(ar, ai, xr, xi):
    return ar * xr - ai * xi, ar * xi + ai * xr


def s5_scan_fwd(cfg, u, a_re, a_im, bre, bim, cre, cim, *, rev, name):
    r, d = u.shape
    ns = a_re.shape[1]
    kb = d // S5_KIN
    tcr = 128
    n8 = tcr // 8
    nch, ncc = r // tcr, cfg["rc"] // tcr
    lw = min(S5_LANES, ns)

    def body(u_ref, ar_ref, ai_ref, bre_ref, bim_ref, cre_ref, cim_ref, sre_ref, sim_ref, ere_ref, eim_ref, y_ref, st_re, st_im):
        k = pl.program_id(0)

        @pl.when(k == 0)
        def _():
            st_re[...] = jnp.zeros_like(st_re)
            st_im[...] = jnp.zeros_like(st_im)

        ub = u_ref[...].astype(MXU)
        for j in range(kb):
            uj = ub[:, j * S5_KIN:(j + 1) * S5_KIN]
            sre_ref[:, :, j * S5_KST:(j + 1) * S5_KST] = _dot(uj, bre_ref[j]).reshape(n8, 8, S5_KST)
            sim_ref[:, :, j * S5_KST:(j + 1) * S5_KST] = _dot(uj, bim_ref[j]).reshape(n8, 8, S5_KST)
        ere_ref[0] = st_re[...]
        eim_ref[0] = st_im[...]
        first = lax.broadcasted_iota(jnp.int32, (8, lw), 0) < 4
        if rev:
            first = jnp.logical_not(first)
        for c in range(ns // lw):
            sl = slice(c * lw, (c + 1) * lw)
            ar = jnp.broadcast_to(ar_ref[:, sl], (8, lw))
            ai = jnp.broadcast_to(ai_ref[:, sl], (8, lw))

            def step(i, carry, sl=sl, ar=ar, ai=ai):
                sr, si = carry
                ii = n8 - 1 - i if rev else i
                tr, ti = sre_ref[ii, :, sl], sim_ref[ii, :, sl]
                pr, pi = _cmul(ar, ai, sr, si)
                t1r, t1i = pr + tr, pi + ti
                s1r = jnp.where(first, t1r, pltpu.roll(t1r, 4, 0))
                s1i = jnp.where(first, t1i, pltpu.roll(t1i, 4, 0))
                pr, pi = _cmul(ar, ai, s1r, s1i)
                t2r, t2i = pr + tr, pi + ti
                sre_ref[ii, :, sl] = jnp.where(first, t1r, t2r)
                sim_ref[ii, :, sl] = jnp.where(first, t1i, t2i)
                return jnp.where(first, pltpu.roll(t2r, 4, 0), t2r), jnp.where(first, pltpu.roll(t2i, 4, 0), t2i)

            sr, si = lax.fori_loop(0, n8, step, (st_re[:, sl], st_im[:, sl]))
            st_re[:, sl] = sr
            st_im[:, sl] = si
        for j in range(kb):
            sr = sre_ref[:, :, j * S5_KST:(j + 1) * S5_KST].reshape(tcr, S5_KST)
            si = sim_ref[:, :, j * S5_KST:(j + 1) * S5_KST].reshape(tcr, S5_KST)
            y_ref[:, j * S5_KIN:(j + 1) * S5_KIN] = _dot(sr, cre_ref[j]) - _dot(si, cim_ref[j])

    cidx = functools.partial(_chunk_order, ncc=ncc, nch=nch, rev=rev)
    full = lambda a: pl.BlockSpec(a.shape, lambda k: (0,) * a.ndim)
    st = pl.BlockSpec((n8, 8, ns), lambda k: (cidx(k), 0, 0))
    en = pl.BlockSpec((1, 8, ns), lambda k: (cidx(k), 0, 0))
    return pl.pallas_call(
        body, name=name, grid=(nch,),
        in_specs=[pl.BlockSpec((tcr, d), lambda k: (cidx(k), 0)), full(a_re), full(a_im), full(bre), full(bim), full(cre), full(cim)],
        out_specs=(st, st, en, en, pl.BlockSpec((tcr, d), lambda k: (cidx(k), 0))),
        out_shape=(jax.ShapeDtypeStruct((r // 8, 8, ns), F32),) * 2 + (jax.ShapeDtypeStruct((nch, 8, ns), F32),) * 2
        + (jax.ShapeDtypeStruct((r, d), F32),),
        scratch_shapes=[pltpu.VMEM((8, ns), F32), pltpu.VMEM((8, ns), F32)], compiler_params=_cp(("arbitrary",)))(
            u, a_re, a_im, bre, bim, cre, cim)


def s5_scan_bwd(cfg, dyb, sre, sim, ere, eim, a_re, a_im, bre, bim, cre, cim, du_in, *, rev, name):
    r, d = dyb.shape
    ns = a_re.shape[1]
    kb = d // S5_KIN
    tcr = 128
    n8 = tcr // 8
    nch, ncc = r // tcr, cfg["rc"] // tcr
    lw = min(S5_LANES, ns)

    def body(dy_ref, sre_ref, sim_ref, ere_ref, eim_ref, ar_ref, ai_ref, bre_ref, bim_ref, cre_ref, cim_ref, duin_ref,
             du_ref, gre_ref, gim_ref, dar_ref, dai_ref, g_re, g_im, gc_re, gc_im):
        k = pl.program_id(0)

        @pl.when(k == 0)
        def _():
            gc_re[...] = jnp.zeros_like(gc_re)
            gc_im[...] = jnp.zeros_like(gc_im)
            dar_ref[...] = jnp.zeros_like(dar_ref)
            dai_ref[...] = jnp.zeros_like(dai_ref)

        dy = dy_ref[...]
        for j in range(kb):
            dyj = dy[:, j * S5_KIN:(j + 1) * S5_KIN]
            g_re[:, :, j * S5_KST:(j + 1) * S5_KST] = _dot(dyj, cre_ref[j], 1, 1).reshape(n8, 8, S5_KST)
            g_im[:, :, j * S5_KST:(j + 1) * S5_KST] = -_dot(dyj, cim_ref[j], 1, 1).reshape(n8, 8, S5_KST)
        first = lax.broadcasted_iota(jnp.int32, (8, lw), 0) < 4
        if rev:
            first = jnp.logical_not(first)
        for c in range(ns // lw):
            sl = slice(c * lw, (c + 1) * lw)
            ar = jnp.broadcast_to(ar_ref[:, sl], (8, lw))
            nai = -jnp.broadcast_to(ai_ref[:, sl], (8, lw))

            def step(i, carry, sl=sl, ar=ar, nai=nai):
                gr, gi, accr, acci = carry
                ii = i if rev else n8 - 1 - i
                tr, ti = g_re[ii, :, sl], g_im[ii, :, sl]
                pr, pi = _cmul(ar, nai, gr, gi)
                t2r, t2i = pr + tr, pi + ti
                g2r = jnp.where(first, pltpu.roll(t2r, 4, 0), t2r)
                g2i = jnp.where(first, pltpu.roll(t2i, 4, 0), t2i)
                pr, pi = _cmul(ar, nai, g2r, g2i)
                t1r, t1i = pr + tr, pi + ti
                outr, outi = jnp.where(first, t1r, t2r), jnp.where(first, t1i, t2i)
                g_re[ii, :, sl] = outr
                g_im[ii, :, sl] = outi
                pv = jnp.clip(ii + 1 if rev else ii - 1, 0, n8 - 1)
                at_entry = (ii == n8 - 1) if rev else (ii == 0)
                pvr = jnp.where(at_entry, ere_ref[0, :, sl], sre_ref[pv, :, sl])
                pvi = jnp.where(at_entry, eim_ref[0, :, sl], sim_ref[pv, :, sl])
                spr = pltpu.roll(jnp.where(first, sre_ref[ii, :, sl], pvr), 4, 0)
                spi = pltpu.roll(jnp.where(first, sim_ref[ii, :, sl], pvi), 4, 0)
                accr = accr + outr * spr + outi * spi
                acci = acci + outi * spr - outr * spi
                return jnp.where(first, t1r, pltpu.roll(t1r, 4, 0)), jnp.where(first, t1i, pltpu.roll(t1i, 4, 0)), accr, acci

            gr, gi, accr, acci = lax.fori_loop(0, n8, step, (gc_re[:, sl], gc_im[:, sl], dar_ref[:, sl], dai_ref[:, sl]))
            gc_re[:, sl] = gr
            gc_im[:, sl] = gi
            dar_ref[:, sl] = accr
            dai_ref[:, sl] = acci
        for j in range(kb):
            gr = g_re[:, :, j * S5_KST:(j + 1) * S5_KST].reshape(tcr, S5_KST)
            gi = g_im[:, :, j * S5_KST:(j + 1) * S5_KST].reshape(tcr, S5_KST)
            gre_ref[:, j * S5_KST:(j + 1) * S5_KST] = gr.astype(MXU)
            gim_ref[:, j * S5_KST:(j + 1) * S5_KST] = gi.astype(MXU)
            du_ref[:, j * S5_KIN:(j + 1) * S5_KIN] = (duin_ref[:, j * S5_KIN:(j + 1) * S5_KIN]
                                                     + _dot(gr, bre_ref[j], 1, 1) + _dot(gi, bim_ref[j], 1, 1))

    def cidx(k):
        return _chunk_order(nch - 1 - k, ncc, nch, rev)

    full = lambda a: pl.BlockSpec(a.shape, lambda k: (0,) * a.ndim)
    st = pl.BlockSpec((n8, 8, ns), lambda k: (cidx(k), 0, 0))
    en = pl.BlockSpec((1, 8, ns), lambda k: (cidx(k), 0, 0))
    rowd = pl.BlockSpec((tcr, d), lambda k: (cidx(k), 0))
    rown = pl.BlockSpec((tcr, ns), lambda k: (cidx(k), 0))
    acc = pl.BlockSpec((8, ns), lambda k: (0, 0))
    return pl.pallas_call(
        body, name=name, grid=(nch,),
        in_specs=[rowd, st, st, en, en, full(a_re), full(a_im), full(bre), full(bim), full(cre), full(cim), rowd],
        out_specs=(rowd, rown, rown, acc, acc),
        out_shape=(jax.ShapeDtypeStruct((r, d), F32), jax.ShapeDtypeStruct((r, ns), MXU), jax.ShapeDtypeStruct((r, ns), MXU),
                   jax.ShapeDtypeStruct((8, ns), F32), jax.ShapeDtypeStruct((8, ns), F32)),
        scratch_shapes=[pltpu.VMEM((n8, 8, ns), F32), pltpu.VMEM((n8, 8, ns), F32), pltpu.VMEM((8, ns), F32), pltpu.VMEM((8, ns), F32)],
        compiler_params=_cp(("arbitrary",)))(dyb, sre, sim, ere, eim, a_re, a_im, bre, bim, cre, cim, du_in)


def rowmap(fn, rows_in, vecs_in, outs, accs=(), *, name):
    r = rows_in[0].shape[0]
    tm = _row_tile(r, max(a.shape[1] for a in rows_in))
    nr, nv, no = len(rows_in), len(vecs_in), len(outs)

    def body(*refs):
        ins = [x[...] for x in refs[:nr + nv]]
        res = fn(*ins)
        if not isinstance(res, (tuple, list)):
            res = (res,)
        out_refs = refs[nr + nv:]
        for o_ref, v in zip(out_refs[:no], res[:no]):
            o_ref[...] = v.astype(o_ref.dtype)
        if accs:
            @pl.when(pl.program_id(0) == 0)
            def _():
                for a_ref in out_refs[no:]:
                    a_ref[...] = jnp.zeros_like(a_ref)
            for a_ref, v in zip(out_refs[no:], res[no:]):
                a_ref[...] += v

    in_specs = [pl.BlockSpec((tm, a.shape[1]), lambda i: (i, 0)) for a in rows_in]
    in_specs += [pl.BlockSpec(v.shape, lambda i, n=v.ndim: (0,) * n) for v in vecs_in]
    out_specs = [pl.BlockSpec((tm, w), lambda i: (i, 0)) for w, _ in outs] + [pl.BlockSpec(s, lambda i, n=len(s): (0,) * n) for s in accs]
    out_shape = [jax.ShapeDtypeStruct((r, w), dt) for w, dt in outs] + [jax.ShapeDtypeStruct(s, F32) for s in accs]
    res = pl.pallas_call(body, name=name, grid=(r // tm,), in_specs=in_specs, out_specs=tuple(out_specs), out_shape=tuple(out_shape),
                         compiler_params=_cp(("arbitrary",) if accs else ("parallel",)))(*rows_in, *vecs_in)
    return res


def _gelu(x):
    return jax.nn.gelu(x, approximate=True)


def _hg_lower_bound(e0, e1):
    m = jnp.maximum(e0, e1)
    a, b = jnp.exp(e0 - m), jnp.exp(e1 - m)
    return b / (a + b)


def _hg_gates(x, lb):
    logf = jnp.log(lb + (1.0 - lb) * jax.nn.sigmoid(x))
    return logf, (1.0 - lb) * jax.nn.sigmoid(-x)


def _hg_masks(rev):
    n = CHUNK_ROWS
    rr = lax.broadcasted_iota(jnp.int32, (n, n), 0)
    ss = lax.broadcasted_iota(jnp.int32, (n, n), 1)
    same = (rr % NB) == (ss % NB)
    causal = same & ((ss >= rr) if rev else (ss <= rr))
    anti = same & ((ss <= rr) if rev else (ss >= rr))
    end0 = 0 if rev else n - NB
    pick_end = ss == (end0 + rr % NB)
    return same, causal, anti, pick_end, end0


def _hg_expand(x):
    ex = lax.broadcasted_iota(jnp.int32, x.shape, 0) % NB
    return jnp.concatenate([jnp.where(ex == b, x, 0.0) for b in range(NB)], axis=1)


def _hg_fold(xe):
    kk = xe.shape[1] // NB
    ex = lax.broadcasted_iota(jnp.int32, (xe.shape[0], kk), 0) % NB
    out = jnp.zeros((xe.shape[0], kk), F32)
    for b in range(NB):
        out = out + jnp.where(ex == b, xe[:, b * kk:(b + 1) * kk], 0.0)
    return out


def _hg_chunk(q, v, x, lb, masks):
    same, causal, anti, pick_end, end0 = masks
    logf, kk = _hg_gates(x, lb)
    b = _dot3(causal.astype(MXU), logf)
    bend_t = _dot3(pick_end.astype(MXU), b)
    bend_flat = jnp.concatenate([b[end0 + i:end0 + i + 1] for i in range(NB)], axis=1)
    eb = jnp.exp(b)
    enb = jnp.exp(-b)
    ee = jnp.exp(bend_t - b)
    qd, kd, ke = q * eb, kk * enb, kk * ee
    att = jnp.where(causal, _dot(qd, kd, 1, 1), 0.0)
    decay = jnp.exp(bend_flat)
    return dict(same=same, causal=causal, anti=anti, logf=logf, kk=kk, b=b, eb=eb, enb=enb, ee=ee, qd=qd, kd=kd, ke=ke, att=att,
                decay=decay, qde=_hg_expand(qd), kee=_hg_expand(ke))


def _hg_chunk_order(cfg, r):
    nch, ncc = r // CHUNK_ROWS, cfg["rc"] // CHUNK_ROWS
    return nch, ncc


def hg_scan_fwd(cfg, z, lb, *, d_dir, name):
    r = z.shape[0]
    d = z.shape[1] // N_PROJ
    nh = d // HEAD
    rev = d_dir == 1
    nch, ncc = _hg_chunk_order(cfg, r)
    n = CHUNK_ROWS

    def body(q_ref, v_ref, x_ref, lb_ref, o_ref, sin_ref, stk):
        @pl.when(pl.program_id(0) == 0)
        def _():
            stk[...] = jnp.zeros_like(stk)

        masks = _hg_masks(rev)
        for h in range(nh):
            sl = slice(h * HEAD, (h + 1) * HEAD)
            s0 = stk[h]
            sin_ref[0, h] = s0
            v = v_ref[:, sl]
            c = _hg_chunk(q_ref[:, sl], v, x_ref[:, sl], lb_ref[:, sl], masks)
            o_ref[:, sl] = _dot(c["att"], v) + _dot(c["qde"], s0, 1, 1)
            stk[h] = s0 * c["decay"] + _dot(v, c["kee"], 0, 0)

    def cidx(k):
        return _chunk_order(k, ncc, nch, rev)

    blk = lambda p: pl.BlockSpec((n, d), lambda k: (cidx(k), p))
    return pl.pallas_call(
        body, name=name, grid=(nch,),
        in_specs=[blk(0), blk(1), blk(2 + d_dir), pl.BlockSpec((1, d), lambda k: (0, 0))],
        out_specs=(blk(0), pl.BlockSpec((1, nh, HEAD, NB * HEAD), lambda k: (cidx(k), 0, 0, 0))),
        out_shape=(jax.ShapeDtypeStruct((r, d), F32), jax.ShapeDtypeStruct((nch, nh, HEAD, NB * HEAD), F32)),
        scratch_shapes=[pltpu.VMEM((nh, HEAD, NB * HEAD), F32)], compiler_params=_cp(("arbitrary",)))(z, z, z, lb)


def hg_scan_bwd(cfg, do, z, lb, sin, dq_in, dv_in, *, d_dir, name):
    r = z.shape[0]
    d = z.shape[1] // N_PROJ
    nh = d // HEAD
    rev = d_dir == 1
    nch, ncc = _hg_chunk_order(cfg, r)
    n = CHUNK_ROWS
    has_in = dq_in is not None

    def body(*refs):
        if has_in:
            do_ref, q_ref, v_ref, x_ref, lb_ref, sin_ref, dqi_ref, dvi_ref, dq_ref, dv_ref, dx_ref, dlb_ref, dstk = refs
        else:
            do_ref, q_ref, v_ref, x_ref, lb_ref, sin_ref, dq_ref, dv_ref, dx_ref, dlb_ref, dstk = refs
        @pl.when(pl.program_id(0) == 0)
        def _():
            dstk[...] = jnp.zeros_like(dstk)
            dlb_ref[...] = jnp.zeros_like(dlb_ref)

        masks = _hg_masks(rev)
        ex = lax.broadcasted_iota(jnp.int32, (n, HEAD), 0) % NB
        for h in range(nh):
            sl = slice(h * HEAD, (h + 1) * HEAD)
            do_, q, v, x, lb_, s0, ds1 = do_ref[:, sl], q_ref[:, sl], v_ref[:, sl], x_ref[:, sl], lb_ref[:, sl], sin_ref[0, h], dstk[h]
            c = _hg_chunk(q, v, x, lb_, masks)
            datt = jnp.where(c["causal"], _dot(do_, v, 1, 1), 0.0)
            dv = _dot(c["att"], do_, 0, 0) + _dot(c["kee"], ds1, 1, 1)
            dqd = _dot(datt, c["kd"]) + _hg_fold(_dot(do_, s0))
            dkd = _dot(datt, c["qd"], 0, 0)
            dke = _hg_fold(_dot(v, ds1))
            dbend_flat = jnp.sum(ds1 * s0, axis=0, keepdims=True) * c["decay"]
            dstk[h] = _dot(do_, c["qde"], 0, 0) + ds1 * c["decay"]
            dq = dqd * c["eb"]
            dk = dkd * c["enb"] + dke * c["ee"]
            db = dqd * c["qd"] - dkd * c["kd"] - dke * c["ke"]
            dbend_rows = jnp.zeros((n, HEAD), F32)
            for b in range(NB):
                dbend_rows = dbend_rows + jnp.where(ex == b, dbend_flat[:, b * HEAD:(b + 1) * HEAD], 0.0)
            dlogf = _dot3(c["anti"].astype(MXU), db) + _dot3(c["same"].astype(MXU), dke * c["ke"]) + dbend_rows
            _, vjp = jax.vjp(_hg_gates, x, lb_)
            dx, dlb = vjp((dlogf, dk))
            if has_in:
                dq = dq + dqi_ref[:, sl]
                dv = dv + dvi_ref[:, sl]
            dq_ref[:, sl] = dq
            dv_ref[:, sl] = dv
            dx_ref[:, sl] = dx
            dlb_ref[:, sl] += dlb

    def cidx(k):
        return _chunk_order(nch - 1 - k, ncc, nch, rev)

    blk = lambda p: pl.BlockSpec((n, d), lambda k: (cidx(k), p))
    vec = pl.BlockSpec((1, d), lambda k: (0, 0))
    in_specs = [blk(0), blk(0), blk(1), blk(2 + d_dir), vec, pl.BlockSpec((1, nh, HEAD, NB * HEAD), lambda k: (cidx(k), 0, 0, 0))]
    args = [do, z, z, z, lb, sin]
    if has_in:
        in_specs += [blk(0), blk(0)]
        args += [dq_in, dv_in]
    rd = jax.ShapeDtypeStruct((r, d), F32)
    return pl.pallas_call(
        body, name=name, grid=(nch,), in_specs=in_specs, out_specs=(blk(0), blk(0), blk(0), vec),
        out_shape=(rd, rd, rd, jax.ShapeDtypeStruct((1, d), F32)),
        scratch_shapes=[pltpu.VMEM((nh, HEAD, NB * HEAD), F32)], compiler_params=_cp(("arbitrary",)))(*args)


def _hg_read(o, g, gw):
    on = o * lax.rsqrt(jnp.mean(o * o, axis=-1, keepdims=True) + NORM_EPS) * gw
    return on * jax.nn.sigmoid(g)


def hg_read_fwd(of, ob, z, gw, *, name):
    r, d = of.shape
    nh = d // HEAD
    tm = _row_tile(r)

    def body(of_ref, ob_ref, g_ref, gw_ref, o_ref):
        o_ref[...] = _hg_read(of_ref[...] + ob_ref[...], g_ref[...], gw_ref[...]).astype(MXU)

    blk = pl.BlockSpec((tm, HEAD), lambda i, h: (i, h))
    return pl.pallas_call(
        body, name=name, grid=(r // tm, nh),
        in_specs=[blk, blk, pl.BlockSpec((tm, HEAD), lambda i, h: (i, (N_PROJ - 1) * nh + h)), pl.BlockSpec((1, HEAD), lambda i, h: (0, 0))],
        out_specs=blk, out_shape=jax.ShapeDtypeStruct((r, d), MXU), compiler_params=_cp(("parallel", "parallel")))(of, ob, z, gw)


def hg_read_bwd(don, of, ob, z, gw, *, name):
    r, d = of.shape
    nh = d // HEAD
    tm = _row_tile(r)

    def body(don_ref, of_ref, ob_ref, g_ref, gw_ref, do_ref, dg_ref, dgw_ref):
        @pl.when((pl.program_id(0) == 0) & (pl.program_id(1) == 0))
        def _():
            dgw_ref[...] = jnp.zeros_like(dgw_ref)

        _, vjp = jax.vjp(_hg_read, of_ref[...] + ob_ref[...], g_ref[...], gw_ref[...])
        do_ref[...], dg_ref[...], dgw = vjp(don_ref[...])
        dgw_ref[...] += dgw

    blk = pl.BlockSpec((tm, HEAD), lambda i, h: (i, h))
    vec = pl.BlockSpec((1, HEAD), lambda i, h: (0, 0))
    rd = jax.ShapeDtypeStruct((r, d), F32)
    return pl.pallas_call(
        body, name=name, grid=(r // tm, nh),
        in_specs=[blk, blk, blk, pl.BlockSpec((tm, HEAD), lambda i, h: (i, (N_PROJ - 1) * nh + h)), vec],
        out_specs=(blk, blk, vec), out_shape=(rd, rd, jax.ShapeDtypeStruct((1, HEAD), F32)),
        compiler_params=_cp(("arbitrary", "arbitrary")))(don, of, ob, z, gw)


FFN_COLS = 256


def _seg_masks(cfg, tr, i):
    t = lax.broadcasted_iota(jnp.int32, (tr, FFN_COLS), 0) // NB
    ctx_steps = cfg["rc"] // NB
    pos = jnp.where(i == 0, t % ctx_steps, t % GRID_W)
    last = jnp.where(i == 0, ctx_steps - 1, GRID_W - 1)
    return pos == 0, pos == last


def _prev(x, start):
    return jnp.where(start, 0.0, pltpu.roll(x, NB, 0))


def _next(x, end):
    return jnp.where(end, 0.0, pltpu.roll(x, x.shape[0] - NB, 0))


def _conv3(u, w, b, start, end):
    return ((b + _prev(u, start) * w[0:1]) + u * w[1:2]) + _next(u, end) * w[2:3]


def ffn_mid_fwd(cfg, u, cw, cb, *, name):
    r, f2 = u.shape
    f = f2 // 2
    tr = cfg["rc"]
    nf = f // FFN_COLS

    def body(ua_ref, ug_ref, wa_ref, wg_ref, ba_ref, bg_ref, o_ref):
        start, end = _seg_masks(cfg, tr, pl.program_id(0))
        a = _conv3(ua_ref[...], wa_ref[...], ba_ref[...], start, end)
        g = _conv3(ug_ref[...], wg_ref[...], bg_ref[...], start, end)
        o_ref[...] = (_silu(a) * g).astype(MXU)

    ca = lambda rows: pl.BlockSpec((rows, FFN_COLS), lambda i, j: (i if rows == tr else 0, j))
    cg = lambda rows: pl.BlockSpec((rows, FFN_COLS), lambda i, j: (i if rows == tr else 0, j + nf))
    return pl.pallas_call(
        body, name=name, grid=(r // tr, nf), in_specs=[ca(tr), cg(tr), ca(3), cg(3), ca(1), cg(1)], out_specs=ca(tr),
        out_shape=jax.ShapeDtypeStruct((r, f), MXU), compiler_params=_cp(("parallel", "parallel")))(u, u, cw, cw, cb, cb)


def ffn_mid_bwd(cfg, dact, u, cw, cb, *, name):
    r, f2 = u.shape
    f = f2 // 2
    tr = cfg["rc"]
    nf = f // FFN_COLS

    def body(da_ref, us_ref, up_ref, ws_ref, wp_ref, bs_ref, bp_ref, du_ref, dcw_ref, dcb_ref):
        i = pl.program_id(1)
        is_a = pl.program_id(0) < nf
        start, end = _seg_masks(cfg, tr, i)
        us, ws = us_ref[...], ws_ref[...]
        cs = _conv3(us, ws, bs_ref[...], start, end)
        cp = _conv3(up_ref[...], wp_ref[...], bp_ref[...], start, end)
        dact_v = da_ref[...]
        sg = jax.nn.sigmoid(cs)
        d_if_a = dact_v * cp * (sg * (1.0 + cs * (1.0 - sg)))
        d_if_g = dact_v * _silu(cp)
        dc = jnp.where(is_a, d_if_a, d_if_g)
        du_ref[...] = (ws[1:2] * dc + ws[0:1] * _next(dc, end) + ws[2:3] * _prev(dc, start)).astype(MXU)

        @pl.when(i == 0)
        def _():
            dcw_ref[...] = jnp.zeros_like(dcw_ref)
            dcb_ref[...] = jnp.zeros_like(dcb_ref)

        dcw_ref[...] += jnp.concatenate([jnp.sum(dc * _prev(us, start), axis=0, keepdims=True), jnp.sum(dc * us, axis=0, keepdims=True),
                                         jnp.sum(dc * _next(us, end), axis=0, keepdims=True)], axis=0)
        dcb_ref[...] += jnp.sum(dc, axis=0, keepdims=True)

    cs_ = lambda rows: pl.BlockSpec((rows, FFN_COLS), lambda j, i: (i if rows == tr else 0, j))
    cp_ = lambda rows: pl.BlockSpec((rows, FFN_COLS), lambda j, i: (i if rows == tr else 0, (j + nf) % (2 * nf)))
    return pl.pallas_call(
        body, name=name, grid=(2 * nf, r // tr),
        in_specs=[pl.BlockSpec((tr, FFN_COLS), lambda j, i: (i, j % nf)), cs_(tr), cp_(tr), cs_(3), cp_(3), cs_(1), cp_(1)],
        out_specs=(cs_(tr), cs_(3), cs_(1)),
        out_shape=(jax.ShapeDtypeStruct((r, f2), MXU), jax.ShapeDtypeStruct((3, f2), F32), jax.ShapeDtypeStruct((1, f2), F32)),
        compiler_params=_cp(("parallel", "arbitrary")))(dact, u, u, cw, cw, cb, cb)


def hg_lb_fwd(e0, e1, *, name):
    def body(a, b, o):
        o[...] = _hg_lower_bound(a[...], b[...])

    return pl.pallas_call(body, name=name, out_shape=jax.ShapeDtypeStruct(e0.shape, F32))(e0, e1)


def hg_lb_bwd(e0, e1, dlb, *, name):
    def body(a, b, g, oa, ob):
        _, vjp = jax.vjp(_hg_lower_bound, a[...], b[...])
        oa[...], ob[...] = vjp(g[...])

    s = jax.ShapeDtypeStruct(e0.shape, F32)
    return pl.pallas_call(body, name=name, out_shape=(s, s))(e0, e1, dlb)


def _adamw(w, g, m, v):
    m = ADAM_B1 * m + (1.0 - ADAM_B1) * g
    v = ADAM_B2 * v + (1.0 - ADAM_B2) * jnp.square(g)
    m_hat = m / (1.0 - ADAM_B1 ** ADAM_STEP)
    v_hat = v / (1.0 - ADAM_B2 ** ADAM_STEP)
    delta = -ADAM_LR * (m_hat / (jnp.sqrt(v_hat) + ADAM_EPS) + ADAM_WD * w)
    return delta, m, v


def _as2d(a):
    if a.ndim >= 2 and a.shape[-1] % 128 == 0:
        return a.reshape(-1, a.shape[-1])
    return a.reshape(-1, 128) if a.size % 128 == 0 else a.reshape(1, -1)


def adamw(w, g, m, v, *, name):
    w2 = _as2d(w)
    outs = rowmap(_adamw, [w2, _as2d(g), _as2d(m), _as2d(v)], [], [(w2.shape[1], F32)] * 3, name=name)
    return tuple(o.reshape(w.shape) for o in outs)


HBM_SPEC = pl.BlockSpec(memory_space=pltpu.HBM)


def _place():
    mx, my, mc = lax.axis_index("x"), lax.axis_index("y"), lax.axis_index("c")
    others = [(1 - mx, my), (mx, 1 - my), (1 - mx, 1 - my)]
    return mx, my, mc, others


def chip_allgather(x, *, name):
    def body(x_ref, o_ref, send_sems, recv_sems, local_sem):
        mx, my, mc, others = _place()
        me = 2 * mx + my
        mine = pltpu.make_async_copy(x_ref, o_ref.at[me], local_sem)
        mine.start()
        sends = [pltpu.make_async_remote_copy(src_ref=x_ref, dst_ref=o_ref.at[me], send_sem=send_sems.at[j], recv_sem=recv_sems.at[j],
                                              device_id=(px, py, mc), device_id_type=MESH) for j, (px, py) in enumerate(others)]
        for cp in sends:
            cp.start()
        for j, (px, py) in enumerate(others):
            pltpu.make_async_remote_copy(src_ref=x_ref, dst_ref=o_ref.at[2 * px + py], send_sem=send_sems.at[j], recv_sem=recv_sems.at[j],
                                         device_id=(px, py, mc), device_id_type=MESH).wait_recv()
        for cp in sends:
            cp.wait_send()
        mine.wait()

    return pl.pallas_call(
        body, name=name, out_shape=jax.ShapeDtypeStruct((4,) + x.shape, x.dtype), in_specs=[HBM_SPEC], out_specs=HBM_SPEC,
        scratch_shapes=[pltpu.SemaphoreType.DMA((3,)), pltpu.SemaphoreType.DMA((3,)), pltpu.SemaphoreType.DMA])(x)


def _win(ref, axis, start, size):
    idx = [slice(None)] * len(ref.shape)
    idx[axis] = pl.ds(start, size)
    return ref.at[tuple(idx)]


def _half_axis(shape, ax):
    if shape[0] == 2:
        return 0
    return 2 if ax == 1 else 1


def _cut(shape, axis, parts):
    return shape[:axis] + (shape[axis] // parts,) + shape[axis + 1:]


def _hbm_call(body, arrays, out_shapes, sems, name):
    n_in = len(arrays)
    return pl.pallas_call(body, name=name, out_shape=tuple(out_shapes), in_specs=[HBM_SPEC] * n_in, out_specs=tuple([HBM_SPEC] * len(out_shapes)),
                          scratch_shapes=sems)(*arrays)


def place_shard(shard, ax, chip, dtype, *, name):
    l, r, c = shard.shape
    tr = _row_tile(r, c)
    per_block = (l, r // tr, 1)[ax]

    def omap(li, ri, cref):
        idx = [li, ri, 0]
        idx[ax] = idx[ax] + cref[0] * per_block
        return tuple(idx)

    def body(c_ref, s_ref, o_ref):
        o_ref[...] = s_ref[...].astype(dtype)

    full = shard.shape[:ax] + (4 * shard.shape[ax],) + shard.shape[ax + 1:]
    return pl.pallas_call(
        body, name=name, out_shape=jax.ShapeDtypeStruct(full, dtype),
        grid_spec=pltpu.PrefetchScalarGridSpec(
            num_scalar_prefetch=1, grid=(l, r // tr),
            in_specs=[pl.BlockSpec((1, tr, c), lambda li, ri, cref: (li, ri, 0))], out_specs=pl.BlockSpec((1, tr, c), omap)),
        compiler_params=_cp(("parallel", "parallel")))(chip, shard)


def gather_placed(arrays, axes, haxes, *, name):
    n = len(arrays)

    def body(*refs):
        ins, outs = refs[:n], refs[n:2 * n]
        send_sems, recv_sems = refs[2 * n:]
        mx, my, mc, others = _place()
        me = 2 * mx + my

        def part(ref, i, chip):
            sz, hs = arrays[i].shape[axes[i]] // 4, arrays[i].shape[haxes[i]] // 2
            return _win(_win(ref, axes[i], chip * sz, sz), haxes[i], mc * hs, hs)

        sends = []
        for i in range(n):
            for j, (px, py) in enumerate(others):
                rc = pltpu.make_async_remote_copy(src_ref=part(ins[i], i, me), dst_ref=part(outs[i], i, me), send_sem=send_sems.at[i, j],
                                                  recv_sem=recv_sems.at[i, j], device_id=(px, py, mc), device_id_type=MESH)
                rc.start()
                sends.append(rc)
        for i in range(n):
            for j, (px, py) in enumerate(others):
                pltpu.make_async_remote_copy(src_ref=part(ins[i], i, me), dst_ref=part(outs[i], i, 2 * px + py), send_sem=send_sems.at[i, j],
                                             recv_sem=recv_sems.at[i, j], device_id=(px, py, mc), device_id_type=MESH).wait_recv()
        for rc in sends:
            rc.wait_send()

    return pl.pallas_call(
        body, name=name, out_shape=tuple(jax.ShapeDtypeStruct(a_.shape, a_.dtype) for a_ in arrays), in_specs=[HBM_SPEC] * n,
        out_specs=tuple([HBM_SPEC] * n), input_output_aliases={i: i for i in range(n)},
        scratch_shapes=[pltpu.SemaphoreType.DMA((n, 3)), pltpu.SemaphoreType.DMA((n, 3))])(*arrays)


def pair_swap_halves(arrays, haxes, *, name):
    n = len(arrays)

    def body(*refs):
        ins, outs = refs[:n], refs[n:2 * n]
        send_sems, recv_sems = refs[2 * n:]
        mx, my, mc, _ = _place()
        cps = []
        for i in range(n):
            hs = arrays[i].shape[haxes[i]] // 2
            cp = pltpu.make_async_remote_copy(src_ref=_win(ins[i], haxes[i], (1 - mc) * hs, hs), dst_ref=outs[i], send_sem=send_sems.at[i],
                                              recv_sem=recv_sems.at[i], device_id=(mx, my, 1 - mc), device_id_type=MESH)
            cp.start()
            cps.append(cp)
        for cp in cps:
            cp.wait()

    outs = [jax.ShapeDtypeStruct(_cut(a_.shape, h_, 2), a_.dtype) for a_, h_ in zip(arrays, haxes)]
    return _hbm_call(body, arrays, outs, [pltpu.SemaphoreType.DMA((n,)), pltpu.SemaphoreType.DMA((n,))], name)


def add_own_half(g, t, hax, core, *, out_dtype, name):
    l, r, c = t.shape
    tr = _row_tile(r, c)
    per_half = (l, r // tr, 1)[hax]

    def imap(li, ri, cref):
        idx = [li, ri, 0]
        idx[hax] = idx[hax] + cref[0] * per_half
        return tuple(idx)

    def body(c_ref, g_ref, t_ref, o_ref):
        o_ref[...] = (g_ref[...] + t_ref[...]).astype(out_dtype)

    return pl.pallas_call(
        body, name=name, out_shape=jax.ShapeDtypeStruct(t.shape, out_dtype),
        grid_spec=pltpu.PrefetchScalarGridSpec(
            num_scalar_prefetch=1, grid=(l, r // tr),
            in_specs=[pl.BlockSpec((1, tr, c), imap), pl.BlockSpec((1, tr, c), lambda li, ri, cref: (li, ri, 0))],
            out_specs=pl.BlockSpec((1, tr, c), lambda li, ri, cref: (li, ri, 0))),
        compiler_params=_cp(("parallel", "parallel")))(core, g, t)


def exchange_blocks(arrays, axes, *, name):
    n = len(arrays)

    def body(*refs):
        ins, outs = refs[:n], refs[n:2 * n]
        send_sems, recv_sems, local_sems = refs[2 * n:]
        mx, my, mc, others = _place()
        me = 2 * mx + my
        waits = []
        for i in range(n):
            sz = arrays[i].shape[axes[i]] // 4
            cp = pltpu.make_async_copy(_win(ins[i], axes[i], me * sz, sz), outs[i].at[me], local_sems.at[i])
            cp.start()
            waits.append(cp.wait)
            for j, (px, py) in enumerate(others):
                rc = pltpu.make_async_remote_copy(src_ref=_win(ins[i], axes[i], (2 * px + py) * sz, sz), dst_ref=outs[i].at[me],
                                                  send_sem=send_sems.at[i, j], recv_sem=recv_sems.at[i, j], device_id=(px, py, mc),
                                                  device_id_type=MESH)
                rc.start()
                waits.append(rc.wait_send)
        for i in range(n):
            sz = arrays[i].shape[axes[i]] // 4
            for j, (px, py) in enumerate(others):
                pltpu.make_async_remote_copy(src_ref=_win(ins[i], axes[i], me * sz, sz), dst_ref=outs[i].at[2 * px + py],
                                             send_sem=send_sems.at[i, j], recv_sem=recv_sems.at[i, j], device_id=(px, py, mc),
                                             device_id_type=MESH).wait_recv()
        for w_ in waits:
            w_()

    outs = [jax.ShapeDtypeStruct((4,) + _cut(a_.shape, ax, 4), a_.dtype) for a_, ax in zip(arrays, axes)]
    return _hbm_call(body, arrays, outs, [pltpu.SemaphoreType.DMA((n, 3)), pltpu.SemaphoreType.DMA((n, 3)), pltpu.SemaphoreType.DMA((n,))], name)


def sum_blocks(e, hax, core, *, name):
    _, l, r, c = e.shape
    tr = _row_tile(r, c)
    per_half = (l, r // tr, 1)[hax]

    def omap(li, ri, cref):
        idx = [li, ri, 0]
        idx[hax] = idx[hax] + cref[0] * per_half
        return tuple(idx)

    def body(c_ref, e_ref, o_ref):
        v = e_ref[...].astype(F32)
        o_ref[...] = ((v[0] + v[1]) + v[2]) + v[3]

    full = (l, r, c)[:hax] + (2 * (l, r, c)[hax],) + (l, r, c)[hax + 1:]
    return pl.pallas_call(
        body, name=name, out_shape=jax.ShapeDtypeStruct(full, F32),
        grid_spec=pltpu.PrefetchScalarGridSpec(
            num_scalar_prefetch=1, grid=(l, r // tr),
            in_specs=[pl.BlockSpec((4, 1, tr, c), lambda li, ri, cref: (0, li, ri, 0))], out_specs=pl.BlockSpec((1, tr, c), omap)),
        compiler_params=_cp(("parallel", "parallel")))(core, e)


def pair_fill_halves(arrays, haxes, *, name):
    n = len(arrays)

    def body(*refs):
        ins, outs = refs[:n], refs[n:2 * n]
        send_sems, recv_sems = refs[2 * n:]
        mx, my, mc, _ = _place()
        cps = []
        for i in range(n):
            hs = arrays[i].shape[haxes[i]] // 2
            mine = _win(ins[i], haxes[i], mc * hs, hs)
            cp = pltpu.make_async_remote_copy(src_ref=mine, dst_ref=_win(outs[i], haxes[i], mc * hs, hs), send_sem=send_sems.at[i],
                                              recv_sem=recv_sems.at[i], device_id=(mx, my, 1 - mc), device_id_type=MESH)
            cp.start()
            cps.append(cp)
        for i in range(n):
            hs = arrays[i].shape[haxes[i]] // 2
            pltpu.make_async_remote_copy(src_ref=_win(ins[i], haxes[i], mc * hs, hs), dst_ref=_win(outs[i], haxes[i], (1 - mc) * hs, hs),
                                         send_sem=send_sems.at[i], recv_sem=recv_sems.at[i], device_id=(mx, my, 1 - mc),
                                         device_id_type=MESH).wait_recv()
        for cp in cps:
            cp.wait_send()

    return pl.pallas_call(
        body, name=name, out_shape=tuple(jax.ShapeDtypeStruct(a_.shape, a_.dtype) for a_ in arrays), in_specs=[HBM_SPEC] * n,
        out_specs=tuple([HBM_SPEC] * n), input_output_aliases={i: i for i in range(n)},
        scratch_shapes=[pltpu.SemaphoreType.DMA((n,)), pltpu.SemaphoreType.DMA((n,))])(*arrays)


WEIGHTS = ['c_ctx', 'w_mod', 'b_mod', 'norm1_w', 'norm2_w', 'final_norm_w', 's5_w_in', 's5_lam_re', 's5_lam_im', 's5_log_step', 's5_b_re', 's5_b_im', 's5_c_re', 's5_c_im', 's5_d', 's5_w_glu', 's5_w_out', 'hg_w_in', 'hg_lower_bounds', 'hg_gnorm_w', 'hg_w_out', 'ffn_w_up', 'ffn_conv_w', 'ffn_conv_b', 'ffn_w_down']
INPUTS = ['x', 'c', 'ctx', 'c_ctx', 'w_mod', 'b_mod', 'norm1_w', 'norm2_w', 'final_norm_w', 's5_w_in', 's5_lam_re', 's5_lam_im', 's5_log_step', 's5_b_re', 's5_b_im', 's5_c_re', 's5_c_im', 's5_d', 's5_w_glu', 's5_w_out', 'hg_w_in', 'hg_lower_bounds', 'hg_gnorm_w', 'hg_w_out', 'ffn_w_up', 'ffn_conv_w', 'ffn_conv_b', 'ffn_w_down', 'loss_target', 'm_c_ctx', 'm_w_mod', 'm_b_mod', 'm_norm1_w', 'm_norm2_w', 'm_final_norm_w', 'm_s5_w_in', 'm_s5_lam_re', 'm_s5_lam_im', 'm_s5_log_step', 'm_s5_b_re', 'm_s5_b_im', 'm_s5_c_re', 'm_s5_c_im', 'm_s5_d', 'm_s5_w_glu', 'm_s5_w_out', 'm_hg_w_in', 'm_hg_lower_bounds', 'm_hg_gnorm_w', 'm_hg_w_out', 'm_ffn_w_up', 'm_ffn_conv_w', 'm_ffn_conv_b', 'm_ffn_w_down', 'v_c_ctx', 'v_w_mod', 'v_b_mod', 'v_norm1_w', 'v_norm2_w', 'v_final_norm_w', 'v_s5_w_in', 'v_s5_lam_re', 'v_s5_lam_im', 'v_s5_log_step', 'v_s5_b_re', 'v_s5_b_im', 'v_s5_c_re', 'v_s5_c_im', 'v_s5_d', 'v_s5_w_glu', 'v_s5_w_out', 'v_hg_w_in', 'v_hg_lower_bounds', 'v_hg_gnorm_w', 'v_hg_w_out', 'v_ffn_w_up', 'v_ffn_conv_w', 'v_ffn_conv_b', 'v_ffn_w_down']
SHARD_AXIS = {"w_mod": 2, "s5_w_in": 1, "s5_w_glu": 1, "s5_w_out": 1, "hg_w_in": 2, "hg_lower_bounds": 2, "hg_w_out": 1,
              "ffn_w_up": 2, "ffn_conv_w": 2, "ffn_w_down": 1}
GATHER_F32 = ("hg_lower_bounds", "ffn_conv_w")
PACK_W = 1024
GRAD_WIRE = jnp.bfloat16


def _gather_weights(a, names, chip):
    axes = [SHARD_AXIS[n] for n in names]
    placed = [place_shard(a[n], ax, chip, F32 if n in GATHER_F32 else MXU, name="place_" + n) for n, ax in zip(names, axes)]
    haxes = [_half_axis(p_.shape, ax) for p_, ax in zip(placed, axes)]
    got = gather_placed(placed, axes, haxes, name="allgather_weights")
    return dict(zip(names, pair_fill_halves(got, haxes, name="allgather_pair_fill")))


def _reduce_grads(a, grads, core):
    sharded = [n for n in WEIGHTS if n in SHARD_AXIS]
    small = [n for n in WEIGHTS if n not in SHARD_AXIS]
    flat = jnp.concatenate([grads[n].reshape(-1) for n in small])
    pad = (-flat.shape[0]) % (64 * PACK_W)
    small_pack = jnp.pad(flat, (0, pad)).reshape(1, -1, PACK_W)
    arrays = [grads[n] for n in sharded] + [small_pack]
    axes = [SHARD_AXIS[n] for n in sharded] + [1]
    haxes = [_half_axis(g_.shape, ax) for g_, ax in zip(arrays, axes)]
    tags = sharded + ["small"]
    t = pair_swap_halves(arrays, haxes, name="grad_pair_swap")
    h = [add_own_half(g_, t_, hx, core, out_dtype=GRAD_WIRE, name="grad_pair_add_" + tg) for g_, t_, hx, tg in zip(arrays, t, haxes, tags)]
    e = exchange_blocks(h, axes, name="grad_chip_exchange")
    s = [sum_blocks(e_, hx, core, name="grad_chip_sum_" + tg) for e_, hx, tg in zip(e, haxes, tags)]
    red = pair_fill_halves(s, haxes, name="grad_pair_fill")
    out = dict(zip(sharded, red[:-1]))
    sm = chip_allgather(red[-1][0], name="allgather_small_grads").reshape(-1)
    off = 0
    for n in small:
        out[n] = sm[off:off + math.prod(a[n].shape)].reshape(a[n].shape)
        off += math.prod(a[n].shape)
    return out


def _blockdiag_b(bb, kb):
    gl = S5_KIN // S5_GROUP
    x = bb.reshape(kb, gl, S5_GROUP, S5_STATE)
    return (x[:, :, :, None, :] * jnp.eye(gl, dtype=bb.dtype)[None, :, None, :, None]).reshape(kb, S5_KIN, S5_KST)


def _blockdiag_c(cc, kb):
    gl = S5_KIN // S5_GROUP
    x = cc.reshape(kb, gl, S5_GROUP, S5_STATE).transpose(0, 1, 3, 2)
    return (x[:, :, :, None, :] * jnp.eye(gl, dtype=cc.dtype)[None, :, None, :, None]).reshape(kb, S5_KST, S5_KIN)


def _diag_b(m, kb):
    gl = S5_KIN // S5_GROUP
    x = m.reshape(kb, gl, S5_GROUP, gl, S5_STATE)
    return jnp.stack([x[:, i, :, i, :] for i in range(gl)], axis=1).reshape(kb * gl, S5_GROUP, S5_STATE)


def _diag_c(m, kb):
    gl = S5_KIN // S5_GROUP
    x = m.reshape(kb, gl, S5_STATE, gl, S5_GROUP)
    return jnp.stack([x[:, i, :, i, :] for i in range(gl)], axis=1).transpose(0, 1, 3, 2).reshape(kb * gl, S5_GROUP, S5_STATE)


def kernel(x, c, ctx, c_ctx, w_mod, b_mod, norm1_w, norm2_w, final_norm_w, s5_w_in, s5_lam_re, s5_lam_im, s5_log_step, s5_b_re, s5_b_im, s5_c_re, s5_c_im, s5_d, s5_w_glu, s5_w_out, hg_w_in, hg_lower_bounds, hg_gnorm_w, hg_w_out, ffn_w_up, ffn_conv_w, ffn_conv_b, ffn_w_down, loss_target, m_c_ctx, m_w_mod, m_b_mod, m_norm1_w, m_norm2_w, m_final_norm_w, m_s5_w_in, m_s5_lam_re, m_s5_lam_im, m_s5_log_step, m_s5_b_re, m_s5_b_im, m_s5_c_re, m_s5_c_im, m_s5_d, m_s5_w_glu, m_s5_w_out, m_hg_w_in, m_hg_lower_bounds, m_hg_gnorm_w, m_hg_w_out, m_ffn_w_up, m_ffn_conv_w, m_ffn_conv_b, m_ffn_w_down, v_c_ctx, v_w_mod, v_b_mod, v_norm1_w, v_norm2_w, v_final_norm_w, v_s5_w_in, v_s5_lam_re, v_s5_lam_im, v_s5_log_step, v_s5_b_re, v_s5_b_im, v_s5_c_re, v_s5_c_im, v_s5_d, v_s5_w_glu, v_s5_w_out, v_hg_w_in, v_hg_lower_bounds, v_hg_gnorm_w, v_hg_w_out, v_ffn_w_up, v_ffn_conv_w, v_ffn_conv_b, v_ffn_w_down):
    a = dict(zip(INPUTS, (x, c, ctx, c_ctx, w_mod, b_mod, norm1_w, norm2_w, final_norm_w, s5_w_in, s5_lam_re, s5_lam_im, s5_log_step, s5_b_re, s5_b_im, s5_c_re, s5_c_im, s5_d, s5_w_glu, s5_w_out, hg_w_in, hg_lower_bounds, hg_gnorm_w, hg_w_out, ffn_w_up, ffn_conv_w, ffn_conv_b, ffn_w_down, loss_target, m_c_ctx, m_w_mod, m_b_mod, m_norm1_w, m_norm2_w, m_final_norm_w, m_s5_w_in, m_s5_lam_re, m_s5_lam_im, m_s5_log_step, m_s5_b_re, m_s5_b_im, m_s5_c_re, m_s5_c_im, m_s5_d, m_s5_w_glu, m_s5_w_out, m_hg_w_in, m_hg_lower_bounds, m_hg_gnorm_w, m_hg_w_out, m_ffn_w_up, m_ffn_conv_w, m_ffn_conv_b, m_ffn_w_down, v_c_ctx, v_w_mod, v_b_mod, v_norm1_w, v_norm2_w, v_final_norm_w, v_s5_w_in, v_s5_lam_re, v_s5_lam_im, v_s5_log_step, v_s5_b_re, v_s5_b_im, v_s5_c_re, v_s5_c_im, v_s5_d, v_s5_w_glu, v_s5_w_out, v_hg_w_in, v_hg_lower_bounds, v_hg_gnorm_w, v_hg_w_out, v_ffn_w_up, v_ffn_conv_w, v_ffn_conv_b, v_ffn_w_down)))
    nb, seq, d = x.shape
    assert nb == NB
    rc = nb * ctx.shape[1]
    cfg = {"rc": rc}
    f = a["ffn_w_down"].shape[1] * 4
    core = lax.axis_index("c").astype(jnp.int32).reshape(1)

    w = {n: a[n] for n in WEIGHTS if n not in SHARD_AXIS}
    chip = (2 * lax.axis_index("x") + lax.axis_index("y")).astype(jnp.int32).reshape(1)
    w.update(_gather_weights(a, [n for n in WEIGHTS if n in SHARD_AXIS], chip))

    tmaj = lambda t: t.transpose(1, 0, 2).reshape(-1, t.shape[-1])
    x0 = jnp.concatenate([tmaj(ctx), tmaj(x)], axis=0)
    tgt = tmaj(a["loss_target"])
    c16 = jnp.concatenate([jnp.broadcast_to(c_ctx[None], (8, d)), c, c], axis=0)
    mt, scb = [], None
    for l in range(2):
        m_, scb = mod_fwd(c16, w["w_mod"][l], w["b_mod"][l][None], name=f"mod_fwd{l}")
        mt.append(m_)
    n1, n2 = w["norm1_w"], w["norm2_w"]

    def ffn_fwd(l, h):
        u = mm(h, w["ffn_w_up"][l], name=f"ffn_up{l}")
        act = ffn_mid_fwd(cfg, u, w["ffn_conv_w"][l], w["ffn_conv_b"][l][None], name=f"ffn_mid{l}")
        return u, act, mm(act, w["ffn_w_down"][l], name=f"ffn_down{l}")

    def ffn_bwd(l, dfo, u, act, h):
        dact = mm(dfo, w["ffn_w_down"][l], tb=True, name=f"ffn_down_dx{l}")
        dwd = mm(act, dfo, ta=True, name=f"ffn_down_dw{l}")
        du, dcw, dcb = ffn_mid_bwd(cfg, dact, u, w["ffn_conv_w"][l], w["ffn_conv_b"][l][None], name=f"ffn_mid_bwd{l}")
        dh = mm(du, w["ffn_w_up"][l], tb=True, name=f"ffn_up_dx{l}")
        dwu = mm(h, du, ta=True, name=f"ffn_up_dw{l}")
        return dh, dwu, dcw, dcb[0], dwd

    g_, p_ = d // S5_GROUP, S5_STATE
    ns, kb = g_ * p_, d // S5_KIN
    s5p = (w["s5_lam_re"][0].reshape(2 * g_, p_), w["s5_lam_im"][0].reshape(2 * g_, p_), w["s5_log_step"][0].reshape(2 * g_, 1),
           w["s5_b_re"][0].transpose(0, 1, 3, 2).reshape(2 * g_, S5_GROUP, p_), w["s5_b_im"][0].transpose(0, 1, 3, 2).reshape(2 * g_, S5_GROUP, p_))
    ar, ai, bbr, bbi = s5_disc_fwd(*s5p, name="s5_disc")
    dsk = w["s5_d"]
    _, h1 = node_fwd(cfg, x0, None, None, 0, n1[0:1], mt[0], 0, name="node0a")
    u0 = mm(h1, w["s5_w_in"][0], name="s5_in")
    s5s, ys = [], []
    for dd in range(2):
        sl = slice(dd * g_, (dd + 1) * g_)
        prm = (ar[sl].reshape(1, ns), ai[sl].reshape(1, ns), _blockdiag_b(bbr[sl], kb).astype(MXU), _blockdiag_b(bbi[sl], kb).astype(MXU),
               _blockdiag_c(w["s5_c_re"][0, dd], kb).astype(MXU), _blockdiag_c(w["s5_c_im"][0, dd], kb).astype(MXU))
        sre, sim, ere, eim, y_ = s5_scan_fwd(cfg, u0, *prm, rev=dd == 1, name=f"s5_scan{dd}")
        s5s.append((sre, sim, ere, eim) + prm)
        ys.append(y_)

    def glu_a(u, y0, y1, ds):
        yp = (ds * u + y0) + y1
        return yp, _gelu(yp)

    ypre, zgb = rowmap(glu_a, [u0, ys[0], ys[1]], [dsk], [(d, F32), (d, MXU)], name="s5_glu_a")
    tg = mm(zgb, w["s5_w_glu"][0], name="s5_glu")
    (z2,) = rowmap(lambda yp, t: _gelu(yp) * jax.nn.sigmoid(t), [ypre, tg], [], [(d, MXU)], name="s5_glu_b")
    y1a = mm(z2, w["s5_w_out"][0], name="s5_out")
    x1a, h2a = node_fwd(cfg, x0, y1a, mt[0], 2, n2[0:1], mt[0], 3, name="node0b")
    ufa, acta, foa = ffn_fwd(0, h2a)

    x2a, h1b = node_fwd(cfg, x1a, foa, mt[0], 5, n1[1:2], mt[1], 0, name="node1a")
    z = mm(h1b, w["hg_w_in"][0], name="hg_in")
    e0, e1 = w["hg_lower_bounds"][:, 0, :], w["hg_lower_bounds"][:, 1, :]
    lb = hg_lb_fwd(e0, e1, name="hg_lb")
    gw = w["hg_gnorm_w"]
    o0, sin0 = hg_scan_fwd(cfg, z, lb[0:1], d_dir=0, name="hg_scan0")
    o1, sin1 = hg_scan_fwd(cfg, z, lb[1:2], d_dir=1, name="hg_scan1")
    onb = hg_read_fwd(o0, o1, z, gw, name="hg_read")
    y1b = mm(onb, w["hg_w_out"][0], name="hg_out")
    x1b, h2b = node_fwd(cfg, x2a, y1b, mt[1], 2, n2[1:2], mt[1], 3, name="node1b")
    ufb, actb, fob = ffn_fwd(1, h2b)
    loss_p, dx2b, dfob, dg2_1, dfnw = final_node(cfg, x1b, fob, mt[1], 5, w["final_norm_w"][None], tgt, name="final_node")

    gr = {}
    dh2b, dwu1, dcw1, dcb1, dwd1 = ffn_bwd(1, dfob, ufb, actb, h2b)
    dx1b, dy1b, dn2_1, dsh2_1, dsc2_1, dg1_1 = node_bwd(cfg, dx2b, dh2b, x1b, y1b, mt[1], 2, n2[1:2], mt[1], 3, name="node1b_bwd")
    don = mm(dy1b, w["hg_w_out"][0], tb=True, name="hg_out_dx")
    gr["hg_w_out"] = mm(onb, dy1b, ta=True, name="hg_out_dw")[None]
    do_, dgate_, dgw = hg_read_bwd(don, o0, o1, z, gw, name="hg_read_bwd")
    dq, dv, dxf, dlb0 = hg_scan_bwd(cfg, do_, z, lb[0:1], sin0, None, None, d_dir=0, name="hg_scan_bwd0")
    dq, dv, dxb, dlb1 = hg_scan_bwd(cfg, do_, z, lb[1:2], sin1, dq, dv, d_dir=1, name="hg_scan_bwd1")
    dz = jnp.concatenate([t_.astype(MXU) for t_ in (dq, dv, dxf, dxb, dgate_)], axis=1)
    dh1b = mm(dz, w["hg_w_in"][0], tb=True, name="hg_in_dx")
    gr["hg_w_in"] = mm(h1b, dz, ta=True, name="hg_in_dw")[None]
    de0, de1 = hg_lb_bwd(e0, e1, jnp.concatenate([dlb0, dlb1], axis=0), name="hg_lb_bwd")
    gr["hg_lower_bounds"] = jnp.stack([de0, de1], axis=1)
    gr["hg_gnorm_w"] = dgw
    dx2a, dfoa, dn1_1, dsh1_1, dsc1_1, dg2_0 = node_bwd(cfg, dx1b, dh1b, x2a, foa, mt[0], 5, n1[1:2], mt[1], 0, name="node1a_bwd")

    dh2a, dwu0, dcw0, dcb0, dwd0 = ffn_bwd(0, dfoa, ufa, acta, h2a)
    dx1a, dy1a, dn2_0, dsh2_0, dsc2_0, dg1_0 = node_bwd(cfg, dx2a, dh2a, x1a, y1a, mt[0], 2, n2[0:1], mt[0], 3, name="node0b_bwd")
    dz2 = mm(dy1a, w["s5_w_out"][0], tb=True, name="s5_out_dx")
    gr["s5_w_out"] = mm(z2, dy1a, ta=True, name="s5_out_dw")[None]

    def glu_b_bwd(dz2_, yp, t):
        zg, sg = _gelu(yp), jax.nn.sigmoid(t)
        return dz2_ * zg * sg * (1.0 - sg), dz2_ * sg

    dtg, dzg_dir = rowmap(glu_b_bwd, [dz2, ypre, tg], [], [(d, MXU), (d, F32)], name="s5_glu_b_bwd")
    dzg_mm = mm(dtg, w["s5_w_glu"][0], tb=True, name="s5_glu_dx")
    gr["s5_w_glu"] = mm(zgb, dtg, ta=True, name="s5_glu_dw")[None]

    def glu_a_bwd(dzd, dzm, yp, u, ds):
        _, vjp = jax.vjp(_gelu, yp)
        (dy,) = vjp(dzd + dzm)
        return dy, dy * ds, jnp.sum(dy * u, axis=0, keepdims=True)

    dyb, du, ddsk = rowmap(glu_a_bwd, [dzg_dir, dzg_mm, ypre, u0], [dsk], [(d, MXU), (d, F32)], [(1, d)], name="s5_glu_a_bwd")
    gr["s5_d"] = ddsk
    dar, dai, dbr, dbi, dcr, dci = [], [], [], [], [], []
    for dd in range(2):
        sre, sim, ere, eim, a_re, a_im, bre, bim, cre, cim = s5s[dd]
        du, gre, gim, da_r, da_i = s5_scan_bwd(cfg, dyb, sre, sim, ere, eim, a_re, a_im, bre, bim, cre, cim, du, rev=dd == 1,
                                               name=f"s5_scan_bwd{dd}")
        dar.append(colsum(da_r, name=f"s5_da_re{dd}").reshape(g_, p_))
        dai.append(colsum(da_i, name=f"s5_da_im{dd}").reshape(g_, p_))
        dbr.append(_diag_b(blockdiag_tn(u0, gre, S5_KIN, S5_KST, name=f"s5_db_re{dd}"), kb))
        dbi.append(_diag_b(blockdiag_tn(u0, gim, S5_KIN, S5_KST, name=f"s5_db_im{dd}"), kb))
        dcr.append(_diag_c(blockdiag_tn(sre.reshape(-1, ns), dyb, S5_KST, S5_KIN, name=f"s5_dc_re{dd}"), kb))
        dci.append(_diag_c(blockdiag_tn(sim.reshape(-1, ns), dyb, S5_KST, S5_KIN, scale=-1.0, name=f"s5_dc_im{dd}"), kb))
    cat = lambda l_: jnp.concatenate(l_, axis=0)
    dlr, dli, dls, dbre, dbim = s5_disc_bwd(*s5p, cat(dar), cat(dai), cat(dbr), cat(dbi), name="s5_disc_bwd")
    gr["s5_lam_re"], gr["s5_lam_im"] = dlr.reshape(1, 2, g_, p_), dli.reshape(1, 2, g_, p_)
    gr["s5_log_step"] = dls.reshape(1, 2, g_)
    gr["s5_b_re"] = dbre.reshape(1, 2, g_, S5_GROUP, p_).transpose(0, 1, 2, 4, 3)
    gr["s5_b_im"] = dbim.reshape(1, 2, g_, S5_GROUP, p_).transpose(0, 1, 2, 4, 3)
    gr["s5_c_re"], gr["s5_c_im"] = jnp.stack(dcr)[None], jnp.stack(dci)[None]
    dh1 = mm(du, w["s5_w_in"][0], tb=True, name="s5_in_dx")
    gr["s5_w_in"] = mm(h1, du, ta=True, name="s5_in_dw")[None]
    dx0, _, dn1_0, dsh1_0, dsc1_0, _ = node_bwd(cfg, dx1a, dh1, x0, None, None, 0, n1[0:1], mt[0], 0, name="node0a_bwd")

    dmt = [jnp.concatenate([dsh1_0, dsc1_0, dg1_0, dsh2_0, dsc2_0, dg2_0], axis=1),
           jnp.concatenate([dsh1_1, dsc1_1, dg1_1, dsh2_1, dsc2_1, dg2_1], axis=1)]
    gr["w_mod"] = jnp.stack([mm(scb, dmt[l], ta=True, name=f"mod_dw{l}") for l in range(2)])
    gr["b_mod"] = jnp.concatenate([colsum(dmt[l], name=f"mod_db{l}") for l in range(2)], axis=0)
    dsc16 = [mm(dmt[l], w["w_mod"][l], tb=True, name=f"mod_dx{l}") for l in range(2)]
    gr["c_ctx"] = cctx_grad(c16, dsc16, name="c_ctx_grad")[0]
    gr["norm1_w"] = jnp.concatenate([dn1_0, dn1_1], axis=0)
    gr["norm2_w"] = jnp.concatenate([dn2_0, dn2_1], axis=0)
    gr["final_norm_w"] = dfnw[0]
    gr["ffn_w_up"], gr["ffn_conv_w"] = jnp.stack([dwu0, dwu1]), jnp.stack([dcw0, dcw1])
    gr["ffn_conv_b"], gr["ffn_w_down"] = jnp.stack([dcb0, dcb1]), jnp.stack([dwd0, dwd1])

    red = _reduce_grads(a, gr, core)
    loss = lax.psum(loss_p[0, 0], ("x", "y", "c"))
    grad_x = dx0[rc:].reshape(seq, nb, d).transpose(1, 0, 2)
    upd = {n: adamw(a[n], red[n], a["m_" + n], a["v_" + n], name="adamw_" + n) for n in WEIGHTS}
    return (loss, grad_x, *[red[n] for n in WEIGHTS], *[upd[n][0] for n in WEIGHTS], *[upd[n][1] for n in WEIGHTS],
            *[upd[n][2] for n in WEIGHTS])
```

```python
import functools
import math

import jax
import jax.numpy as jnp
from jax import lax
from jax.experimental import pallas as pl
from jax.experimental.pallas import tpu as pltpu

F32 = jnp.float32
BF = jnp.bfloat16
MXU = jnp.bfloat16

NORM_EPS = 1e-6
GRID_W = 64
N_MOD = 6
S5_GROUP = 16
S5_STATE = 64
S5_LAM_RE_MAX = -1e-4
S5_KIN = 256
S5_KST = S5_KIN // S5_GROUP * S5_STATE
HEAD = 128
CHUNK_ROWS = 128
N_PROJ = 5
NB = 4
ADAM_LR, ADAM_B1, ADAM_B2, ADAM_EPS, ADAM_WD, ADAM_STEP = 0.001, 0.9, 0.999, 1e-08, 0.01, 10
VMEM_LIMIT = 56 * 1024 * 1024
MESH = pl.DeviceIdType.MESH


def _tile(n, cap):
    if n <= cap:
        return n
    best = None
    for t in range(128, cap + 1, 128):
        if n % t == 0:
            best = t
    assert best is not None, (n, cap)
    return best


def _row_tile(r, width=1024):
    cap = max(8, (512 * 1024) // max(width, 1))
    return next((t for t in (512, 256, 128, 64, 32, 16, 8) if t <= cap and r % t == 0), r)


def _cp(sem):
    return pltpu.CompilerParams(dimension_semantics=sem, vmem_limit_bytes=VMEM_LIMIT)


def _dot(a, b, ca=1, cb=0):
    return lax.dot_general(a.astype(MXU), b.astype(MXU), (((ca,), (cb,)), ((), ())), preferred_element_type=F32)


def _dot3(m, x):
    hi = x.astype(MXU)
    r1 = x - hi.astype(F32)
    mid = r1.astype(MXU)
    lo = (r1 - mid.astype(F32)).astype(MXU)
    return _dot(m, hi) + _dot(m, mid) + _dot(m, lo)


def mm(a, b, *, ta=False, tb=False, out_dtype=F32, name):
    (kd, m) = a.shape if ta else a.shape[::-1]
    (n, kd2) = b.shape if tb else b.shape[::-1]
    assert kd == kd2, (a.shape, b.shape, ta, tb)
    tm, tn, tk = _tile(m, 1024), _tile(n, 1536), _tile(kd, 1024)
    nk = kd // tk

    def body(a_ref, b_ref, o_ref, acc_ref):
        k = pl.program_id(2)

        @pl.when(k == 0)
        def _():
            acc_ref[...] = jnp.zeros_like(acc_ref)

        acc_ref[...] += _dot(a_ref[...], b_ref[...], 0 if ta else 1, 1 if tb else 0)

        @pl.when(k == nk - 1)
        def _():
            o_ref[...] = acc_ref[...].astype(out_dtype)

    a_spec = pl.BlockSpec((tk, tm), lambda i, j, k: (k, i)) if ta else pl.BlockSpec((tm, tk), lambda i, j, k: (i, k))
    b_spec = pl.BlockSpec((tn, tk), lambda i, j, k: (j, k)) if tb else pl.BlockSpec((tk, tn), lambda i, j, k: (k, j))
    return pl.pallas_call(
        body, name=name, grid=(m // tm, n // tn, nk), in_specs=[a_spec, b_spec],
        out_specs=pl.BlockSpec((tm, tn), lambda i, j, k: (i, j)), out_shape=jax.ShapeDtypeStruct((m, n), out_dtype),
        scratch_shapes=[pltpu.VMEM((tm, tn), F32)], compiler_params=_cp(("parallel", "parallel", "arbitrary")))(a, b)


def blockdiag_tn(a, b, wa, wb, *, scale=1.0, name):
    rows = a.shape[0]
    kb = a.shape[1] // wa
    tr = _tile(rows, 1024)
    nr = rows // tr

    def body(a_ref, b_ref, o_ref):
        i = pl.program_id(1)

        @pl.when(i == 0)
        def _():
            o_ref[...] = jnp.zeros_like(o_ref)

        o_ref[0] += scale * _dot(a_ref[...], b_ref[...], 0, 0)

    return pl.pallas_call(
        body, name=name, grid=(kb, nr),
        in_specs=[pl.BlockSpec((tr, wa), lambda k, i: (i, k)), pl.BlockSpec((tr, wb), lambda k, i: (i, k))],
        out_specs=pl.BlockSpec((1, wa, wb), lambda k, i: (k, 0, 0)), out_shape=jax.ShapeDtypeStruct((kb, wa, wb), F32),
        compiler_params=_cp(("parallel", "arbitrary")))(a, b)


def _pat(v, p, op):
    tm, d = v.shape
    return op(v.reshape(tm // 8, 8, d), p[None]).reshape(tm, d)


def _norm_mod(x, nw, shift, scale):
    y = x * lax.rsqrt(jnp.mean(x * x, axis=-1, keepdims=True) + NORM_EPS) * nw
    return _pat(_pat(y, 1.0 + scale, jnp.multiply), shift, jnp.add)


def _mt_spec(d, nct):
    return pl.BlockSpec((8, N_MOD * d), lambda i: (jnp.where(i < nct, 0, 1), 0))


def _acc_spec(d, nct):
    return pl.BlockSpec((8, d), lambda i: (jnp.where(i < nct, 0, 1), 0))


def _rows(cfg):
    tm = min(512, cfg["rc"])
    return tm, cfg["rc"] // tm


def node_fwd(cfg, xp, y, mtg, gi, nw, mtn, si, *, name):
    r, d = xp.shape
    tm, nct = _rows(cfg)
    row = pl.BlockSpec((tm, d), lambda i: (i, 0))
    vec = pl.BlockSpec((1, d), lambda i: (0, 0))

    def body(*refs):
        if y is None:
            xp_ref, nw_ref, mtn_ref, h_ref = refs
            x = xp_ref[...]
        else:
            xp_ref, y_ref, mtg_ref, nw_ref, mtn_ref, xn_ref, h_ref = refs
            x = xp_ref[...] + _pat(y_ref[...], mtg_ref[:, gi * d:(gi + 1) * d], jnp.multiply)
            xn_ref[...] = x
        h_ref[...] = _norm_mod(x, nw_ref[...], mtn_ref[:, si * d:(si + 1) * d], mtn_ref[:, (si + 1) * d:(si + 2) * d]).astype(MXU)

    h_shape = jax.ShapeDtypeStruct((r, d), MXU)
    if y is None:
        h = pl.pallas_call(body, name=name, grid=(r // tm,), in_specs=[row, vec, _mt_spec(d, nct)], out_specs=row,
                           out_shape=h_shape, compiler_params=_cp(("parallel",)))(xp, nw, mtn)
        return xp, h
    return pl.pallas_call(body, name=name, grid=(r // tm,), in_specs=[row, row, _mt_spec(d, nct), vec, _mt_spec(d, nct)],
                          out_specs=(row, row), out_shape=(jax.ShapeDtypeStruct((r, d), F32), h_shape),
                          compiler_params=_cp(("parallel",)))(xp, y, mtg, nw, mtn)


def node_bwd(cfg, dxres, dh, xn, y, mtg, gi, nw, mtn, si, *, name):
    r, d = xn.shape
    tm, nct = _rows(cfg)
    row = pl.BlockSpec((tm, d), lambda i: (i, 0))
    vec = pl.BlockSpec((1, d), lambda i: (0, 0))
    has_y = y is not None

    def body(*refs):
        if has_y:
            dxres_ref, dh_ref, xn_ref, y_ref, mtg_ref, nw_ref, mtn_ref, dxn_ref, dy_ref, dnw_ref, dsh_ref, dsc_ref, dg_ref = refs
        else:
            dxres_ref, dh_ref, xn_ref, nw_ref, mtn_ref, dxn_ref, dnw_ref, dsh_ref, dsc_ref = refs
        i = pl.program_id(0)
        _, vjp = jax.vjp(_norm_mod, xn_ref[...], nw_ref[...], mtn_ref[:, si * d:(si + 1) * d], mtn_ref[:, (si + 1) * d:(si + 2) * d])
        dx, dnw, dsh, dsc = vjp(dh_ref[...])
        dx = dx + dxres_ref[...]
        dxn_ref[...] = dx

        @pl.when(i == 0)
        def _():
            dnw_ref[...] = jnp.zeros_like(dnw_ref)

        @pl.when((i == 0) | (i == nct))
        def _():
            dsh_ref[...] = jnp.zeros_like(dsh_ref)
            dsc_ref[...] = jnp.zeros_like(dsc_ref)
            if has_y:
                dg_ref[...] = jnp.zeros_like(dg_ref)

        dnw_ref[...] += dnw
        dsh_ref[...] += dsh
        dsc_ref[...] += dsc
        if has_y:
            dy_ref[...] = _pat(dx, mtg_ref[:, gi * d:(gi + 1) * d], jnp.multiply).astype(MXU)
            dg_ref[...] += jnp.sum((dx * y_ref[...]).reshape(tm // 8, 8, d), axis=0)

    acc = jax.ShapeDtypeStruct((16, d), F32)
    xs = jax.ShapeDtypeStruct((r, d), F32)
    if has_y:
        return pl.pallas_call(
            body, name=name, grid=(r // tm,), in_specs=[row, row, row, row, _mt_spec(d, nct), vec, _mt_spec(d, nct)],
            out_specs=(row, row, vec, _acc_spec(d, nct), _acc_spec(d, nct), _acc_spec(d, nct)),
            out_shape=(xs, jax.ShapeDtypeStruct((r, d), MXU), jax.ShapeDtypeStruct((1, d), F32), acc, acc, acc),
            compiler_params=_cp(("arbitrary",)))(dxres, dh, xn, y, mtg, nw, mtn)
    dxn, dnw, dsh, dsc = pl.pallas_call(
        body, name=name, grid=(r // tm,), in_specs=[row, row, row, vec, _mt_spec(d, nct)],
        out_specs=(row, vec, _acc_spec(d, nct), _acc_spec(d, nct)),
        out_shape=(xs, jax.ShapeDtypeStruct((1, d), F32), acc, acc), compiler_params=_cp(("arbitrary",)))(dxres, dh, xn, nw, mtn)
    return dxn, None, dnw, dsh, dsc, None


def final_node(cfg, xp, y, mtg, gi, fnw, tgt, *, name):
    r, d = xp.shape
    tm, nct = _rows(cfg)
    row = pl.BlockSpec((tm, d), lambda i: (i, 0))
    vec = pl.BlockSpec((1, d), lambda i: (0, 0))

    def norm(x, w):
        return x * lax.rsqrt(jnp.mean(x * x, axis=-1, keepdims=True) + NORM_EPS) * w

    def body(xp_ref, y_ref, mtg_ref, fnw_ref, tgt_ref, loss_ref, dx_ref, dy_ref, dg_ref, dfnw_ref):
        i = pl.program_id(0)
        g = mtg_ref[:, gi * d:(gi + 1) * d]
        x = xp_ref[...] + _pat(y_ref[...], g, jnp.multiply)
        out, vjp = jax.vjp(norm, x, fnw_ref[...])
        lat = i >= nct
        err = jnp.where(lat, out - tgt_ref[...], 0.0)
        dx, dfnw = vjp(err * (1.0 / d))

        @pl.when(i == 0)
        def _():
            loss_ref[...] = jnp.zeros_like(loss_ref)
            dfnw_ref[...] = jnp.zeros_like(dfnw_ref)

        @pl.when((i == 0) | (i == nct))
        def _():
            dg_ref[...] = jnp.zeros_like(dg_ref)

        loss_ref[...] += jnp.full(loss_ref.shape, 0.5 / d * jnp.sum(err * err), F32)
        dfnw_ref[...] += dfnw
        dx_ref[...] = dx
        dy_ref[...] = _pat(dx, g, jnp.multiply).astype(MXU)
        dg_ref[...] += jnp.sum((dx * y_ref[...]).reshape(tm // 8, 8, d), axis=0)

    return pl.pallas_call(
        body, name=name, grid=(r // tm,),
        in_specs=[row, row, _mt_spec(d, nct), vec, pl.BlockSpec((tm, d), lambda i: (jnp.maximum(i - nct, 0), 0))],
        out_specs=(pl.BlockSpec((8, 128), lambda i: (0, 0)), row, row, _acc_spec(d, nct), vec),
        out_shape=(jax.ShapeDtypeStruct((8, 128), F32), jax.ShapeDtypeStruct((r, d), F32), jax.ShapeDtypeStruct((r, d), MXU),
                   jax.ShapeDtypeStruct((16, d), F32), jax.ShapeDtypeStruct((1, d), F32)),
        compiler_params=_cp(("arbitrary",)))(xp, y, mtg, fnw, tgt)


def _silu(x):
    return x * jax.nn.sigmoid(x)


def mod_fwd(c16, w, b, *, name):
    d, n = w.shape
    tn = _tile(n, 1536)

    def body(c_ref, w_ref, b_ref, o_ref, s_ref):
        s = _silu(c_ref[...])
        s_ref[...] = s.astype(MXU)
        o_ref[...] = _dot(s, w_ref[...]) + b_ref[...]

    return pl.pallas_call(
        body, name=name, grid=(n // tn,),
        in_specs=[pl.BlockSpec((16, d), lambda j: (0, 0)), pl.BlockSpec((d, tn), lambda j: (0, j)), pl.BlockSpec((1, tn), lambda j: (0, j))],
        out_specs=(pl.BlockSpec((16, tn), lambda j: (0, j)), pl.BlockSpec((16, d), lambda j: (0, 0))),
        out_shape=(jax.ShapeDtypeStruct((16, n), F32), jax.ShapeDtypeStruct((16, d), MXU)),
        compiler_params=_cp(("arbitrary",)))(c16, w, b)


def colsum(x, *, name):
    def body(x_ref, o_ref):
        o_ref[...] = jnp.sum(x_ref[...], axis=0, keepdims=True)

    return pl.pallas_call(body, name=name, out_shape=jax.ShapeDtypeStruct((1, x.shape[1]), F32))(x)


def cctx_grad(c16, ds_list, *, name):
    def body(c_ref, *refs):
        o_ref = refs[-1]
        ds = refs[0][...]
        for r_ in refs[1:-1]:
            ds = ds + r_[...]
        _, vjp = jax.vjp(_silu, c_ref[...])
        (dc,) = vjp(ds)
        o_ref[...] = jnp.sum(dc[0:8], axis=0, keepdims=True)

    return pl.pallas_call(body, name=name, out_shape=jax.ShapeDtypeStruct((1, c16.shape[1]), F32))(c16, *ds_list)


def _s5_disc(lam_re, lam_im, log_step, b_re, b_im):
    lr = jnp.minimum(lam_re, S5_LAM_RE_MAX)
    li = lam_im
    dt = jnp.exp(log_step)
    mag = jnp.exp(lr * dt)
    abar_r = mag * jnp.cos(li * dt)
    abar_i = mag * jnp.sin(li * dt)
    den = lr * lr + li * li
    nr = abar_r - 1.0
    coef_r = (nr * lr + abar_i * li) / den
    coef_i = (abar_i * lr - nr * li) / den
    bbar_r = coef_r[:, None, :] * b_re - coef_i[:, None, :] * b_im
    bbar_i = coef_r[:, None, :] * b_im + coef_i[:, None, :] * b_re
    return abar_r, abar_i, bbar_r, bbar_i


def s5_disc_fwd(lam_re, lam_im, log_step, b_re, b_im, *, name):
    def body(lr, li, ls, br, bi, ar_o, ai_o, br_o, bi_o):
        ar_o[...], ai_o[...], br_o[...], bi_o[...] = _s5_disc(lr[...], li[...], ls[...], br[...], bi[...])

    s2, s3 = jax.ShapeDtypeStruct(lam_re.shape, F32), jax.ShapeDtypeStruct(b_re.shape, F32)
    return pl.pallas_call(body, name=name, out_shape=(s2, s2, s3, s3))(lam_re, lam_im, log_step, b_re, b_im)


def s5_disc_bwd(lam_re, lam_im, log_step, b_re, b_im, d_ar, d_ai, d_br, d_bi, *, name):
    def body(lr, li, ls, br, bi, dar, dai, dbr, dbi, o_lr, o_li, o_ls, o_br, o_bi):
        _, vjp = jax.vjp(_s5_disc, lr[...], li[...], ls[...], br[...], bi[...])
        o_lr[...], o_li[...], o_ls[...], o_br[...], o_bi[...] = vjp((dar[...], dai[...], dbr[...], dbi[...]))

    s2, s3 = jax.ShapeDtypeStruct(lam_re.shape, F32), jax.ShapeDtypeStruct(b_re.shape, F32)
    return pl.pallas_call(body, name=name, out_shape=(s2, s2, jax.ShapeDtypeStruct(log_step.shape, F32), s3, s3))(
        lam_re, lam_im, log_step, b_re, b_im, d_ar, d_ai, d_br, d_bi)


S5_LANES = 512


def _chunk_order(k, ncc, nch, rev):
    if not rev:
        return k
    return jnp.where(k < ncc, ncc - 1 - k, nch - 1 - (k - ncc))


def _cmul(ar, ai, xr, xi):
    return ar * xr - ai * xi, ar * xi + ai * xr


S5_BWD_ROWS = 128


def _shift_steps(x, edge_tile, back):
    n = x.shape[0]
    row = lax.broadcasted_iota(jnp.int32, (8, x.shape[1]), 0)
    edge = pltpu.roll(edge_tile, 4, 0)
    if back:
        y = pltpu.roll(x, 4, 0)
        return jnp.concatenate([jnp.where(row < 4, edge, y[0:8]), y[8:]], axis=0)
    y = pltpu.roll(x, n - 4, 0)
    return jnp.concatenate([y[:n - 8], jnp.where(row >= 4, edge, y[n - 8:])], axis=0)


def s5_scan_fwd(cfg, u, a2_re, a2_im, bre, bim, abre, abim, cre, cim, *, rev, name):
    r, d = u.shape
    ns = a2_re.shape[1]
    kb = d // S5_KIN
    tcr = 2 * S5_BWD_ROWS
    n8 = tcr // 8
    nch, ncc = r // tcr, cfg["rc"] // tcr
    lw = min(S5_LANES, ns)

    def body(u_ref, ar_ref, ai_ref, bre_ref, bim_ref, abre_ref, abim_ref, cre_ref, cim_ref, sre_ref, sim_ref, ere_ref, eim_ref, y_ref,
             st_re, st_im, u_edge):
        @pl.when(pl.program_id(0) == 0)
        def _():
            st_re[...] = jnp.zeros_like(st_re)
            st_im[...] = jnp.zeros_like(st_im)
            u_edge[...] = jnp.zeros_like(u_edge)

        u_ = u_ref[...]
        ub = u_.astype(MXU)
        upb = _shift_steps(u_, u_edge[...], back=not rev).astype(MXU)
        u_edge[...] = u_[0:8] if rev else u_[tcr - 8:tcr]
        for j in range(kb):
            uj, upj = ub[:, j * S5_KIN:(j + 1) * S5_KIN], upb[:, j * S5_KIN:(j + 1) * S5_KIN]
            sre_ref[:, :, j * S5_KST:(j + 1) * S5_KST] = (_dot(uj, bre_ref[j]) + _dot(upj, abre_ref[j])).reshape(n8, 8, S5_KST)
            sim_ref[:, :, j * S5_KST:(j + 1) * S5_KST] = (_dot(uj, bim_ref[j]) + _dot(upj, abim_ref[j])).reshape(n8, 8, S5_KST)
        half = n8 // 2
        for c in range(ns // lw):
            sl = slice(c * lw, (c + 1) * lw)
            ar = jnp.broadcast_to(ar_ref[:, sl], (8, lw))
            ai = jnp.broadcast_to(ai_ref[:, sl], (8, lw))

            def step(i, carry, sl=sl, ar=ar, ai=ai):
                sr, si = carry
                ii = n8 - 1 - i if rev else i
                pr, pi = _cmul(ar, ai, sr, si)
                sr, si = pr + sre_ref[ii, :, sl], pi + sim_ref[ii, :, sl]
                sre_ref[ii, :, sl] = sr
                sim_ref[ii, :, sl] = si
                return sr, si

            s0r, s0i = st_re[:, sl], st_im[:, sl]
            first, second = (1, 0) if rev else (0, 1)
            ere_ref[first, :, sl] = s0r
            eim_ref[first, :, sl] = s0i
            sr, si = lax.fori_loop(0, half, step, (s0r, s0i))
            ere_ref[second, :, sl] = sr
            eim_ref[second, :, sl] = si
            sr, si = lax.fori_loop(half, n8, step, (sr, si))
            st_re[:, sl] = sr
            st_im[:, sl] = si
        for j in range(kb):
            sr = sre_ref[:, :, j * S5_KST:(j + 1) * S5_KST].reshape(tcr, S5_KST)
            si = sim_ref[:, :, j * S5_KST:(j + 1) * S5_KST].reshape(tcr, S5_KST)
            y_ref[:, j * S5_KIN:(j + 1) * S5_KIN] = _dot(sr, cre_ref[j]) - _dot(si, cim_ref[j])

    cidx = functools.partial(_chunk_order, ncc=ncc, nch=nch, rev=rev)
    full = lambda a: pl.BlockSpec(a.shape, lambda k: (0,) * a.ndim)
    st = pl.BlockSpec((n8, 8, ns), lambda k: (cidx(k), 0, 0))
    en = pl.BlockSpec((2, 8, ns), lambda k: (cidx(k), 0, 0))
    return pl.pallas_call(
        body, name=name, grid=(nch,),
        in_specs=[pl.BlockSpec((tcr, d), lambda k: (cidx(k), 0)), full(a2_re), full(a2_im), full(bre), full(bim), full(abre), full(abim),
                  full(cre), full(cim)],
        out_specs=(st, st, en, en, pl.BlockSpec((tcr, d), lambda k: (cidx(k), 0))),
        out_shape=(jax.ShapeDtypeStruct((r // 8, 8, ns), F32),) * 2 + (jax.ShapeDtypeStruct((2 * nch, 8, ns), F32),) * 2
        + (jax.ShapeDtypeStruct((r, d), F32),),
        scratch_shapes=[pltpu.VMEM((8, ns), F32), pltpu.VMEM((8, ns), F32), pltpu.VMEM((8, d), F32)],
        compiler_params=_cp(("arbitrary",)))(u, a2_re, a2_im, bre, bim, abre, abim, cre, cim)


def s5_scan_bwd(cfg, dyb, sre, sim, ere, eim, a2_re, a2_im, bre, bim, cre, cim, c2re, c2im, du_in, *, rev, name):
    r, d = dyb.shape
    ns = a2_re.shape[1]
    kb = d // S5_KIN
    tcr = S5_BWD_ROWS
    n8 = tcr // 8
    nch, ncc = r // tcr, cfg["rc"] // tcr
    lw = min(S5_LANES, ns)

    def body(dy_ref, sre_ref, sim_ref, ere_ref, eim_ref, ar_ref, ai_ref, bre_ref, bim_ref, cre_ref, cim_ref, c2re_ref, c2im_ref, duin_ref,
             du_ref, gre_ref, gim_ref, dar_ref, dai_ref, g_re, g_im, gc_re, gc_im, dy_edge):
        k = pl.program_id(0)

        @pl.when(k == 0)
        def _():
            gc_re[...] = jnp.zeros_like(gc_re)
            gc_im[...] = jnp.zeros_like(gc_im)
            dar_ref[...] = jnp.zeros_like(dar_ref)
            dai_ref[...] = jnp.zeros_like(dai_ref)
            dy_edge[...] = jnp.zeros_like(dy_edge)

        dy32 = dy_ref[...].astype(F32)
        dy = dy32.astype(MXU)
        dyn = _shift_steps(dy32, dy_edge[...], back=rev).astype(MXU)
        dy_edge[...] = dy32[tcr - 8:tcr] if rev else dy32[0:8]
        for j in range(kb):
            dyj, dynj = dy[:, j * S5_KIN:(j + 1) * S5_KIN], dyn[:, j * S5_KIN:(j + 1) * S5_KIN]
            g_re[:, :, j * S5_KST:(j + 1) * S5_KST] = (_dot(dyj, cre_ref[j], 1, 1) + _dot(dynj, c2re_ref[j], 1, 1)).reshape(n8, 8, S5_KST)
            g_im[:, :, j * S5_KST:(j + 1) * S5_KST] = -(_dot(dyj, cim_ref[j], 1, 1) + _dot(dynj, c2im_ref[j], 1, 1)).reshape(n8, 8, S5_KST)
        first = lax.broadcasted_iota(jnp.int32, (8, lw), 0) < 4
        if rev:
            first = jnp.logical_not(first)
        for c in range(ns // lw):
            sl = slice(c * lw, (c + 1) * lw)
            ar = jnp.broadcast_to(ar_ref[:, sl], (8, lw))
            nai = -jnp.broadcast_to(ai_ref[:, sl], (8, lw))

            def step(i, carry, sl=sl, ar=ar, nai=nai):
                gr, gi, accr, acci = carry
                ii = i if rev else n8 - 1 - i
                pr, pi = _cmul(ar, nai, gr, gi)
                outr, outi = pr + g_re[ii, :, sl], pi + g_im[ii, :, sl]
                g_re[ii, :, sl] = outr
                g_im[ii, :, sl] = outi
                pv = jnp.clip(ii + 1 if rev else ii - 1, 0, n8 - 1)
                at_entry = (ii == n8 - 1) if rev else (ii == 0)
                pvr = jnp.where(at_entry, ere_ref[0, :, sl], sre_ref[pv, :, sl])
                pvi = jnp.where(at_entry, eim_ref[0, :, sl], sim_ref[pv, :, sl])
                spr = pltpu.roll(jnp.where(first, sre_ref[ii, :, sl], pvr), 4, 0)
                spi = pltpu.roll(jnp.where(first, sim_ref[ii, :, sl], pvi), 4, 0)
                accr = accr + outr * spr + outi * spi
                acci = acci + outi * spr - outr * spi
                return outr, outi, accr, acci

            gr, gi, accr, acci = lax.fori_loop(0, n8, step, (gc_re[:, sl], gc_im[:, sl], dar_ref[:, sl], dai_ref[:, sl]))
            gc_re[:, sl] = gr
            gc_im[:, sl] = gi
            dar_ref[:, sl] = accr
            dai_ref[:, sl] = acci
        for j in range(kb):
            gr = g_re[:, :, j * S5_KST:(j + 1) * S5_KST].reshape(tcr, S5_KST)
            gi = g_im[:, :, j * S5_KST:(j + 1) * S5_KST].reshape(tcr, S5_KST)
            gre_ref[:, j * S5_KST:(j + 1) * S5_KST] = gr.astype(MXU)
            gim_ref[:, j * S5_KST:(j + 1) * S5_KST] = gi.astype(MXU)
            du_ref[:, j * S5_KIN:(j + 1) * S5_KIN] = (duin_ref[:, j * S5_KIN:(j + 1) * S5_KIN]
                                                     + _dot(gr, bre_ref[j], 1, 1) + _dot(gi, bim_ref[j], 1, 1))

    def cidx(k):
        return _chunk_order(nch - 1 - k, ncc, nch, rev)

    full = lambda a: pl.BlockSpec(a.shape, lambda k: (0,) * a.ndim)
    st = pl.BlockSpec((n8, 8, ns), lambda k: (cidx(k), 0, 0))
    en = pl.BlockSpec((1, 8, ns), lambda k: (cidx(k), 0, 0))
    rowd = pl.BlockSpec((tcr, d), lambda k: (cidx(k), 0))
    rown = pl.BlockSpec((tcr, ns), lambda k: (cidx(k), 0))
    acc = pl.BlockSpec((8, ns), lambda k: (0, 0))
    return pl.pallas_call(
        body, name=name, grid=(nch,),
        in_specs=[rowd, st, st, en, en, full(a2_re), full(a2_im), full(bre), full(bim), full(cre), full(cim), full(c2re), full(c2im), rowd],
        out_specs=(rowd, rown, rown, acc, acc),
        out_shape=(jax.ShapeDtypeStruct((r, d), F32), jax.ShapeDtypeStruct((r, ns), MXU), jax.ShapeDtypeStruct((r, ns), MXU),
                   jax.ShapeDtypeStruct((8, ns), F32), jax.ShapeDtypeStruct((8, ns), F32)),
        scratch_shapes=[pltpu.VMEM((n8, 8, ns), F32), pltpu.VMEM((n8, 8, ns), F32), pltpu.VMEM((8, ns), F32), pltpu.VMEM((8, ns), F32),
                        pltpu.VMEM((8, d), F32)],
        compiler_params=_cp(("arbitrary",)))(dyb, sre, sim, ere, eim, a2_re, a2_im, bre, bim, cre, cim, c2re, c2im, du_in)


def rowmap(fn, rows_in, vecs_in, outs, accs=(), *, name):
    r = rows_in[0].shape[0]
    tm = _row_tile(r, max(a.shape[1] for a in rows_in))
    nr, nv, no = len(rows_in), len(vecs_in), len(outs)

    def body(*refs):
        ins = [x[...] for x in refs[:nr + nv]]
        res = fn(*ins)
        if not isinstance(res, (tuple, list)):
            res = (res,)
        out_refs = refs[nr + nv:]
        for o_ref, v in zip(out_refs[:no], res[:no]):
            o_ref[...] = v.astype(o_ref.dtype)
        if accs:
            @pl.when(pl.program_id(0) == 0)
            def _():
                for a_ref in out_refs[no:]:
                    a_ref[...] = jnp.zeros_like(a_ref)
            for a_ref, v in zip(out_refs[no:], res[no:]):
                a_ref[...] += v

    in_specs = [pl.BlockSpec((tm, a.shape[1]), lambda i: (i, 0)) for a in rows_in]
    in_specs += [pl.BlockSpec(v.shape, lambda i, n=v.ndim: (0,) * n) for v in vecs_in]
    out_specs = [pl.BlockSpec((tm, w), lambda i: (i, 0)) for w, _ in outs] + [pl.BlockSpec(s, lambda i, n=len(s): (0,) * n) for s in accs]
    out_shape = [jax.ShapeDtypeStruct((r, w), dt) for w, dt in outs] + [jax.ShapeDtypeStruct(s, F32) for s in accs]
    res = pl.pallas_call(body, name=name, grid=(r // tm,), in_specs=in_specs, out_specs=tuple(out_specs), out_shape=tuple(out_shape),
                         compiler_params=_cp(("arbitrary",) if accs else ("parallel",)))(*rows_in, *vecs_in)
    return res


def _gelu(x):
    return jax.nn.gelu(x, approximate=True)


def _hg_lower_bound(e0, e1):
    m = jnp.maximum(e0, e1)
    a, b = jnp.exp(e0 - m), jnp.exp(e1 - m)
    return b / (a + b)


def _hg_gates(x, lb):
    logf = jnp.log(lb + (1.0 - lb) * jax.nn.sigmoid(x))
    return logf, (1.0 - lb) * jax.nn.sigmoid(-x)


def _hg_masks(rev):
    n = CHUNK_ROWS
    rr = lax.broadcasted_iota(jnp.int32, (n, n), 0)
    ss = lax.broadcasted_iota(jnp.int32, (n, n), 1)
    same = (rr % NB) == (ss % NB)
    causal = same & ((ss >= rr) if rev else (ss <= rr))
    anti = same & ((ss <= rr) if rev else (ss >= rr))
    end0 = 0 if rev else n - NB
    pick_end = ss == (end0 + rr % NB)
    return same, causal, anti, pick_end, end0


def _hg_expand(x):
    ex = lax.broadcasted_iota(jnp.int32, x.shape, 0) % NB
    return jnp.concatenate([jnp.where(ex == b, x, 0.0) for b in range(NB)], axis=1)


def _hg_fold(xe):
    kk = xe.shape[1] // NB
    ex = lax.broadcasted_iota(jnp.int32, (xe.shape[0], kk), 0) % NB
    out = jnp.zeros((xe.shape[0], kk), F32)
    for b in range(NB):
        out = out + jnp.where(ex == b, xe[:, b * kk:(b + 1) * kk], 0.0)
    return out


def _hg_chunk(q, v, x, lb, masks):
    same, causal, anti, pick_end, end0 = masks
    logf, kk = _hg_gates(x, lb)
    b = _dot3(causal.astype(MXU), logf)
    bend_t = _dot3(pick_end.astype(MXU), b)
    bend_flat = jnp.concatenate([b[end0 + i:end0 + i + 1] for i in range(NB)], axis=1)
    eb = jnp.exp(b)
    enb = jnp.exp(-b)
    ee = jnp.exp(bend_t - b)
    qd, kd, ke = q * eb, kk * enb, kk * ee
    att = jnp.where(causal, _dot(qd, kd, 1, 1), 0.0)
    decay = jnp.exp(bend_flat)
    return dict(same=same, causal=causal, anti=anti, logf=logf, kk=kk, b=b, eb=eb, enb=enb, ee=ee, qd=qd, kd=kd, ke=ke, att=att,
                decay=decay, qde=_hg_expand(qd), kee=_hg_expand(ke))


def _hg_chunk_order(cfg, r):
    nch, ncc = r // CHUNK_ROWS, cfg["rc"] // CHUNK_ROWS
    return nch, ncc


def hg_scan_fwd(cfg, z, lb, *, d_dir, name):
    r = z.shape[0]
    d = z.shape[1] // N_PROJ
    nh = d // HEAD
    rev = d_dir == 1
    nch, ncc = _hg_chunk_order(cfg, r)
    n = CHUNK_ROWS

    def body(q_ref, v_ref, x_ref, lb_ref, o_ref, sin_ref, stk):
        @pl.when(pl.program_id(0) == 0)
        def _():
            stk[...] = jnp.zeros_like(stk)

        masks = _hg_masks(rev)
        for h in range(nh):
            sl = slice(h * HEAD, (h + 1) * HEAD)
            s0 = stk[h]
            sin_ref[0, h] = s0
            v = v_ref[:, sl]
            c = _hg_chunk(q_ref[:, sl], v, x_ref[:, sl], lb_ref[:, sl], masks)
            o_ref[:, sl] = _dot(c["att"], v) + _dot(c["qde"], s0, 1, 1)
            stk[h] = s0 * c["decay"] + _dot(v, c["kee"], 0, 0)

    def cidx(k):
        return _chunk_order(k, ncc, nch, rev)

    blk = lambda p: pl.BlockSpec((n, d), lambda k: (cidx(k), p))
    return pl.pallas_call(
        body, name=name, grid=(nch,),
        in_specs=[blk(0), blk(1), blk(2 + d_dir), pl.BlockSpec((1, d), lambda k: (0, 0))],
        out_specs=(blk(0), pl.BlockSpec((1, nh, HEAD, NB * HEAD), lambda k: (cidx(k), 0, 0, 0))),
        out_shape=(jax.ShapeDtypeStruct((r, d), F32), jax.ShapeDtypeStruct((nch, nh, HEAD, NB * HEAD), F32)),
        scratch_shapes=[pltpu.VMEM((nh, HEAD, NB * HEAD), F32)], compiler_params=_cp(("arbitrary",)))(z, z, z, lb)


def hg_scan_bwd(cfg, do, z, lb, sin, dq_in, dv_in, *, d_dir, name):
    r = z.shape[0]
    d = z.shape[1] // N_PROJ
    nh = d // HEAD
    rev = d_dir == 1
    nch, ncc = _hg_chunk_order(cfg, r)
    n = CHUNK_ROWS
    has_in = dq_in is not None

    def body(*refs):
        if has_in:
            do_ref, q_ref, v_ref, x_ref, lb_ref, sin_ref, dqi_ref, dvi_ref, dq_ref, dv_ref, dx_ref, dlb_ref, dstk = refs
        else:
            do_ref, q_ref, v_ref, x_ref, lb_ref, sin_ref, dq_ref, dv_ref, dx_ref, dlb_ref, dstk = refs
        @pl.when(pl.program_id(0) == 0)
        def _():
            dstk[...] = jnp.zeros_like(dstk)
            dlb_ref[...] = jnp.zeros_like(dlb_ref)

        masks = _hg_masks(rev)
        ex = lax.broadcasted_iota(jnp.int32, (n, HEAD), 0) % NB
        for h in range(nh):
            sl = slice(h * HEAD, (h + 1) * HEAD)
            do_, q, v, x, lb_, s0, ds1 = do_ref[:, sl], q_ref[:, sl], v_ref[:, sl], x_ref[:, sl], lb_ref[:, sl], sin_ref[0, h], dstk[h]
            c = _hg_chunk(q, v, x, lb_, masks)
            datt = jnp.where(c["causal"], _dot(do_, v, 1, 1), 0.0)
            dv = _dot(c["att"], do_, 0, 0) + _dot(c["kee"], ds1, 1, 1)
            dqd = _dot(datt, c["kd"]) + _hg_fold(_dot(do_, s0))
            dkd = _dot(datt, c["qd"], 0, 0)
            dke = _hg_fold(_dot(v, ds1))
            dbend_flat = jnp.sum(ds1 * s0, axis=0, keepdims=True) * c["decay"]
            dstk[h] = _dot(do_, c["qde"], 0, 0) + ds1 * c["decay"]
            dq = dqd * c["eb"]
            dk = dkd * c["enb"] + dke * c["ee"]
            db = dqd * c["qd"] - dkd * c["kd"] - dke * c["ke"]
            dbend_rows = jnp.zeros((n, HEAD), F32)
            for b in range(NB):
                dbend_rows = dbend_rows + jnp.where(ex == b, dbend_flat[:, b * HEAD:(b + 1) * HEAD], 0.0)
            dlogf = _dot3(c["anti"].astype(MXU), db) + _dot3(c["same"].astype(MXU), dke * c["ke"]) + dbend_rows
            _, vjp = jax.vjp(_hg_gates, x, lb_)
            dx, dlb = vjp((dlogf, dk))
            if has_in:
                dq = dq + dqi_ref[:, sl]
                dv = dv + dvi_ref[:, sl]
            dq_ref[:, sl] = dq
            dv_ref[:, sl] = dv
            dx_ref[:, sl] = dx
            dlb_ref[:, sl] += dlb

    def cidx(k):
        return _chunk_order(nch - 1 - k, ncc, nch, rev)

    blk = lambda p: pl.BlockSpec((n, d), lambda k: (cidx(k), p))
    vec = pl.BlockSpec((1, d), lambda k: (0, 0))
    in_specs = [blk(0), blk(0), blk(1), blk(2 + d_dir), vec, pl.BlockSpec((1, nh, HEAD, NB * HEAD), lambda k: (cidx(k), 0, 0, 0))]
    args = [do, z, z, z, lb, sin]
    if has_in:
        in_specs += [blk(0), blk(0)]
        args += [dq_in, dv_in]
    rd = jax.ShapeDtypeStruct((r, d), F32)
    return pl.pallas_call(
        body, name=name, grid=(nch,), in_specs=in_specs, out_specs=(blk(0), blk(0), blk(0), vec),
        out_shape=(rd, rd, rd, jax.ShapeDtypeStruct((1, d), F32)),
        scratch_shapes=[pltpu.VMEM((nh, HEAD, NB * HEAD), F32)], compiler_params=_cp(("arbitrary",)))(*args)


def _hg_read(o, g, gw):
    on = o * lax.rsqrt(jnp.mean(o * o, axis=-1, keepdims=True) + NORM_EPS) * gw
    return on * jax.nn.sigmoid(g)


def hg_read_fwd(of, ob, z, gw, *, name):
    r, d = of.shape
    nh = d // HEAD
    tm = _row_tile(r)

    def body(of_ref, ob_ref, g_ref, gw_ref, o_ref):
        o_ref[...] = _hg_read(of_ref[...] + ob_ref[...], g_ref[...], gw_ref[...]).astype(MXU)

    blk = pl.BlockSpec((tm, HEAD), lambda i, h: (i, h))
    return pl.pallas_call(
        body, name=name, grid=(r // tm, nh),
        in_specs=[blk, blk, pl.BlockSpec((tm, HEAD), lambda i, h: (i, (N_PROJ - 1) * nh + h)), pl.BlockSpec((1, HEAD), lambda i, h: (0, 0))],
        out_specs=blk, out_shape=jax.ShapeDtypeStruct((r, d), MXU), compiler_params=_cp(("parallel", "parallel")))(of, ob, z, gw)


def hg_read_bwd(don, of, ob, z, gw, *, name):
    r, d = of.shape
    nh = d // HEAD
    tm = _row_tile(r)

    def body(don_ref, of_ref, ob_ref, g_ref, gw_ref, do_ref, dg_ref, dgw_ref):
        @pl.when((pl.program_id(0) == 0) & (pl.program_id(1) == 0))
        def _():
            dgw_ref[...] = jnp.zeros_like(dgw_ref)

        _, vjp = jax.vjp(_hg_read, of_ref[...] + ob_ref[...], g_ref[...], gw_ref[...])
        do_ref[...], dg_ref[...], dgw = vjp(don_ref[...])
        dgw_ref[...] += dgw

    blk = pl.BlockSpec((tm, HEAD), lambda i, h: (i, h))
    vec = pl.BlockSpec((1, HEAD), lambda i, h: (0, 0))
    rd = jax.ShapeDtypeStruct((r, d), F32)
    return pl.pallas_call(
        body, name=name, grid=(r // tm, nh),
        in_specs=[blk, blk, blk, pl.BlockSpec((tm, HEAD), lambda i, h: (i, (N_PROJ - 1) * nh + h)), vec],
        out_specs=(blk, blk, vec), out_shape=(rd, rd, jax.ShapeDtypeStruct((1, HEAD), F32)),
        compiler_params=_cp(("arbitrary", "arbitrary")))(don, of, ob, z, gw)


FFN_COLS = 256


def _seg_masks(cfg, tr, i):
    t = lax.broadcasted_iota(jnp.int32, (tr, FFN_COLS), 0) // NB
    ctx_steps = cfg["rc"] // NB
    pos = jnp.where(i == 0, t % ctx_steps, t % GRID_W)
    last = jnp.where(i == 0, ctx_steps - 1, GRID_W - 1)
    return pos == 0, pos == last


def _prev(x, start):
    return jnp.where(start, 0.0, pltpu.roll(x, NB, 0))


def _next(x, end):
    return jnp.where(end, 0.0, pltpu.roll(x, x.shape[0] - NB, 0))


def _conv3(u, w, b, start, end):
    return ((b + _prev(u, start) * w[0:1]) + u * w[1:2]) + _next(u, end) * w[2:3]


def ffn_mid_fwd(cfg, u, cw, cb, *, name):
    r, f2 = u.shape
    f = f2 // 2
    tr = cfg["rc"]
    nf = f // FFN_COLS

    def body(ua_ref, ug_ref, wa_ref, wg_ref, ba_ref, bg_ref, o_ref):
        start, end = _seg_masks(cfg, tr, pl.program_id(0))
        a = _conv3(ua_ref[...], wa_ref[...], ba_ref[...], start, end)
        g = _conv3(ug_ref[...], wg_ref[...], bg_ref[...], start, end)
        o_ref[...] = (_silu(a) * g).astype(MXU)

    ca = lambda rows: pl.BlockSpec((rows, FFN_COLS), lambda i, j: (i if rows == tr else 0, j))
    cg = lambda rows: pl.BlockSpec((rows, FFN_COLS), lambda i, j: (i if rows == tr else 0, j + nf))
    return pl.pallas_call(
        body, name=name, grid=(r // tr, nf), in_specs=[ca(tr), cg(tr), ca(3), cg(3), ca(1), cg(1)], out_specs=ca(tr),
        out_shape=jax.ShapeDtypeStruct((r, f), MXU), compiler_params=_cp(("parallel", "parallel")))(u, u, cw, cw, cb, cb)


def ffn_mid_bwd(cfg, dact, u, cw, cb, *, name):
    r, f2 = u.shape
    f = f2 // 2
    tr = cfg["rc"]
    nf = f // FFN_COLS

    def body(da_ref, us_ref, up_ref, ws_ref, wp_ref, bs_ref, bp_ref, du_ref, dcw_ref, dcb_ref):
        i = pl.program_id(1)
        is_a = pl.program_id(0) < nf
        start, end = _seg_masks(cfg, tr, i)
        us, ws = us_ref[...], ws_ref[...]
        cs = _conv3(us, ws, bs_ref[...], start, end)
        cp = _conv3(up_ref[...], wp_ref[...], bp_ref[...], start, end)
        dact_v = da_ref[...]
        sg = jax.nn.sigmoid(cs)
        d_if_a = dact_v * cp * (sg * (1.0 + cs * (1.0 - sg)))
        d_if_g = dact_v * _silu(cp)
        dc = jnp.where(is_a, d_if_a, d_if_g)
        du_ref[...] = (ws[1:2] * dc + ws[0:1] * _next(dc, end) + ws[2:3] * _prev(dc, start)).astype(MXU)

        @pl.when(i == 0)
        def _():
            dcw_ref[...] = jnp.zeros_like(dcw_ref)
            dcb_ref[...] = jnp.zeros_like(dcb_ref)

        dcw_ref[...] += jnp.concatenate([jnp.sum(dc * _prev(us, start), axis=0, keepdims=True), jnp.sum(dc * us, axis=0, keepdims=True),
                                         jnp.sum(dc * _next(us, end), axis=0, keepdims=True)], axis=0)
        dcb_ref[...] += jnp.sum(dc, axis=0, keepdims=True)

    cs_ = lambda rows: pl.BlockSpec((rows, FFN_COLS), lambda j, i: (i if rows == tr else 0, j))
    cp_ = lambda rows: pl.BlockSpec((rows, FFN_COLS), lambda j, i: (i if rows == tr else 0, (j + nf) % (2 * nf)))
    return pl.pallas_call(
        body, name=name, grid=(2 * nf, r // tr),
        in_specs=[pl.BlockSpec((tr, FFN_COLS), lambda j, i: (i, j % nf)), cs_(tr), cp_(tr), cs_(3), cp_(3), cs_(1), cp_(1)],
        out_specs=(cs_(tr), cs_(3), cs_(1)),
        out_shape=(jax.ShapeDtypeStruct((r, f2), MXU), jax.ShapeDtypeStruct((3, f2), F32), jax.ShapeDtypeStruct((1, f2), F32)),
        compiler_params=_cp(("parallel", "arbitrary")))(dact, u, u, cw, cw, cb, cb)


def hg_lb_fwd(e0, e1, *, name):
    def body(a, b, o):
        o[...] = _hg_lower_bound(a[...], b[...])

    return pl.pallas_call(body, name=name, out_shape=jax.ShapeDtypeStruct(e0.shape, F32))(e0, e1)


def hg_lb_bwd(e0, e1, dlb, *, name):
    def body(a, b, g, oa, ob):
        _, vjp = jax.vjp(_hg_lower_bound, a[...], b[...])
        oa[...], ob[...] = vjp(g[...])

    s = jax.ShapeDtypeStruct(e0.shape, F32)
    return pl.pallas_call(body, name=name, out_shape=(s, s))(e0, e1, dlb)


def _adamw(w, g, m, v):
    m = ADAM_B1 * m + (1.0 - ADAM_B1) * g
    v = ADAM_B2 * v + (1.0 - ADAM_B2) * jnp.square(g)
    m_hat = m / (1.0 - ADAM_B1 ** ADAM_STEP)
    v_hat = v / (1.0 - ADAM_B2 ** ADAM_STEP)
    delta = -ADAM_LR * (m_hat / (jnp.sqrt(v_hat) + ADAM_EPS) + ADAM_WD * w)
    return delta, m, v


def _as2d(a):
    if a.ndim >= 2 and a.shape[-1] % 128 == 0:
        return a.reshape(-1, a.shape[-1])
    return a.reshape(-1, 128) if a.size % 128 == 0 else a.reshape(1, -1)


def adamw(w, g, m, v, *, name):
    w2 = _as2d(w)
    outs = rowmap(_adamw, [w2, _as2d(g), _as2d(m), _as2d(v)], [], [(w2.shape[1], F32)] * 3, name=name)
    return tuple(o.reshape(w.shape) for o in outs)


HBM_SPEC = pl.BlockSpec(memory_space=pltpu.HBM)


def _place():
    mx, my, mc = lax.axis_index("x"), lax.axis_index("y"), lax.axis_index("c")
    others = [(1 - mx, my), (mx, 1 - my), (1 - mx, 1 - my)]
    return mx, my, mc, others


def chip_allgather(x, *, name):
    def body(x_ref, o_ref, send_sems, recv_sems, local_sem):
        mx, my, mc, others = _place()
        me = 2 * mx + my
        mine = pltpu.make_async_copy(x_ref, o_ref.at[me], local_sem)
        mine.start()
        sends = [pltpu.make_async_remote_copy(src_ref=x_ref, dst_ref=o_ref.at[me], send_sem=send_sems.at[j], recv_sem=recv_sems.at[j],
                                              device_id=(px, py, mc), device_id_type=MESH) for j, (px, py) in enumerate(others)]
        for cp in sends:
            cp.start()
        for j, (px, py) in enumerate(others):
            pltpu.make_async_remote_copy(src_ref=x_ref, dst_ref=o_ref.at[2 * px + py], send_sem=send_sems.at[j], recv_sem=recv_sems.at[j],
                                         device_id=(px, py, mc), device_id_type=MESH).wait_recv()
        for cp in sends:
            cp.wait_send()
        mine.wait()

    return pl.pallas_call(
        body, name=name, out_shape=jax.ShapeDtypeStruct((4,) + x.shape, x.dtype), in_specs=[HBM_SPEC], out_specs=HBM_SPEC,
        scratch_shapes=[pltpu.SemaphoreType.DMA((3,)), pltpu.SemaphoreType.DMA((3,)), pltpu.SemaphoreType.DMA])(x)


def _win(ref, axis, start, size):
    idx = [slice(None)] * len(ref.shape)
    idx[axis] = pl.ds(start, size)
    return ref.at[tuple(idx)]


def _half_axis(shape, ax):
    if shape[0] == 2:
        return 0
    return 2 if ax == 1 else 1


def _cut(shape, axis, parts):
    return shape[:axis] + (shape[axis] // parts,) + shape[axis + 1:]


def _hbm_call(body, arrays, out_shapes, sems, name):
    n_in = len(arrays)
    return pl.pallas_call(body, name=name, out_shape=tuple(out_shapes), in_specs=[HBM_SPEC] * n_in, out_specs=tuple([HBM_SPEC] * len(out_shapes)),
                          scratch_shapes=sems)(*arrays)


def place_shard(shard, ax, chip, dtype, *, name):
    l, r, c = shard.shape
    tr = _row_tile(r, c)
    per_block = (l, r // tr, 1)[ax]

    def omap(li, ri, cref):
        idx = [li, ri, 0]
        idx[ax] = idx[ax] + cref[0] * per_block
        return tuple(idx)

    def body(c_ref, s_ref, o_ref):
        o_ref[...] = s_ref[...].astype(dtype)

    full = shard.shape[:ax] + (4 * shard.shape[ax],) + shard.shape[ax + 1:]
    return pl.pallas_call(
        body, name=name, out_shape=jax.ShapeDtypeStruct(full, dtype),
        grid_spec=pltpu.PrefetchScalarGridSpec(
            num_scalar_prefetch=1, grid=(l, r // tr),
            in_specs=[pl.BlockSpec((1, tr, c), lambda li, ri, cref: (li, ri, 0))], out_specs=pl.BlockSpec((1, tr, c), omap)),
        compiler_params=_cp(("parallel", "parallel")))(chip, shard)


def gather_placed(arrays, axes, haxes, *, name):
    n = len(arrays)

    def body(*refs):
        ins, outs = refs[:n], refs[n:2 * n]
        send_sems, recv_sems = refs[2 * n:]
        mx, my, mc, others = _place()
        me = 2 * mx + my

        def part(ref, i, chip):
            sz, hs = arrays[i].shape[axes[i]] // 4, arrays[i].shape[haxes[i]] // 2
            return _win(_win(ref, axes[i], chip * sz, sz), haxes[i], mc * hs, hs)

        sends = []
        for i in range(n):
            for j, (px, py) in enumerate(others):
                rc = pltpu.make_async_remote_copy(src_ref=part(ins[i], i, me), dst_ref=part(outs[i], i, me), send_sem=send_sems.at[i, j],
                                                  recv_sem=recv_sems.at[i, j], device_id=(px, py, mc), device_id_type=MESH)
                rc.start()
                sends.append(rc)
        for i in range(n):
            for j, (px, py) in enumerate(others):
                pltpu.make_async_remote_copy(src_ref=part(ins[i], i, me), dst_ref=part(outs[i], i, 2 * px + py), send_sem=send_sems.at[i, j],
                                             recv_sem=recv_sems.at[i, j], device_id=(px, py, mc), device_id_type=MESH).wait_recv()
        for rc in sends:
            rc.wait_send()

    return pl.pallas_call(
        body, name=name, out_shape=tuple(jax.ShapeDtypeStruct(a_.shape, a_.dtype) for a_ in arrays), in_specs=[HBM_SPEC] * n,
        out_specs=tuple([HBM_SPEC] * n), input_output_aliases={i: i for i in range(n)},
        scratch_shapes=[pltpu.SemaphoreType.DMA((n, 3)), pltpu.SemaphoreType.DMA((n, 3))])(*arrays)


def pair_swap_halves(arrays, haxes, *, name):
    n = len(arrays)

    def body(*refs):
        ins, outs = refs[:n], refs[n:2 * n]
        send_sems, recv_sems = refs[2 * n:]
        mx, my, mc, _ = _place()
        cps = []
        for i in range(n):
            hs = arrays[i].shape[haxes[i]] // 2
            cp = pltpu.make_async_remote_copy(src_ref=_win(ins[i], haxes[i], (1 - mc) * hs, hs), dst_ref=outs[i], send_sem=send_sems.at[i],
                                              recv_sem=recv_sems.at[i], device_id=(mx, my, 1 - mc), device_id_type=MESH)
            cp.start()
            cps.append(cp)
        for cp in cps:
            cp.wait()

    outs = [jax.ShapeDtypeStruct(_cut(a_.shape, h_, 2), a_.dtype) for a_, h_ in zip(arrays, haxes)]
    return _hbm_call(body, arrays, outs, [pltpu.SemaphoreType.DMA((n,)), pltpu.SemaphoreType.DMA((n,))], name)


def add_own_half(g, t, hax, core, *, out_dtype, name):
    l, r, c = t.shape
    tr = _row_tile(r, c)
    per_half = (l, r // tr, 1)[hax]

    def imap(li, ri, cref):
        idx = [li, ri, 0]
        idx[hax] = idx[hax] + cref[0] * per_half
        return tuple(idx)

    def body(c_ref, g_ref, t_ref, o_ref):
        o_ref[...] = (g_ref[...] + t_ref[...]).astype(out_dtype)

    return pl.pallas_call(
        body, name=name, out_shape=jax.ShapeDtypeStruct(t.shape, out_dtype),
        grid_spec=pltpu.PrefetchScalarGridSpec(
            num_scalar_prefetch=1, grid=(l, r // tr),
            in_specs=[pl.BlockSpec((1, tr, c), imap), pl.BlockSpec((1, tr, c), lambda li, ri, cref: (li, ri, 0))],
            out_specs=pl.BlockSpec((1, tr, c), lambda li, ri, cref: (li, ri, 0))),
        compiler_params=_cp(("parallel", "parallel")))(core, g, t)


def exchange_blocks(arrays, axes, *, name):
    n = len(arrays)

    def body(*refs):
        ins, outs = refs[:n], refs[n:2 * n]
        send_sems, recv_sems, local_sems = refs[2 * n:]
        mx, my, mc, others = _place()
        me = 2 * mx + my
        waits = []
        for i in range(n):
            sz = arrays[i].shape[axes[i]] // 4
            cp = pltpu.make_async_copy(_win(ins[i], axes[i], me * sz, sz), outs[i].at[me], local_sems.at[i])
            cp.start()
            waits.append(cp.wait)
            for j, (px, py) in enumerate(others):
                rc = pltpu.make_async_remote_copy(src_ref=_win(ins[i], axes[i], (2 * px + py) * sz, sz), dst_ref=outs[i].at[me],
                                                  send_sem=send_sems.at[i, j], recv_sem=recv_sems.at[i, j], device_id=(px, py, mc),
                                                  device_id_type=MESH)
                rc.start()
                waits.append(rc.wait_send)
        for i in range(n):
            sz = arrays[i].shape[axes[i]] // 4
            for j, (px, py) in enumerate(others):
                pltpu.make_async_remote_copy(src_ref=_win(ins[i], axes[i], me * sz, sz), dst_ref=outs[i].at[2 * px + py],
                                             send_sem=send_sems.at[i, j], recv_sem=recv_sems.at[i, j], device_id=(px, py, mc),
                                             device_id_type=MESH).wait_recv()
        for w_ in waits:
            w_()

    outs = [jax.ShapeDtypeStruct((4,) + _cut(a_.shape, ax, 4), a_.dtype) for a_, ax in zip(arrays, axes)]
    return _hbm_call(body, arrays, outs, [pltpu.SemaphoreType.DMA((n, 3)), pltpu.SemaphoreType.DMA((n, 3)), pltpu.SemaphoreType.DMA((n,))], name)


def sum_blocks(e, hax, core, *, name):
    _, l, r, c = e.shape
    tr = _row_tile(r, c)
    per_half = (l, r // tr, 1)[hax]

    def omap(li, ri, cref):
        idx = [li, ri, 0]
        idx[hax] = idx[hax] + cref[0] * per_half
        return tuple(idx)

    def body(c_ref, e_ref, o_ref):
        v = e_ref[...].astype(F32)
        o_ref[...] = ((v[0] + v[1]) + v[2]) + v[3]

    full = (l, r, c)[:hax] + (2 * (l, r, c)[hax],) + (l, r, c)[hax + 1:]
    return pl.pallas_call(
        body, name=name, out_shape=jax.ShapeDtypeStruct(full, F32),
        grid_spec=pltpu.PrefetchScalarGridSpec(
            num_scalar_prefetch=1, grid=(l, r // tr),
            in_specs=[pl.BlockSpec((4, 1, tr, c), lambda li, ri, cref: (0, li, ri, 0))], out_specs=pl.BlockSpec((1, tr, c), omap)),
        compiler_params=_cp(("parallel", "parallel")))(core, e)


def pair_fill_halves(arrays, haxes, *, name):
    n = len(arrays)

    def body(*refs):
        ins, outs = refs[:n], refs[n:2 * n]
        send_sems, recv_sems = refs[2 * n:]
        mx, my, mc, _ = _place()
        cps = []
        for i in range(n):
            hs = arrays[i].shape[haxes[i]] // 2
            mine = _win(ins[i], haxes[i], mc * hs, hs)
            cp = pltpu.make_async_remote_copy(src_ref=mine, dst_ref=_win(outs[i], haxes[i], mc * hs, hs), send_sem=send_sems.at[i],
                                              recv_sem=recv_sems.at[i], device_id=(mx, my, 1 - mc), device_id_type=MESH)
            cp.start()
            cps.append(cp)
        for i in range(n):
            hs = arrays[i].shape[haxes[i]] // 2
            pltpu.make_async_remote_copy(src_ref=_win(ins[i], haxes[i], mc * hs, hs), dst_ref=_win(outs[i], haxes[i], (1 - mc) * hs, hs),
                                         send_sem=send_sems.at[i], recv_sem=recv_sems.at[i], device_id=(mx, my, 1 - mc),
                                         device_id_type=MESH).wait_recv()
        for cp in cps:
            cp.wait_send()

    return pl.pallas_call(
        body, name=name, out_shape=tuple(jax.ShapeDtypeStruct(a_.shape, a_.dtype) for a_ in arrays), in_specs=[HBM_SPEC] * n,
        out_specs=tuple([HBM_SPEC] * n), input_output_aliases={i: i for i in range(n)},
        scratch_shapes=[pltpu.SemaphoreType.DMA((n,)), pltpu.SemaphoreType.DMA((n,))])(*arrays)


WEIGHTS = ['c_ctx', 'w_mod', 'b_mod', 'norm1_w', 'norm2_w', 'final_norm_w', 's5_w_in', 's5_lam_re', 's5_lam_im', 's5_log_step', 's5_b_re', 's5_b_im', 's5_c_re', 's5_c_im', 's5_d', 's5_w_glu', 's5_w_out', 'hg_w_in', 'hg_lower_bounds', 'hg_gnorm_w', 'hg_w_out', 'ffn_w_up', 'ffn_conv_w', 'ffn_conv_b', 'ffn_w_down']
INPUTS = ['x', 'c', 'ctx', 'c_ctx', 'w_mod', 'b_mod', 'norm1_w', 'norm2_w', 'final_norm_w', 's5_w_in', 's5_lam_re', 's5_lam_im', 's5_log_step', 's5_b_re', 's5_b_im', 's5_c_re', 's5_c_im', 's5_d', 's5_w_glu', 's5_w_out', 'hg_w_in', 'hg_lower_bounds', 'hg_gnorm_w', 'hg_w_out', 'ffn_w_up', 'ffn_conv_w', 'ffn_conv_b', 'ffn_w_down', 'loss_target', 'm_c_ctx', 'm_w_mod', 'm_b_mod', 'm_norm1_w', 'm_norm2_w', 'm_final_norm_w', 'm_s5_w_in', 'm_s5_lam_re', 'm_s5_lam_im', 'm_s5_log_step', 'm_s5_b_re', 'm_s5_b_im', 'm_s5_c_re', 'm_s5_c_im', 'm_s5_d', 'm_s5_w_glu', 'm_s5_w_out', 'm_hg_w_in', 'm_hg_lower_bounds', 'm_hg_gnorm_w', 'm_hg_w_out', 'm_ffn_w_up', 'm_ffn_conv_w', 'm_ffn_conv_b', 'm_ffn_w_down', 'v_c_ctx', 'v_w_mod', 'v_b_mod', 'v_norm1_w', 'v_norm2_w', 'v_final_norm_w', 'v_s5_w_in', 'v_s5_lam_re', 'v_s5_lam_im', 'v_s5_log_step', 'v_s5_b_re', 'v_s5_b_im', 'v_s5_c_re', 'v_s5_c_im', 'v_s5_d', 'v_s5_w_glu', 'v_s5_w_out', 'v_hg_w_in', 'v_hg_lower_bounds', 'v_hg_gnorm_w', 'v_hg_w_out', 'v_ffn_w_up', 'v_ffn_conv_w', 'v_ffn_conv_b', 'v_ffn_w_down']
SHARD_AXIS = {"w_mod": 2, "s5_w_in": 1, "s5_w_glu": 1, "s5_w_out": 1, "hg_w_in": 2, "hg_lower_bounds": 2, "hg_w_out": 1,
              "ffn_w_up": 2, "ffn_conv_w": 2, "ffn_w_down": 1}
GATHER_F32 = ("hg_lower_bounds", "ffn_conv_w")
PACK_W = 1024
GRAD_WIRE = jnp.bfloat16


def _gather_weights(a, names, chip):
    axes = [SHARD_AXIS[n] for n in names]
    placed = [place_shard(a[n], ax, chip, F32 if n in GATHER_F32 else MXU, name="place_" + n) for n, ax in zip(names, axes)]
    haxes = [_half_axis(p_.shape, ax) for p_, ax in zip(placed, axes)]
    got = gather_placed(placed, axes, haxes, name="allgather_weights")
    return dict(zip(names, pair_fill_halves(got, haxes, name="allgather_pair_fill")))


def _reduce_grads(a, grads, core):
    sharded = [n for n in WEIGHTS if n in SHARD_AXIS]
    small = [n for n in WEIGHTS if n not in SHARD_AXIS]
    flat = jnp.concatenate([grads[n].reshape(-1) for n in small])
    pad = (-flat.shape[0]) % (64 * PACK_W)
    small_pack = jnp.pad(flat, (0, pad)).reshape(1, -1, PACK_W)
    arrays = [grads[n] for n in sharded] + [small_pack]
    axes = [SHARD_AXIS[n] for n in sharded] + [1]
    haxes = [_half_axis(g_.shape, ax) for g_, ax in zip(arrays, axes)]
    tags = sharded + ["small"]
    t = pair_swap_halves(arrays, haxes, name="grad_pair_swap")
    h = [add_own_half(g_, t_, hx, core, out_dtype=GRAD_WIRE, name="grad_pair_add_" + tg) for g_, t_, hx, tg in zip(arrays, t, haxes, tags)]
    e = exchange_blocks(h, axes, name="grad_chip_exchange")
    s = [sum_blocks(e_, hx, core, name="grad_chip_sum_" + tg) for e_, hx, tg in zip(e, haxes, tags)]
    red = pair_fill_halves(s, haxes, name="grad_pair_fill")
    out = dict(zip(sharded, red[:-1]))
    sm = chip_allgather(red[-1][0], name="allgather_small_grads").reshape(-1)
    off = 0
    for n in small:
        out[n] = sm[off:off + math.prod(a[n].shape)].reshape(a[n].shape)
        off += math.prod(a[n].shape)
    return out


def _blockdiag_b(bb, kb):
    gl = S5_KIN // S5_GROUP
    x = bb.reshape(kb, gl, S5_GROUP, S5_STATE)
    return (x[:, :, :, None, :] * jnp.eye(gl, dtype=bb.dtype)[None, :, None, :, None]).reshape(kb, S5_KIN, S5_KST)


def _blockdiag_c(cc, kb):
    gl = S5_KIN // S5_GROUP
    x = cc.reshape(kb, gl, S5_GROUP, S5_STATE).transpose(0, 1, 3, 2)
    return (x[:, :, :, None, :] * jnp.eye(gl, dtype=cc.dtype)[None, :, None, :, None]).reshape(kb, S5_KST, S5_KIN)


def _diag_b(m, kb):
    gl = S5_KIN // S5_GROUP
    x = m.reshape(kb, gl, S5_GROUP, gl, S5_STATE)
    return jnp.stack([x[:, i, :, i, :] for i in range(gl)], axis=1).reshape(kb * gl, S5_GROUP, S5_STATE)


def _diag_c(m, kb):
    gl = S5_KIN // S5_GROUP
    x = m.reshape(kb, gl, S5_STATE, gl, S5_GROUP)
    return jnp.stack([x[:, i, :, i, :] for i in range(gl)], axis=1).transpose(0, 1, 3, 2).reshape(kb * gl, S5_GROUP, S5_STATE)


def kernel(x, c, ctx, c_ctx, w_mod, b_mod, norm1_w, norm2_w, final_norm_w, s5_w_in, s5_lam_re, s5_lam_im, s5_log_step, s5_b_re, s5_b_im, s5_c_re, s5_c_im, s5_d, s5_w_glu, s5_w_out, hg_w_in, hg_lower_bounds, hg_gnorm_w, hg_w_out, ffn_w_up, ffn_conv_w, ffn_conv_b, ffn_w_down, loss_target, m_c_ctx, m_w_mod, m_b_mod, m_norm1_w, m_norm2_w, m_final_norm_w, m_s5_w_in, m_s5_lam_re, m_s5_lam_im, m_s5_log_step, m_s5_b_re, m_s5_b_im, m_s5_c_re, m_s5_c_im, m_s5_d, m_s5_w_glu, m_s5_w_out, m_hg_w_in, m_hg_lower_bounds, m_hg_gnorm_w, m_hg_w_out, m_ffn_w_up, m_ffn_conv_w, m_ffn_conv_b, m_ffn_w_down, v_c_ctx, v_w_mod, v_b_mod, v_norm1_w, v_norm2_w, v_final_norm_w, v_s5_w_in, v_s5_lam_re, v_s5_lam_im, v_s5_log_step, v_s5_b_re, v_s5_b_im, v_s5_c_re, v_s5_c_im, v_s5_d, v_s5_w_glu, v_s5_w_out, v_hg_w_in, v_hg_lower_bounds, v_hg_gnorm_w, v_hg_w_out, v_ffn_w_up, v_ffn_conv_w, v_ffn_conv_b, v_ffn_w_down):
    a = dict(zip(INPUTS, (x, c, ctx, c_ctx, w_mod, b_mod, norm1_w, norm2_w, final_norm_w, s5_w_in, s5_lam_re, s5_lam_im, s5_log_step, s5_b_re, s5_b_im, s5_c_re, s5_c_im, s5_d, s5_w_glu, s5_w_out, hg_w_in, hg_lower_bounds, hg_gnorm_w, hg_w_out, ffn_w_up, ffn_conv_w, ffn_conv_b, ffn_w_down, loss_target, m_c_ctx, m_w_mod, m_b_mod, m_norm1_w, m_norm2_w, m_final_norm_w, m_s5_w_in, m_s5_lam_re, m_s5_lam_im, m_s5_log_step, m_s5_b_re, m_s5_b_im, m_s5_c_re, m_s5_c_im, m_s5_d, m_s5_w_glu, m_s5_w_out, m_hg_w_in, m_hg_lower_bounds, m_hg_gnorm_w, m_hg_w_out, m_ffn_w_up, m_ffn_conv_w, m_ffn_conv_b, m_ffn_w_down, v_c_ctx, v_w_mod, v_b_mod, v_norm1_w, v_norm2_w, v_final_norm_w, v_s5_w_in, v_s5_lam_re, v_s5_lam_im, v_s5_log_step, v_s5_b_re, v_s5_b_im, v_s5_c_re, v_s5_c_im, v_s5_d, v_s5_w_glu, v_s5_w_out, v_hg_w_in, v_hg_lower_bounds, v_hg_gnorm_w, v_hg_w_out, v_ffn_w_up, v_ffn_conv_w, v_ffn_conv_b, v_ffn_w_down)))
    nb, seq, d = x.shape
    assert nb == NB
    rc = nb * ctx.shape[1]
    cfg = {"rc": rc}
    f = a["ffn_w_down"].shape[1] * 4
    core = lax.axis_index("c").astype(jnp.int32).reshape(1)

    w = {n: a[n] for n in WEIGHTS if n not in SHARD_AXIS}
    chip = (2 * lax.axis_index("x") + lax.axis_index("y")).astype(jnp.int32).reshape(1)
    w.update(_gather_weights(a, [n for n in WEIGHTS if n in SHARD_AXIS], chip))

    tmaj = lambda t: t.transpose(1, 0, 2).reshape(-1, t.shape[-1])
    x0 = jnp.concatenate([tmaj(ctx), tmaj(x)], axis=0)
    tgt = tmaj(a["loss_target"])
    c16 = jnp.concatenate([jnp.broadcast_to(c_ctx[None], (8, d)), c, c], axis=0)
    mt, scb = [], None
    for l in range(2):
        m_, scb = mod_fwd(c16, w["w_mod"][l], w["b_mod"][l][None], name=f"mod_fwd{l}")
        mt.append(m_)
    n1, n2 = w["norm1_w"], w["norm2_w"]

    def ffn_fwd(l, h):
        u = mm(h, w["ffn_w_up"][l], name=f"ffn_up{l}")
        act = ffn_mid_fwd(cfg, u, w["ffn_conv_w"][l], w["ffn_conv_b"][l][None], name=f"ffn_mid{l}")
        return u, act, mm(act, w["ffn_w_down"][l], name=f"ffn_down{l}")

    def ffn_bwd(l, dfo, u, act, h):
        dact = mm(dfo, w["ffn_w_down"][l], tb=True, name=f"ffn_down_dx{l}")
        dwd = mm(act, dfo, ta=True, name=f"ffn_down_dw{l}")
        du, dcw, dcb = ffn_mid_bwd(cfg, dact, u, w["ffn_conv_w"][l], w["ffn_conv_b"][l][None], name=f"ffn_mid_bwd{l}")
        dh = mm(du, w["ffn_w_up"][l], tb=True, name=f"ffn_up_dx{l}")
        dwu = mm(h, du, ta=True, name=f"ffn_up_dw{l}")
        return dh, dwu, dcw, dcb[0], dwd

    g_, p_ = d // S5_GROUP, S5_STATE
    ns, kb = g_ * p_, d // S5_KIN
    s5p = (w["s5_lam_re"][0].reshape(2 * g_, p_), w["s5_lam_im"][0].reshape(2 * g_, p_), w["s5_log_step"][0].reshape(2 * g_, 1),
           w["s5_b_re"][0].transpose(0, 1, 3, 2).reshape(2 * g_, S5_GROUP, p_), w["s5_b_im"][0].transpose(0, 1, 3, 2).reshape(2 * g_, S5_GROUP, p_))
    ar, ai, bbr, bbi = s5_disc_fwd(*s5p, name="s5_disc")
    dsk = w["s5_d"]
    _, h1 = node_fwd(cfg, x0, None, None, 0, n1[0:1], mt[0], 0, name="node0a")
    u0 = mm(h1, w["s5_w_in"][0], name="s5_in")
    s5s, ys = [], []
    for dd in range(2):
        sl = slice(dd * g_, (dd + 1) * g_)
        a_r, a_i = ar[sl].reshape(1, ns), ai[sl].reshape(1, ns)
        a2 = (a_r * a_r - a_i * a_i, 2.0 * a_r * a_i)
        b_r, b_i = _blockdiag_b(bbr[sl], kb), _blockdiag_b(bbi[sl], kb)
        c_r, c_i = _blockdiag_c(w["s5_c_re"][0, dd], kb), _blockdiag_c(w["s5_c_im"][0, dd], kb)
        ak, ai_k = a_r.reshape(kb, 1, S5_KST), a_i.reshape(kb, 1, S5_KST)
        ab = (ak * b_r - ai_k * b_i, ak * b_i + ai_k * b_r)
        akc, aic = ak.reshape(kb, S5_KST, 1), ai_k.reshape(kb, S5_KST, 1)
        c2 = (akc * c_r - aic * c_i, akc * c_i + aic * c_r)
        bf = lambda t_: t_.astype(MXU)
        sre, sim, ere, eim, y_ = s5_scan_fwd(cfg, u0, a2[0], a2[1], bf(b_r), bf(b_i), bf(ab[0]), bf(ab[1]), bf(c_r), bf(c_i), rev=dd == 1,
                                             name=f"s5_scan{dd}")
        s5s.append((sre, sim, ere, eim, a2[0], a2[1], bf(b_r), bf(b_i), bf(c_r), bf(c_i), bf(c2[0]), bf(c2[1])))
        ys.append(y_)

    def glu_a(u, y0, y1, ds):
        yp = (ds * u + y0) + y1
        return yp, _gelu(yp)

    ypre, zgb = rowmap(glu_a, [u0, ys[0], ys[1]], [dsk], [(d, F32), (d, MXU)], name="s5_glu_a")
    tg = mm(zgb, w["s5_w_glu"][0], name="s5_glu")
    (z2,) = rowmap(lambda yp, t: _gelu(yp) * jax.nn.sigmoid(t), [ypre, tg], [], [(d, MXU)], name="s5_glu_b")
    y1a = mm(z2, w["s5_w_out"][0], name="s5_out")
    x1a, h2a = node_fwd(cfg, x0, y1a, mt[0], 2, n2[0:1], mt[0], 3, name="node0b")
    ufa, acta, foa = ffn_fwd(0, h2a)

    x2a, h1b = node_fwd(cfg, x1a, foa, mt[0], 5, n1[1:2], mt[1], 0, name="node1a")
    z = mm(h1b, w["hg_w_in"][0], name="hg_in")
    e0, e1 = w["hg_lower_bounds"][:, 0, :], w["hg_lower_bounds"][:, 1, :]
    lb = hg_lb_fwd(e0, e1, name="hg_lb")
    gw = w["hg_gnorm_w"]
    o0, sin0 = hg_scan_fwd(cfg, z, lb[0:1], d_dir=0, name="hg_scan0")
    o1, sin1 = hg_scan_fwd(cfg, z, lb[1:2], d_dir=1, name="hg_scan1")
    onb = hg_read_fwd(o0, o1, z, gw, name="hg_read")
    y1b = mm(onb, w["hg_w_out"][0], name="hg_out")
    x1b, h2b = node_fwd(cfg, x2a, y1b, mt[1], 2, n2[1:2], mt[1], 3, name="node1b")
    ufb, actb, fob = ffn_fwd(1, h2b)
    loss_p, dx2b, dfob, dg2_1, dfnw = final_node(cfg, x1b, fob, mt[1], 5, w["final_norm_w"][None], tgt, name="final_node")

    gr = {}
    dh2b, dwu1, dcw1, dcb1, dwd1 = ffn_bwd(1, dfob, ufb, actb, h2b)
    dx1b, dy1b, dn2_1, dsh2_1, dsc2_1, dg1_1 = node_bwd(cfg, dx2b, dh2b, x1b, y1b, mt[1], 2, n2[1:2], mt[1], 3, name="node1b_bwd")
    don = mm(dy1b, w["hg_w_out"][0], tb=True, name="hg_out_dx")
    gr["hg_w_out"] = mm(onb, dy1b, ta=True, name="hg_out_dw")[None]
    do_, dgate_, dgw = hg_read_bwd(don, o0, o1, z, gw, name="hg_read_bwd")
    dq, dv, dxf, dlb0 = hg_scan_bwd(cfg, do_, z, lb[0:1], sin0, None, None, d_dir=0, name="hg_scan_bwd0")
    dq, dv, dxb, dlb1 = hg_scan_bwd(cfg, do_, z, lb[1:2], sin1, dq, dv, d_dir=1, name="hg_scan_bwd1")
    dz = jnp.concatenate([t_.astype(MXU) for t_ in (dq, dv, dxf, dxb, dgate_)], axis=1)
    dh1b = mm(dz, w["hg_w_in"][0], tb=True, name="hg_in_dx")
    gr["hg_w_in"] = mm(h1b, dz, ta=True, name="hg_in_dw")[None]
    de0, de1 = hg_lb_bwd(e0, e1, jnp.concatenate([dlb0, dlb1], axis=0), name="hg_lb_bwd")
    gr["hg_lower_bounds"] = jnp.stack([de0, de1], axis=1)
    gr["hg_gnorm_w"] = dgw
    dx2a, dfoa, dn1_1, dsh1_1, dsc1_1, dg2_0 = node_bwd(cfg, dx1b, dh1b, x2a, foa, mt[0], 5, n1[1:2], mt[1], 0, name="node1a_bwd")

    dh2a, dwu0, dcw0, dcb0, dwd0 = ffn_bwd(0, dfoa, ufa, acta, h2a)
    dx1a, dy1a, dn2_0, dsh2_0, dsc2_0, dg1_0 = node_bwd(cfg, dx2a, dh2a, x1a, y1a, mt[0], 2, n2[0:1], mt[0], 3, name="node0b_bwd")
    dz2 = mm(dy1a, w["s5_w_out"][0], tb=True, name="s5_out_dx")
    gr["s5_w_out"] = mm(z2, dy1a, ta=True, name="s5_out_dw")[None]

    def glu_b_bwd(dz2_, yp, t):
        zg, sg = _gelu(yp), jax.nn.sigmoid(t)
        return dz2_ * zg * sg * (1.0 - sg), dz2_ * sg

    dtg, dzg_dir = rowmap(glu_b_bwd, [dz2, ypre, tg], [], [(d, MXU), (d, F32)], name="s5_glu_b_bwd")
    dzg_mm = mm(dtg, w["s5_w_glu"][0], tb=True, name="s5_glu_dx")
    gr["s5_w_glu"] = mm(zgb, dtg, ta=True, name="s5_glu_dw")[None]

    def glu_a_bwd(dzd, dzm, yp, u, ds):
        _, vjp = jax.vjp(_gelu, yp)
        (dy,) = vjp(dzd + dzm)
        return dy, dy * ds, jnp.sum(dy * u, axis=0, keepdims=True)

    dyb, du, ddsk = rowmap(glu_a_bwd, [dzg_dir, dzg_mm, ypre, u0], [dsk], [(d, MXU), (d, F32)], [(1, d)], name="s5_glu_a_bwd")
    gr["s5_d"] = ddsk
    dar, dai, dbr, dbi, dcr, dci = [], [], [], [], [], []
    for dd in range(2):
        sre, sim, ere, eim = s5s[dd][:4]
        du, gre, gim, da_r, da_i = s5_scan_bwd(cfg, dyb, *s5s[dd], du, rev=dd == 1, name=f"s5_scan_bwd{dd}")
        dar.append(colsum(da_r, name=f"s5_da_re{dd}").reshape(g_, p_))
        dai.append(colsum(da_i, name=f"s5_da_im{dd}").reshape(g_, p_))
        dbr.append(_diag_b(blockdiag_tn(u0, gre, S5_KIN, S5_KST, name=f"s5_db_re{dd}"), kb))
        dbi.append(_diag_b(blockdiag_tn(u0, gim, S5_KIN, S5_KST, name=f"s5_db_im{dd}"), kb))
        dcr.append(_diag_c(blockdiag_tn(sre.reshape(-1, ns), dyb, S5_KST, S5_KIN, name=f"s5_dc_re{dd}"), kb))
        dci.append(_diag_c(blockdiag_tn(sim.reshape(-1, ns), dyb, S5_KST, S5_KIN, scale=-1.0, name=f"s5_dc_im{dd}"), kb))
    cat = lambda l_: jnp.concatenate(l_, axis=0)
    dlr, dli, dls, dbre, dbim = s5_disc_bwd(*s5p, cat(dar), cat(dai), cat(dbr), cat(dbi), name="s5_disc_bwd")
    gr["s5_lam_re"], gr["s5_lam_im"] = dlr.reshape(1, 2, g_, p_), dli.reshape(1, 2, g_, p_)
    gr["s5_log_step"] = dls.reshape(1, 2, g_)
    gr["s5_b_re"] = dbre.reshape(1, 2, g_, S5_GROUP, p_).transpose(0, 1, 2, 4, 3)
    gr["s5_b_im"] = dbim.reshape(1, 2, g_, S5_GROUP, p_).transpose(0, 1, 2, 4, 3)
    gr["s5_c_re"], gr["s5_c_im"] = jnp.stack(dcr)[None], jnp.stack(dci)[None]
    dh1 = mm(du, w["s5_w_in"][0], tb=True, name="s5_in_dx")
    gr["s5_w_in"] = mm(h1, du, ta=True, name="s5_in_dw")[None]
    dx0, _, dn1_0, dsh1_0, dsc1_0, _ = node_bwd(cfg, dx1a, dh1, x0, None, None, 0, n1[0:1], mt[0], 0, name="node0a_bwd")

    dmt = [jnp.concatenate([dsh1_0, dsc1_0, dg1_0, dsh2_0, dsc2_0, dg2_0], axis=1),
           jnp.concatenate([dsh1_1, dsc1_1, dg1_1, dsh2_1, dsc2_1, dg2_1], axis=1)]
    gr["w_mod"] = jnp.stack([mm(scb, dmt[l], ta=True, name=f"mod_dw{l}") for l in range(2)])
    gr["b_mod"] = jnp.concatenate([colsum(dmt[l], name=f"mod_db{l}") for l in range(2)], axis=0)
    dsc16 = [mm(dmt[l], w["w_mod"][l], tb=True, name=f"mod_dx{l}") for l in range(2)]
    gr["c_ctx"] = cctx_grad(c16, dsc16, name="c_ctx_grad")[0]
    gr["norm1_w"] = jnp.concatenate([dn1_0, dn1_1], axis=0)
    gr["norm2_w"] = jnp.concatenate([dn2_0, dn2_1], axis=0)
    gr["final_norm_w"] = dfnw[0]
    gr["ffn_w_up"], gr["ffn_conv_w"] = jnp.stack([dwu0, dwu1]), jnp.stack([dcw0, dcw1])
    gr["ffn_conv_b"], gr["ffn_w_down"] = jnp.stack([dcb0, dcb1]), jnp.stack([dwd0, dwd1])

    red = _reduce_grads(a, gr, core)
    loss = lax.psum(loss_p[0, 0], ("x", "y", "c"))
    grad_x = dx0[rc:].reshape(seq, nb, d).transpose(1, 0, 2)
    upd = {n: adamw(a[n], red[n], a["m_" + n], a["v_" + n], name="adamw_" + n) for n in WEIGHTS}
    return (loss, grad_x, *[red[n] for n in WEIGHTS], *[upd[n][0] for n in WEIGHTS], *[upd[n][1] for n in WEIGHTS],
            *[upd[n][2] for n in WEIGHTS])
```

```python
import functools
import math

import jax
import jax.numpy as jnp
from jax import lax
from jax.experimental import pallas as pl
from jax.experimental.pallas import tpu as pltpu

F32 = jnp.float32
BF = jnp.bfloat16
MXU = jnp.bfloat16

NORM_EPS = 1e-6
GRID_W = 64
N_MOD = 6
S5_GROUP = 16
S5_STATE = 64
S5_LAM_RE_MAX = -1e-4
S5_KIN = 256
S5_KST = S5_KIN // S5_GROUP * S5_STATE
HEAD = 128
CHUNK_ROWS = 128
N_PROJ = 5
NB = 4
ADAM_LR, ADAM_B1, ADAM_B2, ADAM_EPS, ADAM_WD, ADAM_STEP = 0.001, 0.9, 0.999, 1e-08, 0.01, 10
VMEM_LIMIT = 56 * 1024 * 1024
MESH = pl.DeviceIdType.MESH


def _tile(n, cap):
    if n <= cap:
        return n
    best = None
    for t in range(128, cap + 1, 128):
        if n % t == 0:
            best = t
    assert best is not None, (n, cap)
    return best


def _row_tile(r, width=1024):
    cap = max(8, (512 * 1024) // max(width, 1))
    return next((t for t in (512, 256, 128, 64, 32, 16, 8) if t <= cap and r % t == 0), r)


def _cp(sem):
    return pltpu.CompilerParams(dimension_semantics=sem, vmem_limit_bytes=VMEM_LIMIT)


def _dot(a, b, ca=1, cb=0):
    return lax.dot_general(a.astype(MXU), b.astype(MXU), (((ca,), (cb,)), ((), ())), preferred_element_type=F32)


def _dot3(m, x):
    hi = x.astype(MXU)
    r1 = x - hi.astype(F32)
    mid = r1.astype(MXU)
    lo = (r1 - mid.astype(F32)).astype(MXU)
    return _dot(m, hi) + _dot(m, mid) + _dot(m, lo)


def mm(a, b, *, ta=False, tb=False, out_dtype=F32, name):
    (kd, m) = a.shape if ta else a.shape[::-1]
    (n, kd2) = b.shape if tb else b.shape[::-1]
    assert kd == kd2, (a.shape, b.shape, ta, tb)
    tm, tn, tk = _tile(m, 1024), _tile(n, 1536), _tile(kd, 1024)
    nk = kd // tk

    def body(a_ref, b_ref, o_ref, acc_ref):
        k = pl.program_id(2)

        @pl.when(k == 0)
        def _():
            acc_ref[...] = jnp.zeros_like(acc_ref)

        acc_ref[...] += _dot(a_ref[...], b_ref[...], 0 if ta else 1, 1 if tb else 0)

        @pl.when(k == nk - 1)
        def _():
            o_ref[...] = acc_ref[...].astype(out_dtype)

    a_spec = pl.BlockSpec((tk, tm), lambda i, j, k: (k, i)) if ta else pl.BlockSpec((tm, tk), lambda i, j, k: (i, k))
    b_spec = pl.BlockSpec((tn, tk), lambda i, j, k: (j, k)) if tb else pl.BlockSpec((tk, tn), lambda i, j, k: (k, j))
    return pl.pallas_call(
        body, name=name, grid=(m // tm, n // tn, nk), in_specs=[a_spec, b_spec],
        out_specs=pl.BlockSpec((tm, tn), lambda i, j, k: (i, j)), out_shape=jax.ShapeDtypeStruct((m, n), out_dtype),
        scratch_shapes=[pltpu.VMEM((tm, tn), F32)], compiler_params=_cp(("parallel", "parallel", "arbitrary")))(a, b)


def blockdiag_tn(a, b, wa, wb, *, scale=1.0, name):
    rows = a.shape[0]
    kb = a.shape[1] // wa
    tr = _tile(rows, 1024)
    nr = rows // tr

    def body(a_ref, b_ref, o_ref):
        i = pl.program_id(1)

        @pl.when(i == 0)
        def _():
            o_ref[...] = jnp.zeros_like(o_ref)

        o_ref[0] += scale * _dot(a_ref[...], b_ref[...], 0, 0)

    return pl.pallas_call(
        body, name=name, grid=(kb, nr),
        in_specs=[pl.BlockSpec((tr, wa), lambda k, i: (i, k)), pl.BlockSpec((tr, wb), lambda k, i: (i, k))],
        out_specs=pl.BlockSpec((1, wa, wb), lambda k, i: (k, 0, 0)), out_shape=jax.ShapeDtypeStruct((kb, wa, wb), F32),
        compiler_params=_cp(("parallel", "arbitrary")))(a, b)


def _pat(v, p, op):
    tm, d = v.shape
    return op(v.reshape(tm // 8, 8, d), p[None]).reshape(tm, d)


def _norm_mod(x, nw, shift, scale):
    y = x * lax.rsqrt(jnp.mean(x * x, axis=-1, keepdims=True) + NORM_EPS) * nw
    return _pat(_pat(y, 1.0 + scale, jnp.multiply), shift, jnp.add)


def _mt_spec(d, nct):
    return pl.BlockSpec((8, N_MOD * d), lambda i: (jnp.where(i < nct, 0, 1), 0))


def _acc_spec(d, nct):
    return pl.BlockSpec((8, d), lambda i: (jnp.where(i < nct, 0, 1), 0))


def _rows(cfg):
    tm = min(512, cfg["rc"])
    return tm, cfg["rc"] // tm


def node_fwd(cfg, xp, y, mtg, gi, nw, mtn, si, *, name):
    r, d = xp.shape
    tm, nct = _rows(cfg)
    row = pl.BlockSpec((tm, d), lambda i: (i, 0))
    vec = pl.BlockSpec((1, d), lambda i: (0, 0))

    def body(*refs):
        if y is None:
            xp_ref, nw_ref, mtn_ref, h_ref = refs
            x = xp_ref[...]
        else:
            xp_ref, y_ref, mtg_ref, nw_ref, mtn_ref, xn_ref, h_ref = refs
            x = xp_ref[...] + _pat(y_ref[...], mtg_ref[:, gi * d:(gi + 1) * d], jnp.multiply)
            xn_ref[...] = x
        h_ref[...] = _norm_mod(x, nw_ref[...], mtn_ref[:, si * d:(si + 1) * d], mtn_ref[:, (si + 1) * d:(si + 2) * d]).astype(MXU)

    h_shape = jax.ShapeDtypeStruct((r, d), MXU)
    if y is None:
        h = pl.pallas_call(body, name=name, grid=(r // tm,), in_specs=[row, vec, _mt_spec(d, nct)], out_specs=row,
                           out_shape=h_shape, compiler_params=_cp(("parallel",)))(xp, nw, mtn)
        return xp, h
    return pl.pallas_call(body, name=name, grid=(r // tm,), in_specs=[row, row, _mt_spec(d, nct), vec, _mt_spec(d, nct)],
                          out_specs=(row, row), out_shape=(jax.ShapeDtypeStruct((r, d), F32), h_shape),
                          compiler_params=_cp(("parallel",)))(xp, y, mtg, nw, mtn)


def node_bwd(cfg, dxres, dh, xn, y, mtg, gi, nw, mtn, si, *, name):
    r, d = xn.shape
    tm, nct = _rows(cfg)
    row = pl.BlockSpec((tm, d), lambda i: (i, 0))
    vec = pl.BlockSpec((1, d), lambda i: (0, 0))
    has_y = y is not None

    def body(*refs):
        if has_y:
            dxres_ref, dh_ref, xn_ref, y_ref, mtg_ref, nw_ref, mtn_ref, dxn_ref, dy_ref, dnw_ref, dsh_ref, dsc_ref, dg_ref = refs
        else:
            dxres_ref, dh_ref, xn_ref, nw_ref, mtn_ref, dxn_ref, dnw_ref, dsh_ref, dsc_ref = refs
        i = pl.program_id(0)
        _, vjp = jax.vjp(_norm_mod, xn_ref[...], nw_ref[...], mtn_ref[:, si * d:(si + 1) * d], mtn_ref[:, (si + 1) * d:(si + 2) * d])
        dx, dnw, dsh, dsc = vjp(dh_ref[...])
        dx = dx + dxres_ref[...]
        dxn_ref[...] = dx

        @pl.when(i == 0)
        def _():
            dnw_ref[...] = jnp.zeros_like(dnw_ref)

        @pl.when((i == 0) | (i == nct))
        def _():
            dsh_ref[...] = jnp.zeros_like(dsh_ref)
            dsc_ref[...] = jnp.zeros_like(dsc_ref)
            if has_y:
                dg_ref[...] = jnp.zeros_like(dg_ref)

        dnw_ref[...] += dnw
        dsh_ref[...] += dsh
        dsc_ref[...] += dsc
        if has_y:
            dy_ref[...] = _pat(dx, mtg_ref[:, gi * d:(gi + 1) * d], jnp.multiply).astype(MXU)
            dg_ref[...] += jnp.sum((dx * y_ref[...]).reshape(tm // 8, 8, d), axis=0)

    acc = jax.ShapeDtypeStruct((16, d), F32)
    xs = jax.ShapeDtypeStruct((r, d), F32)
    if has_y:
        return pl.pallas_call(
            body, name=name, grid=(r // tm,), in_specs=[row, row, row, row, _mt_spec(d, nct), vec, _mt_spec(d, nct)],
            out_specs=(row, row, vec, _acc_spec(d, nct), _acc_spec(d, nct), _acc_spec(d, nct)),
            out_shape=(xs, jax.ShapeDtypeStruct((r, d), MXU), jax.ShapeDtypeStruct((1, d), F32), acc, acc, acc),
            compiler_params=_cp(("arbitrary",)))(dxres, dh, xn, y, mtg, nw, mtn)
    dxn, dnw, dsh, dsc = pl.pallas_call(
        body, name=name, grid=(r // tm,), in_specs=[row, row, row, vec, _mt_spec(d, nct)],
        out_specs=(row, vec, _acc_spec(d, nct), _acc_spec(d, nct)),
        out_shape=(xs, jax.ShapeDtypeStruct((1, d), F32), acc, acc), compiler_params=_cp(("arbitrary",)))(dxres, dh, xn, nw, mtn)
    return dxn, None, dnw, dsh, dsc, None


def final_node(cfg, xp, y, mtg, gi, fnw, tgt, *, name):
    r, d = xp.shape
    tm, nct = _rows(cfg)
    row = pl.BlockSpec((tm, d), lambda i: (i, 0))
    vec = pl.BlockSpec((1, d), lambda i: (0, 0))

    def norm(x, w):
        return x * lax.rsqrt(jnp.mean(x * x, axis=-1, keepdims=True) + NORM_EPS) * w

    def body(xp_ref, y_ref, mtg_ref, fnw_ref, tgt_ref, loss_ref, dx_ref, dy_ref, dg_ref, dfnw_ref):
        i = pl.program_id(0)
        g = mtg_ref[:, gi * d:(gi + 1) * d]
        x = xp_ref[...] + _pat(y_ref[...], g, jnp.multiply)
        out, vjp = jax.vjp(norm, x, fnw_ref[...])
        lat = i >= nct
        err = jnp.where(lat, out - tgt_ref[...], 0.0)
        dx, dfnw = vjp(err * (1.0 / d))

        @pl.when(i == 0)
        def _():
            loss_ref[...] = jnp.zeros_like(loss_ref)
            dfnw_ref[...] = jnp.zeros_like(dfnw_ref)

        @pl.when((i == 0) | (i == nct))
        def _():
            dg_ref[...] = jnp.zeros_like(dg_ref)

        loss_ref[...] += jnp.full(loss_ref.shape, 0.5 / d * jnp.sum(err * err), F32)
        dfnw_ref[...] += dfnw
        dx_ref[...] = dx
        dy_ref[...] = _pat(dx, g, jnp.multiply).astype(MXU)
        dg_ref[...] += jnp.sum((dx * y_ref[...]).reshape(tm // 8, 8, d), axis=0)

    return pl.pallas_call(
        body, name=name, grid=(r // tm,),
        in_specs=[row, row, _mt_spec(d, nct), vec, pl.BlockSpec((tm, d), lambda i: (jnp.maximum(i - nct, 0), 0))],
        out_specs=(pl.BlockSpec((8, 128), lambda i: (0, 0)), row, row, _acc_spec(d, nct), vec),
        out_shape=(jax.ShapeDtypeStruct((8, 128), F32), jax.ShapeDtypeStruct((r, d), F32), jax.ShapeDtypeStruct((r, d), MXU),
                   jax.ShapeDtypeStruct((16, d), F32), jax.ShapeDtypeStruct((1, d), F32)),
        compiler_params=_cp(("arbitrary",)))(xp, y, mtg, fnw, tgt)


def _silu(x):
    return x * jax.nn.sigmoid(x)


def mod_fwd(c16, w, b, *, name):
    d, n = w.shape
    tn = _tile(n, 1536)

    def body(c_ref, w_ref, b_ref, o_ref, s_ref):
        s = _silu(c_ref[...])
        s_ref[...] = s.astype(MXU)
        o_ref[...] = _dot(s, w_ref[...]) + b_ref[...]

    return pl.pallas_call(
        body, name=name, grid=(n // tn,),
        in_specs=[pl.BlockSpec((16, d), lambda j: (0, 0)), pl.BlockSpec((d, tn), lambda j: (0, j)), pl.BlockSpec((1, tn), lambda j: (0, j))],
        out_specs=(pl.BlockSpec((16, tn), lambda j: (0, j)), pl.BlockSpec((16, d), lambda j: (0, 0))),
        out_shape=(jax.ShapeDtypeStruct((16, n), F32), jax.ShapeDtypeStruct((16, d), MXU)),
        compiler_params=_cp(("arbitrary",)))(c16, w, b)


def colsum(x, *, name):
    def body(x_ref, o_ref):
        o_ref[...] = jnp.sum(x_ref[...], axis=0, keepdims=True)

    return pl.pallas_call(body, name=name, out_shape=jax.ShapeDtypeStruct((1, x.shape[1]), F32))(x)


def cctx_grad(c16, ds_list, *, name):
    def body(c_ref, *refs):
        o_ref = refs[-1]
        ds = refs[0][...]
        for r_ in refs[1:-1]:
            ds = ds + r_[...]
        _, vjp = jax.vjp(_silu, c_ref[...])
        (dc,) = vjp(ds)
        o_ref[...] = jnp.sum(dc[0:8], axis=0, keepdims=True)

    return pl.pallas_call(body, name=name, out_shape=jax.ShapeDtypeStruct((1, c16.shape[1]), F32))(c16, *ds_list)


def _s5_disc(lam_re, lam_im, log_step, b_re, b_im):
    lr = jnp.minimum(lam_re, S5_LAM_RE_MAX)
    li = lam_im
    dt = jnp.exp(log_step)
    mag = jnp.exp(lr * dt)
    abar_r = mag * jnp.cos(li * dt)
    abar_i = mag * jnp.sin(li * dt)
    den = lr * lr + li * li
    nr = abar_r - 1.0
    coef_r = (nr * lr + abar_i * li) / den
    coef_i = (abar_i * lr - nr * li) / den
    bbar_r = coef_r[:, None, :] * b_re - coef_i[:, None, :] * b_im
    bbar_i = coef_r[:, None, :] * b_im + coef_i[:, None, :] * b_re
    return abar_r, abar_i, bbar_r, bbar_i


def s5_disc_fwd(lam_re, lam_im, log_step, b_re, b_im, *, name):
    def body(lr, li, ls, br, bi, ar_o, ai_o, br_o, bi_o):
        ar_o[...], ai_o[...], br_o[...], bi_o[...] = _s5_disc(lr[...], li[...], ls[...], br[...], bi[...])

    s2, s3 = jax.ShapeDtypeStruct(lam_re.shape, F32), jax.ShapeDtypeStruct(b_re.shape, F32)
    return pl.pallas_call(body, name=name, out_shape=(s2, s2, s3, s3))(lam_re, lam_im, log_step, b_re, b_im)


def s5_disc_bwd(lam_re, lam_im, log_step, b_re, b_im, d_ar, d_ai, d_br, d_bi, *, name):
    def body(lr, li, ls, br, bi, dar, dai, dbr, dbi, o_lr, o_li, o_ls, o_br, o_bi):
        _, vjp = jax.vjp(_s5_disc, lr[...], li[...], ls[...], br[...], bi[...])
        o_lr[...], o_li[...], o_ls[...], o_br[...], o_bi[...] = vjp((dar[...], dai[...], dbr[...], dbi[...]))

    s2, s3 = jax.ShapeDtypeStruct(lam_re.shape, F32), jax.ShapeDtypeStruct(b_re.shape, F32)
    return pl.pallas_call(body, name=name, out_shape=(s2, s2, jax.ShapeDtypeStruct(log_step.shape, F32), s3, s3))(
        lam_re, lam_im, log_step, b_re, b_im, d_ar, d_ai, d_br, d_bi)


S5_LANES = 512


def _chunk_order(k, ncc, nch, rev):
    if not rev:
        return k
    return jnp.where(k < ncc, ncc - 1 - k, nch - 1 - (k - ncc))


def _cmul(ar, ai, xr, xi):
    return ar * xr - ai * xi, ar * xi + ai * xr


S5_FWD_ROWS = 256
S5_BWD_ROWS = 256


def _const_spec(a):
    return pl.BlockSpec(a.shape, lambda k: (0,) * a.ndim, pipeline_mode=pl.Buffered(1))


def _shift_steps(x, edge_tile, back):
    n = x.shape[0]
    row = lax.broadcasted_iota(jnp.int32, (8, x.shape[1]), 0)
    edge = pltpu.roll(edge_tile, 4, 0)
    if back:
        y = pltpu.roll(x, 4, 0)
        return jnp.concatenate([jnp.where(row < 4, edge, y[0:8]), y[8:]], axis=0)
    y = pltpu.roll(x, n - 4, 0)
    return jnp.concatenate([y[:n - 8], jnp.where(row >= 4, edge, y[n - 8:])], axis=0)


def s5_scan_fwd(cfg, u, a2_re, a2_im, bre, bim, abre, abim, cre, cim, *, rev, name):
    r, d = u.shape
    ns = a2_re.shape[1]
    kb = d // S5_KIN
    tcr = S5_FWD_ROWS
    n8 = tcr // 8
    q = S5_FWD_ROWS // S5_BWD_ROWS
    seg = n8 // q
    nch, ncc = r // tcr, cfg["rc"] // tcr
    lw = min(S5_LANES, ns)

    def body(u_ref, ar_ref, ai_ref, bre_ref, bim_ref, abre_ref, abim_ref, cre_ref, cim_ref, sre_ref, sim_ref, ere_ref, eim_ref, y_ref,
             st_re, st_im, u_edge):
        @pl.when(pl.program_id(0) == 0)
        def _():
            st_re[...] = jnp.zeros_like(st_re)
            st_im[...] = jnp.zeros_like(st_im)
            u_edge[...] = jnp.zeros_like(u_edge)

        u_ = u_ref[...]
        ub = u_.astype(MXU)
        upb = _shift_steps(u_, u_edge[...], back=not rev).astype(MXU)
        u_edge[...] = u_[0:8] if rev else u_[tcr - 8:tcr]
        for j in range(kb):
            uj, upj = ub[:, j * S5_KIN:(j + 1) * S5_KIN], upb[:, j * S5_KIN:(j + 1) * S5_KIN]
            sre_ref[:, :, j * S5_KST:(j + 1) * S5_KST] = (_dot(uj, bre_ref[j]) + _dot(upj, abre_ref[j])).reshape(n8, 8, S5_KST)
            sim_ref[:, :, j * S5_KST:(j + 1) * S5_KST] = (_dot(uj, bim_ref[j]) + _dot(upj, abim_ref[j])).reshape(n8, 8, S5_KST)
        for c in range(ns // lw):
            sl = slice(c * lw, (c + 1) * lw)
            ar = jnp.broadcast_to(ar_ref[:, sl], (8, lw))
            ai = jnp.broadcast_to(ai_ref[:, sl], (8, lw))

            def step(i, carry, sl=sl, ar=ar, ai=ai):
                sr, si = carry
                ii = n8 - 1 - i if rev else i
                pr, pi = _cmul(ar, ai, sr, si)
                sr, si = pr + sre_ref[ii, :, sl], pi + sim_ref[ii, :, sl]
                sre_ref[ii, :, sl] = sr
                sim_ref[ii, :, sl] = si
                return sr, si

            sr, si = st_re[:, sl], st_im[:, sl]
            for s_ in range(q):
                at = q - 1 - s_ if rev else s_
                ere_ref[at, :, sl] = sr
                eim_ref[at, :, sl] = si
                sr, si = lax.fori_loop(s_ * seg, (s_ + 1) * seg, step, (sr, si))
            st_re[:, sl] = sr
            st_im[:, sl] = si
        for j in range(kb):
            sr = sre_ref[:, :, j * S5_KST:(j + 1) * S5_KST].reshape(tcr, S5_KST)
            si = sim_ref[:, :, j * S5_KST:(j + 1) * S5_KST].reshape(tcr, S5_KST)
            y_ref[:, j * S5_KIN:(j + 1) * S5_KIN] = _dot(sr, cre_ref[j]) - _dot(si, cim_ref[j])

    cidx = functools.partial(_chunk_order, ncc=ncc, nch=nch, rev=rev)
    full = _const_spec
    st = pl.BlockSpec((n8, 8, ns), lambda k: (cidx(k), 0, 0))
    en = pl.BlockSpec((q, 8, ns), lambda k: (cidx(k), 0, 0))
    return pl.pallas_call(
        body, name=name, grid=(nch,),
        in_specs=[pl.BlockSpec((tcr, d), lambda k: (cidx(k), 0)), full(a2_re), full(a2_im), full(bre), full(bim), full(abre), full(abim),
                  full(cre), full(cim)],
        out_specs=(st, st, en, en, pl.BlockSpec((tcr, d), lambda k: (cidx(k), 0))),
        out_shape=(jax.ShapeDtypeStruct((r // 8, 8, ns), F32),) * 2 + (jax.ShapeDtypeStruct((q * nch, 8, ns), F32),) * 2
        + (jax.ShapeDtypeStruct((r, d), F32),),
        scratch_shapes=[pltpu.VMEM((8, ns), F32), pltpu.VMEM((8, ns), F32), pltpu.VMEM((8, d), F32)],
        compiler_params=_cp(("arbitrary",)))(u, a2_re, a2_im, bre, bim, abre, abim, cre, cim)


def s5_scan_bwd(cfg, dyb, sre, sim, ere, eim, a2_re, a2_im, bre, bim, cre, cim, c2re, c2im, du_in, *, rev, name):
    r, d = dyb.shape
    ns = a2_re.shape[1]
    kb = d // S5_KIN
    tcr = S5_BWD_ROWS
    n8 = tcr // 8
    nch, ncc = r // tcr, cfg["rc"] // tcr
    lw = min(S5_LANES, ns)

    def body(dy_ref, sre_ref, sim_ref, ere_ref, eim_ref, ar_ref, ai_ref, bre_ref, bim_ref, cre_ref, cim_ref, c2re_ref, c2im_ref, duin_ref,
             du_ref, gre_ref, gim_ref, dar_ref, dai_ref, g_re, g_im, gc_re, gc_im, dy_edge):
        k = pl.program_id(0)

        @pl.when(k == 0)
        def _():
            gc_re[...] = jnp.zeros_like(gc_re)
            gc_im[...] = jnp.zeros_like(gc_im)
            dar_ref[...] = jnp.zeros_like(dar_ref)
            dai_ref[...] = jnp.zeros_like(dai_ref)
            dy_edge[...] = jnp.zeros_like(dy_edge)

        dy32 = dy_ref[...].astype(F32)
        dy = dy32.astype(MXU)
        dyn = _shift_steps(dy32, dy_edge[...], back=rev).astype(MXU)
        dy_edge[...] = dy32[tcr - 8:tcr] if rev else dy32[0:8]
        for j in range(kb):
            dyj, dynj = dy[:, j * S5_KIN:(j + 1) * S5_KIN], dyn[:, j * S5_KIN:(j + 1) * S5_KIN]
            g_re[:, :, j * S5_KST:(j + 1) * S5_KST] = (_dot(dyj, cre_ref[j], 1, 1) + _dot(dynj, c2re_ref[j], 1, 1)).reshape(n8, 8, S5_KST)
            g_im[:, :, j * S5_KST:(j + 1) * S5_KST] = -(_dot(dyj, cim_ref[j], 1, 1) + _dot(dynj, c2im_ref[j], 1, 1)).reshape(n8, 8, S5_KST)
        first = lax.broadcasted_iota(jnp.int32, (8, lw), 0) < 4
        if rev:
            first = jnp.logical_not(first)
        for c in range(ns // lw):
            sl = slice(c * lw, (c + 1) * lw)
            ar = jnp.broadcast_to(ar_ref[:, sl], (8, lw))
            nai = -jnp.broadcast_to(ai_ref[:, sl], (8, lw))

            def step(i, carry, sl=sl, ar=ar, nai=nai):
                gr, gi, accr, acci = carry
                ii = i if rev else n8 - 1 - i
                pr, pi = _cmul(ar, nai, gr, gi)
                outr, outi = pr + g_re[ii, :, sl], pi + g_im[ii, :, sl]
                g_re[ii, :, sl] = outr
                g_im[ii, :, sl] = outi
                pv = jnp.clip(ii + 1 if rev else ii - 1, 0, n8 - 1)
                at_entry = (ii == n8 - 1) if rev else (ii == 0)
                pvr = jnp.where(at_entry, ere_ref[0, :, sl], sre_ref[pv, :, sl])
                pvi = jnp.where(at_entry, eim_ref[0, :, sl], sim_ref[pv, :, sl])
                spr = pltpu.roll(jnp.where(first, sre_ref[ii, :, sl], pvr), 4, 0)
                spi = pltpu.roll(jnp.where(first, sim_ref[ii, :, sl], pvi), 4, 0)
                accr = accr + outr * spr + outi * spi
                acci = acci + outi * spr - outr * spi
                return outr, outi, accr, acci

            gr, gi, accr, acci = lax.fori_loop(0, n8, step, (gc_re[:, sl], gc_im[:, sl], dar_ref[:, sl], dai_ref[:, sl]))
            gc_re[:, sl] = gr
            gc_im[:, sl] = gi
            dar_ref[:, sl] = accr
            dai_ref[:, sl] = acci
        for j in range(kb):
            gr = g_re[:, :, j * S5_KST:(j + 1) * S5_KST].reshape(tcr, S5_KST)
            gi = g_im[:, :, j * S5_KST:(j + 1) * S5_KST].reshape(tcr, S5_KST)
            gre_ref[:, j * S5_KST:(j + 1) * S5_KST] = gr.astype(MXU)
            gim_ref[:, j * S5_KST:(j + 1) * S5_KST] = gi.astype(MXU)
            du_ref[:, j * S5_KIN:(j + 1) * S5_KIN] = (duin_ref[:, j * S5_KIN:(j + 1) * S5_KIN]
                                                     + _dot(gr, bre_ref[j], 1, 1) + _dot(gi, bim_ref[j], 1, 1))

    def cidx(k):
        return _chunk_order(nch - 1 - k, ncc, nch, rev)

    full = _const_spec
    st = pl.BlockSpec((n8, 8, ns), lambda k: (cidx(k), 0, 0))
    en = pl.BlockSpec((1, 8, ns), lambda k: (cidx(k), 0, 0))
    rowd = pl.BlockSpec((tcr, d), lambda k: (cidx(k), 0))
    rown = pl.BlockSpec((tcr, ns), lambda k: (cidx(k), 0))
    acc = pl.BlockSpec((8, ns), lambda k: (0, 0))
    return pl.pallas_call(
        body, name=name, grid=(nch,),
        in_specs=[rowd, st, st, en, en, full(a2_re), full(a2_im), full(bre), full(bim), full(cre), full(cim), full(c2re), full(c2im), rowd],
        out_specs=(rowd, rown, rown, acc, acc),
        out_shape=(jax.ShapeDtypeStruct((r, d), F32), jax.ShapeDtypeStruct((r, ns), MXU), jax.ShapeDtypeStruct((r, ns), MXU),
                   jax.ShapeDtypeStruct((8, ns), F32), jax.ShapeDtypeStruct((8, ns), F32)),
        scratch_shapes=[pltpu.VMEM((n8, 8, ns), F32), pltpu.VMEM((n8, 8, ns), F32), pltpu.VMEM((8, ns), F32), pltpu.VMEM((8, ns), F32),
                        pltpu.VMEM((8, d), F32)],
        compiler_params=_cp(("arbitrary",)))(dyb, sre, sim, ere, eim, a2_re, a2_im, bre, bim, cre, cim, c2re, c2im, du_in)


def rowmap(fn, rows_in, vecs_in, outs, accs=(), *, name):
    r = rows_in[0].shape[0]
    tm = _row_tile(r, max(a.shape[1] for a in rows_in))
    nr, nv, no = len(rows_in), len(vecs_in), len(outs)

    def body(*refs):
        ins = [x[...] for x in refs[:nr + nv]]
        res = fn(*ins)
        if not isinstance(res, (tuple, list)):
            res = (res,)
        out_refs = refs[nr + nv:]
        for o_ref, v in zip(out_refs[:no], res[:no]):
            o_ref[...] = v.astype(o_ref.dtype)
        if accs:
            @pl.when(pl.program_id(0) == 0)
            def _():
                for a_ref in out_refs[no:]:
                    a_ref[...] = jnp.zeros_like(a_ref)
            for a_ref, v in zip(out_refs[no:], res[no:]):
                a_ref[...] += v

    in_specs = [pl.BlockSpec((tm, a.shape[1]), lambda i: (i, 0)) for a in rows_in]
    in_specs += [pl.BlockSpec(v.shape, lambda i, n=v.ndim: (0,) * n) for v in vecs_in]
    out_specs = [pl.BlockSpec((tm, w), lambda i: (i, 0)) for w, _ in outs] + [pl.BlockSpec(s, lambda i, n=len(s): (0,) * n) for s in accs]
    out_shape = [jax.ShapeDtypeStruct((r, w), dt) for w, dt in outs] + [jax.ShapeDtypeStruct(s, F32) for s in accs]
    res = pl.pallas_call(body, name=name, grid=(r // tm,), in_specs=in_specs, out_specs=tuple(out_specs), out_shape=tuple(out_shape),
                         compiler_params=_cp(("arbitrary",) if accs else ("parallel",)))(*rows_in, *vecs_in)
    return res


def _gelu(x):
    return jax.nn.gelu(x, approximate=True)


def _hg_lower_bound(e0, e1):
    m = jnp.maximum(e0, e1)
    a, b = jnp.exp(e0 - m), jnp.exp(e1 - m)
    return b / (a + b)


def _hg_gates(x, lb):
    logf = jnp.log(lb + (1.0 - lb) * jax.nn.sigmoid(x))
    return logf, (1.0 - lb) * jax.nn.sigmoid(-x)


def _hg_masks(rev):
    n = CHUNK_ROWS
    rr = lax.broadcasted_iota(jnp.int32, (n, n), 0)
    ss = lax.broadcasted_iota(jnp.int32, (n, n), 1)
    same = (rr % NB) == (ss % NB)
    causal = same & ((ss >= rr) if rev else (ss <= rr))
    anti = same & ((ss <= rr) if rev else (ss >= rr))
    end0 = 0 if rev else n - NB
    pick_end = ss == (end0 + rr % NB)
    return same, causal, anti, pick_end, end0


def _hg_expand(x):
    ex = lax.broadcasted_iota(jnp.int32, x.shape, 0) % NB
    return jnp.concatenate([jnp.where(ex == b, x, 0.0) for b in range(NB)], axis=1)


def _hg_fold(xe):
    kk = xe.shape[1] // NB
    ex = lax.broadcasted_iota(jnp.int32, (xe.shape[0], kk), 0) % NB
    out = jnp.zeros((xe.shape[0], kk), F32)
    for b in range(NB):
        out = out + jnp.where(ex == b, xe[:, b * kk:(b + 1) * kk], 0.0)
    return out


def _hg_chunk(q, v, x, lb, masks):
    same, causal, anti, pick_end, end0 = masks
    logf, kk = _hg_gates(x, lb)
    b = _dot3(causal.astype(MXU), logf)
    bend_t = _dot3(pick_end.astype(MXU), b)
    bend_flat = jnp.concatenate([b[end0 + i:end0 + i + 1] for i in range(NB)], axis=1)
    eb = jnp.exp(b)
    enb = jnp.exp(-b)
    ee = jnp.exp(bend_t - b)
    qd, kd, ke = q * eb, kk * enb, kk * ee
    att = jnp.where(causal, _dot(qd, kd, 1, 1), 0.0)
    decay = jnp.exp(bend_flat)
    return dict(same=same, causal=causal, anti=anti, logf=logf, kk=kk, b=b, eb=eb, enb=enb, ee=ee, qd=qd, kd=kd, ke=ke, att=att,
                decay=decay, qde=_hg_expand(qd), kee=_hg_expand(ke))


def _hg_chunk_order(cfg, r):
    nch, ncc = r // CHUNK_ROWS, cfg["rc"] // CHUNK_ROWS
    return nch, ncc


def hg_scan_fwd(cfg, z, lb, *, d_dir, name):
    r = z.shape[0]
    d = z.shape[1] // N_PROJ
    nh = d // HEAD
    rev = d_dir == 1
    nch, ncc = _hg_chunk_order(cfg, r)
    n = CHUNK_ROWS

    def body(q_ref, v_ref, x_ref, lb_ref, o_ref, sin_ref, stk):
        @pl.when(pl.program_id(0) == 0)
        def _():
            stk[...] = jnp.zeros_like(stk)

        masks = _hg_masks(rev)
        for h in range(nh):
            sl = slice(h * HEAD, (h + 1) * HEAD)
            s0 = stk[h]
            sin_ref[0, h] = s0
            v = v_ref[:, sl]
            c = _hg_chunk(q_ref[:, sl], v, x_ref[:, sl], lb_ref[:, sl], masks)
            o_ref[:, sl] = _dot(c["att"], v) + _dot(c["qde"], s0, 1, 1)
            stk[h] = s0 * c["decay"] + _dot(v, c["kee"], 0, 0)

    def cidx(k):
        return _chunk_order(k, ncc, nch, rev)

    blk = lambda p: pl.BlockSpec((n, d), lambda k: (cidx(k), p))
    return pl.pallas_call(
        body, name=name, grid=(nch,),
        in_specs=[blk(0), blk(1), blk(2 + d_dir), pl.BlockSpec((1, d), lambda k: (0, 0))],
        out_specs=(blk(0), pl.BlockSpec((1, nh, HEAD, NB * HEAD), lambda k: (cidx(k), 0, 0, 0))),
        out_shape=(jax.ShapeDtypeStruct((r, d), F32), jax.ShapeDtypeStruct((nch, nh, HEAD, NB * HEAD), F32)),
        scratch_shapes=[pltpu.VMEM((nh, HEAD, NB * HEAD), F32)], compiler_params=_cp(("arbitrary",)))(z, z, z, lb)


def hg_scan_bwd(cfg, do, z, lb, sin, dq_in, dv_in, *, d_dir, name):
    r = z.shape[0]
    d = z.shape[1] // N_PROJ
    nh = d // HEAD
    rev = d_dir == 1
    nch, ncc = _hg_chunk_order(cfg, r)
    n = CHUNK_ROWS
    has_in = dq_in is not None

    def body(*refs):
        if has_in:
            do_ref, q_ref, v_ref, x_ref, lb_ref, sin_ref, dqi_ref, dvi_ref, dq_ref, dv_ref, dx_ref, dlb_ref, dstk = refs
        else:
            do_ref, q_ref, v_ref, x_ref, lb_ref, sin_ref, dq_ref, dv_ref, dx_ref, dlb_ref, dstk = refs
        @pl.when(pl.program_id(0) == 0)
        def _():
            dstk[...] = jnp.zeros_like(dstk)
            dlb_ref[...] = jnp.zeros_like(dlb_ref)

        masks = _hg_masks(rev)
        ex = lax.broadcasted_iota(jnp.int32, (n, HEAD), 0) % NB
        for h in range(nh):
            sl = slice(h * HEAD, (h + 1) * HEAD)
            do_, q, v, x, lb_, s0, ds1 = do_ref[:, sl], q_ref[:, sl], v_ref[:, sl], x_ref[:, sl], lb_ref[:, sl], sin_ref[0, h], dstk[h]
            c = _hg_chunk(q, v, x, lb_, masks)
            datt = jnp.where(c["causal"], _dot(do_, v, 1, 1), 0.0)
            dv = _dot(c["att"], do_, 0, 0) + _dot(c["kee"], ds1, 1, 1)
            dqd = _dot(datt, c["kd"]) + _hg_fold(_dot(do_, s0))
            dkd = _dot(datt, c["qd"], 0, 0)
            dke = _hg_fold(_dot(v, ds1))
            dbend_flat = jnp.sum(ds1 * s0, axis=0, keepdims=True) * c["decay"]
            dstk[h] = _dot(do_, c["qde"], 0, 0) + ds1 * c["decay"]
            dq = dqd * c["eb"]
            dk = dkd * c["enb"] + dke * c["ee"]
            db = dqd * c["qd"] - dkd * c["kd"] - dke * c["ke"]
            dbend_rows = jnp.zeros((n, HEAD), F32)
            for b in range(NB):
                dbend_rows = dbend_rows + jnp.where(ex == b, dbend_flat[:, b * HEAD:(b + 1) * HEAD], 0.0)
            dlogf = _dot3(c["anti"].astype(MXU), db) + _dot3(c["same"].astype(MXU), dke * c["ke"]) + dbend_rows
            _, vjp = jax.vjp(_hg_gates, x, lb_)
            dx, dlb = vjp((dlogf, dk))
            if has_in:
                dq = dq + dqi_ref[:, sl]
                dv = dv + dvi_ref[:, sl]
            dq_ref[:, sl] = dq
            dv_ref[:, sl] = dv
            dx_ref[:, sl] = dx
            dlb_ref[:, sl] += dlb

    def cidx(k):
        return _chunk_order(nch - 1 - k, ncc, nch, rev)

    blk = lambda p: pl.BlockSpec((n, d), lambda k: (cidx(k), p))
    vec = pl.BlockSpec((1, d), lambda k: (0, 0))
    in_specs = [blk(0), blk(0), blk(1), blk(2 + d_dir), vec, pl.BlockSpec((1, nh, HEAD, NB * HEAD), lambda k: (cidx(k), 0, 0, 0))]
    args = [do, z, z, z, lb, sin]
    if has_in:
        in_specs += [blk(0), blk(0)]
        args += [dq_in, dv_in]
    rd = jax.ShapeDtypeStruct((r, d), F32)
    return pl.pallas_call(
        body, name=name, grid=(nch,), in_specs=in_specs, out_specs=(blk(0), blk(0), blk(0), vec),
        out_shape=(rd, rd, rd, jax.ShapeDtypeStruct((1, d), F32)),
        scratch_shapes=[pltpu.VMEM((nh, HEAD, NB * HEAD), F32)], compiler_params=_cp(("arbitrary",)))(*args)


def _hg_read(o, g, gw):
    on = o * lax.rsqrt(jnp.mean(o * o, axis=-1, keepdims=True) + NORM_EPS) * gw
    return on * jax.nn.sigmoid(g)


def hg_read_fwd(of, ob, z, gw, *, name):
    r, d = of.shape
    nh = d // HEAD
    tm = _row_tile(r)

    def body(of_ref, ob_ref, g_ref, gw_ref, o_ref):
        for h in range(nh):
            sl = slice(h * HEAD, (h + 1) * HEAD)
            o_ref[:, sl] = _hg_read(of_ref[:, sl] + ob_ref[:, sl], g_ref[:, sl], gw_ref[...]).astype(MXU)

    blk = pl.BlockSpec((tm, d), lambda i: (i, 0))
    return pl.pallas_call(
        body, name=name, grid=(r // tm,),
        in_specs=[blk, blk, pl.BlockSpec((tm, d), lambda i: (i, N_PROJ - 1)), pl.BlockSpec((1, HEAD), lambda i: (0, 0))],
        out_specs=blk, out_shape=jax.ShapeDtypeStruct((r, d), MXU), compiler_params=_cp(("parallel",)))(of, ob, z, gw)


def hg_read_bwd(don, of, ob, z, gw, *, name):
    r, d = of.shape
    nh = d // HEAD
    tm = _row_tile(r)

    def body(don_ref, of_ref, ob_ref, g_ref, gw_ref, do_ref, dg_ref, dgw_ref):
        @pl.when(pl.program_id(0) == 0)
        def _():
            dgw_ref[...] = jnp.zeros_like(dgw_ref)

        for h in range(nh):
            sl = slice(h * HEAD, (h + 1) * HEAD)
            _, vjp = jax.vjp(_hg_read, of_ref[:, sl] + ob_ref[:, sl], g_ref[:, sl], gw_ref[...])
            do_ref[:, sl], dg_ref[:, sl], dgw = vjp(don_ref[:, sl])
            dgw_ref[...] += dgw

    blk = pl.BlockSpec((tm, d), lambda i: (i, 0))
    vec = pl.BlockSpec((1, HEAD), lambda i: (0, 0))
    rd = jax.ShapeDtypeStruct((r, d), F32)
    return pl.pallas_call(
        body, name=name, grid=(r // tm,),
        in_specs=[blk, blk, blk, pl.BlockSpec((tm, d), lambda i: (i, N_PROJ - 1)), vec],
        out_specs=(blk, blk, vec), out_shape=(rd, rd, jax.ShapeDtypeStruct((1, HEAD), F32)),
        compiler_params=_cp(("arbitrary",)))(don, of, ob, z, gw)


FFN_COLS = 256


def _seg_masks(cfg, tr, i):
    t = lax.broadcasted_iota(jnp.int32, (tr, FFN_COLS), 0) // NB
    ctx_steps = cfg["rc"] // NB
    pos = jnp.where(i == 0, t % ctx_steps, t % GRID_W)
    last = jnp.where(i == 0, ctx_steps - 1, GRID_W - 1)
    return pos == 0, pos == last


def _prev(x, start):
    return jnp.where(start, 0.0, pltpu.roll(x, NB, 0))


def _next(x, end):
    return jnp.where(end, 0.0, pltpu.roll(x, x.shape[0] - NB, 0))


def _conv3(u, w, b, start, end):
    return ((b + _prev(u, start) * w[0:1]) + u * w[1:2]) + _next(u, end) * w[2:3]


def ffn_mid_fwd(cfg, u, cw, cb, *, name):
    r, f2 = u.shape
    f = f2 // 2
    tr = cfg["rc"]
    nf = f // FFN_COLS

    def body(ua_ref, ug_ref, wa_ref, wg_ref, ba_ref, bg_ref, o_ref):
        start, end = _seg_masks(cfg, tr, pl.program_id(0))
        a = _conv3(ua_ref[...], wa_ref[...], ba_ref[...], start, end)
        g = _conv3(ug_ref[...], wg_ref[...], bg_ref[...], start, end)
        o_ref[...] = (_silu(a) * g).astype(MXU)

    ca = lambda rows: pl.BlockSpec((rows, FFN_COLS), lambda i, j: (i if rows == tr else 0, j))
    cg = lambda rows: pl.BlockSpec((rows, FFN_COLS), lambda i, j: (i if rows == tr else 0, j + nf))
    return pl.pallas_call(
        body, name=name, grid=(r // tr, nf), in_specs=[ca(tr), cg(tr), ca(3), cg(3), ca(1), cg(1)], out_specs=ca(tr),
        out_shape=jax.ShapeDtypeStruct((r, f), MXU), compiler_params=_cp(("parallel", "parallel")))(u, u, cw, cw, cb, cb)


def ffn_mid_bwd(cfg, dact, u, cw, cb, *, name):
    r, f2 = u.shape
    f = f2 // 2
    tr = cfg["rc"]
    nf = f // FFN_COLS

    def body(da_ref, us_ref, up_ref, ws_ref, wp_ref, bs_ref, bp_ref, du_ref, dcw_ref, dcb_ref):
        i = pl.program_id(1)
        is_a = pl.program_id(0) < nf
        start, end = _seg_masks(cfg, tr, i)
        @pl.when(i == 0)
        def _():
            dcw_ref[...] = jnp.zeros_like(dcw_ref)
            dcb_ref[...] = jnp.zeros_like(dcb_ref)

        def finish(dc):
            us, ws = us_ref[...], ws_ref[...]
            du_ref[...] = (ws[1:2] * dc + ws[0:1] * _next(dc, end) + ws[2:3] * _prev(dc, start)).astype(MXU)
            dcw_ref[...] += jnp.concatenate([jnp.sum(dc * _prev(us, start), axis=0, keepdims=True), jnp.sum(dc * us, axis=0, keepdims=True),
                                             jnp.sum(dc * _next(us, end), axis=0, keepdims=True)], axis=0)
            dcb_ref[...] += jnp.sum(dc, axis=0, keepdims=True)

        @pl.when(is_a)
        def _():
            cs = _conv3(us_ref[...], ws_ref[...], bs_ref[...], start, end)
            cp = _conv3(up_ref[...], wp_ref[...], bp_ref[...], start, end)
            sg = jax.nn.sigmoid(cs)
            finish(da_ref[...] * cp * (sg * (1.0 + cs * (1.0 - sg))))

        @pl.when(jnp.logical_not(is_a))
        def _():
            finish(da_ref[...] * _silu(_conv3(up_ref[...], wp_ref[...], bp_ref[...], start, end)))

    cs_ = lambda rows: pl.BlockSpec((rows, FFN_COLS), lambda j, i: (i if rows == tr else 0, j))
    cp_ = lambda rows: pl.BlockSpec((rows, FFN_COLS), lambda j, i: (i if rows == tr else 0, (j + nf) % (2 * nf)))
    return pl.pallas_call(
        body, name=name, grid=(2 * nf, r // tr),
        in_specs=[pl.BlockSpec((tr, FFN_COLS), lambda j, i: (i, j % nf)), cs_(tr), cp_(tr), cs_(3), cp_(3), cs_(1), cp_(1)],
        out_specs=(cs_(tr), cs_(3), cs_(1)),
        out_shape=(jax.ShapeDtypeStruct((r, f2), MXU), jax.ShapeDtypeStruct((3, f2), F32), jax.ShapeDtypeStruct((1, f2), F32)),
        compiler_params=_cp(("parallel", "arbitrary")))(dact, u, u, cw, cw, cb, cb)


def hg_lb_fwd(e0, e1, *, name):
    def body(a, b, o):
        o[...] = _hg_lower_bound(a[...], b[...])

    return pl.pallas_call(body, name=name, out_shape=jax.ShapeDtypeStruct(e0.shape, F32))(e0, e1)


def hg_lb_bwd(e0, e1, dlb, *, name):
    def body(a, b, g, oa, ob):
        _, vjp = jax.vjp(_hg_lower_bound, a[...], b[...])
        oa[...], ob[...] = vjp(g[...])

    s = jax.ShapeDtypeStruct(e0.shape, F32)
    return pl.pallas_call(body, name=name, out_shape=(s, s))(e0, e1, dlb)


def _adamw(w, g, m, v):
    m = ADAM_B1 * m + (1.0 - ADAM_B1) * g
    v = ADAM_B2 * v + (1.0 - ADAM_B2) * jnp.square(g)
    m_hat = m / (1.0 - ADAM_B1 ** ADAM_STEP)
    v_hat = v / (1.0 - ADAM_B2 ** ADAM_STEP)
    delta = -ADAM_LR * (m_hat / (jnp.sqrt(v_hat) + ADAM_EPS) + ADAM_WD * w)
    return delta, m, v


def _as2d(a):
    if a.ndim >= 2 and a.shape[-1] % 128 == 0:
        return a.reshape(-1, a.shape[-1])
    return a.reshape(-1, 128) if a.size % 128 == 0 else a.reshape(1, -1)


def adamw(w, g, m, v, *, name):
    w2 = _as2d(w)
    outs = rowmap(_adamw, [w2, _as2d(g), _as2d(m), _as2d(v)], [], [(w2.shape[1], F32)] * 3, name=name)
    return tuple(o.reshape(w.shape) for o in outs)


HBM_SPEC = pl.BlockSpec(memory_space=pltpu.HBM)


def _place():
    mx, my, mc = lax.axis_index("x"), lax.axis_index("y"), lax.axis_index("c")
    others = [(1 - mx, my), (mx, 1 - my), (1 - mx, 1 - my)]
    return mx, my, mc, others


def chip_allgather(x, *, name):
    def body(x_ref, o_ref, send_sems, recv_sems, local_sem):
        mx, my, mc, others = _place()
        me = 2 * mx + my
        mine = pltpu.make_async_copy(x_ref, o_ref.at[me], local_sem)
        mine.start()
        sends = [pltpu.make_async_remote_copy(src_ref=x_ref, dst_ref=o_ref.at[me], send_sem=send_sems.at[j], recv_sem=recv_sems.at[j],
                                              device_id=(px, py, mc), device_id_type=MESH) for j, (px, py) in enumerate(others)]
        for cp in sends:
            cp.start()
        for j, (px, py) in enumerate(others):
            pltpu.make_async_remote_copy(src_ref=x_ref, dst_ref=o_ref.at[2 * px + py], send_sem=send_sems.at[j], recv_sem=recv_sems.at[j],
                                         device_id=(px, py, mc), device_id_type=MESH).wait_recv()
        for cp in sends:
            cp.wait_send()
        mine.wait()

    return pl.pallas_call(
        body, name=name, out_shape=jax.ShapeDtypeStruct((4,) + x.shape, x.dtype), in_specs=[HBM_SPEC], out_specs=HBM_SPEC,
        scratch_shapes=[pltpu.SemaphoreType.DMA((3,)), pltpu.SemaphoreType.DMA((3,)), pltpu.SemaphoreType.DMA])(x)


def _win(ref, axis, start, size):
    idx = [slice(None)] * len(ref.shape)
    idx[axis] = pl.ds(start, size)
    return ref.at[tuple(idx)]


def _half_axis(shape, ax):
    if shape[0] == 2:
        return 0
    return 2 if ax == 1 else 1


def _cut(shape, axis, parts):
    return shape[:axis] + (shape[axis] // parts,) + shape[axis + 1:]


def _hbm_call(body, arrays, out_shapes, sems, name):
    n_in = len(arrays)
    return pl.pallas_call(body, name=name, out_shape=tuple(out_shapes), in_specs=[HBM_SPEC] * n_in, out_specs=tuple([HBM_SPEC] * len(out_shapes)),
                          scratch_shapes=sems)(*arrays)


def place_shard(shard, ax, chip, dtype, *, name):
    l, r, c = shard.shape
    tr = _row_tile(r, c)
    per_block = (l, r // tr, 1)[ax]

    def omap(li, ri, cref):
        idx = [li, ri, 0]
        idx[ax] = idx[ax] + cref[0] * per_block
        return tuple(idx)

    def body(c_ref, s_ref, o_ref):
        o_ref[...] = s_ref[...].astype(dtype)

    full = shard.shape[:ax] + (4 * shard.shape[ax],) + shard.shape[ax + 1:]
    return pl.pallas_call(
        body, name=name, out_shape=jax.ShapeDtypeStruct(full, dtype),
        grid_spec=pltpu.PrefetchScalarGridSpec(
            num_scalar_prefetch=1, grid=(l, r // tr),
            in_specs=[pl.BlockSpec((1, tr, c), lambda li, ri, cref: (li, ri, 0))], out_specs=pl.BlockSpec((1, tr, c), omap)),
        compiler_params=_cp(("parallel", "parallel")))(chip, shard)


def gather_placed(arrays, axes, haxes, *, name):
    n = len(arrays)

    def body(*refs):
        ins, outs = refs[:n], refs[n:2 * n]
        send_sems, recv_sems = refs[2 * n:]
        mx, my, mc, others = _place()
        me = 2 * mx + my

        def part(ref, i, chip):
            sz, hs = arrays[i].shape[axes[i]] // 4, arrays[i].shape[haxes[i]] // 2
            return _win(_win(ref, axes[i], chip * sz, sz), haxes[i], mc * hs, hs)

        sends = []
        for i in range(n):
            for j, (px, py) in enumerate(others):
                rc = pltpu.make_async_remote_copy(src_ref=part(ins[i], i, me), dst_ref=part(outs[i], i, me), send_sem=send_sems.at[i, j],
                                                  recv_sem=recv_sems.at[i, j], device_id=(px, py, mc), device_id_type=MESH)
                rc.start()
                sends.append(rc)
        for i in range(n):
            for j, (px, py) in enumerate(others):
                pltpu.make_async_remote_copy(src_ref=part(ins[i], i, me), dst_ref=part(outs[i], i, 2 * px + py), send_sem=send_sems.at[i, j],
                                             recv_sem=recv_sems.at[i, j], device_id=(px, py, mc), device_id_type=MESH).wait_recv()
        for rc in sends:
            rc.wait_send()

    return pl.pallas_call(
        body, name=name, out_shape=tuple(jax.ShapeDtypeStruct(a_.shape, a_.dtype) for a_ in arrays), in_specs=[HBM_SPEC] * n,
        out_specs=tuple([HBM_SPEC] * n), input_output_aliases={i: i for i in range(n)},
        scratch_shapes=[pltpu.SemaphoreType.DMA((n, 3)), pltpu.SemaphoreType.DMA((n, 3))])(*arrays)


def pair_swap_halves(arrays, haxes, *, name):
    n = len(arrays)

    def body(*refs):
        ins, outs = refs[:n], refs[n:2 * n]
        send_sems, recv_sems = refs[2 * n:]
        mx, my, mc, _ = _place()
        cps = []
        for i in range(n):
            hs = arrays[i].shape[haxes[i]] // 2
            cp = pltpu.make_async_remote_copy(src_ref=_win(ins[i], haxes[i], (1 - mc) * hs, hs), dst_ref=outs[i], send_sem=send_sems.at[i],
                                              recv_sem=recv_sems.at[i], device_id=(mx, my, 1 - mc), device_id_type=MESH)
            cp.start()
            cps.append(cp)
        for cp in cps:
            cp.wait()

    outs = [jax.ShapeDtypeStruct(_cut(a_.shape, h_, 2), a_.dtype) for a_, h_ in zip(arrays, haxes)]
    return _hbm_call(body, arrays, outs, [pltpu.SemaphoreType.DMA((n,)), pltpu.SemaphoreType.DMA((n,))], name)


def add_own_half(g, t, hax, core, *, out_dtype, name):
    l, r, c = t.shape
    tr = _row_tile(r, c)
    per_half = (l, r // tr, 1)[hax]

    def imap(li, ri, cref):
        idx = [li, ri, 0]
        idx[hax] = idx[hax] + cref[0] * per_half
        return tuple(idx)

    def body(c_ref, g_ref, t_ref, o_ref):
        o_ref[...] = (g_ref[...] + t_ref[...]).astype(out_dtype)

    return pl.pallas_call(
        body, name=name, out_shape=jax.ShapeDtypeStruct(t.shape, out_dtype),
        grid_spec=pltpu.PrefetchScalarGridSpec(
            num_scalar_prefetch=1, grid=(l, r // tr),
            in_specs=[pl.BlockSpec((1, tr, c), imap), pl.BlockSpec((1, tr, c), lambda li, ri, cref: (li, ri, 0))],
            out_specs=pl.BlockSpec((1, tr, c), lambda li, ri, cref: (li, ri, 0))),
        compiler_params=_cp(("parallel", "parallel")))(core, g, t)


def exchange_blocks(arrays, axes, *, name):
    n = len(arrays)

    def body(*refs):
        ins, outs = refs[:n], refs[n:2 * n]
        send_sems, recv_sems, local_sems = refs[2 * n:]
        mx, my, mc, others = _place()
        me = 2 * mx + my
        waits = []
        for i in range(n):
            sz = arrays[i].shape[axes[i]] // 4
            cp = pltpu.make_async_copy(_win(ins[i], axes[i], me * sz, sz), outs[i].at[me], local_sems.at[i])
            cp.start()
            waits.append(cp.wait)
            for j, (px, py) in enumerate(others):
                rc = pltpu.make_async_remote_copy(src_ref=_win(ins[i], axes[i], (2 * px + py) * sz, sz), dst_ref=outs[i].at[me],
                                                  send_sem=send_sems.at[i, j], recv_sem=recv_sems.at[i, j], device_id=(px, py, mc),
                                                  device_id_type=MESH)
                rc.start()
                waits.append(rc.wait_send)
        for i in range(n):
            sz = arrays[i].shape[axes[i]] // 4
            for j, (px, py) in enumerate(others):
                pltpu.make_async_remote_copy(src_ref=_win(ins[i], axes[i], me * sz, sz), dst_ref=outs[i].at[2 * px + py],
                                             send_sem=send_sems.at[i, j], recv_sem=recv_sems.at[i, j], device_id=(px, py, mc),
                                             device_id_type=MESH).wait_recv()
        for w_ in waits:
            w_()

    outs = [jax.ShapeDtypeStruct((4,) + _cut(a_.shape, ax, 4), a_.dtype) for a_, ax in zip(arrays, axes)]
    return _hbm_call(body, arrays, outs, [pltpu.SemaphoreType.DMA((n, 3)), pltpu.SemaphoreType.DMA((n, 3)), pltpu.SemaphoreType.DMA((n,))], name)


def sum_blocks(e, hax, core, *, name):
    _, l, r, c = e.shape
    tr = _row_tile(r, c)
    per_half = (l, r // tr, 1)[hax]

    def omap(li, ri, cref):
        idx = [li, ri, 0]
        idx[hax] = idx[hax] + cref[0] * per_half
        return tuple(idx)

    def body(c_ref, e_ref, o_ref):
        v = e_ref[...].astype(F32)
        o_ref[...] = ((v[0] + v[1]) + v[2]) + v[3]

    full = (l, r, c)[:hax] + (2 * (l, r, c)[hax],) + (l, r, c)[hax + 1:]
    return pl.pallas_call(
        body, name=name, out_shape=jax.ShapeDtypeStruct(full, F32),
        grid_spec=pltpu.PrefetchScalarGridSpec(
            num_scalar_prefetch=1, grid=(l, r // tr),
            in_specs=[pl.BlockSpec((4, 1, tr, c), lambda li, ri, cref: (0, li, ri, 0))], out_specs=pl.BlockSpec((1, tr, c), omap)),
        compiler_params=_cp(("parallel", "parallel")))(core, e)


def pair_fill_halves(arrays, haxes, *, name):
    n = len(arrays)

    def body(*refs):
        ins, outs = refs[:n], refs[n:2 * n]
        send_sems, recv_sems = refs[2 * n:]
        mx, my, mc, _ = _place()
        cps = []
        for i in range(n):
            hs = arrays[i].shape[haxes[i]] // 2
            mine = _win(ins[i], haxes[i], mc * hs, hs)
            cp = pltpu.make_async_remote_copy(src_ref=mine, dst_ref=_win(outs[i], haxes[i], mc * hs, hs), send_sem=send_sems.at[i],
                                              recv_sem=recv_sems.at[i], device_id=(mx, my, 1 - mc), device_id_type=MESH)
            cp.start()
            cps.append(cp)
        for i in range(n):
            hs = arrays[i].shape[haxes[i]] // 2
            pltpu.make_async_remote_copy(src_ref=_win(ins[i], haxes[i], mc * hs, hs), dst_ref=_win(outs[i], haxes[i], (1 - mc) * hs, hs),
                                         send_sem=send_sems.at[i], recv_sem=recv_sems.at[i], device_id=(mx, my, 1 - mc),
                                         device_id_type=MESH).wait_recv()
        for cp in cps:
            cp.wait_send()

    return pl.pallas_call(
        body, name=name, out_shape=tuple(jax.ShapeDtypeStruct(a_.shape, a_.dtype) for a_ in arrays), in_specs=[HBM_SPEC] * n,
        out_specs=tuple([HBM_SPEC] * n), input_output_aliases={i: i for i in range(n)},
        scratch_shapes=[pltpu.SemaphoreType.DMA((n,)), pltpu.SemaphoreType.DMA((n,))])(*arrays)


WEIGHTS = ['c_ctx', 'w_mod', 'b_mod', 'norm1_w', 'norm2_w', 'final_norm_w', 's5_w_in', 's5_lam_re', 's5_lam_im', 's5_log_step', 's5_b_re', 's5_b_im', 's5_c_re', 's5_c_im', 's5_d', 's5_w_glu', 's5_w_out', 'hg_w_in', 'hg_lower_bounds', 'hg_gnorm_w', 'hg_w_out', 'ffn_w_up', 'ffn_conv_w', 'ffn_conv_b', 'ffn_w_down']
INPUTS = ['x', 'c', 'ctx', 'c_ctx', 'w_mod', 'b_mod', 'norm1_w', 'norm2_w', 'final_norm_w', 's5_w_in', 's5_lam_re', 's5_lam_im', 's5_log_step', 's5_b_re', 's5_b_im', 's5_c_re', 's5_c_im', 's5_d', 's5_w_glu', 's5_w_out', 'hg_w_in', 'hg_lower_bounds', 'hg_gnorm_w', 'hg_w_out', 'ffn_w_up', 'ffn_conv_w', 'ffn_conv_b', 'ffn_w_down', 'loss_target', 'm_c_ctx', 'm_w_mod', 'm_b_mod', 'm_norm1_w', 'm_norm2_w', 'm_final_norm_w', 'm_s5_w_in', 'm_s5_lam_re', 'm_s5_lam_im', 'm_s5_log_step', 'm_s5_b_re', 'm_s5_b_im', 'm_s5_c_re', 'm_s5_c_im', 'm_s5_d', 'm_s5_w_glu', 'm_s5_w_out', 'm_hg_w_in', 'm_hg_lower_bounds', 'm_hg_gnorm_w', 'm_hg_w_out', 'm_ffn_w_up', 'm_ffn_conv_w', 'm_ffn_conv_b', 'm_ffn_w_down', 'v_c_ctx', 'v_w_mod', 'v_b_mod', 'v_norm1_w', 'v_norm2_w', 'v_final_norm_w', 'v_s5_w_in', 'v_s5_lam_re', 'v_s5_lam_im', 'v_s5_log_step', 'v_s5_b_re', 'v_s5_b_im', 'v_s5_c_re', 'v_s5_c_im', 'v_s5_d', 'v_s5_w_glu', 'v_s5_w_out', 'v_hg_w_in', 'v_hg_lower_bounds', 'v_hg_gnorm_w', 'v_hg_w_out', 'v_ffn_w_up', 'v_ffn_conv_w', 'v_ffn_conv_b', 'v_ffn_w_down']
SHARD_AXIS = {"w_mod": 2, "s5_w_in": 1, "s5_w_glu": 1, "s5_w_out": 1, "hg_w_in": 2, "hg_lower_bounds": 2, "hg_w_out": 1,
              "ffn_w_up": 2, "ffn_conv_w": 2, "ffn_w_down": 1}
GATHER_F32 = ("hg_lower_bounds", "ffn_conv_w")
PACK_W = 1024
GRAD_WIRE = jnp.bfloat16


def _gather_weights(a, names, chip):
    axes = [SHARD_AXIS[n] for n in names]
    placed = [place_shard(a[n], ax, chip, F32 if n in GATHER_F32 else MXU, name="place_" + n) for n, ax in zip(names, axes)]
    haxes = [_half_axis(p_.shape, ax) for p_, ax in zip(placed, axes)]
    got = gather_placed(placed, axes, haxes, name="allgather_weights")
    return dict(zip(names, pair_fill_halves(got, haxes, name="allgather_pair_fill")))


def _reduce_grads(a, grads, core):
    sharded = [n for n in WEIGHTS if n in SHARD_AXIS]
    small = [n for n in WEIGHTS if n not in SHARD_AXIS]
    flat = jnp.concatenate([grads[n].reshape(-1) for n in small])
    pad = (-flat.shape[0]) % (64 * PACK_W)
    small_pack = jnp.pad(flat, (0, pad)).reshape(1, -1, PACK_W)
    arrays = [grads[n] for n in sharded] + [small_pack]
    axes = [SHARD_AXIS[n] for n in sharded] + [1]
    haxes = [_half_axis(g_.shape, ax) for g_, ax in zip(arrays, axes)]
    tags = sharded + ["small"]
    t = pair_swap_halves(arrays, haxes, name="grad_pair_swap")
    h = [add_own_half(g_, t_, hx, core, out_dtype=GRAD_WIRE, name="grad_pair_add_" + tg) for g_, t_, hx, tg in zip(arrays, t, haxes, tags)]
    e = exchange_blocks(h, axes, name="grad_chip_exchange")
    s = [sum_blocks(e_, hx, core, name="grad_chip_sum_" + tg) for e_, hx, tg in zip(e, haxes, tags)]
    red = pair_fill_halves(s, haxes, name="grad_pair_fill")
    out = dict(zip(sharded, red[:-1]))
    sm = chip_allgather(red[-1][0], name="allgather_small_grads").reshape(-1)
    off = 0
    for n in small:
        out[n] = sm[off:off + math.prod(a[n].shape)].reshape(a[n].shape)
        off += math.prod(a[n].shape)
    return out


def _blockdiag_b(bb, kb):
    gl = S5_KIN // S5_GROUP
    x = bb.reshape(kb, gl, S5_GROUP, S5_STATE)
    return (x[:, :, :, None, :] * jnp.eye(gl, dtype=bb.dtype)[None, :, None, :, None]).reshape(kb, S5_KIN, S5_KST)


def _blockdiag_c(cc, kb):
    gl = S5_KIN // S5_GROUP
    x = cc.reshape(kb, gl, S5_GROUP, S5_STATE).transpose(0, 1, 3, 2)
    return (x[:, :, :, None, :] * jnp.eye(gl, dtype=cc.dtype)[None, :, None, :, None]).reshape(kb, S5_KST, S5_KIN)


def _diag_b(m, kb):
    gl = S5_KIN // S5_GROUP
    x = m.reshape(kb, gl, S5_GROUP, gl, S5_STATE)
    return jnp.stack([x[:, i, :, i, :] for i in range(gl)], axis=1).reshape(kb * gl, S5_GROUP, S5_STATE)


def _diag_c(m, kb):
    gl = S5_KIN // S5_GROUP
    x = m.reshape(kb, gl, S5_STATE, gl, S5_GROUP)
    return jnp.stack([x[:, i, :, i, :] for i in range(gl)], axis=1).transpose(0, 1, 3, 2).reshape(kb * gl, S5_GROUP, S5_STATE)


def kernel(x, c, ctx, c_ctx, w_mod, b_mod, norm1_w, norm2_w, final_norm_w, s5_w_in, s5_lam_re, s5_lam_im, s5_log_step, s5_b_re, s5_b_im, s5_c_re, s5_c_im, s5_d, s5_w_glu, s5_w_out, hg_w_in, hg_lower_bounds, hg_gnorm_w, hg_w_out, ffn_w_up, ffn_conv_w, ffn_conv_b, ffn_w_down, loss_target, m_c_ctx, m_w_mod, m_b_mod, m_norm1_w, m_norm2_w, m_final_norm_w, m_s5_w_in, m_s5_lam_re, m_s5_lam_im, m_s5_log_step, m_s5_b_re, m_s5_b_im, m_s5_c_re, m_s5_c_im, m_s5_d, m_s5_w_glu, m_s5_w_out, m_hg_w_in, m_hg_lower_bounds, m_hg_gnorm_w, m_hg_w_out, m_ffn_w_up, m_ffn_conv_w, m_ffn_conv_b, m_ffn_w_down, v_c_ctx, v_w_mod, v_b_mod, v_norm1_w, v_norm2_w, v_final_norm_w, v_s5_w_in, v_s5_lam_re, v_s5_lam_im, v_s5_log_step, v_s5_b_re, v_s5_b_im, v_s5_c_re, v_s5_c_im, v_s5_d, v_s5_w_glu, v_s5_w_out, v_hg_w_in, v_hg_lower_bounds, v_hg_gnorm_w, v_hg_w_out, v_ffn_w_up, v_ffn_conv_w, v_ffn_conv_b, v_ffn_w_down):
    a = dict(zip(INPUTS, (x, c, ctx, c_ctx, w_mod, b_mod, norm1_w, norm2_w, final_norm_w, s5_w_in, s5_lam_re, s5_lam_im, s5_log_step, s5_b_re, s5_b_im, s5_c_re, s5_c_im, s5_d, s5_w_glu, s5_w_out, hg_w_in, hg_lower_bounds, hg_gnorm_w, hg_w_out, ffn_w_up, ffn_conv_w, ffn_conv_b, ffn_w_down, loss_target, m_c_ctx, m_w_mod, m_b_mod, m_norm1_w, m_norm2_w, m_final_norm_w, m_s5_w_in, m_s5_lam_re, m_s5_lam_im, m_s5_log_step, m_s5_b_re, m_s5_b_im, m_s5_c_re, m_s5_c_im, m_s5_d, m_s5_w_glu, m_s5_w_out, m_hg_w_in, m_hg_lower_bounds, m_hg_gnorm_w, m_hg_w_out, m_ffn_w_up, m_ffn_conv_w, m_ffn_conv_b, m_ffn_w_down, v_c_ctx, v_w_mod, v_b_mod, v_norm1_w, v_norm2_w, v_final_norm_w, v_s5_w_in, v_s5_lam_re, v_s5_lam_im, v_s5_log_step, v_s5_b_re, v_s5_b_im, v_s5_c_re, v_s5_c_im, v_s5_d, v_s5_w_glu, v_s5_w_out, v_hg_w_in, v_hg_lower_bounds, v_hg_gnorm_w, v_hg_w_out, v_ffn_w_up, v_ffn_conv_w, v_ffn_conv_b, v_ffn_w_down)))
    nb, seq, d = x.shape
    assert nb == NB
    rc = nb * ctx.shape[1]
    cfg = {"rc": rc}
    f = a["ffn_w_down"].shape[1] * 4
    core = lax.axis_index("c").astype(jnp.int32).reshape(1)

    w = {n: a[n] for n in WEIGHTS if n not in SHARD_AXIS}
    chip = (2 * lax.axis_index("x") + lax.axis_index("y")).astype(jnp.int32).reshape(1)
    w.update(_gather_weights(a, [n for n in WEIGHTS if n in SHARD_AXIS], chip))

    tmaj = lambda t: t.transpose(1, 0, 2).reshape(-1, t.shape[-1])
    x0 = jnp.concatenate([tmaj(ctx), tmaj(x)], axis=0)
    tgt = tmaj(a["loss_target"])
    c16 = jnp.concatenate([jnp.broadcast_to(c_ctx[None], (8, d)), c, c], axis=0)
    mt, scb = [], None
    for l in range(2):
        m_, scb = mod_fwd(c16, w["w_mod"][l], w["b_mod"][l][None], name=f"mod_fwd{l}")
        mt.append(m_)
    n1, n2 = w["norm1_w"], w["norm2_w"]

    def ffn_fwd(l, h):
        u = mm(h, w["ffn_w_up"][l], name=f"ffn_up{l}")
        act = ffn_mid_fwd(cfg, u, w["ffn_conv_w"][l], w["ffn_conv_b"][l][None], name=f"ffn_mid{l}")
        return u, act, mm(act, w["ffn_w_down"][l], name=f"ffn_down{l}")

    def ffn_bwd(l, dfo, u, act, h):
        dact = mm(dfo, w["ffn_w_down"][l], tb=True, name=f"ffn_down_dx{l}")
        dwd = mm(act, dfo, ta=True, name=f"ffn_down_dw{l}")
        du, dcw, dcb = ffn_mid_bwd(cfg, dact, u, w["ffn_conv_w"][l], w["ffn_conv_b"][l][None], name=f"ffn_mid_bwd{l}")
        dh = mm(du, w["ffn_w_up"][l], tb=True, name=f"ffn_up_dx{l}")
        dwu = mm(h, du, ta=True, name=f"ffn_up_dw{l}")
        return dh, dwu, dcw, dcb[0], dwd

    g_, p_ = d // S5_GROUP, S5_STATE
    ns, kb = g_ * p_, d // S5_KIN
    s5p = (w["s5_lam_re"][0].reshape(2 * g_, p_), w["s5_lam_im"][0].reshape(2 * g_, p_), w["s5_log_step"][0].reshape(2 * g_, 1),
           w["s5_b_re"][0].transpose(0, 1, 3, 2).reshape(2 * g_, S5_GROUP, p_), w["s5_b_im"][0].transpose(0, 1, 3, 2).reshape(2 * g_, S5_GROUP, p_))
    ar, ai, bbr, bbi = s5_disc_fwd(*s5p, name="s5_disc")
    dsk = w["s5_d"]
    _, h1 = node_fwd(cfg, x0, None, None, 0, n1[0:1], mt[0], 0, name="node0a")
    u0 = mm(h1, w["s5_w_in"][0], name="s5_in")
    s5s, ys = [], []
    for dd in range(2):
        sl = slice(dd * g_, (dd + 1) * g_)
        a_r, a_i = ar[sl].reshape(1, ns), ai[sl].reshape(1, ns)
        a2 = (a_r * a_r - a_i * a_i, 2.0 * a_r * a_i)
        b_r, b_i = _blockdiag_b(bbr[sl], kb), _blockdiag_b(bbi[sl], kb)
        c_r, c_i = _blockdiag_c(w["s5_c_re"][0, dd], kb), _blockdiag_c(w["s5_c_im"][0, dd], kb)
        ak, ai_k = a_r.reshape(kb, 1, S5_KST), a_i.reshape(kb, 1, S5_KST)
        ab = (ak * b_r - ai_k * b_i, ak * b_i + ai_k * b_r)
        akc, aic = ak.reshape(kb, S5_KST, 1), ai_k.reshape(kb, S5_KST, 1)
        c2 = (akc * c_r - aic * c_i, akc * c_i + aic * c_r)
        bf = lambda t_: t_.astype(MXU)
        sre, sim, ere, eim, y_ = s5_scan_fwd(cfg, u0, a2[0], a2[1], bf(b_r), bf(b_i), bf(ab[0]), bf(ab[1]), bf(c_r), bf(c_i), rev=dd == 1,
                                             name=f"s5_scan{dd}")
        s5s.append((sre, sim, ere, eim, a2[0], a2[1], bf(b_r), bf(b_i), bf(c_r), bf(c_i), bf(c2[0]), bf(c2[1])))
        ys.append(y_)

    def glu_a(u, y0, y1, ds):
        yp = (ds * u + y0) + y1
        return yp, _gelu(yp)

    ypre, zgb = rowmap(glu_a, [u0, ys[0], ys[1]], [dsk], [(d, F32), (d, MXU)], name="s5_glu_a")
    tg = mm(zgb, w["s5_w_glu"][0], name="s5_glu")
    (z2,) = rowmap(lambda yp, t: _gelu(yp) * jax.nn.sigmoid(t), [ypre, tg], [], [(d, MXU)], name="s5_glu_b")
    y1a = mm(z2, w["s5_w_out"][0], name="s5_out")
    x1a, h2a = node_fwd(cfg, x0, y1a, mt[0], 2, n2[0:1], mt[0], 3, name="node0b")
    ufa, acta, foa = ffn_fwd(0, h2a)

    x2a, h1b = node_fwd(cfg, x1a, foa, mt[0], 5, n1[1:2], mt[1], 0, name="node1a")
    z = mm(h1b, w["hg_w_in"][0], name="hg_in")
    e0, e1 = w["hg_lower_bounds"][:, 0, :], w["hg_lower_bounds"][:, 1, :]
    lb = hg_lb_fwd(e0, e1, name="hg_lb")
    gw = w["hg_gnorm_w"]
    o0, sin0 = hg_scan_fwd(cfg, z, lb[0:1], d_dir=0, name="hg_scan0")
    o1, sin1 = hg_scan_fwd(cfg, z, lb[1:2], d_dir=1, name="hg_scan1")
    onb = hg_read_fwd(o0, o1, z, gw, name="hg_read")
    y1b = mm(onb, w["hg_w_out"][0], name="hg_out")
    x1b, h2b = node_fwd(cfg, x2a, y1b, mt[1], 2, n2[1:2], mt[1], 3, name="node1b")
    ufb, actb, fob = ffn_fwd(1, h2b)
    loss_p, dx2b, dfob, dg2_1, dfnw = final_node(cfg, x1b, fob, mt[1], 5, w["final_norm_w"][None], tgt, name="final_node")

    gr = {}
    dh2b, dwu1, dcw1, dcb1, dwd1 = ffn_bwd(1, dfob, ufb, actb, h2b)
    dx1b, dy1b, dn2_1, dsh2_1, dsc2_1, dg1_1 = node_bwd(cfg, dx2b, dh2b, x1b, y1b, mt[1], 2, n2[1:2], mt[1], 3, name="node1b_bwd")
    don = mm(dy1b, w["hg_w_out"][0], tb=True, name="hg_out_dx")
    gr["hg_w_out"] = mm(onb, dy1b, ta=True, name="hg_out_dw")[None]
    do_, dgate_, dgw = hg_read_bwd(don, o0, o1, z, gw, name="hg_read_bwd")
    dq, dv, dxf, dlb0 = hg_scan_bwd(cfg, do_, z, lb[0:1], sin0, None, None, d_dir=0, name="hg_scan_bwd0")
    dq, dv, dxb, dlb1 = hg_scan_bwd(cfg, do_, z, lb[1:2], sin1, dq, dv, d_dir=1, name="hg_scan_bwd1")
    dz = jnp.concatenate([t_.astype(MXU) for t_ in (dq, dv, dxf, dxb, dgate_)], axis=1)
    dh1b = mm(dz, w["hg_w_in"][0], tb=True, name="hg_in_dx")
    gr["hg_w_in"] = mm(h1b, dz, ta=True, name="hg_in_dw")[None]
    de0, de1 = hg_lb_bwd(e0, e1, jnp.concatenate([dlb0, dlb1], axis=0), name="hg_lb_bwd")
    gr["hg_lower_bounds"] = jnp.stack([de0, de1], axis=1)
    gr["hg_gnorm_w"] = dgw
    dx2a, dfoa, dn1_1, dsh1_1, dsc1_1, dg2_0 = node_bwd(cfg, dx1b, dh1b, x2a, foa, mt[0], 5, n1[1:2], mt[1], 0, name="node1a_bwd")

    dh2a, dwu0, dcw0, dcb0, dwd0 = ffn_bwd(0, dfoa, ufa, acta, h2a)
    dx1a, dy1a, dn2_0, dsh2_0, dsc2_0, dg1_0 = node_bwd(cfg, dx2a, dh2a, x1a, y1a, mt[0], 2, n2[0:1], mt[0], 3, name="node0b_bwd")
    dz2 = mm(dy1a, w["s5_w_out"][0], tb=True, name="s5_out_dx")
    gr["s5_w_out"] = mm(z2, dy1a, ta=True, name="s5_out_dw")[None]

    def glu_b_bwd(dz2_, yp, t):
        zg, sg = _gelu(yp), jax.nn.sigmoid(t)
        return dz2_ * zg * sg * (1.0 - sg), dz2_ * sg

    dtg, dzg_dir = rowmap(glu_b_bwd, [dz2, ypre, tg], [], [(d, MXU), (d, F32)], name="s5_glu_b_bwd")
    dzg_mm = mm(dtg, w["s5_w_glu"][0], tb=True, name="s5_glu_dx")
    gr["s5_w_glu"] = mm(zgb, dtg, ta=True, name="s5_glu_dw")[None]

    def glu_a_bwd(dzd, dzm, yp, u, ds):
        _, vjp = jax.vjp(_gelu, yp)
        (dy,) = vjp(dzd + dzm)
        return dy, dy * ds, jnp.sum(dy * u, axis=0, keepdims=True)

    dyb, du, ddsk = rowmap(glu_a_bwd, [dzg_dir, dzg_mm, ypre, u0], [dsk], [(d, MXU), (d, F32)], [(1, d)], name="s5_glu_a_bwd")
    gr["s5_d"] = ddsk
    dar, dai, dbr, dbi, dcr, dci = [], [], [], [], [], []
    for dd in range(2):
        sre, sim, ere, eim = s5s[dd][:4]
        du, gre, gim, da_r, da_i = s5_scan_bwd(cfg, dyb, *s5s[dd], du, rev=dd == 1, name=f"s5_scan_bwd{dd}")
        dar.append(colsum(da_r, name=f"s5_da_re{dd}").reshape(g_, p_))
        dai.append(colsum(da_i, name=f"s5_da_im{dd}").reshape(g_, p_))
        dbr.append(_diag_b(blockdiag_tn(u0, gre, S5_KIN, S5_KST, name=f"s5_db_re{dd}"), kb))
        dbi.append(_diag_b(blockdiag_tn(u0, gim, S5_KIN, S5_KST, name=f"s5_db_im{dd}"), kb))
        dcr.append(_diag_c(blockdiag_tn(sre.reshape(-1, ns), dyb, S5_KST, S5_KIN, name=f"s5_dc_re{dd}"), kb))
        dci.append(_diag_c(blockdiag_tn(sim.reshape(-1, ns), dyb, S5_KST, S5_KIN, scale=-1.0, name=f"s5_dc_im{dd}"), kb))
    cat = lambda l_: jnp.concatenate(l_, axis=0)
    dlr, dli, dls, dbre, dbim = s5_disc_bwd(*s5p, cat(dar), cat(dai), cat(dbr), cat(dbi), name="s5_disc_bwd")
    gr["s5_lam_re"], gr["s5_lam_im"] = dlr.reshape(1, 2, g_, p_), dli.reshape(1, 2, g_, p_)
    gr["s5_log_step"] = dls.reshape(1, 2, g_)
    gr["s5_b_re"] = dbre.reshape(1, 2, g_, S5_GROUP, p_).transpose(0, 1, 2, 4, 3)
    gr["s5_b_im"] = dbim.reshape(1, 2, g_, S5_GROUP, p_).transpose(0, 1, 2, 4, 3)
    gr["s5_c_re"], gr["s5_c_im"] = jnp.stack(dcr)[None], jnp.stack(dci)[None]
    dh1 = mm(du, w["s5_w_in"][0], tb=True, name="s5_in_dx")
    gr["s5_w_in"] = mm(h1, du, ta=True, name="s5_in_dw")[None]
    dx0, _, dn1_0, dsh1_0, dsc1_0, _ = node_bwd(cfg, dx1a, dh1, x0, None, None, 0, n1[0:1], mt[0], 0, name="node0a_bwd")

    dmt = [jnp.concatenate([dsh1_0, dsc1_0, dg1_0, dsh2_0, dsc2_0, dg2_0], axis=1),
           jnp.concatenate([dsh1_1, dsc1_1, dg1_1, dsh2_1, dsc2_1, dg2_1], axis=1)]
    gr["w_mod"] = jnp.stack([mm(scb, dmt[l], ta=True, name=f"mod_dw{l}") for l in range(2)])
    gr["b_mod"] = jnp.concatenate([colsum(dmt[l], name=f"mod_db{l}") for l in range(2)], axis=0)
    dsc16 = [mm(dmt[l], w["w_mod"][l], tb=True, name=f"mod_dx{l}") for l in range(2)]
    gr["c_ctx"] = cctx_grad(c16, dsc16, name="c_ctx_grad")[0]
    gr["norm1_w"] = jnp.concatenate([dn1_0, dn1_1], axis=0)
    gr["norm2_w"] = jnp.concatenate([dn2_0, dn2_1], axis=0)
    gr["final_norm_w"] = dfnw[0]
    gr["ffn_w_up"], gr["ffn_conv_w"] = jnp.stack([dwu0, dwu1]), jnp.stack([dcw0, dcw1])
    gr["ffn_conv_b"], gr["ffn_w_down"] = jnp.stack([dcb0, dcb1]), jnp.stack([dwd0, dwd1])

    red = _reduce_grads(a, gr, core)
    loss = lax.psum(loss_p[0, 0], ("x", "y", "c"))
    grad_x = dx0[rc:].reshape(seq, nb, d).transpose(1, 0, 2)
    upd = {n: adamw(a[n], red[n], a["m_" + n], a["v_" + n], name="adamw_" + n) for n in WEIGHTS}
    return (loss, grad_x, *[red[n] for n in WEIGHTS], *[upd[n][0] for n in WEIGHTS], *[upd[n][1] for n in WEIGHTS],
            *[upd[n][2] for n in WEIGHTS])
```

```python
import functools
import math

import jax
import jax.numpy as jnp
from jax import lax
from jax.experimental import pallas as pl
from jax.experimental.pallas import tpu as pltpu

F32 = jnp.float32
BF = jnp.bfloat16
MXU = jnp.bfloat16

NORM_EPS = 1e-6
GRID_W = 64
N_MOD = 6
S5_GROUP = 16
S5_STATE = 64
S5_LAM_RE_MAX = -1e-4
S5_KIN = 256
S5_KST = S5_KIN // S5_GROUP * S5_STATE
HEAD = 128
CHUNK_ROWS = 128
N_PROJ = 5
NB = 4
ADAM_LR, ADAM_B1, ADAM_B2, ADAM_EPS, ADAM_WD, ADAM_STEP = 0.001, 0.9, 0.999, 1e-08, 0.01, 10
VMEM_LIMIT = 56 * 1024 * 1024
MESH = pl.DeviceIdType.MESH


def _tile(n, cap):
    if n <= cap:
        return n
    best = None
    for t in range(128, cap + 1, 128):
        if n % t == 0:
            best = t
    assert best is not None, (n, cap)
    return best


def _row_tile(r, width=1024):
    cap = max(8, (512 * 1024) // max(width, 1))
    return next((t for t in (512, 256, 128, 64, 32, 16, 8) if t <= cap and r % t == 0), r)


def _cp(sem):
    return pltpu.CompilerParams(dimension_semantics=sem, vmem_limit_bytes=VMEM_LIMIT)


def _dot(a, b, ca=1, cb=0):
    return lax.dot_general(a.astype(MXU), b.astype(MXU), (((ca,), (cb,)), ((), ())), preferred_element_type=F32)


def _dot3(m, x):
    hi = x.astype(MXU)
    r1 = x - hi.astype(F32)
    mid = r1.astype(MXU)
    lo = (r1 - mid.astype(F32)).astype(MXU)
    return _dot(m, hi) + _dot(m, mid) + _dot(m, lo)


def mm(a, b, *, ta=False, tb=False, out_dtype=F32, name):
    (kd, m) = a.shape if ta else a.shape[::-1]
    (n, kd2) = b.shape if tb else b.shape[::-1]
    assert kd == kd2, (a.shape, b.shape, ta, tb)
    tm, tn, tk = _tile(m, 1024), _tile(n, 1536), _tile(kd, 1024)
    nk = kd // tk

    def body(a_ref, b_ref, o_ref, acc_ref):
        k = pl.program_id(2)

        @pl.when(k == 0)
        def _():
            acc_ref[...] = jnp.zeros_like(acc_ref)

        acc_ref[...] += _dot(a_ref[...], b_ref[...], 0 if ta else 1, 1 if tb else 0)

        @pl.when(k == nk - 1)
        def _():
            o_ref[...] = acc_ref[...].astype(out_dtype)

    a_spec = pl.BlockSpec((tk, tm), lambda i, j, k: (k, i)) if ta else pl.BlockSpec((tm, tk), lambda i, j, k: (i, k))
    b_spec = pl.BlockSpec((tn, tk), lambda i, j, k: (j, k)) if tb else pl.BlockSpec((tk, tn), lambda i, j, k: (k, j))
    return pl.pallas_call(
        body, name=name, grid=(m // tm, n // tn, nk), in_specs=[a_spec, b_spec],
        out_specs=pl.BlockSpec((tm, tn), lambda i, j, k: (i, j)), out_shape=jax.ShapeDtypeStruct((m, n), out_dtype),
        scratch_shapes=[pltpu.VMEM((tm, tn), F32)], compiler_params=_cp(("parallel", "parallel", "arbitrary")))(a, b)


def blockdiag_tn(a, b, wa, wb, *, scale=1.0, name):
    rows = a.shape[0]
    kb = a.shape[1] // wa
    tr = _tile(rows, 1024)
    nr = rows // tr

    def body(a_ref, b_ref, o_ref):
        i = pl.program_id(1)

        @pl.when(i == 0)
        def _():
            o_ref[...] = jnp.zeros_like(o_ref)

        o_ref[0] += scale * _dot(a_ref[...], b_ref[...], 0, 0)

    return pl.pallas_call(
        body, name=name, grid=(kb, nr),
        in_specs=[pl.BlockSpec((tr, wa), lambda k, i: (i, k)), pl.BlockSpec((tr, wb), lambda k, i: (i, k))],
        out_specs=pl.BlockSpec((1, wa, wb), lambda k, i: (k, 0, 0)), out_shape=jax.ShapeDtypeStruct((kb, wa, wb), F32),
        compiler_params=_cp(("parallel", "arbitrary")))(a, b)


def _pat(v, p, op):
    tm, d = v.shape
    return op(v.reshape(tm // 8, 8, d), p[None]).reshape(tm, d)


def _norm_mod(x, nw, shift, scale):
    y = x * lax.rsqrt(jnp.mean(x * x, axis=-1, keepdims=True) + NORM_EPS) * nw
    return _pat(_pat(y, 1.0 + scale, jnp.multiply), shift, jnp.add)


def _mt_spec(d, nct):
    return pl.BlockSpec((8, N_MOD * d), lambda i: (jnp.where(i < nct, 0, 1), 0))


def _acc_spec(d, nct):
    return pl.BlockSpec((8, d), lambda i: (jnp.where(i < nct, 0, 1), 0))


def _rows(cfg):
    tm = min(512, cfg["rc"])
    return tm, cfg["rc"] // tm


def node_fwd(cfg, xp, y, mtg, gi, nw, mtn, si, *, name):
    r, d = xp.shape
    tm, nct = _rows(cfg)
    row = pl.BlockSpec((tm, d), lambda i: (i, 0))
    vec = pl.BlockSpec((1, d), lambda i: (0, 0))

    def body(*refs):
        if y is None:
            xp_ref, nw_ref, mtn_ref, h_ref = refs
            x = xp_ref[...]
        else:
            xp_ref, y_ref, mtg_ref, nw_ref, mtn_ref, xn_ref, h_ref = refs
            x = xp_ref[...] + _pat(y_ref[...], mtg_ref[:, gi * d:(gi + 1) * d], jnp.multiply)
            xn_ref[...] = x
        h_ref[...] = _norm_mod(x, nw_ref[...], mtn_ref[:, si * d:(si + 1) * d], mtn_ref[:, (si + 1) * d:(si + 2) * d]).astype(MXU)

    h_shape = jax.ShapeDtypeStruct((r, d), MXU)
    if y is None:
        h = pl.pallas_call(body, name=name, grid=(r // tm,), in_specs=[row, vec, _mt_spec(d, nct)], out_specs=row,
                           out_shape=h_shape, compiler_params=_cp(("parallel",)))(xp, nw, mtn)
        return xp, h
    return pl.pallas_call(body, name=name, grid=(r // tm,), in_specs=[row, row, _mt_spec(d, nct), vec, _mt_spec(d, nct)],
                          out_specs=(row, row), out_shape=(jax.ShapeDtypeStruct((r, d), F32), h_shape),
                          compiler_params=_cp(("parallel",)))(xp, y, mtg, nw, mtn)


def node_bwd(cfg, dxres, dh, xn, y, mtg, gi, nw, mtn, si, *, name):
    r, d = xn.shape
    tm, nct = _rows(cfg)
    row = pl.BlockSpec((tm, d), lambda i: (i, 0))
    vec = pl.BlockSpec((1, d), lambda i: (0, 0))
    has_y = y is not None

    def body(*refs):
        if has_y:
            dxres_ref, dh_ref, xn_ref, y_ref, mtg_ref, nw_ref, mtn_ref, dxn_ref, dy_ref, dnw_ref, dsh_ref, dsc_ref, dg_ref = refs
        else:
            dxres_ref, dh_ref, xn_ref, nw_ref, mtn_ref, dxn_ref, dnw_ref, dsh_ref, dsc_ref = refs
        i = pl.program_id(0)
        _, vjp = jax.vjp(_norm_mod, xn_ref[...], nw_ref[...], mtn_ref[:, si * d:(si + 1) * d], mtn_ref[:, (si + 1) * d:(si + 2) * d])
        dx, dnw, dsh, dsc = vjp(dh_ref[...])
        dx = dx + dxres_ref[...]
        dxn_ref[...] = dx

        @pl.when(i == 0)
        def _():
            dnw_ref[...] = jnp.zeros_like(dnw_ref)

        @pl.when((i == 0) | (i == nct))
        def _():
            dsh_ref[...] = jnp.zeros_like(dsh_ref)
            dsc_ref[...] = jnp.zeros_like(dsc_ref)
            if has_y:
                dg_ref[...] = jnp.zeros_like(dg_ref)

        dnw_ref[...] += dnw
        dsh_ref[...] += dsh
        dsc_ref[...] += dsc
        if has_y:
            dy_ref[...] = _pat(dx, mtg_ref[:, gi * d:(gi + 1) * d], jnp.multiply).astype(MXU)
            dg_ref[...] += jnp.sum((dx * y_ref[...]).reshape(tm // 8, 8, d), axis=0)

    acc = jax.ShapeDtypeStruct((16, d), F32)
    xs = jax.ShapeDtypeStruct((r, d), F32)
    if has_y:
        return pl.pallas_call(
            body, name=name, grid=(r // tm,), in_specs=[row, row, row, row, _mt_spec(d, nct), vec, _mt_spec(d, nct)],
            out_specs=(row, row, vec, _acc_spec(d, nct), _acc_spec(d, nct), _acc_spec(d, nct)),
            out_shape=(xs, jax.ShapeDtypeStruct((r, d), MXU), jax.ShapeDtypeStruct((1, d), F32), acc, acc, acc),
            compiler_params=_cp(("arbitrary",)))(dxres, dh, xn, y, mtg, nw, mtn)
    dxn, dnw, dsh, dsc = pl.pallas_call(
        body, name=name, grid=(r // tm,), in_specs=[row, row, row, vec, _mt_spec(d, nct)],
        out_specs=(row, vec, _acc_spec(d, nct), _acc_spec(d, nct)),
        out_shape=(xs, jax.ShapeDtypeStruct((1, d), F32), acc, acc), compiler_params=_cp(("arbitrary",)))(dxres, dh, xn, nw, mtn)
    return dxn, None, dnw, dsh, dsc, None


def final_node(cfg, xp, y, mtg, gi, fnw, tgt, *, name):
    r, d = xp.shape
    tm, nct = _rows(cfg)
    row = pl.BlockSpec((tm, d), lambda i: (i, 0))
    vec = pl.BlockSpec((1, d), lambda i: (0, 0))

    def norm(x, w):
        return x * lax.rsqrt(jnp.mean(x * x, axis=-1, keepdims=True) + NORM_EPS) * w

    def body(xp_ref, y_ref, mtg_ref, fnw_ref, tgt_ref, loss_ref, dx_ref, dy_ref, dg_ref, dfnw_ref):
        i = pl.program_id(0)
        g = mtg_ref[:, gi * d:(gi + 1) * d]
        x = xp_ref[...] + _pat(y_ref[...], g, jnp.multiply)
        out, vjp = jax.vjp(norm, x, fnw_ref[...])
        lat = i >= nct
        err = jnp.where(lat, out - tgt_ref[...], 0.0)
        dx, dfnw = vjp(err * (1.0 / d))

        @pl.when(i == 0)
        def _():
            loss_ref[...] = jnp.zeros_like(loss_ref)
            dfnw_ref[...] = jnp.zeros_like(dfnw_ref)

        @pl.when((i == 0) | (i == nct))
        def _():
            dg_ref[...] = jnp.zeros_like(dg_ref)

        loss_ref[...] += jnp.full(loss_ref.shape, 0.5 / d * jnp.sum(err * err), F32)
        dfnw_ref[...] += dfnw
        dx_ref[...] = dx
        dy_ref[...] = _pat(dx, g, jnp.multiply).astype(MXU)
        dg_ref[...] += jnp.sum((dx * y_ref[...]).reshape(tm // 8, 8, d), axis=0)

    return pl.pallas_call(
        body, name=name, grid=(r // tm,),
        in_specs=[row, row, _mt_spec(d, nct), vec, pl.BlockSpec((tm, d), lambda i: (jnp.maximum(i - nct, 0), 0))],
        out_specs=(pl.BlockSpec((8, 128), lambda i: (0, 0)), row, row, _acc_spec(d, nct), vec),
        out_shape=(jax.ShapeDtypeStruct((8, 128), F32), jax.ShapeDtypeStruct((r, d), F32), jax.ShapeDtypeStruct((r, d), MXU),
                   jax.ShapeDtypeStruct((16, d), F32), jax.ShapeDtypeStruct((1, d), F32)),
        compiler_params=_cp(("arbitrary",)))(xp, y, mtg, fnw, tgt)


def _silu(x):
    return x * jax.nn.sigmoid(x)


def mod_fwd(c16, w, b, *, name):
    d, n = w.shape
    tn = _tile(n, 1536)

    def body(c_ref, w_ref, b_ref, o_ref, s_ref):
        s = _silu(c_ref[...])
        s_ref[...] = s.astype(MXU)
        o_ref[...] = _dot(s, w_ref[...]) + b_ref[...]

    return pl.pallas_call(
        body, name=name, grid=(n // tn,),
        in_specs=[pl.BlockSpec((16, d), lambda j: (0, 0)), pl.BlockSpec((d, tn), lambda j: (0, j)), pl.BlockSpec((1, tn), lambda j: (0, j))],
        out_specs=(pl.BlockSpec((16, tn), lambda j: (0, j)), pl.BlockSpec((16, d), lambda j: (0, 0))),
        out_shape=(jax.ShapeDtypeStruct((16, n), F32), jax.ShapeDtypeStruct((16, d), MXU)),
        compiler_params=_cp(("arbitrary",)))(c16, w, b)


def colsum(x, *, name):
    def body(x_ref, o_ref):
        o_ref[...] = jnp.sum(x_ref[...], axis=0, keepdims=True)

    return pl.pallas_call(body, name=name, out_shape=jax.ShapeDtypeStruct((1, x.shape[1]), F32))(x)


def cctx_grad(c16, ds_list, *, name):
    def body(c_ref, *refs):
        o_ref = refs[-1]
        ds = refs[0][...]
        for r_ in refs[1:-1]:
            ds = ds + r_[...]
        _, vjp = jax.vjp(_silu, c_ref[...])
        (dc,) = vjp(ds)
        o_ref[...] = jnp.sum(dc[0:8], axis=0, keepdims=True)

    return pl.pallas_call(body, name=name, out_shape=jax.ShapeDtypeStruct((1, c16.shape[1]), F32))(c16, *ds_list)


def _s5_disc(lam_re, lam_im, log_step, b_re, b_im):
    lr = jnp.minimum(lam_re, S5_LAM_RE_MAX)
    li = lam_im
    dt = jnp.exp(log_step)
    mag = jnp.exp(lr * dt)
    abar_r = mag * jnp.cos(li * dt)
    abar_i = mag * jnp.sin(li * dt)
    den = lr * lr + li * li
    nr = abar_r - 1.0
    coef_r = (nr * lr + abar_i * li) / den
    coef_i = (abar_i * lr - nr * li) / den
    bbar_r = coef_r[:, None, :] * b_re - coef_i[:, None, :] * b_im
    bbar_i = coef_r[:, None, :] * b_im + coef_i[:, None, :] * b_re
    return abar_r, abar_i, bbar_r, bbar_i


def s5_disc_fwd(lam_re, lam_im, log_step, b_re, b_im, *, name):
    def body(lr, li, ls, br, bi, ar_o, ai_o, br_o, bi_o):
        ar_o[...], ai_o[...], br_o[...], bi_o[...] = _s5_disc(lr[...], li[...], ls[...], br[...], bi[...])

    s2, s3 = jax.ShapeDtypeStruct(lam_re.shape, F32), jax.ShapeDtypeStruct(b_re.shape, F32)
    return pl.pallas_call(body, name=name, out_shape=(s2, s2, s3, s3))(lam_re, lam_im, log_step, b_re, b_im)


def s5_disc_bwd(lam_re, lam_im, log_step, b_re, b_im, d_ar, d_ai, d_br, d_bi, *, name):
    def body(lr, li, ls, br, bi, dar, dai, dbr, dbi, o_lr, o_li, o_ls, o_br, o_bi):
        _, vjp = jax.vjp(_s5_disc, lr[...], li[...], ls[...], br[...], bi[...])
        o_lr[...], o_li[...], o_ls[...], o_br[...], o_bi[...] = vjp((dar[...], dai[...], dbr[...], dbi[...]))

    s2, s3 = jax.ShapeDtypeStruct(lam_re.shape, F32), jax.ShapeDtypeStruct(b_re.shape, F32)
    return pl.pallas_call(body, name=name, out_shape=(s2, s2, jax.ShapeDtypeStruct(log_step.shape, F32), s3, s3))(
        lam_re, lam_im, log_step, b_re, b_im, d_ar, d_ai, d_br, d_bi)


S5_LANES = 512


def _chunk_order(k, ncc, nch, rev):
    if not rev:
        return k
    return jnp.where(k < ncc, ncc - 1 - k, nch - 1 - (k - ncc))


def _cmul(ar, ai, xr, xi):
    return ar * xr - ai * xi, ar * xi + ai * xr


S5_FWD_ROWS = 256
S5_BWD_ROWS = 256


def _const_spec(a):
    return pl.BlockSpec(a.shape, lambda k: (0,) * a.ndim, pipeline_mode=pl.Buffered(1))


def _shift_steps(x, edge_tile, back):
    n = x.shape[0]
    row = lax.broadcasted_iota(jnp.int32, (8, x.shape[1]), 0)
    edge = pltpu.roll(edge_tile, 4, 0)
    if back:
        y = pltpu.roll(x, 4, 0)
        return jnp.concatenate([jnp.where(row < 4, edge, y[0:8]), y[8:]], axis=0)
    y = pltpu.roll(x, n - 4, 0)
    return jnp.concatenate([y[:n - 8], jnp.where(row >= 4, edge, y[n - 8:])], axis=0)


def s5_scan_fwd(cfg, u, a2_re, a2_im, bre, bim, abre, abim, cre, cim, *, rev, name):
    r, d = u.shape
    ns = a2_re.shape[1]
    kb = d // S5_KIN
    tcr = S5_FWD_ROWS
    n8 = tcr // 8
    q = S5_FWD_ROWS // S5_BWD_ROWS
    seg = n8 // q
    nch, ncc = r // tcr, cfg["rc"] // tcr
    lw = min(S5_LANES, ns)

    def body(u_ref, ar_ref, ai_ref, bre_ref, bim_ref, abre_ref, abim_ref, cre_ref, cim_ref, sre_ref, sim_ref, ere_ref, eim_ref, y_ref,
             st_re, st_im, u_edge):
        @pl.when(pl.program_id(0) == 0)
        def _():
            st_re[...] = jnp.zeros_like(st_re)
            st_im[...] = jnp.zeros_like(st_im)
            u_edge[...] = jnp.zeros_like(u_edge)

        u_ = u_ref[...]
        ub = u_.astype(MXU)
        upb = _shift_steps(u_, u_edge[...], back=not rev).astype(MXU)
        u_edge[...] = u_[0:8] if rev else u_[tcr - 8:tcr]
        for j in range(kb):
            uj, upj = ub[:, j * S5_KIN:(j + 1) * S5_KIN], upb[:, j * S5_KIN:(j + 1) * S5_KIN]
            sre_ref[:, :, j * S5_KST:(j + 1) * S5_KST] = (_dot(uj, bre_ref[j]) + _dot(upj, abre_ref[j])).reshape(n8, 8, S5_KST)
            sim_ref[:, :, j * S5_KST:(j + 1) * S5_KST] = (_dot(uj, bim_ref[j]) + _dot(upj, abim_ref[j])).reshape(n8, 8, S5_KST)
        for c in range(ns // lw):
            sl = slice(c * lw, (c + 1) * lw)
            ar = jnp.broadcast_to(ar_ref[:, sl], (8, lw))
            ai = jnp.broadcast_to(ai_ref[:, sl], (8, lw))

            def step(i, carry, sl=sl, ar=ar, ai=ai):
                sr, si = carry
                ii = n8 - 1 - i if rev else i
                pr, pi = _cmul(ar, ai, sr, si)
                sr, si = pr + sre_ref[ii, :, sl], pi + sim_ref[ii, :, sl]
                sre_ref[ii, :, sl] = sr
                sim_ref[ii, :, sl] = si
                return sr, si

            sr, si = st_re[:, sl], st_im[:, sl]
            for s_ in range(q):
                at = q - 1 - s_ if rev else s_
                ere_ref[at, :, sl] = sr
                eim_ref[at, :, sl] = si
                sr, si = lax.fori_loop(s_ * seg, (s_ + 1) * seg, step, (sr, si))
            st_re[:, sl] = sr
            st_im[:, sl] = si
        for j in range(kb):
            sr = sre_ref[:, :, j * S5_KST:(j + 1) * S5_KST].reshape(tcr, S5_KST)
            si = sim_ref[:, :, j * S5_KST:(j + 1) * S5_KST].reshape(tcr, S5_KST)
            y_ref[:, j * S5_KIN:(j + 1) * S5_KIN] = _dot(sr, cre_ref[j]) - _dot(si, cim_ref[j])

    cidx = functools.partial(_chunk_order, ncc=ncc, nch=nch, rev=rev)
    full = _const_spec
    st = pl.BlockSpec((n8, 8, ns), lambda k: (cidx(k), 0, 0))
    en = pl.BlockSpec((q, 8, ns), lambda k: (cidx(k), 0, 0))
    return pl.pallas_call(
        body, name=name, grid=(nch,),
        in_specs=[pl.BlockSpec((tcr, d), lambda k: (cidx(k), 0)), full(a2_re), full(a2_im), full(bre), full(bim), full(abre), full(abim),
                  full(cre), full(cim)],
        out_specs=(st, st, en, en, pl.BlockSpec((tcr, d), lambda k: (cidx(k), 0))),
        out_shape=(jax.ShapeDtypeStruct((r // 8, 8, ns), F32),) * 2 + (jax.ShapeDtypeStruct((q * nch, 8, ns), F32),) * 2
        + (jax.ShapeDtypeStruct((r, d), F32),),
        scratch_shapes=[pltpu.VMEM((8, ns), F32), pltpu.VMEM((8, ns), F32), pltpu.VMEM((8, d), F32)],
        compiler_params=_cp(("arbitrary",)))(u, a2_re, a2_im, bre, bim, abre, abim, cre, cim)


def s5_scan_bwd(cfg, dyb, sre, sim, ere, eim, a2_re, a2_im, bre, bim, cre, cim, c2re, c2im, du_in, *, rev, name):
    r, d = dyb.shape
    ns = a2_re.shape[1]
    kb = d // S5_KIN
    tcr = S5_BWD_ROWS
    n8 = tcr // 8
    nch, ncc = r // tcr, cfg["rc"] // tcr
    lw = min(S5_LANES, ns)

    def body(dy_ref, sre_ref, sim_ref, ere_ref, eim_ref, ar_ref, ai_ref, bre_ref, bim_ref, cre_ref, cim_ref, c2re_ref, c2im_ref, duin_ref,
             du_ref, gre_ref, gim_ref, dar_ref, dai_ref, g_re, g_im, gc_re, gc_im, dy_edge):
        k = pl.program_id(0)

        @pl.when(k == 0)
        def _():
            gc_re[...] = jnp.zeros_like(gc_re)
            gc_im[...] = jnp.zeros_like(gc_im)
            dar_ref[...] = jnp.zeros_like(dar_ref)
            dai_ref[...] = jnp.zeros_like(dai_ref)
            dy_edge[...] = jnp.zeros_like(dy_edge)

        dy32 = dy_ref[...].astype(F32)
        dy = dy32.astype(MXU)
        dyn = _shift_steps(dy32, dy_edge[...], back=rev).astype(MXU)
        dy_edge[...] = dy32[tcr - 8:tcr] if rev else dy32[0:8]
        for j in range(kb):
            dyj, dynj = dy[:, j * S5_KIN:(j + 1) * S5_KIN], dyn[:, j * S5_KIN:(j + 1) * S5_KIN]
            g_re[:, :, j * S5_KST:(j + 1) * S5_KST] = (_dot(dyj, cre_ref[j], 1, 1) + _dot(dynj, c2re_ref[j], 1, 1)).reshape(n8, 8, S5_KST)
            g_im[:, :, j * S5_KST:(j + 1) * S5_KST] = -(_dot(dyj, cim_ref[j], 1, 1) + _dot(dynj, c2im_ref[j], 1, 1)).reshape(n8, 8, S5_KST)
        first = lax.broadcasted_iota(jnp.int32, (8, lw), 0) < 4
        if rev:
            first = jnp.logical_not(first)
        for c in range(ns // lw):
            sl = slice(c * lw, (c + 1) * lw)
            ar = jnp.broadcast_to(ar_ref[:, sl], (8, lw))
            nai = -jnp.broadcast_to(ai_ref[:, sl], (8, lw))

            def step(i, carry, sl=sl, ar=ar, nai=nai):
                gr, gi, accr, acci = carry
                ii = i if rev else n8 - 1 - i
                pr, pi = _cmul(ar, nai, gr, gi)
                outr, outi = pr + g_re[ii, :, sl], pi + g_im[ii, :, sl]
                g_re[ii, :, sl] = outr
                g_im[ii, :, sl] = outi
                pv = jnp.clip(ii + 1 if rev else ii - 1, 0, n8 - 1)
                at_entry = (ii == n8 - 1) if rev else (ii == 0)
                pvr = jnp.where(at_entry, ere_ref[0, :, sl], sre_ref[pv, :, sl])
                pvi = jnp.where(at_entry, eim_ref[0, :, sl], sim_ref[pv, :, sl])
                spr = pltpu.roll(jnp.where(first, sre_ref[ii, :, sl], pvr), 4, 0)
                spi = pltpu.roll(jnp.where(first, sim_ref[ii, :, sl], pvi), 4, 0)
                accr = accr + outr * spr + outi * spi
                acci = acci + outi * spr - outr * spi
                return outr, outi, accr, acci

            gr, gi, accr, acci = lax.fori_loop(0, n8, step, (gc_re[:, sl], gc_im[:, sl], dar_ref[:, sl], dai_ref[:, sl]))
            gc_re[:, sl] = gr
            gc_im[:, sl] = gi
            dar_ref[:, sl] = accr
            dai_ref[:, sl] = acci
        for j in range(kb):
            gr = g_re[:, :, j * S5_KST:(j + 1) * S5_KST].reshape(tcr, S5_KST)
            gi = g_im[:, :, j * S5_KST:(j + 1) * S5_KST].reshape(tcr, S5_KST)
            gre_ref[:, j * S5_KST:(j + 1) * S5_KST] = gr.astype(MXU)
            gim_ref[:, j * S5_KST:(j + 1) * S5_KST] = gi.astype(MXU)
            du_ref[:, j * S5_KIN:(j + 1) * S5_KIN] = (duin_ref[:, j * S5_KIN:(j + 1) * S5_KIN]
                                                     + _dot(gr, bre_ref[j], 1, 1) + _dot(gi, bim_ref[j], 1, 1))

    def cidx(k):
        return _chunk_order(nch - 1 - k, ncc, nch, rev)

    full = _const_spec
    st = pl.BlockSpec((n8, 8, ns), lambda k: (cidx(k), 0, 0))
    en = pl.BlockSpec((1, 8, ns), lambda k: (cidx(k), 0, 0))
    rowd = pl.BlockSpec((tcr, d), lambda k: (cidx(k), 0))
    rown = pl.BlockSpec((tcr, ns), lambda k: (cidx(k), 0))
    acc = pl.BlockSpec((8, ns), lambda k: (0, 0))
    return pl.pallas_call(
        body, name=name, grid=(nch,),
        in_specs=[rowd, st, st, en, en, full(a2_re), full(a2_im), full(bre), full(bim), full(cre), full(cim), full(c2re), full(c2im), rowd],
        out_specs=(rowd, rown, rown, acc, acc),
        out_shape=(jax.ShapeDtypeStruct((r, d), F32), jax.ShapeDtypeStruct((r, ns), MXU), jax.ShapeDtypeStruct((r, ns), MXU),
                   jax.ShapeDtypeStruct((8, ns), F32), jax.ShapeDtypeStruct((8, ns), F32)),
        scratch_shapes=[pltpu.VMEM((n8, 8, ns), F32), pltpu.VMEM((n8, 8, ns), F32), pltpu.VMEM((8, ns), F32), pltpu.VMEM((8, ns), F32),
                        pltpu.VMEM((8, d), F32)],
        compiler_params=_cp(("arbitrary",)))(dyb, sre, sim, ere, eim, a2_re, a2_im, bre, bim, cre, cim, c2re, c2im, du_in)


def rowmap(fn, rows_in, vecs_in, outs, accs=(), *, name):
    r = rows_in[0].shape[0]
    tm = _row_tile(r, max(a.shape[1] for a in rows_in))
    nr, nv, no = len(rows_in), len(vecs_in), len(outs)

    def body(*refs):
        ins = [x[...] for x in refs[:nr + nv]]
        res = fn(*ins)
        if not isinstance(res, (tuple, list)):
            res = (res,)
        out_refs = refs[nr + nv:]
        for o_ref, v in zip(out_refs[:no], res[:no]):
            o_ref[...] = v.astype(o_ref.dtype)
        if accs:
            @pl.when(pl.program_id(0) == 0)
            def _():
                for a_ref in out_refs[no:]:
                    a_ref[...] = jnp.zeros_like(a_ref)
            for a_ref, v in zip(out_refs[no:], res[no:]):
                a_ref[...] += v

    in_specs = [pl.BlockSpec((tm, a.shape[1]), lambda i: (i, 0)) for a in rows_in]
    in_specs += [pl.BlockSpec(v.shape, lambda i, n=v.ndim: (0,) * n) for v in vecs_in]
    out_specs = [pl.BlockSpec((tm, w), lambda i: (i, 0)) for w, _ in outs] + [pl.BlockSpec(s, lambda i, n=len(s): (0,) * n) for s in accs]
    out_shape = [jax.ShapeDtypeStruct((r, w), dt) for w, dt in outs] + [jax.ShapeDtypeStruct(s, F32) for s in accs]
    res = pl.pallas_call(body, name=name, grid=(r // tm,), in_specs=in_specs, out_specs=tuple(out_specs), out_shape=tuple(out_shape),
                         compiler_params=_cp(("arbitrary",) if accs else ("parallel",)))(*rows_in, *vecs_in)
    return res


def _gelu(x):
    return jax.nn.gelu(x, approximate=True)


def _hg_lower_bound(e0, e1):
    m = jnp.maximum(e0, e1)
    a, b = jnp.exp(e0 - m), jnp.exp(e1 - m)
    return b / (a + b)


def _hg_gates(x, lb):
    logf = jnp.log(lb + (1.0 - lb) * jax.nn.sigmoid(x))
    return logf, (1.0 - lb) * jax.nn.sigmoid(-x)


def _hg_masks(rev):
    n = CHUNK_ROWS
    rr = lax.broadcasted_iota(jnp.int32, (n, n), 0)
    ss = lax.broadcasted_iota(jnp.int32, (n, n), 1)
    same = (rr % NB) == (ss % NB)
    causal = same & ((ss >= rr) if rev else (ss <= rr))
    anti = same & ((ss <= rr) if rev else (ss >= rr))
    end0 = 0 if rev else n - NB
    pick_end = ss == (end0 + rr % NB)
    return same, causal, anti, pick_end, end0


def _hg_expand(x):
    ex = lax.broadcasted_iota(jnp.int32, x.shape, 0) % NB
    return jnp.concatenate([jnp.where(ex == b, x, 0.0) for b in range(NB)], axis=1)


def _hg_fold(xe):
    kk = xe.shape[1] // NB
    ex = lax.broadcasted_iota(jnp.int32, (xe.shape[0], kk), 0) % NB
    out = jnp.zeros((xe.shape[0], kk), F32)
    for b in range(NB):
        out = out + jnp.where(ex == b, xe[:, b * kk:(b + 1) * kk], 0.0)
    return out


def _hg_chunk(q, v, x, lb, masks):
    same, causal, anti, pick_end, end0 = masks
    logf, kk = _hg_gates(x, lb)
    b = _dot3(causal.astype(MXU), logf)
    bend_t = _dot3(pick_end.astype(MXU), b)
    bend_flat = jnp.concatenate([b[end0 + i:end0 + i + 1] for i in range(NB)], axis=1)
    eb = jnp.exp(b)
    enb = jnp.exp(-b)
    ee = jnp.exp(bend_t - b)
    qd, kd, ke = q * eb, kk * enb, kk * ee
    att = jnp.where(causal, _dot(qd, kd, 1, 1), 0.0)
    decay = jnp.exp(bend_flat)
    return dict(same=same, causal=causal, anti=anti, logf=logf, kk=kk, b=b, eb=eb, enb=enb, ee=ee, qd=qd, kd=kd, ke=ke, att=att,
                decay=decay, qde=_hg_expand(qd), kee=_hg_expand(ke))


def _hg_chunk_order(cfg, r):
    nch, ncc = r // CHUNK_ROWS, cfg["rc"] // CHUNK_ROWS
    return nch, ncc


def hg_scan_fwd(cfg, z, lb, *, d_dir, name):
    r = z.shape[0]
    d = z.shape[1] // N_PROJ
    nh = d // HEAD
    rev = d_dir == 1
    nch, ncc = _hg_chunk_order(cfg, r)
    n = CHUNK_ROWS

    def body(q_ref, v_ref, x_ref, lb_ref, o_ref, sin_ref, stk):
        @pl.when(pl.program_id(0) == 0)
        def _():
            stk[...] = jnp.zeros_like(stk)

        masks = _hg_masks(rev)
        for h in range(nh):
            sl = slice(h * HEAD, (h + 1) * HEAD)
            s0 = stk[h]
            sin_ref[0, h] = s0
            v = v_ref[:, sl]
            c = _hg_chunk(q_ref[:, sl], v, x_ref[:, sl], lb_ref[:, sl], masks)
            o_ref[:, sl] = _dot(c["att"], v) + _dot(c["qde"], s0, 1, 1)
            stk[h] = s0 * c["decay"] + _dot(v, c["kee"], 0, 0)

    def cidx(k):
        return _chunk_order(k, ncc, nch, rev)

    blk = lambda p: pl.BlockSpec((n, d), lambda k: (cidx(k), p))
    return pl.pallas_call(
        body, name=name, grid=(nch,),
        in_specs=[blk(0), blk(1), blk(2 + d_dir), pl.BlockSpec((1, d), lambda k: (0, 0))],
        out_specs=(blk(0), pl.BlockSpec((1, nh, HEAD, NB * HEAD), lambda k: (cidx(k), 0, 0, 0))),
        out_shape=(jax.ShapeDtypeStruct((r, d), F32), jax.ShapeDtypeStruct((nch, nh, HEAD, NB * HEAD), F32)),
        scratch_shapes=[pltpu.VMEM((nh, HEAD, NB * HEAD), F32)], compiler_params=_cp(("arbitrary",)))(z, z, z, lb)


def hg_scan_bwd(cfg, do, z, lb, sin, dq_in, dv_in, *, d_dir, name):
    r = z.shape[0]
    d = z.shape[1] // N_PROJ
    nh = d // HEAD
    rev = d_dir == 1
    nch, ncc = _hg_chunk_order(cfg, r)
    n = CHUNK_ROWS
    has_in = dq_in is not None

    def body(*refs):
        if has_in:
            do_ref, q_ref, v_ref, x_ref, lb_ref, sin_ref, dqi_ref, dvi_ref, dq_ref, dv_ref, dx_ref, dlb_ref, dstk = refs
        else:
            do_ref, q_ref, v_ref, x_ref, lb_ref, sin_ref, dq_ref, dv_ref, dx_ref, dlb_ref, dstk = refs
        @pl.when(pl.program_id(0) == 0)
        def _():
            dstk[...] = jnp.zeros_like(dstk)
            dlb_ref[...] = jnp.zeros_like(dlb_ref)

        masks = _hg_masks(rev)
        ex = lax.broadcasted_iota(jnp.int32, (n, HEAD), 0) % NB
        for h in range(nh):
            sl = slice(h * HEAD, (h + 1) * HEAD)
            do_, q, v, x, lb_, s0, ds1 = do_ref[:, sl], q_ref[:, sl], v_ref[:, sl], x_ref[:, sl], lb_ref[:, sl], sin_ref[0, h], dstk[h]
            c = _hg_chunk(q, v, x, lb_, masks)
            datt = jnp.where(c["causal"], _dot(do_, v, 1, 1), 0.0)
            dv = _dot(c["att"], do_, 0, 0) + _dot(c["kee"], ds1, 1, 1)
            dqd = _dot(datt, c["kd"]) + _hg_fold(_dot(do_, s0))
            dkd = _dot(datt, c["qd"], 0, 0)
            dke = _hg_fold(_dot(v, ds1))
            dbend_flat = jnp.sum(ds1 * s0, axis=0, keepdims=True) * c["decay"]
            dstk[h] = _dot(do_, c["qde"], 0, 0) + ds1 * c["decay"]
            dq = dqd * c["eb"]
            dk = dkd * c["enb"] + dke * c["ee"]
            db = dqd * c["qd"] - dkd * c["kd"] - dke * c["ke"]
            dbend_rows = jnp.zeros((n, HEAD), F32)
            for b in range(NB):
                dbend_rows = dbend_rows + jnp.where(ex == b, dbend_flat[:, b * HEAD:(b + 1) * HEAD], 0.0)
            dlogf = _dot3(c["anti"].astype(MXU), db) + _dot3(c["same"].astype(MXU), dke * c["ke"]) + dbend_rows
            _, vjp = jax.vjp(_hg_gates, x, lb_)
            dx, dlb = vjp((dlogf, dk))
            if has_in:
                dq = dq + dqi_ref[:, sl]
                dv = dv + dvi_ref[:, sl]
            dq_ref[:, sl] = dq
            dv_ref[:, sl] = dv
            dx_ref[:, sl] = dx
            dlb_ref[:, sl] += dlb

    def cidx(k):
        return _chunk_order(nch - 1 - k, ncc, nch, rev)

    blk = lambda p: pl.BlockSpec((n, d), lambda k: (cidx(k), p))
    vec = pl.BlockSpec((1, d), lambda k: (0, 0))
    in_specs = [blk(0), blk(0), blk(1), blk(2 + d_dir), vec, pl.BlockSpec((1, nh, HEAD, NB * HEAD), lambda k: (cidx(k), 0, 0, 0))]
    args = [do, z, z, z, lb, sin]
    if has_in:
        in_specs += [blk(0), blk(0)]
        args += [dq_in, dv_in]
    rd = jax.ShapeDtypeStruct((r, d), F32)
    return pl.pallas_call(
        body, name=name, grid=(nch,), in_specs=in_specs, out_specs=(blk(0), blk(0), blk(0), vec),
        out_shape=(rd, rd, rd, jax.ShapeDtypeStruct((1, d), F32)),
        scratch_shapes=[pltpu.VMEM((nh, HEAD, NB * HEAD), F32)], compiler_params=_cp(("arbitrary",)))(*args)


def _hg_read(o, g, gw):
    on = o * lax.rsqrt(jnp.mean(o * o, axis=-1, keepdims=True) + NORM_EPS) * gw
    return on * jax.nn.sigmoid(g)


def hg_read_fwd(of, ob, z, gw, *, name):
    r, d = of.shape
    nh = d // HEAD
    tm = _row_tile(r)

    def body(of_ref, ob_ref, g_ref, gw_ref, o_ref):
        for h in range(nh):
            sl = slice(h * HEAD, (h + 1) * HEAD)
            o_ref[:, sl] = _hg_read(of_ref[:, sl] + ob_ref[:, sl], g_ref[:, sl], gw_ref[...]).astype(MXU)

    blk = pl.BlockSpec((tm, d), lambda i: (i, 0))
    return pl.pallas_call(
        body, name=name, grid=(r // tm,),
        in_specs=[blk, blk, pl.BlockSpec((tm, d), lambda i: (i, N_PROJ - 1)), pl.BlockSpec((1, HEAD), lambda i: (0, 0))],
        out_specs=blk, out_shape=jax.ShapeDtypeStruct((r, d), MXU), compiler_params=_cp(("parallel",)))(of, ob, z, gw)


def hg_read_bwd(don, of, ob, z, gw, *, name):
    r, d = of.shape
    nh = d // HEAD
    tm = _row_tile(r)

    def body(don_ref, of_ref, ob_ref, g_ref, gw_ref, do_ref, dg_ref, dgw_ref):
        @pl.when(pl.program_id(0) == 0)
        def _():
            dgw_ref[...] = jnp.zeros_like(dgw_ref)

        for h in range(nh):
            sl = slice(h * HEAD, (h + 1) * HEAD)
            _, vjp = jax.vjp(_hg_read, of_ref[:, sl] + ob_ref[:, sl], g_ref[:, sl], gw_ref[...])
            do_ref[:, sl], dg_ref[:, sl], dgw = vjp(don_ref[:, sl])
            dgw_ref[...] += dgw

    blk = pl.BlockSpec((tm, d), lambda i: (i, 0))
    vec = pl.BlockSpec((1, HEAD), lambda i: (0, 0))
    rd = jax.ShapeDtypeStruct((r, d), F32)
    return pl.pallas_call(
        body, name=name, grid=(r // tm,),
        in_specs=[blk, blk, blk, pl.BlockSpec((tm, d), lambda i: (i, N_PROJ - 1)), vec],
        out_specs=(blk, blk, vec), out_shape=(rd, rd, jax.ShapeDtypeStruct((1, HEAD), F32)),
        compiler_params=_cp(("arbitrary",)))(don, of, ob, z, gw)


FFN_COLS = 256


def _seg_masks(cfg, tr, i):
    t = lax.broadcasted_iota(jnp.int32, (tr, FFN_COLS), 0) // NB
    ctx_steps = cfg["rc"] // NB
    pos = jnp.where(i == 0, t % ctx_steps, t % GRID_W)
    last = jnp.where(i == 0, ctx_steps - 1, GRID_W - 1)
    return pos == 0, pos == last


def _prev(x, start):
    return jnp.where(start, 0.0, pltpu.roll(x, NB, 0))


def _next(x, end):
    return jnp.where(end, 0.0, pltpu.roll(x, x.shape[0] - NB, 0))


def _conv3(u, w, b, start, end):
    return ((b + _prev(u, start) * w[0:1]) + u * w[1:2]) + _next(u, end) * w[2:3]


def ffn_mid_fwd(cfg, u, cw, cb, *, name):
    r, f2 = u.shape
    f = f2 // 2
    tr = cfg["rc"]
    nf = f // FFN_COLS

    def body(ua_ref, ug_ref, wa_ref, wg_ref, ba_ref, bg_ref, o_ref):
        start, end = _seg_masks(cfg, tr, pl.program_id(0))
        a = _conv3(ua_ref[...], wa_ref[...], ba_ref[...], start, end)
        g = _conv3(ug_ref[...], wg_ref[...], bg_ref[...], start, end)
        o_ref[...] = (_silu(a) * g).astype(MXU)

    ca = lambda rows: pl.BlockSpec((rows, FFN_COLS), lambda i, j: (i if rows == tr else 0, j))
    cg = lambda rows: pl.BlockSpec((rows, FFN_COLS), lambda i, j: (i if rows == tr else 0, j + nf))
    return pl.pallas_call(
        body, name=name, grid=(r // tr, nf), in_specs=[ca(tr), cg(tr), ca(3), cg(3), ca(1), cg(1)], out_specs=ca(tr),
        out_shape=jax.ShapeDtypeStruct((r, f), MXU), compiler_params=_cp(("parallel", "parallel")))(u, u, cw, cw, cb, cb)


def ffn_mid_bwd(cfg, dact, u, cw, cb, *, name):
    r, f2 = u.shape
    f = f2 // 2
    tr = cfg["rc"]
    nf = f // FFN_COLS

    def body(da_ref, us_ref, up_ref, ws_ref, wp_ref, bs_ref, bp_ref, du_ref, dcw_ref, dcb_ref):
        i = pl.program_id(1)
        is_a = pl.program_id(0) < nf
        start, end = _seg_masks(cfg, tr, i)
        @pl.when(i == 0)
        def _():
            dcw_ref[...] = jnp.zeros_like(dcw_ref)
            dcb_ref[...] = jnp.zeros_like(dcb_ref)

        def finish(dc):
            us, ws = us_ref[...], ws_ref[...]
            du_ref[...] = (ws[1:2] * dc + ws[0:1] * _next(dc, end) + ws[2:3] * _prev(dc, start)).astype(MXU)
            dcw_ref[...] += jnp.concatenate([jnp.sum(dc * _prev(us, start), axis=0, keepdims=True), jnp.sum(dc * us, axis=0, keepdims=True),
                                             jnp.sum(dc * _next(us, end), axis=0, keepdims=True)], axis=0)
            dcb_ref[...] += jnp.sum(dc, axis=0, keepdims=True)

        @pl.when(is_a)
        def _():
            cs = _conv3(us_ref[...], ws_ref[...], bs_ref[...], start, end)
            cp = _conv3(up_ref[...], wp_ref[...], bp_ref[...], start, end)
            sg = jax.nn.sigmoid(cs)
            finish(da_ref[...] * cp * (sg * (1.0 + cs * (1.0 - sg))))

        @pl.when(jnp.logical_not(is_a))
        def _():
            finish(da_ref[...] * _silu(_conv3(up_ref[...], wp_ref[...], bp_ref[...], start, end)))

    cs_ = lambda rows: pl.BlockSpec((rows, FFN_COLS), lambda j, i: (i if rows == tr else 0, j))
    cp_ = lambda rows: pl.BlockSpec((rows, FFN_COLS), lambda j, i: (i if rows == tr else 0, (j + nf) % (2 * nf)))
    return pl.pallas_call(
        body, name=name, grid=(2 * nf, r // tr),
        in_specs=[pl.BlockSpec((tr, FFN_COLS), lambda j, i: (i, j % nf)), cs_(tr), cp_(tr), cs_(3), cp_(3), cs_(1), cp_(1)],
        out_specs=(cs_(tr), cs_(3), cs_(1)),
        out_shape=(jax.ShapeDtypeStruct((r, f2), MXU), jax.ShapeDtypeStruct((3, f2), F32), jax.ShapeDtypeStruct((1, f2), F32)),
        compiler_params=_cp(("parallel", "arbitrary")))(dact, u, u, cw, cw, cb, cb)


def hg_lb_fwd(e0, e1, *, name):
    def body(a, b, o):
        o[...] = _hg_lower_bound(a[...], b[...])

    return pl.pallas_call(body, name=name, out_shape=jax.ShapeDtypeStruct(e0.shape, F32))(e0, e1)


def hg_lb_bwd(e0, e1, dlb, *, name):
    def body(a, b, g, oa, ob):
        _, vjp = jax.vjp(_hg_lower_bound, a[...], b[...])
        oa[...], ob[...] = vjp(g[...])

    s = jax.ShapeDtypeStruct(e0.shape, F32)
    return pl.pallas_call(body, name=name, out_shape=(s, s))(e0, e1, dlb)


def _adamw(w, g, m, v):
    m = ADAM_B1 * m + (1.0 - ADAM_B1) * g
    v = ADAM_B2 * v + (1.0 - ADAM_B2) * jnp.square(g)
    m_hat = m / (1.0 - ADAM_B1 ** ADAM_STEP)
    v_hat = v / (1.0 - ADAM_B2 ** ADAM_STEP)
    delta = -ADAM_LR * (m_hat / (jnp.sqrt(v_hat) + ADAM_EPS) + ADAM_WD * w)
    return delta, m, v


def _as2d(a):
    if a.ndim >= 2 and a.shape[-1] % 128 == 0:
        return a.reshape(-1, a.shape[-1])
    return a.reshape(-1, 128) if a.size % 128 == 0 else a.reshape(1, -1)


def adamw(w, g, m, v, *, name):
    w2 = _as2d(w)
    outs = rowmap(_adamw, [w2, _as2d(g), _as2d(m), _as2d(v)], [], [(w2.shape[1], F32)] * 3, name=name)
    return tuple(o.reshape(w.shape) for o in outs)


HBM_SPEC = pl.BlockSpec(memory_space=pltpu.HBM)


def _place():
    mx, my, mc = lax.axis_index("x"), lax.axis_index("y"), lax.axis_index("c")
    others = [(1 - mx, my), (mx, 1 - my), (1 - mx, 1 - my)]
    return mx, my, mc, others


def chip_allgather(x, *, name):
    def body(x_ref, o_ref, send_sems, recv_sems, local_sem):
        mx, my, mc, others = _place()
        me = 2 * mx + my
        mine = pltpu.make_async_copy(x_ref, o_ref.at[me], local_sem)
        mine.start()
        sends = [pltpu.make_async_remote_copy(src_ref=x_ref, dst_ref=o_ref.at[me], send_sem=send_sems.at[j], recv_sem=recv_sems.at[j],
                                              device_id=(px, py, mc), device_id_type=MESH) for j, (px, py) in enumerate(others)]
        for cp in sends:
            cp.start()
        for j, (px, py) in enumerate(others):
            pltpu.make_async_remote_copy(src_ref=x_ref, dst_ref=o_ref.at[2 * px + py], send_sem=send_sems.at[j], recv_sem=recv_sems.at[j],
                                         device_id=(px, py, mc), device_id_type=MESH).wait_recv()
        for cp in sends:
            cp.wait_send()
        mine.wait()

    return pl.pallas_call(
        body, name=name, out_shape=jax.ShapeDtypeStruct((4,) + x.shape, x.dtype), in_specs=[HBM_SPEC], out_specs=HBM_SPEC,
        scratch_shapes=[pltpu.SemaphoreType.DMA((3,)), pltpu.SemaphoreType.DMA((3,)), pltpu.SemaphoreType.DMA])(x)


def _win(ref, axis, start, size):
    idx = [slice(None)] * len(ref.shape)
    idx[axis] = pl.ds(start, size)
    return ref.at[tuple(idx)]


def _half_axis(shape, ax):
    if shape[0] == 2:
        return 0
    return 2 if ax == 1 else 1


def _cut(shape, axis, parts):
    return shape[:axis] + (shape[axis] // parts,) + shape[axis + 1:]


def _hbm_call(body, arrays, out_shapes, sems, name):
    n_in = len(arrays)
    return pl.pallas_call(body, name=name, out_shape=tuple(out_shapes), in_specs=[HBM_SPEC] * n_in, out_specs=tuple([HBM_SPEC] * len(out_shapes)),
                          scratch_shapes=sems)(*arrays)


def place_shard(shard, ax, chip, dtype, *, name):
    l, r, c = shard.shape
    tr = _row_tile(r, c)
    per_block = (l, r // tr, 1)[ax]

    def omap(li, ri, cref):
        idx = [li, ri, 0]
        idx[ax] = idx[ax] + cref[0] * per_block
        return tuple(idx)

    def body(c_ref, s_ref, o_ref):
        o_ref[...] = s_ref[...].astype(dtype)

    full = shard.shape[:ax] + (4 * shard.shape[ax],) + shard.shape[ax + 1:]
    return pl.pallas_call(
        body, name=name, out_shape=jax.ShapeDtypeStruct(full, dtype),
        grid_spec=pltpu.PrefetchScalarGridSpec(
            num_scalar_prefetch=1, grid=(l, r // tr),
            in_specs=[pl.BlockSpec((1, tr, c), lambda li, ri, cref: (li, ri, 0))], out_specs=pl.BlockSpec((1, tr, c), omap)),
        compiler_params=_cp(("parallel", "parallel")))(chip, shard)


def gather_placed(arrays, axes, haxes, *, name):
    n = len(arrays)

    def body(*refs):
        ins, outs = refs[:n], refs[n:2 * n]
        send_sems, recv_sems = refs[2 * n:]
        mx, my, mc, others = _place()
        me = 2 * mx + my

        def part(ref, i, chip):
            sz, hs = arrays[i].shape[axes[i]] // 4, arrays[i].shape[haxes[i]] // 2
            return _win(_win(ref, axes[i], chip * sz, sz), haxes[i], mc * hs, hs)

        sends = []
        for i in range(n):
            for j, (px, py) in enumerate(others):
                rc = pltpu.make_async_remote_copy(src_ref=part(ins[i], i, me), dst_ref=part(outs[i], i, me), send_sem=send_sems.at[i, j],
                                                  recv_sem=recv_sems.at[i, j], device_id=(px, py, mc), device_id_type=MESH)
                rc.start()
                sends.append(rc)
        for i in range(n):
            for j, (px, py) in enumerate(others):
                pltpu.make_async_remote_copy(src_ref=part(ins[i], i, me), dst_ref=part(outs[i], i, 2 * px + py), send_sem=send_sems.at[i, j],
                                             recv_sem=recv_sems.at[i, j], device_id=(px, py, mc), device_id_type=MESH).wait_recv()
        for rc in sends:
            rc.wait_send()

    return pl.pallas_call(
        body, name=name, out_shape=tuple(jax.ShapeDtypeStruct(a_.shape, a_.dtype) for a_ in arrays), in_specs=[HBM_SPEC] * n,
        out_specs=tuple([HBM_SPEC] * n), input_output_aliases={i: i for i in range(n)},
        scratch_shapes=[pltpu.SemaphoreType.DMA((n, 3)), pltpu.SemaphoreType.DMA((n, 3))])(*arrays)


SEM_SPEC = pl.BlockSpec(memory_space=pltpu.SEMAPHORE)
SPLIT_COPY = pltpu.CompilerParams(has_side_effects=pltpu.SideEffectType.DATAFLOW_SIDE_EFFECTING)


def _gather_part(ref, shape, ax, hax, chip, core):
    sz, hs = shape[ax] // 4, shape[hax] // 2
    return _win(_win(ref, ax, chip * sz, sz), hax, core * hs, hs)


def gather_placed_start(arrays, axes, haxes, after, *, name):
    n = len(arrays)

    m = 3 * n

    def body(*refs):
        ins, send_sems, recv_sems = refs[:n], refs[n + 1:n + 1 + m], refs[n + 1 + m:n + 1 + 2 * m]
        token = refs[2 * n + 1 + 2 * m]
        mx, my, mc, others = _place()
        me = 2 * mx + my
        for i in range(n):
            for j, (px, py) in enumerate(others):
                part = _gather_part(ins[i], arrays[i].shape, axes[i], haxes[i], me, mc)
                pltpu.make_async_remote_copy(src_ref=part, dst_ref=part, send_sem=send_sems[3 * i + j], recv_sem=recv_sems[3 * i + j],
                                             device_id=(px, py, mc), device_id_type=MESH).start()
        token[...] = jnp.zeros_like(token)

    hbm = [pltpu.with_memory_space_constraint(a_, pltpu.HBM) for a_ in arrays]
    out = pl.pallas_call(
        body, name=name,
        out_shape=tuple([pltpu.SemaphoreType.DMA(())] * (2 * m)) + tuple(pltpu.HBM(a_.shape, a_.dtype) for a_ in arrays)
        + (jax.ShapeDtypeStruct((8, 128), F32),),
        in_specs=[HBM_SPEC] * n + [pl.BlockSpec(memory_space=pl.ANY)],
        out_specs=tuple([SEM_SPEC] * (2 * m)) + tuple([HBM_SPEC] * n) + (pl.BlockSpec(memory_space=pltpu.VMEM),),
        input_output_aliases={i: 2 * m + i for i in range(n)}, compiler_params=SPLIT_COPY)(*hbm, after)
    return list(out[:m]), list(out[m:2 * m]), list(out[2 * m:2 * m + n]), out[2 * m + n]


def gather_placed_wait(arrays, send_sems, recv_sems, axes, haxes, after, *, name):
    n = len(arrays)

    m = 3 * n

    def body(*refs):
        ins, send_refs, recv_refs = refs[:n], refs[n:n + m], refs[n + m:n + 2 * m]
        mx, my, mc, others = _place()
        me = 2 * mx + my
        for i in range(n):
            for j, (px, py) in enumerate(others):
                cp = pltpu.make_async_remote_copy(
                    src_ref=_gather_part(ins[i], arrays[i].shape, axes[i], haxes[i], me, mc),
                    dst_ref=_gather_part(ins[i], arrays[i].shape, axes[i], haxes[i], 2 * px + py, mc),
                    send_sem=send_refs[3 * i + j], recv_sem=recv_refs[3 * i + j], device_id=(px, py, mc), device_id_type=MESH)
                cp.wait_send()
                cp.wait_recv()

    out = pl.pallas_call(
        body, name=name, out_shape=tuple(pltpu.HBM(a_.shape, a_.dtype) for a_ in arrays),
        in_specs=[HBM_SPEC] * n + [SEM_SPEC] * (2 * m) + [pl.BlockSpec(memory_space=pl.ANY)], out_specs=tuple([HBM_SPEC] * n),
        input_output_aliases={i: i for i in range(n)}, compiler_params=SPLIT_COPY)(*arrays, *send_sems, *recv_sems, after)
    return list(out)


def pair_swap_halves(arrays, haxes, *, name):
    n = len(arrays)

    def body(*refs):
        ins, outs = refs[:n], refs[n:2 * n]
        send_sems, recv_sems = refs[2 * n:]
        mx, my, mc, _ = _place()
        cps = []
        for i in range(n):
            hs = arrays[i].shape[haxes[i]] // 2
            cp = pltpu.make_async_remote_copy(src_ref=_win(ins[i], haxes[i], (1 - mc) * hs, hs), dst_ref=outs[i], send_sem=send_sems.at[i],
                                              recv_sem=recv_sems.at[i], device_id=(mx, my, 1 - mc), device_id_type=MESH)
            cp.start()
            cps.append(cp)
        for cp in cps:
            cp.wait()

    outs = [jax.ShapeDtypeStruct(_cut(a_.shape, h_, 2), a_.dtype) for a_, h_ in zip(arrays, haxes)]
    return _hbm_call(body, arrays, outs, [pltpu.SemaphoreType.DMA((n,)), pltpu.SemaphoreType.DMA((n,))], name)


def add_own_half(g, t, hax, core, *, out_dtype, name):
    l, r, c = t.shape
    tr = _row_tile(r, c)
    per_half = (l, r // tr, 1)[hax]

    def imap(li, ri, cref):
        idx = [li, ri, 0]
        idx[hax] = idx[hax] + cref[0] * per_half
        return tuple(idx)

    def body(c_ref, g_ref, t_ref, o_ref):
        o_ref[...] = (g_ref[...] + t_ref[...]).astype(out_dtype)

    return pl.pallas_call(
        body, name=name, out_shape=jax.ShapeDtypeStruct(t.shape, out_dtype),
        grid_spec=pltpu.PrefetchScalarGridSpec(
            num_scalar_prefetch=1, grid=(l, r // tr),
            in_specs=[pl.BlockSpec((1, tr, c), imap), pl.BlockSpec((1, tr, c), lambda li, ri, cref: (li, ri, 0))],
            out_specs=pl.BlockSpec((1, tr, c), lambda li, ri, cref: (li, ri, 0))),
        compiler_params=_cp(("parallel", "parallel")))(core, g, t)


def exchange_blocks(arrays, axes, *, name):
    n = len(arrays)

    def body(*refs):
        ins, outs = refs[:n], refs[n:2 * n]
        send_sems, recv_sems, local_sems = refs[2 * n:]
        mx, my, mc, others = _place()
        me = 2 * mx + my
        waits = []
        for i in range(n):
            sz = arrays[i].shape[axes[i]] // 4
            cp = pltpu.make_async_copy(_win(ins[i], axes[i], me * sz, sz), outs[i].at[me], local_sems.at[i])
            cp.start()
            waits.append(cp.wait)
            for j, (px, py) in enumerate(others):
                rc = pltpu.make_async_remote_copy(src_ref=_win(ins[i], axes[i], (2 * px + py) * sz, sz), dst_ref=outs[i].at[me],
                                                  send_sem=send_sems.at[i, j], recv_sem=recv_sems.at[i, j], device_id=(px, py, mc),
                                                  device_id_type=MESH)
                rc.start()
                waits.append(rc.wait_send)
        for i in range(n):
            sz = arrays[i].shape[axes[i]] // 4
            for j, (px, py) in enumerate(others):
                pltpu.make_async_remote_copy(src_ref=_win(ins[i], axes[i], me * sz, sz), dst_ref=outs[i].at[2 * px + py],
                                             send_sem=send_sems.at[i, j], recv_sem=recv_sems.at[i, j], device_id=(px, py, mc),
                                             device_id_type=MESH).wait_recv()
        for w_ in waits:
            w_()

    outs = [jax.ShapeDtypeStruct((4,) + _cut(a_.shape, ax, 4), a_.dtype) for a_, ax in zip(arrays, axes)]
    return _hbm_call(body, arrays, outs, [pltpu.SemaphoreType.DMA((n, 3)), pltpu.SemaphoreType.DMA((n, 3)), pltpu.SemaphoreType.DMA((n,))], name)


def sum_blocks(e, hax, core, *, name):
    _, l, r, c = e.shape
    tr = _row_tile(r, c)
    per_half = (l, r // tr, 1)[hax]

    def omap(li, ri, cref):
        idx = [li, ri, 0]
        idx[hax] = idx[hax] + cref[0] * per_half
        return tuple(idx)

    def body(c_ref, e_ref, o_ref):
        v = e_ref[...].astype(F32)
        o_ref[...] = ((v[0] + v[1]) + v[2]) + v[3]

    full = (l, r, c)[:hax] + (2 * (l, r, c)[hax],) + (l, r, c)[hax + 1:]
    return pl.pallas_call(
        body, name=name, out_shape=jax.ShapeDtypeStruct(full, F32),
        grid_spec=pltpu.PrefetchScalarGridSpec(
            num_scalar_prefetch=1, grid=(l, r // tr),
            in_specs=[pl.BlockSpec((4, 1, tr, c), lambda li, ri, cref: (0, li, ri, 0))], out_specs=pl.BlockSpec((1, tr, c), omap)),
        compiler_params=_cp(("parallel", "parallel")))(core, e)


def pair_fill_halves(arrays, haxes, *, name):
    n = len(arrays)

    def body(*refs):
        ins, outs = refs[:n], refs[n:2 * n]
        send_sems, recv_sems = refs[2 * n:]
        mx, my, mc, _ = _place()
        cps = []
        for i in range(n):
            hs = arrays[i].shape[haxes[i]] // 2
            mine = _win(ins[i], haxes[i], mc * hs, hs)
            cp = pltpu.make_async_remote_copy(src_ref=mine, dst_ref=_win(outs[i], haxes[i], mc * hs, hs), send_sem=send_sems.at[i],
                                              recv_sem=recv_sems.at[i], device_id=(mx, my, 1 - mc), device_id_type=MESH)
            cp.start()
            cps.append(cp)
        for i in range(n):
            hs = arrays[i].shape[haxes[i]] // 2
            pltpu.make_async_remote_copy(src_ref=_win(ins[i], haxes[i], mc * hs, hs), dst_ref=_win(outs[i], haxes[i], (1 - mc) * hs, hs),
                                         send_sem=send_sems.at[i], recv_sem=recv_sems.at[i], device_id=(mx, my, 1 - mc),
                                         device_id_type=MESH).wait_recv()
        for cp in cps:
            cp.wait_send()

    return pl.pallas_call(
        body, name=name, out_shape=tuple(jax.ShapeDtypeStruct(a_.shape, a_.dtype) for a_ in arrays), in_specs=[HBM_SPEC] * n,
        out_specs=tuple([HBM_SPEC] * n), input_output_aliases={i: i for i in range(n)},
        scratch_shapes=[pltpu.SemaphoreType.DMA((n,)), pltpu.SemaphoreType.DMA((n,))])(*arrays)


WEIGHTS = ['c_ctx', 'w_mod', 'b_mod', 'norm1_w', 'norm2_w', 'final_norm_w', 's5_w_in', 's5_lam_re', 's5_lam_im', 's5_log_step', 's5_b_re', 's5_b_im', 's5_c_re', 's5_c_im', 's5_d', 's5_w_glu', 's5_w_out', 'hg_w_in', 'hg_lower_bounds', 'hg_gnorm_w', 'hg_w_out', 'ffn_w_up', 'ffn_conv_w', 'ffn_conv_b', 'ffn_w_down']
INPUTS = ['x', 'c', 'ctx', 'c_ctx', 'w_mod', 'b_mod', 'norm1_w', 'norm2_w', 'final_norm_w', 's5_w_in', 's5_lam_re', 's5_lam_im', 's5_log_step', 's5_b_re', 's5_b_im', 's5_c_re', 's5_c_im', 's5_d', 's5_w_glu', 's5_w_out', 'hg_w_in', 'hg_lower_bounds', 'hg_gnorm_w', 'hg_w_out', 'ffn_w_up', 'ffn_conv_w', 'ffn_conv_b', 'ffn_w_down', 'loss_target', 'm_c_ctx', 'm_w_mod', 'm_b_mod', 'm_norm1_w', 'm_norm2_w', 'm_final_norm_w', 'm_s5_w_in', 'm_s5_lam_re', 'm_s5_lam_im', 'm_s5_log_step', 'm_s5_b_re', 'm_s5_b_im', 'm_s5_c_re', 'm_s5_c_im', 'm_s5_d', 'm_s5_w_glu', 'm_s5_w_out', 'm_hg_w_in', 'm_hg_lower_bounds', 'm_hg_gnorm_w', 'm_hg_w_out', 'm_ffn_w_up', 'm_ffn_conv_w', 'm_ffn_conv_b', 'm_ffn_w_down', 'v_c_ctx', 'v_w_mod', 'v_b_mod', 'v_norm1_w', 'v_norm2_w', 'v_final_norm_w', 'v_s5_w_in', 'v_s5_lam_re', 'v_s5_lam_im', 'v_s5_log_step', 'v_s5_b_re', 'v_s5_b_im', 'v_s5_c_re', 'v_s5_c_im', 'v_s5_d', 'v_s5_w_glu', 'v_s5_w_out', 'v_hg_w_in', 'v_hg_lower_bounds', 'v_hg_gnorm_w', 'v_hg_w_out', 'v_ffn_w_up', 'v_ffn_conv_w', 'v_ffn_conv_b', 'v_ffn_w_down']
SHARD_AXIS = {"w_mod": 2, "s5_w_in": 1, "s5_w_glu": 1, "s5_w_out": 1, "hg_w_in": 2, "hg_lower_bounds": 2, "hg_w_out": 1,
              "ffn_w_up": 2, "ffn_conv_w": 2, "ffn_w_down": 1}
GATHER_F32 = ("hg_lower_bounds", "ffn_conv_w")
PACK_W = 1024
GRAD_WIRE = jnp.bfloat16


def _reduce_grads(a, grads, core):
    sharded = [n for n in WEIGHTS if n in SHARD_AXIS]
    small = [n for n in WEIGHTS if n not in SHARD_AXIS]
    flat = jnp.concatenate([grads[n].reshape(-1) for n in small])
    pad = (-flat.shape[0]) % (64 * PACK_W)
    small_pack = jnp.pad(flat, (0, pad)).reshape(1, -1, PACK_W)
    arrays = [grads[n] for n in sharded] + [small_pack]
    axes = [SHARD_AXIS[n] for n in sharded] + [1]
    haxes = [_half_axis(g_.shape, ax) for g_, ax in zip(arrays, axes)]
    tags = sharded + ["small"]
    t = pair_swap_halves(arrays, haxes, name="grad_pair_swap")
    h = [add_own_half(g_, t_, hx, core, out_dtype=GRAD_WIRE, name="grad_pair_add_" + tg) for g_, t_, hx, tg in zip(arrays, t, haxes, tags)]
    e = exchange_blocks(h, axes, name="grad_chip_exchange")
    s = [sum_blocks(e_, hx, core, name="grad_chip_sum_" + tg) for e_, hx, tg in zip(e, haxes, tags)]
    red = pair_fill_halves(s, haxes, name="grad_pair_fill")
    out = dict(zip(sharded, red[:-1]))
    sm = chip_allgather(red[-1][0], name="allgather_small_grads").reshape(-1)
    off = 0
    for n in small:
        out[n] = sm[off:off + math.prod(a[n].shape)].reshape(a[n].shape)
        off += math.prod(a[n].shape)
    return out


def _blockdiag_b(bb, kb):
    gl = S5_KIN // S5_GROUP
    x = bb.reshape(kb, gl, S5_GROUP, S5_STATE)
    return (x[:, :, :, None, :] * jnp.eye(gl, dtype=bb.dtype)[None, :, None, :, None]).reshape(kb, S5_KIN, S5_KST)


def _blockdiag_c(cc, kb):
    gl = S5_KIN // S5_GROUP
    x = cc.reshape(kb, gl, S5_GROUP, S5_STATE).transpose(0, 1, 3, 2)
    return (x[:, :, :, None, :] * jnp.eye(gl, dtype=cc.dtype)[None, :, None, :, None]).reshape(kb, S5_KST, S5_KIN)


def _diag_b(m, kb):
    gl = S5_KIN // S5_GROUP
    x = m.reshape(kb, gl, S5_GROUP, gl, S5_STATE)
    return jnp.stack([x[:, i, :, i, :] for i in range(gl)], axis=1).reshape(kb * gl, S5_GROUP, S5_STATE)


def _diag_c(m, kb):
    gl = S5_KIN // S5_GROUP
    x = m.reshape(kb, gl, S5_STATE, gl, S5_GROUP)
    return jnp.stack([x[:, i, :, i, :] for i in range(gl)], axis=1).transpose(0, 1, 3, 2).reshape(kb * gl, S5_GROUP, S5_STATE)


def kernel(x, c, ctx, c_ctx, w_mod, b_mod, norm1_w, norm2_w, final_norm_w, s5_w_in, s5_lam_re, s5_lam_im, s5_log_step, s5_b_re, s5_b_im, s5_c_re, s5_c_im, s5_d, s5_w_glu, s5_w_out, hg_w_in, hg_lower_bounds, hg_gnorm_w, hg_w_out, ffn_w_up, ffn_conv_w, ffn_conv_b, ffn_w_down, loss_target, m_c_ctx, m_w_mod, m_b_mod, m_norm1_w, m_norm2_w, m_final_norm_w, m_s5_w_in, m_s5_lam_re, m_s5_lam_im, m_s5_log_step, m_s5_b_re, m_s5_b_im, m_s5_c_re, m_s5_c_im, m_s5_d, m_s5_w_glu, m_s5_w_out, m_hg_w_in, m_hg_lower_bounds, m_hg_gnorm_w, m_hg_w_out, m_ffn_w_up, m_ffn_conv_w, m_ffn_conv_b, m_ffn_w_down, v_c_ctx, v_w_mod, v_b_mod, v_norm1_w, v_norm2_w, v_final_norm_w, v_s5_w_in, v_s5_lam_re, v_s5_lam_im, v_s5_log_step, v_s5_b_re, v_s5_b_im, v_s5_c_re, v_s5_c_im, v_s5_d, v_s5_w_glu, v_s5_w_out, v_hg_w_in, v_hg_lower_bounds, v_hg_gnorm_w, v_hg_w_out, v_ffn_w_up, v_ffn_conv_w, v_ffn_conv_b, v_ffn_w_down):
    a = dict(zip(INPUTS, (x, c, ctx, c_ctx, w_mod, b_mod, norm1_w, norm2_w, final_norm_w, s5_w_in, s5_lam_re, s5_lam_im, s5_log_step, s5_b_re, s5_b_im, s5_c_re, s5_c_im, s5_d, s5_w_glu, s5_w_out, hg_w_in, hg_lower_bounds, hg_gnorm_w, hg_w_out, ffn_w_up, ffn_conv_w, ffn_conv_b, ffn_w_down, loss_target, m_c_ctx, m_w_mod, m_b_mod, m_norm1_w, m_norm2_w, m_final_norm_w, m_s5_w_in, m_s5_lam_re, m_s5_lam_im, m_s5_log_step, m_s5_b_re, m_s5_b_im, m_s5_c_re, m_s5_c_im, m_s5_d, m_s5_w_glu, m_s5_w_out, m_hg_w_in, m_hg_lower_bounds, m_hg_gnorm_w, m_hg_w_out, m_ffn_w_up, m_ffn_conv_w, m_ffn_conv_b, m_ffn_w_down, v_c_ctx, v_w_mod, v_b_mod, v_norm1_w, v_norm2_w, v_final_norm_w, v_s5_w_in, v_s5_lam_re, v_s5_lam_im, v_s5_log_step, v_s5_b_re, v_s5_b_im, v_s5_c_re, v_s5_c_im, v_s5_d, v_s5_w_glu, v_s5_w_out, v_hg_w_in, v_hg_lower_bounds, v_hg_gnorm_w, v_hg_w_out, v_ffn_w_up, v_ffn_conv_w, v_ffn_conv_b, v_ffn_w_down)))
    nb, seq, d = x.shape
    assert nb == NB
    rc = nb * ctx.shape[1]
    cfg = {"rc": rc}
    f = a["ffn_w_down"].shape[1] * 4
    core = lax.axis_index("c").astype(jnp.int32).reshape(1)

    w = {n: a[n] for n in WEIGHTS if n not in SHARD_AXIS}
    chip = (2 * lax.axis_index("x") + lax.axis_index("y")).astype(jnp.int32).reshape(1)
    now = [("w_mod", a["w_mod"]), ("s5_w_in", a["s5_w_in"]), ("s5_w_glu", a["s5_w_glu"]), ("s5_w_out", a["s5_w_out"]),
           ("hg_lower_bounds", a["hg_lower_bounds"]), ("ffn_conv_w", a["ffn_conv_w"]), ("ffn_w_up0", a["ffn_w_up"][0:1]),
           ("ffn_w_down0", a["ffn_w_down"][0:1])]
    later = [("hg_w_in", a["hg_w_in"]), ("hg_w_out", a["hg_w_out"]), ("ffn_w_up1", a["ffn_w_up"][1:2]), ("ffn_w_down1", a["ffn_w_down"][1:2])]
    shard_axis = lambda n: SHARD_AXIS[n.rstrip("01")]
    place = lambda items: [place_shard(s_, shard_axis(n), chip, F32 if n in GATHER_F32 else MXU, name="place_" + n) for n, s_ in items]
    axes_now, axes_later = [shard_axis(n) for n, _ in now], [shard_axis(n) for n, _ in later]
    placed_now, placed_later = place(now), place(later)
    hax_now = [_half_axis(p_.shape, ax) for p_, ax in zip(placed_now, axes_now)]
    hax_later = [_half_axis(p_.shape, ax) for p_, ax in zip(placed_later, axes_later)]
    got = pair_fill_halves(gather_placed(placed_now, axes_now, hax_now, name="allgather_weights"), hax_now, name="allgather_pair_fill")
    w.update(dict(zip([n for n, _ in now], got)))
    send_l, recv_l, flying, token = gather_placed_start(placed_later, axes_later, hax_later, got[0], name="allgather_later_start")

    tmaj = lambda t: t.transpose(1, 0, 2).reshape(-1, t.shape[-1])
    x0 = jnp.concatenate([tmaj(ctx), tmaj(x)], axis=0)
    tgt = tmaj(a["loss_target"])
    c16 = jnp.concatenate([jnp.broadcast_to(c_ctx[None], (8, d)), c, c], axis=0) + token[0:1, 0:1]
    mt, scb = [], None
    for l in range(2):
        m_, scb = mod_fwd(c16, w["w_mod"][l], w["b_mod"][l][None], name=f"mod_fwd{l}")
        mt.append(m_)
    n1, n2 = w["norm1_w"], w["norm2_w"]
    w["ffn_w_up"], w["ffn_w_down"] = [w["ffn_w_up0"][0], None], [w["ffn_w_down0"][0], None]

    def ffn_fwd(l, h):
        u = mm(h, w["ffn_w_up"][l], name=f"ffn_up{l}")
        act = ffn_mid_fwd(cfg, u, w["ffn_conv_w"][l], w["ffn_conv_b"][l][None], name=f"ffn_mid{l}")
        return u, act, mm(act, w["ffn_w_down"][l], name=f"ffn_down{l}")

    def ffn_bwd(l, dfo, u, act, h):
        dact = mm(dfo, w["ffn_w_down"][l], tb=True, name=f"ffn_down_dx{l}")
        dwd = mm(act, dfo, ta=True, name=f"ffn_down_dw{l}")
        du, dcw, dcb = ffn_mid_bwd(cfg, dact, u, w["ffn_conv_w"][l], w["ffn_conv_b"][l][None], name=f"ffn_mid_bwd{l}")
        dh = mm(du, w["ffn_w_up"][l], tb=True, name=f"ffn_up_dx{l}")
        dwu = mm(h, du, ta=True, name=f"ffn_up_dw{l}")
        return dh, dwu, dcw, dcb[0], dwd

    g_, p_ = d // S5_GROUP, S5_STATE
    ns, kb = g_ * p_, d // S5_KIN
    s5p = (w["s5_lam_re"][0].reshape(2 * g_, p_), w["s5_lam_im"][0].reshape(2 * g_, p_), w["s5_log_step"][0].reshape(2 * g_, 1),
           w["s5_b_re"][0].transpose(0, 1, 3, 2).reshape(2 * g_, S5_GROUP, p_), w["s5_b_im"][0].transpose(0, 1, 3, 2).reshape(2 * g_, S5_GROUP, p_))
    ar, ai, bbr, bbi = s5_disc_fwd(*s5p, name="s5_disc")
    dsk = w["s5_d"]
    _, h1 = node_fwd(cfg, x0, None, None, 0, n1[0:1], mt[0], 0, name="node0a")
    u0 = mm(h1, w["s5_w_in"][0], name="s5_in")
    s5s, ys = [], []
    for dd in range(2):
        sl = slice(dd * g_, (dd + 1) * g_)
        a_r, a_i = ar[sl].reshape(1, ns), ai[sl].reshape(1, ns)
        a2 = (a_r * a_r - a_i * a_i, 2.0 * a_r * a_i)
        b_r, b_i = _blockdiag_b(bbr[sl], kb), _blockdiag_b(bbi[sl], kb)
        c_r, c_i = _blockdiag_c(w["s5_c_re"][0, dd], kb), _blockdiag_c(w["s5_c_im"][0, dd], kb)
        ak, ai_k = a_r.reshape(kb, 1, S5_KST), a_i.reshape(kb, 1, S5_KST)
        ab = (ak * b_r - ai_k * b_i, ak * b_i + ai_k * b_r)
        akc, aic = ak.reshape(kb, S5_KST, 1), ai_k.reshape(kb, S5_KST, 1)
        c2 = (akc * c_r - aic * c_i, akc * c_i + aic * c_r)
        bf = lambda t_: t_.astype(MXU)
        sre, sim, ere, eim, y_ = s5_scan_fwd(cfg, u0, a2[0], a2[1], bf(b_r), bf(b_i), bf(ab[0]), bf(ab[1]), bf(c_r), bf(c_i), rev=dd == 1,
                                             name=f"s5_scan{dd}")
        s5s.append((sre, sim, ere, eim, a2[0], a2[1], bf(b_r), bf(b_i), bf(c_r), bf(c_i), bf(c2[0]), bf(c2[1])))
        ys.append(y_)

    def glu_a(u, y0, y1, ds):
        yp = (ds * u + y0) + y1
        return yp, _gelu(yp)

    ypre, zgb = rowmap(glu_a, [u0, ys[0], ys[1]], [dsk], [(d, F32), (d, MXU)], name="s5_glu_a")
    tg = mm(zgb, w["s5_w_glu"][0], name="s5_glu")
    (z2,) = rowmap(lambda yp, t: _gelu(yp) * jax.nn.sigmoid(t), [ypre, tg], [], [(d, MXU)], name="s5_glu_b")
    y1a = mm(z2, w["s5_w_out"][0], name="s5_out")
    x1a, h2a = node_fwd(cfg, x0, y1a, mt[0], 2, n2[0:1], mt[0], 3, name="node0b")
    ufa, acta, foa = ffn_fwd(0, h2a)

    landed = gather_placed_wait(flying, send_l, recv_l, axes_later, hax_later, foa, name="allgather_later_wait")
    landed = pair_fill_halves(landed, hax_later, name="allgather_later_pair_fill")
    w["hg_w_in"], w["hg_w_out"], w["ffn_w_up"][1], w["ffn_w_down"][1] = landed[0], landed[1], landed[2][0], landed[3][0]
    x2a, h1b = node_fwd(cfg, x1a, foa, mt[0], 5, n1[1:2], mt[1], 0, name="node1a")
    z = mm(h1b, w["hg_w_in"][0], name="hg_in")
    e0, e1 = w["hg_lower_bounds"][:, 0, :], w["hg_lower_bounds"][:, 1, :]
    lb = hg_lb_fwd(e0, e1, name="hg_lb")
    gw = w["hg_gnorm_w"]
    o0, sin0 = hg_scan_fwd(cfg, z, lb[0:1], d_dir=0, name="hg_scan0")
    o1, sin1 = hg_scan_fwd(cfg, z, lb[1:2], d_dir=1, name="hg_scan1")
    onb = hg_read_fwd(o0, o1, z, gw, name="hg_read")
    y1b = mm(onb, w["hg_w_out"][0], name="hg_out")
    x1b, h2b = node_fwd(cfg, x2a, y1b, mt[1], 2, n2[1:2], mt[1], 3, name="node1b")
    ufb, actb, fob = ffn_fwd(1, h2b)
    loss_p, dx2b, dfob, dg2_1, dfnw = final_node(cfg, x1b, fob, mt[1], 5, w["final_norm_w"][None], tgt, name="final_node")

    gr = {}
    dh2b, dwu1, dcw1, dcb1, dwd1 = ffn_bwd(1, dfob, ufb, actb, h2b)
    dx1b, dy1b, dn2_1, dsh2_1, dsc2_1, dg1_1 = node_bwd(cfg, dx2b, dh2b, x1b, y1b, mt[1], 2, n2[1:2], mt[1], 3, name="node1b_bwd")
    don = mm(dy1b, w["hg_w_out"][0], tb=True, name="hg_out_dx")
    gr["hg_w_out"] = mm(onb, dy1b, ta=True, name="hg_out_dw")[None]
    do_, dgate_, dgw = hg_read_bwd(don, o0, o1, z, gw, name="hg_read_bwd")
    dq, dv, dxf, dlb0 = hg_scan_bwd(cfg, do_, z, lb[0:1], sin0, None, None, d_dir=0, name="hg_scan_bwd0")
    dq, dv, dxb, dlb1 = hg_scan_bwd(cfg, do_, z, lb[1:2], sin1, dq, dv, d_dir=1, name="hg_scan_bwd1")
    dz = jnp.concatenate([t_.astype(MXU) for t_ in (dq, dv, dxf, dxb, dgate_)], axis=1)
    dh1b = mm(dz, w["hg_w_in"][0], tb=True, name="hg_in_dx")
    gr["hg_w_in"] = mm(h1b, dz, ta=True, name="hg_in_dw")[None]
    de0, de1 = hg_lb_bwd(e0, e1, jnp.concatenate([dlb0, dlb1], axis=0), name="hg_lb_bwd")
    gr["hg_lower_bounds"] = jnp.stack([de0, de1], axis=1)
    gr["hg_gnorm_w"] = dgw
    dx2a, dfoa, dn1_1, dsh1_1, dsc1_1, dg2_0 = node_bwd(cfg, dx1b, dh1b, x2a, foa, mt[0], 5, n1[1:2], mt[1], 0, name="node1a_bwd")

    dh2a, dwu0, dcw0, dcb0, dwd0 = ffn_bwd(0, dfoa, ufa, acta, h2a)
    dx1a, dy1a, dn2_0, dsh2_0, dsc2_0, dg1_0 = node_bwd(cfg, dx2a, dh2a, x1a, y1a, mt[0], 2, n2[0:1], mt[0], 3, name="node0b_bwd")
    dz2 = mm(dy1a, w["s5_w_out"][0], tb=True, name="s5_out_dx")
    gr["s5_w_out"] = mm(z2, dy1a, ta=True, name="s5_out_dw")[None]

    def glu_b_bwd(dz2_, yp, t):
        zg, sg = _gelu(yp), jax.nn.sigmoid(t)
        return dz2_ * zg * sg * (1.0 - sg), dz2_ * sg

    dtg, dzg_dir = rowmap(glu_b_bwd, [dz2, ypre, tg], [], [(d, MXU), (d, F32)], name="s5_glu_b_bwd")
    dzg_mm = mm(dtg, w["s5_w_glu"][0], tb=True, name="s5_glu_dx")
    gr["s5_w_glu"] = mm(zgb, dtg, ta=True, name="s5_glu_dw")[None]

    def glu_a_bwd(dzd, dzm, yp, u, ds):
        _, vjp = jax.vjp(_gelu, yp)
        (dy,) = vjp(dzd + dzm)
        return dy, dy * ds, jnp.sum(dy * u, axis=0, keepdims=True)

    dyb, du, ddsk = rowmap(glu_a_bwd, [dzg_dir, dzg_mm, ypre, u0], [dsk], [(d, MXU), (d, F32)], [(1, d)], name="s5_glu_a_bwd")
    gr["s5_d"] = ddsk
    dar, dai, dbr, dbi, dcr, dci = [], [], [], [], [], []
    for dd in range(2):
        sre, sim, ere, eim = s5s[dd][:4]
        du, gre, gim, da_r, da_i = s5_scan_bwd(cfg, dyb, *s5s[dd], du, rev=dd == 1, name=f"s5_scan_bwd{dd}")
        dar.append(colsum(da_r, name=f"s5_da_re{dd}").reshape(g_, p_))
        dai.append(colsum(da_i, name=f"s5_da_im{dd}").reshape(g_, p_))
        dbr.append(_diag_b(blockdiag_tn(u0, gre, S5_KIN, S5_KST, name=f"s5_db_re{dd}"), kb))
        dbi.append(_diag_b(blockdiag_tn(u0, gim, S5_KIN, S5_KST, name=f"s5_db_im{dd}"), kb))
        dcr.append(_diag_c(blockdiag_tn(sre.reshape(-1, ns), dyb, S5_KST, S5_KIN, name=f"s5_dc_re{dd}"), kb))
        dci.append(_diag_c(blockdiag_tn(sim.reshape(-1, ns), dyb, S5_KST, S5_KIN, scale=-1.0, name=f"s5_dc_im{dd}"), kb))
    cat = lambda l_: jnp.concatenate(l_, axis=0)
    dlr, dli, dls, dbre, dbim = s5_disc_bwd(*s5p, cat(dar), cat(dai), cat(dbr), cat(dbi), name="s5_disc_bwd")
    gr["s5_lam_re"], gr["s5_lam_im"] = dlr.reshape(1, 2, g_, p_), dli.reshape(1, 2, g_, p_)
    gr["s5_log_step"] = dls.reshape(1, 2, g_)
    gr["s5_b_re"] = dbre.reshape(1, 2, g_, S5_GROUP, p_).transpose(0, 1, 2, 4, 3)
    gr["s5_b_im"] = dbim.reshape(1, 2, g_, S5_GROUP, p_).transpose(0, 1, 2, 4, 3)
    gr["s5_c_re"], gr["s5_c_im"] = jnp.stack(dcr)[None], jnp.stack(dci)[None]
    dh1 = mm(du, w["s5_w_in"][0], tb=True, name="s5_in_dx")
    gr["s5_w_in"] = mm(h1, du, ta=True, name="s5_in_dw")[None]
    dx0, _, dn1_0, dsh1_0, dsc1_0, _ = node_bwd(cfg, dx1a, dh1, x0, None, None, 0, n1[0:1], mt[0], 0, name="node0a_bwd")

    dmt = [jnp.concatenate([dsh1_0, dsc1_0, dg1_0, dsh2_0, dsc2_0, dg2_0], axis=1),
           jnp.concatenate([dsh1_1, dsc1_1, dg1_1, dsh2_1, dsc2_1, dg2_1], axis=1)]
    gr["w_mod"] = jnp.stack([mm(scb, dmt[l], ta=True, name=f"mod_dw{l}") for l in range(2)])
    gr["b_mod"] = jnp.concatenate([colsum(dmt[l], name=f"mod_db{l}") for l in range(2)], axis=0)
    dsc16 = [mm(dmt[l], w["w_mod"][l], tb=True, name=f"mod_dx{l}") for l in range(2)]
    gr["c_ctx"] = cctx_grad(c16, dsc16, name="c_ctx_grad")[0]
    gr["norm1_w"] = jnp.concatenate([dn1_0, dn1_1], axis=0)
    gr["norm2_w"] = jnp.concatenate([dn2_0, dn2_1], axis=0)
    gr["final_norm_w"] = dfnw[0]
    gr["ffn_w_up"], gr["ffn_conv_w"] = jnp.stack([dwu0, dwu1]), jnp.stack([dcw0, dcw1])
    gr["ffn_conv_b"], gr["ffn_w_down"] = jnp.stack([dcb0, dcb1]), jnp.stack([dwd0, dwd1])

    red = _reduce_grads(a, gr, core)
    loss = lax.psum(loss_p[0, 0], ("x", "y", "c"))
    grad_x = dx0[rc:].reshape(seq, nb, d).transpose(1, 0, 2)
    upd = {n: adamw(a[n], red[n], a["m_" + n], a["v_" + n], name="adamw_" + n) for n in WEIGHTS}
    return (loss, grad_x, *[red[n] for n in WEIGHTS], *[upd[n][0] for n in WEIGHTS], *[upd[n][1] for n in WEIGHTS],
            *[upd[n][2] for n in WEIGHTS])
```

```python
import functools
import math

import jax
import jax.numpy as jnp
from jax import lax
from jax.experimental import pallas as pl
from jax.experimental.pallas import tpu as pltpu

F32 = jnp.float32
BF = jnp.bfloat16
MXU = jnp.bfloat16

NORM_EPS = 1e-6
GRID_W = 64
N_MOD = 6
S5_GROUP = 16
S5_STATE = 64
S5_LAM_RE_MAX = -1e-4
S5_KIN = 256
S5_KST = S5_KIN // S5_GROUP * S5_STATE
HEAD = 128
CHUNK_ROWS = 128
N_PROJ = 5
NB = 4
ADAM_LR, ADAM_B1, ADAM_B2, ADAM_EPS, ADAM_WD, ADAM_STEP = 0.001, 0.9, 0.999, 1e-08, 0.01, 10
VMEM_LIMIT = 56 * 1024 * 1024
MESH = pl.DeviceIdType.MESH


def _tile(n, cap):
    if n <= cap:
        return n
    best = None
    for t in range(128, cap + 1, 128):
        if n % t == 0:
            best = t
    assert best is not None, (n, cap)
    return best


def _row_tile(r, width=1024):
    cap = max(8, (512 * 1024) // max(width, 1))
    return next((t for t in (512, 256, 128, 64, 32, 16, 8) if t <= cap and r % t == 0), r)


def _cp(sem):
    return pltpu.CompilerParams(dimension_semantics=sem, vmem_limit_bytes=VMEM_LIMIT)


def _dot(a, b, ca=1, cb=0):
    return lax.dot_general(a.astype(MXU), b.astype(MXU), (((ca,), (cb,)), ((), ())), preferred_element_type=F32)


def _dot3(m, x):
    hi = x.astype(MXU)
    r1 = x - hi.astype(F32)
    mid = r1.astype(MXU)
    lo = (r1 - mid.astype(F32)).astype(MXU)
    return _dot(m, hi) + _dot(m, mid) + _dot(m, lo)


def mm(a, b, *, ta=False, tb=False, out_dtype=F32, name):
    (kd, m) = a.shape if ta else a.shape[::-1]
    (n, kd2) = b.shape if tb else b.shape[::-1]
    assert kd == kd2, (a.shape, b.shape, ta, tb)
    tm, tn, tk = _tile(m, 1024), _tile(n, 1536), _tile(kd, 1024)
    nk = kd // tk

    def body(a_ref, b_ref, o_ref, acc_ref):
        k = pl.program_id(2)

        @pl.when(k == 0)
        def _():
            acc_ref[...] = jnp.zeros_like(acc_ref)

        acc_ref[...] += _dot(a_ref[...], b_ref[...], 0 if ta else 1, 1 if tb else 0)

        @pl.when(k == nk - 1)
        def _():
            o_ref[...] = acc_ref[...].astype(out_dtype)

    a_spec = pl.BlockSpec((tk, tm), lambda i, j, k: (k, i)) if ta else pl.BlockSpec((tm, tk), lambda i, j, k: (i, k))
    b_spec = pl.BlockSpec((tn, tk), lambda i, j, k: (j, k)) if tb else pl.BlockSpec((tk, tn), lambda i, j, k: (k, j))
    return pl.pallas_call(
        body, name=name, grid=(m // tm, n // tn, nk), in_specs=[a_spec, b_spec],
        out_specs=pl.BlockSpec((tm, tn), lambda i, j, k: (i, j)), out_shape=jax.ShapeDtypeStruct((m, n), out_dtype),
        scratch_shapes=[pltpu.VMEM((tm, tn), F32)], compiler_params=_cp(("parallel", "parallel", "arbitrary")))(a, b)


def blockdiag_tn(a, b, wa, wb, *, scale=1.0, name):
    rows = a.shape[0]
    kb = a.shape[1] // wa
    tr = _tile(rows, 1024)
    nr = rows // tr

    def body(a_ref, b_ref, o_ref):
        i = pl.program_id(1)

        @pl.when(i == 0)
        def _():
            o_ref[...] = jnp.zeros_like(o_ref)

        o_ref[0] += scale * _dot(a_ref[...], b_ref[...], 0, 0)

    return pl.pallas_call(
        body, name=name, grid=(kb, nr),
        in_specs=[pl.BlockSpec((tr, wa), lambda k, i: (i, k)), pl.BlockSpec((tr, wb), lambda k, i: (i, k))],
        out_specs=pl.BlockSpec((1, wa, wb), lambda k, i: (k, 0, 0)), out_shape=jax.ShapeDtypeStruct((kb, wa, wb), F32),
        compiler_params=_cp(("parallel", "arbitrary")))(a, b)


def _pat(v, p, op):
    tm, d = v.shape
    return op(v.reshape(tm // 8, 8, d), p[None]).reshape(tm, d)


def _norm_mod(x, nw, shift, scale):
    y = x * lax.rsqrt(jnp.mean(x * x, axis=-1, keepdims=True) + NORM_EPS) * nw
    return _pat(_pat(y, 1.0 + scale, jnp.multiply), shift, jnp.add)


def _mt_spec(d, nct):
    return pl.BlockSpec((8, N_MOD * d), lambda i: (jnp.where(i < nct, 0, 1), 0))


def _acc_spec(d, nct):
    return pl.BlockSpec((8, d), lambda i: (jnp.where(i < nct, 0, 1), 0))


def _rows(cfg):
    tm = min(512, cfg["rc"])
    return tm, cfg["rc"] // tm


def node_fwd(cfg, xp, y, mtg, gi, nw, mtn, si, *, name):
    r, d = xp.shape
    tm, nct = _rows(cfg)
    row = pl.BlockSpec((tm, d), lambda i: (i, 0))
    vec = pl.BlockSpec((1, d), lambda i: (0, 0))

    def body(*refs):
        if y is None:
            xp_ref, nw_ref, mtn_ref, h_ref = refs
            x = xp_ref[...]
        else:
            xp_ref, y_ref, mtg_ref, nw_ref, mtn_ref, xn_ref, h_ref = refs
            x = xp_ref[...] + _pat(y_ref[...], mtg_ref[:, gi * d:(gi + 1) * d], jnp.multiply)
            xn_ref[...] = x
        h_ref[...] = _norm_mod(x, nw_ref[...], mtn_ref[:, si * d:(si + 1) * d], mtn_ref[:, (si + 1) * d:(si + 2) * d]).astype(MXU)

    h_shape = jax.ShapeDtypeStruct((r, d), MXU)
    if y is None:
        h = pl.pallas_call(body, name=name, grid=(r // tm,), in_specs=[row, vec, _mt_spec(d, nct)], out_specs=row,
                           out_shape=h_shape, compiler_params=_cp(("parallel",)))(xp, nw, mtn)
        return xp, h
    return pl.pallas_call(body, name=name, grid=(r // tm,), in_specs=[row, row, _mt_spec(d, nct), vec, _mt_spec(d, nct)],
                          out_specs=(row, row), out_shape=(jax.ShapeDtypeStruct((r, d), F32), h_shape),
                          compiler_params=_cp(("parallel",)))(xp, y, mtg, nw, mtn)


def node_bwd(cfg, dxres, dh, xn, y, mtg, gi, nw, mtn, si, *, name):
    r, d = xn.shape
    tm, nct = _rows(cfg)
    row = pl.BlockSpec((tm, d), lambda i: (i, 0))
    vec = pl.BlockSpec((1, d), lambda i: (0, 0))
    has_y = y is not None

    def body(*refs):
        if has_y:
            dxres_ref, dh_ref, xn_ref, y_ref, mtg_ref, nw_ref, mtn_ref, dxn_ref, dy_ref, dnw_ref, dsh_ref, dsc_ref, dg_ref = refs
        else:
            dxres_ref, dh_ref, xn_ref, nw_ref, mtn_ref, dxn_ref, dnw_ref, dsh_ref, dsc_ref = refs
        i = pl.program_id(0)
        _, vjp = jax.vjp(_norm_mod, xn_ref[...], nw_ref[...], mtn_ref[:, si * d:(si + 1) * d], mtn_ref[:, (si + 1) * d:(si + 2) * d])
        dx, dnw, dsh, dsc = vjp(dh_ref[...])
        dx = dx + dxres_ref[...]
        dxn_ref[...] = dx

        @pl.when(i == 0)
        def _():
            dnw_ref[...] = jnp.zeros_like(dnw_ref)

        @pl.when((i == 0) | (i == nct))
        def _():
            dsh_ref[...] = jnp.zeros_like(dsh_ref)
            dsc_ref[...] = jnp.zeros_like(dsc_ref)
            if has_y:
                dg_ref[...] = jnp.zeros_like(dg_ref)

        dnw_ref[...] += dnw
        dsh_ref[...] += dsh
        dsc_ref[...] += dsc
        if has_y:
            dy_ref[...] = _pat(dx, mtg_ref[:, gi * d:(gi + 1) * d], jnp.multiply).astype(MXU)
            dg_ref[...] += jnp.sum((dx * y_ref[...]).reshape(tm // 8, 8, d), axis=0)

    acc = jax.ShapeDtypeStruct((16, d), F32)
    xs = jax.ShapeDtypeStruct((r, d), F32)
    if has_y:
        return pl.pallas_call(
            body, name=name, grid=(r // tm,), in_specs=[row, row, row, row, _mt_spec(d, nct), vec, _mt_spec(d, nct)],
            out_specs=(row, row, vec, _acc_spec(d, nct), _acc_spec(d, nct), _acc_spec(d, nct)),
            out_shape=(xs, jax.ShapeDtypeStruct((r, d), MXU), jax.ShapeDtypeStruct((1, d), F32), acc, acc, acc),
            compiler_params=_cp(("arbitrary",)))(dxres, dh, xn, y, mtg, nw, mtn)
    dxn, dnw, dsh, dsc = pl.pallas_call(
        body, name=name, grid=(r // tm,), in_specs=[row, row, row, vec, _mt_spec(d, nct)],
        out_specs=(row, vec, _acc_spec(d, nct), _acc_spec(d, nct)),
        out_shape=(xs, jax.ShapeDtypeStruct((1, d), F32), acc, acc), compiler_params=_cp(("arbitrary",)))(dxres, dh, xn, nw, mtn)
    return dxn, None, dnw, dsh, dsc, None


def final_node(cfg, xp, y, mtg, gi, fnw, tgt, *, name):
    r, d = xp.shape
    tm, nct = _rows(cfg)
    row = pl.BlockSpec((tm, d), lambda i: (i, 0))
    vec = pl.BlockSpec((1, d), lambda i: (0, 0))

    def norm(x, w):
        return x * lax.rsqrt(jnp.mean(x * x, axis=-1, keepdims=True) + NORM_EPS) * w

    def body(xp_ref, y_ref, mtg_ref, fnw_ref, tgt_ref, loss_ref, dx_ref, dy_ref, dg_ref, dfnw_ref):
        i = pl.program_id(0)
        g = mtg_ref[:, gi * d:(gi + 1) * d]
        x = xp_ref[...] + _pat(y_ref[...], g, jnp.multiply)
        out, vjp = jax.vjp(norm, x, fnw_ref[...])
        lat = i >= nct
        err = jnp.where(lat, out - tgt_ref[...], 0.0)
        dx, dfnw = vjp(err * (1.0 / d))

        @pl.when(i == 0)
        def _():
            loss_ref[...] = jnp.zeros_like(loss_ref)
            dfnw_ref[...] = jnp.zeros_like(dfnw_ref)

        @pl.when((i == 0) | (i == nct))
        def _():
            dg_ref[...] = jnp.zeros_like(dg_ref)

        loss_ref[...] += jnp.full(loss_ref.shape, 0.5 / d * jnp.sum(err * err), F32)
        dfnw_ref[...] += dfnw
        dx_ref[...] = dx
        dy_ref[...] = _pat(dx, g, jnp.multiply).astype(MXU)
        dg_ref[...] += jnp.sum((dx * y_ref[...]).reshape(tm // 8, 8, d), axis=0)

    return pl.pallas_call(
        body, name=name, grid=(r // tm,),
        in_specs=[row, row, _mt_spec(d, nct), vec, pl.BlockSpec((tm, d), lambda i: (jnp.maximum(i - nct, 0), 0))],
        out_specs=(pl.BlockSpec((8, 128), lambda i: (0, 0)), row, row, _acc_spec(d, nct), vec),
        out_shape=(jax.ShapeDtypeStruct((8, 128), F32), jax.ShapeDtypeStruct((r, d), F32), jax.ShapeDtypeStruct((r, d), MXU),
                   jax.ShapeDtypeStruct((16, d), F32), jax.ShapeDtypeStruct((1, d), F32)),
        compiler_params=_cp(("arbitrary",)))(xp, y, mtg, fnw, tgt)


def _silu(x):
    return x * jax.nn.sigmoid(x)


def mod_fwd(c16, w, b, *, name):
    d, n = w.shape
    tn = _tile(n, 1536)

    def body(c_ref, w_ref, b_ref, o_ref, s_ref):
        s = _silu(c_ref[...])
        s_ref[...] = s.astype(MXU)
        o_ref[...] = _dot(s, w_ref[...]) + b_ref[...]

    return pl.pallas_call(
        body, name=name, grid=(n // tn,),
        in_specs=[pl.BlockSpec((16, d), lambda j: (0, 0)), pl.BlockSpec((d, tn), lambda j: (0, j)), pl.BlockSpec((1, tn), lambda j: (0, j))],
        out_specs=(pl.BlockSpec((16, tn), lambda j: (0, j)), pl.BlockSpec((16, d), lambda j: (0, 0))),
        out_shape=(jax.ShapeDtypeStruct((16, n), F32), jax.ShapeDtypeStruct((16, d), MXU)),
        compiler_params=_cp(("arbitrary",)))(c16, w, b)


def colsum(x, *, name):
    def body(x_ref, o_ref):
        o_ref[...] = jnp.sum(x_ref[...], axis=0, keepdims=True)

    return pl.pallas_call(body, name=name, out_shape=jax.ShapeDtypeStruct((1, x.shape[1]), F32))(x)


def cctx_grad(c16, ds_list, *, name):
    def body(c_ref, *refs):
        o_ref = refs[-1]
        ds = refs[0][...]
        for r_ in refs[1:-1]:
            ds = ds + r_[...]
        _, vjp = jax.vjp(_silu, c_ref[...])
        (dc,) = vjp(ds)
        o_ref[...] = jnp.sum(dc[0:8], axis=0, keepdims=True)

    return pl.pallas_call(body, name=name, out_shape=jax.ShapeDtypeStruct((1, c16.shape[1]), F32))(c16, *ds_list)


def _s5_disc(lam_re, lam_im, log_step, b_re, b_im):
    lr = jnp.minimum(lam_re, S5_LAM_RE_MAX)
    li = lam_im
    dt = jnp.exp(log_step)
    mag = jnp.exp(lr * dt)
    abar_r = mag * jnp.cos(li * dt)
    abar_i = mag * jnp.sin(li * dt)
    den = lr * lr + li * li
    nr = abar_r - 1.0
    coef_r = (nr * lr + abar_i * li) / den
    coef_i = (abar_i * lr - nr * li) / den
    bbar_r = coef_r[:, None, :] * b_re - coef_i[:, None, :] * b_im
    bbar_i = coef_r[:, None, :] * b_im + coef_i[:, None, :] * b_re
    return abar_r, abar_i, bbar_r, bbar_i


def s5_disc_fwd(lam_re, lam_im, log_step, b_re, b_im, *, name):
    def body(lr, li, ls, br, bi, ar_o, ai_o, br_o, bi_o):
        ar_o[...], ai_o[...], br_o[...], bi_o[...] = _s5_disc(lr[...], li[...], ls[...], br[...], bi[...])

    s2, s3 = jax.ShapeDtypeStruct(lam_re.shape, F32), jax.ShapeDtypeStruct(b_re.shape, F32)
    return pl.pallas_call(body, name=name, out_shape=(s2, s2, s3, s3))(lam_re, lam_im, log_step, b_re, b_im)


def s5_disc_bwd(lam_re, lam_im, log_step, b_re, b_im, d_ar, d_ai, d_br, d_bi, *, name):
    def body(lr, li, ls, br, bi, dar, dai, dbr, dbi, o_lr, o_li, o_ls, o_br, o_bi):
        _, vjp = jax.vjp(_s5_disc, lr[...], li[...], ls[...], br[...], bi[...])
        o_lr[...], o_li[...], o_ls[...], o_br[...], o_bi[...] = vjp((dar[...], dai[...], dbr[...], dbi[...]))

    s2, s3 = jax.ShapeDtypeStruct(lam_re.shape, F32), jax.ShapeDtypeStruct(b_re.shape, F32)
    return pl.pallas_call(body, name=name, out_shape=(s2, s2, jax.ShapeDtypeStruct(log_step.shape, F32), s3, s3))(
        lam_re, lam_im, log_step, b_re, b_im, d_ar, d_ai, d_br, d_bi)


S5_LANES = 512


def _chunk_order(k, ncc, nch, rev):
    if not rev:
        return k
    return jnp.where(k < ncc, ncc - 1 - k, nch - 1 - (k - ncc))


def _cmul(ar, ai, xr, xi):
    return ar * xr - ai * xi, ar * xi + ai * xr


S5_FWD_ROWS = 256
S5_BWD_ROWS = 256


def _const_spec(a):
    return pl.BlockSpec(a.shape, lambda k: (0,) * a.ndim, pipeline_mode=pl.Buffered(1))


def _shift_steps(x, edge_tile, back):
    n = x.shape[0]
    row = lax.broadcasted_iota(jnp.int32, (8, x.shape[1]), 0)
    edge = pltpu.roll(edge_tile, 4, 0)
    if back:
        y = pltpu.roll(x, 4, 0)
        return jnp.concatenate([jnp.where(row < 4, edge, y[0:8]), y[8:]], axis=0)
    y = pltpu.roll(x, n - 4, 0)
    return jnp.concatenate([y[:n - 8], jnp.where(row >= 4, edge, y[n - 8:])], axis=0)


def s5_scan_fwd(cfg, u, a2_re, a2_im, bre, bim, abre, abim, cre, cim, *, rev, name):
    r, d = u.shape
    ns = a2_re.shape[1]
    kb = d // S5_KIN
    tcr = S5_FWD_ROWS
    n8 = tcr // 8
    q = S5_FWD_ROWS // S5_BWD_ROWS
    seg = n8 // q
    nch, ncc = r // tcr, cfg["rc"] // tcr
    lw = min(S5_LANES, ns)

    def body(u_ref, ar_ref, ai_ref, bre_ref, bim_ref, abre_ref, abim_ref, cre_ref, cim_ref, sre_ref, sim_ref, ere_ref, eim_ref, y_ref,
             st_re, st_im, u_edge):
        @pl.when(pl.program_id(0) == 0)
        def _():
            st_re[...] = jnp.zeros_like(st_re)
            st_im[...] = jnp.zeros_like(st_im)
            u_edge[...] = jnp.zeros_like(u_edge)

        u_ = u_ref[...]
        ub = u_.astype(MXU)
        upb = _shift_steps(u_, u_edge[...], back=not rev).astype(MXU)
        u_edge[...] = u_[0:8] if rev else u_[tcr - 8:tcr]
        for j in range(kb):
            uj, upj = ub[:, j * S5_KIN:(j + 1) * S5_KIN], upb[:, j * S5_KIN:(j + 1) * S5_KIN]
            sre_ref[:, :, j * S5_KST:(j + 1) * S5_KST] = (_dot(uj, bre_ref[j]) + _dot(upj, abre_ref[j])).reshape(n8, 8, S5_KST)
            sim_ref[:, :, j * S5_KST:(j + 1) * S5_KST] = (_dot(uj, bim_ref[j]) + _dot(upj, abim_ref[j])).reshape(n8, 8, S5_KST)
        for c in range(ns // lw):
            sl = slice(c * lw, (c + 1) * lw)
            ar = jnp.broadcast_to(ar_ref[:, sl], (8, lw))
            ai = jnp.broadcast_to(ai_ref[:, sl], (8, lw))

            def step(i, carry, sl=sl, ar=ar, ai=ai):
                sr, si = carry
                ii = n8 - 1 - i if rev else i
                pr, pi = _cmul(ar, ai, sr, si)
                sr, si = pr + sre_ref[ii, :, sl], pi + sim_ref[ii, :, sl]
                sre_ref[ii, :, sl] = sr
                sim_ref[ii, :, sl] = si
                return sr, si

            sr, si = st_re[:, sl], st_im[:, sl]
            for s_ in range(q):
                at = q - 1 - s_ if rev else s_
                ere_ref[at, :, sl] = sr
                eim_ref[at, :, sl] = si
                sr, si = lax.fori_loop(s_ * seg, (s_ + 1) * seg, step, (sr, si))
            st_re[:, sl] = sr
            st_im[:, sl] = si
        for j in range(kb):
            sr = sre_ref[:, :, j * S5_KST:(j + 1) * S5_KST].reshape(tcr, S5_KST)
            si = sim_ref[:, :, j * S5_KST:(j + 1) * S5_KST].reshape(tcr, S5_KST)
            y_ref[:, j * S5_KIN:(j + 1) * S5_KIN] = _dot(sr, cre_ref[j]) - _dot(si, cim_ref[j])

    cidx = functools.partial(_chunk_order, ncc=ncc, nch=nch, rev=rev)
    full = _const_spec
    st = pl.BlockSpec((n8, 8, ns), lambda k: (cidx(k), 0, 0))
    en = pl.BlockSpec((q, 8, ns), lambda k: (cidx(k), 0, 0))
    return pl.pallas_call(
        body, name=name, grid=(nch,),
        in_specs=[pl.BlockSpec((tcr, d), lambda k: (cidx(k), 0)), full(a2_re), full(a2_im), full(bre), full(bim), full(abre), full(abim),
                  full(cre), full(cim)],
        out_specs=(st, st, en, en, pl.BlockSpec((tcr, d), lambda k: (cidx(k), 0))),
        out_shape=(jax.ShapeDtypeStruct((r // 8, 8, ns), F32),) * 2 + (jax.ShapeDtypeStruct((q * nch, 8, ns), F32),) * 2
        + (jax.ShapeDtypeStruct((r, d), F32),),
        scratch_shapes=[pltpu.VMEM((8, ns), F32), pltpu.VMEM((8, ns), F32), pltpu.VMEM((8, d), F32)],
        compiler_params=_cp(("arbitrary",)))(u, a2_re, a2_im, bre, bim, abre, abim, cre, cim)


def s5_scan_bwd(cfg, dyb, sre, sim, ere, eim, a2_re, a2_im, bre, bim, cre, cim, c2re, c2im, du_in, *, rev, name):
    r, d = dyb.shape
    ns = a2_re.shape[1]
    kb = d // S5_KIN
    tcr = S5_BWD_ROWS
    n8 = tcr // 8
    nch, ncc = r // tcr, cfg["rc"] // tcr
    lw = min(S5_LANES, ns)

    def body(dy_ref, sre_ref, sim_ref, ere_ref, eim_ref, ar_ref, ai_ref, bre_ref, bim_ref, cre_ref, cim_ref, c2re_ref, c2im_ref, duin_ref,
             du_ref, gre_ref, gim_ref, dar_ref, dai_ref, g_re, g_im, gc_re, gc_im, dy_edge):
        k = pl.program_id(0)

        @pl.when(k == 0)
        def _():
            gc_re[...] = jnp.zeros_like(gc_re)
            gc_im[...] = jnp.zeros_like(gc_im)
            dar_ref[...] = jnp.zeros_like(dar_ref)
            dai_ref[...] = jnp.zeros_like(dai_ref)
            dy_edge[...] = jnp.zeros_like(dy_edge)

        dy32 = dy_ref[...].astype(F32)
        dy = dy32.astype(MXU)
        dyn = _shift_steps(dy32, dy_edge[...], back=rev).astype(MXU)
        dy_edge[...] = dy32[tcr - 8:tcr] if rev else dy32[0:8]
        for j in range(kb):
            dyj, dynj = dy[:, j * S5_KIN:(j + 1) * S5_KIN], dyn[:, j * S5_KIN:(j + 1) * S5_KIN]
            g_re[:, :, j * S5_KST:(j + 1) * S5_KST] = (_dot(dyj, cre_ref[j], 1, 1) + _dot(dynj, c2re_ref[j], 1, 1)).reshape(n8, 8, S5_KST)
            g_im[:, :, j * S5_KST:(j + 1) * S5_KST] = -(_dot(dyj, cim_ref[j], 1, 1) + _dot(dynj, c2im_ref[j], 1, 1)).reshape(n8, 8, S5_KST)
        first = lax.broadcasted_iota(jnp.int32, (8, lw), 0) < 4
        if rev:
            first = jnp.logical_not(first)
        for c in range(ns // lw):
            sl = slice(c * lw, (c + 1) * lw)
            ar = jnp.broadcast_to(ar_ref[:, sl], (8, lw))
            nai = -jnp.broadcast_to(ai_ref[:, sl], (8, lw))

            def step(i, carry, sl=sl, ar=ar, nai=nai):
                gr, gi, accr, acci = carry
                ii = i if rev else n8 - 1 - i
                pr, pi = _cmul(ar, nai, gr, gi)
                outr, outi = pr + g_re[ii, :, sl], pi + g_im[ii, :, sl]
                g_re[ii, :, sl] = outr
                g_im[ii, :, sl] = outi
                pv = jnp.clip(ii + 1 if rev else ii - 1, 0, n8 - 1)
                at_entry = (ii == n8 - 1) if rev else (ii == 0)
                pvr = jnp.where(at_entry, ere_ref[0, :, sl], sre_ref[pv, :, sl])
                pvi = jnp.where(at_entry, eim_ref[0, :, sl], sim_ref[pv, :, sl])
                spr = pltpu.roll(jnp.where(first, sre_ref[ii, :, sl], pvr), 4, 0)
                spi = pltpu.roll(jnp.where(first, sim_ref[ii, :, sl], pvi), 4, 0)
                accr = accr + outr * spr + outi * spi
                acci = acci + outi * spr - outr * spi
                return outr, outi, accr, acci

            gr, gi, accr, acci = lax.fori_loop(0, n8, step, (gc_re[:, sl], gc_im[:, sl], dar_ref[:, sl], dai_ref[:, sl]))
            gc_re[:, sl] = gr
            gc_im[:, sl] = gi
            dar_ref[:, sl] = accr
            dai_ref[:, sl] = acci
        for j in range(kb):
            gr = g_re[:, :, j * S5_KST:(j + 1) * S5_KST].reshape(tcr, S5_KST)
            gi = g_im[:, :, j * S5_KST:(j + 1) * S5_KST].reshape(tcr, S5_KST)
            gre_ref[:, j * S5_KST:(j + 1) * S5_KST] = gr.astype(MXU)
            gim_ref[:, j * S5_KST:(j + 1) * S5_KST] = gi.astype(MXU)
            du_ref[:, j * S5_KIN:(j + 1) * S5_KIN] = (duin_ref[:, j * S5_KIN:(j + 1) * S5_KIN]
                                                     + _dot(gr, bre_ref[j], 1, 1) + _dot(gi, bim_ref[j], 1, 1))

    def cidx(k):
        return _chunk_order(nch - 1 - k, ncc, nch, rev)

    full = _const_spec
    st = pl.BlockSpec((n8, 8, ns), lambda k: (cidx(k), 0, 0))
    en = pl.BlockSpec((1, 8, ns), lambda k: (cidx(k), 0, 0))
    rowd = pl.BlockSpec((tcr, d), lambda k: (cidx(k), 0))
    rown = pl.BlockSpec((tcr, ns), lambda k: (cidx(k), 0))
    acc = pl.BlockSpec((8, ns), lambda k: (0, 0))
    return pl.pallas_call(
        body, name=name, grid=(nch,),
        in_specs=[rowd, st, st, en, en, full(a2_re), full(a2_im), full(bre), full(bim), full(cre), full(cim), full(c2re), full(c2im), rowd],
        out_specs=(rowd, rown, rown, acc, acc),
        out_shape=(jax.ShapeDtypeStruct((r, d), F32), jax.ShapeDtypeStruct((r, ns), MXU), jax.ShapeDtypeStruct((r, ns), MXU),
                   jax.ShapeDtypeStruct((8, ns), F32), jax.ShapeDtypeStruct((8, ns), F32)),
        scratch_shapes=[pltpu.VMEM((n8, 8, ns), F32), pltpu.VMEM((n8, 8, ns), F32), pltpu.VMEM((8, ns), F32), pltpu.VMEM((8, ns), F32),
                        pltpu.VMEM((8, d), F32)],
        compiler_params=_cp(("arbitrary",)))(dyb, sre, sim, ere, eim, a2_re, a2_im, bre, bim, cre, cim, c2re, c2im, du_in)


def rowmap(fn, rows_in, vecs_in, outs, accs=(), *, name):
    r = rows_in[0].shape[0]
    tm = _row_tile(r, max(a.shape[1] for a in rows_in))
    nr, nv, no = len(rows_in), len(vecs_in), len(outs)

    def body(*refs):
        ins = [x[...] for x in refs[:nr + nv]]
        res = fn(*ins)
        if not isinstance(res, (tuple, list)):
            res = (res,)
        out_refs = refs[nr + nv:]
        for o_ref, v in zip(out_refs[:no], res[:no]):
            o_ref[...] = v.astype(o_ref.dtype)
        if accs:
            @pl.when(pl.program_id(0) == 0)
            def _():
                for a_ref in out_refs[no:]:
                    a_ref[...] = jnp.zeros_like(a_ref)
            for a_ref, v in zip(out_refs[no:], res[no:]):
                a_ref[...] += v

    in_specs = [pl.BlockSpec((tm, a.shape[1]), lambda i: (i, 0)) for a in rows_in]
    in_specs += [pl.BlockSpec(v.shape, lambda i, n=v.ndim: (0,) * n) for v in vecs_in]
    out_specs = [pl.BlockSpec((tm, w), lambda i: (i, 0)) for w, _ in outs] + [pl.BlockSpec(s, lambda i, n=len(s): (0,) * n) for s in accs]
    out_shape = [jax.ShapeDtypeStruct((r, w), dt) for w, dt in outs] + [jax.ShapeDtypeStruct(s, F32) for s in accs]
    res = pl.pallas_call(body, name=name, grid=(r // tm,), in_specs=in_specs, out_specs=tuple(out_specs), out_shape=tuple(out_shape),
                         compiler_params=_cp(("arbitrary",) if accs else ("parallel",)))(*rows_in, *vecs_in)
    return res


def _gelu(x):
    return jax.nn.gelu(x, approximate=True)


def _hg_lower_bound(e0, e1):
    m = jnp.maximum(e0, e1)
    a, b = jnp.exp(e0 - m), jnp.exp(e1 - m)
    return b / (a + b)


def _hg_gates(x, lb):
    logf = jnp.log(lb + (1.0 - lb) * jax.nn.sigmoid(x))
    return logf, (1.0 - lb) * jax.nn.sigmoid(-x)


def _hg_masks(rev):
    n = CHUNK_ROWS
    rr = lax.broadcasted_iota(jnp.int32, (n, n), 0)
    ss = lax.broadcasted_iota(jnp.int32, (n, n), 1)
    same = (rr % NB) == (ss % NB)
    causal = same & ((ss >= rr) if rev else (ss <= rr))
    anti = same & ((ss <= rr) if rev else (ss >= rr))
    end0 = 0 if rev else n - NB
    pick_end = ss == (end0 + rr % NB)
    return same, causal, anti, pick_end, end0


def _hg_expand(x):
    ex = lax.broadcasted_iota(jnp.int32, x.shape, 0) % NB
    return jnp.concatenate([jnp.where(ex == b, x, 0.0) for b in range(NB)], axis=1)


def _hg_fold(xe):
    kk = xe.shape[1] // NB
    ex = lax.broadcasted_iota(jnp.int32, (xe.shape[0], kk), 0) % NB
    out = jnp.zeros((xe.shape[0], kk), F32)
    for b in range(NB):
        out = out + jnp.where(ex == b, xe[:, b * kk:(b + 1) * kk], 0.0)
    return out


def _hg_chunk(q, v, x, lb, masks):
    same, causal, anti, pick_end, end0 = masks
    logf, kk = _hg_gates(x, lb)
    b = _dot3(causal.astype(MXU), logf)
    bend_t = _dot3(pick_end.astype(MXU), b)
    bend_flat = jnp.concatenate([b[end0 + i:end0 + i + 1] for i in range(NB)], axis=1)
    eb = jnp.exp(b)
    enb = jnp.exp(-b)
    ee = jnp.exp(bend_t - b)
    qd, kd, ke = q * eb, kk * enb, kk * ee
    att = jnp.where(causal, _dot(qd, kd, 1, 1), 0.0)
    decay = jnp.exp(bend_flat)
    return dict(same=same, causal=causal, anti=anti, logf=logf, kk=kk, b=b, eb=eb, enb=enb, ee=ee, qd=qd, kd=kd, ke=ke, att=att,
                decay=decay, qde=_hg_expand(qd), kee=_hg_expand(ke))


def _hg_chunk_order(cfg, r):
    nch, ncc = r // CHUNK_ROWS, cfg["rc"] // CHUNK_ROWS
    return nch, ncc


def hg_scan_fwd(cfg, z, lb, *, d_dir, name):
    r = z.shape[0]
    d = z.shape[1] // N_PROJ
    nh = d // HEAD
    rev = d_dir == 1
    nch, ncc = _hg_chunk_order(cfg, r)
    n = CHUNK_ROWS

    def body(q_ref, v_ref, x_ref, lb_ref, o_ref, sin_ref, stk):
        @pl.when(pl.program_id(0) == 0)
        def _():
            stk[...] = jnp.zeros_like(stk)

        masks = _hg_masks(rev)
        for h in range(nh):
            sl = slice(h * HEAD, (h + 1) * HEAD)
            s0 = stk[h]
            sin_ref[0, h] = s0
            v = v_ref[:, sl]
            c = _hg_chunk(q_ref[:, sl], v, x_ref[:, sl], lb_ref[:, sl], masks)
            o_ref[:, sl] = _dot(c["att"], v) + _dot(c["qde"], s0, 1, 1)
            stk[h] = s0 * c["decay"] + _dot(v, c["kee"], 0, 0)

    def cidx(k):
        return _chunk_order(k, ncc, nch, rev)

    blk = lambda p: pl.BlockSpec((n, d), lambda k: (cidx(k), p))
    return pl.pallas_call(
        body, name=name, grid=(nch,),
        in_specs=[blk(0), blk(1), blk(2 + d_dir), pl.BlockSpec((1, d), lambda k: (0, 0))],
        out_specs=(blk(0), pl.BlockSpec((1, nh, HEAD, NB * HEAD), lambda k: (cidx(k), 0, 0, 0))),
        out_shape=(jax.ShapeDtypeStruct((r, d), F32), jax.ShapeDtypeStruct((nch, nh, HEAD, NB * HEAD), F32)),
        scratch_shapes=[pltpu.VMEM((nh, HEAD, NB * HEAD), F32)], compiler_params=_cp(("arbitrary",)))(z, z, z, lb)


def hg_scan_bwd(cfg, do, z, lb, sin, dq_in, dv_in, *, d_dir, name):
    r = z.shape[0]
    d = z.shape[1] // N_PROJ
    nh = d // HEAD
    rev = d_dir == 1
    nch, ncc = _hg_chunk_order(cfg, r)
    n = CHUNK_ROWS
    has_in = dq_in is not None

    def body(*refs):
        if has_in:
            do_ref, q_ref, v_ref, x_ref, lb_ref, sin_ref, dqi_ref, dvi_ref, dq_ref, dv_ref, dx_ref, dlb_ref, dstk = refs
        else:
            do_ref, q_ref, v_ref, x_ref, lb_ref, sin_ref, dq_ref, dv_ref, dx_ref, dlb_ref, dstk = refs
        @pl.when(pl.program_id(0) == 0)
        def _():
            dstk[...] = jnp.zeros_like(dstk)
            dlb_ref[...] = jnp.zeros_like(dlb_ref)

        masks = _hg_masks(rev)
        ex = lax.broadcasted_iota(jnp.int32, (n, HEAD), 0) % NB
        for h in range(nh):
            sl = slice(h * HEAD, (h + 1) * HEAD)
            do_, q, v, x, lb_, s0, ds1 = do_ref[:, sl], q_ref[:, sl], v_ref[:, sl], x_ref[:, sl], lb_ref[:, sl], sin_ref[0, h], dstk[h]
            c = _hg_chunk(q, v, x, lb_, masks)
            datt = jnp.where(c["causal"], _dot(do_, v, 1, 1), 0.0)
            dv = _dot(c["att"], do_, 0, 0) + _dot(c["kee"], ds1, 1, 1)
            dqd = _dot(datt, c["kd"]) + _hg_fold(_dot(do_, s0))
            dkd = _dot(datt, c["qd"], 0, 0)
            dke = _hg_fold(_dot(v, ds1))
            dbend_flat = jnp.sum(ds1 * s0, axis=0, keepdims=True) * c["decay"]
            dstk[h] = _dot(do_, c["qde"], 0, 0) + ds1 * c["decay"]
            dq = dqd * c["eb"]
            dk = dkd * c["enb"] + dke * c["ee"]
            db = dqd * c["qd"] - dkd * c["kd"] - dke * c["ke"]
            dbend_rows = jnp.zeros((n, HEAD), F32)
            for b in range(NB):
                dbend_rows = dbend_rows + jnp.where(ex == b, dbend_flat[:, b * HEAD:(b + 1) * HEAD], 0.0)
            dlogf = _dot3(c["anti"].astype(MXU), db) + _dot3(c["same"].astype(MXU), dke * c["ke"]) + dbend_rows
            _, vjp = jax.vjp(_hg_gates, x, lb_)
            dx, dlb = vjp((dlogf, dk))
            if has_in:
                dq = dq + dqi_ref[:, sl]
                dv = dv + dvi_ref[:, sl]
            dq_ref[:, sl] = dq
            dv_ref[:, sl] = dv
            dx_ref[:, sl] = dx
            dlb_ref[:, sl] += dlb

    def cidx(k):
        return _chunk_order(nch - 1 - k, ncc, nch, rev)

    blk = lambda p: pl.BlockSpec((n, d), lambda k: (cidx(k), p))
    vec = pl.BlockSpec((1, d), lambda k: (0, 0))
    in_specs = [blk(0), blk(0), blk(1), blk(2 + d_dir), vec, pl.BlockSpec((1, nh, HEAD, NB * HEAD), lambda k: (cidx(k), 0, 0, 0))]
    args = [do, z, z, z, lb, sin]
    if has_in:
        in_specs += [blk(0), blk(0)]
        args += [dq_in, dv_in]
    rd = jax.ShapeDtypeStruct((r, d), F32)
    return pl.pallas_call(
        body, name=name, grid=(nch,), in_specs=in_specs, out_specs=(blk(0), blk(0), blk(0), vec),
        out_shape=(rd, rd, rd, jax.ShapeDtypeStruct((1, d), F32)),
        scratch_shapes=[pltpu.VMEM((nh, HEAD, NB * HEAD), F32)], compiler_params=_cp(("arbitrary",)))(*args)


def _hg_read(o, g, gw):
    on = o * lax.rsqrt(jnp.mean(o * o, axis=-1, keepdims=True) + NORM_EPS) * gw
    return on * jax.nn.sigmoid(g)


def hg_read_fwd(of, ob, z, gw, *, name):
    r, d = of.shape
    nh = d // HEAD
    tm = _row_tile(r)

    def body(of_ref, ob_ref, g_ref, gw_ref, o_ref):
        for h in range(nh):
            sl = slice(h * HEAD, (h + 1) * HEAD)
            o_ref[:, sl] = _hg_read(of_ref[:, sl] + ob_ref[:, sl], g_ref[:, sl], gw_ref[...]).astype(MXU)

    blk = pl.BlockSpec((tm, d), lambda i: (i, 0))
    return pl.pallas_call(
        body, name=name, grid=(r // tm,),
        in_specs=[blk, blk, pl.BlockSpec((tm, d), lambda i: (i, N_PROJ - 1)), pl.BlockSpec((1, HEAD), lambda i: (0, 0))],
        out_specs=blk, out_shape=jax.ShapeDtypeStruct((r, d), MXU), compiler_params=_cp(("parallel",)))(of, ob, z, gw)


def hg_read_bwd(don, of, ob, z, gw, *, name):
    r, d = of.shape
    nh = d // HEAD
    tm = _row_tile(r)

    def body(don_ref, of_ref, ob_ref, g_ref, gw_ref, do_ref, dg_ref, dgw_ref):
        @pl.when(pl.program_id(0) == 0)
        def _():
            dgw_ref[...] = jnp.zeros_like(dgw_ref)

        for h in range(nh):
            sl = slice(h * HEAD, (h + 1) * HEAD)
            _, vjp = jax.vjp(_hg_read, of_ref[:, sl] + ob_ref[:, sl], g_ref[:, sl], gw_ref[...])
            do_ref[:, sl], dg_ref[:, sl], dgw = vjp(don_ref[:, sl])
            dgw_ref[...] += dgw

    blk = pl.BlockSpec((tm, d), lambda i: (i, 0))
    vec = pl.BlockSpec((1, HEAD), lambda i: (0, 0))
    rd = jax.ShapeDtypeStruct((r, d), F32)
    return pl.pallas_call(
        body, name=name, grid=(r // tm,),
        in_specs=[blk, blk, blk, pl.BlockSpec((tm, d), lambda i: (i, N_PROJ - 1)), vec],
        out_specs=(blk, blk, vec), out_shape=(rd, rd, jax.ShapeDtypeStruct((1, HEAD), F32)),
        compiler_params=_cp(("arbitrary",)))(don, of, ob, z, gw)


FFN_COLS = 256


def _seg_masks(cfg, tr, i):
    t = lax.broadcasted_iota(jnp.int32, (tr, FFN_COLS), 0) // NB
    ctx_steps = cfg["rc"] // NB
    pos = jnp.where(i == 0, t % ctx_steps, t % GRID_W)
    last = jnp.where(i == 0, ctx_steps - 1, GRID_W - 1)
    return pos == 0, pos == last


def _prev(x, start):
    return jnp.where(start, 0.0, pltpu.roll(x, NB, 0))


def _next(x, end):
    return jnp.where(end, 0.0, pltpu.roll(x, x.shape[0] - NB, 0))


def _conv3(u, w, b, start, end):
    return ((b + _prev(u, start) * w[0:1]) + u * w[1:2]) + _next(u, end) * w[2:3]


def ffn_mid_fwd(cfg, u, cw, cb, *, name):
    r, f2 = u.shape
    f = f2 // 2
    tr = cfg["rc"]
    nf = f // FFN_COLS

    def body(ua_ref, ug_ref, wa_ref, wg_ref, ba_ref, bg_ref, o_ref):
        start, end = _seg_masks(cfg, tr, pl.program_id(0))
        a = _conv3(ua_ref[...], wa_ref[...], ba_ref[...], start, end)
        g = _conv3(ug_ref[...], wg_ref[...], bg_ref[...], start, end)
        o_ref[...] = (_silu(a) * g).astype(MXU)

    ca = lambda rows: pl.BlockSpec((rows, FFN_COLS), lambda i, j: (i if rows == tr else 0, j))
    cg = lambda rows: pl.BlockSpec((rows, FFN_COLS), lambda i, j: (i if rows == tr else 0, j + nf))
    return pl.pallas_call(
        body, name=name, grid=(r // tr, nf), in_specs=[ca(tr), cg(tr), ca(3), cg(3), ca(1), cg(1)], out_specs=ca(tr),
        out_shape=jax.ShapeDtypeStruct((r, f), MXU), compiler_params=_cp(("parallel", "parallel")))(u, u, cw, cw, cb, cb)


def ffn_mid_bwd(cfg, dact, u, cw, cb, *, name):
    r, f2 = u.shape
    f = f2 // 2
    tr = cfg["rc"]
    nf = f // FFN_COLS

    def body(da_ref, us_ref, up_ref, ws_ref, wp_ref, bs_ref, bp_ref, du_ref, dcw_ref, dcb_ref):
        i = pl.program_id(1)
        is_a = pl.program_id(0) < nf
        start, end = _seg_masks(cfg, tr, i)
        @pl.when(i == 0)
        def _():
            dcw_ref[...] = jnp.zeros_like(dcw_ref)
            dcb_ref[...] = jnp.zeros_like(dcb_ref)

        def finish(dc):
            us, ws = us_ref[...], ws_ref[...]
            du_ref[...] = (ws[1:2] * dc + ws[0:1] * _next(dc, end) + ws[2:3] * _prev(dc, start)).astype(MXU)
            dcw_ref[...] += jnp.concatenate([jnp.sum(dc * _prev(us, start), axis=0, keepdims=True), jnp.sum(dc * us, axis=0, keepdims=True),
                                             jnp.sum(dc * _next(us, end), axis=0, keepdims=True)], axis=0)
            dcb_ref[...] += jnp.sum(dc, axis=0, keepdims=True)

        @pl.when(is_a)
        def _():
            cs = _conv3(us_ref[...], ws_ref[...], bs_ref[...], start, end)
            cp = _conv3(up_ref[...], wp_ref[...], bp_ref[...], start, end)
            sg = jax.nn.sigmoid(cs)
            finish(da_ref[...] * cp * (sg * (1.0 + cs * (1.0 - sg))))

        @pl.when(jnp.logical_not(is_a))
        def _():
            finish(da_ref[...] * _silu(_conv3(up_ref[...], wp_ref[...], bp_ref[...], start, end)))

    cs_ = lambda rows: pl.BlockSpec((rows, FFN_COLS), lambda j, i: (i if rows == tr else 0, j))
    cp_ = lambda rows: pl.BlockSpec((rows, FFN_COLS), lambda j, i: (i if rows == tr else 0, (j + nf) % (2 * nf)))
    return pl.pallas_call(
        body, name=name, grid=(2 * nf, r // tr),
        in_specs=[pl.BlockSpec((tr, FFN_COLS), lambda j, i: (i, j % nf)), cs_(tr), cp_(tr), cs_(3), cp_(3), cs_(1), cp_(1)],
        out_specs=(cs_(tr), cs_(3), cs_(1)),
        out_shape=(jax.ShapeDtypeStruct((r, f2), MXU), jax.ShapeDtypeStruct((3, f2), F32), jax.ShapeDtypeStruct((1, f2), F32)),
        compiler_params=_cp(("parallel", "arbitrary")))(dact, u, u, cw, cw, cb, cb)


def hg_lb_fwd(e0, e1, *, name):
    def body(a, b, o):
        o[...] = _hg_lower_bound(a[...], b[...])

    return pl.pallas_call(body, name=name, out_shape=jax.ShapeDtypeStruct(e0.shape, F32))(e0, e1)


def hg_lb_bwd(e0, e1, dlb, *, name):
    def body(a, b, g, oa, ob):
        _, vjp = jax.vjp(_hg_lower_bound, a[...], b[...])
        oa[...], ob[...] = vjp(g[...])

    s = jax.ShapeDtypeStruct(e0.shape, F32)
    return pl.pallas_call(body, name=name, out_shape=(s, s))(e0, e1, dlb)


def _adamw(w, g, m, v):
    m = ADAM_B1 * m + (1.0 - ADAM_B1) * g
    v = ADAM_B2 * v + (1.0 - ADAM_B2) * jnp.square(g)
    m_hat = m / (1.0 - ADAM_B1 ** ADAM_STEP)
    v_hat = v / (1.0 - ADAM_B2 ** ADAM_STEP)
    delta = -ADAM_LR * (m_hat / (jnp.sqrt(v_hat) + ADAM_EPS) + ADAM_WD * w)
    return delta, m, v


def _as2d(a):
    if a.ndim >= 2 and a.shape[-1] % 128 == 0:
        return a.reshape(-1, a.shape[-1])
    return a.reshape(-1, 128) if a.size % 128 == 0 else a.reshape(1, -1)


def adamw(w, g, m, v, *, name):
    w2 = _as2d(w)
    outs = rowmap(_adamw, [w2, _as2d(g), _as2d(m), _as2d(v)], [], [(w2.shape[1], F32)] * 3, name=name)
    return tuple(o.reshape(w.shape) for o in outs)


HBM_SPEC = pl.BlockSpec(memory_space=pltpu.HBM)


def _place():
    mx, my, mc = lax.axis_index("x"), lax.axis_index("y"), lax.axis_index("c")
    others = [(1 - mx, my), (mx, 1 - my), (1 - mx, 1 - my)]
    return mx, my, mc, others


def chip_allgather(x, *, name):
    def body(x_ref, o_ref, send_sems, recv_sems, local_sem):
        mx, my, mc, others = _place()
        me = 2 * mx + my
        mine = pltpu.make_async_copy(x_ref, o_ref.at[me], local_sem)
        mine.start()
        sends = [pltpu.make_async_remote_copy(src_ref=x_ref, dst_ref=o_ref.at[me], send_sem=send_sems.at[j], recv_sem=recv_sems.at[j],
                                              device_id=(px, py, mc), device_id_type=MESH) for j, (px, py) in enumerate(others)]
        for cp in sends:
            cp.start()
        for j, (px, py) in enumerate(others):
            pltpu.make_async_remote_copy(src_ref=x_ref, dst_ref=o_ref.at[2 * px + py], send_sem=send_sems.at[j], recv_sem=recv_sems.at[j],
                                         device_id=(px, py, mc), device_id_type=MESH).wait_recv()
        for cp in sends:
            cp.wait_send()
        mine.wait()

    return pl.pallas_call(
        body, name=name, out_shape=jax.ShapeDtypeStruct((4,) + x.shape, x.dtype), in_specs=[HBM_SPEC], out_specs=HBM_SPEC,
        scratch_shapes=[pltpu.SemaphoreType.DMA((3,)), pltpu.SemaphoreType.DMA((3,)), pltpu.SemaphoreType.DMA])(x)


def _win(ref, axis, start, size):
    idx = [slice(None)] * len(ref.shape)
    idx[axis] = pl.ds(start, size)
    return ref.at[tuple(idx)]


def _half_axis(shape, ax):
    if shape[0] == 2:
        return 0
    return 2 if ax == 1 else 1


def _cut(shape, axis, parts):
    return shape[:axis] + (shape[axis] // parts,) + shape[axis + 1:]


def _hbm_call(body, arrays, out_shapes, sems, name):
    n_in = len(arrays)
    return pl.pallas_call(body, name=name, out_shape=tuple(out_shapes), in_specs=[HBM_SPEC] * n_in, out_specs=tuple([HBM_SPEC] * len(out_shapes)),
                          scratch_shapes=sems)(*arrays)


def place_shard(shard, ax, chip, dtype, *, name):
    l, r, c = shard.shape
    tr = _row_tile(r, c)
    per_block = (l, r // tr, 1)[ax]

    def omap(li, ri, cref):
        idx = [li, ri, 0]
        idx[ax] = idx[ax] + cref[0] * per_block
        return tuple(idx)

    def body(c_ref, s_ref, o_ref):
        o_ref[...] = s_ref[...].astype(dtype)

    full = shard.shape[:ax] + (4 * shard.shape[ax],) + shard.shape[ax + 1:]
    return pl.pallas_call(
        body, name=name, out_shape=jax.ShapeDtypeStruct(full, dtype),
        grid_spec=pltpu.PrefetchScalarGridSpec(
            num_scalar_prefetch=1, grid=(l, r // tr),
            in_specs=[pl.BlockSpec((1, tr, c), lambda li, ri, cref: (li, ri, 0))], out_specs=pl.BlockSpec((1, tr, c), omap)),
        compiler_params=_cp(("parallel", "parallel")))(chip, shard)


def gather_placed(arrays, axes, haxes, *, name):
    n = len(arrays)

    def body(*refs):
        ins, outs = refs[:n], refs[n:2 * n]
        send_sems, recv_sems = refs[2 * n:]
        mx, my, mc, others = _place()
        me = 2 * mx + my

        def part(ref, i, chip):
            sz, hs = arrays[i].shape[axes[i]] // 4, arrays[i].shape[haxes[i]] // 2
            return _win(_win(ref, axes[i], chip * sz, sz), haxes[i], mc * hs, hs)

        sends = []
        for i in range(n):
            for j, (px, py) in enumerate(others):
                rc = pltpu.make_async_remote_copy(src_ref=part(ins[i], i, me), dst_ref=part(outs[i], i, me), send_sem=send_sems.at[i, j],
                                                  recv_sem=recv_sems.at[i, j], device_id=(px, py, mc), device_id_type=MESH)
                rc.start()
                sends.append(rc)
        for i in range(n):
            for j, (px, py) in enumerate(others):
                pltpu.make_async_remote_copy(src_ref=part(ins[i], i, me), dst_ref=part(outs[i], i, 2 * px + py), send_sem=send_sems.at[i, j],
                                             recv_sem=recv_sems.at[i, j], device_id=(px, py, mc), device_id_type=MESH).wait_recv()
        for rc in sends:
            rc.wait_send()

    return pl.pallas_call(
        body, name=name, out_shape=tuple(jax.ShapeDtypeStruct(a_.shape, a_.dtype) for a_ in arrays), in_specs=[HBM_SPEC] * n,
        out_specs=tuple([HBM_SPEC] * n), input_output_aliases={i: i for i in range(n)},
        scratch_shapes=[pltpu.SemaphoreType.DMA((n, 3)), pltpu.SemaphoreType.DMA((n, 3))])(*arrays)


SEM_SPEC = pl.BlockSpec(memory_space=pltpu.SEMAPHORE)
SPLIT_COPY = pltpu.CompilerParams(has_side_effects=pltpu.SideEffectType.DATAFLOW_SIDE_EFFECTING)


def _gather_part(ref, shape, ax, hax, chip, core):
    sz, hs = shape[ax] // 4, shape[hax] // 2
    return _win(_win(ref, ax, chip * sz, sz), hax, core * hs, hs)


def gather_placed_start(arrays, axes, haxes, after, *, name):
    n = len(arrays)

    m = 3 * n

    def body(*refs):
        ins, send_sems, recv_sems = refs[:n], refs[n + 1:n + 1 + m], refs[n + 1 + m:n + 1 + 2 * m]
        token = refs[2 * n + 1 + 2 * m]
        mx, my, mc, others = _place()
        me = 2 * mx + my
        for i in range(n):
            for j, (px, py) in enumerate(others):
                part = _gather_part(ins[i], arrays[i].shape, axes[i], haxes[i], me, mc)
                pltpu.make_async_remote_copy(src_ref=part, dst_ref=part, send_sem=send_sems[3 * i + j], recv_sem=recv_sems[3 * i + j],
                                             device_id=(px, py, mc), device_id_type=MESH).start()
        token[...] = jnp.zeros_like(token)

    hbm = [pltpu.with_memory_space_constraint(a_, pltpu.HBM) for a_ in arrays]
    out = pl.pallas_call(
        body, name=name,
        out_shape=tuple([pltpu.SemaphoreType.DMA(())] * (2 * m)) + tuple(pltpu.HBM(a_.shape, a_.dtype) for a_ in arrays)
        + (jax.ShapeDtypeStruct((8, 128), F32),),
        in_specs=[HBM_SPEC] * n + [pl.BlockSpec(memory_space=pl.ANY)],
        out_specs=tuple([SEM_SPEC] * (2 * m)) + tuple([HBM_SPEC] * n) + (pl.BlockSpec(memory_space=pltpu.VMEM),),
        input_output_aliases={i: 2 * m + i for i in range(n)}, compiler_params=SPLIT_COPY)(*hbm, after)
    return list(out[:m]), list(out[m:2 * m]), list(out[2 * m:2 * m + n]), out[2 * m + n]


def gather_placed_wait(arrays, send_sems, recv_sems, axes, haxes, after, *, name):
    n = len(arrays)

    m = 3 * n

    def body(*refs):
        ins, send_refs, recv_refs = refs[:n], refs[n:n + m], refs[n + m:n + 2 * m]
        mx, my, mc, others = _place()
        me = 2 * mx + my
        for i in range(n):
            for j, (px, py) in enumerate(others):
                cp = pltpu.make_async_remote_copy(
                    src_ref=_gather_part(ins[i], arrays[i].shape, axes[i], haxes[i], me, mc),
                    dst_ref=_gather_part(ins[i], arrays[i].shape, axes[i], haxes[i], 2 * px + py, mc),
                    send_sem=send_refs[3 * i + j], recv_sem=recv_refs[3 * i + j], device_id=(px, py, mc), device_id_type=MESH)
                cp.wait_send()
                cp.wait_recv()

    out = pl.pallas_call(
        body, name=name, out_shape=tuple(pltpu.HBM(a_.shape, a_.dtype) for a_ in arrays),
        in_specs=[HBM_SPEC] * n + [SEM_SPEC] * (2 * m) + [pl.BlockSpec(memory_space=pl.ANY)], out_specs=tuple([HBM_SPEC] * n),
        input_output_aliases={i: i for i in range(n)}, compiler_params=SPLIT_COPY)(*arrays, *send_sems, *recv_sems, after)
    return list(out)


def pair_swap_halves(arrays, haxes, *, name):
    n = len(arrays)

    def body(*refs):
        ins, outs = refs[:n], refs[n:2 * n]
        send_sems, recv_sems = refs[2 * n:]
        mx, my, mc, _ = _place()
        cps = []
        for i in range(n):
            hs = arrays[i].shape[haxes[i]] // 2
            cp = pltpu.make_async_remote_copy(src_ref=_win(ins[i], haxes[i], (1 - mc) * hs, hs), dst_ref=outs[i], send_sem=send_sems.at[i],
                                              recv_sem=recv_sems.at[i], device_id=(mx, my, 1 - mc), device_id_type=MESH)
            cp.start()
            cps.append(cp)
        for cp in cps:
            cp.wait()

    outs = [jax.ShapeDtypeStruct(_cut(a_.shape, h_, 2), a_.dtype) for a_, h_ in zip(arrays, haxes)]
    return _hbm_call(body, arrays, outs, [pltpu.SemaphoreType.DMA((n,)), pltpu.SemaphoreType.DMA((n,))], name)


def add_own_half(g, t, hax, core, *, out_dtype, name):
    l, r, c = t.shape
    tr = _row_tile(r, c)
    per_half = (l, r // tr, 1)[hax]

    def imap(li, ri, cref):
        idx = [li, ri, 0]
        idx[hax] = idx[hax] + cref[0] * per_half
        return tuple(idx)

    def body(c_ref, g_ref, t_ref, o_ref):
        o_ref[...] = (g_ref[...] + t_ref[...]).astype(out_dtype)

    return pl.pallas_call(
        body, name=name, out_shape=jax.ShapeDtypeStruct(t.shape, out_dtype),
        grid_spec=pltpu.PrefetchScalarGridSpec(
            num_scalar_prefetch=1, grid=(l, r // tr),
            in_specs=[pl.BlockSpec((1, tr, c), imap), pl.BlockSpec((1, tr, c), lambda li, ri, cref: (li, ri, 0))],
            out_specs=pl.BlockSpec((1, tr, c), lambda li, ri, cref: (li, ri, 0))),
        compiler_params=_cp(("parallel", "parallel")))(core, g, t)


def exchange_blocks(arrays, axes, *, name):
    n = len(arrays)

    def body(*refs):
        ins, outs = refs[:n], refs[n:2 * n]
        send_sems, recv_sems, local_sems = refs[2 * n:]
        mx, my, mc, others = _place()
        me = 2 * mx + my
        waits = []
        for i in range(n):
            sz = arrays[i].shape[axes[i]] // 4
            cp = pltpu.make_async_copy(_win(ins[i], axes[i], me * sz, sz), outs[i].at[me], local_sems.at[i])
            cp.start()
            waits.append(cp.wait)
            for j, (px, py) in enumerate(others):
                rc = pltpu.make_async_remote_copy(src_ref=_win(ins[i], axes[i], (2 * px + py) * sz, sz), dst_ref=outs[i].at[me],
                                                  send_sem=send_sems.at[i, j], recv_sem=recv_sems.at[i, j], device_id=(px, py, mc),
                                                  device_id_type=MESH)
                rc.start()
                waits.append(rc.wait_send)
        for i in range(n):
            sz = arrays[i].shape[axes[i]] // 4
            for j, (px, py) in enumerate(others):
                pltpu.make_async_remote_copy(src_ref=_win(ins[i], axes[i], me * sz, sz), dst_ref=outs[i].at[2 * px + py],
                                             send_sem=send_sems.at[i, j], recv_sem=recv_sems.at[i, j], device_id=(px, py, mc),
                                             device_id_type=MESH).wait_recv()
        for w_ in waits:
            w_()

    outs = [jax.ShapeDtypeStruct((4,) + _cut(a_.shape, ax, 4), a_.dtype) for a_, ax in zip(arrays, axes)]
    return _hbm_call(body, arrays, outs, [pltpu.SemaphoreType.DMA((n, 3)), pltpu.SemaphoreType.DMA((n, 3)), pltpu.SemaphoreType.DMA((n,))], name)


def exchange_blocks_start(arrays, axes, *, name):
    n = len(arrays)
    lands = [lax.empty((4,) + _cut(a_.shape, ax, 4), a_.dtype) for a_, ax in zip(arrays, axes)]

    def body(*refs):
        ins, lnd = refs[:n], refs[n:2 * n]
        send_sems, recv_sems = refs[2 * n:6 * n], refs[6 * n:9 * n]
        token = refs[11 * n]
        mx, my, mc, others = _place()
        me = 2 * mx + my
        for i in range(n):
            sz = arrays[i].shape[axes[i]] // 4
            pltpu.make_async_copy(_win(ins[i], axes[i], me * sz, sz), lnd[i].at[me], send_sems[4 * i + 3]).start()
            for j, (px, py) in enumerate(others):
                pltpu.make_async_remote_copy(src_ref=_win(ins[i], axes[i], (2 * px + py) * sz, sz), dst_ref=lnd[i].at[me],
                                             send_sem=send_sems[4 * i + j], recv_sem=recv_sems[3 * i + j], device_id=(px, py, mc),
                                             device_id_type=MESH).start()
        token[...] = jnp.zeros_like(token)

    hbm = [pltpu.with_memory_space_constraint(a_, pltpu.HBM) for a_ in arrays + lands]
    out = pl.pallas_call(
        body, name=name,
        out_shape=tuple([pltpu.SemaphoreType.DMA(())] * (7 * n)) + tuple(pltpu.HBM(a_.shape, a_.dtype) for a_ in arrays + lands)
        + (jax.ShapeDtypeStruct((8, 128), F32),),
        in_specs=[HBM_SPEC] * (2 * n),
        out_specs=tuple([SEM_SPEC] * (7 * n)) + tuple([HBM_SPEC] * (2 * n)) + (pl.BlockSpec(memory_space=pltpu.VMEM),),
        input_output_aliases={i: 7 * n + i for i in range(2 * n)}, compiler_params=SPLIT_COPY)(*hbm)
    return list(out[:7 * n]), list(out[7 * n:8 * n]), list(out[8 * n:9 * n]), out[9 * n]


def exchange_blocks_wait(sems, arrays, lands, axes, after, *, name):
    n = len(arrays)

    def body(*refs):
        ins, lnd = refs[:n], refs[n:2 * n]
        send_sems, recv_sems = refs[2 * n:6 * n], refs[6 * n:9 * n]
        mx, my, mc, others = _place()
        me = 2 * mx + my
        for i in range(n):
            sz = arrays[i].shape[axes[i]] // 4
            mine = _win(ins[i], axes[i], me * sz, sz)
            pltpu.make_async_copy(mine, lnd[i].at[me], send_sems[4 * i + 3]).wait()
            for j, (px, py) in enumerate(others):
                cp = pltpu.make_async_remote_copy(src_ref=mine, dst_ref=lnd[i].at[2 * px + py], send_sem=send_sems[4 * i + j],
                                                  recv_sem=recv_sems[3 * i + j], device_id=(px, py, mc), device_id_type=MESH)
                cp.wait_send()
                cp.wait_recv()

    out = pl.pallas_call(
        body, name=name, out_shape=tuple(pltpu.HBM(a_.shape, a_.dtype) for a_ in arrays + lands),
        in_specs=[HBM_SPEC] * (2 * n) + [SEM_SPEC] * (7 * n) + [pl.BlockSpec(memory_space=pl.ANY)], out_specs=tuple([HBM_SPEC] * (2 * n)),
        input_output_aliases={i: i for i in range(2 * n)}, compiler_params=SPLIT_COPY)(*arrays, *lands, *sems, after)
    return list(out[n:])


def sum_blocks(e, hax, core, *, name):
    _, l, r, c = e.shape
    tr = _row_tile(r, c)
    per_half = (l, r // tr, 1)[hax]

    def omap(li, ri, cref):
        idx = [li, ri, 0]
        idx[hax] = idx[hax] + cref[0] * per_half
        return tuple(idx)

    def body(c_ref, e_ref, o_ref):
        v = e_ref[...].astype(F32)
        o_ref[...] = ((v[0] + v[1]) + v[2]) + v[3]

    full = (l, r, c)[:hax] + (2 * (l, r, c)[hax],) + (l, r, c)[hax + 1:]
    return pl.pallas_call(
        body, name=name, out_shape=jax.ShapeDtypeStruct(full, F32),
        grid_spec=pltpu.PrefetchScalarGridSpec(
            num_scalar_prefetch=1, grid=(l, r // tr),
            in_specs=[pl.BlockSpec((4, 1, tr, c), lambda li, ri, cref: (0, li, ri, 0))], out_specs=pl.BlockSpec((1, tr, c), omap)),
        compiler_params=_cp(("parallel", "parallel")))(core, e)


def pair_fill_halves(arrays, haxes, *, name):
    n = len(arrays)

    def body(*refs):
        ins, outs = refs[:n], refs[n:2 * n]
        send_sems, recv_sems = refs[2 * n:]
        mx, my, mc, _ = _place()
        cps = []
        for i in range(n):
            hs = arrays[i].shape[haxes[i]] // 2
            mine = _win(ins[i], haxes[i], mc * hs, hs)
            cp = pltpu.make_async_remote_copy(src_ref=mine, dst_ref=_win(outs[i], haxes[i], mc * hs, hs), send_sem=send_sems.at[i],
                                              recv_sem=recv_sems.at[i], device_id=(mx, my, 1 - mc), device_id_type=MESH)
            cp.start()
            cps.append(cp)
        for i in range(n):
            hs = arrays[i].shape[haxes[i]] // 2
            pltpu.make_async_remote_copy(src_ref=_win(ins[i], haxes[i], mc * hs, hs), dst_ref=_win(outs[i], haxes[i], (1 - mc) * hs, hs),
                                         send_sem=send_sems.at[i], recv_sem=recv_sems.at[i], device_id=(mx, my, 1 - mc),
                                         device_id_type=MESH).wait_recv()
        for cp in cps:
            cp.wait_send()

    return pl.pallas_call(
        body, name=name, out_shape=tuple(jax.ShapeDtypeStruct(a_.shape, a_.dtype) for a_ in arrays), in_specs=[HBM_SPEC] * n,
        out_specs=tuple([HBM_SPEC] * n), input_output_aliases={i: i for i in range(n)},
        scratch_shapes=[pltpu.SemaphoreType.DMA((n,)), pltpu.SemaphoreType.DMA((n,))])(*arrays)


WEIGHTS = ['c_ctx', 'w_mod', 'b_mod', 'norm1_w', 'norm2_w', 'final_norm_w', 's5_w_in', 's5_lam_re', 's5_lam_im', 's5_log_step', 's5_b_re', 's5_b_im', 's5_c_re', 's5_c_im', 's5_d', 's5_w_glu', 's5_w_out', 'hg_w_in', 'hg_lower_bounds', 'hg_gnorm_w', 'hg_w_out', 'ffn_w_up', 'ffn_conv_w', 'ffn_conv_b', 'ffn_w_down']
INPUTS = ['x', 'c', 'ctx', 'c_ctx', 'w_mod', 'b_mod', 'norm1_w', 'norm2_w', 'final_norm_w', 's5_w_in', 's5_lam_re', 's5_lam_im', 's5_log_step', 's5_b_re', 's5_b_im', 's5_c_re', 's5_c_im', 's5_d', 's5_w_glu', 's5_w_out', 'hg_w_in', 'hg_lower_bounds', 'hg_gnorm_w', 'hg_w_out', 'ffn_w_up', 'ffn_conv_w', 'ffn_conv_b', 'ffn_w_down', 'loss_target', 'm_c_ctx', 'm_w_mod', 'm_b_mod', 'm_norm1_w', 'm_norm2_w', 'm_final_norm_w', 'm_s5_w_in', 'm_s5_lam_re', 'm_s5_lam_im', 'm_s5_log_step', 'm_s5_b_re', 'm_s5_b_im', 'm_s5_c_re', 'm_s5_c_im', 'm_s5_d', 'm_s5_w_glu', 'm_s5_w_out', 'm_hg_w_in', 'm_hg_lower_bounds', 'm_hg_gnorm_w', 'm_hg_w_out', 'm_ffn_w_up', 'm_ffn_conv_w', 'm_ffn_conv_b', 'm_ffn_w_down', 'v_c_ctx', 'v_w_mod', 'v_b_mod', 'v_norm1_w', 'v_norm2_w', 'v_final_norm_w', 'v_s5_w_in', 'v_s5_lam_re', 'v_s5_lam_im', 'v_s5_log_step', 'v_s5_b_re', 'v_s5_b_im', 'v_s5_c_re', 'v_s5_c_im', 'v_s5_d', 'v_s5_w_glu', 'v_s5_w_out', 'v_hg_w_in', 'v_hg_lower_bounds', 'v_hg_gnorm_w', 'v_hg_w_out', 'v_ffn_w_up', 'v_ffn_conv_w', 'v_ffn_conv_b', 'v_ffn_w_down']
SHARD_AXIS = {"w_mod": 2, "s5_w_in": 1, "s5_w_glu": 1, "s5_w_out": 1, "hg_w_in": 2, "hg_lower_bounds": 2, "hg_w_out": 1,
              "ffn_w_up": 2, "ffn_conv_w": 2, "ffn_w_down": 1}
GATHER_F32 = ("hg_lower_bounds", "ffn_conv_w")
PACK_W = 1024
GRAD_WIRE = jnp.bfloat16


def _reduce_start(items, core, tag):
    names, arrays, axes = [n for n, _, _ in items], [g_ for _, g_, _ in items], [ax for _, _, ax in items]
    haxes = [_half_axis(g_.shape, ax) for g_, ax in zip(arrays, axes)]
    t = pair_swap_halves(arrays, haxes, name="grad_pair_swap_" + tag)
    h = [add_own_half(g_, t_, hx, core, out_dtype=GRAD_WIRE, name="grad_pair_add_" + n) for g_, t_, hx, n in zip(arrays, t, haxes, names)]
    sems, h, lands, token = exchange_blocks_start(h, axes, name="grad_exchange_start_" + tag)
    return (names, sems, h, lands, axes, haxes), token


def _reduce_finish(state, core, after, tag):
    names, sems, h, lands, axes, haxes = state
    e = exchange_blocks_wait(sems, h, lands, axes, after, name="grad_exchange_wait_" + tag)
    s = [sum_blocks(e_, hx, core, name="grad_chip_sum_" + n) for e_, hx, n in zip(e, haxes, names)]
    return dict(zip(names, pair_fill_halves(s, haxes, name="grad_pair_fill_" + tag)))


def _reduce_now(a, items, small, grads, core):
    flat = jnp.concatenate([grads[n].reshape(-1) for n in small])
    pad = (-flat.shape[0]) % (64 * PACK_W)
    small_pack = jnp.pad(flat, (0, pad)).reshape(1, -1, PACK_W)
    names = [n for n, _, _ in items] + ["small"]
    arrays = [g_ for _, g_, _ in items] + [small_pack]
    axes = [ax for _, _, ax in items] + [1]
    haxes = [_half_axis(g_.shape, ax) for g_, ax in zip(arrays, axes)]
    t = pair_swap_halves(arrays, haxes, name="grad_pair_swap")
    h = [add_own_half(g_, t_, hx, core, out_dtype=GRAD_WIRE, name="grad_pair_add_" + n) for g_, t_, hx, n in zip(arrays, t, haxes, names)]
    e = exchange_blocks(h, axes, name="grad_chip_exchange")
    s = [sum_blocks(e_, hx, core, name="grad_chip_sum_" + n) for e_, hx, n in zip(e, haxes, names)]
    red = pair_fill_halves(s, haxes, name="grad_pair_fill")
    out = dict(zip(names[:-1], red[:-1]))
    sm = chip_allgather(red[-1][0], name="allgather_small_grads").reshape(-1)
    off = 0
    for n in small:
        out[n] = sm[off:off + math.prod(a[n].shape)].reshape(a[n].shape)
        off += math.prod(a[n].shape)
    return out


def _blockdiag_b(bb, kb):
    gl = S5_KIN // S5_GROUP
    x = bb.reshape(kb, gl, S5_GROUP, S5_STATE)
    return (x[:, :, :, None, :] * jnp.eye(gl, dtype=bb.dtype)[None, :, None, :, None]).reshape(kb, S5_KIN, S5_KST)


def _blockdiag_c(cc, kb):
    gl = S5_KIN // S5_GROUP
    x = cc.reshape(kb, gl, S5_GROUP, S5_STATE).transpose(0, 1, 3, 2)
    return (x[:, :, :, None, :] * jnp.eye(gl, dtype=cc.dtype)[None, :, None, :, None]).reshape(kb, S5_KST, S5_KIN)


def _diag_b(m, kb):
    gl = S5_KIN // S5_GROUP
    x = m.reshape(kb, gl, S5_GROUP, gl, S5_STATE)
    return jnp.stack([x[:, i, :, i, :] for i in range(gl)], axis=1).reshape(kb * gl, S5_GROUP, S5_STATE)


def _diag_c(m, kb):
    gl = S5_KIN // S5_GROUP
    x = m.reshape(kb, gl, S5_STATE, gl, S5_GROUP)
    return jnp.stack([x[:, i, :, i, :] for i in range(gl)], axis=1).transpose(0, 1, 3, 2).reshape(kb * gl, S5_GROUP, S5_STATE)


def kernel(x, c, ctx, c_ctx, w_mod, b_mod, norm1_w, norm2_w, final_norm_w, s5_w_in, s5_lam_re, s5_lam_im, s5_log_step, s5_b_re, s5_b_im, s5_c_re, s5_c_im, s5_d, s5_w_glu, s5_w_out, hg_w_in, hg_lower_bounds, hg_gnorm_w, hg_w_out, ffn_w_up, ffn_conv_w, ffn_conv_b, ffn_w_down, loss_target, m_c_ctx, m_w_mod, m_b_mod, m_norm1_w, m_norm2_w, m_final_norm_w, m_s5_w_in, m_s5_lam_re, m_s5_lam_im, m_s5_log_step, m_s5_b_re, m_s5_b_im, m_s5_c_re, m_s5_c_im, m_s5_d, m_s5_w_glu, m_s5_w_out, m_hg_w_in, m_hg_lower_bounds, m_hg_gnorm_w, m_hg_w_out, m_ffn_w_up, m_ffn_conv_w, m_ffn_conv_b, m_ffn_w_down, v_c_ctx, v_w_mod, v_b_mod, v_norm1_w, v_norm2_w, v_final_norm_w, v_s5_w_in, v_s5_lam_re, v_s5_lam_im, v_s5_log_step, v_s5_b_re, v_s5_b_im, v_s5_c_re, v_s5_c_im, v_s5_d, v_s5_w_glu, v_s5_w_out, v_hg_w_in, v_hg_lower_bounds, v_hg_gnorm_w, v_hg_w_out, v_ffn_w_up, v_ffn_conv_w, v_ffn_conv_b, v_ffn_w_down):
    a = dict(zip(INPUTS, (x, c, ctx, c_ctx, w_mod, b_mod, norm1_w, norm2_w, final_norm_w, s5_w_in, s5_lam_re, s5_lam_im, s5_log_step, s5_b_re, s5_b_im, s5_c_re, s5_c_im, s5_d, s5_w_glu, s5_w_out, hg_w_in, hg_lower_bounds, hg_gnorm_w, hg_w_out, ffn_w_up, ffn_conv_w, ffn_conv_b, ffn_w_down, loss_target, m_c_ctx, m_w_mod, m_b_mod, m_norm1_w, m_norm2_w, m_final_norm_w, m_s5_w_in, m_s5_lam_re, m_s5_lam_im, m_s5_log_step, m_s5_b_re, m_s5_b_im, m_s5_c_re, m_s5_c_im, m_s5_d, m_s5_w_glu, m_s5_w_out, m_hg_w_in, m_hg_lower_bounds, m_hg_gnorm_w, m_hg_w_out, m_ffn_w_up, m_ffn_conv_w, m_ffn_conv_b, m_ffn_w_down, v_c_ctx, v_w_mod, v_b_mod, v_norm1_w, v_norm2_w, v_final_norm_w, v_s5_w_in, v_s5_lam_re, v_s5_lam_im, v_s5_log_step, v_s5_b_re, v_s5_b_im, v_s5_c_re, v_s5_c_im, v_s5_d, v_s5_w_glu, v_s5_w_out, v_hg_w_in, v_hg_lower_bounds, v_hg_gnorm_w, v_hg_w_out, v_ffn_w_up, v_ffn_conv_w, v_ffn_conv_b, v_ffn_w_down)))
    nb, seq, d = x.shape
    assert nb == NB
    rc = nb * ctx.shape[1]
    cfg = {"rc": rc}
    f = a["ffn_w_down"].shape[1] * 4
    core = lax.axis_index("c").astype(jnp.int32).reshape(1)

    w = {n: a[n] for n in WEIGHTS if n not in SHARD_AXIS}
    chip = (2 * lax.axis_index("x") + lax.axis_index("y")).astype(jnp.int32).reshape(1)
    groups = {
        "now": [("w_mod0", a["w_mod"][0:1]), ("s5_w_in", a["s5_w_in"]), ("hg_lower_bounds", a["hg_lower_bounds"]), ("ffn_conv_w", a["ffn_conv_w"])],
        "mid": [("s5_w_glu", a["s5_w_glu"]), ("s5_w_out", a["s5_w_out"]), ("ffn_w_up0", a["ffn_w_up"][0:1]), ("ffn_w_down0", a["ffn_w_down"][0:1])],
        "later": [("w_mod1", a["w_mod"][1:2]), ("hg_w_in", a["hg_w_in"]), ("hg_w_out", a["hg_w_out"]), ("ffn_w_up1", a["ffn_w_up"][1:2]),
                  ("ffn_w_down1", a["ffn_w_down"][1:2])]}
    shard_axis = lambda n: SHARD_AXIS[n.rstrip("01")]
    placed = {g: [place_shard(s_, shard_axis(n), chip, F32 if n in GATHER_F32 else MXU, name="place_" + n) for n, s_ in it] for g, it in groups.items()}
    axes = {g: [shard_axis(n) for n, _ in it] for g, it in groups.items()}
    haxes = {g: [_half_axis(p_.shape, ax) for p_, ax in zip(placed[g], axes[g])] for g in groups}
    got = pair_fill_halves(gather_placed(placed["now"], axes["now"], haxes["now"], name="allgather_weights"), haxes["now"],
                           name="allgather_pair_fill")
    w.update(dict(zip([n for n, _ in groups["now"]], got)))
    fly_mid = gather_placed_start(placed["mid"], axes["mid"], haxes["mid"], got[0], name="allgather_mid_start")
    fly_later = gather_placed_start(placed["later"], axes["later"], haxes["later"], fly_mid[3], name="allgather_later_start")

    def land(fly, g, after):
        send_, recv_, flying, _ = fly
        landed = gather_placed_wait(flying, send_, recv_, axes[g], haxes[g], after, name=f"allgather_{g}_wait")
        w.update(dict(zip([n for n, _ in groups[g]], pair_fill_halves(landed, haxes[g], name=f"allgather_{g}_pair_fill"))))

    tmaj = lambda t: t.transpose(1, 0, 2).reshape(-1, t.shape[-1])
    x0 = jnp.concatenate([tmaj(ctx), tmaj(x)], axis=0)
    tgt = tmaj(a["loss_target"])
    c16 = jnp.concatenate([jnp.broadcast_to(c_ctx[None], (8, d)), c, c], axis=0) + fly_later[3][0:1, 0:1]
    mt0, scb = mod_fwd(c16, w["w_mod0"][0], w["b_mod"][0][None], name="mod_fwd0")
    mt = [mt0, None]
    n1, n2 = w["norm1_w"], w["norm2_w"]
    w["w_mod"], w["ffn_w_up"], w["ffn_w_down"] = [w["w_mod0"][0], None], [None, None], [None, None]

    def ffn_fwd(l, h):
        u = mm(h, w["ffn_w_up"][l], name=f"ffn_up{l}")
        act = ffn_mid_fwd(cfg, u, w["ffn_conv_w"][l], w["ffn_conv_b"][l][None], name=f"ffn_mid{l}")
        return u, act, mm(act, w["ffn_w_down"][l], name=f"ffn_down{l}")

    def ffn_bwd(l, dfo, u, act, h, zero=0.0):
        dact = mm(dfo, w["ffn_w_down"][l], tb=True, name=f"ffn_down_dx{l}")
        dwd = mm(act, dfo, ta=True, name=f"ffn_down_dw{l}")
        du, dcw, dcb = ffn_mid_bwd(cfg, dact, u, w["ffn_conv_w"][l], w["ffn_conv_b"][l][None] + zero, name=f"ffn_mid_bwd{l}")
        dh = mm(du, w["ffn_w_up"][l], tb=True, name=f"ffn_up_dx{l}")
        dwu = mm(h, du, ta=True, name=f"ffn_up_dw{l}")
        return dh, dwu, dcw, dcb[0], dwd

    g_, p_ = d // S5_GROUP, S5_STATE
    ns, kb = g_ * p_, d // S5_KIN
    s5p = (w["s5_lam_re"][0].reshape(2 * g_, p_), w["s5_lam_im"][0].reshape(2 * g_, p_), w["s5_log_step"][0].reshape(2 * g_, 1),
           w["s5_b_re"][0].transpose(0, 1, 3, 2).reshape(2 * g_, S5_GROUP, p_), w["s5_b_im"][0].transpose(0, 1, 3, 2).reshape(2 * g_, S5_GROUP, p_))
    ar, ai, bbr, bbi = s5_disc_fwd(*s5p, name="s5_disc")
    dsk = w["s5_d"]
    _, h1 = node_fwd(cfg, x0, None, None, 0, n1[0:1], mt[0], 0, name="node0a")
    u0 = mm(h1, w["s5_w_in"][0], name="s5_in")
    s5s, ys = [], []
    for dd in range(2):
        sl = slice(dd * g_, (dd + 1) * g_)
        a_r, a_i = ar[sl].reshape(1, ns), ai[sl].reshape(1, ns)
        a2 = (a_r * a_r - a_i * a_i, 2.0 * a_r * a_i)
        b_r, b_i = _blockdiag_b(bbr[sl], kb), _blockdiag_b(bbi[sl], kb)
        c_r, c_i = _blockdiag_c(w["s5_c_re"][0, dd], kb), _blockdiag_c(w["s5_c_im"][0, dd], kb)
        ak, ai_k = a_r.reshape(kb, 1, S5_KST), a_i.reshape(kb, 1, S5_KST)
        ab = (ak * b_r - ai_k * b_i, ak * b_i + ai_k * b_r)
        akc, aic = ak.reshape(kb, S5_KST, 1), ai_k.reshape(kb, S5_KST, 1)
        c2 = (akc * c_r - aic * c_i, akc * c_i + aic * c_r)
        bf = lambda t_: t_.astype(MXU)
        sre, sim, ere, eim, y_ = s5_scan_fwd(cfg, u0, a2[0], a2[1], bf(b_r), bf(b_i), bf(ab[0]), bf(ab[1]), bf(c_r), bf(c_i), rev=dd == 1,
                                             name=f"s5_scan{dd}")
        s5s.append((sre, sim, ere, eim, a2[0], a2[1], bf(b_r), bf(b_i), bf(c_r), bf(c_i), bf(c2[0]), bf(c2[1])))
        ys.append(y_)

    def glu_a(u, y0, y1, ds):
        yp = (ds * u + y0) + y1
        return yp, _gelu(yp)

    ypre, zgb = rowmap(glu_a, [u0, ys[0], ys[1]], [dsk], [(d, F32), (d, MXU)], name="s5_glu_a")
    land(fly_mid, "mid", zgb)
    w["ffn_w_up"][0], w["ffn_w_down"][0] = w["ffn_w_up0"][0], w["ffn_w_down0"][0]
    tg = mm(zgb, w["s5_w_glu"][0], name="s5_glu")
    (z2,) = rowmap(lambda yp, t: _gelu(yp) * jax.nn.sigmoid(t), [ypre, tg], [], [(d, MXU)], name="s5_glu_b")
    y1a = mm(z2, w["s5_w_out"][0], name="s5_out")
    x1a, h2a = node_fwd(cfg, x0, y1a, mt[0], 2, n2[0:1], mt[0], 3, name="node0b")
    ufa, acta, foa = ffn_fwd(0, h2a)

    land(fly_later, "later", foa)
    w["w_mod"][1], w["ffn_w_up"][1], w["ffn_w_down"][1] = w["w_mod1"][0], w["ffn_w_up1"][0], w["ffn_w_down1"][0]
    mt[1], _ = mod_fwd(c16, w["w_mod"][1], w["b_mod"][1][None], name="mod_fwd1")
    x2a, h1b = node_fwd(cfg, x1a, foa, mt[0], 5, n1[1:2], mt[1], 0, name="node1a")
    z = mm(h1b, w["hg_w_in"][0], name="hg_in")
    e0, e1 = w["hg_lower_bounds"][:, 0, :], w["hg_lower_bounds"][:, 1, :]
    lb = hg_lb_fwd(e0, e1, name="hg_lb")
    gw = w["hg_gnorm_w"]
    o0, sin0 = hg_scan_fwd(cfg, z, lb[0:1], d_dir=0, name="hg_scan0")
    o1, sin1 = hg_scan_fwd(cfg, z, lb[1:2], d_dir=1, name="hg_scan1")
    onb = hg_read_fwd(o0, o1, z, gw, name="hg_read")
    y1b = mm(onb, w["hg_w_out"][0], name="hg_out")
    x1b, h2b = node_fwd(cfg, x2a, y1b, mt[1], 2, n2[1:2], mt[1], 3, name="node1b")
    ufb, actb, fob = ffn_fwd(1, h2b)
    loss_p, dx2b, dfob, dg2_1, dfnw = final_node(cfg, x1b, fob, mt[1], 5, w["final_norm_w"][None], tgt, name="final_node")

    gr = {}
    dh2b, dwu1, dcw1, dcb1, dwd1 = ffn_bwd(1, dfob, ufb, actb, h2b)
    dx1b, dy1b, dn2_1, dsh2_1, dsc2_1, dg1_1 = node_bwd(cfg, dx2b, dh2b, x1b, y1b, mt[1], 2, n2[1:2], mt[1], 3, name="node1b_bwd")
    don = mm(dy1b, w["hg_w_out"][0], tb=True, name="hg_out_dx")
    gr["hg_w_out"] = mm(onb, dy1b, ta=True, name="hg_out_dw")[None]
    do_, dgate_, dgw = hg_read_bwd(don, o0, o1, z, gw, name="hg_read_bwd")
    dq, dv, dxf, dlb0 = hg_scan_bwd(cfg, do_, z, lb[0:1], sin0, None, None, d_dir=0, name="hg_scan_bwd0")
    dq, dv, dxb, dlb1 = hg_scan_bwd(cfg, do_, z, lb[1:2], sin1, dq, dv, d_dir=1, name="hg_scan_bwd1")
    dz = jnp.concatenate([t_.astype(MXU) for t_ in (dq, dv, dxf, dxb, dgate_)], axis=1)
    dh1b = mm(dz, w["hg_w_in"][0], tb=True, name="hg_in_dx")
    gr["hg_w_in"] = mm(h1b, dz, ta=True, name="hg_in_dw")[None]
    de0, de1 = hg_lb_bwd(e0, e1, jnp.concatenate([dlb0, dlb1], axis=0), name="hg_lb_bwd")
    gr["hg_lower_bounds"] = jnp.stack([de0, de1], axis=1)
    gr["hg_gnorm_w"] = dgw
    dx2a, dfoa, dn1_1, dsh1_1, dsc1_1, dg2_0 = node_bwd(cfg, dx1b, dh1b, x2a, foa, mt[0], 5, n1[1:2], mt[1], 0, name="node1a_bwd")
    dmt1 = jnp.concatenate([dsh1_1, dsc1_1, dg1_1, dsh2_1, dsc2_1, dg2_1], axis=1)
    red1, tok1 = _reduce_start([("hg_w_in", gr["hg_w_in"], 2), ("hg_w_out", gr["hg_w_out"], 1), ("ffn_w_up1", dwu1[None], 2),
                                ("ffn_w_down1", dwd1[None], 1), ("w_mod1", mm(scb, dmt1, ta=True, name="mod_dw1")[None], 2)], core, "layer1")

    dh2a, dwu0, dcw0, dcb0, dwd0 = ffn_bwd(0, dfoa, ufa, acta, h2a, zero=tok1[0:1, 0:1])
    red2, tok2 = _reduce_start([("ffn_w_up0", dwu0[None], 2), ("ffn_w_down0", dwd0[None], 1)], core, "ffn0")
    dx1a, dy1a, dn2_0, dsh2_0, dsc2_0, dg1_0 = node_bwd(cfg, dx2a, dh2a, x1a, y1a, mt[0], 2, n2[0:1] + tok2[0:1, 0:1], mt[0], 3,
                                                        name="node0b_bwd")
    dz2 = mm(dy1a, w["s5_w_out"][0], tb=True, name="s5_out_dx")
    gr["s5_w_out"] = mm(z2, dy1a, ta=True, name="s5_out_dw")[None]

    def glu_b_bwd(dz2_, yp, t):
        zg, sg = _gelu(yp), jax.nn.sigmoid(t)
        return dz2_ * zg * sg * (1.0 - sg), dz2_ * sg

    dtg, dzg_dir = rowmap(glu_b_bwd, [dz2, ypre, tg], [], [(d, MXU), (d, F32)], name="s5_glu_b_bwd")
    dzg_mm = mm(dtg, w["s5_w_glu"][0], tb=True, name="s5_glu_dx")
    gr["s5_w_glu"] = mm(zgb, dtg, ta=True, name="s5_glu_dw")[None]

    def glu_a_bwd(dzd, dzm, yp, u, ds):
        _, vjp = jax.vjp(_gelu, yp)
        (dy,) = vjp(dzd + dzm)
        return dy, dy * ds, jnp.sum(dy * u, axis=0, keepdims=True)

    dyb, du, ddsk = rowmap(glu_a_bwd, [dzg_dir, dzg_mm, ypre, u0], [dsk], [(d, MXU), (d, F32)], [(1, d)], name="s5_glu_a_bwd")
    gr["s5_d"] = ddsk
    dar, dai, dbr, dbi, dcr, dci = [], [], [], [], [], []
    for dd in range(2):
        sre, sim, ere, eim = s5s[dd][:4]
        du, gre, gim, da_r, da_i = s5_scan_bwd(cfg, dyb, *s5s[dd], du, rev=dd == 1, name=f"s5_scan_bwd{dd}")
        dar.append(colsum(da_r, name=f"s5_da_re{dd}").reshape(g_, p_))
        dai.append(colsum(da_i, name=f"s5_da_im{dd}").reshape(g_, p_))
        dbr.append(_diag_b(blockdiag_tn(u0, gre, S5_KIN, S5_KST, name=f"s5_db_re{dd}"), kb))
        dbi.append(_diag_b(blockdiag_tn(u0, gim, S5_KIN, S5_KST, name=f"s5_db_im{dd}"), kb))
        dcr.append(_diag_c(blockdiag_tn(sre.reshape(-1, ns), dyb, S5_KST, S5_KIN, name=f"s5_dc_re{dd}"), kb))
        dci.append(_diag_c(blockdiag_tn(sim.reshape(-1, ns), dyb, S5_KST, S5_KIN, scale=-1.0, name=f"s5_dc_im{dd}"), kb))
    cat = lambda l_: jnp.concatenate(l_, axis=0)
    dlr, dli, dls, dbre, dbim = s5_disc_bwd(*s5p, cat(dar), cat(dai), cat(dbr), cat(dbi), name="s5_disc_bwd")
    gr["s5_lam_re"], gr["s5_lam_im"] = dlr.reshape(1, 2, g_, p_), dli.reshape(1, 2, g_, p_)
    gr["s5_log_step"] = dls.reshape(1, 2, g_)
    gr["s5_b_re"] = dbre.reshape(1, 2, g_, S5_GROUP, p_).transpose(0, 1, 2, 4, 3)
    gr["s5_b_im"] = dbim.reshape(1, 2, g_, S5_GROUP, p_).transpose(0, 1, 2, 4, 3)
    gr["s5_c_re"], gr["s5_c_im"] = jnp.stack(dcr)[None], jnp.stack(dci)[None]
    dh1 = mm(du, w["s5_w_in"][0], tb=True, name="s5_in_dx")
    gr["s5_w_in"] = mm(h1, du, ta=True, name="s5_in_dw")[None]
    dx0, _, dn1_0, dsh1_0, dsc1_0, _ = node_bwd(cfg, dx1a, dh1, x0, None, None, 0, n1[0:1], mt[0], 0, name="node0a_bwd")

    dmt = [jnp.concatenate([dsh1_0, dsc1_0, dg1_0, dsh2_0, dsc2_0, dg2_0], axis=1), dmt1]
    gr["b_mod"] = jnp.concatenate([colsum(dmt[l], name=f"mod_db{l}") for l in range(2)], axis=0)
    dsc16 = [mm(dmt[l], w["w_mod"][l], tb=True, name=f"mod_dx{l}") for l in range(2)]
    gr["c_ctx"] = cctx_grad(c16, dsc16, name="c_ctx_grad")[0]
    gr["norm1_w"] = jnp.concatenate([dn1_0, dn1_1], axis=0)
    gr["norm2_w"] = jnp.concatenate([dn2_0, dn2_1], axis=0)
    gr["final_norm_w"] = dfnw[0]
    gr["ffn_conv_w"], gr["ffn_conv_b"] = jnp.stack([dcw0, dcw1]), jnp.stack([dcb0, dcb1])

    last = [(n, gr[n], SHARD_AXIS[n]) for n in ("s5_w_in", "s5_w_glu", "s5_w_out", "hg_lower_bounds", "ffn_conv_w")]
    last.append(("w_mod0", mm(scb, dmt[0], ta=True, name="mod_dw0")[None], 2))
    red = _reduce_now(a, last, [n for n in WEIGHTS if n not in SHARD_AXIS], gr, core)
    red.update(_reduce_finish(red1, core, dx0, "layer1"))
    red.update(_reduce_finish(red2, core, dx0, "ffn0"))
    red["w_mod"] = jnp.concatenate([red["w_mod0"], red["w_mod1"]], axis=0)
    red["ffn_w_up"] = jnp.concatenate([red["ffn_w_up0"], red["ffn_w_up1"]], axis=0)
    red["ffn_w_down"] = jnp.concatenate([red["ffn_w_down0"], red["ffn_w_down1"]], axis=0)
    loss = lax.psum(loss_p[0, 0], ("x", "y", "c"))
    grad_x = dx0[rc:].reshape(seq, nb, d).transpose(1, 0, 2)
    upd = {n: adamw(a[n], red[n], a["m_" + n], a["v_" + n], name="adamw_" + n) for n in WEIGHTS}
    return (loss, grad_x, *[red[n] for n in WEIGHTS], *[upd[n][0] for n in WEIGHTS], *[upd[n][1] for n in WEIGHTS],
            *[upd[n][2] for n in WEIGHTS])
```

```python
import functools
import math

import jax
import jax.numpy as jnp
from jax import lax
from jax.experimental import pallas as pl
from jax.experimental.pallas import tpu as pltpu

F32 = jnp.float32
BF = jnp.bfloat16
MXU = jnp.bfloat16

NORM_EPS = 1e-6
GRID_W = 64
N_MOD = 6
S5_GROUP = 16
S5_STATE = 64
S5_LAM_RE_MAX = -1e-4
S5_KIN = 256
S5_KST = S5_KIN // S5_GROUP * S5_STATE
HEAD = 128
CHUNK_ROWS = 128
N_PROJ = 5
NB = 4
ADAM_LR, ADAM_B1, ADAM_B2, ADAM_EPS, ADAM_WD, ADAM_STEP = 0.001, 0.9, 0.999, 1e-08, 0.01, 10
VMEM_LIMIT = 56 * 1024 * 1024
MESH = pl.DeviceIdType.MESH


def _tile(n, cap):
    if n <= cap:
        return n
    best = None
    for t in range(128, cap + 1, 128):
        if n % t == 0:
            best = t
    assert best is not None, (n, cap)
    return best


def _row_tile(r, width=1024):
    cap = max(8, (512 * 1024) // max(width, 1))
    return next((t for t in (512, 256, 128, 64, 32, 16, 8) if t <= cap and r % t == 0), r)


def _cp(sem):
    return pltpu.CompilerParams(dimension_semantics=sem, vmem_limit_bytes=VMEM_LIMIT)


def _dot(a, b, ca=1, cb=0):
    return lax.dot_general(a.astype(MXU), b.astype(MXU), (((ca,), (cb,)), ((), ())), preferred_element_type=F32)


def _dot3(m, x):
    hi = x.astype(MXU)
    r1 = x - hi.astype(F32)
    mid = r1.astype(MXU)
    lo = (r1 - mid.astype(F32)).astype(MXU)
    return _dot(m, hi) + _dot(m, mid) + _dot(m, lo)


def mm(a, b, *, ta=False, tb=False, out_dtype=F32, name):
    (kd, m) = a.shape if ta else a.shape[::-1]
    (n, kd2) = b.shape if tb else b.shape[::-1]
    assert kd == kd2, (a.shape, b.shape, ta, tb)
    tm, tn, tk = _tile(m, 1024), _tile(n, 1536), _tile(kd, 1024)
    nk = kd // tk

    def body(a_ref, b_ref, o_ref, acc_ref):
        k = pl.program_id(2)

        @pl.when(k == 0)
        def _():
            acc_ref[...] = jnp.zeros_like(acc_ref)

        acc_ref[...] += _dot(a_ref[...], b_ref[...], 0 if ta else 1, 1 if tb else 0)

        @pl.when(k == nk - 1)
        def _():
            o_ref[...] = acc_ref[...].astype(out_dtype)

    a_spec = pl.BlockSpec((tk, tm), lambda i, j, k: (k, i)) if ta else pl.BlockSpec((tm, tk), lambda i, j, k: (i, k))
    b_spec = pl.BlockSpec((tn, tk), lambda i, j, k: (j, k)) if tb else pl.BlockSpec((tk, tn), lambda i, j, k: (k, j))
    return pl.pallas_call(
        body, name=name, grid=(m // tm, n // tn, nk), in_specs=[a_spec, b_spec],
        out_specs=pl.BlockSpec((tm, tn), lambda i, j, k: (i, j)), out_shape=jax.ShapeDtypeStruct((m, n), out_dtype),
        scratch_shapes=[pltpu.VMEM((tm, tn), F32)], compiler_params=_cp(("parallel", "parallel", "arbitrary")))(a, b)


def blockdiag_tn(a, b, wa, wb, *, scale=1.0, name):
    rows = a.shape[0]
    kb = a.shape[1] // wa
    tr = _tile(rows, 1024)
    nr = rows // tr

    def body(a_ref, b_ref, o_ref):
        i = pl.program_id(1)

        @pl.when(i == 0)
        def _():
            o_ref[...] = jnp.zeros_like(o_ref)

        o_ref[0] += scale * _dot(a_ref[...], b_ref[...], 0, 0)

    return pl.pallas_call(
        body, name=name, grid=(kb, nr),
        in_specs=[pl.BlockSpec((tr, wa), lambda k, i: (i, k)), pl.BlockSpec((tr, wb), lambda k, i: (i, k))],
        out_specs=pl.BlockSpec((1, wa, wb), lambda k, i: (k, 0, 0)), out_shape=jax.ShapeDtypeStruct((kb, wa, wb), F32),
        compiler_params=_cp(("parallel", "arbitrary")))(a, b)


def _pat(v, p, op):
    tm, d = v.shape
    return op(v.reshape(tm // 8, 8, d), p[None]).reshape(tm, d)


def _norm_mod(x, nw, shift, scale):
    y = x * lax.rsqrt(jnp.mean(x * x, axis=-1, keepdims=True) + NORM_EPS) * nw
    return _pat(_pat(y, 1.0 + scale, jnp.multiply), shift, jnp.add)


def _mt_spec(d, nct):
    return pl.BlockSpec((8, N_MOD * d), lambda i: (jnp.where(i < nct, 0, 1), 0))


def _acc_spec(d, nct):
    return pl.BlockSpec((8, d), lambda i: (jnp.where(i < nct, 0, 1), 0))


def _rows(cfg):
    tm = min(512, cfg["rc"])
    return tm, cfg["rc"] // tm


def node_fwd(cfg, xp, y, mtg, gi, nw, mtn, si, *, name):
    r, d = xp.shape
    tm, nct = _rows(cfg)
    row = pl.BlockSpec((tm, d), lambda i: (i, 0))
    vec = pl.BlockSpec((1, d), lambda i: (0, 0))

    def body(*refs):
        if y is None:
            xp_ref, nw_ref, mtn_ref, h_ref = refs
            x = xp_ref[...]
        else:
            xp_ref, y_ref, mtg_ref, nw_ref, mtn_ref, xn_ref, h_ref = refs
            x = xp_ref[...] + _pat(y_ref[...], mtg_ref[:, gi * d:(gi + 1) * d], jnp.multiply)
            xn_ref[...] = x
        h_ref[...] = _norm_mod(x, nw_ref[...], mtn_ref[:, si * d:(si + 1) * d], mtn_ref[:, (si + 1) * d:(si + 2) * d]).astype(MXU)

    h_shape = jax.ShapeDtypeStruct((r, d), MXU)
    if y is None:
        h = pl.pallas_call(body, name=name, grid=(r // tm,), in_specs=[row, vec, _mt_spec(d, nct)], out_specs=row,
                           out_shape=h_shape, compiler_params=_cp(("parallel",)))(xp, nw, mtn)
        return xp, h
    return pl.pallas_call(body, name=name, grid=(r // tm,), in_specs=[row, row, _mt_spec(d, nct), vec, _mt_spec(d, nct)],
                          out_specs=(row, row), out_shape=(jax.ShapeDtypeStruct((r, d), F32), h_shape),
                          compiler_params=_cp(("parallel",)))(xp, y, mtg, nw, mtn)


def node_bwd(cfg, dxres, dh, xn, y, mtg, gi, nw, mtn, si, *, name):
    r, d = xn.shape
    tm, nct = _rows(cfg)
    row = pl.BlockSpec((tm, d), lambda i: (i, 0))
    vec = pl.BlockSpec((1, d), lambda i: (0, 0))
    has_y = y is not None

    def body(*refs):
        if has_y:
            dxres_ref, dh_ref, xn_ref, y_ref, mtg_ref, nw_ref, mtn_ref, dxn_ref, dy_ref, dnw_ref, dsh_ref, dsc_ref, dg_ref = refs
        else:
            dxres_ref, dh_ref, xn_ref, nw_ref, mtn_ref, dxn_ref, dnw_ref, dsh_ref, dsc_ref = refs
        i = pl.program_id(0)
        _, vjp = jax.vjp(_norm_mod, xn_ref[...], nw_ref[...], mtn_ref[:, si * d:(si + 1) * d], mtn_ref[:, (si + 1) * d:(si + 2) * d])
        dx, dnw, dsh, dsc = vjp(dh_ref[...])
        dx = dx + dxres_ref[...]
        dxn_ref[...] = dx

        @pl.when(i == 0)
        def _():
            dnw_ref[...] = jnp.zeros_like(dnw_ref)

        @pl.when((i == 0) | (i == nct))
        def _():
            dsh_ref[...] = jnp.zeros_like(dsh_ref)
            dsc_ref[...] = jnp.zeros_like(dsc_ref)
            if has_y:
                dg_ref[...] = jnp.zeros_like(dg_ref)

        dnw_ref[...] += dnw
        dsh_ref[...] += dsh
        dsc_ref[...] += dsc
        if has_y:
            dy_ref[...] = _pat(dx, mtg_ref[:, gi * d:(gi + 1) * d], jnp.multiply).astype(MXU)
            dg_ref[...] += jnp.sum((dx * y_ref[...]).reshape(tm // 8, 8, d), axis=0)

    acc = jax.ShapeDtypeStruct((16, d), F32)
    xs = jax.ShapeDtypeStruct((r, d), F32)
    if has_y:
        return pl.pallas_call(
            body, name=name, grid=(r // tm,), in_specs=[row, row, row, row, _mt_spec(d, nct), vec, _mt_spec(d, nct)],
            out_specs=(row, row, vec, _acc_spec(d, nct), _acc_spec(d, nct), _acc_spec(d, nct)),
            out_shape=(xs, jax.ShapeDtypeStruct((r, d), MXU), jax.ShapeDtypeStruct((1, d), F32), acc, acc, acc),
            compiler_params=_cp(("arbitrary",)))(dxres, dh, xn, y, mtg, nw, mtn)
    dxn, dnw, dsh, dsc = pl.pallas_call(
        body, name=name, grid=(r // tm,), in_specs=[row, row, row, vec, _mt_spec(d, nct)],
        out_specs=(row, vec, _acc_spec(d, nct), _acc_spec(d, nct)),
        out_shape=(xs, jax.ShapeDtypeStruct((1, d), F32), acc, acc), compiler_params=_cp(("arbitrary",)))(dxres, dh, xn, nw, mtn)
    return dxn, None, dnw, dsh, dsc, None


def final_node(cfg, xp, y, mtg, gi, fnw, tgt, *, name):
    r, d = xp.shape
    tm, nct = _rows(cfg)
    row = pl.BlockSpec((tm, d), lambda i: (i, 0))
    vec = pl.BlockSpec((1, d), lambda i: (0, 0))

    def norm(x, w):
        return x * lax.rsqrt(jnp.mean(x * x, axis=-1, keepdims=True) + NORM_EPS) * w

    def body(xp_ref, y_ref, mtg_ref, fnw_ref, tgt_ref, loss_ref, dx_ref, dy_ref, dg_ref, dfnw_ref):
        i = pl.program_id(0)
        g = mtg_ref[:, gi * d:(gi + 1) * d]
        x = xp_ref[...] + _pat(y_ref[...], g, jnp.multiply)
        out, vjp = jax.vjp(norm, x, fnw_ref[...])
        lat = i >= nct
        err = jnp.where(lat, out - tgt_ref[...], 0.0)
        dx, dfnw = vjp(err * (1.0 / d))

        @pl.when(i == 0)
        def _():
            loss_ref[...] = jnp.zeros_like(loss_ref)
            dfnw_ref[...] = jnp.zeros_like(dfnw_ref)

        @pl.when((i == 0) | (i == nct))
        def _():
            dg_ref[...] = jnp.zeros_like(dg_ref)

        loss_ref[...] += jnp.full(loss_ref.shape, 0.5 / d * jnp.sum(err * err), F32)
        dfnw_ref[...] += dfnw
        dx_ref[...] = dx
        dy_ref[...] = _pat(dx, g, jnp.multiply).astype(MXU)
        dg_ref[...] += jnp.sum((dx * y_ref[...]).reshape(tm // 8, 8, d), axis=0)

    return pl.pallas_call(
        body, name=name, grid=(r // tm,),
        in_specs=[row, row, _mt_spec(d, nct), vec, pl.BlockSpec((tm, d), lambda i: (jnp.maximum(i - nct, 0), 0))],
        out_specs=(pl.BlockSpec((8, 128), lambda i: (0, 0)), row, row, _acc_spec(d, nct), vec),
        out_shape=(jax.ShapeDtypeStruct((8, 128), F32), jax.ShapeDtypeStruct((r, d), F32), jax.ShapeDtypeStruct((r, d), MXU),
                   jax.ShapeDtypeStruct((16, d), F32), jax.ShapeDtypeStruct((1, d), F32)),
        compiler_params=_cp(("arbitrary",)))(xp, y, mtg, fnw, tgt)


def _silu(x):
    return x * jax.nn.sigmoid(x)


def mod_fwd(c16, w, b, *, name):
    d, n = w.shape
    tn = _tile(n, 1536)

    def body(c_ref, w_ref, b_ref, o_ref, s_ref):
        s = _silu(c_ref[...])
        s_ref[...] = s.astype(MXU)
        o_ref[...] = _dot(s, w_ref[...]) + b_ref[...]

    return pl.pallas_call(
        body, name=name, grid=(n // tn,),
        in_specs=[pl.BlockSpec((16, d), lambda j: (0, 0)), pl.BlockSpec((d, tn), lambda j: (0, j)), pl.BlockSpec((1, tn), lambda j: (0, j))],
        out_specs=(pl.BlockSpec((16, tn), lambda j: (0, j)), pl.BlockSpec((16, d), lambda j: (0, 0))),
        out_shape=(jax.ShapeDtypeStruct((16, n), F32), jax.ShapeDtypeStruct((16, d), MXU)),
        compiler_params=_cp(("arbitrary",)))(c16, w, b)


def colsum(x, *, name):
    def body(x_ref, o_ref):
        o_ref[...] = jnp.sum(x_ref[...], axis=0, keepdims=True)

    return pl.pallas_call(body, name=name, out_shape=jax.ShapeDtypeStruct((1, x.shape[1]), F32))(x)


def cctx_grad(c16, ds_list, *, name):
    def body(c_ref, *refs):
        o_ref = refs[-1]
        ds = refs[0][...]
        for r_ in refs[1:-1]:
            ds = ds + r_[...]
        _, vjp = jax.vjp(_silu, c_ref[...])
        (dc,) = vjp(ds)
        o_ref[...] = jnp.sum(dc[0:8], axis=0, keepdims=True)

    return pl.pallas_call(body, name=name, out_shape=jax.ShapeDtypeStruct((1, c16.shape[1]), F32))(c16, *ds_list)


def _s5_disc(lam_re, lam_im, log_step, b_re, b_im):
    lr = jnp.minimum(lam_re, S5_LAM_RE_MAX)
    li = lam_im
    dt = jnp.exp(log_step)
    mag = jnp.exp(lr * dt)
    abar_r = mag * jnp.cos(li * dt)
    abar_i = mag * jnp.sin(li * dt)
    den = lr * lr + li * li
    nr = abar_r - 1.0
    coef_r = (nr * lr + abar_i * li) / den
    coef_i = (abar_i * lr - nr * li) / den
    bbar_r = coef_r[:, None, :] * b_re - coef_i[:, None, :] * b_im
    bbar_i = coef_r[:, None, :] * b_im + coef_i[:, None, :] * b_re
    return abar_r, abar_i, bbar_r, bbar_i


def s5_disc_fwd(lam_re, lam_im, log_step, b_re, b_im, *, name):
    def body(lr, li, ls, br, bi, ar_o, ai_o, br_o, bi_o):
        ar_o[...], ai_o[...], br_o[...], bi_o[...] = _s5_disc(lr[...], li[...], ls[...], br[...], bi[...])

    s2, s3 = jax.ShapeDtypeStruct(lam_re.shape, F32), jax.ShapeDtypeStruct(b_re.shape, F32)
    return pl.pallas_call(body, name=name, out_shape=(s2, s2, s3, s3))(lam_re, lam_im, log_step, b_re, b_im)


def s5_disc_bwd(lam_re, lam_im, log_step, b_re, b_im, d_ar, d_ai, d_br, d_bi, *, name):
    def body(lr, li, ls, br, bi, dar, dai, dbr, dbi, o_lr, o_li, o_ls, o_br, o_bi):
        _, vjp = jax.vjp(_s5_disc, lr[...], li[...], ls[...], br[...], bi[...])
        o_lr[...], o_li[...], o_ls[...], o_br[...], o_bi[...] = vjp((dar[...], dai[...], dbr[...], dbi[...]))

    s2, s3 = jax.ShapeDtypeStruct(lam_re.shape, F32), jax.ShapeDtypeStruct(b_re.shape, F32)
    return pl.pallas_call(body, name=name, out_shape=(s2, s2, jax.ShapeDtypeStruct(log_step.shape, F32), s3, s3))(
        lam_re, lam_im, log_step, b_re, b_im, d_ar, d_ai, d_br, d_bi)


S5_LANES = 512


def _chunk_order(k, ncc, nch, rev):
    if not rev:
        return k
    return jnp.where(k < ncc, ncc - 1 - k, nch - 1 - (k - ncc))


def _cmul(ar, ai, xr, xi):
    return ar * xr - ai * xi, ar * xi + ai * xr


S5_FWD_ROWS = 256
S5_BWD_ROWS = 256


def _const_spec(a):
    return pl.BlockSpec(a.shape, lambda k: (0,) * a.ndim, pipeline_mode=pl.Buffered(1))


def _shift_steps(x, edge_tile, back):
    n = x.shape[0]
    row = lax.broadcasted_iota(jnp.int32, (8, x.shape[1]), 0)
    edge = pltpu.roll(edge_tile, 4, 0)
    if back:
        y = pltpu.roll(x, 4, 0)
        return jnp.concatenate([jnp.where(row < 4, edge, y[0:8]), y[8:]], axis=0)
    y = pltpu.roll(x, n - 4, 0)
    return jnp.concatenate([y[:n - 8], jnp.where(row >= 4, edge, y[n - 8:])], axis=0)


def s5_scan_fwd(cfg, u, a2_re, a2_im, bre, bim, abre, abim, cre, cim, *, rev, name):
    r, d = u.shape
    ns = a2_re.shape[1]
    kb = d // S5_KIN
    tcr = S5_FWD_ROWS
    n8 = tcr // 8
    q = S5_FWD_ROWS // S5_BWD_ROWS
    seg = n8 // q
    nch, ncc = r // tcr, cfg["rc"] // tcr
    lw = min(S5_LANES, ns)

    def body(u_ref, ar_ref, ai_ref, bre_ref, bim_ref, abre_ref, abim_ref, cre_ref, cim_ref, sre_ref, sim_ref, ere_ref, eim_ref, y_ref,
             st_re, st_im, u_edge):
        @pl.when(pl.program_id(0) == 0)
        def _():
            st_re[...] = jnp.zeros_like(st_re)
            st_im[...] = jnp.zeros_like(st_im)
            u_edge[...] = jnp.zeros_like(u_edge)

        u_ = u_ref[...]
        ub = u_.astype(MXU)
        upb = _shift_steps(u_, u_edge[...], back=not rev).astype(MXU)
        u_edge[...] = u_[0:8] if rev else u_[tcr - 8:tcr]
        for j in range(kb):
            uj, upj = ub[:, j * S5_KIN:(j + 1) * S5_KIN], upb[:, j * S5_KIN:(j + 1) * S5_KIN]
            sre_ref[:, :, j * S5_KST:(j + 1) * S5_KST] = (_dot(uj, bre_ref[j]) + _dot(upj, abre_ref[j])).reshape(n8, 8, S5_KST)
            sim_ref[:, :, j * S5_KST:(j + 1) * S5_KST] = (_dot(uj, bim_ref[j]) + _dot(upj, abim_ref[j])).reshape(n8, 8, S5_KST)
        for c in range(ns // lw):
            sl = slice(c * lw, (c + 1) * lw)
            ar = jnp.broadcast_to(ar_ref[:, sl], (8, lw))
            ai = jnp.broadcast_to(ai_ref[:, sl], (8, lw))

            def step(i, carry, sl=sl, ar=ar, ai=ai):
                sr, si = carry
                ii = n8 - 1 - i if rev else i
                pr, pi = _cmul(ar, ai, sr, si)
                sr, si = pr + sre_ref[ii, :, sl], pi + sim_ref[ii, :, sl]
                sre_ref[ii, :, sl] = sr
                sim_ref[ii, :, sl] = si
                return sr, si

            sr, si = st_re[:, sl], st_im[:, sl]
            for s_ in range(q):
                at = q - 1 - s_ if rev else s_
                ere_ref[at, :, sl] = sr
                eim_ref[at, :, sl] = si
                sr, si = lax.fori_loop(s_ * seg, (s_ + 1) * seg, step, (sr, si))
            st_re[:, sl] = sr
            st_im[:, sl] = si
        for j in range(kb):
            sr = sre_ref[:, :, j * S5_KST:(j + 1) * S5_KST].reshape(tcr, S5_KST)
            si = sim_ref[:, :, j * S5_KST:(j + 1) * S5_KST].reshape(tcr, S5_KST)
            y_ref[:, j * S5_KIN:(j + 1) * S5_KIN] = _dot(sr, cre_ref[j]) - _dot(si, cim_ref[j])

    cidx = functools.partial(_chunk_order, ncc=ncc, nch=nch, rev=rev)
    full = _const_spec
    st = pl.BlockSpec((n8, 8, ns), lambda k: (cidx(k), 0, 0))
    en = pl.BlockSpec((q, 8, ns), lambda k: (cidx(k), 0, 0))
    return pl.pallas_call(
        body, name=name, grid=(nch,),
        in_specs=[pl.BlockSpec((tcr, d), lambda k: (cidx(k), 0)), full(a2_re), full(a2_im), full(bre), full(bim), full(abre), full(abim),
                  full(cre), full(cim)],
        out_specs=(st, st, en, en, pl.BlockSpec((tcr, d), lambda k: (cidx(k), 0))),
        out_shape=(jax.ShapeDtypeStruct((r // 8, 8, ns), F32),) * 2 + (jax.ShapeDtypeStruct((q * nch, 8, ns), F32),) * 2
        + (jax.ShapeDtypeStruct((r, d), F32),),
        scratch_shapes=[pltpu.VMEM((8, ns), F32), pltpu.VMEM((8, ns), F32), pltpu.VMEM((8, d), F32)],
        compiler_params=_cp(("arbitrary",)))(u, a2_re, a2_im, bre, bim, abre, abim, cre, cim)


def s5_scan_bwd(cfg, dyb, sre, sim, ere, eim, a2_re, a2_im, bre, bim, cre, cim, c2re, c2im, du_in, *, rev, name):
    r, d = dyb.shape
    ns = a2_re.shape[1]
    kb = d // S5_KIN
    tcr = S5_BWD_ROWS
    n8 = tcr // 8
    nch, ncc = r // tcr, cfg["rc"] // tcr
    lw = min(S5_LANES, ns)

    def body(dy_ref, sre_ref, sim_ref, ere_ref, eim_ref, ar_ref, ai_ref, bre_ref, bim_ref, cre_ref, cim_ref, c2re_ref, c2im_ref, duin_ref,
             du_ref, gre_ref, gim_ref, dar_ref, dai_ref, g_re, g_im, gc_re, gc_im, dy_edge):
        k = pl.program_id(0)

        @pl.when(k == 0)
        def _():
            gc_re[...] = jnp.zeros_like(gc_re)
            gc_im[...] = jnp.zeros_like(gc_im)
            dar_ref[...] = jnp.zeros_like(dar_ref)
            dai_ref[...] = jnp.zeros_like(dai_ref)
            dy_edge[...] = jnp.zeros_like(dy_edge)

        dy32 = dy_ref[...].astype(F32)
        dy = dy32.astype(MXU)
        dyn = _shift_steps(dy32, dy_edge[...], back=rev).astype(MXU)
        dy_edge[...] = dy32[tcr - 8:tcr] if rev else dy32[0:8]
        for j in range(kb):
            dyj, dynj = dy[:, j * S5_KIN:(j + 1) * S5_KIN], dyn[:, j * S5_KIN:(j + 1) * S5_KIN]
            g_re[:, :, j * S5_KST:(j + 1) * S5_KST] = (_dot(dyj, cre_ref[j], 1, 1) + _dot(dynj, c2re_ref[j], 1, 1)).reshape(n8, 8, S5_KST)
            g_im[:, :, j * S5_KST:(j + 1) * S5_KST] = -(_dot(dyj, cim_ref[j], 1, 1) + _dot(dynj, c2im_ref[j], 1, 1)).reshape(n8, 8, S5_KST)
        first = lax.broadcasted_iota(jnp.int32, (8, lw), 0) < 4
        if rev:
            first = jnp.logical_not(first)
        for c in range(ns // lw):
            sl = slice(c * lw, (c + 1) * lw)
            ar = jnp.broadcast_to(ar_ref[:, sl], (8, lw))
            nai = -jnp.broadcast_to(ai_ref[:, sl], (8, lw))

            def step(i, carry, sl=sl, ar=ar, nai=nai):
                gr, gi, accr, acci = carry
                ii = i if rev else n8 - 1 - i
                pr, pi = _cmul(ar, nai, gr, gi)
                outr, outi = pr + g_re[ii, :, sl], pi + g_im[ii, :, sl]
                g_re[ii, :, sl] = outr
                g_im[ii, :, sl] = outi
                pv = jnp.clip(ii + 1 if rev else ii - 1, 0, n8 - 1)
                at_entry = (ii == n8 - 1) if rev else (ii == 0)
                pvr = jnp.where(at_entry, ere_ref[0, :, sl], sre_ref[pv, :, sl])
                pvi = jnp.where(at_entry, eim_ref[0, :, sl], sim_ref[pv, :, sl])
                spr = pltpu.roll(jnp.where(first, sre_ref[ii, :, sl], pvr), 4, 0)
                spi = pltpu.roll(jnp.where(first, sim_ref[ii, :, sl], pvi), 4, 0)
                accr = accr + outr * spr + outi * spi
                acci = acci + outi * spr - outr * spi
                return outr, outi, accr, acci

            gr, gi, accr, acci = lax.fori_loop(0, n8, step, (gc_re[:, sl], gc_im[:, sl], dar_ref[:, sl], dai_ref[:, sl]))
            gc_re[:, sl] = gr
            gc_im[:, sl] = gi
            dar_ref[:, sl] = accr
            dai_ref[:, sl] = acci
        for j in range(kb):
            gr = g_re[:, :, j * S5_KST:(j + 1) * S5_KST].reshape(tcr, S5_KST)
            gi = g_im[:, :, j * S5_KST:(j + 1) * S5_KST].reshape(tcr, S5_KST)
            gre_ref[:, j * S5_KST:(j + 1) * S5_KST] = gr.astype(MXU)
            gim_ref[:, j * S5_KST:(j + 1) * S5_KST] = gi.astype(MXU)
            du_ref[:, j * S5_KIN:(j + 1) * S5_KIN] = (duin_ref[:, j * S5_KIN:(j + 1) * S5_KIN]
                                                     + _dot(gr, bre_ref[j], 1, 1) + _dot(gi, bim_ref[j], 1, 1))

    def cidx(k):
        return _chunk_order(nch - 1 - k, ncc, nch, rev)

    full = _const_spec
    st = pl.BlockSpec((n8, 8, ns), lambda k: (cidx(k), 0, 0))
    en = pl.BlockSpec((1, 8, ns), lambda k: (cidx(k), 0, 0))
    rowd = pl.BlockSpec((tcr, d), lambda k: (cidx(k), 0))
    rown = pl.BlockSpec((tcr, ns), lambda k: (cidx(k), 0))
    acc = pl.BlockSpec((8, ns), lambda k: (0, 0))
    return pl.pallas_call(
        body, name=name, grid=(nch,),
        in_specs=[rowd, st, st, en, en, full(a2_re), full(a2_im), full(bre), full(bim), full(cre), full(cim), full(c2re), full(c2im), rowd],
        out_specs=(rowd, rown, rown, acc, acc),
        out_shape=(jax.ShapeDtypeStruct((r, d), F32), jax.ShapeDtypeStruct((r, ns), MXU), jax.ShapeDtypeStruct((r, ns), MXU),
                   jax.ShapeDtypeStruct((8, ns), F32), jax.ShapeDtypeStruct((8, ns), F32)),
        scratch_shapes=[pltpu.VMEM((n8, 8, ns), F32), pltpu.VMEM((n8, 8, ns), F32), pltpu.VMEM((8, ns), F32), pltpu.VMEM((8, ns), F32),
                        pltpu.VMEM((8, d), F32)],
        compiler_params=_cp(("arbitrary",)))(dyb, sre, sim, ere, eim, a2_re, a2_im, bre, bim, cre, cim, c2re, c2im, du_in)


def rowmap(fn, rows_in, vecs_in, outs, accs=(), *, name):
    r = rows_in[0].shape[0]
    tm = _row_tile(r, max(a.shape[1] for a in rows_in))
    nr, nv, no = len(rows_in), len(vecs_in), len(outs)

    def body(*refs):
        ins = [x[...] for x in refs[:nr + nv]]
        res = fn(*ins)
        if not isinstance(res, (tuple, list)):
            res = (res,)
        out_refs = refs[nr + nv:]
        for o_ref, v in zip(out_refs[:no], res[:no]):
            o_ref[...] = v.astype(o_ref.dtype)
        if accs:
            @pl.when(pl.program_id(0) == 0)
            def _():
                for a_ref in out_refs[no:]:
                    a_ref[...] = jnp.zeros_like(a_ref)
            for a_ref, v in zip(out_refs[no:], res[no:]):
                a_ref[...] += v

    in_specs = [pl.BlockSpec((tm, a.shape[1]), lambda i: (i, 0)) for a in rows_in]
    in_specs += [pl.BlockSpec(v.shape, lambda i, n=v.ndim: (0,) * n) for v in vecs_in]
    out_specs = [pl.BlockSpec((tm, w), lambda i: (i, 0)) for w, _ in outs] + [pl.BlockSpec(s, lambda i, n=len(s): (0,) * n) for s in accs]
    out_shape = [jax.ShapeDtypeStruct((r, w), dt) for w, dt in outs] + [jax.ShapeDtypeStruct(s, F32) for s in accs]
    res = pl.pallas_call(body, name=name, grid=(r // tm,), in_specs=in_specs, out_specs=tuple(out_specs), out_shape=tuple(out_shape),
                         compiler_params=_cp(("arbitrary",) if accs else ("parallel",)))(*rows_in, *vecs_in)
    return res


def _gelu(x):
    return jax.nn.gelu(x, approximate=True)


def _hg_lower_bound(e0, e1):
    m = jnp.maximum(e0, e1)
    a, b = jnp.exp(e0 - m), jnp.exp(e1 - m)
    return b / (a + b)


def _hg_gates(x, lb):
    logf = jnp.log(lb + (1.0 - lb) * jax.nn.sigmoid(x))
    return logf, (1.0 - lb) * jax.nn.sigmoid(-x)


def _hg_masks(rev):
    n = CHUNK_ROWS
    rr = lax.broadcasted_iota(jnp.int32, (n, n), 0)
    ss = lax.broadcasted_iota(jnp.int32, (n, n), 1)
    same = (rr % NB) == (ss % NB)
    causal = same & ((ss >= rr) if rev else (ss <= rr))
    anti = same & ((ss <= rr) if rev else (ss >= rr))
    end0 = 0 if rev else n - NB
    pick_end = ss == (end0 + rr % NB)
    return same, causal, anti, pick_end, end0


def _hg_expand(x):
    ex = lax.broadcasted_iota(jnp.int32, x.shape, 0) % NB
    return jnp.concatenate([jnp.where(ex == b, x, 0.0) for b in range(NB)], axis=1)


def _hg_fold(xe):
    kk = xe.shape[1] // NB
    ex = lax.broadcasted_iota(jnp.int32, (xe.shape[0], kk), 0) % NB
    out = jnp.zeros((xe.shape[0], kk), F32)
    for b in range(NB):
        out = out + jnp.where(ex == b, xe[:, b * kk:(b + 1) * kk], 0.0)
    return out


def _hg_chunk(q, v, x, lb, masks):
    same, causal, anti, pick_end, end0 = masks
    logf, kk = _hg_gates(x, lb)
    b = _dot3(causal.astype(MXU), logf)
    bend_t = _dot3(pick_end.astype(MXU), b)
    bend_flat = jnp.concatenate([b[end0 + i:end0 + i + 1] for i in range(NB)], axis=1)
    eb = jnp.exp(b)
    enb = jnp.exp(-b)
    ee = jnp.exp(bend_t - b)
    qd, kd, ke = q * eb, kk * enb, kk * ee
    att = jnp.where(causal, _dot(qd, kd, 1, 1), 0.0)
    decay = jnp.exp(bend_flat)
    return dict(same=same, causal=causal, anti=anti, logf=logf, kk=kk, b=b, eb=eb, enb=enb, ee=ee, qd=qd, kd=kd, ke=ke, att=att,
                decay=decay, qde=_hg_expand(qd), kee=_hg_expand(ke))


def _hg_chunk_order(cfg, r):
    nch, ncc = r // CHUNK_ROWS, cfg["rc"] // CHUNK_ROWS
    return nch, ncc


def hg_scan_fwd(cfg, z, lb, *, d_dir, name):
    r = z.shape[0]
    d = z.shape[1] // N_PROJ
    nh = d // HEAD
    rev = d_dir == 1
    nch, ncc = _hg_chunk_order(cfg, r)
    n = CHUNK_ROWS

    def body(q_ref, v_ref, x_ref, lb_ref, o_ref, sin_ref, stk):
        @pl.when(pl.program_id(0) == 0)
        def _():
            stk[...] = jnp.zeros_like(stk)

        masks = _hg_masks(rev)
        for h in range(nh):
            sl = slice(h * HEAD, (h + 1) * HEAD)
            s0 = stk[h]
            sin_ref[0, h] = s0
            v = v_ref[:, sl]
            c = _hg_chunk(q_ref[:, sl], v, x_ref[:, sl], lb_ref[:, sl], masks)
            o_ref[:, sl] = _dot(c["att"], v) + _dot(c["qde"], s0, 1, 1)
            stk[h] = s0 * c["decay"] + _dot(v, c["kee"], 0, 0)

    def cidx(k):
        return _chunk_order(k, ncc, nch, rev)

    blk = lambda p: pl.BlockSpec((n, d), lambda k: (cidx(k), p))
    return pl.pallas_call(
        body, name=name, grid=(nch,),
        in_specs=[blk(0), blk(1), blk(2 + d_dir), pl.BlockSpec((1, d), lambda k: (0, 0))],
        out_specs=(blk(0), pl.BlockSpec((1, nh, HEAD, NB * HEAD), lambda k: (cidx(k), 0, 0, 0))),
        out_shape=(jax.ShapeDtypeStruct((r, d), F32), jax.ShapeDtypeStruct((nch, nh, HEAD, NB * HEAD), F32)),
        scratch_shapes=[pltpu.VMEM((nh, HEAD, NB * HEAD), F32)], compiler_params=_cp(("arbitrary",)))(z, z, z, lb)


def hg_scan_bwd(cfg, do, z, lb, sin, dq_in, dv_in, *, d_dir, name):
    r = z.shape[0]
    d = z.shape[1] // N_PROJ
    nh = d // HEAD
    rev = d_dir == 1
    nch, ncc = _hg_chunk_order(cfg, r)
    n = CHUNK_ROWS
    has_in = dq_in is not None

    def body(*refs):
        if has_in:
            do_ref, q_ref, v_ref, x_ref, lb_ref, sin_ref, dqi_ref, dvi_ref, dq_ref, dv_ref, dx_ref, dlb_ref, dstk = refs
        else:
            do_ref, q_ref, v_ref, x_ref, lb_ref, sin_ref, dq_ref, dv_ref, dx_ref, dlb_ref, dstk = refs
        @pl.when(pl.program_id(0) == 0)
        def _():
            dstk[...] = jnp.zeros_like(dstk)
            dlb_ref[...] = jnp.zeros_like(dlb_ref)

        masks = _hg_masks(rev)
        ex = lax.broadcasted_iota(jnp.int32, (n, HEAD), 0) % NB
        for h in range(nh):
            sl = slice(h * HEAD, (h + 1) * HEAD)
            do_, q, v, x, lb_, s0, ds1 = do_ref[:, sl], q_ref[:, sl], v_ref[:, sl], x_ref[:, sl], lb_ref[:, sl], sin_ref[0, h], dstk[h]
            c = _hg_chunk(q, v, x, lb_, masks)
            datt = jnp.where(c["causal"], _dot(do_, v, 1, 1), 0.0)
            dv = _dot(c["att"], do_, 0, 0) + _dot(c["kee"], ds1, 1, 1)
            dqd = _dot(datt, c["kd"]) + _hg_fold(_dot(do_, s0))
            dkd = _dot(datt, c["qd"], 0, 0)
            dke = _hg_fold(_dot(v, ds1))
            dbend_flat = jnp.sum(ds1 * s0, axis=0, keepdims=True) * c["decay"]
            dstk[h] = _dot(do_, c["qde"], 0, 0) + ds1 * c["decay"]
            dq = dqd * c["eb"]
            dk = dkd * c["enb"] + dke * c["ee"]
            db = dqd * c["qd"] - dkd * c["kd"] - dke * c["ke"]
            dbend_rows = jnp.zeros((n, HEAD), F32)
            for b in range(NB):
                dbend_rows = dbend_rows + jnp.where(ex == b, dbend_flat[:, b * HEAD:(b + 1) * HEAD], 0.0)
            dlogf = _dot3(c["anti"].astype(MXU), db) + _dot3(c["same"].astype(MXU), dke * c["ke"]) + dbend_rows
            _, vjp = jax.vjp(_hg_gates, x, lb_)
            dx, dlb = vjp((dlogf, dk))
            if has_in:
                dq = dq + dqi_ref[:, sl]
                dv = dv + dvi_ref[:, sl]
            dq_ref[:, sl] = dq
            dv_ref[:, sl] = dv
            dx_ref[:, sl] = dx
            dlb_ref[:, sl] += dlb

    def cidx(k):
        return _chunk_order(nch - 1 - k, ncc, nch, rev)

    blk = lambda p: pl.BlockSpec((n, d), lambda k: (cidx(k), p))
    vec = pl.BlockSpec((1, d), lambda k: (0, 0))
    in_specs = [blk(0), blk(0), blk(1), blk(2 + d_dir), vec, pl.BlockSpec((1, nh, HEAD, NB * HEAD), lambda k: (cidx(k), 0, 0, 0))]
    args = [do, z, z, z, lb, sin]
    if has_in:
        in_specs += [blk(0), blk(0)]
        args += [dq_in, dv_in]
    rd = jax.ShapeDtypeStruct((r, d), F32)
    return pl.pallas_call(
        body, name=name, grid=(nch,), in_specs=in_specs, out_specs=(blk(0), blk(0), blk(0), vec),
        out_shape=(rd, rd, rd, jax.ShapeDtypeStruct((1, d), F32)),
        scratch_shapes=[pltpu.VMEM((nh, HEAD, NB * HEAD), F32)], compiler_params=_cp(("arbitrary",)))(*args)


def _hg_read(o, g, gw):
    on = o * lax.rsqrt(jnp.mean(o * o, axis=-1, keepdims=True) + NORM_EPS) * gw
    return on * jax.nn.sigmoid(g)


def hg_read_fwd(of, ob, z, gw, *, name):
    r, d = of.shape
    nh = d // HEAD
    tm = _row_tile(r)

    def body(of_ref, ob_ref, g_ref, gw_ref, o_ref):
        for h in range(nh):
            sl = slice(h * HEAD, (h + 1) * HEAD)
            o_ref[:, sl] = _hg_read(of_ref[:, sl] + ob_ref[:, sl], g_ref[:, sl], gw_ref[...]).astype(MXU)

    blk = pl.BlockSpec((tm, d), lambda i: (i, 0))
    return pl.pallas_call(
        body, name=name, grid=(r // tm,),
        in_specs=[blk, blk, pl.BlockSpec((tm, d), lambda i: (i, N_PROJ - 1)), pl.BlockSpec((1, HEAD), lambda i: (0, 0))],
        out_specs=blk, out_shape=jax.ShapeDtypeStruct((r, d), MXU), compiler_params=_cp(("parallel",)))(of, ob, z, gw)


def hg_read_bwd(don, of, ob, z, gw, *, name):
    r, d = of.shape
    nh = d // HEAD
    tm = _row_tile(r)

    def body(don_ref, of_ref, ob_ref, g_ref, gw_ref, do_ref, dg_ref, dgw_ref):
        @pl.when(pl.program_id(0) == 0)
        def _():
            dgw_ref[...] = jnp.zeros_like(dgw_ref)

        for h in range(nh):
            sl = slice(h * HEAD, (h + 1) * HEAD)
            _, vjp = jax.vjp(_hg_read, of_ref[:, sl] + ob_ref[:, sl], g_ref[:, sl], gw_ref[...])
            do_ref[:, sl], dg_ref[:, sl], dgw = vjp(don_ref[:, sl])
            dgw_ref[...] += dgw

    blk = pl.BlockSpec((tm, d), lambda i: (i, 0))
    vec = pl.BlockSpec((1, HEAD), lambda i: (0, 0))
    rd = jax.ShapeDtypeStruct((r, d), F32)
    return pl.pallas_call(
        body, name=name, grid=(r // tm,),
        in_specs=[blk, blk, blk, pl.BlockSpec((tm, d), lambda i: (i, N_PROJ - 1)), vec],
        out_specs=(blk, blk, vec), out_shape=(rd, rd, jax.ShapeDtypeStruct((1, HEAD), F32)),
        compiler_params=_cp(("arbitrary",)))(don, of, ob, z, gw)


FFN_COLS = 256


def _seg_masks(cfg, tr, i):
    t = lax.broadcasted_iota(jnp.int32, (tr, FFN_COLS), 0) // NB
    ctx_steps = cfg["rc"] // NB
    pos = jnp.where(i == 0, t % ctx_steps, t % GRID_W)
    last = jnp.where(i == 0, ctx_steps - 1, GRID_W - 1)
    return pos == 0, pos == last


def _prev(x, start):
    return jnp.where(start, 0.0, pltpu.roll(x, NB, 0))


def _next(x, end):
    return jnp.where(end, 0.0, pltpu.roll(x, x.shape[0] - NB, 0))


def _conv3(u, w, b, start, end):
    return ((b + _prev(u, start) * w[0:1]) + u * w[1:2]) + _next(u, end) * w[2:3]


def ffn_mid_fwd(cfg, u, cw, cb, *, name):
    r, f2 = u.shape
    f = f2 // 2
    tr = cfg["rc"]
    nf = f // FFN_COLS

    def body(ua_ref, ug_ref, wa_ref, wg_ref, ba_ref, bg_ref, o_ref, ca_ref, cg_ref):
        start, end = _seg_masks(cfg, tr, pl.program_id(0))
        a = _conv3(ua_ref[...], wa_ref[...], ba_ref[...], start, end)
        g = _conv3(ug_ref[...], wg_ref[...], bg_ref[...], start, end)
        ca_ref[...] = a
        cg_ref[...] = g
        o_ref[...] = (_silu(a) * g).astype(MXU)

    ca = lambda rows: pl.BlockSpec((rows, FFN_COLS), lambda i, j: (i if rows == tr else 0, j))
    cg = lambda rows: pl.BlockSpec((rows, FFN_COLS), lambda i, j: (i if rows == tr else 0, j + nf))
    half = jax.ShapeDtypeStruct((r, f), F32)
    return pl.pallas_call(
        body, name=name, grid=(r // tr, nf), in_specs=[ca(tr), cg(tr), ca(3), cg(3), ca(1), cg(1)], out_specs=(ca(tr), ca(tr), ca(tr)),
        out_shape=(jax.ShapeDtypeStruct((r, f), MXU), half, half), compiler_params=_cp(("parallel", "parallel")))(u, u, cw, cw, cb, cb)


def ffn_mid_bwd(cfg, dact, u, ca, cg, cw, *, name):
    r, f2 = u.shape
    f = f2 // 2
    tr = cfg["rc"]
    nf = f // FFN_COLS

    def body(da_ref, us_ref, ca_ref, cg_ref, ws_ref, du_ref, dcw_ref, dcb_ref):
        i = pl.program_id(1)
        is_a = pl.program_id(0) < nf
        start, end = _seg_masks(cfg, tr, i)

        @pl.when(i == 0)
        def _():
            dcw_ref[...] = jnp.zeros_like(dcw_ref)
            dcb_ref[...] = jnp.zeros_like(dcb_ref)

        def finish(dc):
            us, ws = us_ref[...], ws_ref[...]
            du_ref[...] = (ws[1:2] * dc + ws[0:1] * _next(dc, end) + ws[2:3] * _prev(dc, start)).astype(MXU)
            dcw_ref[...] += jnp.concatenate([jnp.sum(dc * _prev(us, start), axis=0, keepdims=True), jnp.sum(dc * us, axis=0, keepdims=True),
                                             jnp.sum(dc * _next(us, end), axis=0, keepdims=True)], axis=0)
            dcb_ref[...] += jnp.sum(dc, axis=0, keepdims=True)

        @pl.when(is_a)
        def _():
            cs = ca_ref[...]
            sg = jax.nn.sigmoid(cs)
            finish(da_ref[...] * cg_ref[...] * (sg * (1.0 + cs * (1.0 - sg))))

        @pl.when(jnp.logical_not(is_a))
        def _():
            finish(da_ref[...] * _silu(ca_ref[...]))

    cs_ = lambda rows: pl.BlockSpec((rows, FFN_COLS), lambda j, i: (i if rows == tr else 0, j))
    hf = pl.BlockSpec((tr, FFN_COLS), lambda j, i: (i, j % nf))
    gate = pl.BlockSpec((tr, FFN_COLS), lambda j, i: (jnp.where(j < nf, i, 0), jnp.where(j < nf, j, 0)))
    return pl.pallas_call(
        body, name=name, grid=(2 * nf, r // tr), in_specs=[hf, cs_(tr), hf, gate, cs_(3)], out_specs=(cs_(tr), cs_(3), cs_(1)),
        out_shape=(jax.ShapeDtypeStruct((r, f2), MXU), jax.ShapeDtypeStruct((3, f2), F32), jax.ShapeDtypeStruct((1, f2), F32)),
        compiler_params=_cp(("parallel", "arbitrary")))(dact, u, ca, cg, cw)


def hg_lb_fwd(e0, e1, *, name):
    def body(a, b, o):
        o[...] = _hg_lower_bound(a[...], b[...])

    return pl.pallas_call(body, name=name, out_shape=jax.ShapeDtypeStruct(e0.shape, F32))(e0, e1)


def hg_lb_bwd(e0, e1, dlb, *, name):
    def body(a, b, g, oa, ob):
        _, vjp = jax.vjp(_hg_lower_bound, a[...], b[...])
        oa[...], ob[...] = vjp(g[...])

    s = jax.ShapeDtypeStruct(e0.shape, F32)
    return pl.pallas_call(body, name=name, out_shape=(s, s))(e0, e1, dlb)


def _adamw(w, g, m, v):
    m = ADAM_B1 * m + (1.0 - ADAM_B1) * g
    v = ADAM_B2 * v + (1.0 - ADAM_B2) * jnp.square(g)
    m_hat = m / (1.0 - ADAM_B1 ** ADAM_STEP)
    v_hat = v / (1.0 - ADAM_B2 ** ADAM_STEP)
    delta = -ADAM_LR * (m_hat / (jnp.sqrt(v_hat) + ADAM_EPS) + ADAM_WD * w)
    return delta, m, v


def _as2d(a):
    if a.ndim >= 2 and a.shape[-1] % 128 == 0:
        return a.reshape(-1, a.shape[-1])
    return a.reshape(-1, 128) if a.size % 128 == 0 else a.reshape(1, -1)


def adamw(w, g, m, v, *, name):
    w2 = _as2d(w)
    outs = rowmap(_adamw, [w2, _as2d(g), _as2d(m), _as2d(v)], [], [(w2.shape[1], F32)] * 3, name=name)
    return tuple(o.reshape(w.shape) for o in outs)


HBM_SPEC = pl.BlockSpec(memory_space=pltpu.HBM)


def _place():
    mx, my, mc = lax.axis_index("x"), lax.axis_index("y"), lax.axis_index("c")
    others = [(1 - mx, my), (mx, 1 - my), (1 - mx, 1 - my)]
    return mx, my, mc, others


def chip_allgather(x, *, name):
    def body(x_ref, o_ref, send_sems, recv_sems, local_sem):
        mx, my, mc, others = _place()
        me = 2 * mx + my
        mine = pltpu.make_async_copy(x_ref, o_ref.at[me], local_sem)
        mine.start()
        sends = [pltpu.make_async_remote_copy(src_ref=x_ref, dst_ref=o_ref.at[me], send_sem=send_sems.at[j], recv_sem=recv_sems.at[j],
                                              device_id=(px, py, mc), device_id_type=MESH) for j, (px, py) in enumerate(others)]
        for cp in sends:
            cp.start()
        for j, (px, py) in enumerate(others):
            pltpu.make_async_remote_copy(src_ref=x_ref, dst_ref=o_ref.at[2 * px + py], send_sem=send_sems.at[j], recv_sem=recv_sems.at[j],
                                         device_id=(px, py, mc), device_id_type=MESH).wait_recv()
        for cp in sends:
            cp.wait_send()
        mine.wait()

    return pl.pallas_call(
        body, name=name, out_shape=jax.ShapeDtypeStruct((4,) + x.shape, x.dtype), in_specs=[HBM_SPEC], out_specs=HBM_SPEC,
        scratch_shapes=[pltpu.SemaphoreType.DMA((3,)), pltpu.SemaphoreType.DMA((3,)), pltpu.SemaphoreType.DMA])(x)


def _win(ref, axis, start, size):
    idx = [slice(None)] * len(ref.shape)
    idx[axis] = pl.ds(start, size)
    return ref.at[tuple(idx)]


def _half_axis(shape, ax):
    if shape[0] == 2:
        return 0
    return 2 if ax == 1 else 1


def _cut(shape, axis, parts):
    return shape[:axis] + (shape[axis] // parts,) + shape[axis + 1:]


def _hbm_call(body, arrays, out_shapes, sems, name):
    n_in = len(arrays)
    return pl.pallas_call(body, name=name, out_shape=tuple(out_shapes), in_specs=[HBM_SPEC] * n_in, out_specs=tuple([HBM_SPEC] * len(out_shapes)),
                          scratch_shapes=sems)(*arrays)


def place_shard(shard, ax, chip, dtype, *, name):
    l, r, c = shard.shape
    tr = _row_tile(r, c)
    per_block = (l, r // tr, 1)[ax]

    def omap(li, ri, cref):
        idx = [li, ri, 0]
        idx[ax] = idx[ax] + cref[0] * per_block
        return tuple(idx)

    def body(c_ref, s_ref, o_ref):
        o_ref[...] = s_ref[...].astype(dtype)

    full = shard.shape[:ax] + (4 * shard.shape[ax],) + shard.shape[ax + 1:]
    return pl.pallas_call(
        body, name=name, out_shape=jax.ShapeDtypeStruct(full, dtype),
        grid_spec=pltpu.PrefetchScalarGridSpec(
            num_scalar_prefetch=1, grid=(l, r // tr),
            in_specs=[pl.BlockSpec((1, tr, c), lambda li, ri, cref: (li, ri, 0))], out_specs=pl.BlockSpec((1, tr, c), omap)),
        compiler_params=_cp(("parallel", "parallel")))(chip, shard)


def gather_placed(arrays, axes, haxes, *, name):
    n = len(arrays)

    def body(*refs):
        ins, outs = refs[:n], refs[n:2 * n]
        send_sems, recv_sems = refs[2 * n:]
        mx, my, mc, others = _place()
        me = 2 * mx + my

        def part(ref, i, chip):
            sz, hs = arrays[i].shape[axes[i]] // 4, arrays[i].shape[haxes[i]] // 2
            return _win(_win(ref, axes[i], chip * sz, sz), haxes[i], mc * hs, hs)

        sends = []
        for i in range(n):
            for j, (px, py) in enumerate(others):
                rc = pltpu.make_async_remote_copy(src_ref=part(ins[i], i, me), dst_ref=part(outs[i], i, me), send_sem=send_sems.at[i, j],
                                                  recv_sem=recv_sems.at[i, j], device_id=(px, py, mc), device_id_type=MESH)
                rc.start()
                sends.append(rc)
        for i in range(n):
            for j, (px, py) in enumerate(others):
                pltpu.make_async_remote_copy(src_ref=part(ins[i], i, me), dst_ref=part(outs[i], i, 2 * px + py), send_sem=send_sems.at[i, j],
                                             recv_sem=recv_sems.at[i, j], device_id=(px, py, mc), device_id_type=MESH).wait_recv()
        for rc in sends:
            rc.wait_send()

    return pl.pallas_call(
        body, name=name, out_shape=tuple(jax.ShapeDtypeStruct(a_.shape, a_.dtype) for a_ in arrays), in_specs=[HBM_SPEC] * n,
        out_specs=tuple([HBM_SPEC] * n), input_output_aliases={i: i for i in range(n)},
        scratch_shapes=[pltpu.SemaphoreType.DMA((n, 3)), pltpu.SemaphoreType.DMA((n, 3))])(*arrays)


SEM_SPEC = pl.BlockSpec(memory_space=pltpu.SEMAPHORE)
SPLIT_COPY = pltpu.CompilerParams(has_side_effects=pltpu.SideEffectType.DATAFLOW_SIDE_EFFECTING)


def _gather_part(ref, shape, ax, hax, chip, core):
    sz, hs = shape[ax] // 4, shape[hax] // 2
    return _win(_win(ref, ax, chip * sz, sz), hax, core * hs, hs)


def gather_placed_start(arrays, axes, haxes, after, *, name):
    n = len(arrays)

    m = 3 * n

    def body(*refs):
        ins, send_sems, recv_sems = refs[:n], refs[n + 1:n + 1 + m], refs[n + 1 + m:n + 1 + 2 * m]
        token = refs[2 * n + 1 + 2 * m]
        mx, my, mc, others = _place()
        me = 2 * mx + my
        for i in range(n):
            for j, (px, py) in enumerate(others):
                part = _gather_part(ins[i], arrays[i].shape, axes[i], haxes[i], me, mc)
                pltpu.make_async_remote_copy(src_ref=part, dst_ref=part, send_sem=send_sems[3 * i + j], recv_sem=recv_sems[3 * i + j],
                                             device_id=(px, py, mc), device_id_type=MESH).start()
        token[...] = jnp.zeros_like(token)

    hbm = [pltpu.with_memory_space_constraint(a_, pltpu.HBM) for a_ in arrays]
    out = pl.pallas_call(
        body, name=name,
        out_shape=tuple([pltpu.SemaphoreType.DMA(())] * (2 * m)) + tuple(pltpu.HBM(a_.shape, a_.dtype) for a_ in arrays)
        + (jax.ShapeDtypeStruct((8, 128), F32),),
        in_specs=[HBM_SPEC] * n + [pl.BlockSpec(memory_space=pl.ANY)],
        out_specs=tuple([SEM_SPEC] * (2 * m)) + tuple([HBM_SPEC] * n) + (pl.BlockSpec(memory_space=pltpu.VMEM),),
        input_output_aliases={i: 2 * m + i for i in range(n)}, compiler_params=SPLIT_COPY)(*hbm, after)
    return list(out[:m]), list(out[m:2 * m]), list(out[2 * m:2 * m + n]), out[2 * m + n]


def gather_placed_wait(arrays, send_sems, recv_sems, axes, haxes, after, *, name):
    n = len(arrays)

    m = 3 * n

    def body(*refs):
        ins, send_refs, recv_refs = refs[:n], refs[n:n + m], refs[n + m:n + 2 * m]
        mx, my, mc, others = _place()
        me = 2 * mx + my
        for i in range(n):
            for j, (px, py) in enumerate(others):
                cp = pltpu.make_async_remote_copy(
                    src_ref=_gather_part(ins[i], arrays[i].shape, axes[i], haxes[i], me, mc),
                    dst_ref=_gather_part(ins[i], arrays[i].shape, axes[i], haxes[i], 2 * px + py, mc),
                    send_sem=send_refs[3 * i + j], recv_sem=recv_refs[3 * i + j], device_id=(px, py, mc), device_id_type=MESH)
                cp.wait_send()
                cp.wait_recv()

    out = pl.pallas_call(
        body, name=name, out_shape=tuple(pltpu.HBM(a_.shape, a_.dtype) for a_ in arrays),
        in_specs=[HBM_SPEC] * n + [SEM_SPEC] * (2 * m) + [pl.BlockSpec(memory_space=pl.ANY)], out_specs=tuple([HBM_SPEC] * n),
        input_output_aliases={i: i for i in range(n)}, compiler_params=SPLIT_COPY)(*arrays, *send_sems, *recv_sems, after)
    return list(out)


def pair_swap_halves(arrays, haxes, *, name):
    n = len(arrays)

    def body(*refs):
        ins, outs = refs[:n], refs[n:2 * n]
        send_sems, recv_sems = refs[2 * n:]
        mx, my, mc, _ = _place()
        cps = []
        for i in range(n):
            hs = arrays[i].shape[haxes[i]] // 2
            cp = pltpu.make_async_remote_copy(src_ref=_win(ins[i], haxes[i], (1 - mc) * hs, hs), dst_ref=outs[i], send_sem=send_sems.at[i],
                                              recv_sem=recv_sems.at[i], device_id=(mx, my, 1 - mc), device_id_type=MESH)
            cp.start()
            cps.append(cp)
        for cp in cps:
            cp.wait()

    outs = [jax.ShapeDtypeStruct(_cut(a_.shape, h_, 2), a_.dtype) for a_, h_ in zip(arrays, haxes)]
    return _hbm_call(body, arrays, outs, [pltpu.SemaphoreType.DMA((n,)), pltpu.SemaphoreType.DMA((n,))], name)


def add_own_half(g, t, hax, core, *, out_dtype, name):
    l, r, c = t.shape
    tr = _row_tile(r, c)
    per_half = (l, r // tr, 1)[hax]

    def imap(li, ri, cref):
        idx = [li, ri, 0]
        idx[hax] = idx[hax] + cref[0] * per_half
        return tuple(idx)

    def body(c_ref, g_ref, t_ref, o_ref):
        o_ref[...] = (g_ref[...] + t_ref[...]).astype(out_dtype)

    return pl.pallas_call(
        body, name=name, out_shape=jax.ShapeDtypeStruct(t.shape, out_dtype),
        grid_spec=pltpu.PrefetchScalarGridSpec(
            num_scalar_prefetch=1, grid=(l, r // tr),
            in_specs=[pl.BlockSpec((1, tr, c), imap), pl.BlockSpec((1, tr, c), lambda li, ri, cref: (li, ri, 0))],
            out_specs=pl.BlockSpec((1, tr, c), lambda li, ri, cref: (li, ri, 0))),
        compiler_params=_cp(("parallel", "parallel")))(core, g, t)


def exchange_blocks(arrays, axes, *, name):
    n = len(arrays)

    def body(*refs):
        ins, outs = refs[:n], refs[n:2 * n]
        send_sems, recv_sems, local_sems = refs[2 * n:]
        mx, my, mc, others = _place()
        me = 2 * mx + my
        waits = []
        for i in range(n):
            sz = arrays[i].shape[axes[i]] // 4
            cp = pltpu.make_async_copy(_win(ins[i], axes[i], me * sz, sz), outs[i].at[me], local_sems.at[i])
            cp.start()
            waits.append(cp.wait)
            for j, (px, py) in enumerate(others):
                rc = pltpu.make_async_remote_copy(src_ref=_win(ins[i], axes[i], (2 * px + py) * sz, sz), dst_ref=outs[i].at[me],
                                                  send_sem=send_sems.at[i, j], recv_sem=recv_sems.at[i, j], device_id=(px, py, mc),
                                                  device_id_type=MESH)
                rc.start()
                waits.append(rc.wait_send)
        for i in range(n):
            sz = arrays[i].shape[axes[i]] // 4
            for j, (px, py) in enumerate(others):
                pltpu.make_async_remote_copy(src_ref=_win(ins[i], axes[i], me * sz, sz), dst_ref=outs[i].at[2 * px + py],
                                             send_sem=send_sems.at[i, j], recv_sem=recv_sems.at[i, j], device_id=(px, py, mc),
                                             device_id_type=MESH).wait_recv()
        for w_ in waits:
            w_()

    outs = [jax.ShapeDtypeStruct((4,) + _cut(a_.shape, ax, 4), a_.dtype) for a_, ax in zip(arrays, axes)]
    return _hbm_call(body, arrays, outs, [pltpu.SemaphoreType.DMA((n, 3)), pltpu.SemaphoreType.DMA((n, 3)), pltpu.SemaphoreType.DMA((n,))], name)


def exchange_blocks_start(arrays, axes, *, name):
    n = len(arrays)
    lands = [lax.empty((4,) + _cut(a_.shape, ax, 4), a_.dtype) for a_, ax in zip(arrays, axes)]

    def body(*refs):
        ins, lnd = refs[:n], refs[n:2 * n]
        send_sems, recv_sems = refs[2 * n:6 * n], refs[6 * n:9 * n]
        token = refs[11 * n]
        mx, my, mc, others = _place()
        me = 2 * mx + my
        for i in range(n):
            sz = arrays[i].shape[axes[i]] // 4
            pltpu.make_async_copy(_win(ins[i], axes[i], me * sz, sz), lnd[i].at[me], send_sems[4 * i + 3]).start()
            for j, (px, py) in enumerate(others):
                pltpu.make_async_remote_copy(src_ref=_win(ins[i], axes[i], (2 * px + py) * sz, sz), dst_ref=lnd[i].at[me],
                                             send_sem=send_sems[4 * i + j], recv_sem=recv_sems[3 * i + j], device_id=(px, py, mc),
                                             device_id_type=MESH).start()
        token[...] = jnp.zeros_like(token)

    hbm = [pltpu.with_memory_space_constraint(a_, pltpu.HBM) for a_ in arrays + lands]
    out = pl.pallas_call(
        body, name=name,
        out_shape=tuple([pltpu.SemaphoreType.DMA(())] * (7 * n)) + tuple(pltpu.HBM(a_.shape, a_.dtype) for a_ in arrays + lands)
        + (jax.ShapeDtypeStruct((8, 128), F32),),
        in_specs=[HBM_SPEC] * (2 * n),
        out_specs=tuple([SEM_SPEC] * (7 * n)) + tuple([HBM_SPEC] * (2 * n)) + (pl.BlockSpec(memory_space=pltpu.VMEM),),
        input_output_aliases={i: 7 * n + i for i in range(2 * n)}, compiler_params=SPLIT_COPY)(*hbm)
    return list(out[:7 * n]), list(out[7 * n:8 * n]), list(out[8 * n:9 * n]), out[9 * n]


def exchange_blocks_wait(sems, arrays, lands, axes, after, *, name):
    n = len(arrays)

    def body(*refs):
        ins, lnd = refs[:n], refs[n:2 * n]
        send_sems, recv_sems = refs[2 * n:6 * n], refs[6 * n:9 * n]
        mx, my, mc, others = _place()
        me = 2 * mx + my
        for i in range(n):
            sz = arrays[i].shape[axes[i]] // 4
            mine = _win(ins[i], axes[i], me * sz, sz)
            pltpu.make_async_copy(mine, lnd[i].at[me], send_sems[4 * i + 3]).wait()
            for j, (px, py) in enumerate(others):
                cp = pltpu.make_async_remote_copy(src_ref=mine, dst_ref=lnd[i].at[2 * px + py], send_sem=send_sems[4 * i + j],
                                                  recv_sem=recv_sems[3 * i + j], device_id=(px, py, mc), device_id_type=MESH)
                cp.wait_send()
                cp.wait_recv()

    out = pl.pallas_call(
        body, name=name, out_shape=tuple(pltpu.HBM(a_.shape, a_.dtype) for a_ in arrays + lands),
        in_specs=[HBM_SPEC] * (2 * n) + [SEM_SPEC] * (7 * n) + [pl.BlockSpec(memory_space=pl.ANY)], out_specs=tuple([HBM_SPEC] * (2 * n)),
        input_output_aliases={i: i for i in range(2 * n)}, compiler_params=SPLIT_COPY)(*arrays, *lands, *sems, after)
    return list(out[n:])


def sum_blocks(e, hax, core, *, name):
    _, l, r, c = e.shape
    tr = _row_tile(r, c)
    per_half = (l, r // tr, 1)[hax]

    def omap(li, ri, cref):
        idx = [li, ri, 0]
        idx[hax] = idx[hax] + cref[0] * per_half
        return tuple(idx)

    def body(c_ref, e_ref, o_ref):
        v = e_ref[...].astype(F32)
        o_ref[...] = ((v[0] + v[1]) + v[2]) + v[3]

    full = (l, r, c)[:hax] + (2 * (l, r, c)[hax],) + (l, r, c)[hax + 1:]
    return pl.pallas_call(
        body, name=name, out_shape=jax.ShapeDtypeStruct(full, F32),
        grid_spec=pltpu.PrefetchScalarGridSpec(
            num_scalar_prefetch=1, grid=(l, r // tr),
            in_specs=[pl.BlockSpec((4, 1, tr, c), lambda li, ri, cref: (0, li, ri, 0))], out_specs=pl.BlockSpec((1, tr, c), omap)),
        compiler_params=_cp(("parallel", "parallel")))(core, e)


def pair_fill_halves(arrays, haxes, *, name):
    n = len(arrays)

    def body(*refs):
        ins, outs = refs[:n], refs[n:2 * n]
        send_sems, recv_sems = refs[2 * n:]
        mx, my, mc, _ = _place()
        cps = []
        for i in range(n):
            hs = arrays[i].shape[haxes[i]] // 2
            mine = _win(ins[i], haxes[i], mc * hs, hs)
            cp = pltpu.make_async_remote_copy(src_ref=mine, dst_ref=_win(outs[i], haxes[i], mc * hs, hs), send_sem=send_sems.at[i],
                                              recv_sem=recv_sems.at[i], device_id=(mx, my, 1 - mc), device_id_type=MESH)
            cp.start()
            cps.append(cp)
        for i in range(n):
            hs = arrays[i].shape[haxes[i]] // 2
            pltpu.make_async_remote_copy(src_ref=_win(ins[i], haxes[i], mc * hs, hs), dst_ref=_win(outs[i], haxes[i], (1 - mc) * hs, hs),
                                         send_sem=send_sems.at[i], recv_sem=recv_sems.at[i], device_id=(mx, my, 1 - mc),
                                         device_id_type=MESH).wait_recv()
        for cp in cps:
            cp.wait_send()

    return pl.pallas_call(
        body, name=name, out_shape=tuple(jax.ShapeDtypeStruct(a_.shape, a_.dtype) for a_ in arrays), in_specs=[HBM_SPEC] * n,
        out_specs=tuple([HBM_SPEC] * n), input_output_aliases={i: i for i in range(n)},
        scratch_shapes=[pltpu.SemaphoreType.DMA((n,)), pltpu.SemaphoreType.DMA((n,))])(*arrays)


WEIGHTS = ['c_ctx', 'w_mod', 'b_mod', 'norm1_w', 'norm2_w', 'final_norm_w', 's5_w_in', 's5_lam_re', 's5_lam_im', 's5_log_step', 's5_b_re', 's5_b_im', 's5_c_re', 's5_c_im', 's5_d', 's5_w_glu', 's5_w_out', 'hg_w_in', 'hg_lower_bounds', 'hg_gnorm_w', 'hg_w_out', 'ffn_w_up', 'ffn_conv_w', 'ffn_conv_b', 'ffn_w_down']
INPUTS = ['x', 'c', 'ctx', 'c_ctx', 'w_mod', 'b_mod', 'norm1_w', 'norm2_w', 'final_norm_w', 's5_w_in', 's5_lam_re', 's5_lam_im', 's5_log_step', 's5_b_re', 's5_b_im', 's5_c_re', 's5_c_im', 's5_d', 's5_w_glu', 's5_w_out', 'hg_w_in', 'hg_lower_bounds', 'hg_gnorm_w', 'hg_w_out', 'ffn_w_up', 'ffn_conv_w', 'ffn_conv_b', 'ffn_w_down', 'loss_target', 'm_c_ctx', 'm_w_mod', 'm_b_mod', 'm_norm1_w', 'm_norm2_w', 'm_final_norm_w', 'm_s5_w_in', 'm_s5_lam_re', 'm_s5_lam_im', 'm_s5_log_step', 'm_s5_b_re', 'm_s5_b_im', 'm_s5_c_re', 'm_s5_c_im', 'm_s5_d', 'm_s5_w_glu', 'm_s5_w_out', 'm_hg_w_in', 'm_hg_lower_bounds', 'm_hg_gnorm_w', 'm_hg_w_out', 'm_ffn_w_up', 'm_ffn_conv_w', 'm_ffn_conv_b', 'm_ffn_w_down', 'v_c_ctx', 'v_w_mod', 'v_b_mod', 'v_norm1_w', 'v_norm2_w', 'v_final_norm_w', 'v_s5_w_in', 'v_s5_lam_re', 'v_s5_lam_im', 'v_s5_log_step', 'v_s5_b_re', 'v_s5_b_im', 'v_s5_c_re', 'v_s5_c_im', 'v_s5_d', 'v_s5_w_glu', 'v_s5_w_out', 'v_hg_w_in', 'v_hg_lower_bounds', 'v_hg_gnorm_w', 'v_hg_w_out', 'v_ffn_w_up', 'v_ffn_conv_w', 'v_ffn_conv_b', 'v_ffn_w_down']
SHARD_AXIS = {"w_mod": 2, "s5_w_in": 1, "s5_w_glu": 1, "s5_w_out": 1, "hg_w_in": 2, "hg_lower_bounds": 2, "hg_w_out": 1,
              "ffn_w_up": 2, "ffn_conv_w": 2, "ffn_w_down": 1}
GATHER_F32 = ("hg_lower_bounds", "ffn_conv_w")
PACK_W = 1024
GRAD_WIRE = jnp.bfloat16


def _reduce_start(items, core, tag):
    names, arrays, axes = [n for n, _, _ in items], [g_ for _, g_, _ in items], [ax for _, _, ax in items]
    haxes = [_half_axis(g_.shape, ax) for g_, ax in zip(arrays, axes)]
    t = pair_swap_halves(arrays, haxes, name="grad_pair_swap_" + tag)
    h = [add_own_half(g_, t_, hx, core, out_dtype=GRAD_WIRE, name="grad_pair_add_" + n) for g_, t_, hx, n in zip(arrays, t, haxes, names)]
    sems, h, lands, token = exchange_blocks_start(h, axes, name="grad_exchange_start_" + tag)
    return (names, sems, h, lands, axes, haxes), token


def _reduce_finish(state, core, after, tag):
    names, sems, h, lands, axes, haxes = state
    e = exchange_blocks_wait(sems, h, lands, axes, after, name="grad_exchange_wait_" + tag)
    s = [sum_blocks(e_, hx, core, name="grad_chip_sum_" + n) for e_, hx, n in zip(e, haxes, names)]
    return dict(zip(names, pair_fill_halves(s, haxes, name="grad_pair_fill_" + tag)))


def _reduce_now(a, items, small, grads, core):
    flat = jnp.concatenate([grads[n].reshape(-1) for n in small])
    pad = (-flat.shape[0]) % (64 * PACK_W)
    small_pack = jnp.pad(flat, (0, pad)).reshape(1, -1, PACK_W)
    names = [n for n, _, _ in items] + ["small"]
    arrays = [g_ for _, g_, _ in items] + [small_pack]
    axes = [ax for _, _, ax in items] + [1]
    haxes = [_half_axis(g_.shape, ax) for g_, ax in zip(arrays, axes)]
    t = pair_swap_halves(arrays, haxes, name="grad_pair_swap")
    h = [add_own_half(g_, t_, hx, core, out_dtype=GRAD_WIRE, name="grad_pair_add_" + n) for g_, t_, hx, n in zip(arrays, t, haxes, names)]
    e = exchange_blocks(h, axes, name="grad_chip_exchange")
    s = [sum_blocks(e_, hx, core, name="grad_chip_sum_" + n) for e_, hx, n in zip(e, haxes, names)]
    red = pair_fill_halves(s, haxes, name="grad_pair_fill")
    out = dict(zip(names[:-1], red[:-1]))
    sm = chip_allgather(red[-1][0], name="allgather_small_grads").reshape(-1)
    off = 0
    for n in small:
        out[n] = sm[off:off + math.prod(a[n].shape)].reshape(a[n].shape)
        off += math.prod(a[n].shape)
    return out


def _blockdiag_b(bb, kb):
    gl = S5_KIN // S5_GROUP
    x = bb.reshape(kb, gl, S5_GROUP, S5_STATE)
    return (x[:, :, :, None, :] * jnp.eye(gl, dtype=bb.dtype)[None, :, None, :, None]).reshape(kb, S5_KIN, S5_KST)


def _blockdiag_c(cc, kb):
    gl = S5_KIN // S5_GROUP
    x = cc.reshape(kb, gl, S5_GROUP, S5_STATE).transpose(0, 1, 3, 2)
    return (x[:, :, :, None, :] * jnp.eye(gl, dtype=cc.dtype)[None, :, None, :, None]).reshape(kb, S5_KST, S5_KIN)


def _diag_b(m, kb):
    gl = S5_KIN // S5_GROUP
    x = m.reshape(kb, gl, S5_GROUP, gl, S5_STATE)
    return jnp.stack([x[:, i, :, i, :] for i in range(gl)], axis=1).reshape(kb * gl, S5_GROUP, S5_STATE)


def _diag_c(m, kb):
    gl = S5_KIN // S5_GROUP
    x = m.reshape(kb, gl, S5_STATE, gl, S5_GROUP)
    return jnp.stack([x[:, i, :, i, :] for i in range(gl)], axis=1).transpose(0, 1, 3, 2).reshape(kb * gl, S5_GROUP, S5_STATE)


def kernel(x, c, ctx, c_ctx, w_mod, b_mod, norm1_w, norm2_w, final_norm_w, s5_w_in, s5_lam_re, s5_lam_im, s5_log_step, s5_b_re, s5_b_im, s5_c_re, s5_c_im, s5_d, s5_w_glu, s5_w_out, hg_w_in, hg_lower_bounds, hg_gnorm_w, hg_w_out, ffn_w_up, ffn_conv_w, ffn_conv_b, ffn_w_down, loss_target, m_c_ctx, m_w_mod, m_b_mod, m_norm1_w, m_norm2_w, m_final_norm_w, m_s5_w_in, m_s5_lam_re, m_s5_lam_im, m_s5_log_step, m_s5_b_re, m_s5_b_im, m_s5_c_re, m_s5_c_im, m_s5_d, m_s5_w_glu, m_s5_w_out, m_hg_w_in, m_hg_lower_bounds, m_hg_gnorm_w, m_hg_w_out, m_ffn_w_up, m_ffn_conv_w, m_ffn_conv_b, m_ffn_w_down, v_c_ctx, v_w_mod, v_b_mod, v_norm1_w, v_norm2_w, v_final_norm_w, v_s5_w_in, v_s5_lam_re, v_s5_lam_im, v_s5_log_step, v_s5_b_re, v_s5_b_im, v_s5_c_re, v_s5_c_im, v_s5_d, v_s5_w_glu, v_s5_w_out, v_hg_w_in, v_hg_lower_bounds, v_hg_gnorm_w, v_hg_w_out, v_ffn_w_up, v_ffn_conv_w, v_ffn_conv_b, v_ffn_w_down):
    a = dict(zip(INPUTS, (x, c, ctx, c_ctx, w_mod, b_mod, norm1_w, norm2_w, final_norm_w, s5_w_in, s5_lam_re, s5_lam_im, s5_log_step, s5_b_re, s5_b_im, s5_c_re, s5_c_im, s5_d, s5_w_glu, s5_w_out, hg_w_in, hg_lower_bounds, hg_gnorm_w, hg_w_out, ffn_w_up, ffn_conv_w, ffn_conv_b, ffn_w_down, loss_target, m_c_ctx, m_w_mod, m_b_mod, m_norm1_w, m_norm2_w, m_final_norm_w, m_s5_w_in, m_s5_lam_re, m_s5_lam_im, m_s5_log_step, m_s5_b_re, m_s5_b_im, m_s5_c_re, m_s5_c_im, m_s5_d, m_s5_w_glu, m_s5_w_out, m_hg_w_in, m_hg_lower_bounds, m_hg_gnorm_w, m_hg_w_out, m_ffn_w_up, m_ffn_conv_w, m_ffn_conv_b, m_ffn_w_down, v_c_ctx, v_w_mod, v_b_mod, v_norm1_w, v_norm2_w, v_final_norm_w, v_s5_w_in, v_s5_lam_re, v_s5_lam_im, v_s5_log_step, v_s5_b_re, v_s5_b_im, v_s5_c_re, v_s5_c_im, v_s5_d, v_s5_w_glu, v_s5_w_out, v_hg_w_in, v_hg_lower_bounds, v_hg_gnorm_w, v_hg_w_out, v_ffn_w_up, v_ffn_conv_w, v_ffn_conv_b, v_ffn_w_down)))
    nb, seq, d = x.shape
    assert nb == NB
    rc = nb * ctx.shape[1]
    cfg = {"rc": rc}
    f = a["ffn_w_down"].shape[1] * 4
    core = lax.axis_index("c").astype(jnp.int32).reshape(1)

    w = {n: a[n] for n in WEIGHTS if n not in SHARD_AXIS}
    chip = (2 * lax.axis_index("x") + lax.axis_index("y")).astype(jnp.int32).reshape(1)
    groups = {
        "now": [("w_mod0", a["w_mod"][0:1]), ("s5_w_in", a["s5_w_in"]), ("hg_lower_bounds", a["hg_lower_bounds"]), ("ffn_conv_w", a["ffn_conv_w"])],
        "mid": [("s5_w_glu", a["s5_w_glu"]), ("s5_w_out", a["s5_w_out"]), ("ffn_w_up0", a["ffn_w_up"][0:1]), ("ffn_w_down0", a["ffn_w_down"][0:1])],
        "later": [("w_mod1", a["w_mod"][1:2]), ("hg_w_in", a["hg_w_in"]), ("hg_w_out", a["hg_w_out"]), ("ffn_w_up1", a["ffn_w_up"][1:2]),
                  ("ffn_w_down1", a["ffn_w_down"][1:2])]}
    shard_axis = lambda n: SHARD_AXIS[n.rstrip("01")]
    placed = {g: [place_shard(s_, shard_axis(n), chip, F32 if n in GATHER_F32 else MXU, name="place_" + n) for n, s_ in it] for g, it in groups.items()}
    axes = {g: [shard_axis(n) for n, _ in it] for g, it in groups.items()}
    haxes = {g: [_half_axis(p_.shape, ax) for p_, ax in zip(placed[g], axes[g])] for g in groups}
    got = pair_fill_halves(gather_placed(placed["now"], axes["now"], haxes["now"], name="allgather_weights"), haxes["now"],
                           name="allgather_pair_fill")
    w.update(dict(zip([n for n, _ in groups["now"]], got)))
    fly_mid = gather_placed_start(placed["mid"], axes["mid"], haxes["mid"], got[0], name="allgather_mid_start")
    fly_later = gather_placed_start(placed["later"], axes["later"], haxes["later"], fly_mid[3], name="allgather_later_start")

    def land(fly, g, after):
        send_, recv_, flying, _ = fly
        landed = gather_placed_wait(flying, send_, recv_, axes[g], haxes[g], after, name=f"allgather_{g}_wait")
        w.update(dict(zip([n for n, _ in groups[g]], pair_fill_halves(landed, haxes[g], name=f"allgather_{g}_pair_fill"))))

    tmaj = lambda t: t.transpose(1, 0, 2).reshape(-1, t.shape[-1])
    x0 = jnp.concatenate([tmaj(ctx), tmaj(x)], axis=0)
    tgt = tmaj(a["loss_target"])
    c16 = jnp.concatenate([jnp.broadcast_to(c_ctx[None], (8, d)), c, c], axis=0) + fly_later[3][0:1, 0:1]
    mt0, scb = mod_fwd(c16, w["w_mod0"][0], w["b_mod"][0][None], name="mod_fwd0")
    mt = [mt0, None]
    n1, n2 = w["norm1_w"], w["norm2_w"]
    w["w_mod"], w["ffn_w_up"], w["ffn_w_down"] = [w["w_mod0"][0], None], [None, None], [None, None]

    def ffn_fwd(l, h):
        u = mm(h, w["ffn_w_up"][l], name=f"ffn_up{l}")
        act, ca, cg = ffn_mid_fwd(cfg, u, w["ffn_conv_w"][l], w["ffn_conv_b"][l][None], name=f"ffn_mid{l}")
        return (u, ca, cg), act, mm(act, w["ffn_w_down"][l], name=f"ffn_down{l}")

    def ffn_bwd(l, dfo, kept, act, h, zero=0.0):
        dact = mm(dfo, w["ffn_w_down"][l], tb=True, name=f"ffn_down_dx{l}")
        dwd = mm(act, dfo, ta=True, name=f"ffn_down_dw{l}")
        du, dcw, dcb = ffn_mid_bwd(cfg, dact, *kept, w["ffn_conv_w"][l] + zero, name=f"ffn_mid_bwd{l}")
        dh = mm(du, w["ffn_w_up"][l], tb=True, name=f"ffn_up_dx{l}")
        dwu = mm(h, du, ta=True, name=f"ffn_up_dw{l}")
        return dh, dwu, dcw, dcb[0], dwd

    g_, p_ = d // S5_GROUP, S5_STATE
    ns, kb = g_ * p_, d // S5_KIN
    s5p = (w["s5_lam_re"][0].reshape(2 * g_, p_), w["s5_lam_im"][0].reshape(2 * g_, p_), w["s5_log_step"][0].reshape(2 * g_, 1),
           w["s5_b_re"][0].transpose(0, 1, 3, 2).reshape(2 * g_, S5_GROUP, p_), w["s5_b_im"][0].transpose(0, 1, 3, 2).reshape(2 * g_, S5_GROUP, p_))
    ar, ai, bbr, bbi = s5_disc_fwd(*s5p, name="s5_disc")
    dsk = w["s5_d"]
    _, h1 = node_fwd(cfg, x0, None, None, 0, n1[0:1], mt[0], 0, name="node0a")
    u0 = mm(h1, w["s5_w_in"][0], name="s5_in")
    s5s, ys = [], []
    for dd in range(2):
        sl = slice(dd * g_, (dd + 1) * g_)
        a_r, a_i = ar[sl].reshape(1, ns), ai[sl].reshape(1, ns)
        a2 = (a_r * a_r - a_i * a_i, 2.0 * a_r * a_i)
        b_r, b_i = _blockdiag_b(bbr[sl], kb), _blockdiag_b(bbi[sl], kb)
        c_r, c_i = _blockdiag_c(w["s5_c_re"][0, dd], kb), _blockdiag_c(w["s5_c_im"][0, dd], kb)
        ak, ai_k = a_r.reshape(kb, 1, S5_KST), a_i.reshape(kb, 1, S5_KST)
        ab = (ak * b_r - ai_k * b_i, ak * b_i + ai_k * b_r)
        akc, aic = ak.reshape(kb, S5_KST, 1), ai_k.reshape(kb, S5_KST, 1)
        c2 = (akc * c_r - aic * c_i, akc * c_i + aic * c_r)
        bf = lambda t_: t_.astype(MXU)
        sre, sim, ere, eim, y_ = s5_scan_fwd(cfg, u0, a2[0], a2[1], bf(b_r), bf(b_i), bf(ab[0]), bf(ab[1]), bf(c_r), bf(c_i), rev=dd == 1,
                                             name=f"s5_scan{dd}")
        s5s.append((sre, sim, ere, eim, a2[0], a2[1], bf(b_r), bf(b_i), bf(c_r), bf(c_i), bf(c2[0]), bf(c2[1])))
        ys.append(y_)

    def glu_a(u, y0, y1, ds):
        yp = (ds * u + y0) + y1
        return yp, _gelu(yp)

    ypre, zgb = rowmap(glu_a, [u0, ys[0], ys[1]], [dsk], [(d, F32), (d, MXU)], name="s5_glu_a")
    land(fly_mid, "mid", zgb)
    w["ffn_w_up"][0], w["ffn_w_down"][0] = w["ffn_w_up0"][0], w["ffn_w_down0"][0]
    tg = mm(zgb, w["s5_w_glu"][0], name="s5_glu")
    (z2,) = rowmap(lambda yp, t: _gelu(yp) * jax.nn.sigmoid(t), [ypre, tg], [], [(d, MXU)], name="s5_glu_b")
    y1a = mm(z2, w["s5_w_out"][0], name="s5_out")
    x1a, h2a = node_fwd(cfg, x0, y1a, mt[0], 2, n2[0:1], mt[0], 3, name="node0b")
    ufa, acta, foa = ffn_fwd(0, h2a)

    land(fly_later, "later", foa)
    w["w_mod"][1], w["ffn_w_up"][1], w["ffn_w_down"][1] = w["w_mod1"][0], w["ffn_w_up1"][0], w["ffn_w_down1"][0]
    mt[1], _ = mod_fwd(c16, w["w_mod"][1], w["b_mod"][1][None], name="mod_fwd1")
    x2a, h1b = node_fwd(cfg, x1a, foa, mt[0], 5, n1[1:2], mt[1], 0, name="node1a")
    z = mm(h1b, w["hg_w_in"][0], name="hg_in")
    e0, e1 = w["hg_lower_bounds"][:, 0, :], w["hg_lower_bounds"][:, 1, :]
    lb = hg_lb_fwd(e0, e1, name="hg_lb")
    gw = w["hg_gnorm_w"]
    o0, sin0 = hg_scan_fwd(cfg, z, lb[0:1], d_dir=0, name="hg_scan0")
    o1, sin1 = hg_scan_fwd(cfg, z, lb[1:2], d_dir=1, name="hg_scan1")
    onb = hg_read_fwd(o0, o1, z, gw, name="hg_read")
    y1b = mm(onb, w["hg_w_out"][0], name="hg_out")
    x1b, h2b = node_fwd(cfg, x2a, y1b, mt[1], 2, n2[1:2], mt[1], 3, name="node1b")
    ufb, actb, fob = ffn_fwd(1, h2b)
    loss_p, dx2b, dfob, dg2_1, dfnw = final_node(cfg, x1b, fob, mt[1], 5, w["final_norm_w"][None], tgt, name="final_node")

    gr = {}
    dh2b, dwu1, dcw1, dcb1, dwd1 = ffn_bwd(1, dfob, ufb, actb, h2b)
    dx1b, dy1b, dn2_1, dsh2_1, dsc2_1, dg1_1 = node_bwd(cfg, dx2b, dh2b, x1b, y1b, mt[1], 2, n2[1:2], mt[1], 3, name="node1b_bwd")
    don = mm(dy1b, w["hg_w_out"][0], tb=True, name="hg_out_dx")
    gr["hg_w_out"] = mm(onb, dy1b, ta=True, name="hg_out_dw")[None]
    do_, dgate_, dgw = hg_read_bwd(don, o0, o1, z, gw, name="hg_read_bwd")
    dq, dv, dxf, dlb0 = hg_scan_bwd(cfg, do_, z, lb[0:1], sin0, None, None, d_dir=0, name="hg_scan_bwd0")
    dq, dv, dxb, dlb1 = hg_scan_bwd(cfg, do_, z, lb[1:2], sin1, dq, dv, d_dir=1, name="hg_scan_bwd1")
    dz = jnp.concatenate([t_.astype(MXU) for t_ in (dq, dv, dxf, dxb, dgate_)], axis=1)
    dh1b = mm(dz, w["hg_w_in"][0], tb=True, name="hg_in_dx")
    gr["hg_w_in"] = mm(h1b, dz, ta=True, name="hg_in_dw")[None]
    de0, de1 = hg_lb_bwd(e0, e1, jnp.concatenate([dlb0, dlb1], axis=0), name="hg_lb_bwd")
    gr["hg_lower_bounds"] = jnp.stack([de0, de1], axis=1)
    gr["hg_gnorm_w"] = dgw
    dx2a, dfoa, dn1_1, dsh1_1, dsc1_1, dg2_0 = node_bwd(cfg, dx1b, dh1b, x2a, foa, mt[0], 5, n1[1:2], mt[1], 0, name="node1a_bwd")
    dmt1 = jnp.concatenate([dsh1_1, dsc1_1, dg1_1, dsh2_1, dsc2_1, dg2_1], axis=1)
    red1, tok1 = _reduce_start([("hg_w_in", gr["hg_w_in"], 2), ("hg_w_out", gr["hg_w_out"], 1), ("ffn_w_up1", dwu1[None], 2),
                                ("ffn_w_down1", dwd1[None], 1), ("w_mod1", mm(scb, dmt1, ta=True, name="mod_dw1")[None], 2)], core, "layer1")

    dh2a, dwu0, dcw0, dcb0, dwd0 = ffn_bwd(0, dfoa, ufa, acta, h2a, zero=tok1[0:1, 0:1])
    red2, tok2 = _reduce_start([("ffn_w_up0", dwu0[None], 2), ("ffn_w_down0", dwd0[None], 1)], core, "ffn0")
    dx1a, dy1a, dn2_0, dsh2_0, dsc2_0, dg1_0 = node_bwd(cfg, dx2a, dh2a, x1a, y1a, mt[0], 2, n2[0:1] + tok2[0:1, 0:1], mt[0], 3,
                                                        name="node0b_bwd")
    dz2 = mm(dy1a, w["s5_w_out"][0], tb=True, name="s5_out_dx")
    gr["s5_w_out"] = mm(z2, dy1a, ta=True, name="s5_out_dw")[None]

    def glu_b_bwd(dz2_, yp, t):
        zg, sg = _gelu(yp), jax.nn.sigmoid(t)
        return dz2_ * zg * sg * (1.0 - sg), dz2_ * sg

    dtg, dzg_dir = rowmap(glu_b_bwd, [dz2, ypre, tg], [], [(d, MXU), (d, F32)], name="s5_glu_b_bwd")
    dzg_mm = mm(dtg, w["s5_w_glu"][0], tb=True, name="s5_glu_dx")
    gr["s5_w_glu"] = mm(zgb, dtg, ta=True, name="s5_glu_dw")[None]

    def glu_a_bwd(dzd, dzm, yp, u, ds):
        _, vjp = jax.vjp(_gelu, yp)
        (dy,) = vjp(dzd + dzm)
        return dy, dy * ds, jnp.sum(dy * u, axis=0, keepdims=True)

    dyb, du, ddsk = rowmap(glu_a_bwd, [dzg_dir, dzg_mm, ypre, u0], [dsk], [(d, MXU), (d, F32)], [(1, d)], name="s5_glu_a_bwd")
    gr["s5_d"] = ddsk
    dar, dai, dbr, dbi, dcr, dci = [], [], [], [], [], []
    for dd in range(2):
        sre, sim, ere, eim = s5s[dd][:4]
        du, gre, gim, da_r, da_i = s5_scan_bwd(cfg, dyb, *s5s[dd], du, rev=dd == 1, name=f"s5_scan_bwd{dd}")
        dar.append(colsum(da_r, name=f"s5_da_re{dd}").reshape(g_, p_))
        dai.append(colsum(da_i, name=f"s5_da_im{dd}").reshape(g_, p_))
        dbr.append(_diag_b(blockdiag_tn(u0, gre, S5_KIN, S5_KST, name=f"s5_db_re{dd}"), kb))
        dbi.append(_diag_b(blockdiag_tn(u0, gim, S5_KIN, S5_KST, name=f"s5_db_im{dd}"), kb))
        dcr.append(_diag_c(blockdiag_tn(sre.reshape(-1, ns), dyb, S5_KST, S5_KIN, name=f"s5_dc_re{dd}"), kb))
        dci.append(_diag_c(blockdiag_tn(sim.reshape(-1, ns), dyb, S5_KST, S5_KIN, scale=-1.0, name=f"s5_dc_im{dd}"), kb))
    cat = lambda l_: jnp.concatenate(l_, axis=0)
    dlr, dli, dls, dbre, dbim = s5_disc_bwd(*s5p, cat(dar), cat(dai), cat(dbr), cat(dbi), name="s5_disc_bwd")
    gr["s5_lam_re"], gr["s5_lam_im"] = dlr.reshape(1, 2, g_, p_), dli.reshape(1, 2, g_, p_)
    gr["s5_log_step"] = dls.reshape(1, 2, g_)
    gr["s5_b_re"] = dbre.reshape(1, 2, g_, S5_GROUP, p_).transpose(0, 1, 2, 4, 3)
    gr["s5_b_im"] = dbim.reshape(1, 2, g_, S5_GROUP, p_).transpose(0, 1, 2, 4, 3)
    gr["s5_c_re"], gr["s5_c_im"] = jnp.stack(dcr)[None], jnp.stack(dci)[None]
    dh1 = mm(du, w["s5_w_in"][0], tb=True, name="s5_in_dx")
    gr["s5_w_in"] = mm(h1, du, ta=True, name="s5_in_dw")[None]
    dx0, _, dn1_0, dsh1_0, dsc1_0, _ = node_bwd(cfg, dx1a, dh1, x0, None, None, 0, n1[0:1], mt[0], 0, name="node0a_bwd")

    dmt = [jnp.concatenate([dsh1_0, dsc1_0, dg1_0, dsh2_0, dsc2_0, dg2_0], axis=1), dmt1]
    gr["b_mod"] = jnp.concatenate([colsum(dmt[l], name=f"mod_db{l}") for l in range(2)], axis=0)
    dsc16 = [mm(dmt[l], w["w_mod"][l], tb=True, name=f"mod_dx{l}") for l in range(2)]
    gr["c_ctx"] = cctx_grad(c16, dsc16, name="c_ctx_grad")[0]
    gr["norm1_w"] = jnp.concatenate([dn1_0, dn1_1], axis=0)
    gr["norm2_w"] = jnp.concatenate([dn2_0, dn2_1], axis=0)
    gr["final_norm_w"] = dfnw[0]
    gr["ffn_conv_w"], gr["ffn_conv_b"] = jnp.stack([dcw0, dcw1]), jnp.stack([dcb0, dcb1])

    last = [(n, gr[n], SHARD_AXIS[n]) for n in ("s5_w_in", "s5_w_glu", "s5_w_out", "hg_lower_bounds", "ffn_conv_w")]
    last.append(("w_mod0", mm(scb, dmt[0], ta=True, name="mod_dw0")[None], 2))
    red = _reduce_now(a, last, [n for n in WEIGHTS if n not in SHARD_AXIS], gr, core)
    red.update(_reduce_finish(red1, core, dx0, "layer1"))
    red.update(_reduce_finish(red2, core, dx0, "ffn0"))
    red["w_mod"] = jnp.concatenate([red["w_mod0"], red["w_mod1"]], axis=0)
    red["ffn_w_up"] = jnp.concatenate([red["ffn_w_up0"], red["ffn_w_up1"]], axis=0)
    red["ffn_w_down"] = jnp.concatenate([red["ffn_w_down0"], red["ffn_w_down1"]], axis=0)
    loss = lax.psum(loss_p[0, 0], ("x", "y", "c"))
    grad_x = dx0[rc:].reshape(seq, nb, d).transpose(1, 0, 2)
    upd = {n: adamw(a[n], red[n], a["m_" + n], a["v_" + n], name="adamw_" + n) for n in WEIGHTS}
    return (loss, grad_x, *[red[n] for n in WEIGHTS], *[upd[n][0] for n in WEIGHTS], *[upd[n][1] for n in WEIGHTS],
            *[upd[n][2] for n in WEIGHTS])
```

```python
import functools
import math

import jax
import jax.numpy as jnp
from jax import lax
from jax.experimental import pallas as pl
from jax.experimental.pallas import tpu as pltpu

F32 = jnp.float32
BF = jnp.bfloat16
MXU = jnp.bfloat16

NORM_EPS = 1e-6
GRID_W = 64
N_MOD = 6
S5_GROUP = 16
S5_STATE = 64
S5_LAM_RE_MAX = -1e-4
S5_KIN = 256
S5_KST = S5_KIN // S5_GROUP * S5_STATE
HEAD = 128
CHUNK_ROWS = 128
N_PROJ = 5
NB = 4
ADAM_LR, ADAM_B1, ADAM_B2, ADAM_EPS, ADAM_WD, ADAM_STEP = 0.001, 0.9, 0.999, 1e-08, 0.01, 10
VMEM_LIMIT = 56 * 1024 * 1024
MESH = pl.DeviceIdType.MESH


def _tile(n, cap):
    if n <= cap:
        return n
    best = None
    for t in range(128, cap + 1, 128):
        if n % t == 0:
            best = t
    assert best is not None, (n, cap)
    return best


def _row_tile(r, width=1024):
    cap = max(8, (512 * 1024) // max(width, 1))
    return next((t for t in (512, 256, 128, 64, 32, 16, 8) if t <= cap and r % t == 0), r)


def _cp(sem):
    return pltpu.CompilerParams(dimension_semantics=sem, vmem_limit_bytes=VMEM_LIMIT)


def _dot(a, b, ca=1, cb=0):
    return lax.dot_general(a.astype(MXU), b.astype(MXU), (((ca,), (cb,)), ((), ())), preferred_element_type=F32)


def _dot3(m, x):
    hi = x.astype(MXU)
    r1 = x - hi.astype(F32)
    mid = r1.astype(MXU)
    lo = (r1 - mid.astype(F32)).astype(MXU)
    return _dot(m, hi) + _dot(m, mid) + _dot(m, lo)


def mm(a, b, *, ta=False, tb=False, out_dtype=F32, name):
    (kd, m) = a.shape if ta else a.shape[::-1]
    (n, kd2) = b.shape if tb else b.shape[::-1]
    assert kd == kd2, (a.shape, b.shape, ta, tb)
    tm, tn, tk = _tile(m, 1024), _tile(n, 1536), _tile(kd, 1024)
    nk = kd // tk

    def body(a_ref, b_ref, o_ref, acc_ref):
        k = pl.program_id(2)

        @pl.when(k == 0)
        def _():
            acc_ref[...] = jnp.zeros_like(acc_ref)

        acc_ref[...] += _dot(a_ref[...], b_ref[...], 0 if ta else 1, 1 if tb else 0)

        @pl.when(k == nk - 1)
        def _():
            o_ref[...] = acc_ref[...].astype(out_dtype)

    a_spec = pl.BlockSpec((tk, tm), lambda i, j, k: (k, i)) if ta else pl.BlockSpec((tm, tk), lambda i, j, k: (i, k))
    b_spec = pl.BlockSpec((tn, tk), lambda i, j, k: (j, k)) if tb else pl.BlockSpec((tk, tn), lambda i, j, k: (k, j))
    return pl.pallas_call(
        body, name=name, grid=(m // tm, n // tn, nk), in_specs=[a_spec, b_spec],
        out_specs=pl.BlockSpec((tm, tn), lambda i, j, k: (i, j)), out_shape=jax.ShapeDtypeStruct((m, n), out_dtype),
        scratch_shapes=[pltpu.VMEM((tm, tn), F32)], compiler_params=_cp(("parallel", "parallel", "arbitrary")))(a, b)


def blockdiag_tn(a, b, wa, wb, *, scale=1.0, name):
    rows = a.shape[0]
    kb = a.shape[1] // wa
    tr = _tile(rows, 1024)
    nr = rows // tr

    def body(a_ref, b_ref, o_ref):
        i = pl.program_id(1)

        @pl.when(i == 0)
        def _():
            o_ref[...] = jnp.zeros_like(o_ref)

        o_ref[0] += scale * _dot(a_ref[...], b_ref[...], 0, 0)

    return pl.pallas_call(
        body, name=name, grid=(kb, nr),
        in_specs=[pl.BlockSpec((tr, wa), lambda k, i: (i, k)), pl.BlockSpec((tr, wb), lambda k, i: (i, k))],
        out_specs=pl.BlockSpec((1, wa, wb), lambda k, i: (k, 0, 0)), out_shape=jax.ShapeDtypeStruct((kb, wa, wb), F32),
        compiler_params=_cp(("parallel", "arbitrary")))(a, b)


def _pat(v, p, op):
    tm, d = v.shape
    return op(v.reshape(tm // 8, 8, d), p[None]).reshape(tm, d)


def _norm_mod(x, nw, shift, scale):
    y = x * lax.rsqrt(jnp.mean(x * x, axis=-1, keepdims=True) + NORM_EPS) * nw
    return _pat(_pat(y, 1.0 + scale, jnp.multiply), shift, jnp.add)


def _mt_spec(d, nct):
    return pl.BlockSpec((8, N_MOD * d), lambda i: (jnp.where(i < nct, 0, 1), 0))


def _acc_spec(d, nct):
    return pl.BlockSpec((8, d), lambda i: (jnp.where(i < nct, 0, 1), 0))


def _rows(cfg):
    tm = min(512, cfg["rc"])
    return tm, cfg["rc"] // tm


def node_fwd(cfg, xp, y, mtg, gi, nw, mtn, si, *, name):
    r, d = xp.shape
    tm, nct = _rows(cfg)
    row = pl.BlockSpec((tm, d), lambda i: (i, 0))
    vec = pl.BlockSpec((1, d), lambda i: (0, 0))

    def body(*refs):
        if y is None:
            xp_ref, nw_ref, mtn_ref, h_ref = refs
            x = xp_ref[...]
        else:
            xp_ref, y_ref, mtg_ref, nw_ref, mtn_ref, xn_ref, h_ref = refs
            x = xp_ref[...] + _pat(y_ref[...], mtg_ref[:, gi * d:(gi + 1) * d], jnp.multiply)
            xn_ref[...] = x
        h_ref[...] = _norm_mod(x, nw_ref[...], mtn_ref[:, si * d:(si + 1) * d], mtn_ref[:, (si + 1) * d:(si + 2) * d]).astype(MXU)

    h_shape = jax.ShapeDtypeStruct((r, d), MXU)
    if y is None:
        h = pl.pallas_call(body, name=name, grid=(r // tm,), in_specs=[row, vec, _mt_spec(d, nct)], out_specs=row,
                           out_shape=h_shape, compiler_params=_cp(("parallel",)))(xp, nw, mtn)
        return xp, h
    return pl.pallas_call(body, name=name, grid=(r // tm,), in_specs=[row, row, _mt_spec(d, nct), vec, _mt_spec(d, nct)],
                          out_specs=(row, row), out_shape=(jax.ShapeDtypeStruct((r, d), F32), h_shape),
                          compiler_params=_cp(("parallel",)))(xp, y, mtg, nw, mtn)


def node_bwd(cfg, dxres, dh, xn, y, mtg, gi, nw, mtn, si, *, name):
    r, d = xn.shape
    tm, nct = _rows(cfg)
    row = pl.BlockSpec((tm, d), lambda i: (i, 0))
    vec = pl.BlockSpec((1, d), lambda i: (0, 0))
    has_y = y is not None

    def body(*refs):
        if has_y:
            dxres_ref, dh_ref, xn_ref, y_ref, mtg_ref, nw_ref, mtn_ref, dxn_ref, dy_ref, dnw_ref, dsh_ref, dsc_ref, dg_ref = refs
        else:
            dxres_ref, dh_ref, xn_ref, nw_ref, mtn_ref, dxn_ref, dnw_ref, dsh_ref, dsc_ref = refs
        i = pl.program_id(0)
        _, vjp = jax.vjp(_norm_mod, xn_ref[...], nw_ref[...], mtn_ref[:, si * d:(si + 1) * d], mtn_ref[:, (si + 1) * d:(si + 2) * d])
        dx, dnw, dsh, dsc = vjp(dh_ref[...])
        dx = dx + dxres_ref[...]
        dxn_ref[...] = dx

        @pl.when(i == 0)
        def _():
            dnw_ref[...] = jnp.zeros_like(dnw_ref)

        @pl.when((i == 0) | (i == nct))
        def _():
            dsh_ref[...] = jnp.zeros_like(dsh_ref)
            dsc_ref[...] = jnp.zeros_like(dsc_ref)
            if has_y:
                dg_ref[...] = jnp.zeros_like(dg_ref)

        dnw_ref[...] += dnw
        dsh_ref[...] += dsh
        dsc_ref[...] += dsc
        if has_y:
            dy_ref[...] = _pat(dx, mtg_ref[:, gi * d:(gi + 1) * d], jnp.multiply).astype(MXU)
            dg_ref[...] += jnp.sum((dx * y_ref[...]).reshape(tm // 8, 8, d), axis=0)

    acc = jax.ShapeDtypeStruct((16, d), F32)
    xs = jax.ShapeDtypeStruct((r, d), F32)
    if has_y:
        return pl.pallas_call(
            body, name=name, grid=(r // tm,), in_specs=[row, row, row, row, _mt_spec(d, nct), vec, _mt_spec(d, nct)],
            out_specs=(row, row, vec, _acc_spec(d, nct), _acc_spec(d, nct), _acc_spec(d, nct)),
            out_shape=(xs, jax.ShapeDtypeStruct((r, d), MXU), jax.ShapeDtypeStruct((1, d), F32), acc, acc, acc),
            compiler_params=_cp(("arbitrary",)))(dxres, dh, xn, y, mtg, nw, mtn)
    dxn, dnw, dsh, dsc = pl.pallas_call(
        body, name=name, grid=(r // tm,), in_specs=[row, row, row, vec, _mt_spec(d, nct)],
        out_specs=(row, vec, _acc_spec(d, nct), _acc_spec(d, nct)),
        out_shape=(xs, jax.ShapeDtypeStruct((1, d), F32), acc, acc), compiler_params=_cp(("arbitrary",)))(dxres, dh, xn, nw, mtn)
    return dxn, None, dnw, dsh, dsc, None


def final_node(cfg, xp, y, mtg, gi, fnw, tgt, *, name):
    r, d = xp.shape
    tm, nct = _rows(cfg)
    row = pl.BlockSpec((tm, d), lambda i: (i, 0))
    vec = pl.BlockSpec((1, d), lambda i: (0, 0))

    def norm(x, w):
        return x * lax.rsqrt(jnp.mean(x * x, axis=-1, keepdims=True) + NORM_EPS) * w

    def body(xp_ref, y_ref, mtg_ref, fnw_ref, tgt_ref, loss_ref, dx_ref, dy_ref, dg_ref, dfnw_ref):
        i = pl.program_id(0)
        g = mtg_ref[:, gi * d:(gi + 1) * d]
        x = xp_ref[...] + _pat(y_ref[...], g, jnp.multiply)
        out, vjp = jax.vjp(norm, x, fnw_ref[...])
        lat = i >= nct
        err = jnp.where(lat, out - tgt_ref[...], 0.0)
        dx, dfnw = vjp(err * (1.0 / d))

        @pl.when(i == 0)
        def _():
            loss_ref[...] = jnp.zeros_like(loss_ref)
            dfnw_ref[...] = jnp.zeros_like(dfnw_ref)

        @pl.when((i == 0) | (i == nct))
        def _():
            dg_ref[...] = jnp.zeros_like(dg_ref)

        loss_ref[...] += jnp.full(loss_ref.shape, 0.5 / d * jnp.sum(err * err), F32)
        dfnw_ref[...] += dfnw
        dx_ref[...] = dx
        dy_ref[...] = _pat(dx, g, jnp.multiply).astype(MXU)
        dg_ref[...] += jnp.sum((dx * y_ref[...]).reshape(tm // 8, 8, d), axis=0)

    return pl.pallas_call(
        body, name=name, grid=(r // tm,),
        in_specs=[row, row, _mt_spec(d, nct), vec, pl.BlockSpec((tm, d), lambda i: (jnp.maximum(i - nct, 0), 0))],
        out_specs=(pl.BlockSpec((8, 128), lambda i: (0, 0)), row, row, _acc_spec(d, nct), vec),
        out_shape=(jax.ShapeDtypeStruct((8, 128), F32), jax.ShapeDtypeStruct((r, d), F32), jax.ShapeDtypeStruct((r, d), MXU),
                   jax.ShapeDtypeStruct((16, d), F32), jax.ShapeDtypeStruct((1, d), F32)),
        compiler_params=_cp(("arbitrary",)))(xp, y, mtg, fnw, tgt)


def _silu(x):
    return x * jax.nn.sigmoid(x)


def mod_fwd(c16, w, b, *, name):
    d, n = w.shape
    tn = _tile(n, 1536)

    def body(c_ref, w_ref, b_ref, o_ref, s_ref):
        s = _silu(c_ref[...])
        s_ref[...] = s.astype(MXU)
        o_ref[...] = _dot(s, w_ref[...]) + b_ref[...]

    return pl.pallas_call(
        body, name=name, grid=(n // tn,),
        in_specs=[pl.BlockSpec((16, d), lambda j: (0, 0)), pl.BlockSpec((d, tn), lambda j: (0, j)), pl.BlockSpec((1, tn), lambda j: (0, j))],
        out_specs=(pl.BlockSpec((16, tn), lambda j: (0, j)), pl.BlockSpec((16, d), lambda j: (0, 0))),
        out_shape=(jax.ShapeDtypeStruct((16, n), F32), jax.ShapeDtypeStruct((16, d), MXU)),
        compiler_params=_cp(("arbitrary",)))(c16, w, b)


def colsum(x, *, name):
    def body(x_ref, o_ref):
        o_ref[...] = jnp.sum(x_ref[...], axis=0, keepdims=True)

    return pl.pallas_call(body, name=name, out_shape=jax.ShapeDtypeStruct((1, x.shape[1]), F32))(x)


def cctx_grad(c16, ds_list, *, name):
    def body(c_ref, *refs):
        o_ref = refs[-1]
        ds = refs[0][...]
        for r_ in refs[1:-1]:
            ds = ds + r_[...]
        _, vjp = jax.vjp(_silu, c_ref[...])
        (dc,) = vjp(ds)
        o_ref[...] = jnp.sum(dc[0:8], axis=0, keepdims=True)

    return pl.pallas_call(body, name=name, out_shape=jax.ShapeDtypeStruct((1, c16.shape[1]), F32))(c16, *ds_list)


def _s5_disc(lam_re, lam_im, log_step, b_re, b_im):
    lr = jnp.minimum(lam_re, S5_LAM_RE_MAX)
    li = lam_im
    dt = jnp.exp(log_step)
    mag = jnp.exp(lr * dt)
    abar_r = mag * jnp.cos(li * dt)
    abar_i = mag * jnp.sin(li * dt)
    den = lr * lr + li * li
    nr = abar_r - 1.0
    coef_r = (nr * lr + abar_i * li) / den
    coef_i = (abar_i * lr - nr * li) / den
    bbar_r = coef_r[:, None, :] * b_re - coef_i[:, None, :] * b_im
    bbar_i = coef_r[:, None, :] * b_im + coef_i[:, None, :] * b_re
    return abar_r, abar_i, bbar_r, bbar_i


def s5_disc_fwd(lam_re, lam_im, log_step, b_re, b_im, *, name):
    def body(lr, li, ls, br, bi, ar_o, ai_o, br_o, bi_o):
        ar_o[...], ai_o[...], br_o[...], bi_o[...] = _s5_disc(lr[...], li[...], ls[...], br[...], bi[...])

    s2, s3 = jax.ShapeDtypeStruct(lam_re.shape, F32), jax.ShapeDtypeStruct(b_re.shape, F32)
    return pl.pallas_call(body, name=name, out_shape=(s2, s2, s3, s3))(lam_re, lam_im, log_step, b_re, b_im)


def s5_disc_bwd(lam_re, lam_im, log_step, b_re, b_im, d_ar, d_ai, d_br, d_bi, *, name):
    def body(lr, li, ls, br, bi, dar, dai, dbr, dbi, o_lr, o_li, o_ls, o_br, o_bi):
        _, vjp = jax.vjp(_s5_disc, lr[...], li[...], ls[...], br[...], bi[...])
        o_lr[...], o_li[...], o_ls[...], o_br[...], o_bi[...] = vjp((dar[...], dai[...], dbr[...], dbi[...]))

    s2, s3 = jax.ShapeDtypeStruct(lam_re.shape, F32), jax.ShapeDtypeStruct(b_re.shape, F32)
    return pl.pallas_call(body, name=name, out_shape=(s2, s2, jax.ShapeDtypeStruct(log_step.shape, F32), s3, s3))(
        lam_re, lam_im, log_step, b_re, b_im, d_ar, d_ai, d_br, d_bi)


S5_LANES = 512


def _chunk_order(k, ncc, nch, rev):
    if not rev:
        return k
    return jnp.where(k < ncc, ncc - 1 - k, nch - 1 - (k - ncc))


def _cmul(ar, ai, xr, xi):
    return ar * xr - ai * xi, ar * xi + ai * xr


S5_FWD_ROWS = 256
S5_BWD_ROWS = 256


def _const_spec(a):
    return pl.BlockSpec(a.shape, lambda k: (0,) * a.ndim, pipeline_mode=pl.Buffered(1))


def _shift_steps(x, edge_tile, back):
    n = x.shape[0]
    row = lax.broadcasted_iota(jnp.int32, (8, x.shape[1]), 0)
    edge = pltpu.roll(edge_tile, 4, 0)
    if back:
        y = pltpu.roll(x, 4, 0)
        return jnp.concatenate([jnp.where(row < 4, edge, y[0:8]), y[8:]], axis=0)
    y = pltpu.roll(x, n - 4, 0)
    return jnp.concatenate([y[:n - 8], jnp.where(row >= 4, edge, y[n - 8:])], axis=0)


def s5_scan_fwd(cfg, u, a2_re, a2_im, bre, bim, abre, abim, cre, cim, *, rev, name):
    r, d = u.shape
    ns = a2_re.shape[1]
    kb = d // S5_KIN
    tcr = S5_FWD_ROWS
    n8 = tcr // 8
    q = S5_FWD_ROWS // S5_BWD_ROWS
    seg = n8 // q
    nch, ncc = r // tcr, cfg["rc"] // tcr
    lw = min(S5_LANES, ns)

    def body(u_ref, ar_ref, ai_ref, bre_ref, bim_ref, abre_ref, abim_ref, cre_ref, cim_ref, sre_ref, sim_ref, ere_ref, eim_ref, y_ref,
             st_re, st_im, u_edge):
        @pl.when(pl.program_id(0) == 0)
        def _():
            st_re[...] = jnp.zeros_like(st_re)
            st_im[...] = jnp.zeros_like(st_im)
            u_edge[...] = jnp.zeros_like(u_edge)

        u_ = u_ref[...]
        ub = u_.astype(MXU)
        upb = _shift_steps(u_, u_edge[...], back=not rev).astype(MXU)
        u_edge[...] = u_[0:8] if rev else u_[tcr - 8:tcr]
        for j in range(kb):
            uj, upj = ub[:, j * S5_KIN:(j + 1) * S5_KIN], upb[:, j * S5_KIN:(j + 1) * S5_KIN]
            sre_ref[:, :, j * S5_KST:(j + 1) * S5_KST] = (_dot(uj, bre_ref[j]) + _dot(upj, abre_ref[j])).reshape(n8, 8, S5_KST)
            sim_ref[:, :, j * S5_KST:(j + 1) * S5_KST] = (_dot(uj, bim_ref[j]) + _dot(upj, abim_ref[j])).reshape(n8, 8, S5_KST)
        for c in range(ns // lw):
            sl = slice(c * lw, (c + 1) * lw)
            ar = jnp.broadcast_to(ar_ref[:, sl], (8, lw))
            ai = jnp.broadcast_to(ai_ref[:, sl], (8, lw))

            def step(i, carry, sl=sl, ar=ar, ai=ai):
                sr, si = carry
                ii = n8 - 1 - i if rev else i
                pr, pi = _cmul(ar, ai, sr, si)
                sr, si = pr + sre_ref[ii, :, sl], pi + sim_ref[ii, :, sl]
                sre_ref[ii, :, sl] = sr
                sim_ref[ii, :, sl] = si
                return sr, si

            sr, si = st_re[:, sl], st_im[:, sl]
            for s_ in range(q):
                at = q - 1 - s_ if rev else s_
                ere_ref[at, :, sl] = sr
                eim_ref[at, :, sl] = si
                sr, si = lax.fori_loop(s_ * seg, (s_ + 1) * seg, step, (sr, si))
            st_re[:, sl] = sr
            st_im[:, sl] = si
        for j in range(kb):
            sr = sre_ref[:, :, j * S5_KST:(j + 1) * S5_KST].reshape(tcr, S5_KST)
            si = sim_ref[:, :, j * S5_KST:(j + 1) * S5_KST].reshape(tcr, S5_KST)
            y_ref[:, j * S5_KIN:(j + 1) * S5_KIN] = _dot(sr, cre_ref[j]) - _dot(si, cim_ref[j])

    cidx = functools.partial(_chunk_order, ncc=ncc, nch=nch, rev=rev)
    full = _const_spec
    st = pl.BlockSpec((n8, 8, ns), lambda k: (cidx(k), 0, 0))
    en = pl.BlockSpec((q, 8, ns), lambda k: (cidx(k), 0, 0))
    return pl.pallas_call(
        body, name=name, grid=(nch,),
        in_specs=[pl.BlockSpec((tcr, d), lambda k: (cidx(k), 0)), full(a2_re), full(a2_im), full(bre), full(bim), full(abre), full(abim),
                  full(cre), full(cim)],
        out_specs=(st, st, en, en, pl.BlockSpec((tcr, d), lambda k: (cidx(k), 0))),
        out_shape=(jax.ShapeDtypeStruct((r // 8, 8, ns), F32),) * 2 + (jax.ShapeDtypeStruct((q * nch, 8, ns), F32),) * 2
        + (jax.ShapeDtypeStruct((r, d), F32),),
        scratch_shapes=[pltpu.VMEM((8, ns), F32), pltpu.VMEM((8, ns), F32), pltpu.VMEM((8, d), F32)],
        compiler_params=_cp(("arbitrary",)))(u, a2_re, a2_im, bre, bim, abre, abim, cre, cim)


def s5_scan_bwd(cfg, dyb, sre, sim, ere, eim, a2_re, a2_im, bre, bim, cre, cim, c2re, c2im, du_in, *, rev, name):
    r, d = dyb.shape
    ns = a2_re.shape[1]
    kb = d // S5_KIN
    tcr = S5_BWD_ROWS
    n8 = tcr // 8
    nch, ncc = r // tcr, cfg["rc"] // tcr
    lw = min(S5_LANES, ns)

    def body(dy_ref, sre_ref, sim_ref, ere_ref, eim_ref, ar_ref, ai_ref, bre_ref, bim_ref, cre_ref, cim_ref, c2re_ref, c2im_ref, duin_ref,
             du_ref, gre_ref, gim_ref, dar_ref, dai_ref, g_re, g_im, gc_re, gc_im, dy_edge):
        k = pl.program_id(0)

        @pl.when(k == 0)
        def _():
            gc_re[...] = jnp.zeros_like(gc_re)
            gc_im[...] = jnp.zeros_like(gc_im)
            dar_ref[...] = jnp.zeros_like(dar_ref)
            dai_ref[...] = jnp.zeros_like(dai_ref)
            dy_edge[...] = jnp.zeros_like(dy_edge)

        dy32 = dy_ref[...].astype(F32)
        dy = dy32.astype(MXU)
        dyn = _shift_steps(dy32, dy_edge[...], back=rev).astype(MXU)
        dy_edge[...] = dy32[tcr - 8:tcr] if rev else dy32[0:8]
        for j in range(kb):
            dyj, dynj = dy[:, j * S5_KIN:(j + 1) * S5_KIN], dyn[:, j * S5_KIN:(j + 1) * S5_KIN]
            g_re[:, :, j * S5_KST:(j + 1) * S5_KST] = (_dot(dyj, cre_ref[j], 1, 1) + _dot(dynj, c2re_ref[j], 1, 1)).reshape(n8, 8, S5_KST)
            g_im[:, :, j * S5_KST:(j + 1) * S5_KST] = -(_dot(dyj, cim_ref[j], 1, 1) + _dot(dynj, c2im_ref[j], 1, 1)).reshape(n8, 8, S5_KST)
        first = lax.broadcasted_iota(jnp.int32, (8, lw), 0) < 4
        if rev:
            first = jnp.logical_not(first)
        for c in range(ns // lw):
            sl = slice(c * lw, (c + 1) * lw)
            ar = jnp.broadcast_to(ar_ref[:, sl], (8, lw))
            nai = -jnp.broadcast_to(ai_ref[:, sl], (8, lw))

            def step(i, carry, sl=sl, ar=ar, nai=nai):
                gr, gi, accr, acci = carry
                ii = i if rev else n8 - 1 - i
                pr, pi = _cmul(ar, nai, gr, gi)
                outr, outi = pr + g_re[ii, :, sl], pi + g_im[ii, :, sl]
                g_re[ii, :, sl] = outr
                g_im[ii, :, sl] = outi
                pv = jnp.clip(ii + 1 if rev else ii - 1, 0, n8 - 1)
                at_entry = (ii == n8 - 1) if rev else (ii == 0)
                pvr = jnp.where(at_entry, ere_ref[0, :, sl], sre_ref[pv, :, sl])
                pvi = jnp.where(at_entry, eim_ref[0, :, sl], sim_ref[pv, :, sl])
                spr = pltpu.roll(jnp.where(first, sre_ref[ii, :, sl], pvr), 4, 0)
                spi = pltpu.roll(jnp.where(first, sim_ref[ii, :, sl], pvi), 4, 0)
                accr = accr + outr * spr + outi * spi
                acci = acci + outi * spr - outr * spi
                return outr, outi, accr, acci

            gr, gi, accr, acci = lax.fori_loop(0, n8, step, (gc_re[:, sl], gc_im[:, sl], dar_ref[:, sl], dai_ref[:, sl]))
            gc_re[:, sl] = gr
            gc_im[:, sl] = gi
            dar_ref[:, sl] = accr
            dai_ref[:, sl] = acci
        for j in range(kb):
            gr = g_re[:, :, j * S5_KST:(j + 1) * S5_KST].reshape(tcr, S5_KST)
            gi = g_im[:, :, j * S5_KST:(j + 1) * S5_KST].reshape(tcr, S5_KST)
            gre_ref[:, j * S5_KST:(j + 1) * S5_KST] = gr.astype(MXU)
            gim_ref[:, j * S5_KST:(j + 1) * S5_KST] = gi.astype(MXU)
            du_ref[:, j * S5_KIN:(j + 1) * S5_KIN] = (duin_ref[:, j * S5_KIN:(j + 1) * S5_KIN]
                                                     + _dot(gr, bre_ref[j], 1, 1) + _dot(gi, bim_ref[j], 1, 1))

    def cidx(k):
        return _chunk_order(nch - 1 - k, ncc, nch, rev)

    full = _const_spec
    st = pl.BlockSpec((n8, 8, ns), lambda k: (cidx(k), 0, 0))
    en = pl.BlockSpec((1, 8, ns), lambda k: (cidx(k), 0, 0))
    rowd = pl.BlockSpec((tcr, d), lambda k: (cidx(k), 0))
    rown = pl.BlockSpec((tcr, ns), lambda k: (cidx(k), 0))
    acc = pl.BlockSpec((8, ns), lambda k: (0, 0))
    return pl.pallas_call(
        body, name=name, grid=(nch,),
        in_specs=[rowd, st, st, en, en, full(a2_re), full(a2_im), full(bre), full(bim), full(cre), full(cim), full(c2re), full(c2im), rowd],
        out_specs=(rowd, rown, rown, acc, acc),
        out_shape=(jax.ShapeDtypeStruct((r, d), F32), jax.ShapeDtypeStruct((r, ns), MXU), jax.ShapeDtypeStruct((r, ns), MXU),
                   jax.ShapeDtypeStruct((8, ns), F32), jax.ShapeDtypeStruct((8, ns), F32)),
        scratch_shapes=[pltpu.VMEM((n8, 8, ns), F32), pltpu.VMEM((n8, 8, ns), F32), pltpu.VMEM((8, ns), F32), pltpu.VMEM((8, ns), F32),
                        pltpu.VMEM((8, d), F32)],
        compiler_params=_cp(("arbitrary",)))(dyb, sre, sim, ere, eim, a2_re, a2_im, bre, bim, cre, cim, c2re, c2im, du_in)


def rowmap(fn, rows_in, vecs_in, outs, accs=(), *, name):
    r = rows_in[0].shape[0]
    tm = _row_tile(r, max(a.shape[1] for a in rows_in))
    nr, nv, no = len(rows_in), len(vecs_in), len(outs)

    def body(*refs):
        ins = [x[...] for x in refs[:nr + nv]]
        res = fn(*ins)
        if not isinstance(res, (tuple, list)):
            res = (res,)
        out_refs = refs[nr + nv:]
        for o_ref, v in zip(out_refs[:no], res[:no]):
            o_ref[...] = v.astype(o_ref.dtype)
        if accs:
            @pl.when(pl.program_id(0) == 0)
            def _():
                for a_ref in out_refs[no:]:
                    a_ref[...] = jnp.zeros_like(a_ref)
            for a_ref, v in zip(out_refs[no:], res[no:]):
                a_ref[...] += v

    in_specs = [pl.BlockSpec((tm, a.shape[1]), lambda i: (i, 0)) for a in rows_in]
    in_specs += [pl.BlockSpec(v.shape, lambda i, n=v.ndim: (0,) * n) for v in vecs_in]
    out_specs = [pl.BlockSpec((tm, w), lambda i: (i, 0)) for w, _ in outs] + [pl.BlockSpec(s, lambda i, n=len(s): (0,) * n) for s in accs]
    out_shape = [jax.ShapeDtypeStruct((r, w), dt) for w, dt in outs] + [jax.ShapeDtypeStruct(s, F32) for s in accs]
    res = pl.pallas_call(body, name=name, grid=(r // tm,), in_specs=in_specs, out_specs=tuple(out_specs), out_shape=tuple(out_shape),
                         compiler_params=_cp(("arbitrary",) if accs else ("parallel",)))(*rows_in, *vecs_in)
    return res


def _gelu(x):
    return jax.nn.gelu(x, approximate=True)


def _hg_lower_bound(e0, e1):
    m = jnp.maximum(e0, e1)
    a, b = jnp.exp(e0 - m), jnp.exp(e1 - m)
    return b / (a + b)


def _hg_gates(x, lb):
    logf = jnp.log(lb + (1.0 - lb) * jax.nn.sigmoid(x))
    return logf, (1.0 - lb) * jax.nn.sigmoid(-x)


def _hg_masks(rev):
    n = CHUNK_ROWS
    rr = lax.broadcasted_iota(jnp.int32, (n, n), 0)
    ss = lax.broadcasted_iota(jnp.int32, (n, n), 1)
    same = (rr % NB) == (ss % NB)
    causal = same & ((ss >= rr) if rev else (ss <= rr))
    anti = same & ((ss <= rr) if rev else (ss >= rr))
    end0 = 0 if rev else n - NB
    pick_end = ss == (end0 + rr % NB)
    return same, causal, anti, pick_end, end0


def _hg_expand(x):
    ex = lax.broadcasted_iota(jnp.int32, x.shape, 0) % NB
    return jnp.concatenate([jnp.where(ex == b, x, 0.0) for b in range(NB)], axis=1)


def _hg_fold(xe):
    kk = xe.shape[1] // NB
    ex = lax.broadcasted_iota(jnp.int32, (xe.shape[0], kk), 0) % NB
    out = jnp.zeros((xe.shape[0], kk), F32)
    for b in range(NB):
        out = out + jnp.where(ex == b, xe[:, b * kk:(b + 1) * kk], 0.0)
    return out


def _hg_chunk(q, v, x, lb, masks):
    same, causal, anti, pick_end, end0 = masks
    logf, kk = _hg_gates(x, lb)
    b = _dot3(causal.astype(MXU), logf)
    bend_t = _dot3(pick_end.astype(MXU), b)
    bend_flat = jnp.concatenate([b[end0 + i:end0 + i + 1] for i in range(NB)], axis=1)
    eb = jnp.exp(b)
    enb = jnp.exp(-b)
    ee = jnp.exp(bend_t - b)
    qd, kd, ke = q * eb, kk * enb, kk * ee
    att = jnp.where(causal, _dot(qd, kd, 1, 1), 0.0)
    decay = jnp.exp(bend_flat)
    return dict(same=same, causal=causal, anti=anti, logf=logf, kk=kk, b=b, eb=eb, enb=enb, ee=ee, qd=qd, kd=kd, ke=ke, att=att,
                decay=decay, qde=_hg_expand(qd), kee=_hg_expand(ke))


def _hg_chunk_order(cfg, r):
    nch, ncc = r // CHUNK_ROWS, cfg["rc"] // CHUNK_ROWS
    return nch, ncc


def hg_scan_fwd(cfg, z, lb, *, d_dir, name):
    r = z.shape[0]
    d = z.shape[1] // N_PROJ
    nh = d // HEAD
    rev = d_dir == 1
    nch, ncc = _hg_chunk_order(cfg, r)
    n = CHUNK_ROWS

    def body(q_ref, v_ref, x_ref, lb_ref, o_ref, sin_ref, stk):
        @pl.when(pl.program_id(0) == 0)
        def _():
            stk[...] = jnp.zeros_like(stk)

        masks = _hg_masks(rev)
        for h in range(nh):
            sl = slice(h * HEAD, (h + 1) * HEAD)
            s0 = stk[h]
            sin_ref[0, h] = s0
            v = v_ref[:, sl]
            c = _hg_chunk(q_ref[:, sl], v, x_ref[:, sl], lb_ref[:, sl], masks)
            o_ref[:, sl] = _dot(c["att"], v) + _dot(c["qde"], s0, 1, 1)
            stk[h] = s0 * c["decay"] + _dot(v, c["kee"], 0, 0)

    def cidx(k):
        return _chunk_order(k, ncc, nch, rev)

    blk = lambda p: pl.BlockSpec((n, d), lambda k: (cidx(k), p))
    return pl.pallas_call(
        body, name=name, grid=(nch,),
        in_specs=[blk(0), blk(1), blk(2 + d_dir), pl.BlockSpec((1, d), lambda k: (0, 0))],
        out_specs=(blk(0), pl.BlockSpec((1, nh, HEAD, NB * HEAD), lambda k: (cidx(k), 0, 0, 0))),
        out_shape=(jax.ShapeDtypeStruct((r, d), F32), jax.ShapeDtypeStruct((nch, nh, HEAD, NB * HEAD), F32)),
        scratch_shapes=[pltpu.VMEM((nh, HEAD, NB * HEAD), F32)], compiler_params=_cp(("arbitrary",)))(z, z, z, lb)


def hg_scan_bwd(cfg, do, z, lb, sin, dq_in, dv_in, *, d_dir, name):
    r = z.shape[0]
    d = z.shape[1] // N_PROJ
    nh = d // HEAD
    rev = d_dir == 1
    nch, ncc = _hg_chunk_order(cfg, r)
    n = CHUNK_ROWS
    has_in = dq_in is not None

    def body(*refs):
        if has_in:
            do_ref, q_ref, v_ref, x_ref, lb_ref, sin_ref, dqi_ref, dvi_ref, dq_ref, dv_ref, dx_ref, dlb_ref, dstk = refs
        else:
            do_ref, q_ref, v_ref, x_ref, lb_ref, sin_ref, dq_ref, dv_ref, dx_ref, dlb_ref, dstk = refs
        @pl.when(pl.program_id(0) == 0)
        def _():
            dstk[...] = jnp.zeros_like(dstk)
            dlb_ref[...] = jnp.zeros_like(dlb_ref)

        masks = _hg_masks(rev)
        ex = lax.broadcasted_iota(jnp.int32, (n, HEAD), 0) % NB
        for h in range(nh):
            sl = slice(h * HEAD, (h + 1) * HEAD)
            do_, q, v, x, lb_, s0, ds1 = do_ref[:, sl], q_ref[:, sl], v_ref[:, sl], x_ref[:, sl], lb_ref[:, sl], sin_ref[0, h], dstk[h]
            c = _hg_chunk(q, v, x, lb_, masks)
            datt = jnp.where(c["causal"], _dot(do_, v, 1, 1), 0.0)
            dv = _dot(c["att"], do_, 0, 0) + _dot(c["kee"], ds1, 1, 1)
            dqd = _dot(datt, c["kd"]) + _hg_fold(_dot(do_, s0))
            dkd = _dot(datt, c["qd"], 0, 0)
            dke = _hg_fold(_dot(v, ds1))
            dbend_flat = jnp.sum(ds1 * s0, axis=0, keepdims=True) * c["decay"]
            dstk[h] = _dot(do_, c["qde"], 0, 0) + ds1 * c["decay"]
            dq = dqd * c["eb"]
            dk = dkd * c["enb"] + dke * c["ee"]
            db = dqd * c["qd"] - dkd * c["kd"] - dke * c["ke"]
            dbend_rows = jnp.zeros((n, HEAD), F32)
            for b in range(NB):
                dbend_rows = dbend_rows + jnp.where(ex == b, dbend_flat[:, b * HEAD:(b + 1) * HEAD], 0.0)
            dlogf = _dot3(c["anti"].astype(MXU), db) + _dot3(c["same"].astype(MXU), dke * c["ke"]) + dbend_rows
            _, vjp = jax.vjp(_hg_gates, x, lb_)
            dx, dlb = vjp((dlogf, dk))
            if has_in:
                dq = dq + dqi_ref[:, sl]
                dv = dv + dvi_ref[:, sl]
            dq_ref[:, sl] = dq
            dv_ref[:, sl] = dv
            dx_ref[:, sl] = dx
            dlb_ref[:, sl] += dlb

    def cidx(k):
        return _chunk_order(nch - 1 - k, ncc, nch, rev)

    blk = lambda p: pl.BlockSpec((n, d), lambda k: (cidx(k), p))
    vec = pl.BlockSpec((1, d), lambda k: (0, 0))
    in_specs = [blk(0), blk(0), blk(1), blk(2 + d_dir), vec, pl.BlockSpec((1, nh, HEAD, NB * HEAD), lambda k: (cidx(k), 0, 0, 0))]
    args = [do, z, z, z, lb, sin]
    if has_in:
        in_specs += [blk(0), blk(0)]
        args += [dq_in, dv_in]
    rd = jax.ShapeDtypeStruct((r, d), F32)
    return pl.pallas_call(
        body, name=name, grid=(nch,), in_specs=in_specs, out_specs=(blk(0), blk(0), blk(0), vec),
        out_shape=(rd, rd, rd, jax.ShapeDtypeStruct((1, d), F32)),
        scratch_shapes=[pltpu.VMEM((nh, HEAD, NB * HEAD), F32)], compiler_params=_cp(("arbitrary",)))(*args)


def _hg_read(o, g, gw):
    on = o * lax.rsqrt(jnp.mean(o * o, axis=-1, keepdims=True) + NORM_EPS) * gw
    return on * jax.nn.sigmoid(g)


def hg_read_fwd(of, ob, z, gw, *, name):
    r, d = of.shape
    nh = d // HEAD
    tm = _row_tile(r)

    def body(of_ref, ob_ref, g_ref, gw_ref, o_ref):
        for h in range(nh):
            sl = slice(h * HEAD, (h + 1) * HEAD)
            o_ref[:, sl] = _hg_read(of_ref[:, sl] + ob_ref[:, sl], g_ref[:, sl], gw_ref[...]).astype(MXU)

    blk = pl.BlockSpec((tm, d), lambda i: (i, 0))
    return pl.pallas_call(
        body, name=name, grid=(r // tm,),
        in_specs=[blk, blk, pl.BlockSpec((tm, d), lambda i: (i, N_PROJ - 1)), pl.BlockSpec((1, HEAD), lambda i: (0, 0))],
        out_specs=blk, out_shape=jax.ShapeDtypeStruct((r, d), MXU), compiler_params=_cp(("parallel",)))(of, ob, z, gw)


def hg_read_bwd(don, of, ob, z, gw, *, name):
    r, d = of.shape
    nh = d // HEAD
    tm = _row_tile(r)

    def body(don_ref, of_ref, ob_ref, g_ref, gw_ref, do_ref, dg_ref, dgw_ref):
        @pl.when(pl.program_id(0) == 0)
        def _():
            dgw_ref[...] = jnp.zeros_like(dgw_ref)

        for h in range(nh):
            sl = slice(h * HEAD, (h + 1) * HEAD)
            _, vjp = jax.vjp(_hg_read, of_ref[:, sl] + ob_ref[:, sl], g_ref[:, sl], gw_ref[...])
            do_ref[:, sl], dg_ref[:, sl], dgw = vjp(don_ref[:, sl])
            dgw_ref[...] += dgw

    blk = pl.BlockSpec((tm, d), lambda i: (i, 0))
    vec = pl.BlockSpec((1, HEAD), lambda i: (0, 0))
    rd = jax.ShapeDtypeStruct((r, d), F32)
    return pl.pallas_call(
        body, name=name, grid=(r // tm,),
        in_specs=[blk, blk, blk, pl.BlockSpec((tm, d), lambda i: (i, N_PROJ - 1)), vec],
        out_specs=(blk, blk, vec), out_shape=(rd, rd, jax.ShapeDtypeStruct((1, HEAD), F32)),
        compiler_params=_cp(("arbitrary",)))(don, of, ob, z, gw)


FFN_COLS = 256


def _seg_masks(cfg, tr, i):
    t = lax.broadcasted_iota(jnp.int32, (tr, FFN_COLS), 0) // NB
    ctx_steps = cfg["rc"] // NB
    pos = jnp.where(i == 0, t % ctx_steps, t % GRID_W)
    last = jnp.where(i == 0, ctx_steps - 1, GRID_W - 1)
    return pos == 0, pos == last


def _prev(x, start):
    return jnp.where(start, 0.0, pltpu.roll(x, NB, 0))


def _next(x, end):
    return jnp.where(end, 0.0, pltpu.roll(x, x.shape[0] - NB, 0))


def _conv3(u, w, b, start, end):
    return ((b + _prev(u, start) * w[0:1]) + u * w[1:2]) + _next(u, end) * w[2:3]


def ffn_mid_fwd(cfg, u, cw, cb, *, name):
    r, f2 = u.shape
    f = f2 // 2
    tr = cfg["rc"]
    nf = f // FFN_COLS

    def body(ua_ref, ug_ref, wa_ref, wg_ref, ba_ref, bg_ref, o_ref, ca_ref, cg_ref):
        start, end = _seg_masks(cfg, tr, pl.program_id(0))
        a = _conv3(ua_ref[...], wa_ref[...], ba_ref[...], start, end)
        g = _conv3(ug_ref[...], wg_ref[...], bg_ref[...], start, end)
        ca_ref[...] = a
        cg_ref[...] = g
        o_ref[...] = (_silu(a) * g).astype(MXU)

    ca = lambda rows: pl.BlockSpec((rows, FFN_COLS), lambda i, j: (i if rows == tr else 0, j))
    cg = lambda rows: pl.BlockSpec((rows, FFN_COLS), lambda i, j: (i if rows == tr else 0, j + nf))
    half = jax.ShapeDtypeStruct((r, f), F32)
    return pl.pallas_call(
        body, name=name, grid=(r // tr, nf), in_specs=[ca(tr), cg(tr), ca(3), cg(3), ca(1), cg(1)], out_specs=(ca(tr), ca(tr), ca(tr)),
        out_shape=(jax.ShapeDtypeStruct((r, f), MXU), half, half), compiler_params=_cp(("parallel", "parallel")))(u, u, cw, cw, cb, cb)


def ffn_mid_bwd(cfg, dact, u, ca, cg, cw, *, name):
    r, f2 = u.shape
    f = f2 // 2
    tr = cfg["rc"]
    nf = f // FFN_COLS

    def body(da_ref, us_ref, ca_ref, cg_ref, ws_ref, du_ref, dcw_ref, dcb_ref):
        i = pl.program_id(1)
        is_a = pl.program_id(0) < nf
        start, end = _seg_masks(cfg, tr, i)

        @pl.when(i == 0)
        def _():
            dcw_ref[...] = jnp.zeros_like(dcw_ref)
            dcb_ref[...] = jnp.zeros_like(dcb_ref)

        def finish(dc):
            us, ws = us_ref[...], ws_ref[...]
            dn, dp = _next(dc, end), _prev(dc, start)
            du_ref[...] = (ws[1:2] * dc + ws[0:1] * dn + ws[2:3] * dp).astype(MXU)
            dcw_ref[...] += jnp.concatenate([jnp.sum(dn * us, axis=0, keepdims=True), jnp.sum(dc * us, axis=0, keepdims=True),
                                             jnp.sum(dp * us, axis=0, keepdims=True)], axis=0)
            dcb_ref[...] += jnp.sum(dc, axis=0, keepdims=True)

        @pl.when(is_a)
        def _():
            cs = ca_ref[...]
            sg = jax.nn.sigmoid(cs)
            finish(da_ref[...] * cg_ref[...] * (sg * (1.0 + cs * (1.0 - sg))))

        @pl.when(jnp.logical_not(is_a))
        def _():
            finish(da_ref[...] * _silu(ca_ref[...]))

    cs_ = lambda rows: pl.BlockSpec((rows, FFN_COLS), lambda j, i: (i if rows == tr else 0, j))
    hf = pl.BlockSpec((tr, FFN_COLS), lambda j, i: (i, j % nf))
    gate = pl.BlockSpec((tr, FFN_COLS), lambda j, i: (jnp.where(j < nf, i, 0), jnp.where(j < nf, j, 0)))
    return pl.pallas_call(
        body, name=name, grid=(2 * nf, r // tr), in_specs=[hf, cs_(tr), hf, gate, cs_(3)], out_specs=(cs_(tr), cs_(3), cs_(1)),
        out_shape=(jax.ShapeDtypeStruct((r, f2), MXU), jax.ShapeDtypeStruct((3, f2), F32), jax.ShapeDtypeStruct((1, f2), F32)),
        compiler_params=_cp(("parallel", "arbitrary")))(dact, u, ca, cg, cw)


def hg_lb_fwd(e0, e1, *, name):
    def body(a, b, o):
        o[...] = _hg_lower_bound(a[...], b[...])

    return pl.pallas_call(body, name=name, out_shape=jax.ShapeDtypeStruct(e0.shape, F32))(e0, e1)


def hg_lb_bwd(e0, e1, dlb, *, name):
    def body(a, b, g, oa, ob):
        _, vjp = jax.vjp(_hg_lower_bound, a[...], b[...])
        oa[...], ob[...] = vjp(g[...])

    s = jax.ShapeDtypeStruct(e0.shape, F32)
    return pl.pallas_call(body, name=name, out_shape=(s, s))(e0, e1, dlb)


def _adamw(w, g, m, v):
    m = ADAM_B1 * m + (1.0 - ADAM_B1) * g
    v = ADAM_B2 * v + (1.0 - ADAM_B2) * jnp.square(g)
    m_hat = m / (1.0 - ADAM_B1 ** ADAM_STEP)
    v_hat = v / (1.0 - ADAM_B2 ** ADAM_STEP)
    delta = -ADAM_LR * (m_hat / (jnp.sqrt(v_hat) + ADAM_EPS) + ADAM_WD * w)
    return delta, m, v


def _as2d(a):
    if a.ndim >= 2 and a.shape[-1] % 128 == 0:
        return a.reshape(-1, a.shape[-1])
    return a.reshape(-1, 128) if a.size % 128 == 0 else a.reshape(1, -1)


def adamw(w, g, m, v, *, name):
    w2 = _as2d(w)
    outs = rowmap(_adamw, [w2, _as2d(g), _as2d(m), _as2d(v)], [], [(w2.shape[1], F32)] * 3, name=name)
    return tuple(o.reshape(w.shape) for o in outs)


HBM_SPEC = pl.BlockSpec(memory_space=pltpu.HBM)


def _place():
    mx, my, mc = lax.axis_index("x"), lax.axis_index("y"), lax.axis_index("c")
    others = [(1 - mx, my), (mx, 1 - my), (1 - mx, 1 - my)]
    return mx, my, mc, others


def chip_allgather(x, *, name):
    def body(x_ref, o_ref, send_sems, recv_sems, local_sem):
        mx, my, mc, others = _place()
        me = 2 * mx + my
        mine = pltpu.make_async_copy(x_ref, o_ref.at[me], local_sem)
        mine.start()
        sends = [pltpu.make_async_remote_copy(src_ref=x_ref, dst_ref=o_ref.at[me], send_sem=send_sems.at[j], recv_sem=recv_sems.at[j],
                                              device_id=(px, py, mc), device_id_type=MESH) for j, (px, py) in enumerate(others)]
        for cp in sends:
            cp.start()
        for j, (px, py) in enumerate(others):
            pltpu.make_async_remote_copy(src_ref=x_ref, dst_ref=o_ref.at[2 * px + py], send_sem=send_sems.at[j], recv_sem=recv_sems.at[j],
                                         device_id=(px, py, mc), device_id_type=MESH).wait_recv()
        for cp in sends:
            cp.wait_send()
        mine.wait()

    return pl.pallas_call(
        body, name=name, out_shape=jax.ShapeDtypeStruct((4,) + x.shape, x.dtype), in_specs=[HBM_SPEC], out_specs=HBM_SPEC,
        scratch_shapes=[pltpu.SemaphoreType.DMA((3,)), pltpu.SemaphoreType.DMA((3,)), pltpu.SemaphoreType.DMA])(x)


def _win(ref, axis, start, size):
    idx = [slice(None)] * len(ref.shape)
    idx[axis] = pl.ds(start, size)
    return ref.at[tuple(idx)]


def _half_axis(shape, ax):
    if shape[0] == 2:
        return 0
    return 2 if ax == 1 else 1


def _cut(shape, axis, parts):
    return shape[:axis] + (shape[axis] // parts,) + shape[axis + 1:]


def _hbm_call(body, arrays, out_shapes, sems, name):
    n_in = len(arrays)
    return pl.pallas_call(body, name=name, out_shape=tuple(out_shapes), in_specs=[HBM_SPEC] * n_in, out_specs=tuple([HBM_SPEC] * len(out_shapes)),
                          scratch_shapes=sems)(*arrays)


def place_shard(shard, ax, chip, dtype, *, name):
    l, r, c = shard.shape
    tr = _row_tile(r, c)
    per_block = (l, r // tr, 1)[ax]

    def omap(li, ri, cref):
        idx = [li, ri, 0]
        idx[ax] = idx[ax] + cref[0] * per_block
        return tuple(idx)

    def body(c_ref, s_ref, o_ref):
        o_ref[...] = s_ref[...].astype(dtype)

    full = shard.shape[:ax] + (4 * shard.shape[ax],) + shard.shape[ax + 1:]
    return pl.pallas_call(
        body, name=name, out_shape=jax.ShapeDtypeStruct(full, dtype),
        grid_spec=pltpu.PrefetchScalarGridSpec(
            num_scalar_prefetch=1, grid=(l, r // tr),
            in_specs=[pl.BlockSpec((1, tr, c), lambda li, ri, cref: (li, ri, 0))], out_specs=pl.BlockSpec((1, tr, c), omap)),
        compiler_params=_cp(("parallel", "parallel")))(chip, shard)


def gather_placed(arrays, axes, haxes, *, name):
    n = len(arrays)

    def body(*refs):
        ins, outs = refs[:n], refs[n:2 * n]
        send_sems, recv_sems = refs[2 * n:]
        mx, my, mc, others = _place()
        me = 2 * mx + my

        def part(ref, i, chip):
            sz, hs = arrays[i].shape[axes[i]] // 4, arrays[i].shape[haxes[i]] // 2
            return _win(_win(ref, axes[i], chip * sz, sz), haxes[i], mc * hs, hs)

        sends = []
        for i in range(n):
            for j, (px, py) in enumerate(others):
                rc = pltpu.make_async_remote_copy(src_ref=part(ins[i], i, me), dst_ref=part(outs[i], i, me), send_sem=send_sems.at[i, j],
                                                  recv_sem=recv_sems.at[i, j], device_id=(px, py, mc), device_id_type=MESH)
                rc.start()
                sends.append(rc)
        for i in range(n):
            for j, (px, py) in enumerate(others):
                pltpu.make_async_remote_copy(src_ref=part(ins[i], i, me), dst_ref=part(outs[i], i, 2 * px + py), send_sem=send_sems.at[i, j],
                                             recv_sem=recv_sems.at[i, j], device_id=(px, py, mc), device_id_type=MESH).wait_recv()
        for rc in sends:
            rc.wait_send()

    return pl.pallas_call(
        body, name=name, out_shape=tuple(jax.ShapeDtypeStruct(a_.shape, a_.dtype) for a_ in arrays), in_specs=[HBM_SPEC] * n,
        out_specs=tuple([HBM_SPEC] * n), input_output_aliases={i: i for i in range(n)},
        scratch_shapes=[pltpu.SemaphoreType.DMA((n, 3)), pltpu.SemaphoreType.DMA((n, 3))])(*arrays)


SEM_SPEC = pl.BlockSpec(memory_space=pltpu.SEMAPHORE)
SPLIT_COPY = pltpu.CompilerParams(has_side_effects=pltpu.SideEffectType.DATAFLOW_SIDE_EFFECTING)


def _gather_part(ref, shape, ax, hax, chip, core):
    sz, hs = shape[ax] // 4, shape[hax] // 2
    return _win(_win(ref, ax, chip * sz, sz), hax, core * hs, hs)


def gather_placed_start(arrays, axes, haxes, after, *, name):
    n = len(arrays)

    m = 3 * n

    def body(*refs):
        ins, send_sems, recv_sems = refs[:n], refs[n + 1:n + 1 + m], refs[n + 1 + m:n + 1 + 2 * m]
        token = refs[2 * n + 1 + 2 * m]
        mx, my, mc, others = _place()
        me = 2 * mx + my
        for i in range(n):
            for j, (px, py) in enumerate(others):
                part = _gather_part(ins[i], arrays[i].shape, axes[i], haxes[i], me, mc)
                pltpu.make_async_remote_copy(src_ref=part, dst_ref=part, send_sem=send_sems[3 * i + j], recv_sem=recv_sems[3 * i + j],
                                             device_id=(px, py, mc), device_id_type=MESH).start()
        token[...] = jnp.zeros_like(token)

    hbm = [pltpu.with_memory_space_constraint(a_, pltpu.HBM) for a_ in arrays]
    out = pl.pallas_call(
        body, name=name,
        out_shape=tuple([pltpu.SemaphoreType.DMA(())] * (2 * m)) + tuple(pltpu.HBM(a_.shape, a_.dtype) for a_ in arrays)
        + (jax.ShapeDtypeStruct((8, 128), F32),),
        in_specs=[HBM_SPEC] * n + [pl.BlockSpec(memory_space=pl.ANY)],
        out_specs=tuple([SEM_SPEC] * (2 * m)) + tuple([HBM_SPEC] * n) + (pl.BlockSpec(memory_space=pltpu.VMEM),),
        input_output_aliases={i: 2 * m + i for i in range(n)}, compiler_params=SPLIT_COPY)(*hbm, after)
    return list(out[:m]), list(out[m:2 * m]), list(out[2 * m:2 * m + n]), out[2 * m + n]


def gather_placed_wait(arrays, send_sems, recv_sems, axes, haxes, after, *, name):
    n = len(arrays)

    m = 3 * n

    def body(*refs):
        ins, send_refs, recv_refs = refs[:n], refs[n:n + m], refs[n + m:n + 2 * m]
        mx, my, mc, others = _place()
        me = 2 * mx + my
        for i in range(n):
            for j, (px, py) in enumerate(others):
                cp = pltpu.make_async_remote_copy(
                    src_ref=_gather_part(ins[i], arrays[i].shape, axes[i], haxes[i], me, mc),
                    dst_ref=_gather_part(ins[i], arrays[i].shape, axes[i], haxes[i], 2 * px + py, mc),
                    send_sem=send_refs[3 * i + j], recv_sem=recv_refs[3 * i + j], device_id=(px, py, mc), device_id_type=MESH)
                cp.wait_send()
                cp.wait_recv()

    out = pl.pallas_call(
        body, name=name, out_shape=tuple(pltpu.HBM(a_.shape, a_.dtype) for a_ in arrays),
        in_specs=[HBM_SPEC] * n + [SEM_SPEC] * (2 * m) + [pl.BlockSpec(memory_space=pl.ANY)], out_specs=tuple([HBM_SPEC] * n),
        input_output_aliases={i: i for i in range(n)}, compiler_params=SPLIT_COPY)(*arrays, *send_sems, *recv_sems, after)
    return list(out)


def pair_swap_halves(arrays, haxes, *, name):
    n = len(arrays)

    def body(*refs):
        ins, outs = refs[:n], refs[n:2 * n]
        send_sems, recv_sems = refs[2 * n:]
        mx, my, mc, _ = _place()
        cps = []
        for i in range(n):
            hs = arrays[i].shape[haxes[i]] // 2
            cp = pltpu.make_async_remote_copy(src_ref=_win(ins[i], haxes[i], (1 - mc) * hs, hs), dst_ref=outs[i], send_sem=send_sems.at[i],
                                              recv_sem=recv_sems.at[i], device_id=(mx, my, 1 - mc), device_id_type=MESH)
            cp.start()
            cps.append(cp)
        for cp in cps:
            cp.wait()

    outs = [jax.ShapeDtypeStruct(_cut(a_.shape, h_, 2), a_.dtype) for a_, h_ in zip(arrays, haxes)]
    return _hbm_call(body, arrays, outs, [pltpu.SemaphoreType.DMA((n,)), pltpu.SemaphoreType.DMA((n,))], name)


def add_own_half(g, t, hax, core, *, out_dtype, name):
    l, r, c = t.shape
    tr = _row_tile(r, c)
    per_half = (l, r // tr, 1)[hax]

    def imap(li, ri, cref):
        idx = [li, ri, 0]
        idx[hax] = idx[hax] + cref[0] * per_half
        return tuple(idx)

    def body(c_ref, g_ref, t_ref, o_ref):
        o_ref[...] = (g_ref[...] + t_ref[...]).astype(out_dtype)

    return pl.pallas_call(
        body, name=name, out_shape=jax.ShapeDtypeStruct(t.shape, out_dtype),
        grid_spec=pltpu.PrefetchScalarGridSpec(
            num_scalar_prefetch=1, grid=(l, r // tr),
            in_specs=[pl.BlockSpec((1, tr, c), imap), pl.BlockSpec((1, tr, c), lambda li, ri, cref: (li, ri, 0))],
            out_specs=pl.BlockSpec((1, tr, c), lambda li, ri, cref: (li, ri, 0))),
        compiler_params=_cp(("parallel", "parallel")))(core, g, t)


def exchange_blocks(arrays, axes, *, name):
    n = len(arrays)

    def body(*refs):
        ins, outs = refs[:n], refs[n:2 * n]
        send_sems, recv_sems, local_sems = refs[2 * n:]
        mx, my, mc, others = _place()
        me = 2 * mx + my
        waits = []
        for i in range(n):
            sz = arrays[i].shape[axes[i]] // 4
            cp = pltpu.make_async_copy(_win(ins[i], axes[i], me * sz, sz), outs[i].at[me], local_sems.at[i])
            cp.start()
            waits.append(cp.wait)
            for j, (px, py) in enumerate(others):
                rc = pltpu.make_async_remote_copy(src_ref=_win(ins[i], axes[i], (2 * px + py) * sz, sz), dst_ref=outs[i].at[me],
                                                  send_sem=send_sems.at[i, j], recv_sem=recv_sems.at[i, j], device_id=(px, py, mc),
                                                  device_id_type=MESH)
                rc.start()
                waits.append(rc.wait_send)
        for i in range(n):
            sz = arrays[i].shape[axes[i]] // 4
            for j, (px, py) in enumerate(others):
                pltpu.make_async_remote_copy(src_ref=_win(ins[i], axes[i], me * sz, sz), dst_ref=outs[i].at[2 * px + py],
                                             send_sem=send_sems.at[i, j], recv_sem=recv_sems.at[i, j], device_id=(px, py, mc),
                                             device_id_type=MESH).wait_recv()
        for w_ in waits:
            w_()

    outs = [jax.ShapeDtypeStruct((4,) + _cut(a_.shape, ax, 4), a_.dtype) for a_, ax in zip(arrays, axes)]
    return _hbm_call(body, arrays, outs, [pltpu.SemaphoreType.DMA((n, 3)), pltpu.SemaphoreType.DMA((n, 3)), pltpu.SemaphoreType.DMA((n,))], name)


def exchange_blocks_start(arrays, axes, *, name):
    n = len(arrays)
    lands = [lax.empty((4,) + _cut(a_.shape, ax, 4), a_.dtype) for a_, ax in zip(arrays, axes)]

    def body(*refs):
        ins, lnd = refs[:n], refs[n:2 * n]
        send_sems, recv_sems = refs[2 * n:6 * n], refs[6 * n:9 * n]
        token = refs[11 * n]
        mx, my, mc, others = _place()
        me = 2 * mx + my
        for i in range(n):
            sz = arrays[i].shape[axes[i]] // 4
            pltpu.make_async_copy(_win(ins[i], axes[i], me * sz, sz), lnd[i].at[me], send_sems[4 * i + 3]).start()
            for j, (px, py) in enumerate(others):
                pltpu.make_async_remote_copy(src_ref=_win(ins[i], axes[i], (2 * px + py) * sz, sz), dst_ref=lnd[i].at[me],
                                             send_sem=send_sems[4 * i + j], recv_sem=recv_sems[3 * i + j], device_id=(px, py, mc),
                                             device_id_type=MESH).start()
        token[...] = jnp.zeros_like(token)

    hbm = [pltpu.with_memory_space_constraint(a_, pltpu.HBM) for a_ in arrays + lands]
    out = pl.pallas_call(
        body, name=name,
        out_shape=tuple([pltpu.SemaphoreType.DMA(())] * (7 * n)) + tuple(pltpu.HBM(a_.shape, a_.dtype) for a_ in arrays + lands)
        + (jax.ShapeDtypeStruct((8, 128), F32),),
        in_specs=[HBM_SPEC] * (2 * n),
        out_specs=tuple([SEM_SPEC] * (7 * n)) + tuple([HBM_SPEC] * (2 * n)) + (pl.BlockSpec(memory_space=pltpu.VMEM),),
        input_output_aliases={i: 7 * n + i for i in range(2 * n)}, compiler_params=SPLIT_COPY)(*hbm)
    return list(out[:7 * n]), list(out[7 * n:8 * n]), list(out[8 * n:9 * n]), out[9 * n]


def exchange_blocks_wait(sems, arrays, lands, axes, after, *, name):
    n = len(arrays)

    def body(*refs):
        ins, lnd = refs[:n], refs[n:2 * n]
        send_sems, recv_sems = refs[2 * n:6 * n], refs[6 * n:9 * n]
        mx, my, mc, others = _place()
        me = 2 * mx + my
        for i in range(n):
            sz = arrays[i].shape[axes[i]] // 4
            mine = _win(ins[i], axes[i], me * sz, sz)
            pltpu.make_async_copy(mine, lnd[i].at[me], send_sems[4 * i + 3]).wait()
            for j, (px, py) in enumerate(others):
                cp = pltpu.make_async_remote_copy(src_ref=mine, dst_ref=lnd[i].at[2 * px + py], send_sem=send_sems[4 * i + j],
                                                  recv_sem=recv_sems[3 * i + j], device_id=(px, py, mc), device_id_type=MESH)
                cp.wait_send()
                cp.wait_recv()

    out = pl.pallas_call(
        body, name=name, out_shape=tuple(pltpu.HBM(a_.shape, a_.dtype) for a_ in arrays + lands),
        in_specs=[HBM_SPEC] * (2 * n) + [SEM_SPEC] * (7 * n) + [pl.BlockSpec(memory_space=pl.ANY)], out_specs=tuple([HBM_SPEC] * (2 * n)),
        input_output_aliases={i: i for i in range(2 * n)}, compiler_params=SPLIT_COPY)(*arrays, *lands, *sems, after)
    return list(out[n:])


def sum_blocks(e, hax, core, *, name):
    _, l, r, c = e.shape
    tr = _row_tile(r, c)
    per_half = (l, r // tr, 1)[hax]

    def omap(li, ri, cref):
        idx = [li, ri, 0]
        idx[hax] = idx[hax] + cref[0] * per_half
        return tuple(idx)

    def body(c_ref, e_ref, o_ref):
        v = e_ref[...].astype(F32)
        o_ref[...] = ((v[0] + v[1]) + v[2]) + v[3]

    full = (l, r, c)[:hax] + (2 * (l, r, c)[hax],) + (l, r, c)[hax + 1:]
    return pl.pallas_call(
        body, name=name, out_shape=jax.ShapeDtypeStruct(full, F32),
        grid_spec=pltpu.PrefetchScalarGridSpec(
            num_scalar_prefetch=1, grid=(l, r // tr),
            in_specs=[pl.BlockSpec((4, 1, tr, c), lambda li, ri, cref: (0, li, ri, 0))], out_specs=pl.BlockSpec((1, tr, c), omap)),
        compiler_params=_cp(("parallel", "parallel")))(core, e)


def pair_fill_halves(arrays, haxes, *, name):
    n = len(arrays)

    def body(*refs):
        ins, outs = refs[:n], refs[n:2 * n]
        send_sems, recv_sems = refs[2 * n:]
        mx, my, mc, _ = _place()
        cps = []
        for i in range(n):
            hs = arrays[i].shape[haxes[i]] // 2
            mine = _win(ins[i], haxes[i], mc * hs, hs)
            cp = pltpu.make_async_remote_copy(src_ref=mine, dst_ref=_win(outs[i], haxes[i], mc * hs, hs), send_sem=send_sems.at[i],
                                              recv_sem=recv_sems.at[i], device_id=(mx, my, 1 - mc), device_id_type=MESH)
            cp.start()
            cps.append(cp)
        for i in range(n):
            hs = arrays[i].shape[haxes[i]] // 2
            pltpu.make_async_remote_copy(src_ref=_win(ins[i], haxes[i], mc * hs, hs), dst_ref=_win(outs[i], haxes[i], (1 - mc) * hs, hs),
                                         send_sem=send_sems.at[i], recv_sem=recv_sems.at[i], device_id=(mx, my, 1 - mc),
                                         device_id_type=MESH).wait_recv()
        for cp in cps:
            cp.wait_send()

    return pl.pallas_call(
        body, name=name, out_shape=tuple(jax.ShapeDtypeStruct(a_.shape, a_.dtype) for a_ in arrays), in_specs=[HBM_SPEC] * n,
        out_specs=tuple([HBM_SPEC] * n), input_output_aliases={i: i for i in range(n)},
        scratch_shapes=[pltpu.SemaphoreType.DMA((n,)), pltpu.SemaphoreType.DMA((n,))])(*arrays)


WEIGHTS = ['c_ctx', 'w_mod', 'b_mod', 'norm1_w', 'norm2_w', 'final_norm_w', 's5_w_in', 's5_lam_re', 's5_lam_im', 's5_log_step', 's5_b_re', 's5_b_im', 's5_c_re', 's5_c_im', 's5_d', 's5_w_glu', 's5_w_out', 'hg_w_in', 'hg_lower_bounds', 'hg_gnorm_w', 'hg_w_out', 'ffn_w_up', 'ffn_conv_w', 'ffn_conv_b', 'ffn_w_down']
INPUTS = ['x', 'c', 'ctx', 'c_ctx', 'w_mod', 'b_mod', 'norm1_w', 'norm2_w', 'final_norm_w', 's5_w_in', 's5_lam_re', 's5_lam_im', 's5_log_step', 's5_b_re', 's5_b_im', 's5_c_re', 's5_c_im', 's5_d', 's5_w_glu', 's5_w_out', 'hg_w_in', 'hg_lower_bounds', 'hg_gnorm_w', 'hg_w_out', 'ffn_w_up', 'ffn_conv_w', 'ffn_conv_b', 'ffn_w_down', 'loss_target', 'm_c_ctx', 'm_w_mod', 'm_b_mod', 'm_norm1_w', 'm_norm2_w', 'm_final_norm_w', 'm_s5_w_in', 'm_s5_lam_re', 'm_s5_lam_im', 'm_s5_log_step', 'm_s5_b_re', 'm_s5_b_im', 'm_s5_c_re', 'm_s5_c_im', 'm_s5_d', 'm_s5_w_glu', 'm_s5_w_out', 'm_hg_w_in', 'm_hg_lower_bounds', 'm_hg_gnorm_w', 'm_hg_w_out', 'm_ffn_w_up', 'm_ffn_conv_w', 'm_ffn_conv_b', 'm_ffn_w_down', 'v_c_ctx', 'v_w_mod', 'v_b_mod', 'v_norm1_w', 'v_norm2_w', 'v_final_norm_w', 'v_s5_w_in', 'v_s5_lam_re', 'v_s5_lam_im', 'v_s5_log_step', 'v_s5_b_re', 'v_s5_b_im', 'v_s5_c_re', 'v_s5_c_im', 'v_s5_d', 'v_s5_w_glu', 'v_s5_w_out', 'v_hg_w_in', 'v_hg_lower_bounds', 'v_hg_gnorm_w', 'v_hg_w_out', 'v_ffn_w_up', 'v_ffn_conv_w', 'v_ffn_conv_b', 'v_ffn_w_down']
SHARD_AXIS = {"w_mod": 2, "s5_w_in": 1, "s5_w_glu": 1, "s5_w_out": 1, "hg_w_in": 2, "hg_lower_bounds": 2, "hg_w_out": 1,
              "ffn_w_up": 2, "ffn_conv_w": 2, "ffn_w_down": 1}
GATHER_F32 = ("hg_lower_bounds", "ffn_conv_w")
PACK_W = 1024
GRAD_WIRE = jnp.bfloat16


def _reduce_start(items, core, tag):
    names, arrays, axes = [n for n, _, _ in items], [g_ for _, g_, _ in items], [ax for _, _, ax in items]
    haxes = [_half_axis(g_.shape, ax) for g_, ax in zip(arrays, axes)]
    t = pair_swap_halves(arrays, haxes, name="grad_pair_swap_" + tag)
    h = [add_own_half(g_, t_, hx, core, out_dtype=GRAD_WIRE, name="grad_pair_add_" + n) for g_, t_, hx, n in zip(arrays, t, haxes, names)]
    sems, h, lands, token = exchange_blocks_start(h, axes, name="grad_exchange_start_" + tag)
    return (names, sems, h, lands, axes, haxes), token


def _reduce_finish(state, core, after, tag):
    names, sems, h, lands, axes, haxes = state
    e = exchange_blocks_wait(sems, h, lands, axes, after, name="grad_exchange_wait_" + tag)
    s = [sum_blocks(e_, hx, core, name="grad_chip_sum_" + n) for e_, hx, n in zip(e, haxes, names)]
    return dict(zip(names, pair_fill_halves(s, haxes, name="grad_pair_fill_" + tag)))


def _reduce_now(a, items, small, grads, core):
    flat = jnp.concatenate([grads[n].reshape(-1) for n in small])
    pad = (-flat.shape[0]) % (64 * PACK_W)
    small_pack = jnp.pad(flat, (0, pad)).reshape(1, -1, PACK_W)
    names = [n for n, _, _ in items] + ["small"]
    arrays = [g_ for _, g_, _ in items] + [small_pack]
    axes = [ax for _, _, ax in items] + [1]
    haxes = [_half_axis(g_.shape, ax) for g_, ax in zip(arrays, axes)]
    t = pair_swap_halves(arrays, haxes, name="grad_pair_swap")
    h = [add_own_half(g_, t_, hx, core, out_dtype=GRAD_WIRE, name="grad_pair_add_" + n) for g_, t_, hx, n in zip(arrays, t, haxes, names)]
    e = exchange_blocks(h, axes, name="grad_chip_exchange")
    s = [sum_blocks(e_, hx, core, name="grad_chip_sum_" + n) for e_, hx, n in zip(e, haxes, names)]
    red = pair_fill_halves(s, haxes, name="grad_pair_fill")
    out = dict(zip(names[:-1], red[:-1]))
    sm = chip_allgather(red[-1][0], name="allgather_small_grads").reshape(-1)
    off = 0
    for n in small:
        out[n] = sm[off:off + math.prod(a[n].shape)].reshape(a[n].shape)
        off += math.prod(a[n].shape)
    return out


def _blockdiag_b(bb, kb):
    gl = S5_KIN // S5_GROUP
    x = bb.reshape(kb, gl, S5_GROUP, S5_STATE)
    return (x[:, :, :, None, :] * jnp.eye(gl, dtype=bb.dtype)[None, :, None, :, None]).reshape(kb, S5_KIN, S5_KST)


def _blockdiag_c(cc, kb):
    gl = S5_KIN // S5_GROUP
    x = cc.reshape(kb, gl, S5_GROUP, S5_STATE).transpose(0, 1, 3, 2)
    return (x[:, :, :, None, :] * jnp.eye(gl, dtype=cc.dtype)[None, :, None, :, None]).reshape(kb, S5_KST, S5_KIN)


def _diag_b(m, kb):
    gl = S5_KIN // S5_GROUP
    x = m.reshape(kb, gl, S5_GROUP, gl, S5_STATE)
    return jnp.stack([x[:, i, :, i, :] for i in range(gl)], axis=1).reshape(kb * gl, S5_GROUP, S5_STATE)


def _diag_c(m, kb):
    gl = S5_KIN // S5_GROUP
    x = m.reshape(kb, gl, S5_STATE, gl, S5_GROUP)
    return jnp.stack([x[:, i, :, i, :] for i in range(gl)], axis=1).transpose(0, 1, 3, 2).reshape(kb * gl, S5_GROUP, S5_STATE)


def kernel(x, c, ctx, c_ctx, w_mod, b_mod, norm1_w, norm2_w, final_norm_w, s5_w_in, s5_lam_re, s5_lam_im, s5_log_step, s5_b_re, s5_b_im, s5_c_re, s5_c_im, s5_d, s5_w_glu, s5_w_out, hg_w_in, hg_lower_bounds, hg_gnorm_w, hg_w_out, ffn_w_up, ffn_conv_w, ffn_conv_b, ffn_w_down, loss_target, m_c_ctx, m_w_mod, m_b_mod, m_norm1_w, m_norm2_w, m_final_norm_w, m_s5_w_in, m_s5_lam_re, m_s5_lam_im, m_s5_log_step, m_s5_b_re, m_s5_b_im, m_s5_c_re, m_s5_c_im, m_s5_d, m_s5_w_glu, m_s5_w_out, m_hg_w_in, m_hg_lower_bounds, m_hg_gnorm_w, m_hg_w_out, m_ffn_w_up, m_ffn_conv_w, m_ffn_conv_b, m_ffn_w_down, v_c_ctx, v_w_mod, v_b_mod, v_norm1_w, v_norm2_w, v_final_norm_w, v_s5_w_in, v_s5_lam_re, v_s5_lam_im, v_s5_log_step, v_s5_b_re, v_s5_b_im, v_s5_c_re, v_s5_c_im, v_s5_d, v_s5_w_glu, v_s5_w_out, v_hg_w_in, v_hg_lower_bounds, v_hg_gnorm_w, v_hg_w_out, v_ffn_w_up, v_ffn_conv_w, v_ffn_conv_b, v_ffn_w_down):
    a = dict(zip(INPUTS, (x, c, ctx, c_ctx, w_mod, b_mod, norm1_w, norm2_w, final_norm_w, s5_w_in, s5_lam_re, s5_lam_im, s5_log_step, s5_b_re, s5_b_im, s5_c_re, s5_c_im, s5_d, s5_w_glu, s5_w_out, hg_w_in, hg_lower_bounds, hg_gnorm_w, hg_w_out, ffn_w_up, ffn_conv_w, ffn_conv_b, ffn_w_down, loss_target, m_c_ctx, m_w_mod, m_b_mod, m_norm1_w, m_norm2_w, m_final_norm_w, m_s5_w_in, m_s5_lam_re, m_s5_lam_im, m_s5_log_step, m_s5_b_re, m_s5_b_im, m_s5_c_re, m_s5_c_im, m_s5_d, m_s5_w_glu, m_s5_w_out, m_hg_w_in, m_hg_lower_bounds, m_hg_gnorm_w, m_hg_w_out, m_ffn_w_up, m_ffn_conv_w, m_ffn_conv_b, m_ffn_w_down, v_c_ctx, v_w_mod, v_b_mod, v_norm1_w, v_norm2_w, v_final_norm_w, v_s5_w_in, v_s5_lam_re, v_s5_lam_im, v_s5_log_step, v_s5_b_re, v_s5_b_im, v_s5_c_re, v_s5_c_im, v_s5_d, v_s5_w_glu, v_s5_w_out, v_hg_w_in, v_hg_lower_bounds, v_hg_gnorm_w, v_hg_w_out, v_ffn_w_up, v_ffn_conv_w, v_ffn_conv_b, v_ffn_w_down)))
    nb, seq, d = x.shape
    assert nb == NB
    rc = nb * ctx.shape[1]
    cfg = {"rc": rc}
    f = a["ffn_w_down"].shape[1] * 4
    core = lax.axis_index("c").astype(jnp.int32).reshape(1)

    w = {n: a[n] for n in WEIGHTS if n not in SHARD_AXIS}
    chip = (2 * lax.axis_index("x") + lax.axis_index("y")).astype(jnp.int32).reshape(1)
    groups = {
        "now": [("w_mod0", a["w_mod"][0:1]), ("s5_w_in", a["s5_w_in"]), ("hg_lower_bounds", a["hg_lower_bounds"]), ("ffn_conv_w", a["ffn_conv_w"])],
        "mid": [("s5_w_glu", a["s5_w_glu"]), ("s5_w_out", a["s5_w_out"]), ("ffn_w_up0", a["ffn_w_up"][0:1]), ("ffn_w_down0", a["ffn_w_down"][0:1])],
        "later": [("w_mod1", a["w_mod"][1:2]), ("hg_w_in", a["hg_w_in"]), ("hg_w_out", a["hg_w_out"]), ("ffn_w_up1", a["ffn_w_up"][1:2]),
                  ("ffn_w_down1", a["ffn_w_down"][1:2])]}
    shard_axis = lambda n: SHARD_AXIS[n.rstrip("01")]
    placed = {g: [place_shard(s_, shard_axis(n), chip, F32 if n in GATHER_F32 else MXU, name="place_" + n) for n, s_ in it] for g, it in groups.items()}
    axes = {g: [shard_axis(n) for n, _ in it] for g, it in groups.items()}
    haxes = {g: [_half_axis(p_.shape, ax) for p_, ax in zip(placed[g], axes[g])] for g in groups}
    got = pair_fill_halves(gather_placed(placed["now"], axes["now"], haxes["now"], name="allgather_weights"), haxes["now"],
                           name="allgather_pair_fill")
    w.update(dict(zip([n for n, _ in groups["now"]], got)))
    fly_mid = gather_placed_start(placed["mid"], axes["mid"], haxes["mid"], got[0], name="allgather_mid_start")
    fly_later = gather_placed_start(placed["later"], axes["later"], haxes["later"], fly_mid[3], name="allgather_later_start")

    def land(fly, g, after):
        send_, recv_, flying, _ = fly
        landed = gather_placed_wait(flying, send_, recv_, axes[g], haxes[g], after, name=f"allgather_{g}_wait")
        w.update(dict(zip([n for n, _ in groups[g]], pair_fill_halves(landed, haxes[g], name=f"allgather_{g}_pair_fill"))))

    tmaj = lambda t: t.transpose(1, 0, 2).reshape(-1, t.shape[-1])
    x0 = jnp.concatenate([tmaj(ctx), tmaj(x)], axis=0)
    tgt = tmaj(a["loss_target"])
    c16 = jnp.concatenate([jnp.broadcast_to(c_ctx[None], (8, d)), c, c], axis=0) + fly_later[3][0:1, 0:1]
    mt0, scb = mod_fwd(c16, w["w_mod0"][0], w["b_mod"][0][None], name="mod_fwd0")
    mt = [mt0, None]
    n1, n2 = w["norm1_w"], w["norm2_w"]
    w["w_mod"], w["ffn_w_up"], w["ffn_w_down"] = [w["w_mod0"][0], None], [None, None], [None, None]

    def ffn_fwd(l, h):
        u = mm(h, w["ffn_w_up"][l], name=f"ffn_up{l}")
        act, ca, cg = ffn_mid_fwd(cfg, u, w["ffn_conv_w"][l], w["ffn_conv_b"][l][None], name=f"ffn_mid{l}")
        return (u, ca, cg), act, mm(act, w["ffn_w_down"][l], name=f"ffn_down{l}")

    def ffn_bwd(l, dfo, kept, act, h, zero=0.0):
        dact = mm(dfo, w["ffn_w_down"][l], tb=True, name=f"ffn_down_dx{l}")
        dwd = mm(act, dfo, ta=True, name=f"ffn_down_dw{l}")
        du, dcw, dcb = ffn_mid_bwd(cfg, dact, *kept, w["ffn_conv_w"][l] + zero, name=f"ffn_mid_bwd{l}")
        dh = mm(du, w["ffn_w_up"][l], tb=True, name=f"ffn_up_dx{l}")
        dwu = mm(h, du, ta=True, name=f"ffn_up_dw{l}")
        return dh, dwu, dcw, dcb[0], dwd

    g_, p_ = d // S5_GROUP, S5_STATE
    ns, kb = g_ * p_, d // S5_KIN
    s5p = (w["s5_lam_re"][0].reshape(2 * g_, p_), w["s5_lam_im"][0].reshape(2 * g_, p_), w["s5_log_step"][0].reshape(2 * g_, 1),
           w["s5_b_re"][0].transpose(0, 1, 3, 2).reshape(2 * g_, S5_GROUP, p_), w["s5_b_im"][0].transpose(0, 1, 3, 2).reshape(2 * g_, S5_GROUP, p_))
    ar, ai, bbr, bbi = s5_disc_fwd(*s5p, name="s5_disc")
    dsk = w["s5_d"]
    _, h1 = node_fwd(cfg, x0, None, None, 0, n1[0:1], mt[0], 0, name="node0a")
    u0 = mm(h1, w["s5_w_in"][0], name="s5_in")
    s5s, ys = [], []
    for dd in range(2):
        sl = slice(dd * g_, (dd + 1) * g_)
        a_r, a_i = ar[sl].reshape(1, ns), ai[sl].reshape(1, ns)
        a2 = (a_r * a_r - a_i * a_i, 2.0 * a_r * a_i)
        b_r, b_i = _blockdiag_b(bbr[sl], kb), _blockdiag_b(bbi[sl], kb)
        c_r, c_i = _blockdiag_c(w["s5_c_re"][0, dd], kb), _blockdiag_c(w["s5_c_im"][0, dd], kb)
        ak, ai_k = a_r.reshape(kb, 1, S5_KST), a_i.reshape(kb, 1, S5_KST)
        ab = (ak * b_r - ai_k * b_i, ak * b_i + ai_k * b_r)
        akc, aic = ak.reshape(kb, S5_KST, 1), ai_k.reshape(kb, S5_KST, 1)
        c2 = (akc * c_r - aic * c_i, akc * c_i + aic * c_r)
        bf = lambda t_: t_.astype(MXU)
        sre, sim, ere, eim, y_ = s5_scan_fwd(cfg, u0, a2[0], a2[1], bf(b_r), bf(b_i), bf(ab[0]), bf(ab[1]), bf(c_r), bf(c_i), rev=dd == 1,
                                             name=f"s5_scan{dd}")
        s5s.append((sre, sim, ere, eim, a2[0], a2[1], bf(b_r), bf(b_i), bf(c_r), bf(c_i), bf(c2[0]), bf(c2[1])))
        ys.append(y_)

    def glu_a(u, y0, y1, ds):
        yp = (ds * u + y0) + y1
        return yp, _gelu(yp)

    ypre, zgb = rowmap(glu_a, [u0, ys[0], ys[1]], [dsk], [(d, F32), (d, MXU)], name="s5_glu_a")
    land(fly_mid, "mid", zgb)
    w["ffn_w_up"][0], w["ffn_w_down"][0] = w["ffn_w_up0"][0], w["ffn_w_down0"][0]
    tg = mm(zgb, w["s5_w_glu"][0], name="s5_glu")
    (z2,) = rowmap(lambda yp, t: _gelu(yp) * jax.nn.sigmoid(t), [ypre, tg], [], [(d, MXU)], name="s5_glu_b")
    y1a = mm(z2, w["s5_w_out"][0], name="s5_out")
    x1a, h2a = node_fwd(cfg, x0, y1a, mt[0], 2, n2[0:1], mt[0], 3, name="node0b")
    ufa, acta, foa = ffn_fwd(0, h2a)

    land(fly_later, "later", foa)
    w["w_mod"][1], w["ffn_w_up"][1], w["ffn_w_down"][1] = w["w_mod1"][0], w["ffn_w_up1"][0], w["ffn_w_down1"][0]
    mt[1], _ = mod_fwd(c16, w["w_mod"][1], w["b_mod"][1][None], name="mod_fwd1")
    x2a, h1b = node_fwd(cfg, x1a, foa, mt[0], 5, n1[1:2], mt[1], 0, name="node1a")
    z = mm(h1b, w["hg_w_in"][0], name="hg_in")
    e0, e1 = w["hg_lower_bounds"][:, 0, :], w["hg_lower_bounds"][:, 1, :]
    lb = hg_lb_fwd(e0, e1, name="hg_lb")
    gw = w["hg_gnorm_w"]
    o0, sin0 = hg_scan_fwd(cfg, z, lb[0:1], d_dir=0, name="hg_scan0")
    o1, sin1 = hg_scan_fwd(cfg, z, lb[1:2], d_dir=1, name="hg_scan1")
    onb = hg_read_fwd(o0, o1, z, gw, name="hg_read")
    y1b = mm(onb, w["hg_w_out"][0], name="hg_out")
    x1b, h2b = node_fwd(cfg, x2a, y1b, mt[1], 2, n2[1:2], mt[1], 3, name="node1b")
    ufb, actb, fob = ffn_fwd(1, h2b)
    loss_p, dx2b, dfob, dg2_1, dfnw = final_node(cfg, x1b, fob, mt[1], 5, w["final_norm_w"][None], tgt, name="final_node")

    gr = {}
    dh2b, dwu1, dcw1, dcb1, dwd1 = ffn_bwd(1, dfob, ufb, actb, h2b)
    dx1b, dy1b, dn2_1, dsh2_1, dsc2_1, dg1_1 = node_bwd(cfg, dx2b, dh2b, x1b, y1b, mt[1], 2, n2[1:2], mt[1], 3, name="node1b_bwd")
    don = mm(dy1b, w["hg_w_out"][0], tb=True, name="hg_out_dx")
    gr["hg_w_out"] = mm(onb, dy1b, ta=True, name="hg_out_dw")[None]
    do_, dgate_, dgw = hg_read_bwd(don, o0, o1, z, gw, name="hg_read_bwd")
    dq, dv, dxf, dlb0 = hg_scan_bwd(cfg, do_, z, lb[0:1], sin0, None, None, d_dir=0, name="hg_scan_bwd0")
    dq, dv, dxb, dlb1 = hg_scan_bwd(cfg, do_, z, lb[1:2], sin1, dq, dv, d_dir=1, name="hg_scan_bwd1")
    dz = jnp.concatenate([t_.astype(MXU) for t_ in (dq, dv, dxf, dxb, dgate_)], axis=1)
    dh1b = mm(dz, w["hg_w_in"][0], tb=True, name="hg_in_dx")
    gr["hg_w_in"] = mm(h1b, dz, ta=True, name="hg_in_dw")[None]
    de0, de1 = hg_lb_bwd(e0, e1, jnp.concatenate([dlb0, dlb1], axis=0), name="hg_lb_bwd")
    gr["hg_lower_bounds"] = jnp.stack([de0, de1], axis=1)
    gr["hg_gnorm_w"] = dgw
    dx2a, dfoa, dn1_1, dsh1_1, dsc1_1, dg2_0 = node_bwd(cfg, dx1b, dh1b, x2a, foa, mt[0], 5, n1[1:2], mt[1], 0, name="node1a_bwd")
    dmt1 = jnp.concatenate([dsh1_1, dsc1_1, dg1_1, dsh2_1, dsc2_1, dg2_1], axis=1)
    red1, tok1 = _reduce_start([("hg_w_in", gr["hg_w_in"], 2), ("hg_w_out", gr["hg_w_out"], 1), ("ffn_w_up1", dwu1[None], 2),
                                ("ffn_w_down1", dwd1[None], 1), ("w_mod1", mm(scb, dmt1, ta=True, name="mod_dw1")[None], 2)], core, "layer1")

    dh2a, dwu0, dcw0, dcb0, dwd0 = ffn_bwd(0, dfoa, ufa, acta, h2a, zero=tok1[0:1, 0:1])
    red2, tok2 = _reduce_start([("ffn_w_up0", dwu0[None], 2), ("ffn_w_down0", dwd0[None], 1)], core, "ffn0")
    dx1a, dy1a, dn2_0, dsh2_0, dsc2_0, dg1_0 = node_bwd(cfg, dx2a, dh2a, x1a, y1a, mt[0], 2, n2[0:1] + tok2[0:1, 0:1], mt[0], 3,
                                                        name="node0b_bwd")
    dz2 = mm(dy1a, w["s5_w_out"][0], tb=True, name="s5_out_dx")
    gr["s5_w_out"] = mm(z2, dy1a, ta=True, name="s5_out_dw")[None]

    def glu_b_bwd(dz2_, yp, t):
        zg, sg = _gelu(yp), jax.nn.sigmoid(t)
        return dz2_ * zg * sg * (1.0 - sg), dz2_ * sg

    dtg, dzg_dir = rowmap(glu_b_bwd, [dz2, ypre, tg], [], [(d, MXU), (d, F32)], name="s5_glu_b_bwd")
    dzg_mm = mm(dtg, w["s5_w_glu"][0], tb=True, name="s5_glu_dx")
    gr["s5_w_glu"] = mm(zgb, dtg, ta=True, name="s5_glu_dw")[None]

    def glu_a_bwd(dzd, dzm, yp, u, ds):
        _, vjp = jax.vjp(_gelu, yp)
        (dy,) = vjp(dzd + dzm)
        return dy, dy * ds, jnp.sum(dy * u, axis=0, keepdims=True)

    dyb, du, ddsk = rowmap(glu_a_bwd, [dzg_dir, dzg_mm, ypre, u0], [dsk], [(d, MXU), (d, F32)], [(1, d)], name="s5_glu_a_bwd")
    gr["s5_d"] = ddsk
    dar, dai, dbr, dbi, dcr, dci = [], [], [], [], [], []
    for dd in range(2):
        sre, sim, ere, eim = s5s[dd][:4]
        du, gre, gim, da_r, da_i = s5_scan_bwd(cfg, dyb, *s5s[dd], du, rev=dd == 1, name=f"s5_scan_bwd{dd}")
        dar.append(colsum(da_r, name=f"s5_da_re{dd}").reshape(g_, p_))
        dai.append(colsum(da_i, name=f"s5_da_im{dd}").reshape(g_, p_))
        dbr.append(_diag_b(blockdiag_tn(u0, gre, S5_KIN, S5_KST, name=f"s5_db_re{dd}"), kb))
        dbi.append(_diag_b(blockdiag_tn(u0, gim, S5_KIN, S5_KST, name=f"s5_db_im{dd}"), kb))
        dcr.append(_diag_c(blockdiag_tn(sre.reshape(-1, ns), dyb, S5_KST, S5_KIN, name=f"s5_dc_re{dd}"), kb))
        dci.append(_diag_c(blockdiag_tn(sim.reshape(-1, ns), dyb, S5_KST, S5_KIN, scale=-1.0, name=f"s5_dc_im{dd}"), kb))
    cat = lambda l_: jnp.concatenate(l_, axis=0)
    dlr, dli, dls, dbre, dbim = s5_disc_bwd(*s5p, cat(dar), cat(dai), cat(dbr), cat(dbi), name="s5_disc_bwd")
    gr["s5_lam_re"], gr["s5_lam_im"] = dlr.reshape(1, 2, g_, p_), dli.reshape(1, 2, g_, p_)
    gr["s5_log_step"] = dls.reshape(1, 2, g_)
    gr["s5_b_re"] = dbre.reshape(1, 2, g_, S5_GROUP, p_).transpose(0, 1, 2, 4, 3)
    gr["s5_b_im"] = dbim.reshape(1, 2, g_, S5_GROUP, p_).transpose(0, 1, 2, 4, 3)
    gr["s5_c_re"], gr["s5_c_im"] = jnp.stack(dcr)[None], jnp.stack(dci)[None]
    dh1 = mm(du, w["s5_w_in"][0], tb=True, name="s5_in_dx")
    gr["s5_w_in"] = mm(h1, du, ta=True, name="s5_in_dw")[None]
    dx0, _, dn1_0, dsh1_0, dsc1_0, _ = node_bwd(cfg, dx1a, dh1, x0, None, None, 0, n1[0:1], mt[0], 0, name="node0a_bwd")

    dmt = [jnp.concatenate([dsh1_0, dsc1_0, dg1_0, dsh2_0, dsc2_0, dg2_0], axis=1), dmt1]
    gr["b_mod"] = jnp.concatenate([colsum(dmt[l], name=f"mod_db{l}") for l in range(2)], axis=0)
    dsc16 = [mm(dmt[l], w["w_mod"][l], tb=True, name=f"mod_dx{l}") for l in range(2)]
    gr["c_ctx"] = cctx_grad(c16, dsc16, name="c_ctx_grad")[0]
    gr["norm1_w"] = jnp.concatenate([dn1_0, dn1_1], axis=0)
    gr["norm2_w"] = jnp.concatenate([dn2_0, dn2_1], axis=0)
    gr["final_norm_w"] = dfnw[0]
    gr["ffn_conv_w"], gr["ffn_conv_b"] = jnp.stack([dcw0, dcw1]), jnp.stack([dcb0, dcb1])

    last = [(n, gr[n], SHARD_AXIS[n]) for n in ("s5_w_in", "s5_w_glu", "s5_w_out", "hg_lower_bounds", "ffn_conv_w")]
    last.append(("w_mod0", mm(scb, dmt[0], ta=True, name="mod_dw0")[None], 2))
    red = _reduce_now(a, last, [n for n in WEIGHTS if n not in SHARD_AXIS], gr, core)
    red.update(_reduce_finish(red1, core, dx0, "layer1"))
    red.update(_reduce_finish(red2, core, dx0, "ffn0"))
    red["w_mod"] = jnp.concatenate([red["w_mod0"], red["w_mod1"]], axis=0)
    red["ffn_w_up"] = jnp.concatenate([red["ffn_w_up0"], red["ffn_w_up1"]], axis=0)
    red["ffn_w_down"] = jnp.concatenate([red["ffn_w_down0"], red["ffn_w_down1"]], axis=0)
    loss = lax.psum(loss_p[0, 0], ("x", "y", "c"))
    grad_x = dx0[rc:].reshape(seq, nb, d).transpose(1, 0, 2)
    upd = {n: adamw(a[n], red[n], a["m_" + n], a["v_" + n], name="adamw_" + n) for n in WEIGHTS}
    return (loss, grad_x, *[red[n] for n in WEIGHTS], *[upd[n][0] for n in WEIGHTS], *[upd[n][1] for n in WEIGHTS],
            *[upd[n][2] for n in WEIGHTS])
```

```python
import functools
import math

import jax
import jax.numpy as jnp
from jax import lax
from jax.experimental import pallas as pl
from jax.experimental.pallas import tpu as pltpu

F32 = jnp.float32
BF = jnp.bfloat16
MXU = jnp.bfloat16

NORM_EPS = 1e-6
GRID_W = 64
N_MOD = 6
S5_GROUP = 16
S5_STATE = 64
S5_LAM_RE_MAX = -1e-4
S5_KIN = 256
S5_KST = S5_KIN // S5_GROUP * S5_STATE
HEAD = 128
CHUNK_ROWS = 128
N_PROJ = 5
NB = 4
ADAM_LR, ADAM_B1, ADAM_B2, ADAM_EPS, ADAM_WD, ADAM_STEP = 0.001, 0.9, 0.999, 1e-08, 0.01, 10
VMEM_LIMIT = 56 * 1024 * 1024
MESH = pl.DeviceIdType.MESH


def _tile(n, cap):
    if n <= cap:
        return n
    best = None
    for t in range(128, cap + 1, 128):
        if n % t == 0:
            best = t
    assert best is not None, (n, cap)
    return best


def _row_tile(r, width=1024):
    cap = max(8, (512 * 1024) // max(width, 1))
    return next((t for t in (512, 256, 128, 64, 32, 16, 8) if t <= cap and r % t == 0), r)


def _cp(sem):
    return pltpu.CompilerParams(dimension_semantics=sem, vmem_limit_bytes=VMEM_LIMIT)


def _dot(a, b, ca=1, cb=0):
    return lax.dot_general(a.astype(MXU), b.astype(MXU), (((ca,), (cb,)), ((), ())), preferred_element_type=F32)


def _dot3(m, x):
    hi = x.astype(MXU)
    r1 = x - hi.astype(F32)
    mid = r1.astype(MXU)
    lo = (r1 - mid.astype(F32)).astype(MXU)
    return _dot(m, hi) + _dot(m, mid) + _dot(m, lo)


def mm(a, b, *, ta=False, tb=False, out_dtype=F32, name):
    (kd, m) = a.shape if ta else a.shape[::-1]
    (n, kd2) = b.shape if tb else b.shape[::-1]
    assert kd == kd2, (a.shape, b.shape, ta, tb)
    tm, tn, tk = _tile(m, 1024), _tile(n, 1536), _tile(kd, 1024)
    nk = kd // tk

    def body(a_ref, b_ref, o_ref, acc_ref):
        k = pl.program_id(2)

        @pl.when(k == 0)
        def _():
            acc_ref[...] = jnp.zeros_like(acc_ref)

        acc_ref[...] += _dot(a_ref[...], b_ref[...], 0 if ta else 1, 1 if tb else 0)

        @pl.when(k == nk - 1)
        def _():
            o_ref[...] = acc_ref[...].astype(out_dtype)

    a_spec = pl.BlockSpec((tk, tm), lambda i, j, k: (k, i)) if ta else pl.BlockSpec((tm, tk), lambda i, j, k: (i, k))
    b_spec = pl.BlockSpec((tn, tk), lambda i, j, k: (j, k)) if tb else pl.BlockSpec((tk, tn), lambda i, j, k: (k, j))
    return pl.pallas_call(
        body, name=name, grid=(m // tm, n // tn, nk), in_specs=[a_spec, b_spec],
        out_specs=pl.BlockSpec((tm, tn), lambda i, j, k: (i, j)), out_shape=jax.ShapeDtypeStruct((m, n), out_dtype),
        scratch_shapes=[pltpu.VMEM((tm, tn), F32)], compiler_params=_cp(("parallel", "parallel", "arbitrary")))(a, b)


def blockdiag_tn(a, b, wa, wb, *, scale=1.0, name):
    rows = a.shape[0]
    kb = a.shape[1] // wa
    tr = _tile(rows, 1024)
    nr = rows // tr

    def body(a_ref, b_ref, o_ref):
        i = pl.program_id(1)

        @pl.when(i == 0)
        def _():
            o_ref[...] = jnp.zeros_like(o_ref)

        o_ref[0] += scale * _dot(a_ref[...], b_ref[...], 0, 0)

    return pl.pallas_call(
        body, name=name, grid=(kb, nr),
        in_specs=[pl.BlockSpec((tr, wa), lambda k, i: (i, k)), pl.BlockSpec((tr, wb), lambda k, i: (i, k))],
        out_specs=pl.BlockSpec((1, wa, wb), lambda k, i: (k, 0, 0)), out_shape=jax.ShapeDtypeStruct((kb, wa, wb), F32),
        compiler_params=_cp(("parallel", "arbitrary")))(a, b)


def _pat(v, p, op):
    tm, d = v.shape
    return op(v.reshape(tm // 8, 8, d), p[None]).reshape(tm, d)


def _norm_mod(x, nw, shift, scale):
    y = x * lax.rsqrt(jnp.mean(x * x, axis=-1, keepdims=True) + NORM_EPS) * nw
    return _pat(_pat(y, 1.0 + scale, jnp.multiply), shift, jnp.add)


def _mt_spec(d, nct):
    return pl.BlockSpec((8, N_MOD * d), lambda i: (jnp.where(i < nct, 0, 1), 0))


def _acc_spec(d, nct):
    return pl.BlockSpec((8, d), lambda i: (jnp.where(i < nct, 0, 1), 0))


def _rows(cfg):
    tm = min(512, cfg["rc"])
    return tm, cfg["rc"] // tm


def node_fwd(cfg, xp, y, mtg, gi, nw, mtn, si, *, name):
    r, d = xp.shape
    tm, nct = _rows(cfg)
    row = pl.BlockSpec((tm, d), lambda i: (i, 0))
    vec = pl.BlockSpec((1, d), lambda i: (0, 0))

    def body(*refs):
        if y is None:
            xp_ref, nw_ref, mtn_ref, h_ref = refs
            x = xp_ref[...]
        else:
            xp_ref, y_ref, mtg_ref, nw_ref, mtn_ref, xn_ref, h_ref = refs
            x = xp_ref[...] + _pat(y_ref[...], mtg_ref[:, gi * d:(gi + 1) * d], jnp.multiply)
            xn_ref[...] = x
        h_ref[...] = _norm_mod(x, nw_ref[...], mtn_ref[:, si * d:(si + 1) * d], mtn_ref[:, (si + 1) * d:(si + 2) * d]).astype(MXU)

    h_shape = jax.ShapeDtypeStruct((r, d), MXU)
    if y is None:
        h = pl.pallas_call(body, name=name, grid=(r // tm,), in_specs=[row, vec, _mt_spec(d, nct)], out_specs=row,
                           out_shape=h_shape, compiler_params=_cp(("parallel",)))(xp, nw, mtn)
        return xp, h
    return pl.pallas_call(body, name=name, grid=(r // tm,), in_specs=[row, row, _mt_spec(d, nct), vec, _mt_spec(d, nct)],
                          out_specs=(row, row), out_shape=(jax.ShapeDtypeStruct((r, d), F32), h_shape),
                          compiler_params=_cp(("parallel",)))(xp, y, mtg, nw, mtn)


def node_bwd(cfg, dxres, dh, xn, y, mtg, gi, nw, mtn, si, *, name):
    r, d = xn.shape
    tm, nct = _rows(cfg)
    row = pl.BlockSpec((tm, d), lambda i: (i, 0))
    vec = pl.BlockSpec((1, d), lambda i: (0, 0))
    has_y = y is not None

    def body(*refs):
        if has_y:
            dxres_ref, dh_ref, xn_ref, y_ref, mtg_ref, nw_ref, mtn_ref, dxn_ref, dy_ref, dnw_ref, dsh_ref, dsc_ref, dg_ref = refs
        else:
            dxres_ref, dh_ref, xn_ref, nw_ref, mtn_ref, dxn_ref, dnw_ref, dsh_ref, dsc_ref = refs
        i = pl.program_id(0)
        _, vjp = jax.vjp(_norm_mod, xn_ref[...], nw_ref[...], mtn_ref[:, si * d:(si + 1) * d], mtn_ref[:, (si + 1) * d:(si + 2) * d])
        dx, dnw, dsh, dsc = vjp(dh_ref[...])
        dx = dx + dxres_ref[...]
        dxn_ref[...] = dx

        @pl.when(i == 0)
        def _():
            dnw_ref[...] = jnp.zeros_like(dnw_ref)

        @pl.when((i == 0) | (i == nct))
        def _():
            dsh_ref[...] = jnp.zeros_like(dsh_ref)
            dsc_ref[...] = jnp.zeros_like(dsc_ref)
            if has_y:
                dg_ref[...] = jnp.zeros_like(dg_ref)

        dnw_ref[...] += dnw
        dsh_ref[...] += dsh
        dsc_ref[...] += dsc
        if has_y:
            dy_ref[...] = _pat(dx, mtg_ref[:, gi * d:(gi + 1) * d], jnp.multiply).astype(MXU)
            dg_ref[...] += jnp.sum((dx * y_ref[...]).reshape(tm // 8, 8, d), axis=0)

    acc = jax.ShapeDtypeStruct((16, d), F32)
    xs = jax.ShapeDtypeStruct((r, d), F32)
    if has_y:
        return pl.pallas_call(
            body, name=name, grid=(r // tm,), in_specs=[row, row, row, row, _mt_spec(d, nct), vec, _mt_spec(d, nct)],
            out_specs=(row, row, vec, _acc_spec(d, nct), _acc_spec(d, nct), _acc_spec(d, nct)),
            out_shape=(xs, jax.ShapeDtypeStruct((r, d), MXU), jax.ShapeDtypeStruct((1, d), F32), acc, acc, acc),
            compiler_params=_cp(("arbitrary",)))(dxres, dh, xn, y, mtg, nw, mtn)
    dxn, dnw, dsh, dsc = pl.pallas_call(
        body, name=name, grid=(r // tm,), in_specs=[row, row, row, vec, _mt_spec(d, nct)],
        out_specs=(row, vec, _acc_spec(d, nct), _acc_spec(d, nct)),
        out_shape=(xs, jax.ShapeDtypeStruct((1, d), F32), acc, acc), compiler_params=_cp(("arbitrary",)))(dxres, dh, xn, nw, mtn)
    return dxn, None, dnw, dsh, dsc, None


def final_node(cfg, xp, y, mtg, gi, fnw, tgt, *, name):
    r, d = xp.shape
    tm, nct = _rows(cfg)
    row = pl.BlockSpec((tm, d), lambda i: (i, 0))
    vec = pl.BlockSpec((1, d), lambda i: (0, 0))

    def norm(x, w):
        return x * lax.rsqrt(jnp.mean(x * x, axis=-1, keepdims=True) + NORM_EPS) * w

    def body(xp_ref, y_ref, mtg_ref, fnw_ref, tgt_ref, loss_ref, dx_ref, dy_ref, dg_ref, dfnw_ref):
        i = pl.program_id(0)
        g = mtg_ref[:, gi * d:(gi + 1) * d]
        x = xp_ref[...] + _pat(y_ref[...], g, jnp.multiply)
        out, vjp = jax.vjp(norm, x, fnw_ref[...])
        lat = i >= nct
        err = jnp.where(lat, out - tgt_ref[...], 0.0)
        dx, dfnw = vjp(err * (1.0 / d))

        @pl.when(i == 0)
        def _():
            loss_ref[...] = jnp.zeros_like(loss_ref)
            dfnw_ref[...] = jnp.zeros_like(dfnw_ref)

        @pl.when((i == 0) | (i == nct))
        def _():
            dg_ref[...] = jnp.zeros_like(dg_ref)

        loss_ref[...] += jnp.full(loss_ref.shape, 0.5 / d * jnp.sum(err * err), F32)
        dfnw_ref[...] += dfnw
        dx_ref[...] = dx
        dy_ref[...] = _pat(dx, g, jnp.multiply).astype(MXU)
        dg_ref[...] += jnp.sum((dx * y_ref[...]).reshape(tm // 8, 8, d), axis=0)

    return pl.pallas_call(
        body, name=name, grid=(r // tm,),
        in_specs=[row, row, _mt_spec(d, nct), vec, pl.BlockSpec((tm, d), lambda i: (jnp.maximum(i - nct, 0), 0))],
        out_specs=(pl.BlockSpec((8, 128), lambda i: (0, 0)), row, row, _acc_spec(d, nct), vec),
        out_shape=(jax.ShapeDtypeStruct((8, 128), F32), jax.ShapeDtypeStruct((r, d), F32), jax.ShapeDtypeStruct((r, d), MXU),
                   jax.ShapeDtypeStruct((16, d), F32), jax.ShapeDtypeStruct((1, d), F32)),
        compiler_params=_cp(("arbitrary",)))(xp, y, mtg, fnw, tgt)


def _silu(x):
    return x * jax.nn.sigmoid(x)


def mod_fwd(c16, w, b, *, name):
    d, n = w.shape
    tn = _tile(n, 1536)

    def body(c_ref, w_ref, b_ref, o_ref, s_ref):
        s = _silu(c_ref[...])
        s_ref[...] = s.astype(MXU)
        o_ref[...] = _dot(s, w_ref[...]) + b_ref[...]

    return pl.pallas_call(
        body, name=name, grid=(n // tn,),
        in_specs=[pl.BlockSpec((16, d), lambda j: (0, 0)), pl.BlockSpec((d, tn), lambda j: (0, j)), pl.BlockSpec((1, tn), lambda j: (0, j))],
        out_specs=(pl.BlockSpec((16, tn), lambda j: (0, j)), pl.BlockSpec((16, d), lambda j: (0, 0))),
        out_shape=(jax.ShapeDtypeStruct((16, n), F32), jax.ShapeDtypeStruct((16, d), MXU)),
        compiler_params=_cp(("arbitrary",)))(c16, w, b)


def colsum(x, *, name):
    def body(x_ref, o_ref):
        o_ref[...] = jnp.sum(x_ref[...], axis=0, keepdims=True)

    return pl.pallas_call(body, name=name, out_shape=jax.ShapeDtypeStruct((1, x.shape[1]), F32))(x)


def cctx_grad(c16, ds_list, *, name):
    def body(c_ref, *refs):
        o_ref = refs[-1]
        ds = refs[0][...]
        for r_ in refs[1:-1]:
            ds = ds + r_[...]
        _, vjp = jax.vjp(_silu, c_ref[...])
        (dc,) = vjp(ds)
        o_ref[...] = jnp.sum(dc[0:8], axis=0, keepdims=True)

    return pl.pallas_call(body, name=name, out_shape=jax.ShapeDtypeStruct((1, c16.shape[1]), F32))(c16, *ds_list)


def _s5_disc(lam_re, lam_im, log_step, b_re, b_im):
    lr = jnp.minimum(lam_re, S5_LAM_RE_MAX)
    li = lam_im
    dt = jnp.exp(log_step)
    mag = jnp.exp(lr * dt)
    abar_r = mag * jnp.cos(li * dt)
    abar_i = mag * jnp.sin(li * dt)
    den = lr * lr + li * li
    nr = abar_r - 1.0
    coef_r = (nr * lr + abar_i * li) / den
    coef_i = (abar_i * lr - nr * li) / den
    bbar_r = coef_r[:, None, :] * b_re - coef_i[:, None, :] * b_im
    bbar_i = coef_r[:, None, :] * b_im + coef_i[:, None, :] * b_re
    return abar_r, abar_i, bbar_r, bbar_i


def s5_disc_fwd(lam_re, lam_im, log_step, b_re, b_im, *, name):
    def body(lr, li, ls, br, bi, ar_o, ai_o, br_o, bi_o):
        ar_o[...], ai_o[...], br_o[...], bi_o[...] = _s5_disc(lr[...], li[...], ls[...], br[...], bi[...])

    s2, s3 = jax.ShapeDtypeStruct(lam_re.shape, F32), jax.ShapeDtypeStruct(b_re.shape, F32)
    return pl.pallas_call(body, name=name, out_shape=(s2, s2, s3, s3))(lam_re, lam_im, log_step, b_re, b_im)


def s5_disc_bwd(lam_re, lam_im, log_step, b_re, b_im, d_ar, d_ai, d_br, d_bi, *, name):
    def body(lr, li, ls, br, bi, dar, dai, dbr, dbi, o_lr, o_li, o_ls, o_br, o_bi):
        _, vjp = jax.vjp(_s5_disc, lr[...], li[...], ls[...], br[...], bi[...])
        o_lr[...], o_li[...], o_ls[...], o_br[...], o_bi[...] = vjp((dar[...], dai[...], dbr[...], dbi[...]))

    s2, s3 = jax.ShapeDtypeStruct(lam_re.shape, F32), jax.ShapeDtypeStruct(b_re.shape, F32)
    return pl.pallas_call(body, name=name, out_shape=(s2, s2, jax.ShapeDtypeStruct(log_step.shape, F32), s3, s3))(
        lam_re, lam_im, log_step, b_re, b_im, d_ar, d_ai, d_br, d_bi)


S5_LANES = 512


def _chunk_order(k, ncc, nch, rev):
    if not rev:
        return k
    return jnp.where(k < ncc, ncc - 1 - k, nch - 1 - (k - ncc))


def _cmul(ar, ai, xr, xi):
    return ar * xr - ai * xi, ar * xi + ai * xr


S5_FWD_ROWS = 256
S5_BWD_ROWS = 256


def _const_spec(a):
    return pl.BlockSpec(a.shape, lambda k: (0,) * a.ndim, pipeline_mode=pl.Buffered(1))


def _shift_steps(x, edge_tile, back):
    n = x.shape[0]
    row = lax.broadcasted_iota(jnp.int32, (8, x.shape[1]), 0)
    edge = pltpu.roll(edge_tile, 4, 0)
    if back:
        y = pltpu.roll(x, 4, 0)
        return jnp.concatenate([jnp.where(row < 4, edge, y[0:8]), y[8:]], axis=0)
    y = pltpu.roll(x, n - 4, 0)
    return jnp.concatenate([y[:n - 8], jnp.where(row >= 4, edge, y[n - 8:])], axis=0)


def s5_scan_fwd(cfg, u, a2_re, a2_im, bre, bim, abre, abim, cre, cim, *, rev, name):
    r, d = u.shape
    ns = a2_re.shape[1]
    kb = d // S5_KIN
    tcr = S5_FWD_ROWS
    n8 = tcr // 8
    q = S5_FWD_ROWS // S5_BWD_ROWS
    seg = n8 // q
    nch, ncc = r // tcr, cfg["rc"] // tcr
    lw = min(S5_LANES, ns)

    def body(u_ref, ar_ref, ai_ref, bre_ref, bim_ref, abre_ref, abim_ref, cre_ref, cim_ref, sre_ref, sim_ref, ere_ref, eim_ref, y_ref,
             st_re, st_im, u_edge):
        @pl.when(pl.program_id(0) == 0)
        def _():
            st_re[...] = jnp.zeros_like(st_re)
            st_im[...] = jnp.zeros_like(st_im)
            u_edge[...] = jnp.zeros_like(u_edge)

        u_ = u_ref[...]
        ub = u_.astype(MXU)
        upb = _shift_steps(u_, u_edge[...], back=not rev).astype(MXU)
        u_edge[...] = u_[0:8] if rev else u_[tcr - 8:tcr]
        for j in range(kb):
            uj, upj = ub[:, j * S5_KIN:(j + 1) * S5_KIN], upb[:, j * S5_KIN:(j + 1) * S5_KIN]
            sre_ref[:, :, j * S5_KST:(j + 1) * S5_KST] = (_dot(uj, bre_ref[j]) + _dot(upj, abre_ref[j])).reshape(n8, 8, S5_KST)
            sim_ref[:, :, j * S5_KST:(j + 1) * S5_KST] = (_dot(uj, bim_ref[j]) + _dot(upj, abim_ref[j])).reshape(n8, 8, S5_KST)
        for c in range(ns // lw):
            sl = slice(c * lw, (c + 1) * lw)
            ar = jnp.broadcast_to(ar_ref[:, sl], (8, lw))
            ai = jnp.broadcast_to(ai_ref[:, sl], (8, lw))

            def step(i, carry, sl=sl, ar=ar, ai=ai):
                sr, si = carry
                ii = n8 - 1 - i if rev else i
                pr, pi = _cmul(ar, ai, sr, si)
                sr, si = pr + sre_ref[ii, :, sl], pi + sim_ref[ii, :, sl]
                sre_ref[ii, :, sl] = sr
                sim_ref[ii, :, sl] = si
                return sr, si

            sr, si = st_re[:, sl], st_im[:, sl]
            for s_ in range(q):
                at = q - 1 - s_ if rev else s_
                ere_ref[at, :, sl] = sr
                eim_ref[at, :, sl] = si
                sr, si = lax.fori_loop(s_ * seg, (s_ + 1) * seg, step, (sr, si))
            st_re[:, sl] = sr
            st_im[:, sl] = si
        for j in range(kb):
            sr = sre_ref[:, :, j * S5_KST:(j + 1) * S5_KST].reshape(tcr, S5_KST)
            si = sim_ref[:, :, j * S5_KST:(j + 1) * S5_KST].reshape(tcr, S5_KST)
            y_ref[:, j * S5_KIN:(j + 1) * S5_KIN] = _dot(sr, cre_ref[j]) - _dot(si, cim_ref[j])

    cidx = functools.partial(_chunk_order, ncc=ncc, nch=nch, rev=rev)
    full = _const_spec
    st = pl.BlockSpec((n8, 8, ns), lambda k: (cidx(k), 0, 0))
    en = pl.BlockSpec((q, 8, ns), lambda k: (cidx(k), 0, 0))
    return pl.pallas_call(
        body, name=name, grid=(nch,),
        in_specs=[pl.BlockSpec((tcr, d), lambda k: (cidx(k), 0)), full(a2_re), full(a2_im), full(bre), full(bim), full(abre), full(abim),
                  full(cre), full(cim)],
        out_specs=(st, st, en, en, pl.BlockSpec((tcr, d), lambda k: (cidx(k), 0))),
        out_shape=(jax.ShapeDtypeStruct((r // 8, 8, ns), F32),) * 2 + (jax.ShapeDtypeStruct((q * nch, 8, ns), F32),) * 2
        + (jax.ShapeDtypeStruct((r, d), F32),),
        scratch_shapes=[pltpu.VMEM((8, ns), F32), pltpu.VMEM((8, ns), F32), pltpu.VMEM((8, d), F32)],
        compiler_params=_cp(("arbitrary",)))(u, a2_re, a2_im, bre, bim, abre, abim, cre, cim)


def s5_scan_bwd(cfg, dyb, sre, sim, ere, eim, a2_re, a2_im, bre, bim, cre, cim, c2re, c2im, du_in, *, rev, name):
    r, d = dyb.shape
    ns = a2_re.shape[1]
    kb = d // S5_KIN
    tcr = S5_BWD_ROWS
    n8 = tcr // 8
    nch, ncc = r // tcr, cfg["rc"] // tcr
    lw = min(S5_LANES, ns)

    def body(dy_ref, sre_ref, sim_ref, ere_ref, eim_ref, ar_ref, ai_ref, bre_ref, bim_ref, cre_ref, cim_ref, c2re_ref, c2im_ref, duin_ref,
             du_ref, gre_ref, gim_ref, dar_ref, dai_ref, g_re, g_im, gc_re, gc_im, dy_edge):
        k = pl.program_id(0)

        @pl.when(k == 0)
        def _():
            gc_re[...] = jnp.zeros_like(gc_re)
            gc_im[...] = jnp.zeros_like(gc_im)
            dar_ref[...] = jnp.zeros_like(dar_ref)
            dai_ref[...] = jnp.zeros_like(dai_ref)
            dy_edge[...] = jnp.zeros_like(dy_edge)

        dy32 = dy_ref[...].astype(F32)
        dy = dy32.astype(MXU)
        dyn = _shift_steps(dy32, dy_edge[...], back=rev).astype(MXU)
        dy_edge[...] = dy32[tcr - 8:tcr] if rev else dy32[0:8]
        for j in range(kb):
            dyj, dynj = dy[:, j * S5_KIN:(j + 1) * S5_KIN], dyn[:, j * S5_KIN:(j + 1) * S5_KIN]
            g_re[:, :, j * S5_KST:(j + 1) * S5_KST] = (_dot(dyj, cre_ref[j], 1, 1) + _dot(dynj, c2re_ref[j], 1, 1)).reshape(n8, 8, S5_KST)
            g_im[:, :, j * S5_KST:(j + 1) * S5_KST] = -(_dot(dyj, cim_ref[j], 1, 1) + _dot(dynj, c2im_ref[j], 1, 1)).reshape(n8, 8, S5_KST)
        first = lax.broadcasted_iota(jnp.int32, (8, lw), 0) < 4
        if rev:
            first = jnp.logical_not(first)
        for c in range(ns // lw):
            sl = slice(c * lw, (c + 1) * lw)
            ar = jnp.broadcast_to(ar_ref[:, sl], (8, lw))
            nai = -jnp.broadcast_to(ai_ref[:, sl], (8, lw))

            def step(i, carry, sl=sl, ar=ar, nai=nai):
                gr, gi, accr, acci = carry
                ii = i if rev else n8 - 1 - i
                pr, pi = _cmul(ar, nai, gr, gi)
                outr, outi = pr + g_re[ii, :, sl], pi + g_im[ii, :, sl]
                g_re[ii, :, sl] = outr
                g_im[ii, :, sl] = outi
                pv = jnp.clip(ii + 1 if rev else ii - 1, 0, n8 - 1)
                at_entry = (ii == n8 - 1) if rev else (ii == 0)
                pvr = jnp.where(at_entry, ere_ref[0, :, sl], sre_ref[pv, :, sl])
                pvi = jnp.where(at_entry, eim_ref[0, :, sl], sim_ref[pv, :, sl])
                spr = pltpu.roll(jnp.where(first, sre_ref[ii, :, sl], pvr), 4, 0)
                spi = pltpu.roll(jnp.where(first, sim_ref[ii, :, sl], pvi), 4, 0)
                accr = accr + outr * spr + outi * spi
                acci = acci + outi * spr - outr * spi
                return outr, outi, accr, acci

            gr, gi, accr, acci = lax.fori_loop(0, n8, step, (gc_re[:, sl], gc_im[:, sl], dar_ref[:, sl], dai_ref[:, sl]))
            gc_re[:, sl] = gr
            gc_im[:, sl] = gi
            dar_ref[:, sl] = accr
            dai_ref[:, sl] = acci
        for j in range(kb):
            gr = g_re[:, :, j * S5_KST:(j + 1) * S5_KST].reshape(tcr, S5_KST)
            gi = g_im[:, :, j * S5_KST:(j + 1) * S5_KST].reshape(tcr, S5_KST)
            gre_ref[:, j * S5_KST:(j + 1) * S5_KST] = gr.astype(MXU)
            gim_ref[:, j * S5_KST:(j + 1) * S5_KST] = gi.astype(MXU)
            du_ref[:, j * S5_KIN:(j + 1) * S5_KIN] = (duin_ref[:, j * S5_KIN:(j + 1) * S5_KIN]
                                                     + _dot(gr, bre_ref[j], 1, 1) + _dot(gi, bim_ref[j], 1, 1))

    def cidx(k):
        return _chunk_order(nch - 1 - k, ncc, nch, rev)

    full = _const_spec
    st = pl.BlockSpec((n8, 8, ns), lambda k: (cidx(k), 0, 0))
    en = pl.BlockSpec((1, 8, ns), lambda k: (cidx(k), 0, 0))
    rowd = pl.BlockSpec((tcr, d), lambda k: (cidx(k), 0))
    rown = pl.BlockSpec((tcr, ns), lambda k: (cidx(k), 0))
    acc = pl.BlockSpec((8, ns), lambda k: (0, 0))
    return pl.pallas_call(
        body, name=name, grid=(nch,),
        in_specs=[rowd, st, st, en, en, full(a2_re), full(a2_im), full(bre), full(bim), full(cre), full(cim), full(c2re), full(c2im), rowd],
        out_specs=(rowd, rown, rown, acc, acc),
        out_shape=(jax.ShapeDtypeStruct((r, d), F32), jax.ShapeDtypeStruct((r, ns), MXU), jax.ShapeDtypeStruct((r, ns), MXU),
                   jax.ShapeDtypeStruct((8, ns), F32), jax.ShapeDtypeStruct((8, ns), F32)),
        scratch_shapes=[pltpu.VMEM((n8, 8, ns), F32), pltpu.VMEM((n8, 8, ns), F32), pltpu.VMEM((8, ns), F32), pltpu.VMEM((8, ns), F32),
                        pltpu.VMEM((8, d), F32)],
        compiler_params=_cp(("arbitrary",)))(dyb, sre, sim, ere, eim, a2_re, a2_im, bre, bim, cre, cim, c2re, c2im, du_in)


def rowmap(fn, rows_in, vecs_in, outs, accs=(), *, name):
    r = rows_in[0].shape[0]
    tm = _row_tile(r, max(a.shape[1] for a in rows_in))
    nr, nv, no = len(rows_in), len(vecs_in), len(outs)

    def body(*refs):
        ins = [x[...] for x in refs[:nr + nv]]
        res = fn(*ins)
        if not isinstance(res, (tuple, list)):
            res = (res,)
        out_refs = refs[nr + nv:]
        for o_ref, v in zip(out_refs[:no], res[:no]):
            o_ref[...] = v.astype(o_ref.dtype)
        if accs:
            @pl.when(pl.program_id(0) == 0)
            def _():
                for a_ref in out_refs[no:]:
                    a_ref[...] = jnp.zeros_like(a_ref)
            for a_ref, v in zip(out_refs[no:], res[no:]):
                a_ref[...] += v

    in_specs = [pl.BlockSpec((tm, a.shape[1]), lambda i: (i, 0)) for a in rows_in]
    in_specs += [pl.BlockSpec(v.shape, lambda i, n=v.ndim: (0,) * n) for v in vecs_in]
    out_specs = [pl.BlockSpec((tm, w), lambda i: (i, 0)) for w, _ in outs] + [pl.BlockSpec(s, lambda i, n=len(s): (0,) * n) for s in accs]
    out_shape = [jax.ShapeDtypeStruct((r, w), dt) for w, dt in outs] + [jax.ShapeDtypeStruct(s, F32) for s in accs]
    res = pl.pallas_call(body, name=name, grid=(r // tm,), in_specs=in_specs, out_specs=tuple(out_specs), out_shape=tuple(out_shape),
                         compiler_params=_cp(("arbitrary",) if accs else ("parallel",)))(*rows_in, *vecs_in)
    return res


def _gelu(x):
    return jax.nn.gelu(x, approximate=True)


def _hg_lower_bound(e0, e1):
    m = jnp.maximum(e0, e1)
    a, b = jnp.exp(e0 - m), jnp.exp(e1 - m)
    return b / (a + b)


def _hg_gates(x, lb):
    logf = jnp.log(lb + (1.0 - lb) * jax.nn.sigmoid(x))
    return logf, (1.0 - lb) * jax.nn.sigmoid(-x)


def _hg_masks(rev):
    n = CHUNK_ROWS
    rr = lax.broadcasted_iota(jnp.int32, (n, n), 0)
    ss = lax.broadcasted_iota(jnp.int32, (n, n), 1)
    same = (rr % NB) == (ss % NB)
    causal = same & ((ss >= rr) if rev else (ss <= rr))
    anti = same & ((ss <= rr) if rev else (ss >= rr))
    end0 = 0 if rev else n - NB
    pick_end = ss == (end0 + rr % NB)
    return same, causal, anti, pick_end, end0


def _hg_expand(x):
    ex = lax.broadcasted_iota(jnp.int32, x.shape, 0) % NB
    return jnp.concatenate([jnp.where(ex == b, x, 0.0) for b in range(NB)], axis=1)


def _hg_fold(xe):
    kk = xe.shape[1] // NB
    ex = lax.broadcasted_iota(jnp.int32, (xe.shape[0], kk), 0) % NB
    out = jnp.zeros((xe.shape[0], kk), F32)
    for b in range(NB):
        out = out + jnp.where(ex == b, xe[:, b * kk:(b + 1) * kk], 0.0)
    return out


def _hg_chunk(q, v, x, lb, masks):
    same, causal, anti, pick_end, end0 = masks
    logf, kk = _hg_gates(x, lb)
    b = _dot3(causal.astype(MXU), logf)
    bend_t = _dot3(pick_end.astype(MXU), b)
    bend_flat = jnp.concatenate([b[end0 + i:end0 + i + 1] for i in range(NB)], axis=1)
    eb = jnp.exp(b)
    enb = jnp.exp(-b)
    ee = jnp.exp(bend_t - b)
    qd, kd, ke = q * eb, kk * enb, kk * ee
    att = jnp.where(causal, _dot(qd, kd, 1, 1), 0.0)
    decay = jnp.exp(bend_flat)
    return dict(same=same, causal=causal, anti=anti, logf=logf, kk=kk, b=b, eb=eb, enb=enb, ee=ee, qd=qd, kd=kd, ke=ke, att=att,
                decay=decay, qde=_hg_expand(qd), kee=_hg_expand(ke))


def _hg_chunk_order(cfg, r):
    nch, ncc = r // CHUNK_ROWS, cfg["rc"] // CHUNK_ROWS
    return nch, ncc


def hg_scan_fwd(cfg, z, lb, *, d_dir, name):
    r = z.shape[0]
    d = z.shape[1] // N_PROJ
    nh = d // HEAD
    rev = d_dir == 1
    nch, ncc = _hg_chunk_order(cfg, r)
    n = CHUNK_ROWS

    def body(q_ref, v_ref, x_ref, lb_ref, o_ref, sin_ref, stk):
        @pl.when(pl.program_id(0) == 0)
        def _():
            stk[...] = jnp.zeros_like(stk)

        masks = _hg_masks(rev)
        for h in range(nh):
            sl = slice(h * HEAD, (h + 1) * HEAD)
            s0 = stk[h]
            sin_ref[0, h] = s0
            v = v_ref[:, sl]
            c = _hg_chunk(q_ref[:, sl], v, x_ref[:, sl], lb_ref[:, sl], masks)
            o_ref[:, sl] = _dot(c["att"], v) + _dot(c["qde"], s0, 1, 1)
            stk[h] = s0 * c["decay"] + _dot(v, c["kee"], 0, 0)

    def cidx(k):
        return _chunk_order(k, ncc, nch, rev)

    blk = lambda p: pl.BlockSpec((n, d), lambda k: (cidx(k), p))
    return pl.pallas_call(
        body, name=name, grid=(nch,),
        in_specs=[blk(0), blk(1), blk(2 + d_dir), pl.BlockSpec((1, d), lambda k: (0, 0))],
        out_specs=(blk(0), pl.BlockSpec((1, nh, HEAD, NB * HEAD), lambda k: (cidx(k), 0, 0, 0))),
        out_shape=(jax.ShapeDtypeStruct((r, d), F32), jax.ShapeDtypeStruct((nch, nh, HEAD, NB * HEAD), F32)),
        scratch_shapes=[pltpu.VMEM((nh, HEAD, NB * HEAD), F32)], compiler_params=_cp(("arbitrary",)))(z, z, z, lb)


def hg_scan_bwd(cfg, do, z, lb, sin, dq_in, dv_in, *, d_dir, name):
    r = z.shape[0]
    d = z.shape[1] // N_PROJ
    nh = d // HEAD
    rev = d_dir == 1
    nch, ncc = _hg_chunk_order(cfg, r)
    n = CHUNK_ROWS
    has_in = dq_in is not None

    def body(*refs):
        if has_in:
            do_ref, q_ref, v_ref, x_ref, lb_ref, sin_ref, dqi_ref, dvi_ref, dq_ref, dv_ref, dx_ref, dlb_ref, dstk = refs
        else:
            do_ref, q_ref, v_ref, x_ref, lb_ref, sin_ref, dq_ref, dv_ref, dx_ref, dlb_ref, dstk = refs
        @pl.when(pl.program_id(0) == 0)
        def _():
            dstk[...] = jnp.zeros_like(dstk)
            dlb_ref[...] = jnp.zeros_like(dlb_ref)

        masks = _hg_masks(rev)
        ex = lax.broadcasted_iota(jnp.int32, (n, HEAD), 0) % NB
        for h in range(nh):
            sl = slice(h * HEAD, (h + 1) * HEAD)
            do_, q, v, x, lb_, s0, ds1 = do_ref[:, sl], q_ref[:, sl], v_ref[:, sl], x_ref[:, sl], lb_ref[:, sl], sin_ref[0, h], dstk[h]
            c = _hg_chunk(q, v, x, lb_, masks)
            datt = jnp.where(c["causal"], _dot(do_, v, 1, 1), 0.0)
            dv = _dot(c["att"], do_, 0, 0) + _dot(c["kee"], ds1, 1, 1)
            dqd = _dot(datt, c["kd"]) + _hg_fold(_dot(do_, s0))
            dkd = _dot(datt, c["qd"], 0, 0)
            dke = _hg_fold(_dot(v, ds1))
            dbend_flat = jnp.sum(ds1 * s0, axis=0, keepdims=True) * c["decay"]
            dstk[h] = _dot(do_, c["qde"], 0, 0) + ds1 * c["decay"]
            dq = dqd * c["eb"]
            dk = dkd * c["enb"] + dke * c["ee"]
            db = dqd * c["qd"] - dkd * c["kd"] - dke * c["ke"]
            dbend_rows = jnp.zeros((n, HEAD), F32)
            for b in range(NB):
                dbend_rows = dbend_rows + jnp.where(ex == b, dbend_flat[:, b * HEAD:(b + 1) * HEAD], 0.0)
            dlogf = _dot3(c["anti"].astype(MXU), db) + _dot3(c["same"].astype(MXU), dke * c["ke"]) + dbend_rows
            _, vjp = jax.vjp(_hg_gates, x, lb_)
            dx, dlb = vjp((dlogf, dk))
            if has_in:
                dq = dq + dqi_ref[:, sl]
                dv = dv + dvi_ref[:, sl]
            dq_ref[:, sl] = dq
            dv_ref[:, sl] = dv
            dx_ref[:, sl] = dx
            dlb_ref[:, sl] += dlb

    def cidx(k):
        return _chunk_order(nch - 1 - k, ncc, nch, rev)

    blk = lambda p: pl.BlockSpec((n, d), lambda k: (cidx(k), p))
    vec = pl.BlockSpec((1, d), lambda k: (0, 0))
    in_specs = [blk(0), blk(0), blk(1), blk(2 + d_dir), vec, pl.BlockSpec((1, nh, HEAD, NB * HEAD), lambda k: (cidx(k), 0, 0, 0))]
    args = [do, z, z, z, lb, sin]
    if has_in:
        in_specs += [blk(0), blk(0)]
        args += [dq_in, dv_in]
    rd = jax.ShapeDtypeStruct((r, d), F32)
    return pl.pallas_call(
        body, name=name, grid=(nch,), in_specs=in_specs, out_specs=(blk(0), blk(0), blk(0), vec),
        out_shape=(rd, rd, rd, jax.ShapeDtypeStruct((1, d), F32)),
        scratch_shapes=[pltpu.VMEM((nh, HEAD, NB * HEAD), F32)], compiler_params=_cp(("arbitrary",)))(*args)


def _hg_read(o, g, gw):
    on = o * lax.rsqrt(jnp.mean(o * o, axis=-1, keepdims=True) + NORM_EPS) * gw
    return on * jax.nn.sigmoid(g)


def hg_read_fwd(of, ob, z, gw, *, name):
    r, d = of.shape
    nh = d // HEAD
    tm = _row_tile(r)

    def body(of_ref, ob_ref, g_ref, gw_ref, o_ref):
        for h in range(nh):
            sl = slice(h * HEAD, (h + 1) * HEAD)
            o_ref[:, sl] = _hg_read(of_ref[:, sl] + ob_ref[:, sl], g_ref[:, sl], gw_ref[...]).astype(MXU)

    blk = pl.BlockSpec((tm, d), lambda i: (i, 0))
    return pl.pallas_call(
        body, name=name, grid=(r // tm,),
        in_specs=[blk, blk, pl.BlockSpec((tm, d), lambda i: (i, N_PROJ - 1)), pl.BlockSpec((1, HEAD), lambda i: (0, 0))],
        out_specs=blk, out_shape=jax.ShapeDtypeStruct((r, d), MXU), compiler_params=_cp(("parallel",)))(of, ob, z, gw)


def hg_read_bwd(don, of, ob, z, gw, *, name):
    r, d = of.shape
    nh = d // HEAD
    tm = _row_tile(r)

    def body(don_ref, of_ref, ob_ref, g_ref, gw_ref, do_ref, dg_ref, dgw_ref):
        @pl.when(pl.program_id(0) == 0)
        def _():
            dgw_ref[...] = jnp.zeros_like(dgw_ref)

        for h in range(nh):
            sl = slice(h * HEAD, (h + 1) * HEAD)
            _, vjp = jax.vjp(_hg_read, of_ref[:, sl] + ob_ref[:, sl], g_ref[:, sl], gw_ref[...])
            do_ref[:, sl], dg_ref[:, sl], dgw = vjp(don_ref[:, sl])
            dgw_ref[...] += dgw

    blk = pl.BlockSpec((tm, d), lambda i: (i, 0))
    vec = pl.BlockSpec((1, HEAD), lambda i: (0, 0))
    rd = jax.ShapeDtypeStruct((r, d), F32)
    return pl.pallas_call(
        body, name=name, grid=(r // tm,),
        in_specs=[blk, blk, blk, pl.BlockSpec((tm, d), lambda i: (i, N_PROJ - 1)), vec],
        out_specs=(blk, blk, vec), out_shape=(rd, rd, jax.ShapeDtypeStruct((1, HEAD), F32)),
        compiler_params=_cp(("arbitrary",)))(don, of, ob, z, gw)


FFN_COLS = 256


def _seg_masks(cfg, tr, i):
    t = lax.broadcasted_iota(jnp.int32, (tr, FFN_COLS), 0) // NB
    ctx_steps = cfg["rc"] // NB
    pos = jnp.where(i == 0, t % ctx_steps, t % GRID_W)
    last = jnp.where(i == 0, ctx_steps - 1, GRID_W - 1)
    return pos == 0, pos == last


def _prev(x, start):
    return jnp.where(start, 0.0, pltpu.roll(x, NB, 0))


def _next(x, end):
    return jnp.where(end, 0.0, pltpu.roll(x, x.shape[0] - NB, 0))


def _conv3(u, w, b, start, end):
    return ((b + _prev(u, start) * w[0:1]) + u * w[1:2]) + _next(u, end) * w[2:3]


def ffn_mid_fwd(cfg, u, cw, cb, *, name):
    r, f2 = u.shape
    f = f2 // 2
    tr = cfg["rc"]
    nf = f // FFN_COLS

    def body(ua_ref, ug_ref, wa_ref, wg_ref, ba_ref, bg_ref, o_ref, ca_ref, cg_ref):
        start, end = _seg_masks(cfg, tr, pl.program_id(0))
        a = _conv3(ua_ref[...], wa_ref[...], ba_ref[...], start, end)
        g = _conv3(ug_ref[...], wg_ref[...], bg_ref[...], start, end)
        ca_ref[...] = a.astype(MXU)
        cg_ref[...] = g.astype(MXU)
        o_ref[...] = (_silu(a) * g).astype(MXU)

    ca = lambda rows: pl.BlockSpec((rows, FFN_COLS), lambda i, j: (i if rows == tr else 0, j))
    cg = lambda rows: pl.BlockSpec((rows, FFN_COLS), lambda i, j: (i if rows == tr else 0, j + nf))
    half = jax.ShapeDtypeStruct((r, f), MXU)
    return pl.pallas_call(
        body, name=name, grid=(r // tr, nf), in_specs=[ca(tr), cg(tr), ca(3), cg(3), ca(1), cg(1)], out_specs=(ca(tr), ca(tr), ca(tr)),
        out_shape=(jax.ShapeDtypeStruct((r, f), MXU), half, half), compiler_params=_cp(("parallel", "parallel")))(u, u, cw, cw, cb, cb)


def ffn_mid_bwd(cfg, dact, u, ca, cg, cw, *, name):
    r, f2 = u.shape
    f = f2 // 2
    tr = cfg["rc"]
    nf = f // FFN_COLS

    def body(da_ref, us_ref, ca_ref, cg_ref, ws_ref, du_ref, dcw_ref, dcb_ref):
        i = pl.program_id(1)
        is_a = pl.program_id(0) < nf
        start, end = _seg_masks(cfg, tr, i)

        @pl.when(i == 0)
        def _():
            dcw_ref[...] = jnp.zeros_like(dcw_ref)
            dcb_ref[...] = jnp.zeros_like(dcb_ref)

        def finish(dc):
            us, ws = us_ref[...], ws_ref[...]
            dn, dp = _next(dc, end), _prev(dc, start)
            du_ref[...] = (ws[1:2] * dc + ws[0:1] * dn + ws[2:3] * dp).astype(MXU)
            dcw_ref[...] += jnp.concatenate([jnp.sum(dn * us, axis=0, keepdims=True), jnp.sum(dc * us, axis=0, keepdims=True),
                                             jnp.sum(dp * us, axis=0, keepdims=True)], axis=0)
            dcb_ref[...] += jnp.sum(dc, axis=0, keepdims=True)

        @pl.when(is_a)
        def _():
            cs = ca_ref[...].astype(F32)
            sg = jax.nn.sigmoid(cs)
            finish(da_ref[...].astype(F32) * cg_ref[...].astype(F32) * (sg * (1.0 + cs * (1.0 - sg))))

        @pl.when(jnp.logical_not(is_a))
        def _():
            finish(da_ref[...].astype(F32) * _silu(ca_ref[...].astype(F32)))

    cs_ = lambda rows: pl.BlockSpec((rows, FFN_COLS), lambda j, i: (i if rows == tr else 0, j))
    hf = pl.BlockSpec((tr, FFN_COLS), lambda j, i: (i, j % nf))
    gate = pl.BlockSpec((tr, FFN_COLS), lambda j, i: (jnp.where(j < nf, i, 0), jnp.where(j < nf, j, 0)))
    return pl.pallas_call(
        body, name=name, grid=(2 * nf, r // tr), in_specs=[hf, cs_(tr), hf, gate, cs_(3)], out_specs=(cs_(tr), cs_(3), cs_(1)),
        out_shape=(jax.ShapeDtypeStruct((r, f2), MXU), jax.ShapeDtypeStruct((3, f2), F32), jax.ShapeDtypeStruct((1, f2), F32)),
        compiler_params=_cp(("parallel", "arbitrary")))(dact, u, ca, cg, cw)


def hg_lb_fwd(e0, e1, *, name):
    def body(a, b, o):
        o[...] = _hg_lower_bound(a[...], b[...])

    return pl.pallas_call(body, name=name, out_shape=jax.ShapeDtypeStruct(e0.shape, F32))(e0, e1)


def hg_lb_bwd(e0, e1, dlb, *, name):
    def body(a, b, g, oa, ob):
        _, vjp = jax.vjp(_hg_lower_bound, a[...], b[...])
        oa[...], ob[...] = vjp(g[...])

    s = jax.ShapeDtypeStruct(e0.shape, F32)
    return pl.pallas_call(body, name=name, out_shape=(s, s))(e0, e1, dlb)


def _adamw(w, g, m, v):
    m = ADAM_B1 * m + (1.0 - ADAM_B1) * g
    v = ADAM_B2 * v + (1.0 - ADAM_B2) * jnp.square(g)
    m_hat = m / (1.0 - ADAM_B1 ** ADAM_STEP)
    v_hat = v / (1.0 - ADAM_B2 ** ADAM_STEP)
    delta = -ADAM_LR * (m_hat / (jnp.sqrt(v_hat) + ADAM_EPS) + ADAM_WD * w)
    return delta, m, v


def _as2d(a):
    if a.ndim >= 2 and a.shape[-1] % 128 == 0:
        return a.reshape(-1, a.shape[-1])
    return a.reshape(-1, 128) if a.size % 128 == 0 else a.reshape(1, -1)


def adamw(w, g, m, v, *, name):
    w2 = _as2d(w)
    outs = rowmap(_adamw, [w2, _as2d(g), _as2d(m), _as2d(v)], [], [(w2.shape[1], F32)] * 3, name=name)
    return tuple(o.reshape(w.shape) for o in outs)


HBM_SPEC = pl.BlockSpec(memory_space=pltpu.HBM)


def _place():
    mx, my, mc = lax.axis_index("x"), lax.axis_index("y"), lax.axis_index("c")
    others = [(1 - mx, my), (mx, 1 - my), (1 - mx, 1 - my)]
    return mx, my, mc, others


def chip_allgather(x, *, name):
    def body(x_ref, o_ref, send_sems, recv_sems, local_sem):
        mx, my, mc, others = _place()
        me = 2 * mx + my
        mine = pltpu.make_async_copy(x_ref, o_ref.at[me], local_sem)
        mine.start()
        sends = [pltpu.make_async_remote_copy(src_ref=x_ref, dst_ref=o_ref.at[me], send_sem=send_sems.at[j], recv_sem=recv_sems.at[j],
                                              device_id=(px, py, mc), device_id_type=MESH) for j, (px, py) in enumerate(others)]
        for cp in sends:
            cp.start()
        for j, (px, py) in enumerate(others):
            pltpu.make_async_remote_copy(src_ref=x_ref, dst_ref=o_ref.at[2 * px + py], send_sem=send_sems.at[j], recv_sem=recv_sems.at[j],
                                         device_id=(px, py, mc), device_id_type=MESH).wait_recv()
        for cp in sends:
            cp.wait_send()
        mine.wait()

    return pl.pallas_call(
        body, name=name, out_shape=jax.ShapeDtypeStruct((4,) + x.shape, x.dtype), in_specs=[HBM_SPEC], out_specs=HBM_SPEC,
        scratch_shapes=[pltpu.SemaphoreType.DMA((3,)), pltpu.SemaphoreType.DMA((3,)), pltpu.SemaphoreType.DMA])(x)


def _win(ref, axis, start, size):
    idx = [slice(None)] * len(ref.shape)
    idx[axis] = pl.ds(start, size)
    return ref.at[tuple(idx)]


def _half_axis(shape, ax):
    if shape[0] == 2:
        return 0
    return 2 if ax == 1 else 1


def _cut(shape, axis, parts):
    return shape[:axis] + (shape[axis] // parts,) + shape[axis + 1:]


def _hbm_call(body, arrays, out_shapes, sems, name):
    n_in = len(arrays)
    return pl.pallas_call(body, name=name, out_shape=tuple(out_shapes), in_specs=[HBM_SPEC] * n_in, out_specs=tuple([HBM_SPEC] * len(out_shapes)),
                          scratch_shapes=sems)(*arrays)


def place_shard(shard, ax, chip, dtype, *, name):
    l, r, c = shard.shape
    tr = _row_tile(r, c)
    per_block = (l, r // tr, 1)[ax]

    def omap(li, ri, cref):
        idx = [li, ri, 0]
        idx[ax] = idx[ax] + cref[0] * per_block
        return tuple(idx)

    def body(c_ref, s_ref, o_ref):
        o_ref[...] = s_ref[...].astype(dtype)

    full = shard.shape[:ax] + (4 * shard.shape[ax],) + shard.shape[ax + 1:]
    return pl.pallas_call(
        body, name=name, out_shape=jax.ShapeDtypeStruct(full, dtype),
        grid_spec=pltpu.PrefetchScalarGridSpec(
            num_scalar_prefetch=1, grid=(l, r // tr),
            in_specs=[pl.BlockSpec((1, tr, c), lambda li, ri, cref: (li, ri, 0))], out_specs=pl.BlockSpec((1, tr, c), omap)),
        compiler_params=_cp(("parallel", "parallel")))(chip, shard)


def gather_placed(arrays, axes, haxes, *, name):
    n = len(arrays)

    def body(*refs):
        ins, outs = refs[:n], refs[n:2 * n]
        send_sems, recv_sems = refs[2 * n:]
        mx, my, mc, others = _place()
        me = 2 * mx + my

        def part(ref, i, chip):
            sz, hs = arrays[i].shape[axes[i]] // 4, arrays[i].shape[haxes[i]] // 2
            return _win(_win(ref, axes[i], chip * sz, sz), haxes[i], mc * hs, hs)

        sends = []
        for i in range(n):
            for j, (px, py) in enumerate(others):
                rc = pltpu.make_async_remote_copy(src_ref=part(ins[i], i, me), dst_ref=part(outs[i], i, me), send_sem=send_sems.at[i, j],
                                                  recv_sem=recv_sems.at[i, j], device_id=(px, py, mc), device_id_type=MESH)
                rc.start()
                sends.append(rc)
        for i in range(n):
            for j, (px, py) in enumerate(others):
                pltpu.make_async_remote_copy(src_ref=part(ins[i], i, me), dst_ref=part(outs[i], i, 2 * px + py), send_sem=send_sems.at[i, j],
                                             recv_sem=recv_sems.at[i, j], device_id=(px, py, mc), device_id_type=MESH).wait_recv()
        for rc in sends:
            rc.wait_send()

    return pl.pallas_call(
        body, name=name, out_shape=tuple(jax.ShapeDtypeStruct(a_.shape, a_.dtype) for a_ in arrays), in_specs=[HBM_SPEC] * n,
        out_specs=tuple([HBM_SPEC] * n), input_output_aliases={i: i for i in range(n)},
        scratch_shapes=[pltpu.SemaphoreType.DMA((n, 3)), pltpu.SemaphoreType.DMA((n, 3))])(*arrays)


SEM_SPEC = pl.BlockSpec(memory_space=pltpu.SEMAPHORE)
SPLIT_COPY = pltpu.CompilerParams(has_side_effects=pltpu.SideEffectType.DATAFLOW_SIDE_EFFECTING)


def _gather_part(ref, shape, ax, hax, chip, core):
    sz, hs = shape[ax] // 4, shape[hax] // 2
    return _win(_win(ref, ax, chip * sz, sz), hax, core * hs, hs)


def gather_placed_start(arrays, axes, haxes, after, *, name):
    n = len(arrays)

    m = 3 * n

    def body(*refs):
        ins, send_sems, recv_sems = refs[:n], refs[n + 1:n + 1 + m], refs[n + 1 + m:n + 1 + 2 * m]
        token = refs[2 * n + 1 + 2 * m]
        mx, my, mc, others = _place()
        me = 2 * mx + my
        for i in range(n):
            for j, (px, py) in enumerate(others):
                part = _gather_part(ins[i], arrays[i].shape, axes[i], haxes[i], me, mc)
                pltpu.make_async_remote_copy(src_ref=part, dst_ref=part, send_sem=send_sems[3 * i + j], recv_sem=recv_sems[3 * i + j],
                                             device_id=(px, py, mc), device_id_type=MESH).start()
        token[...] = jnp.zeros_like(token)

    hbm = [pltpu.with_memory_space_constraint(a_, pltpu.HBM) for a_ in arrays]
    out = pl.pallas_call(
        body, name=name,
        out_shape=tuple([pltpu.SemaphoreType.DMA(())] * (2 * m)) + tuple(pltpu.HBM(a_.shape, a_.dtype) for a_ in arrays)
        + (jax.ShapeDtypeStruct((8, 128), F32),),
        in_specs=[HBM_SPEC] * n + [pl.BlockSpec(memory_space=pl.ANY)],
        out_specs=tuple([SEM_SPEC] * (2 * m)) + tuple([HBM_SPEC] * n) + (pl.BlockSpec(memory_space=pltpu.VMEM),),
        input_output_aliases={i: 2 * m + i for i in range(n)}, compiler_params=SPLIT_COPY)(*hbm, after)
    return list(out[:m]), list(out[m:2 * m]), list(out[2 * m:2 * m + n]), out[2 * m + n]


def gather_placed_wait(arrays, send_sems, recv_sems, axes, haxes, after, *, name):
    n = len(arrays)

    m = 3 * n

    def body(*refs):
        ins, send_refs, recv_refs = refs[:n], refs[n:n + m], refs[n + m:n + 2 * m]
        mx, my, mc, others = _place()
        me = 2 * mx + my
        for i in range(n):
            for j, (px, py) in enumerate(others):
                cp = pltpu.make_async_remote_copy(
                    src_ref=_gather_part(ins[i], arrays[i].shape, axes[i], haxes[i], me, mc),
                    dst_ref=_gather_part(ins[i], arrays[i].shape, axes[i], haxes[i], 2 * px + py, mc),
                    send_sem=send_refs[3 * i + j], recv_sem=recv_refs[3 * i + j], device_id=(px, py, mc), device_id_type=MESH)
                cp.wait_send()
                cp.wait_recv()

    out = pl.pallas_call(
        body, name=name, out_shape=tuple(pltpu.HBM(a_.shape, a_.dtype) for a_ in arrays),
        in_specs=[HBM_SPEC] * n + [SEM_SPEC] * (2 * m) + [pl.BlockSpec(memory_space=pl.ANY)], out_specs=tuple([HBM_SPEC] * n),
        input_output_aliases={i: i for i in range(n)}, compiler_params=SPLIT_COPY)(*arrays, *send_sems, *recv_sems, after)
    return list(out)


def pair_swap_halves(arrays, haxes, *, name):
    n = len(arrays)

    def body(*refs):
        ins, outs = refs[:n], refs[n:2 * n]
        send_sems, recv_sems = refs[2 * n:]
        mx, my, mc, _ = _place()
        cps = []
        for i in range(n):
            hs = arrays[i].shape[haxes[i]] // 2
            cp = pltpu.make_async_remote_copy(src_ref=_win(ins[i], haxes[i], (1 - mc) * hs, hs), dst_ref=outs[i], send_sem=send_sems.at[i],
                                              recv_sem=recv_sems.at[i], device_id=(mx, my, 1 - mc), device_id_type=MESH)
            cp.start()
            cps.append(cp)
        for cp in cps:
            cp.wait()

    outs = [jax.ShapeDtypeStruct(_cut(a_.shape, h_, 2), a_.dtype) for a_, h_ in zip(arrays, haxes)]
    return _hbm_call(body, arrays, outs, [pltpu.SemaphoreType.DMA((n,)), pltpu.SemaphoreType.DMA((n,))], name)


def add_own_half(g, t, hax, core, *, out_dtype, name):
    l, r, c = t.shape
    tr = _row_tile(r, c)
    per_half = (l, r // tr, 1)[hax]

    def imap(li, ri, cref):
        idx = [li, ri, 0]
        idx[hax] = idx[hax] + cref[0] * per_half
        return tuple(idx)

    def body(c_ref, g_ref, t_ref, o_ref):
        o_ref[...] = (g_ref[...] + t_ref[...]).astype(out_dtype)

    return pl.pallas_call(
        body, name=name, out_shape=jax.ShapeDtypeStruct(t.shape, out_dtype),
        grid_spec=pltpu.PrefetchScalarGridSpec(
            num_scalar_prefetch=1, grid=(l, r // tr),
            in_specs=[pl.BlockSpec((1, tr, c), imap), pl.BlockSpec((1, tr, c), lambda li, ri, cref: (li, ri, 0))],
            out_specs=pl.BlockSpec((1, tr, c), lambda li, ri, cref: (li, ri, 0))),
        compiler_params=_cp(("parallel", "parallel")))(core, g, t)


def exchange_blocks(arrays, axes, *, name):
    n = len(arrays)

    def body(*refs):
        ins, outs = refs[:n], refs[n:2 * n]
        send_sems, recv_sems, local_sems = refs[2 * n:]
        mx, my, mc, others = _place()
        me = 2 * mx + my
        waits = []
        for i in range(n):
            sz = arrays[i].shape[axes[i]] // 4
            cp = pltpu.make_async_copy(_win(ins[i], axes[i], me * sz, sz), outs[i].at[me], local_sems.at[i])
            cp.start()
            waits.append(cp.wait)
            for j, (px, py) in enumerate(others):
                rc = pltpu.make_async_remote_copy(src_ref=_win(ins[i], axes[i], (2 * px + py) * sz, sz), dst_ref=outs[i].at[me],
                                                  send_sem=send_sems.at[i, j], recv_sem=recv_sems.at[i, j], device_id=(px, py, mc),
                                                  device_id_type=MESH)
                rc.start()
                waits.append(rc.wait_send)
        for i in range(n):
            sz = arrays[i].shape[axes[i]] // 4
            for j, (px, py) in enumerate(others):
                pltpu.make_async_remote_copy(src_ref=_win(ins[i], axes[i], me * sz, sz), dst_ref=outs[i].at[2 * px + py],
                                             send_sem=send_sems.at[i, j], recv_sem=recv_sems.at[i, j], device_id=(px, py, mc),
                                             device_id_type=MESH).wait_recv()
        for w_ in waits:
            w_()

    outs = [jax.ShapeDtypeStruct((4,) + _cut(a_.shape, ax, 4), a_.dtype) for a_, ax in zip(arrays, axes)]
    return _hbm_call(body, arrays, outs, [pltpu.SemaphoreType.DMA((n, 3)), pltpu.SemaphoreType.DMA((n, 3)), pltpu.SemaphoreType.DMA((n,))], name)


def exchange_blocks_start(arrays, axes, *, name):
    n = len(arrays)
    lands = [lax.empty((4,) + _cut(a_.shape, ax, 4), a_.dtype) for a_, ax in zip(arrays, axes)]

    def body(*refs):
        ins, lnd = refs[:n], refs[n:2 * n]
        send_sems, recv_sems = refs[2 * n:6 * n], refs[6 * n:9 * n]
        token = refs[11 * n]
        mx, my, mc, others = _place()
        me = 2 * mx + my
        for i in range(n):
            sz = arrays[i].shape[axes[i]] // 4
            pltpu.make_async_copy(_win(ins[i], axes[i], me * sz, sz), lnd[i].at[me], send_sems[4 * i + 3]).start()
            for j, (px, py) in enumerate(others):
                pltpu.make_async_remote_copy(src_ref=_win(ins[i], axes[i], (2 * px + py) * sz, sz), dst_ref=lnd[i].at[me],
                                             send_sem=send_sems[4 * i + j], recv_sem=recv_sems[3 * i + j], device_id=(px, py, mc),
                                             device_id_type=MESH).start()
        token[...] = jnp.zeros_like(token)

    hbm = [pltpu.with_memory_space_constraint(a_, pltpu.HBM) for a_ in arrays + lands]
    out = pl.pallas_call(
        body, name=name,
        out_shape=tuple([pltpu.SemaphoreType.DMA(())] * (7 * n)) + tuple(pltpu.HBM(a_.shape, a_.dtype) for a_ in arrays + lands)
        + (jax.ShapeDtypeStruct((8, 128), F32),),
        in_specs=[HBM_SPEC] * (2 * n),
        out_specs=tuple([SEM_SPEC] * (7 * n)) + tuple([HBM_SPEC] * (2 * n)) + (pl.BlockSpec(memory_space=pltpu.VMEM),),
        input_output_aliases={i: 7 * n + i for i in range(2 * n)}, compiler_params=SPLIT_COPY)(*hbm)
    return list(out[:7 * n]), list(out[7 * n:8 * n]), list(out[8 * n:9 * n]), out[9 * n]


def exchange_blocks_wait(sems, arrays, lands, axes, after, *, name):
    n = len(arrays)

    def body(*refs):
        ins, lnd = refs[:n], refs[n:2 * n]
        send_sems, recv_sems = refs[2 * n:6 * n], refs[6 * n:9 * n]
        mx, my, mc, others = _place()
        me = 2 * mx + my
        for i in range(n):
            sz = arrays[i].shape[axes[i]] // 4
            mine = _win(ins[i], axes[i], me * sz, sz)
            pltpu.make_async_copy(mine, lnd[i].at[me], send_sems[4 * i + 3]).wait()
            for j, (px, py) in enumerate(others):
                cp = pltpu.make_async_remote_copy(src_ref=mine, dst_ref=lnd[i].at[2 * px + py], send_sem=send_sems[4 * i + j],
                                                  recv_sem=recv_sems[3 * i + j], device_id=(px, py, mc), device_id_type=MESH)
                cp.wait_send()
                cp.wait_recv()

    out = pl.pallas_call(
        body, name=name, out_shape=tuple(pltpu.HBM(a_.shape, a_.dtype) for a_ in arrays + lands),
        in_specs=[HBM_SPEC] * (2 * n) + [SEM_SPEC] * (7 * n) + [pl.BlockSpec(memory_space=pl.ANY)], out_specs=tuple([HBM_SPEC] * (2 * n)),
        input_output_aliases={i: i for i in range(2 * n)}, compiler_params=SPLIT_COPY)(*arrays, *lands, *sems, after)
    return list(out[n:])


def sum_blocks(e, hax, core, *, name):
    _, l, r, c = e.shape
    tr = _row_tile(r, c)
    per_half = (l, r // tr, 1)[hax]

    def omap(li, ri, cref):
        idx = [li, ri, 0]
        idx[hax] = idx[hax] + cref[0] * per_half
        return tuple(idx)

    def body(c_ref, e_ref, o_ref):
        v = e_ref[...].astype(F32)
        o_ref[...] = ((v[0] + v[1]) + v[2]) + v[3]

    full = (l, r, c)[:hax] + (2 * (l, r, c)[hax],) + (l, r, c)[hax + 1:]
    return pl.pallas_call(
        body, name=name, out_shape=jax.ShapeDtypeStruct(full, F32),
        grid_spec=pltpu.PrefetchScalarGridSpec(
            num_scalar_prefetch=1, grid=(l, r // tr),
            in_specs=[pl.BlockSpec((4, 1, tr, c), lambda li, ri, cref: (0, li, ri, 0))], out_specs=pl.BlockSpec((1, tr, c), omap)),
        compiler_params=_cp(("parallel", "parallel")))(core, e)


def pair_fill_halves(arrays, haxes, *, name):
    n = len(arrays)

    def body(*refs):
        ins, outs = refs[:n], refs[n:2 * n]
        send_sems, recv_sems = refs[2 * n:]
        mx, my, mc, _ = _place()
        cps = []
        for i in range(n):
            hs = arrays[i].shape[haxes[i]] // 2
            mine = _win(ins[i], haxes[i], mc * hs, hs)
            cp = pltpu.make_async_remote_copy(src_ref=mine, dst_ref=_win(outs[i], haxes[i], mc * hs, hs), send_sem=send_sems.at[i],
                                              recv_sem=recv_sems.at[i], device_id=(mx, my, 1 - mc), device_id_type=MESH)
            cp.start()
            cps.append(cp)
        for i in range(n):
            hs = arrays[i].shape[haxes[i]] // 2
            pltpu.make_async_remote_copy(src_ref=_win(ins[i], haxes[i], mc * hs, hs), dst_ref=_win(outs[i], haxes[i], (1 - mc) * hs, hs),
                                         send_sem=send_sems.at[i], recv_sem=recv_sems.at[i], device_id=(mx, my, 1 - mc),
                                         device_id_type=MESH).wait_recv()
        for cp in cps:
            cp.wait_send()

    return pl.pallas_call(
        body, name=name, out_shape=tuple(jax.ShapeDtypeStruct(a_.shape, a_.dtype) for a_ in arrays), in_specs=[HBM_SPEC] * n,
        out_specs=tuple([HBM_SPEC] * n), input_output_aliases={i: i for i in range(n)},
        scratch_shapes=[pltpu.SemaphoreType.DMA((n,)), pltpu.SemaphoreType.DMA((n,))])(*arrays)


WEIGHTS = ['c_ctx', 'w_mod', 'b_mod', 'norm1_w', 'norm2_w', 'final_norm_w', 's5_w_in', 's5_lam_re', 's5_lam_im', 's5_log_step', 's5_b_re', 's5_b_im', 's5_c_re', 's5_c_im', 's5_d', 's5_w_glu', 's5_w_out', 'hg_w_in', 'hg_lower_bounds', 'hg_gnorm_w', 'hg_w_out', 'ffn_w_up', 'ffn_conv_w', 'ffn_conv_b', 'ffn_w_down']
INPUTS = ['x', 'c', 'ctx', 'c_ctx', 'w_mod', 'b_mod', 'norm1_w', 'norm2_w', 'final_norm_w', 's5_w_in', 's5_lam_re', 's5_lam_im', 's5_log_step', 's5_b_re', 's5_b_im', 's5_c_re', 's5_c_im', 's5_d', 's5_w_glu', 's5_w_out', 'hg_w_in', 'hg_lower_bounds', 'hg_gnorm_w', 'hg_w_out', 'ffn_w_up', 'ffn_conv_w', 'ffn_conv_b', 'ffn_w_down', 'loss_target', 'm_c_ctx', 'm_w_mod', 'm_b_mod', 'm_norm1_w', 'm_norm2_w', 'm_final_norm_w', 'm_s5_w_in', 'm_s5_lam_re', 'm_s5_lam_im', 'm_s5_log_step', 'm_s5_b_re', 'm_s5_b_im', 'm_s5_c_re', 'm_s5_c_im', 'm_s5_d', 'm_s5_w_glu', 'm_s5_w_out', 'm_hg_w_in', 'm_hg_lower_bounds', 'm_hg_gnorm_w', 'm_hg_w_out', 'm_ffn_w_up', 'm_ffn_conv_w', 'm_ffn_conv_b', 'm_ffn_w_down', 'v_c_ctx', 'v_w_mod', 'v_b_mod', 'v_norm1_w', 'v_norm2_w', 'v_final_norm_w', 'v_s5_w_in', 'v_s5_lam_re', 'v_s5_lam_im', 'v_s5_log_step', 'v_s5_b_re', 'v_s5_b_im', 'v_s5_c_re', 'v_s5_c_im', 'v_s5_d', 'v_s5_w_glu', 'v_s5_w_out', 'v_hg_w_in', 'v_hg_lower_bounds', 'v_hg_gnorm_w', 'v_hg_w_out', 'v_ffn_w_up', 'v_ffn_conv_w', 'v_ffn_conv_b', 'v_ffn_w_down']
SHARD_AXIS = {"w_mod": 2, "s5_w_in": 1, "s5_w_glu": 1, "s5_w_out": 1, "hg_w_in": 2, "hg_lower_bounds": 2, "hg_w_out": 1,
              "ffn_w_up": 2, "ffn_conv_w": 2, "ffn_w_down": 1}
GATHER_F32 = ("hg_lower_bounds", "ffn_conv_w")
PACK_W = 1024
GRAD_WIRE = jnp.bfloat16


def _reduce_start(items, core, tag):
    names, arrays, axes = [n for n, _, _ in items], [g_ for _, g_, _ in items], [ax for _, _, ax in items]
    haxes = [_half_axis(g_.shape, ax) for g_, ax in zip(arrays, axes)]
    t = pair_swap_halves(arrays, haxes, name="grad_pair_swap_" + tag)
    h = [add_own_half(g_, t_, hx, core, out_dtype=GRAD_WIRE, name="grad_pair_add_" + n) for g_, t_, hx, n in zip(arrays, t, haxes, names)]
    sems, h, lands, token = exchange_blocks_start(h, axes, name="grad_exchange_start_" + tag)
    return (names, sems, h, lands, axes, haxes), token


def _reduce_finish(state, core, after, tag):
    names, sems, h, lands, axes, haxes = state
    e = exchange_blocks_wait(sems, h, lands, axes, after, name="grad_exchange_wait_" + tag)
    s = [sum_blocks(e_, hx, core, name="grad_chip_sum_" + n) for e_, hx, n in zip(e, haxes, names)]
    return dict(zip(names, pair_fill_halves(s, haxes, name="grad_pair_fill_" + tag)))


def _reduce_now(a, items, small, grads, core):
    flat = jnp.concatenate([grads[n].reshape(-1) for n in small])
    pad = (-flat.shape[0]) % (64 * PACK_W)
    small_pack = jnp.pad(flat, (0, pad)).reshape(1, -1, PACK_W)
    names = [n for n, _, _ in items] + ["small"]
    arrays = [g_ for _, g_, _ in items] + [small_pack]
    axes = [ax for _, _, ax in items] + [1]
    haxes = [_half_axis(g_.shape, ax) for g_, ax in zip(arrays, axes)]
    t = pair_swap_halves(arrays, haxes, name="grad_pair_swap")
    h = [add_own_half(g_, t_, hx, core, out_dtype=GRAD_WIRE, name="grad_pair_add_" + n) for g_, t_, hx, n in zip(arrays, t, haxes, names)]
    e = exchange_blocks(h, axes, name="grad_chip_exchange")
    s = [sum_blocks(e_, hx, core, name="grad_chip_sum_" + n) for e_, hx, n in zip(e, haxes, names)]
    red = pair_fill_halves(s, haxes, name="grad_pair_fill")
    out = dict(zip(names[:-1], red[:-1]))
    sm = chip_allgather(red[-1][0], name="allgather_small_grads").reshape(-1)
    off = 0
    for n in small:
        out[n] = sm[off:off + math.prod(a[n].shape)].reshape(a[n].shape)
        off += math.prod(a[n].shape)
    return out


def _blockdiag_b(bb, kb):
    gl = S5_KIN // S5_GROUP
    x = bb.reshape(kb, gl, S5_GROUP, S5_STATE)
    return (x[:, :, :, None, :] * jnp.eye(gl, dtype=bb.dtype)[None, :, None, :, None]).reshape(kb, S5_KIN, S5_KST)


def _blockdiag_c(cc, kb):
    gl = S5_KIN // S5_GROUP
    x = cc.reshape(kb, gl, S5_GROUP, S5_STATE).transpose(0, 1, 3, 2)
    return (x[:, :, :, None, :] * jnp.eye(gl, dtype=cc.dtype)[None, :, None, :, None]).reshape(kb, S5_KST, S5_KIN)


def _diag_b(m, kb):
    gl = S5_KIN // S5_GROUP
    x = m.reshape(kb, gl, S5_GROUP, gl, S5_STATE)
    return jnp.stack([x[:, i, :, i, :] for i in range(gl)], axis=1).reshape(kb * gl, S5_GROUP, S5_STATE)


def _diag_c(m, kb):
    gl = S5_KIN // S5_GROUP
    x = m.reshape(kb, gl, S5_STATE, gl, S5_GROUP)
    return jnp.stack([x[:, i, :, i, :] for i in range(gl)], axis=1).transpose(0, 1, 3, 2).reshape(kb * gl, S5_GROUP, S5_STATE)


def kernel(x, c, ctx, c_ctx, w_mod, b_mod, norm1_w, norm2_w, final_norm_w, s5_w_in, s5_lam_re, s5_lam_im, s5_log_step, s5_b_re, s5_b_im, s5_c_re, s5_c_im, s5_d, s5_w_glu, s5_w_out, hg_w_in, hg_lower_bounds, hg_gnorm_w, hg_w_out, ffn_w_up, ffn_conv_w, ffn_conv_b, ffn_w_down, loss_target, m_c_ctx, m_w_mod, m_b_mod, m_norm1_w, m_norm2_w, m_final_norm_w, m_s5_w_in, m_s5_lam_re, m_s5_lam_im, m_s5_log_step, m_s5_b_re, m_s5_b_im, m_s5_c_re, m_s5_c_im, m_s5_d, m_s5_w_glu, m_s5_w_out, m_hg_w_in, m_hg_lower_bounds, m_hg_gnorm_w, m_hg_w_out, m_ffn_w_up, m_ffn_conv_w, m_ffn_conv_b, m_ffn_w_down, v_c_ctx, v_w_mod, v_b_mod, v_norm1_w, v_norm2_w, v_final_norm_w, v_s5_w_in, v_s5_lam_re, v_s5_lam_im, v_s5_log_step, v_s5_b_re, v_s5_b_im, v_s5_c_re, v_s5_c_im, v_s5_d, v_s5_w_glu, v_s5_w_out, v_hg_w_in, v_hg_lower_bounds, v_hg_gnorm_w, v_hg_w_out, v_ffn_w_up, v_ffn_conv_w, v_ffn_conv_b, v_ffn_w_down):
    a = dict(zip(INPUTS, (x, c, ctx, c_ctx, w_mod, b_mod, norm1_w, norm2_w, final_norm_w, s5_w_in, s5_lam_re, s5_lam_im, s5_log_step, s5_b_re, s5_b_im, s5_c_re, s5_c_im, s5_d, s5_w_glu, s5_w_out, hg_w_in, hg_lower_bounds, hg_gnorm_w, hg_w_out, ffn_w_up, ffn_conv_w, ffn_conv_b, ffn_w_down, loss_target, m_c_ctx, m_w_mod, m_b_mod, m_norm1_w, m_norm2_w, m_final_norm_w, m_s5_w_in, m_s5_lam_re, m_s5_lam_im, m_s5_log_step, m_s5_b_re, m_s5_b_im, m_s5_c_re, m_s5_c_im, m_s5_d, m_s5_w_glu, m_s5_w_out, m_hg_w_in, m_hg_lower_bounds, m_hg_gnorm_w, m_hg_w_out, m_ffn_w_up, m_ffn_conv_w, m_ffn_conv_b, m_ffn_w_down, v_c_ctx, v_w_mod, v_b_mod, v_norm1_w, v_norm2_w, v_final_norm_w, v_s5_w_in, v_s5_lam_re, v_s5_lam_im, v_s5_log_step, v_s5_b_re, v_s5_b_im, v_s5_c_re, v_s5_c_im, v_s5_d, v_s5_w_glu, v_s5_w_out, v_hg_w_in, v_hg_lower_bounds, v_hg_gnorm_w, v_hg_w_out, v_ffn_w_up, v_ffn_conv_w, v_ffn_conv_b, v_ffn_w_down)))
    nb, seq, d = x.shape
    assert nb == NB
    rc = nb * ctx.shape[1]
    cfg = {"rc": rc}
    f = a["ffn_w_down"].shape[1] * 4
    core = lax.axis_index("c").astype(jnp.int32).reshape(1)

    w = {n: a[n] for n in WEIGHTS if n not in SHARD_AXIS}
    chip = (2 * lax.axis_index("x") + lax.axis_index("y")).astype(jnp.int32).reshape(1)
    groups = {
        "now": [("w_mod0", a["w_mod"][0:1]), ("s5_w_in", a["s5_w_in"]), ("hg_lower_bounds", a["hg_lower_bounds"]), ("ffn_conv_w", a["ffn_conv_w"])],
        "mid": [("s5_w_glu", a["s5_w_glu"]), ("s5_w_out", a["s5_w_out"]), ("ffn_w_up0", a["ffn_w_up"][0:1]), ("ffn_w_down0", a["ffn_w_down"][0:1])],
        "later": [("w_mod1", a["w_mod"][1:2]), ("hg_w_in", a["hg_w_in"]), ("hg_w_out", a["hg_w_out"]), ("ffn_w_up1", a["ffn_w_up"][1:2]),
                  ("ffn_w_down1", a["ffn_w_down"][1:2])]}
    shard_axis = lambda n: SHARD_AXIS[n.rstrip("01")]
    placed = {g: [place_shard(s_, shard_axis(n), chip, F32 if n in GATHER_F32 else MXU, name="place_" + n) for n, s_ in it] for g, it in groups.items()}
    axes = {g: [shard_axis(n) for n, _ in it] for g, it in groups.items()}
    haxes = {g: [_half_axis(p_.shape, ax) for p_, ax in zip(placed[g], axes[g])] for g in groups}
    got = pair_fill_halves(gather_placed(placed["now"], axes["now"], haxes["now"], name="allgather_weights"), haxes["now"],
                           name="allgather_pair_fill")
    w.update(dict(zip([n for n, _ in groups["now"]], got)))
    fly_mid = gather_placed_start(placed["mid"], axes["mid"], haxes["mid"], got[0], name="allgather_mid_start")
    fly_later = gather_placed_start(placed["later"], axes["later"], haxes["later"], fly_mid[3], name="allgather_later_start")

    def land(fly, g, after):
        send_, recv_, flying, _ = fly
        landed = gather_placed_wait(flying, send_, recv_, axes[g], haxes[g], after, name=f"allgather_{g}_wait")
        w.update(dict(zip([n for n, _ in groups[g]], pair_fill_halves(landed, haxes[g], name=f"allgather_{g}_pair_fill"))))

    tmaj = lambda t: t.transpose(1, 0, 2).reshape(-1, t.shape[-1])
    x0 = jnp.concatenate([tmaj(ctx), tmaj(x)], axis=0)
    tgt = tmaj(a["loss_target"])
    c16 = jnp.concatenate([jnp.broadcast_to(c_ctx[None], (8, d)), c, c], axis=0) + fly_later[3][0:1, 0:1]
    mt0, scb = mod_fwd(c16, w["w_mod0"][0], w["b_mod"][0][None], name="mod_fwd0")
    mt = [mt0, None]
    n1, n2 = w["norm1_w"], w["norm2_w"]
    w["w_mod"], w["ffn_w_up"], w["ffn_w_down"] = [w["w_mod0"][0], None], [None, None], [None, None]

    def ffn_fwd(l, h):
        u = mm(h, w["ffn_w_up"][l], name=f"ffn_up{l}")
        act, ca, cg = ffn_mid_fwd(cfg, u, w["ffn_conv_w"][l], w["ffn_conv_b"][l][None], name=f"ffn_mid{l}")
        return (u, ca, cg), act, mm(act, w["ffn_w_down"][l], name=f"ffn_down{l}")

    def ffn_bwd(l, dfo, kept, act, h, zero=0.0):
        dact = mm(dfo, w["ffn_w_down"][l], tb=True, out_dtype=MXU, name=f"ffn_down_dx{l}")
        dwd = mm(act, dfo, ta=True, name=f"ffn_down_dw{l}")
        du, dcw, dcb = ffn_mid_bwd(cfg, dact, *kept, w["ffn_conv_w"][l] + zero, name=f"ffn_mid_bwd{l}")
        dh = mm(du, w["ffn_w_up"][l], tb=True, name=f"ffn_up_dx{l}")
        dwu = mm(h, du, ta=True, name=f"ffn_up_dw{l}")
        return dh, dwu, dcw, dcb[0], dwd

    g_, p_ = d // S5_GROUP, S5_STATE
    ns, kb = g_ * p_, d // S5_KIN
    s5p = (w["s5_lam_re"][0].reshape(2 * g_, p_), w["s5_lam_im"][0].reshape(2 * g_, p_), w["s5_log_step"][0].reshape(2 * g_, 1),
           w["s5_b_re"][0].transpose(0, 1, 3, 2).reshape(2 * g_, S5_GROUP, p_), w["s5_b_im"][0].transpose(0, 1, 3, 2).reshape(2 * g_, S5_GROUP, p_))
    ar, ai, bbr, bbi = s5_disc_fwd(*s5p, name="s5_disc")
    dsk = w["s5_d"]
    _, h1 = node_fwd(cfg, x0, None, None, 0, n1[0:1], mt[0], 0, name="node0a")
    u0 = mm(h1, w["s5_w_in"][0], name="s5_in")
    s5s, ys = [], []
    for dd in range(2):
        sl = slice(dd * g_, (dd + 1) * g_)
        a_r, a_i = ar[sl].reshape(1, ns), ai[sl].reshape(1, ns)
        a2 = (a_r * a_r - a_i * a_i, 2.0 * a_r * a_i)
        b_r, b_i = _blockdiag_b(bbr[sl], kb), _blockdiag_b(bbi[sl], kb)
        c_r, c_i = _blockdiag_c(w["s5_c_re"][0, dd], kb), _blockdiag_c(w["s5_c_im"][0, dd], kb)
        ak, ai_k = a_r.reshape(kb, 1, S5_KST), a_i.reshape(kb, 1, S5_KST)
        ab = (ak * b_r - ai_k * b_i, ak * b_i + ai_k * b_r)
        akc, aic = ak.reshape(kb, S5_KST, 1), ai_k.reshape(kb, S5_KST, 1)
        c2 = (akc * c_r - aic * c_i, akc * c_i + aic * c_r)
        bf = lambda t_: t_.astype(MXU)
        sre, sim, ere, eim, y_ = s5_scan_fwd(cfg, u0, a2[0], a2[1], bf(b_r), bf(b_i), bf(ab[0]), bf(ab[1]), bf(c_r), bf(c_i), rev=dd == 1,
                                             name=f"s5_scan{dd}")
        s5s.append((sre, sim, ere, eim, a2[0], a2[1], bf(b_r), bf(b_i), bf(c_r), bf(c_i), bf(c2[0]), bf(c2[1])))
        ys.append(y_)

    def glu_a(u, y0, y1, ds):
        yp = (ds * u + y0) + y1
        return yp, _gelu(yp)

    ypre, zgb = rowmap(glu_a, [u0, ys[0], ys[1]], [dsk], [(d, F32), (d, MXU)], name="s5_glu_a")
    land(fly_mid, "mid", zgb)
    w["ffn_w_up"][0], w["ffn_w_down"][0] = w["ffn_w_up0"][0], w["ffn_w_down0"][0]
    tg = mm(zgb, w["s5_w_glu"][0], name="s5_glu")
    (z2,) = rowmap(lambda yp, t: _gelu(yp) * jax.nn.sigmoid(t), [ypre, tg], [], [(d, MXU)], name="s5_glu_b")
    y1a = mm(z2, w["s5_w_out"][0], name="s5_out")
    x1a, h2a = node_fwd(cfg, x0, y1a, mt[0], 2, n2[0:1], mt[0], 3, name="node0b")
    ufa, acta, foa = ffn_fwd(0, h2a)

    land(fly_later, "later", foa)
    w["w_mod"][1], w["ffn_w_up"][1], w["ffn_w_down"][1] = w["w_mod1"][0], w["ffn_w_up1"][0], w["ffn_w_down1"][0]
    mt[1], _ = mod_fwd(c16, w["w_mod"][1], w["b_mod"][1][None], name="mod_fwd1")
    x2a, h1b = node_fwd(cfg, x1a, foa, mt[0], 5, n1[1:2], mt[1], 0, name="node1a")
    z = mm(h1b, w["hg_w_in"][0], name="hg_in")
    e0, e1 = w["hg_lower_bounds"][:, 0, :], w["hg_lower_bounds"][:, 1, :]
    lb = hg_lb_fwd(e0, e1, name="hg_lb")
    gw = w["hg_gnorm_w"]
    o0, sin0 = hg_scan_fwd(cfg, z, lb[0:1], d_dir=0, name="hg_scan0")
    o1, sin1 = hg_scan_fwd(cfg, z, lb[1:2], d_dir=1, name="hg_scan1")
    onb = hg_read_fwd(o0, o1, z, gw, name="hg_read")
    y1b = mm(onb, w["hg_w_out"][0], name="hg_out")
    x1b, h2b = node_fwd(cfg, x2a, y1b, mt[1], 2, n2[1:2], mt[1], 3, name="node1b")
    ufb, actb, fob = ffn_fwd(1, h2b)
    loss_p, dx2b, dfob, dg2_1, dfnw = final_node(cfg, x1b, fob, mt[1], 5, w["final_norm_w"][None], tgt, name="final_node")

    gr = {}
    dh2b, dwu1, dcw1, dcb1, dwd1 = ffn_bwd(1, dfob, ufb, actb, h2b)
    dx1b, dy1b, dn2_1, dsh2_1, dsc2_1, dg1_1 = node_bwd(cfg, dx2b, dh2b, x1b, y1b, mt[1], 2, n2[1:2], mt[1], 3, name="node1b_bwd")
    don = mm(dy1b, w["hg_w_out"][0], tb=True, name="hg_out_dx")
    gr["hg_w_out"] = mm(onb, dy1b, ta=True, name="hg_out_dw")[None]
    do_, dgate_, dgw = hg_read_bwd(don, o0, o1, z, gw, name="hg_read_bwd")
    dq, dv, dxf, dlb0 = hg_scan_bwd(cfg, do_, z, lb[0:1], sin0, None, None, d_dir=0, name="hg_scan_bwd0")
    dq, dv, dxb, dlb1 = hg_scan_bwd(cfg, do_, z, lb[1:2], sin1, dq, dv, d_dir=1, name="hg_scan_bwd1")
    dz = jnp.concatenate([t_.astype(MXU) for t_ in (dq, dv, dxf, dxb, dgate_)], axis=1)
    dh1b = mm(dz, w["hg_w_in"][0], tb=True, name="hg_in_dx")
    gr["hg_w_in"] = mm(h1b, dz, ta=True, name="hg_in_dw")[None]
    de0, de1 = hg_lb_bwd(e0, e1, jnp.concatenate([dlb0, dlb1], axis=0), name="hg_lb_bwd")
    gr["hg_lower_bounds"] = jnp.stack([de0, de1], axis=1)
    gr["hg_gnorm_w"] = dgw
    dx2a, dfoa, dn1_1, dsh1_1, dsc1_1, dg2_0 = node_bwd(cfg, dx1b, dh1b, x2a, foa, mt[0], 5, n1[1:2], mt[1], 0, name="node1a_bwd")
    dmt1 = jnp.concatenate([dsh1_1, dsc1_1, dg1_1, dsh2_1, dsc2_1, dg2_1], axis=1)
    red1, tok1 = _reduce_start([("hg_w_in", gr["hg_w_in"], 2), ("hg_w_out", gr["hg_w_out"], 1), ("ffn_w_up1", dwu1[None], 2),
                                ("ffn_w_down1", dwd1[None], 1), ("w_mod1", mm(scb, dmt1, ta=True, name="mod_dw1")[None], 2)], core, "layer1")

    dh2a, dwu0, dcw0, dcb0, dwd0 = ffn_bwd(0, dfoa, ufa, acta, h2a, zero=tok1[0:1, 0:1])
    red2, tok2 = _reduce_start([("ffn_w_up0", dwu0[None], 2), ("ffn_w_down0", dwd0[None], 1)], core, "ffn0")
    dx1a, dy1a, dn2_0, dsh2_0, dsc2_0, dg1_0 = node_bwd(cfg, dx2a, dh2a, x1a, y1a, mt[0], 2, n2[0:1] + tok2[0:1, 0:1], mt[0], 3,
                                                        name="node0b_bwd")
    dz2 = mm(dy1a, w["s5_w_out"][0], tb=True, name="s5_out_dx")
    gr["s5_w_out"] = mm(z2, dy1a, ta=True, name="s5_out_dw")[None]

    def glu_b_bwd(dz2_, yp, t):
        zg, sg = _gelu(yp), jax.nn.sigmoid(t)
        return dz2_ * zg * sg * (1.0 - sg), dz2_ * sg

    dtg, dzg_dir = rowmap(glu_b_bwd, [dz2, ypre, tg], [], [(d, MXU), (d, F32)], name="s5_glu_b_bwd")
    dzg_mm = mm(dtg, w["s5_w_glu"][0], tb=True, name="s5_glu_dx")
    gr["s5_w_glu"] = mm(zgb, dtg, ta=True, name="s5_glu_dw")[None]

    def glu_a_bwd(dzd, dzm, yp, u, ds):
        _, vjp = jax.vjp(_gelu, yp)
        (dy,) = vjp(dzd + dzm)
        return dy, dy * ds, jnp.sum(dy * u, axis=0, keepdims=True)

    dyb, du, ddsk = rowmap(glu_a_bwd, [dzg_dir, dzg_mm, ypre, u0], [dsk], [(d, MXU), (d, F32)], [(1, d)], name="s5_glu_a_bwd")
    gr["s5_d"] = ddsk
    dar, dai, dbr, dbi, dcr, dci = [], [], [], [], [], []
    for dd in range(2):
        sre, sim, ere, eim = s5s[dd][:4]
        du, gre, gim, da_r, da_i = s5_scan_bwd(cfg, dyb, *s5s[dd], du, rev=dd == 1, name=f"s5_scan_bwd{dd}")
        dar.append(colsum(da_r, name=f"s5_da_re{dd}").reshape(g_, p_))
        dai.append(colsum(da_i, name=f"s5_da_im{dd}").reshape(g_, p_))
        dbr.append(_diag_b(blockdiag_tn(u0, gre, S5_KIN, S5_KST, name=f"s5_db_re{dd}"), kb))
        dbi.append(_diag_b(blockdiag_tn(u0, gim, S5_KIN, S5_KST, name=f"s5_db_im{dd}"), kb))
        dcr.append(_diag_c(blockdiag_tn(sre.reshape(-1, ns), dyb, S5_KST, S5_KIN, name=f"s5_dc_re{dd}"), kb))
        dci.append(_diag_c(blockdiag_tn(sim.reshape(-1, ns), dyb, S5_KST, S5_KIN, scale=-1.0, name=f"s5_dc_im{dd}"), kb))
    cat = lambda l_: jnp.concatenate(l_, axis=0)
    dlr, dli, dls, dbre, dbim = s5_disc_bwd(*s5p, cat(dar), cat(dai), cat(dbr), cat(dbi), name="s5_disc_bwd")
    gr["s5_lam_re"], gr["s5_lam_im"] = dlr.reshape(1, 2, g_, p_), dli.reshape(1, 2, g_, p_)
    gr["s5_log_step"] = dls.reshape(1, 2, g_)
    gr["s5_b_re"] = dbre.reshape(1, 2, g_, S5_GROUP, p_).transpose(0, 1, 2, 4, 3)
    gr["s5_b_im"] = dbim.reshape(1, 2, g_, S5_GROUP, p_).transpose(0, 1, 2, 4, 3)
    gr["s5_c_re"], gr["s5_c_im"] = jnp.stack(dcr)[None], jnp.stack(dci)[None]
    dh1 = mm(du, w["s5_w_in"][0], tb=True, name="s5_in_dx")
    gr["s5_w_in"] = mm(h1, du, ta=True, name="s5_in_dw")[None]
    dx0, _, dn1_0, dsh1_0, dsc1_0, _ = node_bwd(cfg, dx1a, dh1, x0, None, None, 0, n1[0:1], mt[0], 0, name="node0a_bwd")

    dmt = [jnp.concatenate([dsh1_0, dsc1_0, dg1_0, dsh2_0, dsc2_0, dg2_0], axis=1), dmt1]
    gr["b_mod"] = jnp.concatenate([colsum(dmt[l], name=f"mod_db{l}") for l in range(2)], axis=0)
    dsc16 = [mm(dmt[l], w["w_mod"][l], tb=True, name=f"mod_dx{l}") for l in range(2)]
    gr["c_ctx"] = cctx_grad(c16, dsc16, name="c_ctx_grad")[0]
    gr["norm1_w"] = jnp.concatenate([dn1_0, dn1_1], axis=0)
    gr["norm2_w"] = jnp.concatenate([dn2_0, dn2_1], axis=0)
    gr["final_norm_w"] = dfnw[0]
    gr["ffn_conv_w"], gr["ffn_conv_b"] = jnp.stack([dcw0, dcw1]), jnp.stack([dcb0, dcb1])

    last = [(n, gr[n], SHARD_AXIS[n]) for n in ("s5_w_in", "s5_w_glu", "s5_w_out", "hg_lower_bounds", "ffn_conv_w")]
    last.append(("w_mod0", mm(scb, dmt[0], ta=True, name="mod_dw0")[None], 2))
    red = _reduce_now(a, last, [n for n in WEIGHTS if n not in SHARD_AXIS], gr, core)
    red.update(_reduce_finish(red1, core, dx0, "layer1"))
    red.update(_reduce_finish(red2, core, dx0, "ffn0"))
    red["w_mod"] = jnp.concatenate([red["w_mod0"], red["w_mod1"]], axis=0)
    red["ffn_w_up"] = jnp.concatenate([red["ffn_w_up0"], red["ffn_w_up1"]], axis=0)
    red["ffn_w_down"] = jnp.concatenate([red["ffn_w_down0"], red["ffn_w_down1"]], axis=0)
    loss = lax.psum(loss_p[0, 0], ("x", "y", "c"))
    grad_x = dx0[rc:].reshape(seq, nb, d).transpose(1, 0, 2)
    upd = {n: adamw(a[n], red[n], a["m_" + n], a["v_" + n], name="adamw_" + n) for n in WEIGHTS}
    return (loss, grad_x, *[red[n] for n in WEIGHTS], *[upd[n][0] for n in WEIGHTS], *[upd[n][1] for n in WEIGHTS],
            *[upd[n][2] for n in WEIGHTS])
```

```python
import functools
import math

import jax
import jax.numpy as jnp
from jax import lax
from jax.experimental import pallas as pl
from jax.experimental.pallas import tpu as pltpu

F32 = jnp.float32
BF = jnp.bfloat16
MXU = jnp.bfloat16

NORM_EPS = 1e-6
GRID_W = 64
N_MOD = 6
S5_GROUP = 16
S5_STATE = 64
S5_LAM_RE_MAX = -1e-4
S5_KIN = 256
S5_KST = S5_KIN // S5_GROUP * S5_STATE
HEAD = 128
CHUNK_ROWS = 128
N_PROJ = 5
NB = 4
ADAM_LR, ADAM_B1, ADAM_B2, ADAM_EPS, ADAM_WD, ADAM_STEP = 0.001, 0.9, 0.999, 1e-08, 0.01, 10
VMEM_LIMIT = 56 * 1024 * 1024
MESH = pl.DeviceIdType.MESH


def _tile(n, cap):
    if n <= cap:
        return n
    best = None
    for t in range(128, cap + 1, 128):
        if n % t == 0:
            best = t
    assert best is not None, (n, cap)
    return best


def _row_tile(r, width=1024):
    cap = max(8, (512 * 1024) // max(width, 1))
    return next((t for t in (512, 256, 128, 64, 32, 16, 8) if t <= cap and r % t == 0), r)


def _cp(sem):
    return pltpu.CompilerParams(dimension_semantics=sem, vmem_limit_bytes=VMEM_LIMIT)


def _dot(a, b, ca=1, cb=0):
    return lax.dot_general(a.astype(MXU), b.astype(MXU), (((ca,), (cb,)), ((), ())), preferred_element_type=F32)


def _dot3(m, x):
    hi = x.astype(MXU)
    r1 = x - hi.astype(F32)
    mid = r1.astype(MXU)
    lo = (r1 - mid.astype(F32)).astype(MXU)
    return _dot(m, hi) + _dot(m, mid) + _dot(m, lo)


def mm(a, b, *, ta=False, tb=False, out_dtype=F32, name):
    (kd, m) = a.shape if ta else a.shape[::-1]
    (n, kd2) = b.shape if tb else b.shape[::-1]
    assert kd == kd2, (a.shape, b.shape, ta, tb)
    tm, tn, tk = _tile(m, 1024), _tile(n, 1536), _tile(kd, 1024)
    nk = kd // tk

    def body(a_ref, b_ref, o_ref, acc_ref):
        k = pl.program_id(2)

        @pl.when(k == 0)
        def _():
            acc_ref[...] = jnp.zeros_like(acc_ref)

        acc_ref[...] += _dot(a_ref[...], b_ref[...], 0 if ta else 1, 1 if tb else 0)

        @pl.when(k == nk - 1)
        def _():
            o_ref[...] = acc_ref[...].astype(out_dtype)

    a_spec = pl.BlockSpec((tk, tm), lambda i, j, k: (k, i)) if ta else pl.BlockSpec((tm, tk), lambda i, j, k: (i, k))
    b_spec = pl.BlockSpec((tn, tk), lambda i, j, k: (j, k)) if tb else pl.BlockSpec((tk, tn), lambda i, j, k: (k, j))
    return pl.pallas_call(
        body, name=name, grid=(m // tm, n // tn, nk), in_specs=[a_spec, b_spec],
        out_specs=pl.BlockSpec((tm, tn), lambda i, j, k: (i, j)), out_shape=jax.ShapeDtypeStruct((m, n), out_dtype),
        scratch_shapes=[pltpu.VMEM((tm, tn), F32)], compiler_params=_cp(("parallel", "parallel", "arbitrary")))(a, b)


def blockdiag_tn(a, b, wa, wb, *, scale=1.0, name):
    rows = a.shape[0]
    kb = a.shape[1] // wa
    tr = _tile(rows, 1024)
    nr = rows // tr

    def body(a_ref, b_ref, o_ref):
        i = pl.program_id(1)

        @pl.when(i == 0)
        def _():
            o_ref[...] = jnp.zeros_like(o_ref)

        o_ref[0] += scale * _dot(a_ref[...], b_ref[...], 0, 0)

    return pl.pallas_call(
        body, name=name, grid=(kb, nr),
        in_specs=[pl.BlockSpec((tr, wa), lambda k, i: (i, k)), pl.BlockSpec((tr, wb), lambda k, i: (i, k))],
        out_specs=pl.BlockSpec((1, wa, wb), lambda k, i: (k, 0, 0)), out_shape=jax.ShapeDtypeStruct((kb, wa, wb), F32),
        compiler_params=_cp(("parallel", "arbitrary")))(a, b)


def _pat(v, p, op):
    tm, d = v.shape
    return op(v.reshape(tm // 8, 8, d), p[None]).reshape(tm, d)


def _norm_mod(x, nw, shift, scale):
    y = x * lax.rsqrt(jnp.mean(x * x, axis=-1, keepdims=True) + NORM_EPS) * nw
    return _pat(_pat(y, 1.0 + scale, jnp.multiply), shift, jnp.add)


def _mt_spec(d, nct):
    return pl.BlockSpec((8, N_MOD * d), lambda i: (jnp.where(i < nct, 0, 1), 0))


def _acc_spec(d, nct):
    return pl.BlockSpec((8, d), lambda i: (jnp.where(i < nct, 0, 1), 0))


def _rows(cfg):
    tm = min(512, cfg["rc"])
    return tm, cfg["rc"] // tm


def node_fwd(cfg, xp, y, mtg, gi, nw, mtn, si, *, name):
    r, d = xp.shape
    tm, nct = _rows(cfg)
    row = pl.BlockSpec((tm, d), lambda i: (i, 0))
    vec = pl.BlockSpec((1, d), lambda i: (0, 0))

    def body(*refs):
        if y is None:
            xp_ref, nw_ref, mtn_ref, h_ref = refs
            x = xp_ref[...]
        else:
            xp_ref, y_ref, mtg_ref, nw_ref, mtn_ref, xn_ref, h_ref = refs
            x = xp_ref[...] + _pat(y_ref[...], mtg_ref[:, gi * d:(gi + 1) * d], jnp.multiply)
            xn_ref[...] = x
        h_ref[...] = _norm_mod(x, nw_ref[...], mtn_ref[:, si * d:(si + 1) * d], mtn_ref[:, (si + 1) * d:(si + 2) * d]).astype(MXU)

    h_shape = jax.ShapeDtypeStruct((r, d), MXU)
    if y is None:
        h = pl.pallas_call(body, name=name, grid=(r // tm,), in_specs=[row, vec, _mt_spec(d, nct)], out_specs=row,
                           out_shape=h_shape, compiler_params=_cp(("parallel",)))(xp, nw, mtn)
        return xp, h
    return pl.pallas_call(body, name=name, grid=(r // tm,), in_specs=[row, row, _mt_spec(d, nct), vec, _mt_spec(d, nct)],
                          out_specs=(row, row), out_shape=(jax.ShapeDtypeStruct((r, d), F32), h_shape),
                          compiler_params=_cp(("parallel",)))(xp, y, mtg, nw, mtn)


def node_bwd(cfg, dxres, dh, xn, y, mtg, gi, nw, mtn, si, *, name):
    r, d = xn.shape
    tm, nct = _rows(cfg)
    row = pl.BlockSpec((tm, d), lambda i: (i, 0))
    vec = pl.BlockSpec((1, d), lambda i: (0, 0))
    has_y = y is not None

    def body(*refs):
        if has_y:
            dxres_ref, dh_ref, xn_ref, y_ref, mtg_ref, nw_ref, mtn_ref, dxn_ref, dy_ref, dnw_ref, dsh_ref, dsc_ref, dg_ref = refs
        else:
            dxres_ref, dh_ref, xn_ref, nw_ref, mtn_ref, dxn_ref, dnw_ref, dsh_ref, dsc_ref = refs
        i = pl.program_id(0)
        _, vjp = jax.vjp(_norm_mod, xn_ref[...], nw_ref[...], mtn_ref[:, si * d:(si + 1) * d], mtn_ref[:, (si + 1) * d:(si + 2) * d])
        dx, dnw, dsh, dsc = vjp(dh_ref[...])
        dx = dx + dxres_ref[...]
        dxn_ref[...] = dx

        @pl.when(i == 0)
        def _():
            dnw_ref[...] = jnp.zeros_like(dnw_ref)

        @pl.when((i == 0) | (i == nct))
        def _():
            dsh_ref[...] = jnp.zeros_like(dsh_ref)
            dsc_ref[...] = jnp.zeros_like(dsc_ref)
            if has_y:
                dg_ref[...] = jnp.zeros_like(dg_ref)

        dnw_ref[...] += dnw
        dsh_ref[...] += dsh
        dsc_ref[...] += dsc
        if has_y:
            dy_ref[...] = _pat(dx, mtg_ref[:, gi * d:(gi + 1) * d], jnp.multiply).astype(MXU)
            dg_ref[...] += jnp.sum((dx * y_ref[...]).reshape(tm // 8, 8, d), axis=0)

    acc = jax.ShapeDtypeStruct((16, d), F32)
    xs = jax.ShapeDtypeStruct((r, d), F32)
    if has_y:
        return pl.pallas_call(
            body, name=name, grid=(r // tm,), in_specs=[row, row, row, row, _mt_spec(d, nct), vec, _mt_spec(d, nct)],
            out_specs=(row, row, vec, _acc_spec(d, nct), _acc_spec(d, nct), _acc_spec(d, nct)),
            out_shape=(xs, jax.ShapeDtypeStruct((r, d), MXU), jax.ShapeDtypeStruct((1, d), F32), acc, acc, acc),
            compiler_params=_cp(("arbitrary",)))(dxres, dh, xn, y, mtg, nw, mtn)
    dxn, dnw, dsh, dsc = pl.pallas_call(
        body, name=name, grid=(r // tm,), in_specs=[row, row, row, vec, _mt_spec(d, nct)],
        out_specs=(row, vec, _acc_spec(d, nct), _acc_spec(d, nct)),
        out_shape=(xs, jax.ShapeDtypeStruct((1, d), F32), acc, acc), compiler_params=_cp(("arbitrary",)))(dxres, dh, xn, nw, mtn)
    return dxn, None, dnw, dsh, dsc, None


def final_node(cfg, xp, y, mtg, gi, fnw, tgt, *, name):
    r, d = xp.shape
    tm, nct = _rows(cfg)
    row = pl.BlockSpec((tm, d), lambda i: (i, 0))
    vec = pl.BlockSpec((1, d), lambda i: (0, 0))

    def norm(x, w):
        return x * lax.rsqrt(jnp.mean(x * x, axis=-1, keepdims=True) + NORM_EPS) * w

    def body(xp_ref, y_ref, mtg_ref, fnw_ref, tgt_ref, loss_ref, dx_ref, dy_ref, dg_ref, dfnw_ref):
        i = pl.program_id(0)
        g = mtg_ref[:, gi * d:(gi + 1) * d]
        x = xp_ref[...] + _pat(y_ref[...], g, jnp.multiply)
        out, vjp = jax.vjp(norm, x, fnw_ref[...])
        lat = i >= nct
        err = jnp.where(lat, out - tgt_ref[...], 0.0)
        dx, dfnw = vjp(err * (1.0 / d))

        @pl.when(i == 0)
        def _():
            loss_ref[...] = jnp.zeros_like(loss_ref)
            dfnw_ref[...] = jnp.zeros_like(dfnw_ref)

        @pl.when((i == 0) | (i == nct))
        def _():
            dg_ref[...] = jnp.zeros_like(dg_ref)

        loss_ref[...] += jnp.full(loss_ref.shape, 0.5 / d * jnp.sum(err * err), F32)
        dfnw_ref[...] += dfnw
        dx_ref[...] = dx
        dy_ref[...] = _pat(dx, g, jnp.multiply).astype(MXU)
        dg_ref[...] += jnp.sum((dx * y_ref[...]).reshape(tm // 8, 8, d), axis=0)

    return pl.pallas_call(
        body, name=name, grid=(r // tm,),
        in_specs=[row, row, _mt_spec(d, nct), vec, pl.BlockSpec((tm, d), lambda i: (jnp.maximum(i - nct, 0), 0))],
        out_specs=(pl.BlockSpec((8, 128), lambda i: (0, 0)), row, row, _acc_spec(d, nct), vec),
        out_shape=(jax.ShapeDtypeStruct((8, 128), F32), jax.ShapeDtypeStruct((r, d), F32), jax.ShapeDtypeStruct((r, d), MXU),
                   jax.ShapeDtypeStruct((16, d), F32), jax.ShapeDtypeStruct((1, d), F32)),
        compiler_params=_cp(("arbitrary",)))(xp, y, mtg, fnw, tgt)


def _silu(x):
    return x * jax.nn.sigmoid(x)


def mod_fwd(c16, w, b, *, name):
    d, n = w.shape
    tn = _tile(n, 1536)

    def body(c_ref, w_ref, b_ref, o_ref, s_ref):
        s = _silu(c_ref[...])
        s_ref[...] = s.astype(MXU)
        o_ref[...] = _dot(s, w_ref[...]) + b_ref[...]

    return pl.pallas_call(
        body, name=name, grid=(n // tn,),
        in_specs=[pl.BlockSpec((16, d), lambda j: (0, 0)), pl.BlockSpec((d, tn), lambda j: (0, j)), pl.BlockSpec((1, tn), lambda j: (0, j))],
        out_specs=(pl.BlockSpec((16, tn), lambda j: (0, j)), pl.BlockSpec((16, d), lambda j: (0, 0))),
        out_shape=(jax.ShapeDtypeStruct((16, n), F32), jax.ShapeDtypeStruct((16, d), MXU)),
        compiler_params=_cp(("arbitrary",)))(c16, w, b)


def colsum(x, *, name):
    def body(x_ref, o_ref):
        o_ref[...] = jnp.sum(x_ref[...], axis=0, keepdims=True)

    return pl.pallas_call(body, name=name, out_shape=jax.ShapeDtypeStruct((1, x.shape[1]), F32))(x)


def cctx_grad(c16, ds_list, *, name):
    def body(c_ref, *refs):
        o_ref = refs[-1]
        ds = refs[0][...]
        for r_ in refs[1:-1]:
            ds = ds + r_[...]
        _, vjp = jax.vjp(_silu, c_ref[...])
        (dc,) = vjp(ds)
        o_ref[...] = jnp.sum(dc[0:8], axis=0, keepdims=True)

    return pl.pallas_call(body, name=name, out_shape=jax.ShapeDtypeStruct((1, c16.shape[1]), F32))(c16, *ds_list)


def _s5_disc(lam_re, lam_im, log_step, b_re, b_im):
    lr = jnp.minimum(lam_re, S5_LAM_RE_MAX)
    li = lam_im
    dt = jnp.exp(log_step)
    mag = jnp.exp(lr * dt)
    abar_r = mag * jnp.cos(li * dt)
    abar_i = mag * jnp.sin(li * dt)
    den = lr * lr + li * li
    nr = abar_r - 1.0
    coef_r = (nr * lr + abar_i * li) / den
    coef_i = (abar_i * lr - nr * li) / den
    bbar_r = coef_r[:, None, :] * b_re - coef_i[:, None, :] * b_im
    bbar_i = coef_r[:, None, :] * b_im + coef_i[:, None, :] * b_re
    return abar_r, abar_i, bbar_r, bbar_i


def s5_disc_fwd(lam_re, lam_im, log_step, b_re, b_im, *, name):
    def body(lr, li, ls, br, bi, ar_o, ai_o, br_o, bi_o):
        ar_o[...], ai_o[...], br_o[...], bi_o[...] = _s5_disc(lr[...], li[...], ls[...], br[...], bi[...])

    s2, s3 = jax.ShapeDtypeStruct(lam_re.shape, F32), jax.ShapeDtypeStruct(b_re.shape, F32)
    return pl.pallas_call(body, name=name, out_shape=(s2, s2, s3, s3))(lam_re, lam_im, log_step, b_re, b_im)


def s5_disc_bwd(lam_re, lam_im, log_step, b_re, b_im, d_ar, d_ai, d_br, d_bi, *, name):
    def body(lr, li, ls, br, bi, dar, dai, dbr, dbi, o_lr, o_li, o_ls, o_br, o_bi):
        _, vjp = jax.vjp(_s5_disc, lr[...], li[...], ls[...], br[...], bi[...])
        o_lr[...], o_li[...], o_ls[...], o_br[...], o_bi[...] = vjp((dar[...], dai[...], dbr[...], dbi[...]))

    s2, s3 = jax.ShapeDtypeStruct(lam_re.shape, F32), jax.ShapeDtypeStruct(b_re.shape, F32)
    return pl.pallas_call(body, name=name, out_shape=(s2, s2, jax.ShapeDtypeStruct(log_step.shape, F32), s3, s3))(
        lam_re, lam_im, log_step, b_re, b_im, d_ar, d_ai, d_br, d_bi)


S5_LANES = 512


def _chunk_order(k, ncc, nch, rev):
    if not rev:
        return k
    return jnp.where(k < ncc, ncc - 1 - k, nch - 1 - (k - ncc))


def _cmul(ar, ai, xr, xi):
    return ar * xr - ai * xi, ar * xi + ai * xr


S5_FWD_ROWS = 256
S5_BWD_ROWS = 256


def _const_spec(a):
    return pl.BlockSpec(a.shape, lambda k: (0,) * a.ndim, pipeline_mode=pl.Buffered(1))


def _shift_steps(x, edge_tile, back):
    n = x.shape[0]
    row = lax.broadcasted_iota(jnp.int32, (8, x.shape[1]), 0)
    edge = pltpu.roll(edge_tile, 4, 0)
    if back:
        y = pltpu.roll(x, 4, 0)
        return jnp.concatenate([jnp.where(row < 4, edge, y[0:8]), y[8:]], axis=0)
    y = pltpu.roll(x, n - 4, 0)
    return jnp.concatenate([y[:n - 8], jnp.where(row >= 4, edge, y[n - 8:])], axis=0)


def s5_scan_fwd(cfg, u, a2_re, a2_im, bre, bim, abre, abim, cre, cim, *, rev, name):
    r, d = u.shape
    ns = a2_re.shape[1]
    kb = d // S5_KIN
    tcr = S5_FWD_ROWS
    n8 = tcr // 8
    q = S5_FWD_ROWS // S5_BWD_ROWS
    seg = n8 // q
    nch, ncc = r // tcr, cfg["rc"] // tcr
    lw = min(S5_LANES, ns)

    def body(u_ref, ar_ref, ai_ref, bre_ref, bim_ref, abre_ref, abim_ref, cre_ref, cim_ref, sre_ref, sim_ref, ere_ref, eim_ref, y_ref,
             st_re, st_im, u_edge):
        @pl.when(pl.program_id(0) == 0)
        def _():
            st_re[...] = jnp.zeros_like(st_re)
            st_im[...] = jnp.zeros_like(st_im)
            u_edge[...] = jnp.zeros_like(u_edge)

        u_ = u_ref[...]
        ub = u_.astype(MXU)
        upb = _shift_steps(u_, u_edge[...], back=not rev).astype(MXU)
        u_edge[...] = u_[0:8] if rev else u_[tcr - 8:tcr]
        for j in range(kb):
            uj, upj = ub[:, j * S5_KIN:(j + 1) * S5_KIN], upb[:, j * S5_KIN:(j + 1) * S5_KIN]
            sre_ref[:, :, j * S5_KST:(j + 1) * S5_KST] = (_dot(uj, bre_ref[j]) + _dot(upj, abre_ref[j])).reshape(n8, 8, S5_KST)
            sim_ref[:, :, j * S5_KST:(j + 1) * S5_KST] = (_dot(uj, bim_ref[j]) + _dot(upj, abim_ref[j])).reshape(n8, 8, S5_KST)
        for c in range(ns // lw):
            sl = slice(c * lw, (c + 1) * lw)
            ar = jnp.broadcast_to(ar_ref[:, sl], (8, lw))
            ai = jnp.broadcast_to(ai_ref[:, sl], (8, lw))

            def step(i, carry, sl=sl, ar=ar, ai=ai):
                sr, si = carry
                ii = n8 - 1 - i if rev else i
                pr, pi = _cmul(ar, ai, sr, si)
                sr, si = pr + sre_ref[ii, :, sl], pi + sim_ref[ii, :, sl]
                sre_ref[ii, :, sl] = sr
                sim_ref[ii, :, sl] = si
                return sr, si

            sr, si = st_re[:, sl], st_im[:, sl]
            for s_ in range(q):
                at = q - 1 - s_ if rev else s_
                ere_ref[at, :, sl] = sr
                eim_ref[at, :, sl] = si
                sr, si = lax.fori_loop(s_ * seg, (s_ + 1) * seg, step, (sr, si))
            st_re[:, sl] = sr
            st_im[:, sl] = si
        for j in range(kb):
            sr = sre_ref[:, :, j * S5_KST:(j + 1) * S5_KST].reshape(tcr, S5_KST)
            si = sim_ref[:, :, j * S5_KST:(j + 1) * S5_KST].reshape(tcr, S5_KST)
            y_ref[:, j * S5_KIN:(j + 1) * S5_KIN] = _dot(sr, cre_ref[j]) - _dot(si, cim_ref[j])

    cidx = functools.partial(_chunk_order, ncc=ncc, nch=nch, rev=rev)
    full = _const_spec
    st = pl.BlockSpec((n8, 8, ns), lambda k: (cidx(k), 0, 0))
    en = pl.BlockSpec((q, 8, ns), lambda k: (cidx(k), 0, 0))
    return pl.pallas_call(
        body, name=name, grid=(nch,),
        in_specs=[pl.BlockSpec((tcr, d), lambda k: (cidx(k), 0)), full(a2_re), full(a2_im), full(bre), full(bim), full(abre), full(abim),
                  full(cre), full(cim)],
        out_specs=(st, st, en, en, pl.BlockSpec((tcr, d), lambda k: (cidx(k), 0))),
        out_shape=(jax.ShapeDtypeStruct((r // 8, 8, ns), F32),) * 2 + (jax.ShapeDtypeStruct((q * nch, 8, ns), F32),) * 2
        + (jax.ShapeDtypeStruct((r, d), F32),),
        scratch_shapes=[pltpu.VMEM((8, ns), F32), pltpu.VMEM((8, ns), F32), pltpu.VMEM((8, d), F32)],
        compiler_params=_cp(("arbitrary",)))(u, a2_re, a2_im, bre, bim, abre, abim, cre, cim)


def s5_scan_bwd(cfg, dyb, sre, sim, ere, eim, a2_re, a2_im, bre, bim, cre, cim, c2re, c2im, du_in, *, rev, name):
    r, d = dyb.shape
    ns = a2_re.shape[1]
    kb = d // S5_KIN
    tcr = S5_BWD_ROWS
    n8 = tcr // 8
    nch, ncc = r // tcr, cfg["rc"] // tcr
    lw = min(S5_LANES, ns)

    def body(dy_ref, sre_ref, sim_ref, ere_ref, eim_ref, ar_ref, ai_ref, bre_ref, bim_ref, cre_ref, cim_ref, c2re_ref, c2im_ref, duin_ref,
             du_ref, gre_ref, gim_ref, dar_ref, dai_ref, g_re, g_im, gc_re, gc_im, dy_edge):
        k = pl.program_id(0)

        @pl.when(k == 0)
        def _():
            gc_re[...] = jnp.zeros_like(gc_re)
            gc_im[...] = jnp.zeros_like(gc_im)
            dar_ref[...] = jnp.zeros_like(dar_ref)
            dai_ref[...] = jnp.zeros_like(dai_ref)
            dy_edge[...] = jnp.zeros_like(dy_edge)

        dy32 = dy_ref[...].astype(F32)
        dy = dy32.astype(MXU)
        dyn = _shift_steps(dy32, dy_edge[...], back=rev).astype(MXU)
        dy_edge[...] = dy32[tcr - 8:tcr] if rev else dy32[0:8]
        for j in range(kb):
            dyj, dynj = dy[:, j * S5_KIN:(j + 1) * S5_KIN], dyn[:, j * S5_KIN:(j + 1) * S5_KIN]
            g_re[:, :, j * S5_KST:(j + 1) * S5_KST] = (_dot(dyj, cre_ref[j], 1, 1) + _dot(dynj, c2re_ref[j], 1, 1)).reshape(n8, 8, S5_KST)
            g_im[:, :, j * S5_KST:(j + 1) * S5_KST] = -(_dot(dyj, cim_ref[j], 1, 1) + _dot(dynj, c2im_ref[j], 1, 1)).reshape(n8, 8, S5_KST)
        first = lax.broadcasted_iota(jnp.int32, (8, lw), 0) < 4
        if rev:
            first = jnp.logical_not(first)
        for c in range(ns // lw):
            sl = slice(c * lw, (c + 1) * lw)
            ar = jnp.broadcast_to(ar_ref[:, sl], (8, lw))
            nai = -jnp.broadcast_to(ai_ref[:, sl], (8, lw))

            def step(i, carry, sl=sl, ar=ar, nai=nai):
                gr, gi, accr, acci = carry
                ii = i if rev else n8 - 1 - i
                pr, pi = _cmul(ar, nai, gr, gi)
                outr, outi = pr + g_re[ii, :, sl], pi + g_im[ii, :, sl]
                g_re[ii, :, sl] = outr
                g_im[ii, :, sl] = outi
                pv = jnp.clip(ii + 1 if rev else ii - 1, 0, n8 - 1)
                at_entry = (ii == n8 - 1) if rev else (ii == 0)
                pvr = jnp.where(at_entry, ere_ref[0, :, sl], sre_ref[pv, :, sl])
                pvi = jnp.where(at_entry, eim_ref[0, :, sl], sim_ref[pv, :, sl])
                spr = pltpu.roll(jnp.where(first, sre_ref[ii, :, sl], pvr), 4, 0)
                spi = pltpu.roll(jnp.where(first, sim_ref[ii, :, sl], pvi), 4, 0)
                accr = accr + outr * spr + outi * spi
                acci = acci + outi * spr - outr * spi
                return outr, outi, accr, acci

            gr, gi, accr, acci = lax.fori_loop(0, n8, step, (gc_re[:, sl], gc_im[:, sl], dar_ref[:, sl], dai_ref[:, sl]))
            gc_re[:, sl] = gr
            gc_im[:, sl] = gi
            dar_ref[:, sl] = accr
            dai_ref[:, sl] = acci
        for j in range(kb):
            gr = g_re[:, :, j * S5_KST:(j + 1) * S5_KST].reshape(tcr, S5_KST)
            gi = g_im[:, :, j * S5_KST:(j + 1) * S5_KST].reshape(tcr, S5_KST)
            gre_ref[:, j * S5_KST:(j + 1) * S5_KST] = gr.astype(MXU)
            gim_ref[:, j * S5_KST:(j + 1) * S5_KST] = gi.astype(MXU)
            du_ref[:, j * S5_KIN:(j + 1) * S5_KIN] = (duin_ref[:, j * S5_KIN:(j + 1) * S5_KIN]
                                                     + _dot(gr, bre_ref[j], 1, 1) + _dot(gi, bim_ref[j], 1, 1))

    def cidx(k):
        return _chunk_order(nch - 1 - k, ncc, nch, rev)

    full = _const_spec
    st = pl.BlockSpec((n8, 8, ns), lambda k: (cidx(k), 0, 0))
    en = pl.BlockSpec((1, 8, ns), lambda k: (cidx(k), 0, 0))
    rowd = pl.BlockSpec((tcr, d), lambda k: (cidx(k), 0))
    rown = pl.BlockSpec((tcr, ns), lambda k: (cidx(k), 0))
    acc = pl.BlockSpec((8, ns), lambda k: (0, 0))
    return pl.pallas_call(
        body, name=name, grid=(nch,),
        in_specs=[rowd, st, st, en, en, full(a2_re), full(a2_im), full(bre), full(bim), full(cre), full(cim), full(c2re), full(c2im), rowd],
        out_specs=(rowd, rown, rown, acc, acc),
        out_shape=(jax.ShapeDtypeStruct((r, d), F32), jax.ShapeDtypeStruct((r, ns), MXU), jax.ShapeDtypeStruct((r, ns), MXU),
                   jax.ShapeDtypeStruct((8, ns), F32), jax.ShapeDtypeStruct((8, ns), F32)),
        scratch_shapes=[pltpu.VMEM((n8, 8, ns), F32), pltpu.VMEM((n8, 8, ns), F32), pltpu.VMEM((8, ns), F32), pltpu.VMEM((8, ns), F32),
                        pltpu.VMEM((8, d), F32)],
        compiler_params=_cp(("arbitrary",)))(dyb, sre, sim, ere, eim, a2_re, a2_im, bre, bim, cre, cim, c2re, c2im, du_in)


def rowmap(fn, rows_in, vecs_in, outs, accs=(), *, name):
    r = rows_in[0].shape[0]
    tm = _row_tile(r, max(a.shape[1] for a in rows_in))
    nr, nv, no = len(rows_in), len(vecs_in), len(outs)

    def body(*refs):
        ins = [x[...] for x in refs[:nr + nv]]
        res = fn(*ins)
        if not isinstance(res, (tuple, list)):
            res = (res,)
        out_refs = refs[nr + nv:]
        for o_ref, v in zip(out_refs[:no], res[:no]):
            o_ref[...] = v.astype(o_ref.dtype)
        if accs:
            @pl.when(pl.program_id(0) == 0)
            def _():
                for a_ref in out_refs[no:]:
                    a_ref[...] = jnp.zeros_like(a_ref)
            for a_ref, v in zip(out_refs[no:], res[no:]):
                a_ref[...] += v

    in_specs = [pl.BlockSpec((tm, a.shape[1]), lambda i: (i, 0)) for a in rows_in]
    in_specs += [pl.BlockSpec(v.shape, lambda i, n=v.ndim: (0,) * n) for v in vecs_in]
    out_specs = [pl.BlockSpec((tm, w), lambda i: (i, 0)) for w, _ in outs] + [pl.BlockSpec(s, lambda i, n=len(s): (0,) * n) for s in accs]
    out_shape = [jax.ShapeDtypeStruct((r, w), dt) for w, dt in outs] + [jax.ShapeDtypeStruct(s, F32) for s in accs]
    res = pl.pallas_call(body, name=name, grid=(r // tm,), in_specs=in_specs, out_specs=tuple(out_specs), out_shape=tuple(out_shape),
                         compiler_params=_cp(("arbitrary",) if accs else ("parallel",)))(*rows_in, *vecs_in)
    return res


def _gelu(x):
    return jax.nn.gelu(x, approximate=True)


def _hg_lower_bound(e0, e1):
    m = jnp.maximum(e0, e1)
    a, b = jnp.exp(e0 - m), jnp.exp(e1 - m)
    return b / (a + b)


def _hg_gates(x, lb):
    logf = jnp.log(lb + (1.0 - lb) * jax.nn.sigmoid(x))
    return logf, (1.0 - lb) * jax.nn.sigmoid(-x)


def _hg_masks(rev):
    n = CHUNK_ROWS
    rr = lax.broadcasted_iota(jnp.int32, (n, n), 0)
    ss = lax.broadcasted_iota(jnp.int32, (n, n), 1)
    same = (rr % NB) == (ss % NB)
    causal = same & ((ss >= rr) if rev else (ss <= rr))
    anti = same & ((ss <= rr) if rev else (ss >= rr))
    end0 = 0 if rev else n - NB
    pick_end = ss == (end0 + rr % NB)
    return same, causal, anti, pick_end, end0


def _hg_expand(x):
    ex = lax.broadcasted_iota(jnp.int32, x.shape, 0) % NB
    return jnp.concatenate([jnp.where(ex == b, x, 0.0) for b in range(NB)], axis=1)


def _hg_fold(xe):
    kk = xe.shape[1] // NB
    ex = lax.broadcasted_iota(jnp.int32, (xe.shape[0], kk), 0) % NB
    out = jnp.zeros((xe.shape[0], kk), F32)
    for b in range(NB):
        out = out + jnp.where(ex == b, xe[:, b * kk:(b + 1) * kk], 0.0)
    return out


def _hg_chunk(q, v, x, lb, masks):
    same, causal, anti, pick_end, end0 = masks
    logf, kk = _hg_gates(x, lb)
    b = _dot3(causal.astype(MXU), logf)
    bend_t = _dot3(pick_end.astype(MXU), b)
    bend_flat = jnp.concatenate([b[end0 + i:end0 + i + 1] for i in range(NB)], axis=1)
    eb = jnp.exp(b)
    enb = jnp.exp(-b)
    ee = jnp.exp(bend_t - b)
    qd, kd, ke = q * eb, kk * enb, kk * ee
    att = jnp.where(causal, _dot(qd, kd, 1, 1), 0.0)
    decay = jnp.exp(bend_flat)
    return dict(same=same, causal=causal, anti=anti, logf=logf, kk=kk, b=b, eb=eb, enb=enb, ee=ee, qd=qd, kd=kd, ke=ke, att=att,
                decay=decay, qde=_hg_expand(qd), kee=_hg_expand(ke))


def _hg_chunk_order(cfg, r):
    nch, ncc = r // CHUNK_ROWS, cfg["rc"] // CHUNK_ROWS
    return nch, ncc


def hg_scan_fwd(cfg, z, lb, *, d_dir, name):
    r = z.shape[0]
    d = z.shape[1] // N_PROJ
    nh = d // HEAD
    rev = d_dir == 1
    nch, ncc = _hg_chunk_order(cfg, r)
    n = CHUNK_ROWS

    def body(q_ref, v_ref, x_ref, lb_ref, o_ref, sin_ref, stk):
        @pl.when(pl.program_id(0) == 0)
        def _():
            stk[...] = jnp.zeros_like(stk)

        masks = _hg_masks(rev)
        for h in range(nh):
            sl = slice(h * HEAD, (h + 1) * HEAD)
            s0 = stk[h]
            sin_ref[0, h] = s0
            v = v_ref[:, sl]
            c = _hg_chunk(q_ref[:, sl], v, x_ref[:, sl], lb_ref[:, sl], masks)
            o_ref[:, sl] = _dot(c["att"], v) + _dot(c["qde"], s0, 1, 1)
            stk[h] = s0 * c["decay"] + _dot(v, c["kee"], 0, 0)

    def cidx(k):
        return _chunk_order(k, ncc, nch, rev)

    blk = lambda p: pl.BlockSpec((n, d), lambda k: (cidx(k), p))
    return pl.pallas_call(
        body, name=name, grid=(nch,),
        in_specs=[blk(0), blk(1), blk(2 + d_dir), pl.BlockSpec((1, d), lambda k: (0, 0))],
        out_specs=(blk(0), pl.BlockSpec((1, nh, HEAD, NB * HEAD), lambda k: (cidx(k), 0, 0, 0))),
        out_shape=(jax.ShapeDtypeStruct((r, d), F32), jax.ShapeDtypeStruct((nch, nh, HEAD, NB * HEAD), F32)),
        scratch_shapes=[pltpu.VMEM((nh, HEAD, NB * HEAD), F32)], compiler_params=_cp(("arbitrary",)))(z, z, z, lb)


def hg_scan_bwd(cfg, do, z, lb, sin, dq_in, dv_in, *, d_dir, name):
    r = z.shape[0]
    d = z.shape[1] // N_PROJ
    nh = d // HEAD
    rev = d_dir == 1
    nch, ncc = _hg_chunk_order(cfg, r)
    n = CHUNK_ROWS
    has_in = dq_in is not None

    def body(*refs):
        if has_in:
            do_ref, q_ref, v_ref, x_ref, lb_ref, sin_ref, dqi_ref, dvi_ref, dq_ref, dv_ref, dx_ref, dlb_ref, dstk = refs
        else:
            do_ref, q_ref, v_ref, x_ref, lb_ref, sin_ref, dq_ref, dv_ref, dx_ref, dlb_ref, dstk = refs
        @pl.when(pl.program_id(0) == 0)
        def _():
            dstk[...] = jnp.zeros_like(dstk)
            dlb_ref[...] = jnp.zeros_like(dlb_ref)

        masks = _hg_masks(rev)
        ex = lax.broadcasted_iota(jnp.int32, (n, HEAD), 0) % NB
        for h in range(nh):
            sl = slice(h * HEAD, (h + 1) * HEAD)
            do_, q, v, x, lb_, s0, ds1 = do_ref[:, sl], q_ref[:, sl], v_ref[:, sl], x_ref[:, sl], lb_ref[:, sl], sin_ref[0, h], dstk[h]
            c = _hg_chunk(q, v, x, lb_, masks)
            datt = jnp.where(c["causal"], _dot(do_, v, 1, 1), 0.0)
            dv = _dot(c["att"], do_, 0, 0) + _dot(c["kee"], ds1, 1, 1)
            dqd = _dot(datt, c["kd"]) + _hg_fold(_dot(do_, s0))
            dkd = _dot(datt, c["qd"], 0, 0)
            dke = _hg_fold(_dot(v, ds1))
            dbend_flat = jnp.sum(ds1 * s0, axis=0, keepdims=True) * c["decay"]
            dstk[h] = _dot(do_, c["qde"], 0, 0) + ds1 * c["decay"]
            dq = dqd * c["eb"]
            dk = dkd * c["enb"] + dke * c["ee"]
            db = dqd * c["qd"] - dkd * c["kd"] - dke * c["ke"]
            dbend_rows = jnp.zeros((n, HEAD), F32)
            for b in range(NB):
                dbend_rows = dbend_rows + jnp.where(ex == b, dbend_flat[:, b * HEAD:(b + 1) * HEAD], 0.0)
            dlogf = _dot3(c["anti"].astype(MXU), db) + _dot3(c["same"].astype(MXU), dke * c["ke"]) + dbend_rows
            _, vjp = jax.vjp(_hg_gates, x, lb_)
            dx, dlb = vjp((dlogf, dk))
            if has_in:
                dq = dq + dqi_ref[:, sl]
                dv = dv + dvi_ref[:, sl]
            dq_ref[:, sl] = dq
            dv_ref[:, sl] = dv
            dx_ref[:, sl] = dx
            dlb_ref[:, sl] += dlb

    def cidx(k):
        return _chunk_order(nch - 1 - k, ncc, nch, rev)

    blk = lambda p: pl.BlockSpec((n, d), lambda k: (cidx(k), p))
    vec = pl.BlockSpec((1, d), lambda k: (0, 0))
    in_specs = [blk(0), blk(0), blk(1), blk(2 + d_dir), vec, pl.BlockSpec((1, nh, HEAD, NB * HEAD), lambda k: (cidx(k), 0, 0, 0))]
    args = [do, z, z, z, lb, sin]
    if has_in:
        in_specs += [blk(0), blk(0)]
        args += [dq_in, dv_in]
    rd = jax.ShapeDtypeStruct((r, d), F32)
    return pl.pallas_call(
        body, name=name, grid=(nch,), in_specs=in_specs, out_specs=(blk(0), blk(0), blk(0), vec),
        out_shape=(rd, rd, rd, jax.ShapeDtypeStruct((1, d), F32)),
        scratch_shapes=[pltpu.VMEM((nh, HEAD, NB * HEAD), F32)], compiler_params=_cp(("arbitrary",)))(*args)


def _hg_read(o, g, gw):
    on = o * lax.rsqrt(jnp.mean(o * o, axis=-1, keepdims=True) + NORM_EPS) * gw
    return on * jax.nn.sigmoid(g)


def hg_read_fwd(of, ob, z, gw, *, name):
    r, d = of.shape
    nh = d // HEAD
    tm = _row_tile(r)

    def body(of_ref, ob_ref, g_ref, gw_ref, o_ref):
        for h in range(nh):
            sl = slice(h * HEAD, (h + 1) * HEAD)
            o_ref[:, sl] = _hg_read(of_ref[:, sl] + ob_ref[:, sl], g_ref[:, sl], gw_ref[...]).astype(MXU)

    blk = pl.BlockSpec((tm, d), lambda i: (i, 0))
    return pl.pallas_call(
        body, name=name, grid=(r // tm,),
        in_specs=[blk, blk, pl.BlockSpec((tm, d), lambda i: (i, N_PROJ - 1)), pl.BlockSpec((1, HEAD), lambda i: (0, 0))],
        out_specs=blk, out_shape=jax.ShapeDtypeStruct((r, d), MXU), compiler_params=_cp(("parallel",)))(of, ob, z, gw)


def hg_read_bwd(don, of, ob, z, gw, *, name):
    r, d = of.shape
    nh = d // HEAD
    tm = _row_tile(r)

    def body(don_ref, of_ref, ob_ref, g_ref, gw_ref, do_ref, dg_ref, dgw_ref):
        @pl.when(pl.program_id(0) == 0)
        def _():
            dgw_ref[...] = jnp.zeros_like(dgw_ref)

        for h in range(nh):
            sl = slice(h * HEAD, (h + 1) * HEAD)
            _, vjp = jax.vjp(_hg_read, of_ref[:, sl] + ob_ref[:, sl], g_ref[:, sl], gw_ref[...])
            do_ref[:, sl], dg_ref[:, sl], dgw = vjp(don_ref[:, sl])
            dgw_ref[...] += dgw

    blk = pl.BlockSpec((tm, d), lambda i: (i, 0))
    vec = pl.BlockSpec((1, HEAD), lambda i: (0, 0))
    rd = jax.ShapeDtypeStruct((r, d), F32)
    return pl.pallas_call(
        body, name=name, grid=(r // tm,),
        in_specs=[blk, blk, blk, pl.BlockSpec((tm, d), lambda i: (i, N_PROJ - 1)), vec],
        out_specs=(blk, blk, vec), out_shape=(rd, rd, jax.ShapeDtypeStruct((1, HEAD), F32)),
        compiler_params=_cp(("arbitrary",)))(don, of, ob, z, gw)


FFN_COLS = 256


def _seg_masks(cfg, tr, i):
    t = lax.broadcasted_iota(jnp.int32, (tr, FFN_COLS), 0) // NB
    ctx_steps = cfg["rc"] // NB
    pos = jnp.where(i == 0, t % ctx_steps, t % GRID_W)
    last = jnp.where(i == 0, ctx_steps - 1, GRID_W - 1)
    return pos == 0, pos == last


def _prev(x, start):
    return jnp.where(start, 0.0, pltpu.roll(x, NB, 0))


def _next(x, end):
    return jnp.where(end, 0.0, pltpu.roll(x, x.shape[0] - NB, 0))


def _conv3(u, w, b, start, end):
    return ((b + _prev(u, start) * w[0:1]) + u * w[1:2]) + _next(u, end) * w[2:3]


def ffn_mid_fwd(cfg, u, cw, cb, *, name):
    r, f2 = u.shape
    f = f2 // 2
    tr = cfg["rc"]
    nf = f // FFN_COLS

    def body(ua_ref, ug_ref, wa_ref, wg_ref, ba_ref, bg_ref, o_ref, ca_ref, cg_ref):
        start, end = _seg_masks(cfg, tr, pl.program_id(0))
        a = _conv3(ua_ref[...], wa_ref[...], ba_ref[...], start, end)
        g = _conv3(ug_ref[...], wg_ref[...], bg_ref[...], start, end)
        ca_ref[...] = a.astype(MXU)
        cg_ref[...] = g.astype(MXU)
        o_ref[...] = (_silu(a) * g).astype(MXU)

    ca = lambda rows: pl.BlockSpec((rows, FFN_COLS), lambda i, j: (i if rows == tr else 0, j))
    cg = lambda rows: pl.BlockSpec((rows, FFN_COLS), lambda i, j: (i if rows == tr else 0, j + nf))
    half = jax.ShapeDtypeStruct((r, f), MXU)
    return pl.pallas_call(
        body, name=name, grid=(r // tr, nf), in_specs=[ca(tr), cg(tr), ca(3), cg(3), ca(1), cg(1)], out_specs=(ca(tr), ca(tr), ca(tr)),
        out_shape=(jax.ShapeDtypeStruct((r, f), MXU), half, half), compiler_params=_cp(("parallel", "parallel")))(u, u, cw, cw, cb, cb)


def ffn_mid_bwd(cfg, dact, u, ca, cg, cw, *, name):
    r, f2 = u.shape
    f = f2 // 2
    tr = cfg["rc"]
    nf = f // FFN_COLS

    def body(da_ref, us_ref, ca_ref, cg_ref, ws_ref, du_ref, dcw_ref, dcb_ref):
        i = pl.program_id(1)
        is_a = pl.program_id(0) < nf
        start, end = _seg_masks(cfg, tr, i)

        @pl.when(i == 0)
        def _():
            dcw_ref[...] = jnp.zeros_like(dcw_ref)
            dcb_ref[...] = jnp.zeros_like(dcb_ref)

        def finish(dc):
            us, ws = us_ref[...], ws_ref[...]
            dn, dp = _next(dc, end), _prev(dc, start)
            du_ref[...] = (ws[1:2] * dc + ws[0:1] * dn + ws[2:3] * dp).astype(MXU)
            dcw_ref[...] += jnp.concatenate([jnp.sum(dn * us, axis=0, keepdims=True), jnp.sum(dc * us, axis=0, keepdims=True),
                                             jnp.sum(dp * us, axis=0, keepdims=True)], axis=0)
            dcb_ref[...] += jnp.sum(dc, axis=0, keepdims=True)

        @pl.when(is_a)
        def _():
            cs = ca_ref[...].astype(F32)
            sg = jax.nn.sigmoid(cs)
            finish(da_ref[...].astype(F32) * cg_ref[...].astype(F32) * (sg * (1.0 + cs * (1.0 - sg))))

        @pl.when(jnp.logical_not(is_a))
        def _():
            finish(da_ref[...].astype(F32) * _silu(ca_ref[...].astype(F32)))

    cs_ = lambda rows: pl.BlockSpec((rows, FFN_COLS), lambda j, i: (i if rows == tr else 0, j))
    hf = pl.BlockSpec((tr, FFN_COLS), lambda j, i: (i, j % nf))
    gate = pl.BlockSpec((tr, FFN_COLS), lambda j, i: (jnp.where(j < nf, i, 0), jnp.where(j < nf, j, 0)))
    return pl.pallas_call(
        body, name=name, grid=(2 * nf, r // tr), in_specs=[hf, cs_(tr), hf, gate, cs_(3)], out_specs=(cs_(tr), cs_(3), cs_(1)),
        out_shape=(jax.ShapeDtypeStruct((r, f2), MXU), jax.ShapeDtypeStruct((3, f2), F32), jax.ShapeDtypeStruct((1, f2), F32)),
        compiler_params=_cp(("parallel", "arbitrary")))(dact, u, ca, cg, cw)


def hg_lb_fwd(e0, e1, *, name):
    def body(a, b, o):
        o[...] = _hg_lower_bound(a[...], b[...])

    return pl.pallas_call(body, name=name, out_shape=jax.ShapeDtypeStruct(e0.shape, F32))(e0, e1)


def hg_lb_bwd(e0, e1, dlb, *, name):
    def body(a, b, g, oa, ob):
        _, vjp = jax.vjp(_hg_lower_bound, a[...], b[...])
        oa[...], ob[...] = vjp(g[...])

    s = jax.ShapeDtypeStruct(e0.shape, F32)
    return pl.pallas_call(body, name=name, out_shape=(s, s))(e0, e1, dlb)


def _adamw(w, g, m, v):
    m = ADAM_B1 * m + (1.0 - ADAM_B1) * g
    v = ADAM_B2 * v + (1.0 - ADAM_B2) * jnp.square(g)
    m_hat = m / (1.0 - ADAM_B1 ** ADAM_STEP)
    v_hat = v / (1.0 - ADAM_B2 ** ADAM_STEP)
    delta = -ADAM_LR * (m_hat / (jnp.sqrt(v_hat) + ADAM_EPS) + ADAM_WD * w)
    return delta, m, v


def _as2d(a):
    if a.ndim >= 2 and a.shape[-1] % 128 == 0:
        return a.reshape(-1, a.shape[-1])
    return a.reshape(-1, 128) if a.size % 128 == 0 else a.reshape(1, -1)


def adamw(w, g, m, v, *, name):
    w2 = _as2d(w)
    outs = rowmap(_adamw, [w2, _as2d(g), _as2d(m), _as2d(v)], [], [(w2.shape[1], F32)] * 3, name=name)
    return tuple(o.reshape(w.shape) for o in outs)


HBM_SPEC = pl.BlockSpec(memory_space=pltpu.HBM)


def _place():
    mx, my, mc = lax.axis_index("x"), lax.axis_index("y"), lax.axis_index("c")
    others = [(1 - mx, my), (mx, 1 - my), (1 - mx, 1 - my)]
    return mx, my, mc, others


def chip_allgather(x, *, name):
    def body(x_ref, o_ref, send_sems, recv_sems, local_sem):
        mx, my, mc, others = _place()
        me = 2 * mx + my
        mine = pltpu.make_async_copy(x_ref, o_ref.at[me], local_sem)
        mine.start()
        sends = [pltpu.make_async_remote_copy(src_ref=x_ref, dst_ref=o_ref.at[me], send_sem=send_sems.at[j], recv_sem=recv_sems.at[j],
                                              device_id=(px, py, mc), device_id_type=MESH) for j, (px, py) in enumerate(others)]
        for cp in sends:
            cp.start()
        for j, (px, py) in enumerate(others):
            pltpu.make_async_remote_copy(src_ref=x_ref, dst_ref=o_ref.at[2 * px + py], send_sem=send_sems.at[j], recv_sem=recv_sems.at[j],
                                         device_id=(px, py, mc), device_id_type=MESH).wait_recv()
        for cp in sends:
            cp.wait_send()
        mine.wait()

    return pl.pallas_call(
        body, name=name, out_shape=jax.ShapeDtypeStruct((4,) + x.shape, x.dtype), in_specs=[HBM_SPEC], out_specs=HBM_SPEC,
        scratch_shapes=[pltpu.SemaphoreType.DMA((3,)), pltpu.SemaphoreType.DMA((3,)), pltpu.SemaphoreType.DMA])(x)


def _win(ref, axis, start, size):
    idx = [slice(None)] * len(ref.shape)
    idx[axis] = pl.ds(start, size)
    return ref.at[tuple(idx)]


def _half_axis(shape, ax):
    if shape[0] == 2:
        return 0
    return 2 if ax == 1 else 1


def _cut(shape, axis, parts):
    return shape[:axis] + (shape[axis] // parts,) + shape[axis + 1:]


def _hbm_call(body, arrays, out_shapes, sems, name):
    n_in = len(arrays)
    return pl.pallas_call(body, name=name, out_shape=tuple(out_shapes), in_specs=[HBM_SPEC] * n_in, out_specs=tuple([HBM_SPEC] * len(out_shapes)),
                          scratch_shapes=sems)(*arrays)


def place_shard(shard, ax, chip, dtype, *, name):
    l, r, c = shard.shape
    tr = _row_tile(r, c)
    per_block = (l, r // tr, 1)[ax]

    def omap(li, ri, cref):
        idx = [li, ri, 0]
        idx[ax] = idx[ax] + cref[0] * per_block
        return tuple(idx)

    def body(c_ref, s_ref, o_ref):
        o_ref[...] = s_ref[...].astype(dtype)

    full = shard.shape[:ax] + (4 * shard.shape[ax],) + shard.shape[ax + 1:]
    return pl.pallas_call(
        body, name=name, out_shape=jax.ShapeDtypeStruct(full, dtype),
        grid_spec=pltpu.PrefetchScalarGridSpec(
            num_scalar_prefetch=1, grid=(l, r // tr),
            in_specs=[pl.BlockSpec((1, tr, c), lambda li, ri, cref: (li, ri, 0))], out_specs=pl.BlockSpec((1, tr, c), omap)),
        compiler_params=_cp(("parallel", "parallel")))(chip, shard)


def gather_placed(arrays, axes, haxes, *, name):
    n = len(arrays)

    def body(*refs):
        ins, outs = refs[:n], refs[n:2 * n]
        send_sems, recv_sems = refs[2 * n:]
        mx, my, mc, others = _place()
        me = 2 * mx + my

        def part(ref, i, chip):
            sz, hs = arrays[i].shape[axes[i]] // 4, arrays[i].shape[haxes[i]] // 2
            return _win(_win(ref, axes[i], chip * sz, sz), haxes[i], mc * hs, hs)

        sends = []
        for i in range(n):
            for j, (px, py) in enumerate(others):
                rc = pltpu.make_async_remote_copy(src_ref=part(ins[i], i, me), dst_ref=part(outs[i], i, me), send_sem=send_sems.at[i, j],
                                                  recv_sem=recv_sems.at[i, j], device_id=(px, py, mc), device_id_type=MESH)
                rc.start()
                sends.append(rc)
        for i in range(n):
            for j, (px, py) in enumerate(others):
                pltpu.make_async_remote_copy(src_ref=part(ins[i], i, me), dst_ref=part(outs[i], i, 2 * px + py), send_sem=send_sems.at[i, j],
                                             recv_sem=recv_sems.at[i, j], device_id=(px, py, mc), device_id_type=MESH).wait_recv()
        for rc in sends:
            rc.wait_send()

    return pl.pallas_call(
        body, name=name, out_shape=tuple(jax.ShapeDtypeStruct(a_.shape, a_.dtype) for a_ in arrays), in_specs=[HBM_SPEC] * n,
        out_specs=tuple([HBM_SPEC] * n), input_output_aliases={i: i for i in range(n)},
        scratch_shapes=[pltpu.SemaphoreType.DMA((n, 3)), pltpu.SemaphoreType.DMA((n, 3))])(*arrays)


SEM_SPEC = pl.BlockSpec(memory_space=pltpu.SEMAPHORE)
SPLIT_COPY = pltpu.CompilerParams(has_side_effects=pltpu.SideEffectType.DATAFLOW_SIDE_EFFECTING)


def _gather_part(ref, shape, ax, hax, chip, core):
    sz, hs = shape[ax] // 4, shape[hax] // 2
    return _win(_win(ref, ax, chip * sz, sz), hax, core * hs, hs)


def gather_placed_start(arrays, axes, haxes, after, *, name):
    n = len(arrays)

    m = 3 * n

    def body(*refs):
        ins, send_sems, recv_sems = refs[:n], refs[n + 1:n + 1 + m], refs[n + 1 + m:n + 1 + 2 * m]
        token = refs[2 * n + 1 + 2 * m]
        mx, my, mc, others = _place()
        me = 2 * mx + my
        for i in range(n):
            for j, (px, py) in enumerate(others):
                part = _gather_part(ins[i], arrays[i].shape, axes[i], haxes[i], me, mc)
                pltpu.make_async_remote_copy(src_ref=part, dst_ref=part, send_sem=send_sems[3 * i + j], recv_sem=recv_sems[3 * i + j],
                                             device_id=(px, py, mc), device_id_type=MESH).start()
        token[...] = jnp.zeros_like(token)

    hbm = [pltpu.with_memory_space_constraint(a_, pltpu.HBM) for a_ in arrays]
    out = pl.pallas_call(
        body, name=name,
        out_shape=tuple([pltpu.SemaphoreType.DMA(())] * (2 * m)) + tuple(pltpu.HBM(a_.shape, a_.dtype) for a_ in arrays)
        + (jax.ShapeDtypeStruct((8, 128), F32),),
        in_specs=[HBM_SPEC] * n + [pl.BlockSpec(memory_space=pl.ANY)],
        out_specs=tuple([SEM_SPEC] * (2 * m)) + tuple([HBM_SPEC] * n) + (pl.BlockSpec(memory_space=pltpu.VMEM),),
        input_output_aliases={i: 2 * m + i for i in range(n)}, compiler_params=SPLIT_COPY)(*hbm, after)
    return list(out[:m]), list(out[m:2 * m]), list(out[2 * m:2 * m + n]), out[2 * m + n]


def gather_placed_wait(arrays, send_sems, recv_sems, axes, haxes, after, *, name):
    n = len(arrays)

    m = 3 * n

    def body(*refs):
        ins, send_refs, recv_refs = refs[:n], refs[n:n + m], refs[n + m:n + 2 * m]
        mx, my, mc, others = _place()
        me = 2 * mx + my
        for i in range(n):
            for j, (px, py) in enumerate(others):
                cp = pltpu.make_async_remote_copy(
                    src_ref=_gather_part(ins[i], arrays[i].shape, axes[i], haxes[i], me, mc),
                    dst_ref=_gather_part(ins[i], arrays[i].shape, axes[i], haxes[i], 2 * px + py, mc),
                    send_sem=send_refs[3 * i + j], recv_sem=recv_refs[3 * i + j], device_id=(px, py, mc), device_id_type=MESH)
                cp.wait_send()
                cp.wait_recv()

    out = pl.pallas_call(
        body, name=name, out_shape=tuple(pltpu.HBM(a_.shape, a_.dtype) for a_ in arrays),
        in_specs=[HBM_SPEC] * n + [SEM_SPEC] * (2 * m) + [pl.BlockSpec(memory_space=pl.ANY)], out_specs=tuple([HBM_SPEC] * n),
        input_output_aliases={i: i for i in range(n)}, compiler_params=SPLIT_COPY)(*arrays, *send_sems, *recv_sems, after)
    return list(out)


def pair_swap_halves(arrays, haxes, *, name):
    n = len(arrays)

    def body(*refs):
        ins, outs = refs[:n], refs[n:2 * n]
        send_sems, recv_sems = refs[2 * n:]
        mx, my, mc, _ = _place()
        cps = []
        for i in range(n):
            hs = arrays[i].shape[haxes[i]] // 2
            cp = pltpu.make_async_remote_copy(src_ref=_win(ins[i], haxes[i], (1 - mc) * hs, hs), dst_ref=outs[i], send_sem=send_sems.at[i],
                                              recv_sem=recv_sems.at[i], device_id=(mx, my, 1 - mc), device_id_type=MESH)
            cp.start()
            cps.append(cp)
        for cp in cps:
            cp.wait()

    outs = [jax.ShapeDtypeStruct(_cut(a_.shape, h_, 2), a_.dtype) for a_, h_ in zip(arrays, haxes)]
    return _hbm_call(body, arrays, outs, [pltpu.SemaphoreType.DMA((n,)), pltpu.SemaphoreType.DMA((n,))], name)


def add_own_half(g, t, hax, core, *, out_dtype, name):
    l, r, c = t.shape
    tr = _row_tile(r, c)
    per_half = (l, r // tr, 1)[hax]

    def imap(li, ri, cref):
        idx = [li, ri, 0]
        idx[hax] = idx[hax] + cref[0] * per_half
        return tuple(idx)

    def body(c_ref, g_ref, t_ref, o_ref):
        o_ref[...] = (g_ref[...] + t_ref[...]).astype(out_dtype)

    return pl.pallas_call(
        body, name=name, out_shape=jax.ShapeDtypeStruct(t.shape, out_dtype),
        grid_spec=pltpu.PrefetchScalarGridSpec(
            num_scalar_prefetch=1, grid=(l, r // tr),
            in_specs=[pl.BlockSpec((1, tr, c), imap), pl.BlockSpec((1, tr, c), lambda li, ri, cref: (li, ri, 0))],
            out_specs=pl.BlockSpec((1, tr, c), lambda li, ri, cref: (li, ri, 0))),
        compiler_params=_cp(("parallel", "parallel")))(core, g, t)


def exchange_blocks(arrays, axes, *, name):
    n = len(arrays)

    def body(*refs):
        ins, outs = refs[:n], refs[n:2 * n]
        send_sems, recv_sems, local_sems = refs[2 * n:]
        mx, my, mc, others = _place()
        me = 2 * mx + my
        waits = []
        for i in range(n):
            sz = arrays[i].shape[axes[i]] // 4
            cp = pltpu.make_async_copy(_win(ins[i], axes[i], me * sz, sz), outs[i].at[me], local_sems.at[i])
            cp.start()
            waits.append(cp.wait)
            for j, (px, py) in enumerate(others):
                rc = pltpu.make_async_remote_copy(src_ref=_win(ins[i], axes[i], (2 * px + py) * sz, sz), dst_ref=outs[i].at[me],
                                                  send_sem=send_sems.at[i, j], recv_sem=recv_sems.at[i, j], device_id=(px, py, mc),
                                                  device_id_type=MESH)
                rc.start()
                waits.append(rc.wait_send)
        for i in range(n):
            sz = arrays[i].shape[axes[i]] // 4
            for j, (px, py) in enumerate(others):
                pltpu.make_async_remote_copy(src_ref=_win(ins[i], axes[i], me * sz, sz), dst_ref=outs[i].at[2 * px + py],
                                             send_sem=send_sems.at[i, j], recv_sem=recv_sems.at[i, j], device_id=(px, py, mc),
                                             device_id_type=MESH).wait_recv()
        for w_ in waits:
            w_()

    outs = [jax.ShapeDtypeStruct((4,) + _cut(a_.shape, ax, 4), a_.dtype) for a_, ax in zip(arrays, axes)]
    return _hbm_call(body, arrays, outs, [pltpu.SemaphoreType.DMA((n, 3)), pltpu.SemaphoreType.DMA((n, 3)), pltpu.SemaphoreType.DMA((n,))], name)


def exchange_blocks_start(arrays, axes, *, name):
    n = len(arrays)
    lands = [lax.empty((4,) + _cut(a_.shape, ax, 4), a_.dtype) for a_, ax in zip(arrays, axes)]

    def body(*refs):
        ins, lnd = refs[:n], refs[n:2 * n]
        send_sems, recv_sems = refs[2 * n:6 * n], refs[6 * n:9 * n]
        token = refs[11 * n]
        mx, my, mc, others = _place()
        me = 2 * mx + my
        for i in range(n):
            sz = arrays[i].shape[axes[i]] // 4
            pltpu.make_async_copy(_win(ins[i], axes[i], me * sz, sz), lnd[i].at[me], send_sems[4 * i + 3]).start()
            for j, (px, py) in enumerate(others):
                pltpu.make_async_remote_copy(src_ref=_win(ins[i], axes[i], (2 * px + py) * sz, sz), dst_ref=lnd[i].at[me],
                                             send_sem=send_sems[4 * i + j], recv_sem=recv_sems[3 * i + j], device_id=(px, py, mc),
                                             device_id_type=MESH).start()
        token[...] = jnp.zeros_like(token)

    hbm = [pltpu.with_memory_space_constraint(a_, pltpu.HBM) for a_ in arrays + lands]
    out = pl.pallas_call(
        body, name=name,
        out_shape=tuple([pltpu.SemaphoreType.DMA(())] * (7 * n)) + tuple(pltpu.HBM(a_.shape, a_.dtype) for a_ in arrays + lands)
        + (jax.ShapeDtypeStruct((8, 128), F32),),
        in_specs=[HBM_SPEC] * (2 * n),
        out_specs=tuple([SEM_SPEC] * (7 * n)) + tuple([HBM_SPEC] * (2 * n)) + (pl.BlockSpec(memory_space=pltpu.VMEM),),
        input_output_aliases={i: 7 * n + i for i in range(2 * n)}, compiler_params=SPLIT_COPY)(*hbm)
    return list(out[:7 * n]), list(out[7 * n:8 * n]), list(out[8 * n:9 * n]), out[9 * n]


def exchange_blocks_wait(sems, arrays, lands, axes, after, *, name):
    n = len(arrays)

    def body(*refs):
        ins, lnd = refs[:n], refs[n:2 * n]
        send_sems, recv_sems = refs[2 * n:6 * n], refs[6 * n:9 * n]
        mx, my, mc, others = _place()
        me = 2 * mx + my
        for i in range(n):
            sz = arrays[i].shape[axes[i]] // 4
            mine = _win(ins[i], axes[i], me * sz, sz)
            pltpu.make_async_copy(mine, lnd[i].at[me], send_sems[4 * i + 3]).wait()
            for j, (px, py) in enumerate(others):
                cp = pltpu.make_async_remote_copy(src_ref=mine, dst_ref=lnd[i].at[2 * px + py], send_sem=send_sems[4 * i + j],
                                                  recv_sem=recv_sems[3 * i + j], device_id=(px, py, mc), device_id_type=MESH)
                cp.wait_send()
                cp.wait_recv()

    out = pl.pallas_call(
        body, name=name, out_shape=tuple(pltpu.HBM(a_.shape, a_.dtype) for a_ in arrays + lands),
        in_specs=[HBM_SPEC] * (2 * n) + [SEM_SPEC] * (7 * n) + [pl.BlockSpec(memory_space=pl.ANY)], out_specs=tuple([HBM_SPEC] * (2 * n)),
        input_output_aliases={i: i for i in range(2 * n)}, compiler_params=SPLIT_COPY)(*arrays, *lands, *sems, after)
    return list(out[n:])


def sum_blocks(e, hax, core, *, name):
    _, l, r, c = e.shape
    tr = _row_tile(r, c)
    per_half = (l, r // tr, 1)[hax]

    def omap(li, ri, cref):
        idx = [li, ri, 0]
        idx[hax] = idx[hax] + cref[0] * per_half
        return tuple(idx)

    def body(c_ref, e_ref, o_ref):
        v = e_ref[...].astype(F32)
        o_ref[...] = ((v[0] + v[1]) + v[2]) + v[3]

    full = (l, r, c)[:hax] + (2 * (l, r, c)[hax],) + (l, r, c)[hax + 1:]
    return pl.pallas_call(
        body, name=name, out_shape=jax.ShapeDtypeStruct(full, F32),
        grid_spec=pltpu.PrefetchScalarGridSpec(
            num_scalar_prefetch=1, grid=(l, r // tr),
            in_specs=[pl.BlockSpec((4, 1, tr, c), lambda li, ri, cref: (0, li, ri, 0))], out_specs=pl.BlockSpec((1, tr, c), omap)),
        compiler_params=_cp(("parallel", "parallel")))(core, e)


def pair_fill_halves(arrays, haxes, *, name):
    n = len(arrays)

    def body(*refs):
        ins, outs = refs[:n], refs[n:2 * n]
        send_sems, recv_sems = refs[2 * n:]
        mx, my, mc, _ = _place()
        cps = []
        for i in range(n):
            hs = arrays[i].shape[haxes[i]] // 2
            mine = _win(ins[i], haxes[i], mc * hs, hs)
            cp = pltpu.make_async_remote_copy(src_ref=mine, dst_ref=_win(outs[i], haxes[i], mc * hs, hs), send_sem=send_sems.at[i],
                                              recv_sem=recv_sems.at[i], device_id=(mx, my, 1 - mc), device_id_type=MESH)
            cp.start()
            cps.append(cp)
        for i in range(n):
            hs = arrays[i].shape[haxes[i]] // 2
            pltpu.make_async_remote_copy(src_ref=_win(ins[i], haxes[i], mc * hs, hs), dst_ref=_win(outs[i], haxes[i], (1 - mc) * hs, hs),
                                         send_sem=send_sems.at[i], recv_sem=recv_sems.at[i], device_id=(mx, my, 1 - mc),
                                         device_id_type=MESH).wait_recv()
        for cp in cps:
            cp.wait_send()

    return pl.pallas_call(
        body, name=name, out_shape=tuple(jax.ShapeDtypeStruct(a_.shape, a_.dtype) for a_ in arrays), in_specs=[HBM_SPEC] * n,
        out_specs=tuple([HBM_SPEC] * n), input_output_aliases={i: i for i in range(n)},
        scratch_shapes=[pltpu.SemaphoreType.DMA((n,)), pltpu.SemaphoreType.DMA((n,))])(*arrays)


WEIGHTS = ['c_ctx', 'w_mod', 'b_mod', 'norm1_w', 'norm2_w', 'final_norm_w', 's5_w_in', 's5_lam_re', 's5_lam_im', 's5_log_step', 's5_b_re', 's5_b_im', 's5_c_re', 's5_c_im', 's5_d', 's5_w_glu', 's5_w_out', 'hg_w_in', 'hg_lower_bounds', 'hg_gnorm_w', 'hg_w_out', 'ffn_w_up', 'ffn_conv_w', 'ffn_conv_b', 'ffn_w_down']
INPUTS = ['x', 'c', 'ctx', 'c_ctx', 'w_mod', 'b_mod', 'norm1_w', 'norm2_w', 'final_norm_w', 's5_w_in', 's5_lam_re', 's5_lam_im', 's5_log_step', 's5_b_re', 's5_b_im', 's5_c_re', 's5_c_im', 's5_d', 's5_w_glu', 's5_w_out', 'hg_w_in', 'hg_lower_bounds', 'hg_gnorm_w', 'hg_w_out', 'ffn_w_up', 'ffn_conv_w', 'ffn_conv_b', 'ffn_w_down', 'loss_target', 'm_c_ctx', 'm_w_mod', 'm_b_mod', 'm_norm1_w', 'm_norm2_w', 'm_final_norm_w', 'm_s5_w_in', 'm_s5_lam_re', 'm_s5_lam_im', 'm_s5_log_step', 'm_s5_b_re', 'm_s5_b_im', 'm_s5_c_re', 'm_s5_c_im', 'm_s5_d', 'm_s5_w_glu', 'm_s5_w_out', 'm_hg_w_in', 'm_hg_lower_bounds', 'm_hg_gnorm_w', 'm_hg_w_out', 'm_ffn_w_up', 'm_ffn_conv_w', 'm_ffn_conv_b', 'm_ffn_w_down', 'v_c_ctx', 'v_w_mod', 'v_b_mod', 'v_norm1_w', 'v_norm2_w', 'v_final_norm_w', 'v_s5_w_in', 'v_s5_lam_re', 'v_s5_lam_im', 'v_s5_log_step', 'v_s5_b_re', 'v_s5_b_im', 'v_s5_c_re', 'v_s5_c_im', 'v_s5_d', 'v_s5_w_glu', 'v_s5_w_out', 'v_hg_w_in', 'v_hg_lower_bounds', 'v_hg_gnorm_w', 'v_hg_w_out', 'v_ffn_w_up', 'v_ffn_conv_w', 'v_ffn_conv_b', 'v_ffn_w_down']
SHARD_AXIS = {"w_mod": 2, "s5_w_in": 1, "s5_w_glu": 1, "s5_w_out": 1, "hg_w_in": 2, "hg_lower_bounds": 2, "hg_w_out": 1,
              "ffn_w_up": 2, "ffn_conv_w": 2, "ffn_w_down": 1}
GATHER_F32 = ("hg_lower_bounds", "ffn_conv_w")
PACK_W = 1024
GRAD_WIRE = jnp.bfloat16


def _reduce_start(items, core, tag):
    names, arrays, axes = [n for n, _, _ in items], [g_ for _, g_, _ in items], [ax for _, _, ax in items]
    haxes = [_half_axis(g_.shape, ax) for g_, ax in zip(arrays, axes)]
    t = pair_swap_halves(arrays, haxes, name="grad_pair_swap_" + tag)
    h = [add_own_half(g_, t_, hx, core, out_dtype=GRAD_WIRE, name="grad_pair_add_" + n) for g_, t_, hx, n in zip(arrays, t, haxes, names)]
    sems, h, lands, token = exchange_blocks_start(h, axes, name="grad_exchange_start_" + tag)
    return (names, sems, h, lands, axes, haxes), token


def _reduce_finish(state, core, after, tag):
    names, sems, h, lands, axes, haxes = state
    e = exchange_blocks_wait(sems, h, lands, axes, after, name="grad_exchange_wait_" + tag)
    s = [sum_blocks(e_, hx, core, name="grad_chip_sum_" + n) for e_, hx, n in zip(e, haxes, names)]
    return dict(zip(names, pair_fill_halves(s, haxes, name="grad_pair_fill_" + tag)))


def _pack_small(grads, small):
    flat = jnp.concatenate([grads[n].reshape(-1) for n in small])
    pad = (-flat.shape[0]) % (64 * PACK_W)
    return jnp.pad(flat, (0, pad)).reshape(1, -1, PACK_W)


def _unpack_small(a, block, small):
    sm = chip_allgather(block[0], name="allgather_small_grads").reshape(-1)
    out, off = {}, 0
    for n in small:
        out[n] = sm[off:off + math.prod(a[n].shape)].reshape(a[n].shape)
        off += math.prod(a[n].shape)
    return out


def _blockdiag_b(bb, kb):
    gl = S5_KIN // S5_GROUP
    x = bb.reshape(kb, gl, S5_GROUP, S5_STATE)
    return (x[:, :, :, None, :] * jnp.eye(gl, dtype=bb.dtype)[None, :, None, :, None]).reshape(kb, S5_KIN, S5_KST)


def _blockdiag_c(cc, kb):
    gl = S5_KIN // S5_GROUP
    x = cc.reshape(kb, gl, S5_GROUP, S5_STATE).transpose(0, 1, 3, 2)
    return (x[:, :, :, None, :] * jnp.eye(gl, dtype=cc.dtype)[None, :, None, :, None]).reshape(kb, S5_KST, S5_KIN)


def _diag_b(m, kb):
    gl = S5_KIN // S5_GROUP
    x = m.reshape(kb, gl, S5_GROUP, gl, S5_STATE)
    return jnp.stack([x[:, i, :, i, :] for i in range(gl)], axis=1).reshape(kb * gl, S5_GROUP, S5_STATE)


def _diag_c(m, kb):
    gl = S5_KIN // S5_GROUP
    x = m.reshape(kb, gl, S5_STATE, gl, S5_GROUP)
    return jnp.stack([x[:, i, :, i, :] for i in range(gl)], axis=1).transpose(0, 1, 3, 2).reshape(kb * gl, S5_GROUP, S5_STATE)


def kernel(x, c, ctx, c_ctx, w_mod, b_mod, norm1_w, norm2_w, final_norm_w, s5_w_in, s5_lam_re, s5_lam_im, s5_log_step, s5_b_re, s5_b_im, s5_c_re, s5_c_im, s5_d, s5_w_glu, s5_w_out, hg_w_in, hg_lower_bounds, hg_gnorm_w, hg_w_out, ffn_w_up, ffn_conv_w, ffn_conv_b, ffn_w_down, loss_target, m_c_ctx, m_w_mod, m_b_mod, m_norm1_w, m_norm2_w, m_final_norm_w, m_s5_w_in, m_s5_lam_re, m_s5_lam_im, m_s5_log_step, m_s5_b_re, m_s5_b_im, m_s5_c_re, m_s5_c_im, m_s5_d, m_s5_w_glu, m_s5_w_out, m_hg_w_in, m_hg_lower_bounds, m_hg_gnorm_w, m_hg_w_out, m_ffn_w_up, m_ffn_conv_w, m_ffn_conv_b, m_ffn_w_down, v_c_ctx, v_w_mod, v_b_mod, v_norm1_w, v_norm2_w, v_final_norm_w, v_s5_w_in, v_s5_lam_re, v_s5_lam_im, v_s5_log_step, v_s5_b_re, v_s5_b_im, v_s5_c_re, v_s5_c_im, v_s5_d, v_s5_w_glu, v_s5_w_out, v_hg_w_in, v_hg_lower_bounds, v_hg_gnorm_w, v_hg_w_out, v_ffn_w_up, v_ffn_conv_w, v_ffn_conv_b, v_ffn_w_down):
    a = dict(zip(INPUTS, (x, c, ctx, c_ctx, w_mod, b_mod, norm1_w, norm2_w, final_norm_w, s5_w_in, s5_lam_re, s5_lam_im, s5_log_step, s5_b_re, s5_b_im, s5_c_re, s5_c_im, s5_d, s5_w_glu, s5_w_out, hg_w_in, hg_lower_bounds, hg_gnorm_w, hg_w_out, ffn_w_up, ffn_conv_w, ffn_conv_b, ffn_w_down, loss_target, m_c_ctx, m_w_mod, m_b_mod, m_norm1_w, m_norm2_w, m_final_norm_w, m_s5_w_in, m_s5_lam_re, m_s5_lam_im, m_s5_log_step, m_s5_b_re, m_s5_b_im, m_s5_c_re, m_s5_c_im, m_s5_d, m_s5_w_glu, m_s5_w_out, m_hg_w_in, m_hg_lower_bounds, m_hg_gnorm_w, m_hg_w_out, m_ffn_w_up, m_ffn_conv_w, m_ffn_conv_b, m_ffn_w_down, v_c_ctx, v_w_mod, v_b_mod, v_norm1_w, v_norm2_w, v_final_norm_w, v_s5_w_in, v_s5_lam_re, v_s5_lam_im, v_s5_log_step, v_s5_b_re, v_s5_b_im, v_s5_c_re, v_s5_c_im, v_s5_d, v_s5_w_glu, v_s5_w_out, v_hg_w_in, v_hg_lower_bounds, v_hg_gnorm_w, v_hg_w_out, v_ffn_w_up, v_ffn_conv_w, v_ffn_conv_b, v_ffn_w_down)))
    nb, seq, d = x.shape
    assert nb == NB
    rc = nb * ctx.shape[1]
    cfg = {"rc": rc}
    f = a["ffn_w_down"].shape[1] * 4
    core = lax.axis_index("c").astype(jnp.int32).reshape(1)

    w = {n: a[n] for n in WEIGHTS if n not in SHARD_AXIS}
    chip = (2 * lax.axis_index("x") + lax.axis_index("y")).astype(jnp.int32).reshape(1)
    groups = {
        "now": [("w_mod0", a["w_mod"][0:1]), ("s5_w_in", a["s5_w_in"]), ("hg_lower_bounds", a["hg_lower_bounds"]), ("ffn_conv_w", a["ffn_conv_w"])],
        "mid": [("s5_w_glu", a["s5_w_glu"]), ("s5_w_out", a["s5_w_out"]), ("ffn_w_up0", a["ffn_w_up"][0:1]), ("ffn_w_down0", a["ffn_w_down"][0:1])],
        "later": [("w_mod1", a["w_mod"][1:2]), ("hg_w_in", a["hg_w_in"]), ("hg_w_out", a["hg_w_out"]), ("ffn_w_up1", a["ffn_w_up"][1:2]),
                  ("ffn_w_down1", a["ffn_w_down"][1:2])]}
    shard_axis = lambda n: SHARD_AXIS[n.rstrip("01")]
    placed = {g: [place_shard(s_, shard_axis(n), chip, F32 if n in GATHER_F32 else MXU, name="place_" + n) for n, s_ in it] for g, it in groups.items()}
    axes = {g: [shard_axis(n) for n, _ in it] for g, it in groups.items()}
    haxes = {g: [_half_axis(p_.shape, ax) for p_, ax in zip(placed[g], axes[g])] for g in groups}
    fly_now = gather_placed_start(placed["now"], axes["now"], haxes["now"], chip, name="allgather_now_start")
    fly_mid = gather_placed_start(placed["mid"], axes["mid"], haxes["mid"], fly_now[3], name="allgather_mid_start")
    fly_later = gather_placed_start(placed["later"], axes["later"], haxes["later"], fly_mid[3], name="allgather_later_start")

    def land(fly, g, after):
        send_, recv_, flying, _ = fly
        landed = gather_placed_wait(flying, send_, recv_, axes[g], haxes[g], after, name=f"allgather_{g}_wait")
        w.update(dict(zip([n for n, _ in groups[g]], pair_fill_halves(landed, haxes[g], name=f"allgather_{g}_pair_fill"))))

    tmaj = lambda t: t.transpose(1, 0, 2).reshape(-1, t.shape[-1])
    zero = fly_later[3][0:1, 0:1]
    x0 = jnp.concatenate([tmaj(ctx), tmaj(x)], axis=0)
    tgt = tmaj(a["loss_target"])
    land(fly_now, "now", x0)
    c16 = jnp.concatenate([jnp.broadcast_to(c_ctx[None], (8, d)), c, c], axis=0) + zero
    mt0, scb = mod_fwd(c16, w["w_mod0"][0], w["b_mod"][0][None], name="mod_fwd0")
    mt = [mt0, None]
    n1, n2 = w["norm1_w"], w["norm2_w"]
    w["w_mod"], w["ffn_w_up"], w["ffn_w_down"] = [w["w_mod0"][0], None], [None, None], [None, None]

    def ffn_fwd(l, h):
        u = mm(h, w["ffn_w_up"][l], name=f"ffn_up{l}")
        act, ca, cg = ffn_mid_fwd(cfg, u, w["ffn_conv_w"][l], w["ffn_conv_b"][l][None], name=f"ffn_mid{l}")
        return (u, ca, cg), act, mm(act, w["ffn_w_down"][l], name=f"ffn_down{l}")

    def ffn_bwd(l, dfo, kept, act, h, zero=0.0):
        dact = mm(dfo, w["ffn_w_down"][l], tb=True, out_dtype=MXU, name=f"ffn_down_dx{l}")
        dwd = mm(act, dfo, ta=True, name=f"ffn_down_dw{l}")
        du, dcw, dcb = ffn_mid_bwd(cfg, dact, *kept, w["ffn_conv_w"][l] + zero, name=f"ffn_mid_bwd{l}")
        dh = mm(du, w["ffn_w_up"][l], tb=True, name=f"ffn_up_dx{l}")
        dwu = mm(h, du, ta=True, name=f"ffn_up_dw{l}")
        return dh, dwu, dcw, dcb[0], dwd

    g_, p_ = d // S5_GROUP, S5_STATE
    ns, kb = g_ * p_, d // S5_KIN
    s5p = (w["s5_lam_re"][0].reshape(2 * g_, p_), w["s5_lam_im"][0].reshape(2 * g_, p_), w["s5_log_step"][0].reshape(2 * g_, 1),
           w["s5_b_re"][0].transpose(0, 1, 3, 2).reshape(2 * g_, S5_GROUP, p_), w["s5_b_im"][0].transpose(0, 1, 3, 2).reshape(2 * g_, S5_GROUP, p_))
    ar, ai, bbr, bbi = s5_disc_fwd(*s5p, name="s5_disc")
    dsk = w["s5_d"]
    _, h1 = node_fwd(cfg, x0, None, None, 0, n1[0:1], mt[0], 0, name="node0a")
    u0 = mm(h1, w["s5_w_in"][0], name="s5_in")
    s5s, ys = [], []
    for dd in range(2):
        sl = slice(dd * g_, (dd + 1) * g_)
        a_r, a_i = ar[sl].reshape(1, ns), ai[sl].reshape(1, ns)
        a2 = (a_r * a_r - a_i * a_i, 2.0 * a_r * a_i)
        b_r, b_i = _blockdiag_b(bbr[sl], kb), _blockdiag_b(bbi[sl], kb)
        c_r, c_i = _blockdiag_c(w["s5_c_re"][0, dd], kb), _blockdiag_c(w["s5_c_im"][0, dd], kb)
        ak, ai_k = a_r.reshape(kb, 1, S5_KST), a_i.reshape(kb, 1, S5_KST)
        ab = (ak * b_r - ai_k * b_i, ak * b_i + ai_k * b_r)
        akc, aic = ak.reshape(kb, S5_KST, 1), ai_k.reshape(kb, S5_KST, 1)
        c2 = (akc * c_r - aic * c_i, akc * c_i + aic * c_r)
        bf = lambda t_: t_.astype(MXU)
        sre, sim, ere, eim, y_ = s5_scan_fwd(cfg, u0, a2[0], a2[1], bf(b_r), bf(b_i), bf(ab[0]), bf(ab[1]), bf(c_r), bf(c_i), rev=dd == 1,
                                             name=f"s5_scan{dd}")
        s5s.append((sre, sim, ere, eim, a2[0], a2[1], bf(b_r), bf(b_i), bf(c_r), bf(c_i), bf(c2[0]), bf(c2[1])))
        ys.append(y_)

    def glu_a(u, y0, y1, ds):
        yp = (ds * u + y0) + y1
        return yp, _gelu(yp)

    ypre, zgb = rowmap(glu_a, [u0, ys[0], ys[1]], [dsk], [(d, F32), (d, MXU)], name="s5_glu_a")
    land(fly_mid, "mid", zgb)
    w["ffn_w_up"][0], w["ffn_w_down"][0] = w["ffn_w_up0"][0], w["ffn_w_down0"][0]
    tg = mm(zgb, w["s5_w_glu"][0], name="s5_glu")
    (z2,) = rowmap(lambda yp, t: _gelu(yp) * jax.nn.sigmoid(t), [ypre, tg], [], [(d, MXU)], name="s5_glu_b")
    y1a = mm(z2, w["s5_w_out"][0], name="s5_out")
    x1a, h2a = node_fwd(cfg, x0, y1a, mt[0], 2, n2[0:1], mt[0], 3, name="node0b")
    ufa, acta, foa = ffn_fwd(0, h2a)

    land(fly_later, "later", foa)
    w["w_mod"][1], w["ffn_w_up"][1], w["ffn_w_down"][1] = w["w_mod1"][0], w["ffn_w_up1"][0], w["ffn_w_down1"][0]
    mt[1], _ = mod_fwd(c16, w["w_mod"][1], w["b_mod"][1][None], name="mod_fwd1")
    x2a, h1b = node_fwd(cfg, x1a, foa, mt[0], 5, n1[1:2], mt[1], 0, name="node1a")
    z = mm(h1b, w["hg_w_in"][0], name="hg_in")
    e0, e1 = w["hg_lower_bounds"][:, 0, :], w["hg_lower_bounds"][:, 1, :]
    lb = hg_lb_fwd(e0, e1, name="hg_lb")
    gw = w["hg_gnorm_w"]
    o0, sin0 = hg_scan_fwd(cfg, z, lb[0:1], d_dir=0, name="hg_scan0")
    o1, sin1 = hg_scan_fwd(cfg, z, lb[1:2], d_dir=1, name="hg_scan1")
    onb = hg_read_fwd(o0, o1, z, gw, name="hg_read")
    y1b = mm(onb, w["hg_w_out"][0], name="hg_out")
    x1b, h2b = node_fwd(cfg, x2a, y1b, mt[1], 2, n2[1:2], mt[1], 3, name="node1b")
    ufb, actb, fob = ffn_fwd(1, h2b)
    loss_p, dx2b, dfob, dg2_1, dfnw = final_node(cfg, x1b, fob, mt[1], 5, w["final_norm_w"][None], tgt, name="final_node")

    gr = {}
    dh2b, dwu1, dcw1, dcb1, dwd1 = ffn_bwd(1, dfob, ufb, actb, h2b)
    dx1b, dy1b, dn2_1, dsh2_1, dsc2_1, dg1_1 = node_bwd(cfg, dx2b, dh2b, x1b, y1b, mt[1], 2, n2[1:2], mt[1], 3, name="node1b_bwd")
    don = mm(dy1b, w["hg_w_out"][0], tb=True, name="hg_out_dx")
    gr["hg_w_out"] = mm(onb, dy1b, ta=True, name="hg_out_dw")[None]
    do_, dgate_, dgw = hg_read_bwd(don, o0, o1, z, gw, name="hg_read_bwd")
    dq, dv, dxf, dlb0 = hg_scan_bwd(cfg, do_, z, lb[0:1], sin0, None, None, d_dir=0, name="hg_scan_bwd0")
    dq, dv, dxb, dlb1 = hg_scan_bwd(cfg, do_, z, lb[1:2], sin1, dq, dv, d_dir=1, name="hg_scan_bwd1")
    dz = jnp.concatenate([t_.astype(MXU) for t_ in (dq, dv, dxf, dxb, dgate_)], axis=1)
    dh1b = mm(dz, w["hg_w_in"][0], tb=True, name="hg_in_dx")
    gr["hg_w_in"] = mm(h1b, dz, ta=True, name="hg_in_dw")[None]
    de0, de1 = hg_lb_bwd(e0, e1, jnp.concatenate([dlb0, dlb1], axis=0), name="hg_lb_bwd")
    gr["hg_lower_bounds"] = jnp.stack([de0, de1], axis=1)
    gr["hg_gnorm_w"] = dgw
    dx2a, dfoa, dn1_1, dsh1_1, dsc1_1, dg2_0 = node_bwd(cfg, dx1b, dh1b, x2a, foa, mt[0], 5, n1[1:2], mt[1], 0, name="node1a_bwd")
    dmt1 = jnp.concatenate([dsh1_1, dsc1_1, dg1_1, dsh2_1, dsc2_1, dg2_1], axis=1)
    red1, tok1 = _reduce_start([("hg_w_in", gr["hg_w_in"], 2), ("hg_w_out", gr["hg_w_out"], 1), ("ffn_w_up1", dwu1[None], 2),
                                ("ffn_w_down1", dwd1[None], 1), ("w_mod1", mm(scb, dmt1, ta=True, name="mod_dw1")[None], 2)], core, "layer1")

    dh2a, dwu0, dcw0, dcb0, dwd0 = ffn_bwd(0, dfoa, ufa, acta, h2a, zero=tok1[0:1, 0:1])
    red2, tok2 = _reduce_start([("ffn_w_up0", dwu0[None], 2), ("ffn_w_down0", dwd0[None], 1)], core, "ffn0")
    dx1a, dy1a, dn2_0, dsh2_0, dsc2_0, dg1_0 = node_bwd(cfg, dx2a, dh2a, x1a, y1a, mt[0], 2, n2[0:1] + tok2[0:1, 0:1], mt[0], 3,
                                                        name="node0b_bwd")
    dz2 = mm(dy1a, w["s5_w_out"][0], tb=True, name="s5_out_dx")
    gr["s5_w_out"] = mm(z2, dy1a, ta=True, name="s5_out_dw")[None]

    def glu_b_bwd(dz2_, yp, t):
        zg, sg = _gelu(yp), jax.nn.sigmoid(t)
        return dz2_ * zg * sg * (1.0 - sg), dz2_ * sg

    dtg, dzg_dir = rowmap(glu_b_bwd, [dz2, ypre, tg], [], [(d, MXU), (d, F32)], name="s5_glu_b_bwd")
    dzg_mm = mm(dtg, w["s5_w_glu"][0], tb=True, name="s5_glu_dx")
    gr["s5_w_glu"] = mm(zgb, dtg, ta=True, name="s5_glu_dw")[None]

    def glu_a_bwd(dzd, dzm, yp, u, ds):
        _, vjp = jax.vjp(_gelu, yp)
        (dy,) = vjp(dzd + dzm)
        return dy, dy * ds, jnp.sum(dy * u, axis=0, keepdims=True)

    dyb, du, ddsk = rowmap(glu_a_bwd, [dzg_dir, dzg_mm, ypre, u0], [dsk], [(d, MXU), (d, F32)], [(1, d)], name="s5_glu_a_bwd")
    gr["s5_d"] = ddsk
    dar, dai, dbr, dbi, dcr, dci = [], [], [], [], [], []
    for dd in range(2):
        sre, sim, ere, eim = s5s[dd][:4]
        du, gre, gim, da_r, da_i = s5_scan_bwd(cfg, dyb, *s5s[dd], du, rev=dd == 1, name=f"s5_scan_bwd{dd}")
        dar.append(colsum(da_r, name=f"s5_da_re{dd}").reshape(g_, p_))
        dai.append(colsum(da_i, name=f"s5_da_im{dd}").reshape(g_, p_))
        dbr.append(_diag_b(blockdiag_tn(u0, gre, S5_KIN, S5_KST, name=f"s5_db_re{dd}"), kb))
        dbi.append(_diag_b(blockdiag_tn(u0, gim, S5_KIN, S5_KST, name=f"s5_db_im{dd}"), kb))
        dcr.append(_diag_c(blockdiag_tn(sre.reshape(-1, ns), dyb, S5_KST, S5_KIN, name=f"s5_dc_re{dd}"), kb))
        dci.append(_diag_c(blockdiag_tn(sim.reshape(-1, ns), dyb, S5_KST, S5_KIN, scale=-1.0, name=f"s5_dc_im{dd}"), kb))
    cat = lambda l_: jnp.concatenate(l_, axis=0)
    dlr, dli, dls, dbre, dbim = s5_disc_bwd(*s5p, cat(dar), cat(dai), cat(dbr), cat(dbi), name="s5_disc_bwd")
    gr["s5_lam_re"], gr["s5_lam_im"] = dlr.reshape(1, 2, g_, p_), dli.reshape(1, 2, g_, p_)
    gr["s5_log_step"] = dls.reshape(1, 2, g_)
    gr["s5_b_re"] = dbre.reshape(1, 2, g_, S5_GROUP, p_).transpose(0, 1, 2, 4, 3)
    gr["s5_b_im"] = dbim.reshape(1, 2, g_, S5_GROUP, p_).transpose(0, 1, 2, 4, 3)
    gr["s5_c_re"], gr["s5_c_im"] = jnp.stack(dcr)[None], jnp.stack(dci)[None]
    dh1 = mm(du, w["s5_w_in"][0], tb=True, name="s5_in_dx")
    gr["s5_w_in"] = mm(h1, du, ta=True, name="s5_in_dw")[None]
    dx0, _, dn1_0, dsh1_0, dsc1_0, _ = node_bwd(cfg, dx1a, dh1, x0, None, None, 0, n1[0:1], mt[0], 0, name="node0a_bwd")

    dmt = [jnp.concatenate([dsh1_0, dsc1_0, dg1_0, dsh2_0, dsc2_0, dg2_0], axis=1), dmt1]
    gr["b_mod"] = jnp.concatenate([colsum(dmt[l], name=f"mod_db{l}") for l in range(2)], axis=0)
    dsc16 = [mm(dmt[l], w["w_mod"][l], tb=True, name=f"mod_dx{l}") for l in range(2)]
    gr["c_ctx"] = cctx_grad(c16, dsc16, name="c_ctx_grad")[0]
    gr["norm1_w"] = jnp.concatenate([dn1_0, dn1_1], axis=0)
    gr["norm2_w"] = jnp.concatenate([dn2_0, dn2_1], axis=0)
    gr["final_norm_w"] = dfnw[0]
    gr["ffn_conv_w"], gr["ffn_conv_b"] = jnp.stack([dcw0, dcw1]), jnp.stack([dcb0, dcb1])

    last = [(n, gr[n], SHARD_AXIS[n]) for n in ("s5_w_in", "s5_w_glu", "s5_w_out", "hg_lower_bounds", "ffn_conv_w")]
    last.append(("w_mod0", mm(scb, dmt[0], ta=True, name="mod_dw0")[None], 2))
    small = [n for n in WEIGHTS if n not in SHARD_AXIS]
    last.append(("small", _pack_small(gr, small), 1))
    red3, _ = _reduce_start(last, core, "last")
    red = _reduce_finish(red1, core, dx0, "layer1")
    red.update(_reduce_finish(red2, core, dx0, "ffn0"))
    red["ffn_w_up"] = jnp.concatenate([red["ffn_w_up0"], red["ffn_w_up1"]], axis=0)
    red["ffn_w_down"] = jnp.concatenate([red["ffn_w_down0"], red["ffn_w_down1"]], axis=0)
    early = ("hg_w_in", "hg_w_out", "ffn_w_up", "ffn_w_down")
    upd = {n: adamw(a[n], red[n], a["m_" + n], a["v_" + n], name="adamw_" + n) for n in early}
    red.update(_reduce_finish(red3, core, upd["ffn_w_up"][0], "last"))
    red.update(_unpack_small(a, red["small"], small))
    red["w_mod"] = jnp.concatenate([red["w_mod0"], red["w_mod1"]], axis=0)
    loss = lax.psum(loss_p[0, 0], ("x", "y", "c"))
    grad_x = dx0[rc:].reshape(seq, nb, d).transpose(1, 0, 2)
    upd.update({n: adamw(a[n], red[n], a["m_" + n], a["v_" + n], name="adamw_" + n) for n in WEIGHTS if n not in early})
    return (loss, grad_x, *[red[n] for n in WEIGHTS], *[upd[n][0] for n in WEIGHTS], *[upd[n][1] for n in WEIGHTS],
            *[upd[n][2] for n in WEIGHTS])
```

```python
import functools
import math

import jax
import jax.numpy as jnp
from jax import lax
from jax.experimental import pallas as pl
from jax.experimental.pallas import tpu as pltpu

F32 = jnp.float32
BF = jnp.bfloat16
MXU = jnp.bfloat16

NORM_EPS = 1e-6
GRID_W = 64
N_MOD = 6
S5_GROUP = 16
S5_STATE = 64
S5_LAM_RE_MAX = -1e-4
S5_KIN = 256
S5_KST = S5_KIN // S5_GROUP * S5_STATE
HEAD = 128
CHUNK_ROWS = 128
N_PROJ = 5
NB = 4
ADAM_LR, ADAM_B1, ADAM_B2, ADAM_EPS, ADAM_WD, ADAM_STEP = 0.001, 0.9, 0.999, 1e-08, 0.01, 10
VMEM_LIMIT = 56 * 1024 * 1024
MESH = pl.DeviceIdType.MESH


def _tile(n, cap):
    if n <= cap:
        return n
    best = None
    for t in range(128, cap + 1, 128):
        if n % t == 0:
            best = t
    assert best is not None, (n, cap)
    return best


def _row_tile(r, width=1024):
    cap = max(8, (512 * 1024) // max(width, 1))
    return next((t for t in (512, 256, 128, 64, 32, 16, 8) if t <= cap and r % t == 0), r)


def _cp(sem):
    return pltpu.CompilerParams(dimension_semantics=sem, vmem_limit_bytes=VMEM_LIMIT)


def _dot(a, b, ca=1, cb=0):
    return lax.dot_general(a.astype(MXU), b.astype(MXU), (((ca,), (cb,)), ((), ())), preferred_element_type=F32)


def _dot3(m, x):
    hi = x.astype(MXU)
    r1 = x - hi.astype(F32)
    mid = r1.astype(MXU)
    lo = (r1 - mid.astype(F32)).astype(MXU)
    return _dot(m, hi) + _dot(m, mid) + _dot(m, lo)


def mm(a, b, *, ta=False, tb=False, out_dtype=F32, name):
    (kd, m) = a.shape if ta else a.shape[::-1]
    (n, kd2) = b.shape if tb else b.shape[::-1]
    assert kd == kd2, (a.shape, b.shape, ta, tb)
    tm, tn, tk = _tile(m, 1024), _tile(n, 1536), _tile(kd, 1024)
    nk = kd // tk

    def body(a_ref, b_ref, o_ref, acc_ref):
        k = pl.program_id(2)

        @pl.when(k == 0)
        def _():
            acc_ref[...] = jnp.zeros_like(acc_ref)

        acc_ref[...] += _dot(a_ref[...], b_ref[...], 0 if ta else 1, 1 if tb else 0)

        @pl.when(k == nk - 1)
        def _():
            o_ref[...] = acc_ref[...].astype(out_dtype)

    a_spec = pl.BlockSpec((tk, tm), lambda i, j, k: (k, i)) if ta else pl.BlockSpec((tm, tk), lambda i, j, k: (i, k))
    b_spec = pl.BlockSpec((tn, tk), lambda i, j, k: (j, k)) if tb else pl.BlockSpec((tk, tn), lambda i, j, k: (k, j))
    return pl.pallas_call(
        body, name=name, grid=(m // tm, n // tn, nk), in_specs=[a_spec, b_spec],
        out_specs=pl.BlockSpec((tm, tn), lambda i, j, k: (i, j)), out_shape=jax.ShapeDtypeStruct((m, n), out_dtype),
        scratch_shapes=[pltpu.VMEM((tm, tn), F32)], compiler_params=_cp(("parallel", "parallel", "arbitrary")))(a, b)


def blockdiag_tn(a, b, wa, wb, *, scale=1.0, name):
    rows = a.shape[0]
    kb = a.shape[1] // wa
    tr = _tile(rows, 1024)
    nr = rows // tr

    def body(a_ref, b_ref, o_ref):
        i = pl.program_id(1)

        @pl.when(i == 0)
        def _():
            o_ref[...] = jnp.zeros_like(o_ref)

        o_ref[0] += scale * _dot(a_ref[...], b_ref[...], 0, 0)

    return pl.pallas_call(
        body, name=name, grid=(kb, nr),
        in_specs=[pl.BlockSpec((tr, wa), lambda k, i: (i, k)), pl.BlockSpec((tr, wb), lambda k, i: (i, k))],
        out_specs=pl.BlockSpec((1, wa, wb), lambda k, i: (k, 0, 0)), out_shape=jax.ShapeDtypeStruct((kb, wa, wb), F32),
        compiler_params=_cp(("parallel", "arbitrary")))(a, b)


def _pat(v, p, op):
    tm, d = v.shape
    return op(v.reshape(tm // 8, 8, d), p[None]).reshape(tm, d)


def _norm_mod(x, nw, shift, scale):
    y = x * lax.rsqrt(jnp.mean(x * x, axis=-1, keepdims=True) + NORM_EPS) * nw
    return _pat(_pat(y, 1.0 + scale, jnp.multiply), shift, jnp.add)


def _mt_spec(d, nct):
    return pl.BlockSpec((8, N_MOD * d), lambda i: (jnp.where(i < nct, 0, 1), 0))


def _acc_spec(d, nct):
    return pl.BlockSpec((8, d), lambda i: (jnp.where(i < nct, 0, 1), 0))


def _rows(cfg):
    tm = min(512, cfg["rc"])
    return tm, cfg["rc"] // tm


def node_fwd(cfg, xp, y, mtg, gi, nw, mtn, si, *, name):
    r, d = xp.shape
    tm, nct = _rows(cfg)
    row = pl.BlockSpec((tm, d), lambda i: (i, 0))
    vec = pl.BlockSpec((1, d), lambda i: (0, 0))

    def body(*refs):
        if y is None:
            xp_ref, nw_ref, mtn_ref, h_ref = refs
            x = xp_ref[...]
        else:
            xp_ref, y_ref, mtg_ref, nw_ref, mtn_ref, xn_ref, h_ref = refs
            x = xp_ref[...] + _pat(y_ref[...], mtg_ref[:, gi * d:(gi + 1) * d], jnp.multiply)
            xn_ref[...] = x
        h_ref[...] = _norm_mod(x, nw_ref[...], mtn_ref[:, si * d:(si + 1) * d], mtn_ref[:, (si + 1) * d:(si + 2) * d]).astype(MXU)

    h_shape = jax.ShapeDtypeStruct((r, d), MXU)
    if y is None:
        h = pl.pallas_call(body, name=name, grid=(r // tm,), in_specs=[row, vec, _mt_spec(d, nct)], out_specs=row,
                           out_shape=h_shape, compiler_params=_cp(("parallel",)))(xp, nw, mtn)
        return xp, h
    return pl.pallas_call(body, name=name, grid=(r // tm,), in_specs=[row, row, _mt_spec(d, nct), vec, _mt_spec(d, nct)],
                          out_specs=(row, row), out_shape=(jax.ShapeDtypeStruct((r, d), F32), h_shape),
                          compiler_params=_cp(("parallel",)))(xp, y, mtg, nw, mtn)


def node_bwd(cfg, dxres, dh, xn, y, mtg, gi, nw, mtn, si, *, name):
    r, d = xn.shape
    tm, nct = _rows(cfg)
    row = pl.BlockSpec((tm, d), lambda i: (i, 0))
    vec = pl.BlockSpec((1, d), lambda i: (0, 0))
    has_y = y is not None

    def body(*refs):
        if has_y:
            dxres_ref, dh_ref, xn_ref, y_ref, mtg_ref, nw_ref, mtn_ref, dxn_ref, dy_ref, dnw_ref, dsh_ref, dsc_ref, dg_ref = refs
        else:
            dxres_ref, dh_ref, xn_ref, nw_ref, mtn_ref, dxn_ref, dnw_ref, dsh_ref, dsc_ref = refs
        i = pl.program_id(0)
        _, vjp = jax.vjp(_norm_mod, xn_ref[...], nw_ref[...], mtn_ref[:, si * d:(si + 1) * d], mtn_ref[:, (si + 1) * d:(si + 2) * d])
        dx, dnw, dsh, dsc = vjp(dh_ref[...])
        dx = dx + dxres_ref[...]
        dxn_ref[...] = dx

        @pl.when(i == 0)
        def _():
            dnw_ref[...] = jnp.zeros_like(dnw_ref)

        @pl.when((i == 0) | (i == nct))
        def _():
            dsh_ref[...] = jnp.zeros_like(dsh_ref)
            dsc_ref[...] = jnp.zeros_like(dsc_ref)
            if has_y:
                dg_ref[...] = jnp.zeros_like(dg_ref)

        dnw_ref[...] += dnw
        dsh_ref[...] += dsh
        dsc_ref[...] += dsc
        if has_y:
            dy_ref[...] = _pat(dx, mtg_ref[:, gi * d:(gi + 1) * d], jnp.multiply).astype(MXU)
            dg_ref[...] += jnp.sum((dx * y_ref[...]).reshape(tm // 8, 8, d), axis=0)

    acc = jax.ShapeDtypeStruct((16, d), F32)
    xs = jax.ShapeDtypeStruct((r, d), F32)
    if has_y:
        return pl.pallas_call(
            body, name=name, grid=(r // tm,), in_specs=[row, row, row, row, _mt_spec(d, nct), vec, _mt_spec(d, nct)],
            out_specs=(row, row, vec, _acc_spec(d, nct), _acc_spec(d, nct), _acc_spec(d, nct)),
            out_shape=(xs, jax.ShapeDtypeStruct((r, d), MXU), jax.ShapeDtypeStruct((1, d), F32), acc, acc, acc),
            compiler_params=_cp(("arbitrary",)))(dxres, dh, xn, y, mtg, nw, mtn)
    dxn, dnw, dsh, dsc = pl.pallas_call(
        body, name=name, grid=(r // tm,), in_specs=[row, row, row, vec, _mt_spec(d, nct)],
        out_specs=(row, vec, _acc_spec(d, nct), _acc_spec(d, nct)),
        out_shape=(xs, jax.ShapeDtypeStruct((1, d), F32), acc, acc), compiler_params=_cp(("arbitrary",)))(dxres, dh, xn, nw, mtn)
    return dxn, None, dnw, dsh, dsc, None


def final_node(cfg, xp, y, mtg, gi, fnw, tgt, *, name):
    r, d = xp.shape
    tm, nct = _rows(cfg)
    row = pl.BlockSpec((tm, d), lambda i: (i, 0))
    vec = pl.BlockSpec((1, d), lambda i: (0, 0))

    def norm(x, w):
        return x * lax.rsqrt(jnp.mean(x * x, axis=-1, keepdims=True) + NORM_EPS) * w

    def body(xp_ref, y_ref, mtg_ref, fnw_ref, tgt_ref, loss_ref, dx_ref, dy_ref, dg_ref, dfnw_ref):
        i = pl.program_id(0)
        g = mtg_ref[:, gi * d:(gi + 1) * d]
        x = xp_ref[...] + _pat(y_ref[...], g, jnp.multiply)
        out, vjp = jax.vjp(norm, x, fnw_ref[...])
        lat = i >= nct
        err = jnp.where(lat, out - tgt_ref[...], 0.0)
        dx, dfnw = vjp(err * (1.0 / d))

        @pl.when(i == 0)
        def _():
            loss_ref[...] = jnp.zeros_like(loss_ref)
            dfnw_ref[...] = jnp.zeros_like(dfnw_ref)

        @pl.when((i == 0) | (i == nct))
        def _():
            dg_ref[...] = jnp.zeros_like(dg_ref)

        loss_ref[...] += jnp.full(loss_ref.shape, 0.5 / d * jnp.sum(err * err), F32)
        dfnw_ref[...] += dfnw
        dx_ref[...] = dx
        dy_ref[...] = _pat(dx, g, jnp.multiply).astype(MXU)
        dg_ref[...] += jnp.sum((dx * y_ref[...]).reshape(tm // 8, 8, d), axis=0)

    return pl.pallas_call(
        body, name=name, grid=(r // tm,),
        in_specs=[row, row, _mt_spec(d, nct), vec, pl.BlockSpec((tm, d), lambda i: (jnp.maximum(i - nct, 0), 0))],
        out_specs=(pl.BlockSpec((8, 128), lambda i: (0, 0)), row, row, _acc_spec(d, nct), vec),
        out_shape=(jax.ShapeDtypeStruct((8, 128), F32), jax.ShapeDtypeStruct((r, d), F32), jax.ShapeDtypeStruct((r, d), MXU),
                   jax.ShapeDtypeStruct((16, d), F32), jax.ShapeDtypeStruct((1, d), F32)),
        compiler_params=_cp(("arbitrary",)))(xp, y, mtg, fnw, tgt)


def _silu(x):
    return x * jax.nn.sigmoid(x)


def mod_fwd(c16, w, b, *, name):
    d, n = w.shape
    tn = _tile(n, 1536)

    def body(c_ref, w_ref, b_ref, o_ref, s_ref):
        s = _silu(c_ref[...])
        s_ref[...] = s.astype(MXU)
        o_ref[...] = _dot(s, w_ref[...]) + b_ref[...]

    return pl.pallas_call(
        body, name=name, grid=(n // tn,),
        in_specs=[pl.BlockSpec((16, d), lambda j: (0, 0)), pl.BlockSpec((d, tn), lambda j: (0, j)), pl.BlockSpec((1, tn), lambda j: (0, j))],
        out_specs=(pl.BlockSpec((16, tn), lambda j: (0, j)), pl.BlockSpec((16, d), lambda j: (0, 0))),
        out_shape=(jax.ShapeDtypeStruct((16, n), F32), jax.ShapeDtypeStruct((16, d), MXU)),
        compiler_params=_cp(("arbitrary",)))(c16, w, b)


def colsum(x, *, name):
    def body(x_ref, o_ref):
        o_ref[...] = jnp.sum(x_ref[...], axis=0, keepdims=True)

    return pl.pallas_call(body, name=name, out_shape=jax.ShapeDtypeStruct((1, x.shape[1]), F32))(x)


def cctx_grad(c16, ds_list, *, name):
    def body(c_ref, *refs):
        o_ref = refs[-1]
        ds = refs[0][...]
        for r_ in refs[1:-1]:
            ds = ds + r_[...]
        _, vjp = jax.vjp(_silu, c_ref[...])
        (dc,) = vjp(ds)
        o_ref[...] = jnp.sum(dc[0:8], axis=0, keepdims=True)

    return pl.pallas_call(body, name=name, out_shape=jax.ShapeDtypeStruct((1, c16.shape[1]), F32))(c16, *ds_list)


def _s5_disc(lam_re, lam_im, log_step, b_re, b_im):
    lr = jnp.minimum(lam_re, S5_LAM_RE_MAX)
    li = lam_im
    dt = jnp.exp(log_step)
    mag = jnp.exp(lr * dt)
    abar_r = mag * jnp.cos(li * dt)
    abar_i = mag * jnp.sin(li * dt)
    den = lr * lr + li * li
    nr = abar_r - 1.0
    coef_r = (nr * lr + abar_i * li) / den
    coef_i = (abar_i * lr - nr * li) / den
    bbar_r = coef_r[:, None, :] * b_re - coef_i[:, None, :] * b_im
    bbar_i = coef_r[:, None, :] * b_im + coef_i[:, None, :] * b_re
    return abar_r, abar_i, bbar_r, bbar_i


def s5_disc_fwd(lam_re, lam_im, log_step, b_re, b_im, *, name):
    def body(lr, li, ls, br, bi, ar_o, ai_o, br_o, bi_o):
        ar_o[...], ai_o[...], br_o[...], bi_o[...] = _s5_disc(lr[...], li[...], ls[...], br[...], bi[...])

    s2, s3 = jax.ShapeDtypeStruct(lam_re.shape, F32), jax.ShapeDtypeStruct(b_re.shape, F32)
    return pl.pallas_call(body, name=name, out_shape=(s2, s2, s3, s3))(lam_re, lam_im, log_step, b_re, b_im)


def s5_disc_bwd(lam_re, lam_im, log_step, b_re, b_im, d_ar, d_ai, d_br, d_bi, *, name):
    def body(lr, li, ls, br, bi, dar, dai, dbr, dbi, o_lr, o_li, o_ls, o_br, o_bi):
        _, vjp = jax.vjp(_s5_disc, lr[...], li[...], ls[...], br[...], bi[...])
        o_lr[...], o_li[...], o_ls[...], o_br[...], o_bi[...] = vjp((dar[...], dai[...], dbr[...], dbi[...]))

    s2, s3 = jax.ShapeDtypeStruct(lam_re.shape, F32), jax.ShapeDtypeStruct(b_re.shape, F32)
    return pl.pallas_call(body, name=name, out_shape=(s2, s2, jax.ShapeDtypeStruct(log_step.shape, F32), s3, s3))(
        lam_re, lam_im, log_step, b_re, b_im, d_ar, d_ai, d_br, d_bi)


S5_LANES = 512


def _chunk_order(k, ncc, nch, rev):
    if not rev:
        return k
    return jnp.where(k < ncc, ncc - 1 - k, nch - 1 - (k - ncc))


def _cmul(ar, ai, xr, xi):
    return ar * xr - ai * xi, ar * xi + ai * xr


S5_FWD_ROWS = 256
S5_BWD_ROWS = 256


def _const_spec(a):
    return pl.BlockSpec(a.shape, lambda k: (0,) * a.ndim, pipeline_mode=pl.Buffered(1))


def _shift_steps(x, edge_tile, back):
    n = x.shape[0]
    row = lax.broadcasted_iota(jnp.int32, (8, x.shape[1]), 0)
    edge = pltpu.roll(edge_tile, 4, 0)
    if back:
        y = pltpu.roll(x, 4, 0)
        return jnp.concatenate([jnp.where(row < 4, edge, y[0:8]), y[8:]], axis=0)
    y = pltpu.roll(x, n - 4, 0)
    return jnp.concatenate([y[:n - 8], jnp.where(row >= 4, edge, y[n - 8:])], axis=0)


def s5_scan_fwd(cfg, u, a2_re, a2_im, bre, bim, abre, abim, cre, cim, *, rev, name):
    r, d = u.shape
    ns = a2_re.shape[1]
    kb = d // S5_KIN
    tcr = S5_FWD_ROWS
    n8 = tcr // 8
    q = S5_FWD_ROWS // S5_BWD_ROWS
    seg = n8 // q
    nch, ncc = r // tcr, cfg["rc"] // tcr
    lw = min(S5_LANES, ns)

    def body(u_ref, ar_ref, ai_ref, bre_ref, bim_ref, abre_ref, abim_ref, cre_ref, cim_ref, sre_ref, sim_ref, ere_ref, eim_ref, y_ref,
             st_re, st_im, u_edge):
        @pl.when(pl.program_id(0) == 0)
        def _():
            st_re[...] = jnp.zeros_like(st_re)
            st_im[...] = jnp.zeros_like(st_im)
            u_edge[...] = jnp.zeros_like(u_edge)

        u_ = u_ref[...]
        ub = u_.astype(MXU)
        upb = _shift_steps(u_, u_edge[...], back=not rev).astype(MXU)
        u_edge[...] = u_[0:8] if rev else u_[tcr - 8:tcr]
        for j in range(kb):
            uj, upj = ub[:, j * S5_KIN:(j + 1) * S5_KIN], upb[:, j * S5_KIN:(j + 1) * S5_KIN]
            sre_ref[:, :, j * S5_KST:(j + 1) * S5_KST] = (_dot(uj, bre_ref[j]) + _dot(upj, abre_ref[j])).reshape(n8, 8, S5_KST)
            sim_ref[:, :, j * S5_KST:(j + 1) * S5_KST] = (_dot(uj, bim_ref[j]) + _dot(upj, abim_ref[j])).reshape(n8, 8, S5_KST)
        for c in range(ns // lw):
            sl = slice(c * lw, (c + 1) * lw)
            ar = jnp.broadcast_to(ar_ref[:, sl], (8, lw))
            ai = jnp.broadcast_to(ai_ref[:, sl], (8, lw))

            def step(i, carry, sl=sl, ar=ar, ai=ai):
                sr, si = carry
                ii = n8 - 1 - i if rev else i
                pr, pi = _cmul(ar, ai, sr, si)
                sr, si = pr + sre_ref[ii, :, sl], pi + sim_ref[ii, :, sl]
                sre_ref[ii, :, sl] = sr
                sim_ref[ii, :, sl] = si
                return sr, si

            sr, si = st_re[:, sl], st_im[:, sl]
            for s_ in range(q):
                at = q - 1 - s_ if rev else s_
                ere_ref[at, :, sl] = sr
                eim_ref[at, :, sl] = si
                sr, si = lax.fori_loop(s_ * seg, (s_ + 1) * seg, step, (sr, si))
            st_re[:, sl] = sr
            st_im[:, sl] = si
        for j in range(kb):
            sr = sre_ref[:, :, j * S5_KST:(j + 1) * S5_KST].reshape(tcr, S5_KST)
            si = sim_ref[:, :, j * S5_KST:(j + 1) * S5_KST].reshape(tcr, S5_KST)
            y_ref[:, j * S5_KIN:(j + 1) * S5_KIN] = _dot(sr, cre_ref[j]) - _dot(si, cim_ref[j])

    cidx = functools.partial(_chunk_order, ncc=ncc, nch=nch, rev=rev)
    full = _const_spec
    st = pl.BlockSpec((n8, 8, ns), lambda k: (cidx(k), 0, 0))
    en = pl.BlockSpec((q, 8, ns), lambda k: (cidx(k), 0, 0))
    return pl.pallas_call(
        body, name=name, grid=(nch,),
        in_specs=[pl.BlockSpec((tcr, d), lambda k: (cidx(k), 0)), full(a2_re), full(a2_im), full(bre), full(bim), full(abre), full(abim),
                  full(cre), full(cim)],
        out_specs=(st, st, en, en, pl.BlockSpec((tcr, d), lambda k: (cidx(k), 0))),
        out_shape=(jax.ShapeDtypeStruct((r // 8, 8, ns), F32),) * 2 + (jax.ShapeDtypeStruct((q * nch, 8, ns), F32),) * 2
        + (jax.ShapeDtypeStruct((r, d), F32),),
        scratch_shapes=[pltpu.VMEM((8, ns), F32), pltpu.VMEM((8, ns), F32), pltpu.VMEM((8, d), F32)],
        compiler_params=_cp(("arbitrary",)))(u, a2_re, a2_im, bre, bim, abre, abim, cre, cim)


def s5_scan_bwd(cfg, dyb, sre, sim, ere, eim, a2_re, a2_im, bre, bim, cre, cim, c2re, c2im, du_in, *, rev, name):
    r, d = dyb.shape
    ns = a2_re.shape[1]
    kb = d // S5_KIN
    tcr = S5_BWD_ROWS
    n8 = tcr // 8
    nch, ncc = r // tcr, cfg["rc"] // tcr
    lw = min(S5_LANES, ns)

    def body(dy_ref, sre_ref, sim_ref, ere_ref, eim_ref, ar_ref, ai_ref, bre_ref, bim_ref, cre_ref, cim_ref, c2re_ref, c2im_ref, duin_ref,
             du_ref, gre_ref, gim_ref, dar_ref, dai_ref, g_re, g_im, gc_re, gc_im, dy_edge):
        k = pl.program_id(0)

        @pl.when(k == 0)
        def _():
            gc_re[...] = jnp.zeros_like(gc_re)
            gc_im[...] = jnp.zeros_like(gc_im)
            dar_ref[...] = jnp.zeros_like(dar_ref)
            dai_ref[...] = jnp.zeros_like(dai_ref)
            dy_edge[...] = jnp.zeros_like(dy_edge)

        dy32 = dy_ref[...].astype(F32)
        dy = dy32.astype(MXU)
        dyn = _shift_steps(dy32, dy_edge[...], back=rev).astype(MXU)
        dy_edge[...] = dy32[tcr - 8:tcr] if rev else dy32[0:8]
        for j in range(kb):
            dyj, dynj = dy[:, j * S5_KIN:(j + 1) * S5_KIN], dyn[:, j * S5_KIN:(j + 1) * S5_KIN]
            g_re[:, :, j * S5_KST:(j + 1) * S5_KST] = (_dot(dyj, cre_ref[j], 1, 1) + _dot(dynj, c2re_ref[j], 1, 1)).reshape(n8, 8, S5_KST)
            g_im[:, :, j * S5_KST:(j + 1) * S5_KST] = -(_dot(dyj, cim_ref[j], 1, 1) + _dot(dynj, c2im_ref[j], 1, 1)).reshape(n8, 8, S5_KST)
        first = lax.broadcasted_iota(jnp.int32, (8, lw), 0) < 4
        if rev:
            first = jnp.logical_not(first)
        for c in range(ns // lw):
            sl = slice(c * lw, (c + 1) * lw)
            ar = jnp.broadcast_to(ar_ref[:, sl], (8, lw))
            nai = -jnp.broadcast_to(ai_ref[:, sl], (8, lw))

            def step(i, carry, sl=sl, ar=ar, nai=nai):
                gr, gi, accr, acci = carry
                ii = i if rev else n8 - 1 - i
                pr, pi = _cmul(ar, nai, gr, gi)
                outr, outi = pr + g_re[ii, :, sl], pi + g_im[ii, :, sl]
                g_re[ii, :, sl] = outr
                g_im[ii, :, sl] = outi
                pv = jnp.clip(ii + 1 if rev else ii - 1, 0, n8 - 1)
                at_entry = (ii == n8 - 1) if rev else (ii == 0)
                pvr = jnp.where(at_entry, ere_ref[0, :, sl], sre_ref[pv, :, sl])
                pvi = jnp.where(at_entry, eim_ref[0, :, sl], sim_ref[pv, :, sl])
                spr = pltpu.roll(jnp.where(first, sre_ref[ii, :, sl], pvr), 4, 0)
                spi = pltpu.roll(jnp.where(first, sim_ref[ii, :, sl], pvi), 4, 0)
                accr = accr + outr * spr + outi * spi
                acci = acci + outi * spr - outr * spi
                return outr, outi, accr, acci

            gr, gi, accr, acci = lax.fori_loop(0, n8, step, (gc_re[:, sl], gc_im[:, sl], dar_ref[:, sl], dai_ref[:, sl]))
            gc_re[:, sl] = gr
            gc_im[:, sl] = gi
            dar_ref[:, sl] = accr
            dai_ref[:, sl] = acci
        for j in range(kb):
            gr = g_re[:, :, j * S5_KST:(j + 1) * S5_KST].reshape(tcr, S5_KST)
            gi = g_im[:, :, j * S5_KST:(j + 1) * S5_KST].reshape(tcr, S5_KST)
            gre_ref[:, j * S5_KST:(j + 1) * S5_KST] = gr.astype(MXU)
            gim_ref[:, j * S5_KST:(j + 1) * S5_KST] = gi.astype(MXU)
            du_ref[:, j * S5_KIN:(j + 1) * S5_KIN] = (duin_ref[:, j * S5_KIN:(j + 1) * S5_KIN]
                                                     + _dot(gr, bre_ref[j], 1, 1) + _dot(gi, bim_ref[j], 1, 1))

    def cidx(k):
        return _chunk_order(nch - 1 - k, ncc, nch, rev)

    full = _const_spec
    st = pl.BlockSpec((n8, 8, ns), lambda k: (cidx(k), 0, 0))
    en = pl.BlockSpec((1, 8, ns), lambda k: (cidx(k), 0, 0))
    rowd = pl.BlockSpec((tcr, d), lambda k: (cidx(k), 0))
    rown = pl.BlockSpec((tcr, ns), lambda k: (cidx(k), 0))
    acc = pl.BlockSpec((8, ns), lambda k: (0, 0))
    return pl.pallas_call(
        body, name=name, grid=(nch,),
        in_specs=[rowd, st, st, en, en, full(a2_re), full(a2_im), full(bre), full(bim), full(cre), full(cim), full(c2re), full(c2im), rowd],
        out_specs=(rowd, rown, rown, acc, acc),
        out_shape=(jax.ShapeDtypeStruct((r, d), F32), jax.ShapeDtypeStruct((r, ns), MXU), jax.ShapeDtypeStruct((r, ns), MXU),
                   jax.ShapeDtypeStruct((8, ns), F32), jax.ShapeDtypeStruct((8, ns), F32)),
        scratch_shapes=[pltpu.VMEM((n8, 8, ns), F32), pltpu.VMEM((n8, 8, ns), F32), pltpu.VMEM((8, ns), F32), pltpu.VMEM((8, ns), F32),
                        pltpu.VMEM((8, d), F32)],
        compiler_params=_cp(("arbitrary",)))(dyb, sre, sim, ere, eim, a2_re, a2_im, bre, bim, cre, cim, c2re, c2im, du_in)


def rowmap(fn, rows_in, vecs_in, outs, accs=(), *, name):
    r = rows_in[0].shape[0]
    tm = _row_tile(r, max(a.shape[1] for a in rows_in))
    nr, nv, no = len(rows_in), len(vecs_in), len(outs)

    def body(*refs):
        ins = [x[...] for x in refs[:nr + nv]]
        res = fn(*ins)
        if not isinstance(res, (tuple, list)):
            res = (res,)
        out_refs = refs[nr + nv:]
        for o_ref, v in zip(out_refs[:no], res[:no]):
            o_ref[...] = v.astype(o_ref.dtype)
        if accs:
            @pl.when(pl.program_id(0) == 0)
            def _():
                for a_ref in out_refs[no:]:
                    a_ref[...] = jnp.zeros_like(a_ref)
            for a_ref, v in zip(out_refs[no:], res[no:]):
                a_ref[...] += v

    in_specs = [pl.BlockSpec((tm, a.shape[1]), lambda i: (i, 0)) for a in rows_in]
    in_specs += [pl.BlockSpec(v.shape, lambda i, n=v.ndim: (0,) * n) for v in vecs_in]
    out_specs = [pl.BlockSpec((tm, w), lambda i: (i, 0)) for w, _ in outs] + [pl.BlockSpec(s, lambda i, n=len(s): (0,) * n) for s in accs]
    out_shape = [jax.ShapeDtypeStruct((r, w), dt) for w, dt in outs] + [jax.ShapeDtypeStruct(s, F32) for s in accs]
    res = pl.pallas_call(body, name=name, grid=(r // tm,), in_specs=in_specs, out_specs=tuple(out_specs), out_shape=tuple(out_shape),
                         compiler_params=_cp(("arbitrary",) if accs else ("parallel",)))(*rows_in, *vecs_in)
    return res


def _gelu(x):
    return jax.nn.gelu(x, approximate=True)


def _hg_lower_bound(e0, e1):
    m = jnp.maximum(e0, e1)
    a, b = jnp.exp(e0 - m), jnp.exp(e1 - m)
    return b / (a + b)


def _hg_gates(x, lb):
    logf = jnp.log(lb + (1.0 - lb) * jax.nn.sigmoid(x))
    return logf, (1.0 - lb) * jax.nn.sigmoid(-x)


def _hg_masks(rev):
    n = CHUNK_ROWS
    rr = lax.broadcasted_iota(jnp.int32, (n, n), 0)
    ss = lax.broadcasted_iota(jnp.int32, (n, n), 1)
    same = (rr % NB) == (ss % NB)
    causal = same & ((ss >= rr) if rev else (ss <= rr))
    anti = same & ((ss <= rr) if rev else (ss >= rr))
    end0 = 0 if rev else n - NB
    pick_end = ss == (end0 + rr % NB)
    return same, causal, anti, pick_end, end0


def _hg_expand(x):
    ex = lax.broadcasted_iota(jnp.int32, x.shape, 0) % NB
    return jnp.concatenate([jnp.where(ex == b, x, 0.0) for b in range(NB)], axis=1)


def _hg_fold(xe):
    kk = xe.shape[1] // NB
    ex = lax.broadcasted_iota(jnp.int32, (xe.shape[0], kk), 0) % NB
    out = jnp.zeros((xe.shape[0], kk), F32)
    for b in range(NB):
        out = out + jnp.where(ex == b, xe[:, b * kk:(b + 1) * kk], 0.0)
    return out


def _hg_chunk(q, v, x, lb, masks):
    same, causal, anti, pick_end, end0 = masks
    logf, kk = _hg_gates(x, lb)
    b = _dot3(causal.astype(MXU), logf)
    bend_t = _dot3(pick_end.astype(MXU), b)
    bend_flat = jnp.concatenate([b[end0 + i:end0 + i + 1] for i in range(NB)], axis=1)
    eb = jnp.exp(b)
    enb = jnp.exp(-b)
    ee = jnp.exp(bend_t - b)
    qd, kd, ke = q * eb, kk * enb, kk * ee
    att = jnp.where(causal, _dot(qd, kd, 1, 1), 0.0)
    decay = jnp.exp(bend_flat)
    return dict(same=same, causal=causal, anti=anti, logf=logf, kk=kk, b=b, eb=eb, enb=enb, ee=ee, qd=qd, kd=kd, ke=ke, att=att,
                decay=decay, qde=_hg_expand(qd), kee=_hg_expand(ke))


def _hg_chunk_order(cfg, r):
    nch, ncc = r // CHUNK_ROWS, cfg["rc"] // CHUNK_ROWS
    return nch, ncc


def hg_scan_fwd(cfg, z, lb, *, d_dir, name):
    r = z.shape[0]
    d = z.shape[1] // N_PROJ
    nh = d // HEAD
    rev = d_dir == 1
    nch, ncc = _hg_chunk_order(cfg, r)
    n = CHUNK_ROWS

    def body(q_ref, v_ref, x_ref, lb_ref, o_ref, sin_ref, stk):
        @pl.when(pl.program_id(0) == 0)
        def _():
            stk[...] = jnp.zeros_like(stk)

        masks = _hg_masks(rev)
        for h in range(nh):
            sl = slice(h * HEAD, (h + 1) * HEAD)
            s0 = stk[h]
            sin_ref[0, h] = s0
            v = v_ref[:, sl]
            c = _hg_chunk(q_ref[:, sl], v, x_ref[:, sl], lb_ref[:, sl], masks)
            o_ref[:, sl] = _dot(c["att"], v) + _dot(c["qde"], s0, 1, 1)
            stk[h] = s0 * c["decay"] + _dot(v, c["kee"], 0, 0)

    def cidx(k):
        return _chunk_order(k, ncc, nch, rev)

    blk = lambda p: pl.BlockSpec((n, d), lambda k: (cidx(k), p))
    return pl.pallas_call(
        body, name=name, grid=(nch,),
        in_specs=[blk(0), blk(1), blk(2 + d_dir), pl.BlockSpec((1, d), lambda k: (0, 0))],
        out_specs=(blk(0), pl.BlockSpec((1, nh, HEAD, NB * HEAD), lambda k: (cidx(k), 0, 0, 0))),
        out_shape=(jax.ShapeDtypeStruct((r, d), F32), jax.ShapeDtypeStruct((nch, nh, HEAD, NB * HEAD), F32)),
        scratch_shapes=[pltpu.VMEM((nh, HEAD, NB * HEAD), F32)], compiler_params=_cp(("arbitrary",)))(z, z, z, lb)


def hg_scan_bwd(cfg, do, z, lb, sin, dq_in, dv_in, *, d_dir, name):
    r = z.shape[0]
    d = z.shape[1] // N_PROJ
    nh = d // HEAD
    rev = d_dir == 1
    nch, ncc = _hg_chunk_order(cfg, r)
    n = CHUNK_ROWS
    has_in = dq_in is not None

    def body(*refs):
        if has_in:
            do_ref, q_ref, v_ref, x_ref, lb_ref, sin_ref, dqi_ref, dvi_ref, dq_ref, dv_ref, dx_ref, dlb_ref, dstk = refs
        else:
            do_ref, q_ref, v_ref, x_ref, lb_ref, sin_ref, dq_ref, dv_ref, dx_ref, dlb_ref, dstk = refs
        @pl.when(pl.program_id(0) == 0)
        def _():
            dstk[...] = jnp.zeros_like(dstk)
            dlb_ref[...] = jnp.zeros_like(dlb_ref)

        masks = _hg_masks(rev)
        ex = lax.broadcasted_iota(jnp.int32, (n, HEAD), 0) % NB
        for h in range(nh):
            sl = slice(h * HEAD, (h + 1) * HEAD)
            do_, q, v, x, lb_, s0, ds1 = do_ref[:, sl], q_ref[:, sl], v_ref[:, sl], x_ref[:, sl], lb_ref[:, sl], sin_ref[0, h], dstk[h]
            c = _hg_chunk(q, v, x, lb_, masks)
            datt = jnp.where(c["causal"], _dot(do_, v, 1, 1), 0.0)
            dv = _dot(c["att"], do_, 0, 0) + _dot(c["kee"], ds1, 1, 1)
            dqd = _dot(datt, c["kd"]) + _hg_fold(_dot(do_, s0))
            dkd = _dot(datt, c["qd"], 0, 0)
            dke = _hg_fold(_dot(v, ds1))
            dbend_flat = jnp.sum(ds1 * s0, axis=0, keepdims=True) * c["decay"]
            dstk[h] = _dot(do_, c["qde"], 0, 0) + ds1 * c["decay"]
            dq = dqd * c["eb"]
            dk = dkd * c["enb"] + dke * c["ee"]
            db = dqd * c["qd"] - dkd * c["kd"] - dke * c["ke"]
            dbend_rows = jnp.zeros((n, HEAD), F32)
            for b in range(NB):
                dbend_rows = dbend_rows + jnp.where(ex == b, dbend_flat[:, b * HEAD:(b + 1) * HEAD], 0.0)
            dlogf = _dot3(c["anti"].astype(MXU), db) + _dot3(c["same"].astype(MXU), dke * c["ke"]) + dbend_rows
            _, vjp = jax.vjp(_hg_gates, x, lb_)
            dx, dlb = vjp((dlogf, dk))
            if has_in:
                dq = dq + dqi_ref[:, sl]
                dv = dv + dvi_ref[:, sl]
            dq_ref[:, sl] = dq
            dv_ref[:, sl] = dv
            dx_ref[:, sl] = dx
            dlb_ref[:, sl] += dlb

    def cidx(k):
        return _chunk_order(nch - 1 - k, ncc, nch, rev)

    blk = lambda p: pl.BlockSpec((n, d), lambda k: (cidx(k), p))
    vec = pl.BlockSpec((1, d), lambda k: (0, 0))
    in_specs = [blk(0), blk(0), blk(1), blk(2 + d_dir), vec, pl.BlockSpec((1, nh, HEAD, NB * HEAD), lambda k: (cidx(k), 0, 0, 0))]
    args = [do, z, z, z, lb, sin]
    if has_in:
        in_specs += [blk(0), blk(0)]
        args += [dq_in, dv_in]
    rd = jax.ShapeDtypeStruct((r, d), F32)
    return pl.pallas_call(
        body, name=name, grid=(nch,), in_specs=in_specs, out_specs=(blk(0), blk(0), blk(0), vec),
        out_shape=(rd, rd, rd, jax.ShapeDtypeStruct((1, d), F32)),
        scratch_shapes=[pltpu.VMEM((nh, HEAD, NB * HEAD), F32)], compiler_params=_cp(("arbitrary",)))(*args)


def _hg_read(o, g, gw):
    on = o * lax.rsqrt(jnp.mean(o * o, axis=-1, keepdims=True) + NORM_EPS) * gw
    return on * jax.nn.sigmoid(g)


def hg_read_fwd(of, ob, z, gw, *, name):
    r, d = of.shape
    nh = d // HEAD
    tm = _row_tile(r)

    def body(of_ref, ob_ref, g_ref, gw_ref, o_ref):
        for h in range(nh):
            sl = slice(h * HEAD, (h + 1) * HEAD)
            o_ref[:, sl] = _hg_read(of_ref[:, sl] + ob_ref[:, sl], g_ref[:, sl], gw_ref[...]).astype(MXU)

    blk = pl.BlockSpec((tm, d), lambda i: (i, 0))
    return pl.pallas_call(
        body, name=name, grid=(r // tm,),
        in_specs=[blk, blk, pl.BlockSpec((tm, d), lambda i: (i, N_PROJ - 1)), pl.BlockSpec((1, HEAD), lambda i: (0, 0))],
        out_specs=blk, out_shape=jax.ShapeDtypeStruct((r, d), MXU), compiler_params=_cp(("parallel",)))(of, ob, z, gw)


def hg_read_bwd(don, of, ob, z, gw, *, name):
    r, d = of.shape
    nh = d // HEAD
    tm = _row_tile(r)

    def body(don_ref, of_ref, ob_ref, g_ref, gw_ref, do_ref, dg_ref, dgw_ref):
        @pl.when(pl.program_id(0) == 0)
        def _():
            dgw_ref[...] = jnp.zeros_like(dgw_ref)

        for h in range(nh):
            sl = slice(h * HEAD, (h + 1) * HEAD)
            _, vjp = jax.vjp(_hg_read, of_ref[:, sl] + ob_ref[:, sl], g_ref[:, sl], gw_ref[...])
            do_ref[:, sl], dg_ref[:, sl], dgw = vjp(don_ref[:, sl])
            dgw_ref[...] += dgw

    blk = pl.BlockSpec((tm, d), lambda i: (i, 0))
    vec = pl.BlockSpec((1, HEAD), lambda i: (0, 0))
    rd = jax.ShapeDtypeStruct((r, d), F32)
    return pl.pallas_call(
        body, name=name, grid=(r // tm,),
        in_specs=[blk, blk, blk, pl.BlockSpec((tm, d), lambda i: (i, N_PROJ - 1)), vec],
        out_specs=(blk, blk, vec), out_shape=(rd, rd, jax.ShapeDtypeStruct((1, HEAD), F32)),
        compiler_params=_cp(("arbitrary",)))(don, of, ob, z, gw)


FFN_COLS = 256


def _seg_masks(cfg, tr, i):
    t = lax.broadcasted_iota(jnp.int32, (tr, FFN_COLS), 0) // NB
    ctx_steps = cfg["rc"] // NB
    pos = jnp.where(i == 0, t % ctx_steps, t % GRID_W)
    last = jnp.where(i == 0, ctx_steps - 1, GRID_W - 1)
    return pos == 0, pos == last


def _prev(x, start):
    return jnp.where(start, 0.0, pltpu.roll(x, NB, 0))


def _next(x, end):
    return jnp.where(end, 0.0, pltpu.roll(x, x.shape[0] - NB, 0))


def _conv3(u, w, b, start, end):
    return ((b + _prev(u, start) * w[0:1]) + u * w[1:2]) + _next(u, end) * w[2:3]


def ffn_mid_fwd(cfg, u, cw, cb, *, name):
    r, f2 = u.shape
    f = f2 // 2
    tr = cfg["rc"]
    nf = f // FFN_COLS

    def body(ua_ref, ug_ref, wa_ref, wg_ref, ba_ref, bg_ref, o_ref, ca_ref, cg_ref):
        start, end = _seg_masks(cfg, tr, pl.program_id(0))
        a = _conv3(ua_ref[...], wa_ref[...], ba_ref[...], start, end)
        g = _conv3(ug_ref[...], wg_ref[...], bg_ref[...], start, end)
        ca_ref[...] = a.astype(MXU)
        cg_ref[...] = g.astype(MXU)
        o_ref[...] = (_silu(a) * g).astype(MXU)

    ca = lambda rows: pl.BlockSpec((rows, FFN_COLS), lambda i, j: (i if rows == tr else 0, j))
    cg = lambda rows: pl.BlockSpec((rows, FFN_COLS), lambda i, j: (i if rows == tr else 0, j + nf))
    half = jax.ShapeDtypeStruct((r, f), MXU)
    return pl.pallas_call(
        body, name=name, grid=(r // tr, nf), in_specs=[ca(tr), cg(tr), ca(3), cg(3), ca(1), cg(1)], out_specs=(ca(tr), ca(tr), ca(tr)),
        out_shape=(jax.ShapeDtypeStruct((r, f), MXU), half, half), compiler_params=_cp(("parallel", "parallel")))(u, u, cw, cw, cb, cb)


def ffn_mid_bwd(cfg, dact, u, ca, cg, cw, *, name):
    r, f2 = u.shape
    f = f2 // 2
    tr = cfg["rc"]
    nf = f // FFN_COLS

    def body(da_ref, us_ref, ca_ref, cg_ref, ws_ref, du_ref, dcw_ref, dcb_ref):
        i = pl.program_id(1)
        is_a = pl.program_id(0) < nf
        start, end = _seg_masks(cfg, tr, i)

        @pl.when(i == 0)
        def _():
            dcw_ref[...] = jnp.zeros_like(dcw_ref)
            dcb_ref[...] = jnp.zeros_like(dcb_ref)

        def finish(dc):
            us, ws = us_ref[...], ws_ref[...]
            dn, dp = _next(dc, end), _prev(dc, start)
            du_ref[...] = (ws[1:2] * dc + ws[0:1] * dn + ws[2:3] * dp).astype(MXU)
            dcw_ref[...] += jnp.concatenate([jnp.sum(dn * us, axis=0, keepdims=True), jnp.sum(dc * us, axis=0, keepdims=True),
                                             jnp.sum(dp * us, axis=0, keepdims=True)], axis=0)
            dcb_ref[...] += jnp.sum(dc, axis=0, keepdims=True)

        @pl.when(is_a)
        def _():
            cs = ca_ref[...].astype(F32)
            sg = jax.nn.sigmoid(cs)
            finish(da_ref[...].astype(F32) * cg_ref[...].astype(F32) * (sg * (1.0 + cs * (1.0 - sg))))

        @pl.when(jnp.logical_not(is_a))
        def _():
            finish(da_ref[...].astype(F32) * _silu(ca_ref[...].astype(F32)))

    cs_ = lambda rows: pl.BlockSpec((rows, FFN_COLS), lambda j, i: (i if rows == tr else 0, j))
    hf = pl.BlockSpec((tr, FFN_COLS), lambda j, i: (i, j % nf))
    gate = pl.BlockSpec((tr, FFN_COLS), lambda j, i: (jnp.where(j < nf, i, 0), jnp.where(j < nf, j, 0)))
    return pl.pallas_call(
        body, name=name, grid=(2 * nf, r // tr), in_specs=[hf, cs_(tr), hf, gate, cs_(3)], out_specs=(cs_(tr), cs_(3), cs_(1)),
        out_shape=(jax.ShapeDtypeStruct((r, f2), MXU), jax.ShapeDtypeStruct((3, f2), F32), jax.ShapeDtypeStruct((1, f2), F32)),
        compiler_params=_cp(("parallel", "arbitrary")))(dact, u, ca, cg, cw)


def hg_lb_fwd(e0, e1, *, name):
    def body(a, b, o):
        o[...] = _hg_lower_bound(a[...], b[...])

    return pl.pallas_call(body, name=name, out_shape=jax.ShapeDtypeStruct(e0.shape, F32))(e0, e1)


def hg_lb_bwd(e0, e1, dlb, *, name):
    def body(a, b, g, oa, ob):
        _, vjp = jax.vjp(_hg_lower_bound, a[...], b[...])
        oa[...], ob[...] = vjp(g[...])

    s = jax.ShapeDtypeStruct(e0.shape, F32)
    return pl.pallas_call(body, name=name, out_shape=(s, s))(e0, e1, dlb)


def _adamw(w, g, m, v):
    m = ADAM_B1 * m + (1.0 - ADAM_B1) * g
    v = ADAM_B2 * v + (1.0 - ADAM_B2) * jnp.square(g)
    m_hat = m / (1.0 - ADAM_B1 ** ADAM_STEP)
    v_hat = v / (1.0 - ADAM_B2 ** ADAM_STEP)
    delta = -ADAM_LR * (m_hat / (jnp.sqrt(v_hat) + ADAM_EPS) + ADAM_WD * w)
    return delta, m, v


def _as2d(a):
    if a.ndim >= 2 and a.shape[-1] % 128 == 0:
        return a.reshape(-1, a.shape[-1])
    return a.reshape(-1, 128) if a.size % 128 == 0 else a.reshape(1, -1)


def adamw(w, g, m, v, *, name):
    w2 = _as2d(w)
    outs = rowmap(_adamw, [w2, _as2d(g), _as2d(m), _as2d(v)], [], [(w2.shape[1], F32)] * 3, name=name)
    return tuple(o.reshape(w.shape) for o in outs)


HBM_SPEC = pl.BlockSpec(memory_space=pltpu.HBM)


def _place():
    mx, my, mc = lax.axis_index("x"), lax.axis_index("y"), lax.axis_index("c")
    others = [(1 - mx, my), (mx, 1 - my), (1 - mx, 1 - my)]
    return mx, my, mc, others


def chip_allgather(x, *, name):
    def body(x_ref, o_ref, send_sems, recv_sems, local_sem):
        mx, my, mc, others = _place()
        me = 2 * mx + my
        mine = pltpu.make_async_copy(x_ref, o_ref.at[me], local_sem)
        mine.start()
        sends = [pltpu.make_async_remote_copy(src_ref=x_ref, dst_ref=o_ref.at[me], send_sem=send_sems.at[j], recv_sem=recv_sems.at[j],
                                              device_id=(px, py, mc), device_id_type=MESH) for j, (px, py) in enumerate(others)]
        for cp in sends:
            cp.start()
        for j, (px, py) in enumerate(others):
            pltpu.make_async_remote_copy(src_ref=x_ref, dst_ref=o_ref.at[2 * px + py], send_sem=send_sems.at[j], recv_sem=recv_sems.at[j],
                                         device_id=(px, py, mc), device_id_type=MESH).wait_recv()
        for cp in sends:
            cp.wait_send()
        mine.wait()

    return pl.pallas_call(
        body, name=name, out_shape=jax.ShapeDtypeStruct((4,) + x.shape, x.dtype), in_specs=[HBM_SPEC], out_specs=HBM_SPEC,
        scratch_shapes=[pltpu.SemaphoreType.DMA((3,)), pltpu.SemaphoreType.DMA((3,)), pltpu.SemaphoreType.DMA])(x)


def _win(ref, axis, start, size):
    idx = [slice(None)] * len(ref.shape)
    idx[axis] = pl.ds(start, size)
    return ref.at[tuple(idx)]


def _half_axis(shape, ax):
    if shape[0] == 2:
        return 0
    return 2 if ax == 1 else 1


def _cut(shape, axis, parts):
    return shape[:axis] + (shape[axis] // parts,) + shape[axis + 1:]


def _hbm_call(body, arrays, out_shapes, sems, name):
    n_in = len(arrays)
    return pl.pallas_call(body, name=name, out_shape=tuple(out_shapes), in_specs=[HBM_SPEC] * n_in, out_specs=tuple([HBM_SPEC] * len(out_shapes)),
                          scratch_shapes=sems)(*arrays)


def place_shard(shard, ax, chip, dtype, *, name):
    l, r, c = shard.shape
    tr = _row_tile(r, c)
    per_block = (l, r // tr, 1)[ax]

    def omap(li, ri, cref):
        idx = [li, ri, 0]
        idx[ax] = idx[ax] + cref[0] * per_block
        return tuple(idx)

    def body(c_ref, s_ref, o_ref):
        o_ref[...] = s_ref[...].astype(dtype)

    full = shard.shape[:ax] + (4 * shard.shape[ax],) + shard.shape[ax + 1:]
    return pl.pallas_call(
        body, name=name, out_shape=jax.ShapeDtypeStruct(full, dtype),
        grid_spec=pltpu.PrefetchScalarGridSpec(
            num_scalar_prefetch=1, grid=(l, r // tr),
            in_specs=[pl.BlockSpec((1, tr, c), lambda li, ri, cref: (li, ri, 0))], out_specs=pl.BlockSpec((1, tr, c), omap)),
        compiler_params=_cp(("parallel", "parallel")))(chip, shard)


def gather_placed(arrays, axes, haxes, *, name):
    n = len(arrays)

    def body(*refs):
        ins, outs = refs[:n], refs[n:2 * n]
        send_sems, recv_sems = refs[2 * n:]
        mx, my, mc, others = _place()
        me = 2 * mx + my

        def part(ref, i, chip):
            sz, hs = arrays[i].shape[axes[i]] // 4, arrays[i].shape[haxes[i]] // 2
            return _win(_win(ref, axes[i], chip * sz, sz), haxes[i], mc * hs, hs)

        sends = []
        for i in range(n):
            for j, (px, py) in enumerate(others):
                rc = pltpu.make_async_remote_copy(src_ref=part(ins[i], i, me), dst_ref=part(outs[i], i, me), send_sem=send_sems.at[i, j],
                                                  recv_sem=recv_sems.at[i, j], device_id=(px, py, mc), device_id_type=MESH)
                rc.start()
                sends.append(rc)
        for i in range(n):
            for j, (px, py) in enumerate(others):
                pltpu.make_async_remote_copy(src_ref=part(ins[i], i, me), dst_ref=part(outs[i], i, 2 * px + py), send_sem=send_sems.at[i, j],
                                             recv_sem=recv_sems.at[i, j], device_id=(px, py, mc), device_id_type=MESH).wait_recv()
        for rc in sends:
            rc.wait_send()

    return pl.pallas_call(
        body, name=name, out_shape=tuple(jax.ShapeDtypeStruct(a_.shape, a_.dtype) for a_ in arrays), in_specs=[HBM_SPEC] * n,
        out_specs=tuple([HBM_SPEC] * n), input_output_aliases={i: i for i in range(n)},
        scratch_shapes=[pltpu.SemaphoreType.DMA((n, 3)), pltpu.SemaphoreType.DMA((n, 3))])(*arrays)


SEM_SPEC = pl.BlockSpec(memory_space=pltpu.SEMAPHORE)
SPLIT_COPY = pltpu.CompilerParams(has_side_effects=pltpu.SideEffectType.DATAFLOW_SIDE_EFFECTING)


def _gather_part(ref, shape, ax, hax, chip, core):
    sz, hs = shape[ax] // 4, shape[hax] // 2
    return _win(_win(ref, ax, chip * sz, sz), hax, core * hs, hs)


def gather_placed_start(arrays, axes, haxes, after, *, name):
    n = len(arrays)

    m = 3 * n

    def body(*refs):
        ins, send_sems, recv_sems = refs[:n], refs[n + 1:n + 1 + m], refs[n + 1 + m:n + 1 + 2 * m]
        token = refs[2 * n + 1 + 2 * m]
        mx, my, mc, others = _place()
        me = 2 * mx + my
        for i in range(n):
            for j, (px, py) in enumerate(others):
                part = _gather_part(ins[i], arrays[i].shape, axes[i], haxes[i], me, mc)
                pltpu.make_async_remote_copy(src_ref=part, dst_ref=part, send_sem=send_sems[3 * i + j], recv_sem=recv_sems[3 * i + j],
                                             device_id=(px, py, mc), device_id_type=MESH).start()
        token[...] = jnp.zeros_like(token)

    hbm = [pltpu.with_memory_space_constraint(a_, pltpu.HBM) for a_ in arrays]
    out = pl.pallas_call(
        body, name=name,
        out_shape=tuple([pltpu.SemaphoreType.DMA(())] * (2 * m)) + tuple(pltpu.HBM(a_.shape, a_.dtype) for a_ in arrays)
        + (jax.ShapeDtypeStruct((8, 128), F32),),
        in_specs=[HBM_SPEC] * n + [pl.BlockSpec(memory_space=pl.ANY)],
        out_specs=tuple([SEM_SPEC] * (2 * m)) + tuple([HBM_SPEC] * n) + (pl.BlockSpec(memory_space=pltpu.VMEM),),
        input_output_aliases={i: 2 * m + i for i in range(n)}, compiler_params=SPLIT_COPY)(*hbm, after)
    return list(out[:m]), list(out[m:2 * m]), list(out[2 * m:2 * m + n]), out[2 * m + n]


def gather_placed_wait(arrays, send_sems, recv_sems, axes, haxes, after, *, name):
    n = len(arrays)

    m = 3 * n

    def body(*refs):
        ins, send_refs, recv_refs = refs[:n], refs[n:n + m], refs[n + m:n + 2 * m]
        mx, my, mc, others = _place()
        me = 2 * mx + my
        for i in range(n):
            for j, (px, py) in enumerate(others):
                cp = pltpu.make_async_remote_copy(
                    src_ref=_gather_part(ins[i], arrays[i].shape, axes[i], haxes[i], me, mc),
                    dst_ref=_gather_part(ins[i], arrays[i].shape, axes[i], haxes[i], 2 * px + py, mc),
                    send_sem=send_refs[3 * i + j], recv_sem=recv_refs[3 * i + j], device_id=(px, py, mc), device_id_type=MESH)
                cp.wait_send()
                cp.wait_recv()

    out = pl.pallas_call(
        body, name=name, out_shape=tuple(pltpu.HBM(a_.shape, a_.dtype) for a_ in arrays),
        in_specs=[HBM_SPEC] * n + [SEM_SPEC] * (2 * m) + [pl.BlockSpec(memory_space=pl.ANY)], out_specs=tuple([HBM_SPEC] * n),
        input_output_aliases={i: i for i in range(n)}, compiler_params=SPLIT_COPY)(*arrays, *send_sems, *recv_sems, after)
    return list(out)


def pair_swap_halves(arrays, haxes, *, name):
    n = len(arrays)

    def body(*refs):
        ins, outs = refs[:n], refs[n:2 * n]
        send_sems, recv_sems = refs[2 * n:]
        mx, my, mc, _ = _place()
        cps = []
        for i in range(n):
            hs = arrays[i].shape[haxes[i]] // 2
            cp = pltpu.make_async_remote_copy(src_ref=_win(ins[i], haxes[i], (1 - mc) * hs, hs), dst_ref=outs[i], send_sem=send_sems.at[i],
                                              recv_sem=recv_sems.at[i], device_id=(mx, my, 1 - mc), device_id_type=MESH)
            cp.start()
            cps.append(cp)
        for cp in cps:
            cp.wait()

    outs = [jax.ShapeDtypeStruct(_cut(a_.shape, h_, 2), a_.dtype) for a_, h_ in zip(arrays, haxes)]
    return _hbm_call(body, arrays, outs, [pltpu.SemaphoreType.DMA((n,)), pltpu.SemaphoreType.DMA((n,))], name)


def add_own_half(g, t, hax, core, *, out_dtype, name):
    l, r, c = t.shape
    tr = _row_tile(r, c)
    per_half = (l, r // tr, 1)[hax]

    def imap(li, ri, cref):
        idx = [li, ri, 0]
        idx[hax] = idx[hax] + cref[0] * per_half
        return tuple(idx)

    def body(c_ref, g_ref, t_ref, o_ref):
        o_ref[...] = (g_ref[...] + t_ref[...]).astype(out_dtype)

    return pl.pallas_call(
        body, name=name, out_shape=jax.ShapeDtypeStruct(t.shape, out_dtype),
        grid_spec=pltpu.PrefetchScalarGridSpec(
            num_scalar_prefetch=1, grid=(l, r // tr),
            in_specs=[pl.BlockSpec((1, tr, c), imap), pl.BlockSpec((1, tr, c), lambda li, ri, cref: (li, ri, 0))],
            out_specs=pl.BlockSpec((1, tr, c), lambda li, ri, cref: (li, ri, 0))),
        compiler_params=_cp(("parallel", "parallel")))(core, g, t)


def exchange_blocks(arrays, axes, *, name):
    n = len(arrays)

    def body(*refs):
        ins, outs = refs[:n], refs[n:2 * n]
        send_sems, recv_sems, local_sems = refs[2 * n:]
        mx, my, mc, others = _place()
        me = 2 * mx + my
        waits = []
        for i in range(n):
            sz = arrays[i].shape[axes[i]] // 4
            cp = pltpu.make_async_copy(_win(ins[i], axes[i], me * sz, sz), outs[i].at[me], local_sems.at[i])
            cp.start()
            waits.append(cp.wait)
            for j, (px, py) in enumerate(others):
                rc = pltpu.make_async_remote_copy(src_ref=_win(ins[i], axes[i], (2 * px + py) * sz, sz), dst_ref=outs[i].at[me],
                                                  send_sem=send_sems.at[i, j], recv_sem=recv_sems.at[i, j], device_id=(px, py, mc),
                                                  device_id_type=MESH)
                rc.start()
                waits.append(rc.wait_send)
        for i in range(n):
            sz = arrays[i].shape[axes[i]] // 4
            for j, (px, py) in enumerate(others):
                pltpu.make_async_remote_copy(src_ref=_win(ins[i], axes[i], me * sz, sz), dst_ref=outs[i].at[2 * px + py],
                                             send_sem=send_sems.at[i, j], recv_sem=recv_sems.at[i, j], device_id=(px, py, mc),
                                             device_id_type=MESH).wait_recv()
        for w_ in waits:
            w_()

    outs = [jax.ShapeDtypeStruct((4,) + _cut(a_.shape, ax, 4), a_.dtype) for a_, ax in zip(arrays, axes)]
    return _hbm_call(body, arrays, outs, [pltpu.SemaphoreType.DMA((n, 3)), pltpu.SemaphoreType.DMA((n, 3)), pltpu.SemaphoreType.DMA((n,))], name)


def exchange_blocks_start(arrays, axes, *, name):
    n = len(arrays)
    lands = [lax.empty((4,) + _cut(a_.shape, ax, 4), a_.dtype) for a_, ax in zip(arrays, axes)]

    def body(*refs):
        ins, lnd = refs[:n], refs[n:2 * n]
        send_sems, recv_sems = refs[2 * n:6 * n], refs[6 * n:9 * n]
        token = refs[11 * n]
        mx, my, mc, others = _place()
        me = 2 * mx + my
        for i in range(n):
            sz = arrays[i].shape[axes[i]] // 4
            pltpu.make_async_copy(_win(ins[i], axes[i], me * sz, sz), lnd[i].at[me], send_sems[4 * i + 3]).start()
            for j, (px, py) in enumerate(others):
                pltpu.make_async_remote_copy(src_ref=_win(ins[i], axes[i], (2 * px + py) * sz, sz), dst_ref=lnd[i].at[me],
                                             send_sem=send_sems[4 * i + j], recv_sem=recv_sems[3 * i + j], device_id=(px, py, mc),
                                             device_id_type=MESH).start()
        token[...] = jnp.zeros_like(token)

    hbm = [pltpu.with_memory_space_constraint(a_, pltpu.HBM) for a_ in arrays + lands]
    out = pl.pallas_call(
        body, name=name,
        out_shape=tuple([pltpu.SemaphoreType.DMA(())] * (7 * n)) + tuple(pltpu.HBM(a_.shape, a_.dtype) for a_ in arrays + lands)
        + (jax.ShapeDtypeStruct((8, 128), F32),),
        in_specs=[HBM_SPEC] * (2 * n),
        out_specs=tuple([SEM_SPEC] * (7 * n)) + tuple([HBM_SPEC] * (2 * n)) + (pl.BlockSpec(memory_space=pltpu.VMEM),),
        input_output_aliases={i: 7 * n + i for i in range(2 * n)}, compiler_params=SPLIT_COPY)(*hbm)
    return list(out[:7 * n]), list(out[7 * n:8 * n]), list(out[8 * n:9 * n]), out[9 * n]


def exchange_blocks_wait(sems, arrays, lands, axes, after, *, name):
    n = len(arrays)

    def body(*refs):
        ins, lnd = refs[:n], refs[n:2 * n]
        send_sems, recv_sems = refs[2 * n:6 * n], refs[6 * n:9 * n]
        mx, my, mc, others = _place()
        me = 2 * mx + my
        for i in range(n):
            sz = arrays[i].shape[axes[i]] // 4
            mine = _win(ins[i], axes[i], me * sz, sz)
            pltpu.make_async_copy(mine, lnd[i].at[me], send_sems[4 * i + 3]).wait()
            for j, (px, py) in enumerate(others):
                cp = pltpu.make_async_remote_copy(src_ref=mine, dst_ref=lnd[i].at[2 * px + py], send_sem=send_sems[4 * i + j],
                                                  recv_sem=recv_sems[3 * i + j], device_id=(px, py, mc), device_id_type=MESH)
                cp.wait_send()
                cp.wait_recv()

    out = pl.pallas_call(
        body, name=name, out_shape=tuple(pltpu.HBM(a_.shape, a_.dtype) for a_ in arrays + lands),
        in_specs=[HBM_SPEC] * (2 * n) + [SEM_SPEC] * (7 * n) + [pl.BlockSpec(memory_space=pl.ANY)] * len(after),
        out_specs=tuple([HBM_SPEC] * (2 * n)), input_output_aliases={i: i for i in range(2 * n)}, compiler_params=SPLIT_COPY)(
            *arrays, *lands, *sems, *after)
    return list(out[n:])


def sum_blocks(e, hax, core, *, name):
    _, l, r, c = e.shape
    tr = _row_tile(r, c)
    per_half = (l, r // tr, 1)[hax]

    def omap(li, ri, cref):
        idx = [li, ri, 0]
        idx[hax] = idx[hax] + cref[0] * per_half
        return tuple(idx)

    def body(c_ref, e_ref, o_ref):
        v = e_ref[...].astype(F32)
        o_ref[...] = ((v[0] + v[1]) + v[2]) + v[3]

    full = (l, r, c)[:hax] + (2 * (l, r, c)[hax],) + (l, r, c)[hax + 1:]
    return pl.pallas_call(
        body, name=name, out_shape=jax.ShapeDtypeStruct(full, F32),
        grid_spec=pltpu.PrefetchScalarGridSpec(
            num_scalar_prefetch=1, grid=(l, r // tr),
            in_specs=[pl.BlockSpec((4, 1, tr, c), lambda li, ri, cref: (0, li, ri, 0))], out_specs=pl.BlockSpec((1, tr, c), omap)),
        compiler_params=_cp(("parallel", "parallel")))(core, e)


def pair_fill_halves(arrays, haxes, *, name):
    n = len(arrays)

    def body(*refs):
        ins, outs = refs[:n], refs[n:2 * n]
        send_sems, recv_sems = refs[2 * n:]
        mx, my, mc, _ = _place()
        cps = []
        for i in range(n):
            hs = arrays[i].shape[haxes[i]] // 2
            mine = _win(ins[i], haxes[i], mc * hs, hs)
            cp = pltpu.make_async_remote_copy(src_ref=mine, dst_ref=_win(outs[i], haxes[i], mc * hs, hs), send_sem=send_sems.at[i],
                                              recv_sem=recv_sems.at[i], device_id=(mx, my, 1 - mc), device_id_type=MESH)
            cp.start()
            cps.append(cp)
        for i in range(n):
            hs = arrays[i].shape[haxes[i]] // 2
            pltpu.make_async_remote_copy(src_ref=_win(ins[i], haxes[i], mc * hs, hs), dst_ref=_win(outs[i], haxes[i], (1 - mc) * hs, hs),
                                         send_sem=send_sems.at[i], recv_sem=recv_sems.at[i], device_id=(mx, my, 1 - mc),
                                         device_id_type=MESH).wait_recv()
        for cp in cps:
            cp.wait_send()

    return pl.pallas_call(
        body, name=name, out_shape=tuple(jax.ShapeDtypeStruct(a_.shape, a_.dtype) for a_ in arrays), in_specs=[HBM_SPEC] * n,
        out_specs=tuple([HBM_SPEC] * n), input_output_aliases={i: i for i in range(n)},
        scratch_shapes=[pltpu.SemaphoreType.DMA((n,)), pltpu.SemaphoreType.DMA((n,))])(*arrays)


WEIGHTS = ['c_ctx', 'w_mod', 'b_mod', 'norm1_w', 'norm2_w', 'final_norm_w', 's5_w_in', 's5_lam_re', 's5_lam_im', 's5_log_step', 's5_b_re', 's5_b_im', 's5_c_re', 's5_c_im', 's5_d', 's5_w_glu', 's5_w_out', 'hg_w_in', 'hg_lower_bounds', 'hg_gnorm_w', 'hg_w_out', 'ffn_w_up', 'ffn_conv_w', 'ffn_conv_b', 'ffn_w_down']
INPUTS = ['x', 'c', 'ctx', 'c_ctx', 'w_mod', 'b_mod', 'norm1_w', 'norm2_w', 'final_norm_w', 's5_w_in', 's5_lam_re', 's5_lam_im', 's5_log_step', 's5_b_re', 's5_b_im', 's5_c_re', 's5_c_im', 's5_d', 's5_w_glu', 's5_w_out', 'hg_w_in', 'hg_lower_bounds', 'hg_gnorm_w', 'hg_w_out', 'ffn_w_up', 'ffn_conv_w', 'ffn_conv_b', 'ffn_w_down', 'loss_target', 'm_c_ctx', 'm_w_mod', 'm_b_mod', 'm_norm1_w', 'm_norm2_w', 'm_final_norm_w', 'm_s5_w_in', 'm_s5_lam_re', 'm_s5_lam_im', 'm_s5_log_step', 'm_s5_b_re', 'm_s5_b_im', 'm_s5_c_re', 'm_s5_c_im', 'm_s5_d', 'm_s5_w_glu', 'm_s5_w_out', 'm_hg_w_in', 'm_hg_lower_bounds', 'm_hg_gnorm_w', 'm_hg_w_out', 'm_ffn_w_up', 'm_ffn_conv_w', 'm_ffn_conv_b', 'm_ffn_w_down', 'v_c_ctx', 'v_w_mod', 'v_b_mod', 'v_norm1_w', 'v_norm2_w', 'v_final_norm_w', 'v_s5_w_in', 'v_s5_lam_re', 'v_s5_lam_im', 'v_s5_log_step', 'v_s5_b_re', 'v_s5_b_im', 'v_s5_c_re', 'v_s5_c_im', 'v_s5_d', 'v_s5_w_glu', 'v_s5_w_out', 'v_hg_w_in', 'v_hg_lower_bounds', 'v_hg_gnorm_w', 'v_hg_w_out', 'v_ffn_w_up', 'v_ffn_conv_w', 'v_ffn_conv_b', 'v_ffn_w_down']
SHARD_AXIS = {"w_mod": 2, "s5_w_in": 1, "s5_w_glu": 1, "s5_w_out": 1, "hg_w_in": 2, "hg_lower_bounds": 2, "hg_w_out": 1,
              "ffn_w_up": 2, "ffn_conv_w": 2, "ffn_w_down": 1}
GATHER_F32 = ("hg_lower_bounds", "ffn_conv_w")
PACK_W = 1024
GRAD_WIRE = jnp.bfloat16


def _reduce_start(items, core, tag):
    names, arrays, axes = [n for n, _, _ in items], [g_ for _, g_, _ in items], [ax for _, _, ax in items]
    haxes = [_half_axis(g_.shape, ax) for g_, ax in zip(arrays, axes)]
    t = pair_swap_halves(arrays, haxes, name="grad_pair_swap_" + tag)
    h = [add_own_half(g_, t_, hx, core, out_dtype=GRAD_WIRE, name="grad_pair_add_" + n) for g_, t_, hx, n in zip(arrays, t, haxes, names)]
    sems, h, lands, token = exchange_blocks_start(h, axes, name="grad_exchange_start_" + tag)
    return (names, sems, h, lands, axes, haxes), token


def _reduce_finish(state, core, after, tag):
    names, sems, h, lands, axes, haxes = state
    e = exchange_blocks_wait(sems, h, lands, axes, list(after), name="grad_exchange_wait_" + tag)
    s = [sum_blocks(e_, hx, core, name="grad_chip_sum_" + n) for e_, hx, n in zip(e, haxes, names)]
    return dict(zip(names, pair_fill_halves(s, haxes, name="grad_pair_fill_" + tag)))


def _pack_small(grads, small):
    flat = jnp.concatenate([grads[n].reshape(-1) for n in small])
    pad = (-flat.shape[0]) % (64 * PACK_W)
    return jnp.pad(flat, (0, pad)).reshape(1, -1, PACK_W)


def _unpack_small(a, block, small):
    sm = chip_allgather(block[0], name="allgather_small_grads").reshape(-1)
    out, off = {}, 0
    for n in small:
        out[n] = sm[off:off + math.prod(a[n].shape)].reshape(a[n].shape)
        off += math.prod(a[n].shape)
    return out


def _blockdiag_b(bb, kb):
    gl = S5_KIN // S5_GROUP
    x = bb.reshape(kb, gl, S5_GROUP, S5_STATE)
    return (x[:, :, :, None, :] * jnp.eye(gl, dtype=bb.dtype)[None, :, None, :, None]).reshape(kb, S5_KIN, S5_KST)


def _blockdiag_c(cc, kb):
    gl = S5_KIN // S5_GROUP
    x = cc.reshape(kb, gl, S5_GROUP, S5_STATE).transpose(0, 1, 3, 2)
    return (x[:, :, :, None, :] * jnp.eye(gl, dtype=cc.dtype)[None, :, None, :, None]).reshape(kb, S5_KST, S5_KIN)


def _diag_b(m, kb):
    gl = S5_KIN // S5_GROUP
    x = m.reshape(kb, gl, S5_GROUP, gl, S5_STATE)
    return jnp.stack([x[:, i, :, i, :] for i in range(gl)], axis=1).reshape(kb * gl, S5_GROUP, S5_STATE)


def _diag_c(m, kb):
    gl = S5_KIN // S5_GROUP
    x = m.reshape(kb, gl, S5_STATE, gl, S5_GROUP)
    return jnp.stack([x[:, i, :, i, :] for i in range(gl)], axis=1).transpose(0, 1, 3, 2).reshape(kb * gl, S5_GROUP, S5_STATE)


def kernel(x, c, ctx, c_ctx, w_mod, b_mod, norm1_w, norm2_w, final_norm_w, s5_w_in, s5_lam_re, s5_lam_im, s5_log_step, s5_b_re, s5_b_im, s5_c_re, s5_c_im, s5_d, s5_w_glu, s5_w_out, hg_w_in, hg_lower_bounds, hg_gnorm_w, hg_w_out, ffn_w_up, ffn_conv_w, ffn_conv_b, ffn_w_down, loss_target, m_c_ctx, m_w_mod, m_b_mod, m_norm1_w, m_norm2_w, m_final_norm_w, m_s5_w_in, m_s5_lam_re, m_s5_lam_im, m_s5_log_step, m_s5_b_re, m_s5_b_im, m_s5_c_re, m_s5_c_im, m_s5_d, m_s5_w_glu, m_s5_w_out, m_hg_w_in, m_hg_lower_bounds, m_hg_gnorm_w, m_hg_w_out, m_ffn_w_up, m_ffn_conv_w, m_ffn_conv_b, m_ffn_w_down, v_c_ctx, v_w_mod, v_b_mod, v_norm1_w, v_norm2_w, v_final_norm_w, v_s5_w_in, v_s5_lam_re, v_s5_lam_im, v_s5_log_step, v_s5_b_re, v_s5_b_im, v_s5_c_re, v_s5_c_im, v_s5_d, v_s5_w_glu, v_s5_w_out, v_hg_w_in, v_hg_lower_bounds, v_hg_gnorm_w, v_hg_w_out, v_ffn_w_up, v_ffn_conv_w, v_ffn_conv_b, v_ffn_w_down):
    a = dict(zip(INPUTS, (x, c, ctx, c_ctx, w_mod, b_mod, norm1_w, norm2_w, final_norm_w, s5_w_in, s5_lam_re, s5_lam_im, s5_log_step, s5_b_re, s5_b_im, s5_c_re, s5_c_im, s5_d, s5_w_glu, s5_w_out, hg_w_in, hg_lower_bounds, hg_gnorm_w, hg_w_out, ffn_w_up, ffn_conv_w, ffn_conv_b, ffn_w_down, loss_target, m_c_ctx, m_w_mod, m_b_mod, m_norm1_w, m_norm2_w, m_final_norm_w, m_s5_w_in, m_s5_lam_re, m_s5_lam_im, m_s5_log_step, m_s5_b_re, m_s5_b_im, m_s5_c_re, m_s5_c_im, m_s5_d, m_s5_w_glu, m_s5_w_out, m_hg_w_in, m_hg_lower_bounds, m_hg_gnorm_w, m_hg_w_out, m_ffn_w_up, m_ffn_conv_w, m_ffn_conv_b, m_ffn_w_down, v_c_ctx, v_w_mod, v_b_mod, v_norm1_w, v_norm2_w, v_final_norm_w, v_s5_w_in, v_s5_lam_re, v_s5_lam_im, v_s5_log_step, v_s5_b_re, v_s5_b_im, v_s5_c_re, v_s5_c_im, v_s5_d, v_s5_w_glu, v_s5_w_out, v_hg_w_in, v_hg_lower_bounds, v_hg_gnorm_w, v_hg_w_out, v_ffn_w_up, v_ffn_conv_w, v_ffn_conv_b, v_ffn_w_down)))
    nb, seq, d = x.shape
    assert nb == NB
    rc = nb * ctx.shape[1]
    cfg = {"rc": rc}
    f = a["ffn_w_down"].shape[1] * 4
    core = lax.axis_index("c").astype(jnp.int32).reshape(1)

    w = {n: a[n] for n in WEIGHTS if n not in SHARD_AXIS}
    chip = (2 * lax.axis_index("x") + lax.axis_index("y")).astype(jnp.int32).reshape(1)
    groups = {
        "now": [("w_mod0", a["w_mod"][0:1]), ("s5_w_in", a["s5_w_in"]), ("hg_lower_bounds", a["hg_lower_bounds"]), ("ffn_conv_w", a["ffn_conv_w"])],
        "mid": [("s5_w_glu", a["s5_w_glu"]), ("s5_w_out", a["s5_w_out"]), ("ffn_w_up0", a["ffn_w_up"][0:1]), ("ffn_w_down0", a["ffn_w_down"][0:1])],
        "later": [("w_mod1", a["w_mod"][1:2]), ("hg_w_in", a["hg_w_in"]), ("hg_w_out", a["hg_w_out"]), ("ffn_w_up1", a["ffn_w_up"][1:2]),
                  ("ffn_w_down1", a["ffn_w_down"][1:2])]}
    shard_axis = lambda n: SHARD_AXIS[n.rstrip("01")]
    placed = {g: [place_shard(s_, shard_axis(n), chip, F32 if n in GATHER_F32 else MXU, name="place_" + n) for n, s_ in it] for g, it in groups.items()}
    axes = {g: [shard_axis(n) for n, _ in it] for g, it in groups.items()}
    haxes = {g: [_half_axis(p_.shape, ax) for p_, ax in zip(placed[g], axes[g])] for g in groups}
    fly_now = gather_placed_start(placed["now"], axes["now"], haxes["now"], chip, name="allgather_now_start")
    fly_mid = gather_placed_start(placed["mid"], axes["mid"], haxes["mid"], fly_now[3], name="allgather_mid_start")
    fly_later = gather_placed_start(placed["later"], axes["later"], haxes["later"], fly_mid[3], name="allgather_later_start")

    def land(fly, g, after):
        send_, recv_, flying, _ = fly
        landed = gather_placed_wait(flying, send_, recv_, axes[g], haxes[g], after, name=f"allgather_{g}_wait")
        w.update(dict(zip([n for n, _ in groups[g]], pair_fill_halves(landed, haxes[g], name=f"allgather_{g}_pair_fill"))))

    tmaj = lambda t: t.transpose(1, 0, 2).reshape(-1, t.shape[-1])
    zero = fly_later[3][0:1, 0:1]
    x0 = jnp.concatenate([tmaj(ctx), tmaj(x)], axis=0)
    tgt = tmaj(a["loss_target"])
    land(fly_now, "now", x0)
    c16 = jnp.concatenate([jnp.broadcast_to(c_ctx[None], (8, d)), c, c], axis=0) + zero
    mt0, scb = mod_fwd(c16, w["w_mod0"][0], w["b_mod"][0][None], name="mod_fwd0")
    mt = [mt0, None]
    n1, n2 = w["norm1_w"], w["norm2_w"]
    w["w_mod"], w["ffn_w_up"], w["ffn_w_down"] = [w["w_mod0"][0], None], [None, None], [None, None]

    def ffn_fwd(l, h):
        u = mm(h, w["ffn_w_up"][l], name=f"ffn_up{l}")
        act, ca, cg = ffn_mid_fwd(cfg, u, w["ffn_conv_w"][l], w["ffn_conv_b"][l][None], name=f"ffn_mid{l}")
        return (u, ca, cg), act, mm(act, w["ffn_w_down"][l], name=f"ffn_down{l}")

    def ffn_bwd(l, dfo, kept, act, h, zero=0.0):
        dact = mm(dfo, w["ffn_w_down"][l], tb=True, out_dtype=MXU, name=f"ffn_down_dx{l}")
        dwd = mm(act, dfo, ta=True, name=f"ffn_down_dw{l}")
        du, dcw, dcb = ffn_mid_bwd(cfg, dact, *kept, w["ffn_conv_w"][l] + zero, name=f"ffn_mid_bwd{l}")
        dh = mm(du, w["ffn_w_up"][l], tb=True, name=f"ffn_up_dx{l}")
        dwu = mm(h, du, ta=True, name=f"ffn_up_dw{l}")
        return dh, dwu, dcw, dcb[0], dwd

    g_, p_ = d // S5_GROUP, S5_STATE
    ns, kb = g_ * p_, d // S5_KIN
    s5p = (w["s5_lam_re"][0].reshape(2 * g_, p_), w["s5_lam_im"][0].reshape(2 * g_, p_), w["s5_log_step"][0].reshape(2 * g_, 1),
           w["s5_b_re"][0].transpose(0, 1, 3, 2).reshape(2 * g_, S5_GROUP, p_), w["s5_b_im"][0].transpose(0, 1, 3, 2).reshape(2 * g_, S5_GROUP, p_))
    ar, ai, bbr, bbi = s5_disc_fwd(*s5p, name="s5_disc")
    dsk = w["s5_d"]
    _, h1 = node_fwd(cfg, x0, None, None, 0, n1[0:1], mt[0], 0, name="node0a")
    u0 = mm(h1, w["s5_w_in"][0], name="s5_in")
    s5s, ys = [], []
    for dd in range(2):
        sl = slice(dd * g_, (dd + 1) * g_)
        a_r, a_i = ar[sl].reshape(1, ns), ai[sl].reshape(1, ns)
        a2 = (a_r * a_r - a_i * a_i, 2.0 * a_r * a_i)
        b_r, b_i = _blockdiag_b(bbr[sl], kb), _blockdiag_b(bbi[sl], kb)
        c_r, c_i = _blockdiag_c(w["s5_c_re"][0, dd], kb), _blockdiag_c(w["s5_c_im"][0, dd], kb)
        ak, ai_k = a_r.reshape(kb, 1, S5_KST), a_i.reshape(kb, 1, S5_KST)
        ab = (ak * b_r - ai_k * b_i, ak * b_i + ai_k * b_r)
        akc, aic = ak.reshape(kb, S5_KST, 1), ai_k.reshape(kb, S5_KST, 1)
        c2 = (akc * c_r - aic * c_i, akc * c_i + aic * c_r)
        bf = lambda t_: t_.astype(MXU)
        sre, sim, ere, eim, y_ = s5_scan_fwd(cfg, u0, a2[0], a2[1], bf(b_r), bf(b_i), bf(ab[0]), bf(ab[1]), bf(c_r), bf(c_i), rev=dd == 1,
                                             name=f"s5_scan{dd}")
        s5s.append((sre, sim, ere, eim, a2[0], a2[1], bf(b_r), bf(b_i), bf(c_r), bf(c_i), bf(c2[0]), bf(c2[1])))
        ys.append(y_)

    def glu_a(u, y0, y1, ds):
        yp = (ds * u + y0) + y1
        return yp, _gelu(yp)

    ypre, zgb = rowmap(glu_a, [u0, ys[0], ys[1]], [dsk], [(d, F32), (d, MXU)], name="s5_glu_a")
    land(fly_mid, "mid", zgb)
    w["ffn_w_up"][0], w["ffn_w_down"][0] = w["ffn_w_up0"][0], w["ffn_w_down0"][0]
    tg = mm(zgb, w["s5_w_glu"][0], name="s5_glu")
    (z2,) = rowmap(lambda yp, t: _gelu(yp) * jax.nn.sigmoid(t), [ypre, tg], [], [(d, MXU)], name="s5_glu_b")
    y1a = mm(z2, w["s5_w_out"][0], name="s5_out")
    x1a, h2a = node_fwd(cfg, x0, y1a, mt[0], 2, n2[0:1], mt[0], 3, name="node0b")
    ufa, acta, foa = ffn_fwd(0, h2a)

    land(fly_later, "later", foa)
    w["w_mod"][1], w["ffn_w_up"][1], w["ffn_w_down"][1] = w["w_mod1"][0], w["ffn_w_up1"][0], w["ffn_w_down1"][0]
    mt[1], _ = mod_fwd(c16, w["w_mod"][1], w["b_mod"][1][None], name="mod_fwd1")
    x2a, h1b = node_fwd(cfg, x1a, foa, mt[0], 5, n1[1:2], mt[1], 0, name="node1a")
    z = mm(h1b, w["hg_w_in"][0], name="hg_in")
    e0, e1 = w["hg_lower_bounds"][:, 0, :], w["hg_lower_bounds"][:, 1, :]
    lb = hg_lb_fwd(e0, e1, name="hg_lb")
    gw = w["hg_gnorm_w"]
    o0, sin0 = hg_scan_fwd(cfg, z, lb[0:1], d_dir=0, name="hg_scan0")
    o1, sin1 = hg_scan_fwd(cfg, z, lb[1:2], d_dir=1, name="hg_scan1")
    onb = hg_read_fwd(o0, o1, z, gw, name="hg_read")
    y1b = mm(onb, w["hg_w_out"][0], name="hg_out")
    x1b, h2b = node_fwd(cfg, x2a, y1b, mt[1], 2, n2[1:2], mt[1], 3, name="node1b")
    ufb, actb, fob = ffn_fwd(1, h2b)
    loss_p, dx2b, dfob, dg2_1, dfnw = final_node(cfg, x1b, fob, mt[1], 5, w["final_norm_w"][None], tgt, name="final_node")

    gr = {}
    dh2b, dwu1, dcw1, dcb1, dwd1 = ffn_bwd(1, dfob, ufb, actb, h2b)
    dx1b, dy1b, dn2_1, dsh2_1, dsc2_1, dg1_1 = node_bwd(cfg, dx2b, dh2b, x1b, y1b, mt[1], 2, n2[1:2], mt[1], 3, name="node1b_bwd")
    don = mm(dy1b, w["hg_w_out"][0], tb=True, name="hg_out_dx")
    gr["hg_w_out"] = mm(onb, dy1b, ta=True, name="hg_out_dw")[None]
    do_, dgate_, dgw = hg_read_bwd(don, o0, o1, z, gw, name="hg_read_bwd")
    dq, dv, dxf, dlb0 = hg_scan_bwd(cfg, do_, z, lb[0:1], sin0, None, None, d_dir=0, name="hg_scan_bwd0")
    dq, dv, dxb, dlb1 = hg_scan_bwd(cfg, do_, z, lb[1:2], sin1, dq, dv, d_dir=1, name="hg_scan_bwd1")
    dz = jnp.concatenate([t_.astype(MXU) for t_ in (dq, dv, dxf, dxb, dgate_)], axis=1)
    dh1b = mm(dz, w["hg_w_in"][0], tb=True, name="hg_in_dx")
    gr["hg_w_in"] = mm(h1b, dz, ta=True, name="hg_in_dw")[None]
    de0, de1 = hg_lb_bwd(e0, e1, jnp.concatenate([dlb0, dlb1], axis=0), name="hg_lb_bwd")
    gr["hg_lower_bounds"] = jnp.stack([de0, de1], axis=1)
    gr["hg_gnorm_w"] = dgw
    dx2a, dfoa, dn1_1, dsh1_1, dsc1_1, dg2_0 = node_bwd(cfg, dx1b, dh1b, x2a, foa, mt[0], 5, n1[1:2], mt[1], 0, name="node1a_bwd")
    dmt1 = jnp.concatenate([dsh1_1, dsc1_1, dg1_1, dsh2_1, dsc2_1, dg2_1], axis=1)
    red1, tok1 = _reduce_start([("hg_w_in", gr["hg_w_in"], 2), ("hg_w_out", gr["hg_w_out"], 1), ("ffn_w_up1", dwu1[None], 2),
                                ("ffn_w_down1", dwd1[None], 1), ("w_mod1", mm(scb, dmt1, ta=True, name="mod_dw1")[None], 2)], core, "layer1")

    dh2a, dwu0, dcw0, dcb0, dwd0 = ffn_bwd(0, dfoa, ufa, acta, h2a, zero=tok1[0:1, 0:1])
    red2, tok2 = _reduce_start([("ffn_w_up0", dwu0[None], 2), ("ffn_w_down0", dwd0[None], 1)], core, "ffn0")
    dx1a, dy1a, dn2_0, dsh2_0, dsc2_0, dg1_0 = node_bwd(cfg, dx2a, dh2a, x1a, y1a, mt[0], 2, n2[0:1] + tok2[0:1, 0:1], mt[0], 3,
                                                        name="node0b_bwd")
    dz2 = mm(dy1a, w["s5_w_out"][0], tb=True, name="s5_out_dx")
    gr["s5_w_out"] = mm(z2, dy1a, ta=True, name="s5_out_dw")[None]

    def glu_b_bwd(dz2_, yp, t):
        zg, sg = _gelu(yp), jax.nn.sigmoid(t)
        return dz2_ * zg * sg * (1.0 - sg), dz2_ * sg

    dtg, dzg_dir = rowmap(glu_b_bwd, [dz2, ypre, tg], [], [(d, MXU), (d, F32)], name="s5_glu_b_bwd")
    dzg_mm = mm(dtg, w["s5_w_glu"][0], tb=True, name="s5_glu_dx")
    gr["s5_w_glu"] = mm(zgb, dtg, ta=True, name="s5_glu_dw")[None]

    def glu_a_bwd(dzd, dzm, yp, u, ds):
        _, vjp = jax.vjp(_gelu, yp)
        (dy,) = vjp(dzd + dzm)
        return dy, dy * ds, jnp.sum(dy * u, axis=0, keepdims=True)

    dyb, du, ddsk = rowmap(glu_a_bwd, [dzg_dir, dzg_mm, ypre, u0], [dsk], [(d, MXU), (d, F32)], [(1, d)], name="s5_glu_a_bwd")
    gr["s5_d"] = ddsk
    dar, dai, dbr, dbi, dcr, dci = [], [], [], [], [], []
    for dd in range(2):
        sre, sim, ere, eim = s5s[dd][:4]
        du, gre, gim, da_r, da_i = s5_scan_bwd(cfg, dyb, *s5s[dd], du, rev=dd == 1, name=f"s5_scan_bwd{dd}")
        dar.append(colsum(da_r, name=f"s5_da_re{dd}").reshape(g_, p_))
        dai.append(colsum(da_i, name=f"s5_da_im{dd}").reshape(g_, p_))
        dbr.append(_diag_b(blockdiag_tn(u0, gre, S5_KIN, S5_KST, name=f"s5_db_re{dd}"), kb))
        dbi.append(_diag_b(blockdiag_tn(u0, gim, S5_KIN, S5_KST, name=f"s5_db_im{dd}"), kb))
        dcr.append(_diag_c(blockdiag_tn(sre.reshape(-1, ns), dyb, S5_KST, S5_KIN, name=f"s5_dc_re{dd}"), kb))
        dci.append(_diag_c(blockdiag_tn(sim.reshape(-1, ns), dyb, S5_KST, S5_KIN, scale=-1.0, name=f"s5_dc_im{dd}"), kb))
    cat = lambda l_: jnp.concatenate(l_, axis=0)
    dlr, dli, dls, dbre, dbim = s5_disc_bwd(*s5p, cat(dar), cat(dai), cat(dbr), cat(dbi), name="s5_disc_bwd")
    gr["s5_lam_re"], gr["s5_lam_im"] = dlr.reshape(1, 2, g_, p_), dli.reshape(1, 2, g_, p_)
    gr["s5_log_step"] = dls.reshape(1, 2, g_)
    gr["s5_b_re"] = dbre.reshape(1, 2, g_, S5_GROUP, p_).transpose(0, 1, 2, 4, 3)
    gr["s5_b_im"] = dbim.reshape(1, 2, g_, S5_GROUP, p_).transpose(0, 1, 2, 4, 3)
    gr["s5_c_re"], gr["s5_c_im"] = jnp.stack(dcr)[None], jnp.stack(dci)[None]
    dh1 = mm(du, w["s5_w_in"][0], tb=True, name="s5_in_dx")
    gr["s5_w_in"] = mm(h1, du, ta=True, name="s5_in_dw")[None]
    dx0, _, dn1_0, dsh1_0, dsc1_0, _ = node_bwd(cfg, dx1a, dh1, x0, None, None, 0, n1[0:1], mt[0], 0, name="node0a_bwd")

    dmt = [jnp.concatenate([dsh1_0, dsc1_0, dg1_0, dsh2_0, dsc2_0, dg2_0], axis=1), dmt1]
    gr["b_mod"] = jnp.concatenate([colsum(dmt[l], name=f"mod_db{l}") for l in range(2)], axis=0)
    dsc16 = [mm(dmt[l], w["w_mod"][l], tb=True, name=f"mod_dx{l}") for l in range(2)]
    gr["c_ctx"] = cctx_grad(c16, dsc16, name="c_ctx_grad")[0]
    gr["norm1_w"] = jnp.concatenate([dn1_0, dn1_1], axis=0)
    gr["norm2_w"] = jnp.concatenate([dn2_0, dn2_1], axis=0)
    gr["final_norm_w"] = dfnw[0]
    gr["ffn_conv_w"], gr["ffn_conv_b"] = jnp.stack([dcw0, dcw1]), jnp.stack([dcb0, dcb1])

    last = [(n, gr[n], SHARD_AXIS[n]) for n in ("s5_w_in", "s5_w_glu", "s5_w_out", "hg_lower_bounds", "ffn_conv_w")]
    last.append(("w_mod0", mm(scb, dmt[0], ta=True, name="mod_dw0")[None], 2))
    small = [n for n in WEIGHTS if n not in SHARD_AXIS]
    last.append(("small", _pack_small(gr, small), 1))
    red3, tok3 = _reduce_start(last, core, "last")
    red = _reduce_finish(red1, core, [tok3], "layer1")
    red.update(_reduce_finish(red2, core, [tok3], "ffn0"))
    red["ffn_w_up"] = jnp.concatenate([red["ffn_w_up0"], red["ffn_w_up1"]], axis=0)
    red["ffn_w_down"] = jnp.concatenate([red["ffn_w_down0"], red["ffn_w_down1"]], axis=0)
    early = ("hg_w_in", "hg_w_out", "ffn_w_up", "ffn_w_down")
    upd = {n: adamw(a[n], red[n], a["m_" + n], a["v_" + n], name="adamw_" + n) for n in early}
    grad_x = dx0[rc:].reshape(seq, nb, d).transpose(1, 0, 2)
    red.update(_reduce_finish(red3, core, [upd[n][0] for n in early] + [grad_x], "last"))
    red.update(_unpack_small(a, red["small"], small))
    red["w_mod"] = jnp.concatenate([red["w_mod0"], red["w_mod1"]], axis=0)
    loss = lax.psum(loss_p[0, 0], ("x", "y", "c"))
    upd.update({n: adamw(a[n], red[n], a["m_" + n], a["v_" + n], name="adamw_" + n) for n in WEIGHTS if n not in early})
    return (loss, grad_x, *[red[n] for n in WEIGHTS], *[upd[n][0] for n in WEIGHTS], *[upd[n][1] for n in WEIGHTS],
            *[upd[n][2] for n in WEIGHTS])
```

```python
import functools
import math

import jax
import jax.numpy as jnp
from jax import lax
from jax.experimental import pallas as pl
from jax.experimental.pallas import tpu as pltpu

F32 = jnp.float32
BF = jnp.bfloat16
MXU = jnp.bfloat16

NORM_EPS = 1e-6
GRID_W = 64
N_MOD = 6
S5_GROUP = 16
S5_STATE = 64
S5_LAM_RE_MAX = -1e-4
S5_KIN = 256
S5_KST = S5_KIN // S5_GROUP * S5_STATE
HEAD = 128
CHUNK_ROWS = 128
N_PROJ = 5
NB = 4
ADAM_LR, ADAM_B1, ADAM_B2, ADAM_EPS, ADAM_WD, ADAM_STEP = 0.001, 0.9, 0.999, 1e-08, 0.01, 10
VMEM_LIMIT = 56 * 1024 * 1024
MESH = pl.DeviceIdType.MESH


def _tile(n, cap):
    if n <= cap:
        return n
    best = None
    for t in range(128, cap + 1, 128):
        if n % t == 0:
            best = t
    assert best is not None, (n, cap)
    return best


def _row_tile(r, width=1024):
    cap = max(8, (512 * 1024) // max(width, 1))
    return next((t for t in (512, 256, 128, 64, 32, 16, 8) if t <= cap and r % t == 0), r)


def _cp(sem):
    return pltpu.CompilerParams(dimension_semantics=sem, vmem_limit_bytes=VMEM_LIMIT)


def _dot(a, b, ca=1, cb=0):
    return lax.dot_general(a.astype(MXU), b.astype(MXU), (((ca,), (cb,)), ((), ())), preferred_element_type=F32)


def _dot3(m, x):
    hi = x.astype(MXU)
    lo = (x - hi.astype(F32)).astype(MXU)
    return _dot(m, hi) + _dot(m, lo)


def mm(a, b, *, ta=False, tb=False, out_dtype=F32, name):
    (kd, m) = a.shape if ta else a.shape[::-1]
    (n, kd2) = b.shape if tb else b.shape[::-1]
    assert kd == kd2, (a.shape, b.shape, ta, tb)
    tm, tn, tk = _tile(m, 1024), _tile(n, 1536), _tile(kd, 1024)
    nk = kd // tk

    def body(a_ref, b_ref, o_ref, acc_ref):
        k = pl.program_id(2)

        @pl.when(k == 0)
        def _():
            acc_ref[...] = jnp.zeros_like(acc_ref)

        acc_ref[...] += _dot(a_ref[...], b_ref[...], 0 if ta else 1, 1 if tb else 0)

        @pl.when(k == nk - 1)
        def _():
            o_ref[...] = acc_ref[...].astype(out_dtype)

    a_spec = pl.BlockSpec((tk, tm), lambda i, j, k: (k, i)) if ta else pl.BlockSpec((tm, tk), lambda i, j, k: (i, k))
    b_spec = pl.BlockSpec((tn, tk), lambda i, j, k: (j, k)) if tb else pl.BlockSpec((tk, tn), lambda i, j, k: (k, j))
    return pl.pallas_call(
        body, name=name, grid=(m // tm, n // tn, nk), in_specs=[a_spec, b_spec],
        out_specs=pl.BlockSpec((tm, tn), lambda i, j, k: (i, j)), out_shape=jax.ShapeDtypeStruct((m, n), out_dtype),
        scratch_shapes=[pltpu.VMEM((tm, tn), F32)], compiler_params=_cp(("parallel", "parallel", "arbitrary")))(a, b)


def mm_cat_nt(parts, b, *, name):
    m, wd = parts[0].shape
    n = b.shape[0]
    np_ = len(parts)
    tm, tn = _tile(m, 1024), _tile(n, 1024)

    def body(*refs):
        b_ref, o_ref, acc_ref = refs[np_], refs[np_ + 1], refs[np_ + 2]
        k = pl.program_id(2)

        @pl.when(k == 0)
        def _():
            acc_ref[...] = jnp.zeros_like(acc_ref)

        for p in range(np_):
            @pl.when(k == p)
            def _(p=p):
                acc_ref[...] += _dot(refs[p][...], b_ref[...], 1, 1)

        @pl.when(k == np_ - 1)
        def _():
            o_ref[...] = acc_ref[...]

    return pl.pallas_call(
        body, name=name, grid=(m // tm, n // tn, np_),
        in_specs=[pl.BlockSpec((tm, wd), lambda i, j, k: (i, 0))] * np_ + [pl.BlockSpec((tn, wd), lambda i, j, k: (j, k))],
        out_specs=pl.BlockSpec((tm, tn), lambda i, j, k: (i, j)), out_shape=jax.ShapeDtypeStruct((m, n), F32),
        scratch_shapes=[pltpu.VMEM((tm, tn), F32)], compiler_params=_cp(("parallel", "parallel", "arbitrary")))(*parts, b)


def mm_tn_cat(a, parts, *, name):
    kd, m = a.shape
    wd = parts[0].shape[1]
    np_ = len(parts)
    tm, tk = _tile(m, 1024), _tile(kd, 1024)
    nk = kd // tk

    def body(*refs):
        a_ref, o_ref, acc_ref = refs[0], refs[np_ + 1], refs[np_ + 2]
        j, k = pl.program_id(1), pl.program_id(2)

        @pl.when(k == 0)
        def _():
            acc_ref[...] = jnp.zeros_like(acc_ref)

        for p in range(np_):
            @pl.when(j == p)
            def _(p=p):
                acc_ref[...] += _dot(a_ref[...], refs[1 + p][...], 0, 0)

        @pl.when(k == nk - 1)
        def _():
            o_ref[...] = acc_ref[...]

    part_spec = lambda p: pl.BlockSpec((tk, wd), lambda i, j, k: (jnp.where(j == p, k, 0), 0))
    return pl.pallas_call(
        body, name=name, grid=(m // tm, np_, nk), in_specs=[pl.BlockSpec((tk, tm), lambda i, j, k: (k, i))] + [part_spec(p) for p in range(np_)],
        out_specs=pl.BlockSpec((tm, wd), lambda i, j, k: (i, j)), out_shape=jax.ShapeDtypeStruct((m, np_ * wd), F32),
        scratch_shapes=[pltpu.VMEM((tm, wd), F32)], compiler_params=_cp(("parallel", "parallel", "arbitrary")))(a, *parts)


def blockdiag_tn(a, b, wa, wb, *, scale=1.0, name):
    rows = a.shape[0]
    kb = a.shape[1] // wa
    tr = _tile(rows, 1024)
    nr = rows // tr

    def body(a_ref, b_ref, o_ref):
        i = pl.program_id(1)

        @pl.when(i == 0)
        def _():
            o_ref[...] = jnp.zeros_like(o_ref)

        o_ref[0] += scale * _dot(a_ref[...], b_ref[...], 0, 0)

    return pl.pallas_call(
        body, name=name, grid=(kb, nr),
        in_specs=[pl.BlockSpec((tr, wa), lambda k, i: (i, k)), pl.BlockSpec((tr, wb), lambda k, i: (i, k))],
        out_specs=pl.BlockSpec((1, wa, wb), lambda k, i: (k, 0, 0)), out_shape=jax.ShapeDtypeStruct((kb, wa, wb), F32),
        compiler_params=_cp(("parallel", "arbitrary")))(a, b)


def _pat(v, p, op):
    tm, d = v.shape
    return op(v.reshape(tm // 8, 8, d), p[None]).reshape(tm, d)


def _norm_mod(x, nw, shift, scale):
    y = x * lax.rsqrt(jnp.mean(x * x, axis=-1, keepdims=True) + NORM_EPS) * nw
    return _pat(_pat(y, 1.0 + scale, jnp.multiply), shift, jnp.add)


def _mt_spec(d, nct):
    return pl.BlockSpec((8, N_MOD * d), lambda i: (jnp.where(i < nct, 0, 1), 0))


def _acc_spec(d, nct):
    return pl.BlockSpec((8, d), lambda i: (jnp.where(i < nct, 0, 1), 0))


def _rows(cfg):
    tm = min(512, cfg["rc"])
    return tm, cfg["rc"] // tm


def node_fwd(cfg, xp, y, mtg, gi, nw, mtn, si, *, name):
    r, d = xp.shape
    tm, nct = _rows(cfg)
    row = pl.BlockSpec((tm, d), lambda i: (i, 0))
    vec = pl.BlockSpec((1, d), lambda i: (0, 0))

    def body(*refs):
        if y is None:
            xp_ref, nw_ref, mtn_ref, h_ref = refs
            x = xp_ref[...]
        else:
            xp_ref, y_ref, mtg_ref, nw_ref, mtn_ref, xn_ref, h_ref = refs
            x = xp_ref[...] + _pat(y_ref[...], mtg_ref[:, gi * d:(gi + 1) * d], jnp.multiply)
            xn_ref[...] = x
        h_ref[...] = _norm_mod(x, nw_ref[...], mtn_ref[:, si * d:(si + 1) * d], mtn_ref[:, (si + 1) * d:(si + 2) * d]).astype(MXU)

    h_shape = jax.ShapeDtypeStruct((r, d), MXU)
    if y is None:
        h = pl.pallas_call(body, name=name, grid=(r // tm,), in_specs=[row, vec, _mt_spec(d, nct)], out_specs=row,
                           out_shape=h_shape, compiler_params=_cp(("parallel",)))(xp, nw, mtn)
        return xp, h
    return pl.pallas_call(body, name=name, grid=(r // tm,), in_specs=[row, row, _mt_spec(d, nct), vec, _mt_spec(d, nct)],
                          out_specs=(row, row), out_shape=(jax.ShapeDtypeStruct((r, d), F32), h_shape),
                          compiler_params=_cp(("parallel",)))(xp, y, mtg, nw, mtn)


def node_bwd(cfg, dxres, dh, xn, y, mtg, gi, nw, mtn, si, *, name):
    r, d = xn.shape
    tm, nct = _rows(cfg)
    row = pl.BlockSpec((tm, d), lambda i: (i, 0))
    vec = pl.BlockSpec((1, d), lambda i: (0, 0))
    has_y = y is not None

    def body(*refs):
        if has_y:
            dxres_ref, dh_ref, xn_ref, y_ref, mtg_ref, nw_ref, mtn_ref, dxn_ref, dy_ref, dnw_ref, dsh_ref, dsc_ref, dg_ref = refs
        else:
            dxres_ref, dh_ref, xn_ref, nw_ref, mtn_ref, dxn_ref, dnw_ref, dsh_ref, dsc_ref = refs
        i = pl.program_id(0)
        _, vjp = jax.vjp(_norm_mod, xn_ref[...], nw_ref[...], mtn_ref[:, si * d:(si + 1) * d], mtn_ref[:, (si + 1) * d:(si + 2) * d])
        dx, dnw, dsh, dsc = vjp(dh_ref[...])
        dx = dx + dxres_ref[...]
        dxn_ref[...] = dx

        @pl.when(i == 0)
        def _():
            dnw_ref[...] = jnp.zeros_like(dnw_ref)

        @pl.when((i == 0) | (i == nct))
        def _():
            dsh_ref[...] = jnp.zeros_like(dsh_ref)
            dsc_ref[...] = jnp.zeros_like(dsc_ref)
            if has_y:
                dg_ref[...] = jnp.zeros_like(dg_ref)

        dnw_ref[...] += dnw
        dsh_ref[...] += dsh
        dsc_ref[...] += dsc
        if has_y:
            dy_ref[...] = _pat(dx, mtg_ref[:, gi * d:(gi + 1) * d], jnp.multiply).astype(MXU)
            dg_ref[...] += jnp.sum((dx * y_ref[...]).reshape(tm // 8, 8, d), axis=0)

    acc = jax.ShapeDtypeStruct((16, d), F32)
    xs = jax.ShapeDtypeStruct((r, d), F32)
    if has_y:
        return pl.pallas_call(
            body, name=name, grid=(r // tm,), in_specs=[row, row, row, row, _mt_spec(d, nct), vec, _mt_spec(d, nct)],
            out_specs=(row, row, vec, _acc_spec(d, nct), _acc_spec(d, nct), _acc_spec(d, nct)),
            out_shape=(xs, jax.ShapeDtypeStruct((r, d), MXU), jax.ShapeDtypeStruct((1, d), F32), acc, acc, acc),
            compiler_params=_cp(("arbitrary",)))(dxres, dh, xn, y, mtg, nw, mtn)
    dxn, dnw, dsh, dsc = pl.pallas_call(
        body, name=name, grid=(r // tm,), in_specs=[row, row, row, vec, _mt_spec(d, nct)],
        out_specs=(row, vec, _acc_spec(d, nct), _acc_spec(d, nct)),
        out_shape=(xs, jax.ShapeDtypeStruct((1, d), F32), acc, acc), compiler_params=_cp(("arbitrary",)))(dxres, dh, xn, nw, mtn)
    return dxn, None, dnw, dsh, dsc, None


def final_node(cfg, xp, y, mtg, gi, fnw, tgt, *, name):
    r, d = xp.shape
    tm, nct = _rows(cfg)
    row = pl.BlockSpec((tm, d), lambda i: (i, 0))
    vec = pl.BlockSpec((1, d), lambda i: (0, 0))

    def norm(x, w):
        return x * lax.rsqrt(jnp.mean(x * x, axis=-1, keepdims=True) + NORM_EPS) * w

    def body(xp_ref, y_ref, mtg_ref, fnw_ref, tgt_ref, loss_ref, dx_ref, dy_ref, dg_ref, dfnw_ref):
        i = pl.program_id(0)
        g = mtg_ref[:, gi * d:(gi + 1) * d]
        x = xp_ref[...] + _pat(y_ref[...], g, jnp.multiply)
        out, vjp = jax.vjp(norm, x, fnw_ref[...])
        lat = i >= nct
        err = jnp.where(lat, out - tgt_ref[...], 0.0)
        dx, dfnw = vjp(err * (1.0 / d))

        @pl.when(i == 0)
        def _():
            loss_ref[...] = jnp.zeros_like(loss_ref)
            dfnw_ref[...] = jnp.zeros_like(dfnw_ref)

        @pl.when((i == 0) | (i == nct))
        def _():
            dg_ref[...] = jnp.zeros_like(dg_ref)

        loss_ref[...] += jnp.full(loss_ref.shape, 0.5 / d * jnp.sum(err * err), F32)
        dfnw_ref[...] += dfnw
        dx_ref[...] = dx
        dy_ref[...] = _pat(dx, g, jnp.multiply).astype(MXU)
        dg_ref[...] += jnp.sum((dx * y_ref[...]).reshape(tm // 8, 8, d), axis=0)

    return pl.pallas_call(
        body, name=name, grid=(r // tm,),
        in_specs=[row, row, _mt_spec(d, nct), vec, pl.BlockSpec((tm, d), lambda i: (jnp.maximum(i - nct, 0), 0))],
        out_specs=(pl.BlockSpec((8, 128), lambda i: (0, 0)), row, row, _acc_spec(d, nct), vec),
        out_shape=(jax.ShapeDtypeStruct((8, 128), F32), jax.ShapeDtypeStruct((r, d), F32), jax.ShapeDtypeStruct((r, d), MXU),
                   jax.ShapeDtypeStruct((16, d), F32), jax.ShapeDtypeStruct((1, d), F32)),
        compiler_params=_cp(("arbitrary",)))(xp, y, mtg, fnw, tgt)


def _silu(x):
    return x * jax.nn.sigmoid(x)


def mod_fwd(c16, w, b, *, name):
    d, n = w.shape
    tn = _tile(n, 1536)

    def body(c_ref, w_ref, b_ref, o_ref, s_ref):
        s = _silu(c_ref[...])
        s_ref[...] = s.astype(MXU)
        o_ref[...] = _dot(s, w_ref[...]) + b_ref[...]

    return pl.pallas_call(
        body, name=name, grid=(n // tn,),
        in_specs=[pl.BlockSpec((16, d), lambda j: (0, 0)), pl.BlockSpec((d, tn), lambda j: (0, j)), pl.BlockSpec((1, tn), lambda j: (0, j))],
        out_specs=(pl.BlockSpec((16, tn), lambda j: (0, j)), pl.BlockSpec((16, d), lambda j: (0, 0))),
        out_shape=(jax.ShapeDtypeStruct((16, n), F32), jax.ShapeDtypeStruct((16, d), MXU)),
        compiler_params=_cp(("arbitrary",)))(c16, w, b)


def colsum(x, *, name):
    def body(x_ref, o_ref):
        o_ref[...] = jnp.sum(x_ref[...], axis=0, keepdims=True)

    return pl.pallas_call(body, name=name, out_shape=jax.ShapeDtypeStruct((1, x.shape[1]), F32))(x)


def cctx_grad(c16, ds_list, *, name):
    def body(c_ref, *refs):
        o_ref = refs[-1]
        ds = refs[0][...]
        for r_ in refs[1:-1]:
            ds = ds + r_[...]
        _, vjp = jax.vjp(_silu, c_ref[...])
        (dc,) = vjp(ds)
        o_ref[...] = jnp.sum(dc[0:8], axis=0, keepdims=True)

    return pl.pallas_call(body, name=name, out_shape=jax.ShapeDtypeStruct((1, c16.shape[1]), F32))(c16, *ds_list)


def _s5_disc(lam_re, lam_im, log_step, b_re, b_im):
    lr = jnp.minimum(lam_re, S5_LAM_RE_MAX)
    li = lam_im
    dt = jnp.exp(log_step)
    mag = jnp.exp(lr * dt)
    abar_r = mag * jnp.cos(li * dt)
    abar_i = mag * jnp.sin(li * dt)
    den = lr * lr + li * li
    nr = abar_r - 1.0
    coef_r = (nr * lr + abar_i * li) / den
    coef_i = (abar_i * lr - nr * li) / den
    bbar_r = coef_r[:, None, :] * b_re - coef_i[:, None, :] * b_im
    bbar_i = coef_r[:, None, :] * b_im + coef_i[:, None, :] * b_re
    return abar_r, abar_i, bbar_r, bbar_i


def s5_disc_fwd(lam_re, lam_im, log_step, b_re, b_im, *, name):
    def body(lr, li, ls, br, bi, ar_o, ai_o, br_o, bi_o):
        ar_o[...], ai_o[...], br_o[...], bi_o[...] = _s5_disc(lr[...], li[...], ls[...], br[...], bi[...])

    s2, s3 = jax.ShapeDtypeStruct(lam_re.shape, F32), jax.ShapeDtypeStruct(b_re.shape, F32)
    return pl.pallas_call(body, name=name, out_shape=(s2, s2, s3, s3))(lam_re, lam_im, log_step, b_re, b_im)


def s5_disc_bwd(lam_re, lam_im, log_step, b_re, b_im, d_ar, d_ai, d_br, d_bi, *, name):
    def body(lr, li, ls, br, bi, dar, dai, dbr, dbi, o_lr, o_li, o_ls, o_br, o_bi):
        _, vjp = jax.vjp(_s5_disc, lr[...], li[...], ls[...], br[...], bi[...])
        o_lr[...], o_li[...], o_ls[...], o_br[...], o_bi[...] = vjp((dar[...], dai[...], dbr[...], dbi[...]))

    s2, s3 = jax.ShapeDtypeStruct(lam_re.shape, F32), jax.ShapeDtypeStruct(b_re.shape, F32)
    return pl.pallas_call(body, name=name, out_shape=(s2, s2, jax.ShapeDtypeStruct(log_step.shape, F32), s3, s3))(
        lam_re, lam_im, log_step, b_re, b_im, d_ar, d_ai, d_br, d_bi)


S5_LANES = 512


def _chunk_order(k, ncc, nch, rev):
    if not rev:
        return k
    return jnp.where(k < ncc, ncc - 1 - k, nch - 1 - (k - ncc))


def _cmul(ar, ai, xr, xi):
    return ar * xr - ai * xi, ar * xi + ai * xr


S5_FWD_ROWS = 256
S5_BWD_ROWS = 256


def _const_spec(a):
    return pl.BlockSpec(a.shape, lambda k: (0,) * a.ndim, pipeline_mode=pl.Buffered(1))


def _shift_steps(x, edge_tile, back):
    n = x.shape[0]
    row = lax.broadcasted_iota(jnp.int32, (8, x.shape[1]), 0)
    edge = pltpu.roll(edge_tile, 4, 0)
    if back:
        y = pltpu.roll(x, 4, 0)
        return jnp.concatenate([jnp.where(row < 4, edge, y[0:8]), y[8:]], axis=0)
    y = pltpu.roll(x, n - 4, 0)
    return jnp.concatenate([y[:n - 8], jnp.where(row >= 4, edge, y[n - 8:])], axis=0)


def s5_scan_fwd(cfg, u, a2_re, a2_im, bre, bim, abre, abim, cre, cim, *, rev, name):
    r, d = u.shape
    ns = a2_re.shape[1]
    kb = d // S5_KIN
    tcr = S5_FWD_ROWS
    n8 = tcr // 8
    q = S5_FWD_ROWS // S5_BWD_ROWS
    seg = n8 // q
    nch, ncc = r // tcr, cfg["rc"] // tcr
    lw = min(S5_LANES, ns)

    def body(u_ref, ar_ref, ai_ref, bre_ref, bim_ref, abre_ref, abim_ref, cre_ref, cim_ref, sre_ref, sim_ref, ere_ref, eim_ref, y_ref,
             st_re, st_im, u_edge):
        @pl.when(pl.program_id(0) == 0)
        def _():
            st_re[...] = jnp.zeros_like(st_re)
            st_im[...] = jnp.zeros_like(st_im)
            u_edge[...] = jnp.zeros_like(u_edge)

        u_ = u_ref[...]
        ub = u_.astype(MXU)
        upb = _shift_steps(u_, u_edge[...], back=not rev).astype(MXU)
        u_edge[...] = u_[0:8] if rev else u_[tcr - 8:tcr]
        for j in range(kb):
            uj, upj = ub[:, j * S5_KIN:(j + 1) * S5_KIN], upb[:, j * S5_KIN:(j + 1) * S5_KIN]
            sre_ref[:, :, j * S5_KST:(j + 1) * S5_KST] = (_dot(uj, bre_ref[j]) + _dot(upj, abre_ref[j])).reshape(n8, 8, S5_KST)
            sim_ref[:, :, j * S5_KST:(j + 1) * S5_KST] = (_dot(uj, bim_ref[j]) + _dot(upj, abim_ref[j])).reshape(n8, 8, S5_KST)
        for c in range(ns // lw):
            sl = slice(c * lw, (c + 1) * lw)
            ar = jnp.broadcast_to(ar_ref[:, sl], (8, lw))
            ai = jnp.broadcast_to(ai_ref[:, sl], (8, lw))

            def step(i, carry, sl=sl, ar=ar, ai=ai):
                sr, si = carry
                ii = n8 - 1 - i if rev else i
                pr, pi = _cmul(ar, ai, sr, si)
                sr, si = pr + sre_ref[ii, :, sl], pi + sim_ref[ii, :, sl]
                sre_ref[ii, :, sl] = sr
                sim_ref[ii, :, sl] = si
                return sr, si

            sr, si = st_re[:, sl], st_im[:, sl]
            for s_ in range(q):
                at = q - 1 - s_ if rev else s_
                ere_ref[at, :, sl] = sr
                eim_ref[at, :, sl] = si
                sr, si = lax.fori_loop(s_ * seg, (s_ + 1) * seg, step, (sr, si))
            st_re[:, sl] = sr
            st_im[:, sl] = si
        for j in range(kb):
            sr = sre_ref[:, :, j * S5_KST:(j + 1) * S5_KST].reshape(tcr, S5_KST)
            si = sim_ref[:, :, j * S5_KST:(j + 1) * S5_KST].reshape(tcr, S5_KST)
            y_ref[:, j * S5_KIN:(j + 1) * S5_KIN] = _dot(sr, cre_ref[j]) - _dot(si, cim_ref[j])

    cidx = functools.partial(_chunk_order, ncc=ncc, nch=nch, rev=rev)
    full = _const_spec
    st = pl.BlockSpec((n8, 8, ns), lambda k: (cidx(k), 0, 0))
    en = pl.BlockSpec((q, 8, ns), lambda k: (cidx(k), 0, 0))
    return pl.pallas_call(
        body, name=name, grid=(nch,),
        in_specs=[pl.BlockSpec((tcr, d), lambda k: (cidx(k), 0)), full(a2_re), full(a2_im), full(bre), full(bim), full(abre), full(abim),
                  full(cre), full(cim)],
        out_specs=(st, st, en, en, pl.BlockSpec((tcr, d), lambda k: (cidx(k), 0))),
        out_shape=(jax.ShapeDtypeStruct((r // 8, 8, ns), F32),) * 2 + (jax.ShapeDtypeStruct((q * nch, 8, ns), F32),) * 2
        + (jax.ShapeDtypeStruct((r, d), F32),),
        scratch_shapes=[pltpu.VMEM((8, ns), F32), pltpu.VMEM((8, ns), F32), pltpu.VMEM((8, d), F32)],
        compiler_params=_cp(("arbitrary",)))(u, a2_re, a2_im, bre, bim, abre, abim, cre, cim)


def s5_scan_bwd(cfg, dyb, sre, sim, ere, eim, a2_re, a2_im, bre, bim, cre, cim, c2re, c2im, du_in, *, rev, name):
    r, d = dyb.shape
    ns = a2_re.shape[1]
    kb = d // S5_KIN
    tcr = S5_BWD_ROWS
    n8 = tcr // 8
    nch, ncc = r // tcr, cfg["rc"] // tcr
    lw = min(S5_LANES, ns)

    def body(dy_ref, sre_ref, sim_ref, ere_ref, eim_ref, ar_ref, ai_ref, bre_ref, bim_ref, cre_ref, cim_ref, c2re_ref, c2im_ref, duin_ref,
             du_ref, gre_ref, gim_ref, dar_ref, dai_ref, g_re, g_im, gc_re, gc_im, dy_edge):
        k = pl.program_id(0)

        @pl.when(k == 0)
        def _():
            gc_re[...] = jnp.zeros_like(gc_re)
            gc_im[...] = jnp.zeros_like(gc_im)
            dar_ref[...] = jnp.zeros_like(dar_ref)
            dai_ref[...] = jnp.zeros_like(dai_ref)
            dy_edge[...] = jnp.zeros_like(dy_edge)

        dy32 = dy_ref[...].astype(F32)
        dy = dy32.astype(MXU)
        dyn = _shift_steps(dy32, dy_edge[...], back=rev).astype(MXU)
        dy_edge[...] = dy32[tcr - 8:tcr] if rev else dy32[0:8]
        for j in range(kb):
            dyj, dynj = dy[:, j * S5_KIN:(j + 1) * S5_KIN], dyn[:, j * S5_KIN:(j + 1) * S5_KIN]
            g_re[:, :, j * S5_KST:(j + 1) * S5_KST] = (_dot(dyj, cre_ref[j], 1, 1) + _dot(dynj, c2re_ref[j], 1, 1)).reshape(n8, 8, S5_KST)
            g_im[:, :, j * S5_KST:(j + 1) * S5_KST] = -(_dot(dyj, cim_ref[j], 1, 1) + _dot(dynj, c2im_ref[j], 1, 1)).reshape(n8, 8, S5_KST)
        first = lax.broadcasted_iota(jnp.int32, (8, lw), 0) < 4
        if rev:
            first = jnp.logical_not(first)
        for c in range(ns // lw):
            sl = slice(c * lw, (c + 1) * lw)
            ar = jnp.broadcast_to(ar_ref[:, sl], (8, lw))
            nai = -jnp.broadcast_to(ai_ref[:, sl], (8, lw))

            def step(i, carry, sl=sl, ar=ar, nai=nai):
                gr, gi, accr, acci = carry
                ii = i if rev else n8 - 1 - i
                pr, pi = _cmul(ar, nai, gr, gi)
                outr, outi = pr + g_re[ii, :, sl], pi + g_im[ii, :, sl]
                g_re[ii, :, sl] = outr
                g_im[ii, :, sl] = outi
                pv = jnp.clip(ii + 1 if rev else ii - 1, 0, n8 - 1)
                at_entry = (ii == n8 - 1) if rev else (ii == 0)
                pvr = jnp.where(at_entry, ere_ref[0, :, sl], sre_ref[pv, :, sl])
                pvi = jnp.where(at_entry, eim_ref[0, :, sl], sim_ref[pv, :, sl])
                spr = pltpu.roll(jnp.where(first, sre_ref[ii, :, sl], pvr), 4, 0)
                spi = pltpu.roll(jnp.where(first, sim_ref[ii, :, sl], pvi), 4, 0)
                accr = accr + outr * spr + outi * spi
                acci = acci + outi * spr - outr * spi
                return outr, outi, accr, acci

            gr, gi, accr, acci = lax.fori_loop(0, n8, step, (gc_re[:, sl], gc_im[:, sl], dar_ref[:, sl], dai_ref[:, sl]))
            gc_re[:, sl] = gr
            gc_im[:, sl] = gi
            dar_ref[:, sl] = accr
            dai_ref[:, sl] = acci
        for j in range(kb):
            gr = g_re[:, :, j * S5_KST:(j + 1) * S5_KST].reshape(tcr, S5_KST)
            gi = g_im[:, :, j * S5_KST:(j + 1) * S5_KST].reshape(tcr, S5_KST)
            gre_ref[:, j * S5_KST:(j + 1) * S5_KST] = gr.astype(MXU)
            gim_ref[:, j * S5_KST:(j + 1) * S5_KST] = gi.astype(MXU)
            du_ref[:, j * S5_KIN:(j + 1) * S5_KIN] = (duin_ref[:, j * S5_KIN:(j + 1) * S5_KIN]
                                                     + _dot(gr, bre_ref[j], 1, 1) + _dot(gi, bim_ref[j], 1, 1))

    def cidx(k):
        return _chunk_order(nch - 1 - k, ncc, nch, rev)

    full = _const_spec
    st = pl.BlockSpec((n8, 8, ns), lambda k: (cidx(k), 0, 0))
    en = pl.BlockSpec((1, 8, ns), lambda k: (cidx(k), 0, 0))
    rowd = pl.BlockSpec((tcr, d), lambda k: (cidx(k), 0))
    rown = pl.BlockSpec((tcr, ns), lambda k: (cidx(k), 0))
    acc = pl.BlockSpec((8, ns), lambda k: (0, 0))
    return pl.pallas_call(
        body, name=name, grid=(nch,),
        in_specs=[rowd, st, st, en, en, full(a2_re), full(a2_im), full(bre), full(bim), full(cre), full(cim), full(c2re), full(c2im), rowd],
        out_specs=(rowd, rown, rown, acc, acc),
        out_shape=(jax.ShapeDtypeStruct((r, d), F32), jax.ShapeDtypeStruct((r, ns), MXU), jax.ShapeDtypeStruct((r, ns), MXU),
                   jax.ShapeDtypeStruct((8, ns), F32), jax.ShapeDtypeStruct((8, ns), F32)),
        scratch_shapes=[pltpu.VMEM((n8, 8, ns), F32), pltpu.VMEM((n8, 8, ns), F32), pltpu.VMEM((8, ns), F32), pltpu.VMEM((8, ns), F32),
                        pltpu.VMEM((8, d), F32)],
        compiler_params=_cp(("arbitrary",)))(dyb, sre, sim, ere, eim, a2_re, a2_im, bre, bim, cre, cim, c2re, c2im, du_in)


def rowmap(fn, rows_in, vecs_in, outs, accs=(), *, name):
    r = rows_in[0].shape[0]
    tm = _row_tile(r, max(a.shape[1] for a in rows_in))
    nr, nv, no = len(rows_in), len(vecs_in), len(outs)

    def body(*refs):
        ins = [x[...] for x in refs[:nr + nv]]
        res = fn(*ins)
        if not isinstance(res, (tuple, list)):
            res = (res,)
        out_refs = refs[nr + nv:]
        for o_ref, v in zip(out_refs[:no], res[:no]):
            o_ref[...] = v.astype(o_ref.dtype)
        if accs:
            @pl.when(pl.program_id(0) == 0)
            def _():
                for a_ref in out_refs[no:]:
                    a_ref[...] = jnp.zeros_like(a_ref)
            for a_ref, v in zip(out_refs[no:], res[no:]):
                a_ref[...] += v

    in_specs = [pl.BlockSpec((tm, a.shape[1]), lambda i: (i, 0)) for a in rows_in]
    in_specs += [pl.BlockSpec(v.shape, lambda i, n=v.ndim: (0,) * n) for v in vecs_in]
    out_specs = [pl.BlockSpec((tm, w), lambda i: (i, 0)) for w, _ in outs] + [pl.BlockSpec(s, lambda i, n=len(s): (0,) * n) for s in accs]
    out_shape = [jax.ShapeDtypeStruct((r, w), dt) for w, dt in outs] + [jax.ShapeDtypeStruct(s, F32) for s in accs]
    res = pl.pallas_call(body, name=name, grid=(r // tm,), in_specs=in_specs, out_specs=tuple(out_specs), out_shape=tuple(out_shape),
                         compiler_params=_cp(("arbitrary",) if accs else ("parallel",)))(*rows_in, *vecs_in)
    return res


def _gelu(x):
    return jax.nn.gelu(x, approximate=True)


def _hg_lower_bound(e0, e1):
    m = jnp.maximum(e0, e1)
    a, b = jnp.exp(e0 - m), jnp.exp(e1 - m)
    return b / (a + b)


def _hg_gates(x, lb):
    logf = jnp.log(lb + (1.0 - lb) * jax.nn.sigmoid(x))
    return logf, (1.0 - lb) * jax.nn.sigmoid(-x)


def _hg_masks(rev):
    n = CHUNK_ROWS
    rr = lax.broadcasted_iota(jnp.int32, (n, n), 0)
    ss = lax.broadcasted_iota(jnp.int32, (n, n), 1)
    same = (rr % NB) == (ss % NB)
    causal = same & ((ss >= rr) if rev else (ss <= rr))
    anti = same & ((ss <= rr) if rev else (ss >= rr))
    end0 = 0 if rev else n - NB
    pick_end = ss == (end0 + rr % NB)
    return same, causal, anti, pick_end, end0


def _hg_expand(x):
    ex = lax.broadcasted_iota(jnp.int32, x.shape, 0) % NB
    return jnp.concatenate([jnp.where(ex == b, x, 0.0) for b in range(NB)], axis=1)


def _hg_fold(xe):
    kk = xe.shape[1] // NB
    ex = lax.broadcasted_iota(jnp.int32, (xe.shape[0], kk), 0) % NB
    out = jnp.zeros((xe.shape[0], kk), F32)
    for b in range(NB):
        out = out + jnp.where(ex == b, xe[:, b * kk:(b + 1) * kk], 0.0)
    return out


def _hg_chunk(q, v, x, lb, masks):
    same, causal, anti, pick_end, end0 = masks
    logf, kk = _hg_gates(x, lb)
    b = _dot3(causal.astype(MXU), logf)
    bend_t = _dot3(pick_end.astype(MXU), b)
    bend_flat = jnp.concatenate([b[end0 + i:end0 + i + 1] for i in range(NB)], axis=1)
    eb = jnp.exp(b)
    enb = jnp.exp(-b)
    ee = jnp.exp(bend_t - b)
    qd, kd, ke = q * eb, kk * enb, kk * ee
    att = jnp.where(causal, _dot(qd, kd, 1, 1), 0.0)
    decay = jnp.exp(bend_flat)
    return dict(same=same, causal=causal, anti=anti, logf=logf, kk=kk, b=b, eb=eb, enb=enb, ee=ee, qd=qd, kd=kd, ke=ke, att=att,
                decay=decay, qde=_hg_expand(qd), kee=_hg_expand(ke))


def _hg_chunk_order(cfg, r):
    nch, ncc = r // CHUNK_ROWS, cfg["rc"] // CHUNK_ROWS
    return nch, ncc


def hg_scan_fwd(cfg, z, lb, *, d_dir, name):
    r = z.shape[0]
    d = z.shape[1] // N_PROJ
    nh = d // HEAD
    rev = d_dir == 1
    nch, ncc = _hg_chunk_order(cfg, r)
    n = CHUNK_ROWS

    def body(q_ref, v_ref, x_ref, lb_ref, o_ref, sin_ref, stk):
        @pl.when(pl.program_id(0) == 0)
        def _():
            stk[...] = jnp.zeros_like(stk)

        masks = _hg_masks(rev)
        for h in range(nh):
            sl = slice(h * HEAD, (h + 1) * HEAD)
            s0 = stk[h]
            sin_ref[0, h] = s0
            v = v_ref[:, sl]
            c = _hg_chunk(q_ref[:, sl], v, x_ref[:, sl], lb_ref[:, sl], masks)
            o_ref[:, sl] = _dot(c["att"], v) + _dot(c["qde"], s0, 1, 1)
            stk[h] = s0 * c["decay"] + _dot(v, c["kee"], 0, 0)

    def cidx(k):
        return _chunk_order(k, ncc, nch, rev)

    blk = lambda p: pl.BlockSpec((n, d), lambda k: (cidx(k), p))
    return pl.pallas_call(
        body, name=name, grid=(nch,),
        in_specs=[blk(0), blk(1), blk(2 + d_dir), pl.BlockSpec((1, d), lambda k: (0, 0))],
        out_specs=(blk(0), pl.BlockSpec((1, nh, HEAD, NB * HEAD), lambda k: (cidx(k), 0, 0, 0))),
        out_shape=(jax.ShapeDtypeStruct((r, d), F32), jax.ShapeDtypeStruct((nch, nh, HEAD, NB * HEAD), F32)),
        scratch_shapes=[pltpu.VMEM((nh, HEAD, NB * HEAD), F32)], compiler_params=_cp(("arbitrary",)))(z, z, z, lb)


def hg_scan_bwd(cfg, do, z, lb, sin, dq_in, dv_in, *, d_dir, name):
    r = z.shape[0]
    d = z.shape[1] // N_PROJ
    nh = d // HEAD
    rev = d_dir == 1
    nch, ncc = _hg_chunk_order(cfg, r)
    n = CHUNK_ROWS
    has_in = dq_in is not None

    def body(*refs):
        if has_in:
            do_ref, q_ref, v_ref, x_ref, lb_ref, sin_ref, dqi_ref, dvi_ref, dq_ref, dv_ref, dx_ref, dlb_ref, dstk = refs
        else:
            do_ref, q_ref, v_ref, x_ref, lb_ref, sin_ref, dq_ref, dv_ref, dx_ref, dlb_ref, dstk = refs
        @pl.when(pl.program_id(0) == 0)
        def _():
            dstk[...] = jnp.zeros_like(dstk)
            dlb_ref[...] = jnp.zeros_like(dlb_ref)

        masks = _hg_masks(rev)
        ex = lax.broadcasted_iota(jnp.int32, (n, HEAD), 0) % NB
        for h in range(nh):
            sl = slice(h * HEAD, (h + 1) * HEAD)
            do_, q, v, x, lb_, s0, ds1 = do_ref[:, sl], q_ref[:, sl], v_ref[:, sl], x_ref[:, sl], lb_ref[:, sl], sin_ref[0, h], dstk[h]
            c = _hg_chunk(q, v, x, lb_, masks)
            datt = jnp.where(c["causal"], _dot(do_, v, 1, 1), 0.0)
            dv = _dot(c["att"], do_, 0, 0) + _dot(c["kee"], ds1, 1, 1)
            dqd = _dot(datt, c["kd"]) + _hg_fold(_dot(do_, s0))
            dkd = _dot(datt, c["qd"], 0, 0)
            dke = _hg_fold(_dot(v, ds1))
            dbend_flat = jnp.sum(ds1 * s0, axis=0, keepdims=True) * c["decay"]
            dstk[h] = _dot(do_, c["qde"], 0, 0) + ds1 * c["decay"]
            dq = dqd * c["eb"]
            dk = dkd * c["enb"] + dke * c["ee"]
            db = dqd * c["qd"] - dkd * c["kd"] - dke * c["ke"]
            dbend_rows = jnp.zeros((n, HEAD), F32)
            for b in range(NB):
                dbend_rows = dbend_rows + jnp.where(ex == b, dbend_flat[:, b * HEAD:(b + 1) * HEAD], 0.0)
            dlogf = _dot3(c["anti"].astype(MXU), db) + _dot3(c["same"].astype(MXU), dke * c["ke"]) + dbend_rows
            _, vjp = jax.vjp(_hg_gates, x, lb_)
            dx, dlb = vjp((dlogf, dk))
            if has_in:
                dq = dq + dqi_ref[:, sl]
                dv = dv + dvi_ref[:, sl]
            dq_ref[:, sl] = dq.astype(dq_ref.dtype)
            dv_ref[:, sl] = dv.astype(dv_ref.dtype)
            dx_ref[:, sl] = dx.astype(dx_ref.dtype)
            dlb_ref[:, sl] += dlb

    def cidx(k):
        return _chunk_order(nch - 1 - k, ncc, nch, rev)

    blk = lambda p: pl.BlockSpec((n, d), lambda k: (cidx(k), p))
    vec = pl.BlockSpec((1, d), lambda k: (0, 0))
    in_specs = [blk(0), blk(0), blk(1), blk(2 + d_dir), vec, pl.BlockSpec((1, nh, HEAD, NB * HEAD), lambda k: (cidx(k), 0, 0, 0))]
    args = [do, z, z, z, lb, sin]
    if has_in:
        in_specs += [blk(0), blk(0)]
        args += [dq_in, dv_in]
    rd = jax.ShapeDtypeStruct((r, d), MXU if has_in else F32)
    return pl.pallas_call(
        body, name=name, grid=(nch,), in_specs=in_specs, out_specs=(blk(0), blk(0), blk(0), vec),
        out_shape=(rd, rd, jax.ShapeDtypeStruct((r, d), MXU), jax.ShapeDtypeStruct((1, d), F32)),
        scratch_shapes=[pltpu.VMEM((nh, HEAD, NB * HEAD), F32)], compiler_params=_cp(("arbitrary",)))(*args)


def _hg_read(o, g, gw):
    on = o * lax.rsqrt(jnp.mean(o * o, axis=-1, keepdims=True) + NORM_EPS) * gw
    return on * jax.nn.sigmoid(g)


def hg_read_fwd(of, ob, z, gw, *, name):
    r, d = of.shape
    nh = d // HEAD
    tm = _row_tile(r)

    def body(of_ref, ob_ref, g_ref, gw_ref, o_ref):
        for h in range(nh):
            sl = slice(h * HEAD, (h + 1) * HEAD)
            o_ref[:, sl] = _hg_read(of_ref[:, sl] + ob_ref[:, sl], g_ref[:, sl], gw_ref[...]).astype(MXU)

    blk = pl.BlockSpec((tm, d), lambda i: (i, 0))
    return pl.pallas_call(
        body, name=name, grid=(r // tm,),
        in_specs=[blk, blk, pl.BlockSpec((tm, d), lambda i: (i, N_PROJ - 1)), pl.BlockSpec((1, HEAD), lambda i: (0, 0))],
        out_specs=blk, out_shape=jax.ShapeDtypeStruct((r, d), MXU), compiler_params=_cp(("parallel",)))(of, ob, z, gw)


def hg_read_bwd(don, of, ob, z, gw, *, name):
    r, d = of.shape
    nh = d // HEAD
    tm = _row_tile(r)

    def body(don_ref, of_ref, ob_ref, g_ref, gw_ref, do_ref, dg_ref, dgw_ref):
        @pl.when(pl.program_id(0) == 0)
        def _():
            dgw_ref[...] = jnp.zeros_like(dgw_ref)

        for h in range(nh):
            sl = slice(h * HEAD, (h + 1) * HEAD)
            _, vjp = jax.vjp(_hg_read, of_ref[:, sl] + ob_ref[:, sl], g_ref[:, sl], gw_ref[...])
            do_ref[:, sl], dg, dgw = vjp(don_ref[:, sl])
            dg_ref[:, sl] = dg.astype(MXU)
            dgw_ref[...] += dgw

    blk = pl.BlockSpec((tm, d), lambda i: (i, 0))
    vec = pl.BlockSpec((1, HEAD), lambda i: (0, 0))
    rd = jax.ShapeDtypeStruct((r, d), F32)
    return pl.pallas_call(
        body, name=name, grid=(r // tm,),
        in_specs=[blk, blk, blk, pl.BlockSpec((tm, d), lambda i: (i, N_PROJ - 1)), vec],
        out_specs=(blk, blk, vec), out_shape=(rd, jax.ShapeDtypeStruct((r, d), MXU), jax.ShapeDtypeStruct((1, HEAD), F32)),
        compiler_params=_cp(("arbitrary",)))(don, of, ob, z, gw)


FFN_COLS = 256


def _seg_masks(cfg, tr, i):
    t = lax.broadcasted_iota(jnp.int32, (tr, FFN_COLS), 0) // NB
    ctx_steps = cfg["rc"] // NB
    pos = jnp.where(i == 0, t % ctx_steps, t % GRID_W)
    last = jnp.where(i == 0, ctx_steps - 1, GRID_W - 1)
    return pos == 0, pos == last


def _prev(x, start):
    return jnp.where(start, 0.0, pltpu.roll(x, NB, 0))


def _next(x, end):
    return jnp.where(end, 0.0, pltpu.roll(x, x.shape[0] - NB, 0))


def _conv3(u, w, b, start, end):
    return ((b + _prev(u, start) * w[0:1]) + u * w[1:2]) + _next(u, end) * w[2:3]


def ffn_mid_fwd(cfg, u, cw, cb, *, name):
    r, f2 = u.shape
    f = f2 // 2
    tr = cfg["rc"]
    nf = f // FFN_COLS

    def body(ua_ref, ug_ref, wa_ref, wg_ref, ba_ref, bg_ref, o_ref, ca_ref, cg_ref):
        start, end = _seg_masks(cfg, tr, pl.program_id(0))
        a = _conv3(ua_ref[...], wa_ref[...], ba_ref[...], start, end)
        g = _conv3(ug_ref[...], wg_ref[...], bg_ref[...], start, end)
        ca_ref[...] = a.astype(MXU)
        cg_ref[...] = g.astype(MXU)
        o_ref[...] = (_silu(a) * g).astype(MXU)

    ca = lambda rows: pl.BlockSpec((rows, FFN_COLS), lambda i, j: (i if rows == tr else 0, j))
    cg = lambda rows: pl.BlockSpec((rows, FFN_COLS), lambda i, j: (i if rows == tr else 0, j + nf))
    half = jax.ShapeDtypeStruct((r, f), MXU)
    return pl.pallas_call(
        body, name=name, grid=(r // tr, nf), in_specs=[ca(tr), cg(tr), ca(3), cg(3), ca(1), cg(1)], out_specs=(ca(tr), ca(tr), ca(tr)),
        out_shape=(jax.ShapeDtypeStruct((r, f), MXU), half, half), compiler_params=_cp(("parallel", "parallel")))(u, u, cw, cw, cb, cb)


def ffn_mid_bwd(cfg, dact, u, ca, cg, cw, *, name):
    r, f2 = u.shape
    f = f2 // 2
    tr = cfg["rc"]
    nf = f // FFN_COLS

    def body(da_ref, us_ref, ca_ref, cg_ref, ws_ref, du_ref, dcw_ref, dcb_ref):
        i = pl.program_id(1)
        is_a = pl.program_id(0) < nf
        start, end = _seg_masks(cfg, tr, i)

        @pl.when(i == 0)
        def _():
            dcw_ref[...] = jnp.zeros_like(dcw_ref)
            dcb_ref[...] = jnp.zeros_like(dcb_ref)

        def finish(dc):
            us, ws = us_ref[...], ws_ref[...]
            dn, dp = _next(dc, end), _prev(dc, start)
            du_ref[...] = (ws[1:2] * dc + ws[0:1] * dn + ws[2:3] * dp).astype(MXU)
            dcw_ref[...] += jnp.concatenate([jnp.sum(dn * us, axis=0, keepdims=True), jnp.sum(dc * us, axis=0, keepdims=True),
                                             jnp.sum(dp * us, axis=0, keepdims=True)], axis=0)
            dcb_ref[...] += jnp.sum(dc, axis=0, keepdims=True)

        @pl.when(is_a)
        def _():
            cs = ca_ref[...].astype(F32)
            sg = jax.nn.sigmoid(cs)
            finish(da_ref[...].astype(F32) * cg_ref[...].astype(F32) * (sg * (1.0 + cs * (1.0 - sg))))

        @pl.when(jnp.logical_not(is_a))
        def _():
            finish(da_ref[...].astype(F32) * _silu(ca_ref[...].astype(F32)))

    cs_ = lambda rows: pl.BlockSpec((rows, FFN_COLS), lambda j, i: (i if rows == tr else 0, j))
    hf = pl.BlockSpec((tr, FFN_COLS), lambda j, i: (i, j % nf))
    gate = pl.BlockSpec((tr, FFN_COLS), lambda j, i: (jnp.where(j < nf, i, 0), jnp.where(j < nf, j, 0)))
    return pl.pallas_call(
        body, name=name, grid=(2 * nf, r // tr), in_specs=[hf, cs_(tr), hf, gate, cs_(3)], out_specs=(cs_(tr), cs_(3), cs_(1)),
        out_shape=(jax.ShapeDtypeStruct((r, f2), MXU), jax.ShapeDtypeStruct((3, f2), F32), jax.ShapeDtypeStruct((1, f2), F32)),
        compiler_params=_cp(("parallel", "arbitrary")))(dact, u, ca, cg, cw)


def hg_lb_fwd(e0, e1, *, name):
    def body(a, b, o):
        o[...] = _hg_lower_bound(a[...], b[...])

    return pl.pallas_call(body, name=name, out_shape=jax.ShapeDtypeStruct(e0.shape, F32))(e0, e1)


def hg_lb_bwd(e0, e1, dlb, *, name):
    def body(a, b, g, oa, ob):
        _, vjp = jax.vjp(_hg_lower_bound, a[...], b[...])
        oa[...], ob[...] = vjp(g[...])

    s = jax.ShapeDtypeStruct(e0.shape, F32)
    return pl.pallas_call(body, name=name, out_shape=(s, s))(e0, e1, dlb)


def _adamw(w, g, m, v):
    m = ADAM_B1 * m + (1.0 - ADAM_B1) * g
    v = ADAM_B2 * v + (1.0 - ADAM_B2) * jnp.square(g)
    m_hat = m / (1.0 - ADAM_B1 ** ADAM_STEP)
    v_hat = v / (1.0 - ADAM_B2 ** ADAM_STEP)
    delta = -ADAM_LR * (m_hat / (jnp.sqrt(v_hat) + ADAM_EPS) + ADAM_WD * w)
    return delta, m, v


def _as2d(a):
    if a.ndim >= 2 and a.shape[-1] % 128 == 0:
        return a.reshape(-1, a.shape[-1])
    return a.reshape(-1, 128) if a.size % 128 == 0 else a.reshape(1, -1)


def adamw(w, g, m, v, *, name):
    w2 = _as2d(w)
    outs = rowmap(_adamw, [w2, _as2d(g), _as2d(m), _as2d(v)], [], [(w2.shape[1], F32)] * 3, name=name)
    return tuple(o.reshape(w.shape) for o in outs)


HBM_SPEC = pl.BlockSpec(memory_space=pltpu.HBM)


def _place():
    mx, my, mc = lax.axis_index("x"), lax.axis_index("y"), lax.axis_index("c")
    others = [(1 - mx, my), (mx, 1 - my), (1 - mx, 1 - my)]
    return mx, my, mc, others


def chip_allgather(x, *, name):
    def body(x_ref, o_ref, send_sems, recv_sems, local_sem):
        mx, my, mc, others = _place()
        me = 2 * mx + my
        mine = pltpu.make_async_copy(x_ref, o_ref.at[me], local_sem)
        mine.start()
        sends = [pltpu.make_async_remote_copy(src_ref=x_ref, dst_ref=o_ref.at[me], send_sem=send_sems.at[j], recv_sem=recv_sems.at[j],
                                              device_id=(px, py, mc), device_id_type=MESH) for j, (px, py) in enumerate(others)]
        for cp in sends:
            cp.start()
        for j, (px, py) in enumerate(others):
            pltpu.make_async_remote_copy(src_ref=x_ref, dst_ref=o_ref.at[2 * px + py], send_sem=send_sems.at[j], recv_sem=recv_sems.at[j],
                                         device_id=(px, py, mc), device_id_type=MESH).wait_recv()
        for cp in sends:
            cp.wait_send()
        mine.wait()

    return pl.pallas_call(
        body, name=name, out_shape=jax.ShapeDtypeStruct((4,) + x.shape, x.dtype), in_specs=[HBM_SPEC], out_specs=HBM_SPEC,
        scratch_shapes=[pltpu.SemaphoreType.DMA((3,)), pltpu.SemaphoreType.DMA((3,)), pltpu.SemaphoreType.DMA])(x)


def _win(ref, axis, start, size):
    idx = [slice(None)] * len(ref.shape)
    idx[axis] = pl.ds(start, size)
    return ref.at[tuple(idx)]


def _half_axis(shape, ax):
    if shape[0] == 2:
        return 0
    return 2 if ax == 1 else 1


def _cut(shape, axis, parts):
    return shape[:axis] + (shape[axis] // parts,) + shape[axis + 1:]


def _hbm_call(body, arrays, out_shapes, sems, name):
    n_in = len(arrays)
    return pl.pallas_call(body, name=name, out_shape=tuple(out_shapes), in_specs=[HBM_SPEC] * n_in, out_specs=tuple([HBM_SPEC] * len(out_shapes)),
                          scratch_shapes=sems)(*arrays)


def place_shard(shard, ax, chip, dtype, *, name):
    l, r, c = shard.shape
    tr = _row_tile(r, c)
    per_block = (l, r // tr, 1)[ax]

    def omap(li, ri, cref):
        idx = [li, ri, 0]
        idx[ax] = idx[ax] + cref[0] * per_block
        return tuple(idx)

    def body(c_ref, s_ref, o_ref):
        o_ref[...] = s_ref[...].astype(dtype)

    full = shard.shape[:ax] + (4 * shard.shape[ax],) + shard.shape[ax + 1:]
    return pl.pallas_call(
        body, name=name, out_shape=jax.ShapeDtypeStruct(full, dtype),
        grid_spec=pltpu.PrefetchScalarGridSpec(
            num_scalar_prefetch=1, grid=(l, r // tr),
            in_specs=[pl.BlockSpec((1, tr, c), lambda li, ri, cref: (li, ri, 0))], out_specs=pl.BlockSpec((1, tr, c), omap)),
        compiler_params=_cp(("parallel", "parallel")))(chip, shard)


def gather_placed(arrays, axes, haxes, *, name):
    n = len(arrays)

    def body(*refs):
        ins, outs = refs[:n], refs[n:2 * n]
        send_sems, recv_sems = refs[2 * n:]
        mx, my, mc, others = _place()
        me = 2 * mx + my

        def part(ref, i, chip):
            sz, hs = arrays[i].shape[axes[i]] // 4, arrays[i].shape[haxes[i]] // 2
            return _win(_win(ref, axes[i], chip * sz, sz), haxes[i], mc * hs, hs)

        sends = []
        for i in range(n):
            for j, (px, py) in enumerate(others):
                rc = pltpu.make_async_remote_copy(src_ref=part(ins[i], i, me), dst_ref=part(outs[i], i, me), send_sem=send_sems.at[i, j],
                                                  recv_sem=recv_sems.at[i, j], device_id=(px, py, mc), device_id_type=MESH)
                rc.start()
                sends.append(rc)
        for i in range(n):
            for j, (px, py) in enumerate(others):
                pltpu.make_async_remote_copy(src_ref=part(ins[i], i, me), dst_ref=part(outs[i], i, 2 * px + py), send_sem=send_sems.at[i, j],
                                             recv_sem=recv_sems.at[i, j], device_id=(px, py, mc), device_id_type=MESH).wait_recv()
        for rc in sends:
            rc.wait_send()

    return pl.pallas_call(
        body, name=name, out_shape=tuple(jax.ShapeDtypeStruct(a_.shape, a_.dtype) for a_ in arrays), in_specs=[HBM_SPEC] * n,
        out_specs=tuple([HBM_SPEC] * n), input_output_aliases={i: i for i in range(n)},
        scratch_shapes=[pltpu.SemaphoreType.DMA((n, 3)), pltpu.SemaphoreType.DMA((n, 3))])(*arrays)


SEM_SPEC = pl.BlockSpec(memory_space=pltpu.SEMAPHORE)
SPLIT_COPY = pltpu.CompilerParams(has_side_effects=pltpu.SideEffectType.DATAFLOW_SIDE_EFFECTING)


def _gather_part(ref, shape, ax, hax, chip, core):
    sz, hs = shape[ax] // 4, shape[hax] // 2
    return _win(_win(ref, ax, chip * sz, sz), hax, core * hs, hs)


def gather_placed_start(arrays, axes, haxes, after, *, name):
    n = len(arrays)

    m = 3 * n

    def body(*refs):
        ins, send_sems, recv_sems = refs[:n], refs[n + 1:n + 1 + m], refs[n + 1 + m:n + 1 + 2 * m]
        token = refs[2 * n + 1 + 2 * m]
        mx, my, mc, others = _place()
        me = 2 * mx + my
        for i in range(n):
            for j, (px, py) in enumerate(others):
                part = _gather_part(ins[i], arrays[i].shape, axes[i], haxes[i], me, mc)
                pltpu.make_async_remote_copy(src_ref=part, dst_ref=part, send_sem=send_sems[3 * i + j], recv_sem=recv_sems[3 * i + j],
                                             device_id=(px, py, mc), device_id_type=MESH).start()
        token[...] = jnp.zeros_like(token)

    hbm = [pltpu.with_memory_space_constraint(a_, pltpu.HBM) for a_ in arrays]
    out = pl.pallas_call(
        body, name=name,
        out_shape=tuple([pltpu.SemaphoreType.DMA(())] * (2 * m)) + tuple(pltpu.HBM(a_.shape, a_.dtype) for a_ in arrays)
        + (jax.ShapeDtypeStruct((8, 128), F32),),
        in_specs=[HBM_SPEC] * n + [pl.BlockSpec(memory_space=pl.ANY)],
        out_specs=tuple([SEM_SPEC] * (2 * m)) + tuple([HBM_SPEC] * n) + (pl.BlockSpec(memory_space=pltpu.VMEM),),
        input_output_aliases={i: 2 * m + i for i in range(n)}, compiler_params=SPLIT_COPY)(*hbm, after)
    return list(out[:m]), list(out[m:2 * m]), list(out[2 * m:2 * m + n]), out[2 * m + n]


def gather_placed_wait(arrays, send_sems, recv_sems, axes, haxes, after, *, name):
    n = len(arrays)

    m = 3 * n

    def body(*refs):
        ins, send_refs, recv_refs = refs[:n], refs[n:n + m], refs[n + m:n + 2 * m]
        mx, my, mc, others = _place()
        me = 2 * mx + my
        for i in range(n):
            for j, (px, py) in enumerate(others):
                cp = pltpu.make_async_remote_copy(
                    src_ref=_gather_part(ins[i], arrays[i].shape, axes[i], haxes[i], me, mc),
                    dst_ref=_gather_part(ins[i], arrays[i].shape, axes[i], haxes[i], 2 * px + py, mc),
                    send_sem=send_refs[3 * i + j], recv_sem=recv_refs[3 * i + j], device_id=(px, py, mc), device_id_type=MESH)
                cp.wait_send()
                cp.wait_recv()

    out = pl.pallas_call(
        body, name=name, out_shape=tuple(pltpu.HBM(a_.shape, a_.dtype) for a_ in arrays),
        in_specs=[HBM_SPEC] * n + [SEM_SPEC] * (2 * m) + [pl.BlockSpec(memory_space=pl.ANY)], out_specs=tuple([HBM_SPEC] * n),
        input_output_aliases={i: i for i in range(n)}, compiler_params=SPLIT_COPY)(*arrays, *send_sems, *recv_sems, after)
    return list(out)


def pair_swap_halves(arrays, haxes, *, name):
    n = len(arrays)

    def body(*refs):
        ins, outs = refs[:n], refs[n:2 * n]
        send_sems, recv_sems = refs[2 * n:]
        mx, my, mc, _ = _place()
        cps = []
        for i in range(n):
            hs = arrays[i].shape[haxes[i]] // 2
            cp = pltpu.make_async_remote_copy(src_ref=_win(ins[i], haxes[i], (1 - mc) * hs, hs), dst_ref=outs[i], send_sem=send_sems.at[i],
                                              recv_sem=recv_sems.at[i], device_id=(mx, my, 1 - mc), device_id_type=MESH)
            cp.start()
            cps.append(cp)
        for cp in cps:
            cp.wait()

    outs = [jax.ShapeDtypeStruct(_cut(a_.shape, h_, 2), a_.dtype) for a_, h_ in zip(arrays, haxes)]
    return _hbm_call(body, arrays, outs, [pltpu.SemaphoreType.DMA((n,)), pltpu.SemaphoreType.DMA((n,))], name)


def add_own_half(g, t, hax, core, *, out_dtype, name):
    l, r, c = t.shape
    tr = _row_tile(r, c)
    per_half = (l, r // tr, 1)[hax]

    def imap(li, ri, cref):
        idx = [li, ri, 0]
        idx[hax] = idx[hax] + cref[0] * per_half
        return tuple(idx)

    def body(c_ref, g_ref, t_ref, o_ref):
        o_ref[...] = (g_ref[...] + t_ref[...]).astype(out_dtype)

    return pl.pallas_call(
        body, name=name, out_shape=jax.ShapeDtypeStruct(t.shape, out_dtype),
        grid_spec=pltpu.PrefetchScalarGridSpec(
            num_scalar_prefetch=1, grid=(l, r // tr),
            in_specs=[pl.BlockSpec((1, tr, c), imap), pl.BlockSpec((1, tr, c), lambda li, ri, cref: (li, ri, 0))],
            out_specs=pl.BlockSpec((1, tr, c), lambda li, ri, cref: (li, ri, 0))),
        compiler_params=_cp(("parallel", "parallel")))(core, g, t)


def exchange_blocks(arrays, axes, *, name):
    n = len(arrays)

    def body(*refs):
        ins, outs = refs[:n], refs[n:2 * n]
        send_sems, recv_sems, local_sems = refs[2 * n:]
        mx, my, mc, others = _place()
        me = 2 * mx + my
        waits = []
        for i in range(n):
            sz = arrays[i].shape[axes[i]] // 4
            cp = pltpu.make_async_copy(_win(ins[i], axes[i], me * sz, sz), outs[i].at[me], local_sems.at[i])
            cp.start()
            waits.append(cp.wait)
            for j, (px, py) in enumerate(others):
                rc = pltpu.make_async_remote_copy(src_ref=_win(ins[i], axes[i], (2 * px + py) * sz, sz), dst_ref=outs[i].at[me],
                                                  send_sem=send_sems.at[i, j], recv_sem=recv_sems.at[i, j], device_id=(px, py, mc),
                                                  device_id_type=MESH)
                rc.start()
                waits.append(rc.wait_send)
        for i in range(n):
            sz = arrays[i].shape[axes[i]] // 4
            for j, (px, py) in enumerate(others):
                pltpu.make_async_remote_copy(src_ref=_win(ins[i], axes[i], me * sz, sz), dst_ref=outs[i].at[2 * px + py],
                                             send_sem=send_sems.at[i, j], recv_sem=recv_sems.at[i, j], device_id=(px, py, mc),
                                             device_id_type=MESH).wait_recv()
        for w_ in waits:
            w_()

    outs = [jax.ShapeDtypeStruct((4,) + _cut(a_.shape, ax, 4), a_.dtype) for a_, ax in zip(arrays, axes)]
    return _hbm_call(body, arrays, outs, [pltpu.SemaphoreType.DMA((n, 3)), pltpu.SemaphoreType.DMA((n, 3)), pltpu.SemaphoreType.DMA((n,))], name)


def exchange_blocks_start(arrays, axes, *, name):
    n = len(arrays)
    lands = [lax.empty((4,) + _cut(a_.shape, ax, 4), a_.dtype) for a_, ax in zip(arrays, axes)]

    def body(*refs):
        ins, lnd = refs[:n], refs[n:2 * n]
        send_sems, recv_sems = refs[2 * n:6 * n], refs[6 * n:9 * n]
        token = refs[11 * n]
        mx, my, mc, others = _place()
        me = 2 * mx + my
        for i in range(n):
            sz = arrays[i].shape[axes[i]] // 4
            pltpu.make_async_copy(_win(ins[i], axes[i], me * sz, sz), lnd[i].at[me], send_sems[4 * i + 3]).start()
            for j, (px, py) in enumerate(others):
                pltpu.make_async_remote_copy(src_ref=_win(ins[i], axes[i], (2 * px + py) * sz, sz), dst_ref=lnd[i].at[me],
                                             send_sem=send_sems[4 * i + j], recv_sem=recv_sems[3 * i + j], device_id=(px, py, mc),
                                             device_id_type=MESH).start()
        token[...] = jnp.zeros_like(token)

    hbm = [pltpu.with_memory_space_constraint(a_, pltpu.HBM) for a_ in arrays + lands]
    out = pl.pallas_call(
        body, name=name,
        out_shape=tuple([pltpu.SemaphoreType.DMA(())] * (7 * n)) + tuple(pltpu.HBM(a_.shape, a_.dtype) for a_ in arrays + lands)
        + (jax.ShapeDtypeStruct((8, 128), F32),),
        in_specs=[HBM_SPEC] * (2 * n),
        out_specs=tuple([SEM_SPEC] * (7 * n)) + tuple([HBM_SPEC] * (2 * n)) + (pl.BlockSpec(memory_space=pltpu.VMEM),),
        input_output_aliases={i: 7 * n + i for i in range(2 * n)}, compiler_params=SPLIT_COPY)(*hbm)
    return list(out[:7 * n]), list(out[7 * n:8 * n]), list(out[8 * n:9 * n]), out[9 * n]


def exchange_blocks_wait(sems, arrays, lands, axes, after, *, name):
    n = len(arrays)

    def body(*refs):
        ins, lnd = refs[:n], refs[n:2 * n]
        send_sems, recv_sems = refs[2 * n:6 * n], refs[6 * n:9 * n]
        mx, my, mc, others = _place()
        me = 2 * mx + my
        for i in range(n):
            sz = arrays[i].shape[axes[i]] // 4
            mine = _win(ins[i], axes[i], me * sz, sz)
            pltpu.make_async_copy(mine, lnd[i].at[me], send_sems[4 * i + 3]).wait()
            for j, (px, py) in enumerate(others):
                cp = pltpu.make_async_remote_copy(src_ref=mine, dst_ref=lnd[i].at[2 * px + py], send_sem=send_sems[4 * i + j],
                                                  recv_sem=recv_sems[3 * i + j], device_id=(px, py, mc), device_id_type=MESH)
                cp.wait_send()
                cp.wait_recv()

    out = pl.pallas_call(
        body, name=name, out_shape=tuple(pltpu.HBM(a_.shape, a_.dtype) for a_ in arrays + lands),
        in_specs=[HBM_SPEC] * (2 * n) + [SEM_SPEC] * (7 * n) + [pl.BlockSpec(memory_space=pl.ANY)] * len(after),
        out_specs=tuple([HBM_SPEC] * (2 * n)), input_output_aliases={i: i for i in range(2 * n)}, compiler_params=SPLIT_COPY)(
            *arrays, *lands, *sems, *after)
    return list(out[n:])


def sum_blocks(e, hax, core, *, name):
    _, l, r, c = e.shape
    tr = _row_tile(r, c)
    per_half = (l, r // tr, 1)[hax]

    def omap(li, ri, cref):
        idx = [li, ri, 0]
        idx[hax] = idx[hax] + cref[0] * per_half
        return tuple(idx)

    def body(c_ref, e_ref, o_ref):
        v = e_ref[...].astype(F32)
        o_ref[...] = ((v[0] + v[1]) + v[2]) + v[3]

    full = (l, r, c)[:hax] + (2 * (l, r, c)[hax],) + (l, r, c)[hax + 1:]
    return pl.pallas_call(
        body, name=name, out_shape=jax.ShapeDtypeStruct(full, F32),
        grid_spec=pltpu.PrefetchScalarGridSpec(
            num_scalar_prefetch=1, grid=(l, r // tr),
            in_specs=[pl.BlockSpec((4, 1, tr, c), lambda li, ri, cref: (0, li, ri, 0))], out_specs=pl.BlockSpec((1, tr, c), omap)),
        compiler_params=_cp(("parallel", "parallel")))(core, e)


def pair_fill_halves(arrays, haxes, *, name):
    n = len(arrays)

    def body(*refs):
        ins, outs = refs[:n], refs[n:2 * n]
        send_sems, recv_sems = refs[2 * n:]
        mx, my, mc, _ = _place()
        cps = []
        for i in range(n):
            hs = arrays[i].shape[haxes[i]] // 2
            mine = _win(ins[i], haxes[i], mc * hs, hs)
            cp = pltpu.make_async_remote_copy(src_ref=mine, dst_ref=_win(outs[i], haxes[i], mc * hs, hs), send_sem=send_sems.at[i],
                                              recv_sem=recv_sems.at[i], device_id=(mx, my, 1 - mc), device_id_type=MESH)
            cp.start()
            cps.append(cp)
        for i in range(n):
            hs = arrays[i].shape[haxes[i]] // 2
            pltpu.make_async_remote_copy(src_ref=_win(ins[i], haxes[i], mc * hs, hs), dst_ref=_win(outs[i], haxes[i], (1 - mc) * hs, hs),
                                         send_sem=send_sems.at[i], recv_sem=recv_sems.at[i], device_id=(mx, my, 1 - mc),
                                         device_id_type=MESH).wait_recv()
        for cp in cps:
            cp.wait_send()

    return pl.pallas_call(
        body, name=name, out_shape=tuple(jax.ShapeDtypeStruct(a_.shape, a_.dtype) for a_ in arrays), in_specs=[HBM_SPEC] * n,
        out_specs=tuple([HBM_SPEC] * n), input_output_aliases={i: i for i in range(n)},
        scratch_shapes=[pltpu.SemaphoreType.DMA((n,)), pltpu.SemaphoreType.DMA((n,))])(*arrays)


WEIGHTS = ['c_ctx', 'w_mod', 'b_mod', 'norm1_w', 'norm2_w', 'final_norm_w', 's5_w_in', 's5_lam_re', 's5_lam_im', 's5_log_step', 's5_b_re', 's5_b_im', 's5_c_re', 's5_c_im', 's5_d', 's5_w_glu', 's5_w_out', 'hg_w_in', 'hg_lower_bounds', 'hg_gnorm_w', 'hg_w_out', 'ffn_w_up', 'ffn_conv_w', 'ffn_conv_b', 'ffn_w_down']
INPUTS = ['x', 'c', 'ctx', 'c_ctx', 'w_mod', 'b_mod', 'norm1_w', 'norm2_w', 'final_norm_w', 's5_w_in', 's5_lam_re', 's5_lam_im', 's5_log_step', 's5_b_re', 's5_b_im', 's5_c_re', 's5_c_im', 's5_d', 's5_w_glu', 's5_w_out', 'hg_w_in', 'hg_lower_bounds', 'hg_gnorm_w', 'hg_w_out', 'ffn_w_up', 'ffn_conv_w', 'ffn_conv_b', 'ffn_w_down', 'loss_target', 'm_c_ctx', 'm_w_mod', 'm_b_mod', 'm_norm1_w', 'm_norm2_w', 'm_final_norm_w', 'm_s5_w_in', 'm_s5_lam_re', 'm_s5_lam_im', 'm_s5_log_step', 'm_s5_b_re', 'm_s5_b_im', 'm_s5_c_re', 'm_s5_c_im', 'm_s5_d', 'm_s5_w_glu', 'm_s5_w_out', 'm_hg_w_in', 'm_hg_lower_bounds', 'm_hg_gnorm_w', 'm_hg_w_out', 'm_ffn_w_up', 'm_ffn_conv_w', 'm_ffn_conv_b', 'm_ffn_w_down', 'v_c_ctx', 'v_w_mod', 'v_b_mod', 'v_norm1_w', 'v_norm2_w', 'v_final_norm_w', 'v_s5_w_in', 'v_s5_lam_re', 'v_s5_lam_im', 'v_s5_log_step', 'v_s5_b_re', 'v_s5_b_im', 'v_s5_c_re', 'v_s5_c_im', 'v_s5_d', 'v_s5_w_glu', 'v_s5_w_out', 'v_hg_w_in', 'v_hg_lower_bounds', 'v_hg_gnorm_w', 'v_hg_w_out', 'v_ffn_w_up', 'v_ffn_conv_w', 'v_ffn_conv_b', 'v_ffn_w_down']
SHARD_AXIS = {"w_mod": 2, "s5_w_in": 1, "s5_w_glu": 1, "s5_w_out": 1, "hg_w_in": 2, "hg_lower_bounds": 2, "hg_w_out": 1,
              "ffn_w_up": 2, "ffn_conv_w": 2, "ffn_w_down": 1}
GATHER_F32 = ("hg_lower_bounds", "ffn_conv_w")
PACK_W = 1024
GRAD_WIRE = jnp.bfloat16


def _reduce_start(items, core, tag):
    names, arrays, axes = [n for n, _, _ in items], [g_ for _, g_, _ in items], [ax for _, _, ax in items]
    haxes = [_half_axis(g_.shape, ax) for g_, ax in zip(arrays, axes)]
    t = pair_swap_halves(arrays, haxes, name="grad_pair_swap_" + tag)
    h = [add_own_half(g_, t_, hx, core, out_dtype=GRAD_WIRE, name="grad_pair_add_" + n) for g_, t_, hx, n in zip(arrays, t, haxes, names)]
    sems, h, lands, token = exchange_blocks_start(h, axes, name="grad_exchange_start_" + tag)
    return (names, sems, h, lands, axes, haxes), token


def _reduce_finish(state, core, after, tag):
    names, sems, h, lands, axes, haxes = state
    e = exchange_blocks_wait(sems, h, lands, axes, list(after), name="grad_exchange_wait_" + tag)
    s = [sum_blocks(e_, hx, core, name="grad_chip_sum_" + n) for e_, hx, n in zip(e, haxes, names)]
    return dict(zip(names, pair_fill_halves(s, haxes, name="grad_pair_fill_" + tag)))


def _pack_small(grads, small):
    flat = jnp.concatenate([grads[n].reshape(-1) for n in small])
    pad = (-flat.shape[0]) % (64 * PACK_W)
    return jnp.pad(flat, (0, pad)).reshape(1, -1, PACK_W)


def _unpack_small(a, block, small):
    sm = chip_allgather(block[0], name="allgather_small_grads").reshape(-1)
    out, off = {}, 0
    for n in small:
        out[n] = sm[off:off + math.prod(a[n].shape)].reshape(a[n].shape)
        off += math.prod(a[n].shape)
    return out


def _blockdiag_b(bb, kb):
    gl = S5_KIN // S5_GROUP
    x = bb.reshape(kb, gl, S5_GROUP, S5_STATE)
    return (x[:, :, :, None, :] * jnp.eye(gl, dtype=bb.dtype)[None, :, None, :, None]).reshape(kb, S5_KIN, S5_KST)


def _blockdiag_c(cc, kb):
    gl = S5_KIN // S5_GROUP
    x = cc.reshape(kb, gl, S5_GROUP, S5_STATE).transpose(0, 1, 3, 2)
    return (x[:, :, :, None, :] * jnp.eye(gl, dtype=cc.dtype)[None, :, None, :, None]).reshape(kb, S5_KST, S5_KIN)


def _diag_b(m, kb):
    gl = S5_KIN // S5_GROUP
    x = m.reshape(kb, gl, S5_GROUP, gl, S5_STATE)
    return jnp.stack([x[:, i, :, i, :] for i in range(gl)], axis=1).reshape(kb * gl, S5_GROUP, S5_STATE)


def _diag_c(m, kb):
    gl = S5_KIN // S5_GROUP
    x = m.reshape(kb, gl, S5_STATE, gl, S5_GROUP)
    return jnp.stack([x[:, i, :, i, :] for i in range(gl)], axis=1).transpose(0, 1, 3, 2).reshape(kb * gl, S5_GROUP, S5_STATE)


def kernel(x, c, ctx, c_ctx, w_mod, b_mod, norm1_w, norm2_w, final_norm_w, s5_w_in, s5_lam_re, s5_lam_im, s5_log_step, s5_b_re, s5_b_im, s5_c_re, s5_c_im, s5_d, s5_w_glu, s5_w_out, hg_w_in, hg_lower_bounds, hg_gnorm_w, hg_w_out, ffn_w_up, ffn_conv_w, ffn_conv_b, ffn_w_down, loss_target, m_c_ctx, m_w_mod, m_b_mod, m_norm1_w, m_norm2_w, m_final_norm_w, m_s5_w_in, m_s5_lam_re, m_s5_lam_im, m_s5_log_step, m_s5_b_re, m_s5_b_im, m_s5_c_re, m_s5_c_im, m_s5_d, m_s5_w_glu, m_s5_w_out, m_hg_w_in, m_hg_lower_bounds, m_hg_gnorm_w, m_hg_w_out, m_ffn_w_up, m_ffn_conv_w, m_ffn_conv_b, m_ffn_w_down, v_c_ctx, v_w_mod, v_b_mod, v_norm1_w, v_norm2_w, v_final_norm_w, v_s5_w_in, v_s5_lam_re, v_s5_lam_im, v_s5_log_step, v_s5_b_re, v_s5_b_im, v_s5_c_re, v_s5_c_im, v_s5_d, v_s5_w_glu, v_s5_w_out, v_hg_w_in, v_hg_lower_bounds, v_hg_gnorm_w, v_hg_w_out, v_ffn_w_up, v_ffn_conv_w, v_ffn_conv_b, v_ffn_w_down):
    a = dict(zip(INPUTS, (x, c, ctx, c_ctx, w_mod, b_mod, norm1_w, norm2_w, final_norm_w, s5_w_in, s5_lam_re, s5_lam_im, s5_log_step, s5_b_re, s5_b_im, s5_c_re, s5_c_im, s5_d, s5_w_glu, s5_w_out, hg_w_in, hg_lower_bounds, hg_gnorm_w, hg_w_out, ffn_w_up, ffn_conv_w, ffn_conv_b, ffn_w_down, loss_target, m_c_ctx, m_w_mod, m_b_mod, m_norm1_w, m_norm2_w, m_final_norm_w, m_s5_w_in, m_s5_lam_re, m_s5_lam_im, m_s5_log_step, m_s5_b_re, m_s5_b_im, m_s5_c_re, m_s5_c_im, m_s5_d, m_s5_w_glu, m_s5_w_out, m_hg_w_in, m_hg_lower_bounds, m_hg_gnorm_w, m_hg_w_out, m_ffn_w_up, m_ffn_conv_w, m_ffn_conv_b, m_ffn_w_down, v_c_ctx, v_w_mod, v_b_mod, v_norm1_w, v_norm2_w, v_final_norm_w, v_s5_w_in, v_s5_lam_re, v_s5_lam_im, v_s5_log_step, v_s5_b_re, v_s5_b_im, v_s5_c_re, v_s5_c_im, v_s5_d, v_s5_w_glu, v_s5_w_out, v_hg_w_in, v_hg_lower_bounds, v_hg_gnorm_w, v_hg_w_out, v_ffn_w_up, v_ffn_conv_w, v_ffn_conv_b, v_ffn_w_down)))
    nb, seq, d = x.shape
    assert nb == NB
    rc = nb * ctx.shape[1]
    cfg = {"rc": rc}
    f = a["ffn_w_down"].shape[1] * 4
    core = lax.axis_index("c").astype(jnp.int32).reshape(1)

    w = {n: a[n] for n in WEIGHTS if n not in SHARD_AXIS}
    chip = (2 * lax.axis_index("x") + lax.axis_index("y")).astype(jnp.int32).reshape(1)
    groups = {
        "now": [("w_mod0", a["w_mod"][0:1]), ("s5_w_in", a["s5_w_in"]), ("hg_lower_bounds", a["hg_lower_bounds"]), ("ffn_conv_w", a["ffn_conv_w"])],
        "mid": [("s5_w_glu", a["s5_w_glu"]), ("s5_w_out", a["s5_w_out"]), ("ffn_w_up0", a["ffn_w_up"][0:1]), ("ffn_w_down0", a["ffn_w_down"][0:1])],
        "later": [("w_mod1", a["w_mod"][1:2]), ("hg_w_in", a["hg_w_in"]), ("hg_w_out", a["hg_w_out"]), ("ffn_w_up1", a["ffn_w_up"][1:2]),
                  ("ffn_w_down1", a["ffn_w_down"][1:2])]}
    shard_axis = lambda n: SHARD_AXIS[n.rstrip("01")]
    placed = {g: [place_shard(s_, shard_axis(n), chip, F32 if n in GATHER_F32 else MXU, name="place_" + n) for n, s_ in it] for g, it in groups.items()}
    axes = {g: [shard_axis(n) for n, _ in it] for g, it in groups.items()}
    haxes = {g: [_half_axis(p_.shape, ax) for p_, ax in zip(placed[g], axes[g])] for g in groups}
    fly_now = gather_placed_start(placed["now"], axes["now"], haxes["now"], chip, name="allgather_now_start")
    fly_mid = gather_placed_start(placed["mid"], axes["mid"], haxes["mid"], fly_now[3], name="allgather_mid_start")
    fly_later = gather_placed_start(placed["later"], axes["later"], haxes["later"], fly_mid[3], name="allgather_later_start")

    def land(fly, g, after):
        send_, recv_, flying, _ = fly
        landed = gather_placed_wait(flying, send_, recv_, axes[g], haxes[g], after, name=f"allgather_{g}_wait")
        w.update(dict(zip([n for n, _ in groups[g]], pair_fill_halves(landed, haxes[g], name=f"allgather_{g}_pair_fill"))))

    tmaj = lambda t: t.transpose(1, 0, 2).reshape(-1, t.shape[-1])
    zero = fly_later[3][0:1, 0:1]
    x0 = jnp.concatenate([tmaj(ctx), tmaj(x)], axis=0)
    tgt = tmaj(a["loss_target"])
    land(fly_now, "now", x0)
    c16 = jnp.concatenate([jnp.broadcast_to(c_ctx[None], (8, d)), c, c], axis=0) + zero
    mt0, scb = mod_fwd(c16, w["w_mod0"][0], w["b_mod"][0][None], name="mod_fwd0")
    mt = [mt0, None]
    n1, n2 = w["norm1_w"], w["norm2_w"]
    w["w_mod"], w["ffn_w_up"], w["ffn_w_down"] = [w["w_mod0"][0], None], [None, None], [None, None]

    def ffn_fwd(l, h):
        u = mm(h, w["ffn_w_up"][l], name=f"ffn_up{l}")
        act, ca, cg = ffn_mid_fwd(cfg, u, w["ffn_conv_w"][l], w["ffn_conv_b"][l][None], name=f"ffn_mid{l}")
        return (u, ca, cg), act, mm(act, w["ffn_w_down"][l], name=f"ffn_down{l}")

    def ffn_bwd(l, dfo, kept, act, h, zero=0.0):
        dact = mm(dfo, w["ffn_w_down"][l], tb=True, out_dtype=MXU, name=f"ffn_down_dx{l}")
        dwd = mm(act, dfo, ta=True, name=f"ffn_down_dw{l}")
        du, dcw, dcb = ffn_mid_bwd(cfg, dact, *kept, w["ffn_conv_w"][l] + zero, name=f"ffn_mid_bwd{l}")
        dh = mm(du, w["ffn_w_up"][l], tb=True, name=f"ffn_up_dx{l}")
        dwu = mm(h, du, ta=True, name=f"ffn_up_dw{l}")
        return dh, dwu, dcw, dcb[0], dwd

    g_, p_ = d // S5_GROUP, S5_STATE
    ns, kb = g_ * p_, d // S5_KIN
    s5p = (w["s5_lam_re"][0].reshape(2 * g_, p_), w["s5_lam_im"][0].reshape(2 * g_, p_), w["s5_log_step"][0].reshape(2 * g_, 1),
           w["s5_b_re"][0].transpose(0, 1, 3, 2).reshape(2 * g_, S5_GROUP, p_), w["s5_b_im"][0].transpose(0, 1, 3, 2).reshape(2 * g_, S5_GROUP, p_))
    ar, ai, bbr, bbi = s5_disc_fwd(*s5p, name="s5_disc")
    dsk = w["s5_d"]
    _, h1 = node_fwd(cfg, x0, None, None, 0, n1[0:1], mt[0], 0, name="node0a")
    u0 = mm(h1, w["s5_w_in"][0], name="s5_in")
    s5s, ys = [], []
    for dd in range(2):
        sl = slice(dd * g_, (dd + 1) * g_)
        a_r, a_i = ar[sl].reshape(1, ns), ai[sl].reshape(1, ns)
        a2 = (a_r * a_r - a_i * a_i, 2.0 * a_r * a_i)
        b_r, b_i = _blockdiag_b(bbr[sl], kb), _blockdiag_b(bbi[sl], kb)
        c_r, c_i = _blockdiag_c(w["s5_c_re"][0, dd], kb), _blockdiag_c(w["s5_c_im"][0, dd], kb)
        ak, ai_k = a_r.reshape(kb, 1, S5_KST), a_i.reshape(kb, 1, S5_KST)
        ab = (ak * b_r - ai_k * b_i, ak * b_i + ai_k * b_r)
        akc, aic = ak.reshape(kb, S5_KST, 1), ai_k.reshape(kb, S5_KST, 1)
        c2 = (akc * c_r - aic * c_i, akc * c_i + aic * c_r)
        bf = lambda t_: t_.astype(MXU)
        sre, sim, ere, eim, y_ = s5_scan_fwd(cfg, u0, a2[0], a2[1], bf(b_r), bf(b_i), bf(ab[0]), bf(ab[1]), bf(c_r), bf(c_i), rev=dd == 1,
                                             name=f"s5_scan{dd}")
        s5s.append((sre, sim, ere, eim, a2[0], a2[1], bf(b_r), bf(b_i), bf(c_r), bf(c_i), bf(c2[0]), bf(c2[1])))
        ys.append(y_)

    def glu_a(u, y0, y1, ds):
        yp = (ds * u + y0) + y1
        return yp, _gelu(yp)

    ypre, zgb = rowmap(glu_a, [u0, ys[0], ys[1]], [dsk], [(d, F32), (d, MXU)], name="s5_glu_a")
    land(fly_mid, "mid", zgb)
    w["ffn_w_up"][0], w["ffn_w_down"][0] = w["ffn_w_up0"][0], w["ffn_w_down0"][0]
    tg = mm(zgb, w["s5_w_glu"][0], name="s5_glu")
    (z2,) = rowmap(lambda yp, t: _gelu(yp) * jax.nn.sigmoid(t), [ypre, tg], [], [(d, MXU)], name="s5_glu_b")
    y1a = mm(z2, w["s5_w_out"][0], name="s5_out")
    x1a, h2a = node_fwd(cfg, x0, y1a, mt[0], 2, n2[0:1], mt[0], 3, name="node0b")
    ufa, acta, foa = ffn_fwd(0, h2a)

    land(fly_later, "later", foa)
    w["w_mod"][1], w["ffn_w_up"][1], w["ffn_w_down"][1] = w["w_mod1"][0], w["ffn_w_up1"][0], w["ffn_w_down1"][0]
    mt[1], _ = mod_fwd(c16, w["w_mod"][1], w["b_mod"][1][None], name="mod_fwd1")
    x2a, h1b = node_fwd(cfg, x1a, foa, mt[0], 5, n1[1:2], mt[1], 0, name="node1a")
    z = mm(h1b, w["hg_w_in"][0], name="hg_in")
    e0, e1 = w["hg_lower_bounds"][:, 0, :], w["hg_lower_bounds"][:, 1, :]
    lb = hg_lb_fwd(e0, e1, name="hg_lb")
    gw = w["hg_gnorm_w"]
    o0, sin0 = hg_scan_fwd(cfg, z, lb[0:1], d_dir=0, name="hg_scan0")
    o1, sin1 = hg_scan_fwd(cfg, z, lb[1:2], d_dir=1, name="hg_scan1")
    onb = hg_read_fwd(o0, o1, z, gw, name="hg_read")
    y1b = mm(onb, w["hg_w_out"][0], name="hg_out")
    x1b, h2b = node_fwd(cfg, x2a, y1b, mt[1], 2, n2[1:2], mt[1], 3, name="node1b")
    ufb, actb, fob = ffn_fwd(1, h2b)
    loss_p, dx2b, dfob, dg2_1, dfnw = final_node(cfg, x1b, fob, mt[1], 5, w["final_norm_w"][None], tgt, name="final_node")

    gr = {}
    dh2b, dwu1, dcw1, dcb1, dwd1 = ffn_bwd(1, dfob, ufb, actb, h2b)
    dx1b, dy1b, dn2_1, dsh2_1, dsc2_1, dg1_1 = node_bwd(cfg, dx2b, dh2b, x1b, y1b, mt[1], 2, n2[1:2], mt[1], 3, name="node1b_bwd")
    don = mm(dy1b, w["hg_w_out"][0], tb=True, name="hg_out_dx")
    gr["hg_w_out"] = mm(onb, dy1b, ta=True, name="hg_out_dw")[None]
    do_, dgate_, dgw = hg_read_bwd(don, o0, o1, z, gw, name="hg_read_bwd")
    dq, dv, dxf, dlb0 = hg_scan_bwd(cfg, do_, z, lb[0:1], sin0, None, None, d_dir=0, name="hg_scan_bwd0")
    dq, dv, dxb, dlb1 = hg_scan_bwd(cfg, do_, z, lb[1:2], sin1, dq, dv, d_dir=1, name="hg_scan_bwd1")
    dz = [dq, dv, dxf, dxb, dgate_]
    dh1b = mm_cat_nt(dz, w["hg_w_in"][0], name="hg_in_dx")
    gr["hg_w_in"] = mm_tn_cat(h1b, dz, name="hg_in_dw")[None]
    de0, de1 = hg_lb_bwd(e0, e1, jnp.concatenate([dlb0, dlb1], axis=0), name="hg_lb_bwd")
    gr["hg_lower_bounds"] = jnp.stack([de0, de1], axis=1)
    gr["hg_gnorm_w"] = dgw
    dx2a, dfoa, dn1_1, dsh1_1, dsc1_1, dg2_0 = node_bwd(cfg, dx1b, dh1b, x2a, foa, mt[0], 5, n1[1:2], mt[1], 0, name="node1a_bwd")
    dmt1 = jnp.concatenate([dsh1_1, dsc1_1, dg1_1, dsh2_1, dsc2_1, dg2_1], axis=1)
    red1, tok1 = _reduce_start([("hg_w_in", gr["hg_w_in"], 2), ("hg_w_out", gr["hg_w_out"], 1), ("ffn_w_up1", dwu1[None], 2),
                                ("ffn_w_down1", dwd1[None], 1), ("w_mod1", mm(scb, dmt1, ta=True, name="mod_dw1")[None], 2)], core, "layer1")

    dh2a, dwu0, dcw0, dcb0, dwd0 = ffn_bwd(0, dfoa, ufa, acta, h2a, zero=tok1[0:1, 0:1])
    red2, tok2 = _reduce_start([("ffn_w_up0", dwu0[None], 2), ("ffn_w_down0", dwd0[None], 1)], core, "ffn0")
    dx1a, dy1a, dn2_0, dsh2_0, dsc2_0, dg1_0 = node_bwd(cfg, dx2a, dh2a, x1a, y1a, mt[0], 2, n2[0:1] + tok2[0:1, 0:1], mt[0], 3,
                                                        name="node0b_bwd")
    dz2 = mm(dy1a, w["s5_w_out"][0], tb=True, name="s5_out_dx")
    gr["s5_w_out"] = mm(z2, dy1a, ta=True, name="s5_out_dw")[None]

    def glu_b_bwd(dz2_, yp, t):
        zg, sg = _gelu(yp), jax.nn.sigmoid(t)
        return dz2_ * zg * sg * (1.0 - sg), dz2_ * sg

    dtg, dzg_dir = rowmap(glu_b_bwd, [dz2, ypre, tg], [], [(d, MXU), (d, F32)], name="s5_glu_b_bwd")
    dzg_mm = mm(dtg, w["s5_w_glu"][0], tb=True, name="s5_glu_dx")
    gr["s5_w_glu"] = mm(zgb, dtg, ta=True, name="s5_glu_dw")[None]

    def glu_a_bwd(dzd, dzm, yp, u, ds):
        _, vjp = jax.vjp(_gelu, yp)
        (dy,) = vjp(dzd + dzm)
        return dy, dy * ds, jnp.sum(dy * u, axis=0, keepdims=True)

    dyb, du, ddsk = rowmap(glu_a_bwd, [dzg_dir, dzg_mm, ypre, u0], [dsk], [(d, MXU), (d, F32)], [(1, d)], name="s5_glu_a_bwd")
    gr["s5_d"] = ddsk
    dar, dai, dbr, dbi, dcr, dci = [], [], [], [], [], []
    for dd in range(2):
        sre, sim, ere, eim = s5s[dd][:4]
        du, gre, gim, da_r, da_i = s5_scan_bwd(cfg, dyb, *s5s[dd], du, rev=dd == 1, name=f"s5_scan_bwd{dd}")
        dar.append(colsum(da_r, name=f"s5_da_re{dd}").reshape(g_, p_))
        dai.append(colsum(da_i, name=f"s5_da_im{dd}").reshape(g_, p_))
        dbr.append(_diag_b(blockdiag_tn(u0, gre, S5_KIN, S5_KST, name=f"s5_db_re{dd}"), kb))
        dbi.append(_diag_b(blockdiag_tn(u0, gim, S5_KIN, S5_KST, name=f"s5_db_im{dd}"), kb))
        dcr.append(_diag_c(blockdiag_tn(sre.reshape(-1, ns), dyb, S5_KST, S5_KIN, name=f"s5_dc_re{dd}"), kb))
        dci.append(_diag_c(blockdiag_tn(sim.reshape(-1, ns), dyb, S5_KST, S5_KIN, scale=-1.0, name=f"s5_dc_im{dd}"), kb))
    cat = lambda l_: jnp.concatenate(l_, axis=0)
    dlr, dli, dls, dbre, dbim = s5_disc_bwd(*s5p, cat(dar), cat(dai), cat(dbr), cat(dbi), name="s5_disc_bwd")
    gr["s5_lam_re"], gr["s5_lam_im"] = dlr.reshape(1, 2, g_, p_), dli.reshape(1, 2, g_, p_)
    gr["s5_log_step"] = dls.reshape(1, 2, g_)
    gr["s5_b_re"] = dbre.reshape(1, 2, g_, S5_GROUP, p_).transpose(0, 1, 2, 4, 3)
    gr["s5_b_im"] = dbim.reshape(1, 2, g_, S5_GROUP, p_).transpose(0, 1, 2, 4, 3)
    gr["s5_c_re"], gr["s5_c_im"] = jnp.stack(dcr)[None], jnp.stack(dci)[None]
    dh1 = mm(du, w["s5_w_in"][0], tb=True, name="s5_in_dx")
    gr["s5_w_in"] = mm(h1, du, ta=True, name="s5_in_dw")[None]
    dx0, _, dn1_0, dsh1_0, dsc1_0, _ = node_bwd(cfg, dx1a, dh1, x0, None, None, 0, n1[0:1], mt[0], 0, name="node0a_bwd")

    dmt = [jnp.concatenate([dsh1_0, dsc1_0, dg1_0, dsh2_0, dsc2_0, dg2_0], axis=1), dmt1]
    gr["b_mod"] = jnp.concatenate([colsum(dmt[l], name=f"mod_db{l}") for l in range(2)], axis=0)
    dsc16 = [mm(dmt[l], w["w_mod"][l], tb=True, name=f"mod_dx{l}") for l in range(2)]
    gr["c_ctx"] = cctx_grad(c16, dsc16, name="c_ctx_grad")[0]
    gr["norm1_w"] = jnp.concatenate([dn1_0, dn1_1], axis=0)
    gr["norm2_w"] = jnp.concatenate([dn2_0, dn2_1], axis=0)
    gr["final_norm_w"] = dfnw[0]
    gr["ffn_conv_w"], gr["ffn_conv_b"] = jnp.stack([dcw0, dcw1]), jnp.stack([dcb0, dcb1])

    last = [(n, gr[n], SHARD_AXIS[n]) for n in ("s5_w_in", "s5_w_glu", "s5_w_out", "hg_lower_bounds", "ffn_conv_w")]
    last.append(("w_mod0", mm(scb, dmt[0], ta=True, name="mod_dw0")[None], 2))
    small = [n for n in WEIGHTS if n not in SHARD_AXIS]
    last.append(("small", _pack_small(gr, small), 1))
    red3, tok3 = _reduce_start(last, core, "last")
    red = _reduce_finish(red1, core, [tok3], "layer1")
    red.update(_reduce_finish(red2, core, [tok3], "ffn0"))
    red["ffn_w_up"] = jnp.concatenate([red["ffn_w_up0"], red["ffn_w_up1"]], axis=0)
    red["ffn_w_down"] = jnp.concatenate([red["ffn_w_down0"], red["ffn_w_down1"]], axis=0)
    early = ("hg_w_in", "hg_w_out", "ffn_w_up", "ffn_w_down")
    upd = {n: adamw(a[n], red[n], a["m_" + n], a["v_" + n], name="adamw_" + n) for n in early}
    grad_x = dx0[rc:].reshape(seq, nb, d).transpose(1, 0, 2)
    red.update(_reduce_finish(red3, core, [upd[n][0] for n in early] + [grad_x], "last"))
    red.update(_unpack_small(a, red["small"], small))
    red["w_mod"] = jnp.concatenate([red["w_mod0"], red["w_mod1"]], axis=0)
    loss = lax.psum(loss_p[0, 0], ("x", "y", "c"))
    upd.update({n: adamw(a[n], red[n], a["m_" + n], a["v_" + n], name="adamw_" + n) for n in WEIGHTS if n not in early})
    return (loss, grad_x, *[red[n] for n in WEIGHTS], *[upd[n][0] for n in WEIGHTS], *[upd[n][1] for n in WEIGHTS],
            *[upd[n][2] for n in WEIGHTS])
```

```python
import functools
import math

import jax
import jax.numpy as jnp
from jax import lax
from jax.experimental import pallas as pl
from jax.experimental.pallas import tpu as pltpu

F32 = jnp.float32
BF = jnp.bfloat16
MXU = jnp.bfloat16

NORM_EPS = 1e-6
GRID_W = 64
N_MOD = 6
S5_GROUP = 16
S5_STATE = 64
S5_LAM_RE_MAX = -1e-4
S5_KIN = 256
S5_KST = S5_KIN // S5_GROUP * S5_STATE
HEAD = 128
CHUNK_ROWS = 128
N_PROJ = 5
NB = 4
ADAM_LR, ADAM_B1, ADAM_B2, ADAM_EPS, ADAM_WD, ADAM_STEP = 0.001, 0.9, 0.999, 1e-08, 0.01, 10
VMEM_LIMIT = 56 * 1024 * 1024
MESH = pl.DeviceIdType.MESH


def _tile(n, cap):
    if n <= cap:
        return n
    best = None
    for t in range(128, cap + 1, 128):
        if n % t == 0:
            best = t
    assert best is not None, (n, cap)
    return best


def _row_tile(r, width=1024):
    cap = max(8, (512 * 1024) // max(width, 1))
    return next((t for t in (512, 256, 128, 64, 32, 16, 8) if t <= cap and r % t == 0), r)


def _cp(sem):
    return pltpu.CompilerParams(dimension_semantics=sem, vmem_limit_bytes=VMEM_LIMIT)


def _dot(a, b, ca=1, cb=0):
    return lax.dot_general(a.astype(MXU), b.astype(MXU), (((ca,), (cb,)), ((), ())), preferred_element_type=F32)


def _dot3(m, x):
    hi = x.astype(MXU)
    lo = (x - hi.astype(F32)).astype(MXU)
    return _dot(m, hi) + _dot(m, lo)


def mm(a, b, *, ta=False, tb=False, out_dtype=F32, name):
    (kd, m) = a.shape if ta else a.shape[::-1]
    (n, kd2) = b.shape if tb else b.shape[::-1]
    assert kd == kd2, (a.shape, b.shape, ta, tb)
    tm, tn, tk = _tile(m, 1024), _tile(n, 1536), _tile(kd, 1024)
    nk = kd // tk

    def body(a_ref, b_ref, o_ref, acc_ref):
        k = pl.program_id(2)

        @pl.when(k == 0)
        def _():
            acc_ref[...] = jnp.zeros_like(acc_ref)

        acc_ref[...] += _dot(a_ref[...], b_ref[...], 0 if ta else 1, 1 if tb else 0)

        @pl.when(k == nk - 1)
        def _():
            o_ref[...] = acc_ref[...].astype(out_dtype)

    a_spec = pl.BlockSpec((tk, tm), lambda i, j, k: (k, i)) if ta else pl.BlockSpec((tm, tk), lambda i, j, k: (i, k))
    b_spec = pl.BlockSpec((tn, tk), lambda i, j, k: (j, k)) if tb else pl.BlockSpec((tk, tn), lambda i, j, k: (k, j))
    return pl.pallas_call(
        body, name=name, grid=(m // tm, n // tn, nk), in_specs=[a_spec, b_spec],
        out_specs=pl.BlockSpec((tm, tn), lambda i, j, k: (i, j)), out_shape=jax.ShapeDtypeStruct((m, n), out_dtype),
        scratch_shapes=[pltpu.VMEM((tm, tn), F32)], compiler_params=_cp(("parallel", "parallel", "arbitrary")))(a, b)


def mm_cat_nt(parts, b, *, name):
    m, wd = parts[0].shape
    n = b.shape[0]
    np_ = len(parts)
    tm, tn = _tile(m, 1024), _tile(n, 1024)

    def body(*refs):
        b_ref, o_ref, acc_ref = refs[np_], refs[np_ + 1], refs[np_ + 2]
        k = pl.program_id(2)

        @pl.when(k == 0)
        def _():
            acc_ref[...] = jnp.zeros_like(acc_ref)

        for p in range(np_):
            @pl.when(k == p)
            def _(p=p):
                acc_ref[...] += _dot(refs[p][...], b_ref[...], 1, 1)

        @pl.when(k == np_ - 1)
        def _():
            o_ref[...] = acc_ref[...]

    return pl.pallas_call(
        body, name=name, grid=(m // tm, n // tn, np_),
        in_specs=[pl.BlockSpec((tm, wd), lambda i, j, k: (i, 0))] * np_ + [pl.BlockSpec((tn, wd), lambda i, j, k: (j, k))],
        out_specs=pl.BlockSpec((tm, tn), lambda i, j, k: (i, j)), out_shape=jax.ShapeDtypeStruct((m, n), F32),
        scratch_shapes=[pltpu.VMEM((tm, tn), F32)], compiler_params=_cp(("parallel", "parallel", "arbitrary")))(*parts, b)


def mm_tn_cat(a, parts, *, name):
    kd, m = a.shape
    wd = parts[0].shape[1]
    np_ = len(parts)
    tm, tk = _tile(m, 1024), _tile(kd, 1024)
    nk = kd // tk

    def body(*refs):
        a_ref, o_ref, acc_ref = refs[0], refs[np_ + 1], refs[np_ + 2]
        j, k = pl.program_id(1), pl.program_id(2)

        @pl.when(k == 0)
        def _():
            acc_ref[...] = jnp.zeros_like(acc_ref)

        for p in range(np_):
            @pl.when(j == p)
            def _(p=p):
                acc_ref[...] += _dot(a_ref[...], refs[1 + p][...], 0, 0)

        @pl.when(k == nk - 1)
        def _():
            o_ref[...] = acc_ref[...]

    part_spec = lambda p: pl.BlockSpec((tk, wd), lambda i, j, k: (jnp.where(j == p, k, 0), 0))
    return pl.pallas_call(
        body, name=name, grid=(m // tm, np_, nk), in_specs=[pl.BlockSpec((tk, tm), lambda i, j, k: (k, i))] + [part_spec(p) for p in range(np_)],
        out_specs=pl.BlockSpec((tm, wd), lambda i, j, k: (i, j)), out_shape=jax.ShapeDtypeStruct((m, np_ * wd), F32),
        scratch_shapes=[pltpu.VMEM((tm, wd), F32)], compiler_params=_cp(("parallel", "parallel", "arbitrary")))(a, *parts)


def blockdiag_tn(a, b, wa, wb, *, scale=1.0, name):
    rows = a.shape[0]
    kb = a.shape[1] // wa
    tr = _tile(rows, 1024)
    nr = rows // tr

    def body(a_ref, b_ref, o_ref):
        i = pl.program_id(1)

        @pl.when(i == 0)
        def _():
            o_ref[...] = jnp.zeros_like(o_ref)

        o_ref[0] += scale * _dot(a_ref[...], b_ref[...], 0, 0)

    return pl.pallas_call(
        body, name=name, grid=(kb, nr),
        in_specs=[pl.BlockSpec((tr, wa), lambda k, i: (i, k)), pl.BlockSpec((tr, wb), lambda k, i: (i, k))],
        out_specs=pl.BlockSpec((1, wa, wb), lambda k, i: (k, 0, 0)), out_shape=jax.ShapeDtypeStruct((kb, wa, wb), F32),
        compiler_params=_cp(("parallel", "arbitrary")))(a, b)


def _pat(v, p, op):
    tm, d = v.shape
    return op(v.reshape(tm // 8, 8, d), p[None]).reshape(tm, d)


def _norm_mod(x, nw, shift, scale):
    y = x * lax.rsqrt(jnp.mean(x * x, axis=-1, keepdims=True) + NORM_EPS) * nw
    return _pat(_pat(y, 1.0 + scale, jnp.multiply), shift, jnp.add)


def _mt_spec(d, nct):
    return pl.BlockSpec((8, N_MOD * d), lambda i: (jnp.where(i < nct, 0, 1), 0))


def _acc_spec(d, nct):
    return pl.BlockSpec((8, d), lambda i: (jnp.where(i < nct, 0, 1), 0))


def _rows(cfg):
    tm = min(512, cfg["rc"])
    return tm, cfg["rc"] // tm


def node_fwd(cfg, xp, y, mtg, gi, nw, mtn, si, *, name):
    r, d = xp.shape
    tm, nct = _rows(cfg)
    row = pl.BlockSpec((tm, d), lambda i: (i, 0))
    vec = pl.BlockSpec((1, d), lambda i: (0, 0))

    def body(*refs):
        if y is None:
            xp_ref, nw_ref, mtn_ref, h_ref = refs
            x = xp_ref[...]
        else:
            xp_ref, y_ref, mtg_ref, nw_ref, mtn_ref, xn_ref, h_ref = refs
            x = xp_ref[...] + _pat(y_ref[...], mtg_ref[:, gi * d:(gi + 1) * d], jnp.multiply)
            xn_ref[...] = x
        h_ref[...] = _norm_mod(x, nw_ref[...], mtn_ref[:, si * d:(si + 1) * d], mtn_ref[:, (si + 1) * d:(si + 2) * d]).astype(MXU)

    h_shape = jax.ShapeDtypeStruct((r, d), MXU)
    if y is None:
        h = pl.pallas_call(body, name=name, grid=(r // tm,), in_specs=[row, vec, _mt_spec(d, nct)], out_specs=row,
                           out_shape=h_shape, compiler_params=_cp(("parallel",)))(xp, nw, mtn)
        return xp, h
    return pl.pallas_call(body, name=name, grid=(r // tm,), in_specs=[row, row, _mt_spec(d, nct), vec, _mt_spec(d, nct)],
                          out_specs=(row, row), out_shape=(jax.ShapeDtypeStruct((r, d), F32), h_shape),
                          compiler_params=_cp(("parallel",)))(xp, y, mtg, nw, mtn)


def node_bwd(cfg, dxres, dh, xn, y, mtg, gi, nw, mtn, si, *, name):
    r, d = xn.shape
    tm, nct = _rows(cfg)
    row = pl.BlockSpec((tm, d), lambda i: (i, 0))
    vec = pl.BlockSpec((1, d), lambda i: (0, 0))
    has_y = y is not None

    def body(*refs):
        if has_y:
            dxres_ref, dh_ref, xn_ref, y_ref, mtg_ref, nw_ref, mtn_ref, dxn_ref, dy_ref, dnw_ref, dsh_ref, dsc_ref, dg_ref = refs
        else:
            dxres_ref, dh_ref, xn_ref, nw_ref, mtn_ref, dxn_ref, dnw_ref, dsh_ref, dsc_ref = refs
        i = pl.program_id(0)
        _, vjp = jax.vjp(_norm_mod, xn_ref[...], nw_ref[...], mtn_ref[:, si * d:(si + 1) * d], mtn_ref[:, (si + 1) * d:(si + 2) * d])
        dx, dnw, dsh, dsc = vjp(dh_ref[...])
        dx = dx + dxres_ref[...]
        dxn_ref[...] = dx

        @pl.when(i == 0)
        def _():
            dnw_ref[...] = jnp.zeros_like(dnw_ref)

        @pl.when((i == 0) | (i == nct))
        def _():
            dsh_ref[...] = jnp.zeros_like(dsh_ref)
            dsc_ref[...] = jnp.zeros_like(dsc_ref)
            if has_y:
                dg_ref[...] = jnp.zeros_like(dg_ref)

        dnw_ref[...] += dnw
        dsh_ref[...] += dsh
        dsc_ref[...] += dsc
        if has_y:
            dy_ref[...] = _pat(dx, mtg_ref[:, gi * d:(gi + 1) * d], jnp.multiply).astype(MXU)
            dg_ref[...] += jnp.sum((dx * y_ref[...]).reshape(tm // 8, 8, d), axis=0)

    acc = jax.ShapeDtypeStruct((16, d), F32)
    xs = jax.ShapeDtypeStruct((r, d), F32)
    if has_y:
        return pl.pallas_call(
            body, name=name, grid=(r // tm,), in_specs=[row, row, row, row, _mt_spec(d, nct), vec, _mt_spec(d, nct)],
            out_specs=(row, row, vec, _acc_spec(d, nct), _acc_spec(d, nct), _acc_spec(d, nct)),
            out_shape=(xs, jax.ShapeDtypeStruct((r, d), MXU), jax.ShapeDtypeStruct((1, d), F32), acc, acc, acc),
            compiler_params=_cp(("arbitrary",)))(dxres, dh, xn, y, mtg, nw, mtn)
    dxn, dnw, dsh, dsc = pl.pallas_call(
        body, name=name, grid=(r // tm,), in_specs=[row, row, row, vec, _mt_spec(d, nct)],
        out_specs=(row, vec, _acc_spec(d, nct), _acc_spec(d, nct)),
        out_shape=(xs, jax.ShapeDtypeStruct((1, d), F32), acc, acc), compiler_params=_cp(("arbitrary",)))(dxres, dh, xn, nw, mtn)
    return dxn, None, dnw, dsh, dsc, None


def final_node(cfg, xp, y, mtg, gi, fnw, tgt, *, name):
    r, d = xp.shape
    tm, nct = _rows(cfg)
    row = pl.BlockSpec((tm, d), lambda i: (i, 0))
    vec = pl.BlockSpec((1, d), lambda i: (0, 0))

    def norm(x, w):
        return x * lax.rsqrt(jnp.mean(x * x, axis=-1, keepdims=True) + NORM_EPS) * w

    def body(xp_ref, y_ref, mtg_ref, fnw_ref, tgt_ref, loss_ref, dx_ref, dy_ref, dg_ref, dfnw_ref):
        i = pl.program_id(0)
        g = mtg_ref[:, gi * d:(gi + 1) * d]
        x = xp_ref[...] + _pat(y_ref[...], g, jnp.multiply)
        out, vjp = jax.vjp(norm, x, fnw_ref[...])
        lat = i >= nct
        err = jnp.where(lat, out - tgt_ref[...], 0.0)
        dx, dfnw = vjp(err * (1.0 / d))

        @pl.when(i == 0)
        def _():
            loss_ref[...] = jnp.zeros_like(loss_ref)
            dfnw_ref[...] = jnp.zeros_like(dfnw_ref)

        @pl.when((i == 0) | (i == nct))
        def _():
            dg_ref[...] = jnp.zeros_like(dg_ref)

        loss_ref[...] += jnp.full(loss_ref.shape, 0.5 / d * jnp.sum(err * err), F32)
        dfnw_ref[...] += dfnw
        dx_ref[...] = dx
        dy_ref[...] = _pat(dx, g, jnp.multiply).astype(MXU)
        dg_ref[...] += jnp.sum((dx * y_ref[...]).reshape(tm // 8, 8, d), axis=0)

    return pl.pallas_call(
        body, name=name, grid=(r // tm,),
        in_specs=[row, row, _mt_spec(d, nct), vec, pl.BlockSpec((tm, d), lambda i: (jnp.maximum(i - nct, 0), 0))],
        out_specs=(pl.BlockSpec((8, 128), lambda i: (0, 0)), row, row, _acc_spec(d, nct), vec),
        out_shape=(jax.ShapeDtypeStruct((8, 128), F32), jax.ShapeDtypeStruct((r, d), F32), jax.ShapeDtypeStruct((r, d), MXU),
                   jax.ShapeDtypeStruct((16, d), F32), jax.ShapeDtypeStruct((1, d), F32)),
        compiler_params=_cp(("arbitrary",)))(xp, y, mtg, fnw, tgt)


def _silu(x):
    return x * jax.nn.sigmoid(x)


def mod_fwd(c16, w, b, *, name):
    d, n = w.shape
    tn = _tile(n, 1536)

    def body(c_ref, w_ref, b_ref, o_ref, s_ref):
        s = _silu(c_ref[...])
        s_ref[...] = s.astype(MXU)
        o_ref[...] = _dot(s, w_ref[...]) + b_ref[...]

    return pl.pallas_call(
        body, name=name, grid=(n // tn,),
        in_specs=[pl.BlockSpec((16, d), lambda j: (0, 0)), pl.BlockSpec((d, tn), lambda j: (0, j)), pl.BlockSpec((1, tn), lambda j: (0, j))],
        out_specs=(pl.BlockSpec((16, tn), lambda j: (0, j)), pl.BlockSpec((16, d), lambda j: (0, 0))),
        out_shape=(jax.ShapeDtypeStruct((16, n), F32), jax.ShapeDtypeStruct((16, d), MXU)),
        compiler_params=_cp(("arbitrary",)))(c16, w, b)


def colsum(x, *, name):
    def body(x_ref, o_ref):
        o_ref[...] = jnp.sum(x_ref[...], axis=0, keepdims=True)

    return pl.pallas_call(body, name=name, out_shape=jax.ShapeDtypeStruct((1, x.shape[1]), F32))(x)


def cctx_grad(c16, ds_list, *, name):
    def body(c_ref, *refs):
        o_ref = refs[-1]
        ds = refs[0][...]
        for r_ in refs[1:-1]:
            ds = ds + r_[...]
        _, vjp = jax.vjp(_silu, c_ref[...])
        (dc,) = vjp(ds)
        o_ref[...] = jnp.sum(dc[0:8], axis=0, keepdims=True)

    return pl.pallas_call(body, name=name, out_shape=jax.ShapeDtypeStruct((1, c16.shape[1]), F32))(c16, *ds_list)


def _s5_disc(lam_re, lam_im, log_step, b_re, b_im):
    lr = jnp.minimum(lam_re, S5_LAM_RE_MAX)
    li = lam_im
    dt = jnp.exp(log_step)
    mag = jnp.exp(lr * dt)
    abar_r = mag * jnp.cos(li * dt)
    abar_i = mag * jnp.sin(li * dt)
    den = lr * lr + li * li
    nr = abar_r - 1.0
    coef_r = (nr * lr + abar_i * li) / den
    coef_i = (abar_i * lr - nr * li) / den
    bbar_r = coef_r[:, None, :] * b_re - coef_i[:, None, :] * b_im
    bbar_i = coef_r[:, None, :] * b_im + coef_i[:, None, :] * b_re
    return abar_r, abar_i, bbar_r, bbar_i


def s5_disc_fwd(lam_re, lam_im, log_step, b_re, b_im, *, name):
    def body(lr, li, ls, br, bi, ar_o, ai_o, br_o, bi_o):
        ar_o[...], ai_o[...], br_o[...], bi_o[...] = _s5_disc(lr[...], li[...], ls[...], br[...], bi[...])

    s2, s3 = jax.ShapeDtypeStruct(lam_re.shape, F32), jax.ShapeDtypeStruct(b_re.shape, F32)
    return pl.pallas_call(body, name=name, out_shape=(s2, s2, s3, s3))(lam_re, lam_im, log_step, b_re, b_im)


def s5_disc_bwd(lam_re, lam_im, log_step, b_re, b_im, d_ar, d_ai, d_br, d_bi, *, name):
    def body(lr, li, ls, br, bi, dar, dai, dbr, dbi, o_lr, o_li, o_ls, o_br, o_bi):
        _, vjp = jax.vjp(_s5_disc, lr[...], li[...], ls[...], br[...], bi[...])
        o_lr[...], o_li[...], o_ls[...], o_br[...], o_bi[...] = vjp((dar[...], dai[...], dbr[...], dbi[...]))

    s2, s3 = jax.ShapeDtypeStruct(lam_re.shape, F32), jax.ShapeDtypeStruct(b_re.shape, F32)
    return pl.pallas_call(body, name=name, out_shape=(s2, s2, jax.ShapeDtypeStruct(log_step.shape, F32), s3, s3))(
        lam_re, lam_im, log_step, b_re, b_im, d_ar, d_ai, d_br, d_bi)


S5_LANES = 512


def _chunk_order(k, ncc, nch, rev):
    if not rev:
        return k
    return jnp.where(k < ncc, ncc - 1 - k, nch - 1 - (k - ncc))


def _cmul(ar, ai, xr, xi):
    return ar * xr - ai * xi, ar * xi + ai * xr


S5_FWD_ROWS = 256
S5_BWD_ROWS = 256


def _const_spec(a):
    return pl.BlockSpec(a.shape, lambda k: (0,) * a.ndim, pipeline_mode=pl.Buffered(1))


def _shift_steps(x, edge_tile, back):
    n = x.shape[0]
    row = lax.broadcasted_iota(jnp.int32, (8, x.shape[1]), 0)
    edge = pltpu.roll(edge_tile, 4, 0)
    if back:
        y = pltpu.roll(x, 4, 0)
        return jnp.concatenate([jnp.where(row < 4, edge, y[0:8]), y[8:]], axis=0)
    y = pltpu.roll(x, n - 4, 0)
    return jnp.concatenate([y[:n - 8], jnp.where(row >= 4, edge, y[n - 8:])], axis=0)


def s5_scan_fwd(cfg, u, a2_re, a2_im, bre, bim, abre, abim, cre, cim, *, rev, name):
    r, d = u.shape
    ns = a2_re.shape[1]
    kb = d // S5_KIN
    tcr = S5_FWD_ROWS
    n8 = tcr // 8
    q = S5_FWD_ROWS // S5_BWD_ROWS
    seg = n8 // q
    nch, ncc = r // tcr, cfg["rc"] // tcr
    lw = min(S5_LANES, ns)

    def body(u_ref, ar_ref, ai_ref, bre_ref, bim_ref, abre_ref, abim_ref, cre_ref, cim_ref, sre_ref, sim_ref, ere_ref, eim_ref, y_ref,
             st_re, st_im, u_edge):
        @pl.when(pl.program_id(0) == 0)
        def _():
            st_re[...] = jnp.zeros_like(st_re)
            st_im[...] = jnp.zeros_like(st_im)
            u_edge[...] = jnp.zeros_like(u_edge)

        u_ = u_ref[...]
        ub = u_.astype(MXU)
        upb = _shift_steps(u_, u_edge[...], back=not rev).astype(MXU)
        u_edge[...] = u_[0:8] if rev else u_[tcr - 8:tcr]
        for j in range(kb):
            uj, upj = ub[:, j * S5_KIN:(j + 1) * S5_KIN], upb[:, j * S5_KIN:(j + 1) * S5_KIN]
            sre_ref[:, :, j * S5_KST:(j + 1) * S5_KST] = (_dot(uj, bre_ref[j]) + _dot(upj, abre_ref[j])).reshape(n8, 8, S5_KST)
            sim_ref[:, :, j * S5_KST:(j + 1) * S5_KST] = (_dot(uj, bim_ref[j]) + _dot(upj, abim_ref[j])).reshape(n8, 8, S5_KST)
        for c in range(ns // lw):
            sl = slice(c * lw, (c + 1) * lw)
            ar = jnp.broadcast_to(ar_ref[:, sl], (8, lw))
            ai = jnp.broadcast_to(ai_ref[:, sl], (8, lw))

            def step(i, carry, sl=sl, ar=ar, ai=ai):
                sr, si = carry
                ii = n8 - 1 - i if rev else i
                pr, pi = _cmul(ar, ai, sr, si)
                sr, si = pr + sre_ref[ii, :, sl], pi + sim_ref[ii, :, sl]
                sre_ref[ii, :, sl] = sr
                sim_ref[ii, :, sl] = si
                return sr, si

            sr, si = st_re[:, sl], st_im[:, sl]
            for s_ in range(q):
                at = q - 1 - s_ if rev else s_
                ere_ref[at, :, sl] = sr
                eim_ref[at, :, sl] = si
                sr, si = lax.fori_loop(s_ * seg, (s_ + 1) * seg, step, (sr, si))
            st_re[:, sl] = sr
            st_im[:, sl] = si
        for j in range(kb):
            sr = sre_ref[:, :, j * S5_KST:(j + 1) * S5_KST].reshape(tcr, S5_KST)
            si = sim_ref[:, :, j * S5_KST:(j + 1) * S5_KST].reshape(tcr, S5_KST)
            y_ref[:, j * S5_KIN:(j + 1) * S5_KIN] = _dot(sr, cre_ref[j]) - _dot(si, cim_ref[j])

    cidx = functools.partial(_chunk_order, ncc=ncc, nch=nch, rev=rev)
    full = _const_spec
    st = pl.BlockSpec((n8, 8, ns), lambda k: (cidx(k), 0, 0))
    en = pl.BlockSpec((q, 8, ns), lambda k: (cidx(k), 0, 0))
    return pl.pallas_call(
        body, name=name, grid=(nch,),
        in_specs=[pl.BlockSpec((tcr, d), lambda k: (cidx(k), 0)), full(a2_re), full(a2_im), full(bre), full(bim), full(abre), full(abim),
                  full(cre), full(cim)],
        out_specs=(st, st, en, en, pl.BlockSpec((tcr, d), lambda k: (cidx(k), 0))),
        out_shape=(jax.ShapeDtypeStruct((r // 8, 8, ns), F32),) * 2 + (jax.ShapeDtypeStruct((q * nch, 8, ns), F32),) * 2
        + (jax.ShapeDtypeStruct((r, d), F32),),
        scratch_shapes=[pltpu.VMEM((8, ns), F32), pltpu.VMEM((8, ns), F32), pltpu.VMEM((8, d), F32)],
        compiler_params=_cp(("arbitrary",)))(u, a2_re, a2_im, bre, bim, abre, abim, cre, cim)


def s5_scan_bwd(cfg, dyb, sre, sim, ere, eim, a2_re, a2_im, bre, bim, cre, cim, c2re, c2im, du_in, *, rev, name):
    r, d = dyb.shape
    ns = a2_re.shape[1]
    kb = d // S5_KIN
    tcr = S5_BWD_ROWS
    n8 = tcr // 8
    nch, ncc = r // tcr, cfg["rc"] // tcr
    lw = min(S5_LANES, ns)

    def body(dy_ref, sre_ref, sim_ref, ere_ref, eim_ref, ar_ref, ai_ref, bre_ref, bim_ref, cre_ref, cim_ref, c2re_ref, c2im_ref, duin_ref,
             du_ref, gre_ref, gim_ref, dar_ref, dai_ref, g_re, g_im, gc_re, gc_im, dy_edge):
        k = pl.program_id(0)

        @pl.when(k == 0)
        def _():
            gc_re[...] = jnp.zeros_like(gc_re)
            gc_im[...] = jnp.zeros_like(gc_im)
            dar_ref[...] = jnp.zeros_like(dar_ref)
            dai_ref[...] = jnp.zeros_like(dai_ref)
            dy_edge[...] = jnp.zeros_like(dy_edge)

        dy32 = dy_ref[...].astype(F32)
        dy = dy32.astype(MXU)
        dyn = _shift_steps(dy32, dy_edge[...], back=rev).astype(MXU)
        dy_edge[...] = dy32[tcr - 8:tcr] if rev else dy32[0:8]
        for j in range(kb):
            dyj, dynj = dy[:, j * S5_KIN:(j + 1) * S5_KIN], dyn[:, j * S5_KIN:(j + 1) * S5_KIN]
            g_re[:, :, j * S5_KST:(j + 1) * S5_KST] = (_dot(dyj, cre_ref[j], 1, 1) + _dot(dynj, c2re_ref[j], 1, 1)).reshape(n8, 8, S5_KST)
            g_im[:, :, j * S5_KST:(j + 1) * S5_KST] = -(_dot(dyj, cim_ref[j], 1, 1) + _dot(dynj, c2im_ref[j], 1, 1)).reshape(n8, 8, S5_KST)
        first = lax.broadcasted_iota(jnp.int32, (8, lw), 0) < 4
        if rev:
            first = jnp.logical_not(first)
        for c in range(ns // lw):
            sl = slice(c * lw, (c + 1) * lw)
            ar = jnp.broadcast_to(ar_ref[:, sl], (8, lw))
            nai = -jnp.broadcast_to(ai_ref[:, sl], (8, lw))

            def step(i, carry, sl=sl, ar=ar, nai=nai):
                gr, gi, accr, acci = carry
                ii = i if rev else n8 - 1 - i
                pr, pi = _cmul(ar, nai, gr, gi)
                outr, outi = pr + g_re[ii, :, sl], pi + g_im[ii, :, sl]
                g_re[ii, :, sl] = outr
                g_im[ii, :, sl] = outi
                pv = jnp.clip(ii + 1 if rev else ii - 1, 0, n8 - 1)
                at_entry = (ii == n8 - 1) if rev else (ii == 0)
                pvr = jnp.where(at_entry, ere_ref[0, :, sl], sre_ref[pv, :, sl])
                pvi = jnp.where(at_entry, eim_ref[0, :, sl], sim_ref[pv, :, sl])
                spr = pltpu.roll(jnp.where(first, sre_ref[ii, :, sl], pvr), 4, 0)
                spi = pltpu.roll(jnp.where(first, sim_ref[ii, :, sl], pvi), 4, 0)
                accr = accr + outr * spr + outi * spi
                acci = acci + outi * spr - outr * spi
                return outr, outi, accr, acci

            gr, gi, accr, acci = lax.fori_loop(0, n8, step, (gc_re[:, sl], gc_im[:, sl], dar_ref[:, sl], dai_ref[:, sl]))
            gc_re[:, sl] = gr
            gc_im[:, sl] = gi
            dar_ref[:, sl] = accr
            dai_ref[:, sl] = acci
        for j in range(kb):
            gr = g_re[:, :, j * S5_KST:(j + 1) * S5_KST].reshape(tcr, S5_KST)
            gi = g_im[:, :, j * S5_KST:(j + 1) * S5_KST].reshape(tcr, S5_KST)
            gre_ref[:, j * S5_KST:(j + 1) * S5_KST] = gr.astype(MXU)
            gim_ref[:, j * S5_KST:(j + 1) * S5_KST] = gi.astype(MXU)
            du_ref[:, j * S5_KIN:(j + 1) * S5_KIN] = (duin_ref[:, j * S5_KIN:(j + 1) * S5_KIN]
                                                     + _dot(gr, bre_ref[j], 1, 1) + _dot(gi, bim_ref[j], 1, 1))

    def cidx(k):
        return _chunk_order(nch - 1 - k, ncc, nch, rev)

    full = _const_spec
    st = pl.BlockSpec((n8, 8, ns), lambda k: (cidx(k), 0, 0))
    en = pl.BlockSpec((1, 8, ns), lambda k: (cidx(k), 0, 0))
    rowd = pl.BlockSpec((tcr, d), lambda k: (cidx(k), 0))
    rown = pl.BlockSpec((tcr, ns), lambda k: (cidx(k), 0))
    acc = pl.BlockSpec((8, ns), lambda k: (0, 0))
    return pl.pallas_call(
        body, name=name, grid=(nch,),
        in_specs=[rowd, st, st, en, en, full(a2_re), full(a2_im), full(bre), full(bim), full(cre), full(cim), full(c2re), full(c2im), rowd],
        out_specs=(rowd, rown, rown, acc, acc),
        out_shape=(jax.ShapeDtypeStruct((r, d), F32), jax.ShapeDtypeStruct((r, ns), MXU), jax.ShapeDtypeStruct((r, ns), MXU),
                   jax.ShapeDtypeStruct((8, ns), F32), jax.ShapeDtypeStruct((8, ns), F32)),
        scratch_shapes=[pltpu.VMEM((n8, 8, ns), F32), pltpu.VMEM((n8, 8, ns), F32), pltpu.VMEM((8, ns), F32), pltpu.VMEM((8, ns), F32),
                        pltpu.VMEM((8, d), F32)],
        compiler_params=_cp(("arbitrary",)))(dyb, sre, sim, ere, eim, a2_re, a2_im, bre, bim, cre, cim, c2re, c2im, du_in)


def rowmap(fn, rows_in, vecs_in, outs, accs=(), *, name):
    r = rows_in[0].shape[0]
    tm = _row_tile(r, max(a.shape[1] for a in rows_in))
    nr, nv, no = len(rows_in), len(vecs_in), len(outs)

    def body(*refs):
        ins = [x[...] for x in refs[:nr + nv]]
        res = fn(*ins)
        if not isinstance(res, (tuple, list)):
            res = (res,)
        out_refs = refs[nr + nv:]
        for o_ref, v in zip(out_refs[:no], res[:no]):
            o_ref[...] = v.astype(o_ref.dtype)
        if accs:
            @pl.when(pl.program_id(0) == 0)
            def _():
                for a_ref in out_refs[no:]:
                    a_ref[...] = jnp.zeros_like(a_ref)
            for a_ref, v in zip(out_refs[no:], res[no:]):
                a_ref[...] += v

    in_specs = [pl.BlockSpec((tm, a.shape[1]), lambda i: (i, 0)) for a in rows_in]
    in_specs += [pl.BlockSpec(v.shape, lambda i, n=v.ndim: (0,) * n) for v in vecs_in]
    out_specs = [pl.BlockSpec((tm, w), lambda i: (i, 0)) for w, _ in outs] + [pl.BlockSpec(s, lambda i, n=len(s): (0,) * n) for s in accs]
    out_shape = [jax.ShapeDtypeStruct((r, w), dt) for w, dt in outs] + [jax.ShapeDtypeStruct(s, F32) for s in accs]
    res = pl.pallas_call(body, name=name, grid=(r // tm,), in_specs=in_specs, out_specs=tuple(out_specs), out_shape=tuple(out_shape),
                         compiler_params=_cp(("arbitrary",) if accs else ("parallel",)))(*rows_in, *vecs_in)
    return res


def _gelu(x):
    return jax.nn.gelu(x, approximate=True)


def _hg_lower_bound(e0, e1):
    m = jnp.maximum(e0, e1)
    a, b = jnp.exp(e0 - m), jnp.exp(e1 - m)
    return b / (a + b)


def _hg_gates(x, lb):
    logf = jnp.log(lb + (1.0 - lb) * jax.nn.sigmoid(x))
    return logf, (1.0 - lb) * jax.nn.sigmoid(-x)


def _hg_masks(rev):
    n = CHUNK_ROWS
    rr = lax.broadcasted_iota(jnp.int32, (n, n), 0)
    ss = lax.broadcasted_iota(jnp.int32, (n, n), 1)
    same = (rr % NB) == (ss % NB)
    causal = same & ((ss >= rr) if rev else (ss <= rr))
    anti = same & ((ss <= rr) if rev else (ss >= rr))
    end0 = 0 if rev else n - NB
    pick_end = ss == (end0 + rr % NB)
    return same, causal, anti, pick_end, end0


def _hg_expand(x):
    ex = lax.broadcasted_iota(jnp.int32, x.shape, 0) % NB
    return jnp.concatenate([jnp.where(ex == b, x, 0.0) for b in range(NB)], axis=1)


def _hg_fold(xe):
    kk = xe.shape[1] // NB
    ex = lax.broadcasted_iota(jnp.int32, (xe.shape[0], kk), 0) % NB
    out = jnp.zeros((xe.shape[0], kk), F32)
    for b in range(NB):
        out = out + jnp.where(ex == b, xe[:, b * kk:(b + 1) * kk], 0.0)
    return out


def _hg_chunk(q, v, x, lb, masks):
    same, causal, anti, pick_end, end0 = masks
    logf, kk = _hg_gates(x, lb)
    b = _dot3(causal.astype(MXU), logf)
    bend_t = _dot3(pick_end.astype(MXU), b)
    bend_flat = jnp.concatenate([b[end0 + i:end0 + i + 1] for i in range(NB)], axis=1)
    eb = jnp.exp(b)
    enb = jnp.exp(-b)
    ee = jnp.exp(bend_t - b)
    qd, kd, ke = q * eb, kk * enb, kk * ee
    att = jnp.where(causal, _dot(qd, kd, 1, 1), 0.0)
    decay = jnp.exp(bend_flat)
    return dict(same=same, causal=causal, anti=anti, logf=logf, kk=kk, b=b, eb=eb, enb=enb, ee=ee, qd=qd, kd=kd, ke=ke, att=att,
                decay=decay, qde=_hg_expand(qd), kee=_hg_expand(ke))


def _hg_chunk_order(cfg, r):
    nch, ncc = r // CHUNK_ROWS, cfg["rc"] // CHUNK_ROWS
    return nch, ncc


def hg_scan_fwd(cfg, z, lb, *, d_dir, name):
    r = z.shape[0]
    d = z.shape[1] // N_PROJ
    nh = d // HEAD
    rev = d_dir == 1
    nch, ncc = _hg_chunk_order(cfg, r)
    n = CHUNK_ROWS

    def body(q_ref, v_ref, x_ref, lb_ref, o_ref, sin_ref, stk):
        @pl.when(pl.program_id(0) == 0)
        def _():
            stk[...] = jnp.zeros_like(stk)

        masks = _hg_masks(rev)
        for h in range(nh):
            sl = slice(h * HEAD, (h + 1) * HEAD)
            s0 = stk[h]
            sin_ref[0, h] = s0
            v = v_ref[:, sl]
            c = _hg_chunk(q_ref[:, sl], v, x_ref[:, sl], lb_ref[:, sl], masks)
            o_ref[:, sl] = _dot(c["att"], v) + _dot(c["qde"], s0, 1, 1)
            stk[h] = s0 * c["decay"] + _dot(v, c["kee"], 0, 0)

    def cidx(k):
        return _chunk_order(k, ncc, nch, rev)

    blk = lambda p: pl.BlockSpec((n, d), lambda k: (cidx(k), p))
    return pl.pallas_call(
        body, name=name, grid=(nch,),
        in_specs=[blk(0), blk(1), blk(2 + d_dir), pl.BlockSpec((1, d), lambda k: (0, 0))],
        out_specs=(blk(0), pl.BlockSpec((1, nh, HEAD, NB * HEAD), lambda k: (cidx(k), 0, 0, 0))),
        out_shape=(jax.ShapeDtypeStruct((r, d), F32), jax.ShapeDtypeStruct((nch, nh, HEAD, NB * HEAD), F32)),
        scratch_shapes=[pltpu.VMEM((nh, HEAD, NB * HEAD), F32)], compiler_params=_cp(("arbitrary",)))(z, z, z, lb)


def hg_scan_bwd(cfg, do, z, lb, sin, dq_in, dv_in, *, d_dir, name):
    r = z.shape[0]
    d = z.shape[1] // N_PROJ
    nh = d // HEAD
    rev = d_dir == 1
    nch, ncc = _hg_chunk_order(cfg, r)
    n = CHUNK_ROWS
    has_in = dq_in is not None

    def body(*refs):
        if has_in:
            do_ref, q_ref, v_ref, x_ref, lb_ref, sin_ref, dqi_ref, dvi_ref, dq_ref, dv_ref, dx_ref, dlb_ref, dstk = refs
        else:
            do_ref, q_ref, v_ref, x_ref, lb_ref, sin_ref, dq_ref, dv_ref, dx_ref, dlb_ref, dstk = refs
        @pl.when(pl.program_id(0) == 0)
        def _():
            dstk[...] = jnp.zeros_like(dstk)
            dlb_ref[...] = jnp.zeros_like(dlb_ref)

        masks = _hg_masks(rev)
        ex = lax.broadcasted_iota(jnp.int32, (n, HEAD), 0) % NB
        for h in range(nh):
            sl = slice(h * HEAD, (h + 1) * HEAD)
            do_, q, v, x, lb_, s0, ds1 = do_ref[:, sl], q_ref[:, sl], v_ref[:, sl], x_ref[:, sl], lb_ref[:, sl], sin_ref[0, h], dstk[h]
            c = _hg_chunk(q, v, x, lb_, masks)
            datt = jnp.where(c["causal"], _dot(do_, v, 1, 1), 0.0)
            dv = _dot(c["att"], do_, 0, 0) + _dot(c["kee"], ds1, 1, 1)
            dqd = _dot(datt, c["kd"]) + _hg_fold(_dot(do_, s0))
            dkd = _dot(datt, c["qd"], 0, 0)
            dke = _hg_fold(_dot(v, ds1))
            dbend_flat = jnp.sum(ds1 * s0, axis=0, keepdims=True) * c["decay"]
            dstk[h] = _dot(do_, c["qde"], 0, 0) + ds1 * c["decay"]
            dq = dqd * c["eb"]
            dk = dkd * c["enb"] + dke * c["ee"]
            db = dqd * c["qd"] - dkd * c["kd"] - dke * c["ke"]
            dbend_rows = jnp.zeros((n, HEAD), F32)
            for b in range(NB):
                dbend_rows = dbend_rows + jnp.where(ex == b, dbend_flat[:, b * HEAD:(b + 1) * HEAD], 0.0)
            dlogf = _dot3(c["anti"].astype(MXU), db) + _dot3(c["same"].astype(MXU), dke * c["ke"]) + dbend_rows
            _, vjp = jax.vjp(_hg_gates, x, lb_)
            dx, dlb = vjp((dlogf, dk))
            if has_in:
                dq = dq + dqi_ref[:, sl]
                dv = dv + dvi_ref[:, sl]
            dq_ref[:, sl] = dq.astype(dq_ref.dtype)
            dv_ref[:, sl] = dv.astype(dv_ref.dtype)
            dx_ref[:, sl] = dx.astype(dx_ref.dtype)
            dlb_ref[:, sl] += dlb

    def cidx(k):
        return _chunk_order(nch - 1 - k, ncc, nch, rev)

    blk = lambda p: pl.BlockSpec((n, d), lambda k: (cidx(k), p))
    vec = pl.BlockSpec((1, d), lambda k: (0, 0))
    in_specs = [blk(0), blk(0), blk(1), blk(2 + d_dir), vec, pl.BlockSpec((1, nh, HEAD, NB * HEAD), lambda k: (cidx(k), 0, 0, 0))]
    args = [do, z, z, z, lb, sin]
    if has_in:
        in_specs += [blk(0), blk(0)]
        args += [dq_in, dv_in]
    rd = jax.ShapeDtypeStruct((r, d), MXU if has_in else F32)
    return pl.pallas_call(
        body, name=name, grid=(nch,), in_specs=in_specs, out_specs=(blk(0), blk(0), blk(0), vec),
        out_shape=(rd, rd, jax.ShapeDtypeStruct((r, d), MXU), jax.ShapeDtypeStruct((1, d), F32)),
        scratch_shapes=[pltpu.VMEM((nh, HEAD, NB * HEAD), F32)], compiler_params=_cp(("arbitrary",)))(*args)


def _hg_read(o, g, gw):
    on = o * lax.rsqrt(jnp.mean(o * o, axis=-1, keepdims=True) + NORM_EPS) * gw
    return on * jax.nn.sigmoid(g)


def hg_read_fwd(of, ob, z, gw, *, name):
    r, d = of.shape
    nh = d // HEAD
    tm = _row_tile(r)

    def body(of_ref, ob_ref, g_ref, gw_ref, o_ref):
        for h in range(nh):
            sl = slice(h * HEAD, (h + 1) * HEAD)
            o_ref[:, sl] = _hg_read(of_ref[:, sl] + ob_ref[:, sl], g_ref[:, sl], gw_ref[...]).astype(MXU)

    blk = pl.BlockSpec((tm, d), lambda i: (i, 0))
    return pl.pallas_call(
        body, name=name, grid=(r // tm,),
        in_specs=[blk, blk, pl.BlockSpec((tm, d), lambda i: (i, N_PROJ - 1)), pl.BlockSpec((1, HEAD), lambda i: (0, 0))],
        out_specs=blk, out_shape=jax.ShapeDtypeStruct((r, d), MXU), compiler_params=_cp(("parallel",)))(of, ob, z, gw)


def hg_read_bwd(don, of, ob, z, gw, *, name):
    r, d = of.shape
    nh = d // HEAD
    tm = _row_tile(r)

    def body(don_ref, of_ref, ob_ref, g_ref, gw_ref, do_ref, dg_ref, dgw_ref):
        @pl.when(pl.program_id(0) == 0)
        def _():
            dgw_ref[...] = jnp.zeros_like(dgw_ref)

        for h in range(nh):
            sl = slice(h * HEAD, (h + 1) * HEAD)
            _, vjp = jax.vjp(_hg_read, of_ref[:, sl] + ob_ref[:, sl], g_ref[:, sl], gw_ref[...])
            do_ref[:, sl], dg, dgw = vjp(don_ref[:, sl])
            dg_ref[:, sl] = dg.astype(MXU)
            dgw_ref[...] += dgw

    blk = pl.BlockSpec((tm, d), lambda i: (i, 0))
    vec = pl.BlockSpec((1, HEAD), lambda i: (0, 0))
    rd = jax.ShapeDtypeStruct((r, d), F32)
    return pl.pallas_call(
        body, name=name, grid=(r // tm,),
        in_specs=[blk, blk, blk, pl.BlockSpec((tm, d), lambda i: (i, N_PROJ - 1)), vec],
        out_specs=(blk, blk, vec), out_shape=(rd, jax.ShapeDtypeStruct((r, d), MXU), jax.ShapeDtypeStruct((1, HEAD), F32)),
        compiler_params=_cp(("arbitrary",)))(don, of, ob, z, gw)


FFN_COLS = 256


def _seg_masks(cfg, tr, i):
    t = lax.broadcasted_iota(jnp.int32, (tr, FFN_COLS), 0) // NB
    ctx_steps = cfg["rc"] // NB
    pos = jnp.where(i == 0, t % ctx_steps, t % GRID_W)
    last = jnp.where(i == 0, ctx_steps - 1, GRID_W - 1)
    return pos == 0, pos == last


def _prev(x, start):
    return jnp.where(start, 0.0, pltpu.roll(x, NB, 0))


def _next(x, end):
    return jnp.where(end, 0.0, pltpu.roll(x, x.shape[0] - NB, 0))


def _conv3(u, w, b, start, end):
    return ((b + _prev(u, start) * w[0:1]) + u * w[1:2]) + _next(u, end) * w[2:3]


def ffn_mid_fwd(cfg, u, cw, cb, *, name):
    r, f2 = u.shape
    f = f2 // 2
    tr = cfg["rc"]
    nf = f // FFN_COLS

    def body(ua_ref, ug_ref, wa_ref, wg_ref, ba_ref, bg_ref, o_ref, ca_ref, cg_ref):
        start, end = _seg_masks(cfg, tr, pl.program_id(0))
        a = _conv3(ua_ref[...].astype(F32), wa_ref[...], ba_ref[...], start, end)
        g = _conv3(ug_ref[...].astype(F32), wg_ref[...], bg_ref[...], start, end)
        ca_ref[...] = a.astype(MXU)
        cg_ref[...] = g.astype(MXU)
        o_ref[...] = (_silu(a) * g).astype(MXU)

    ca = lambda rows: pl.BlockSpec((rows, FFN_COLS), lambda i, j: (i if rows == tr else 0, j))
    cg = lambda rows: pl.BlockSpec((rows, FFN_COLS), lambda i, j: (i if rows == tr else 0, j + nf))
    half = jax.ShapeDtypeStruct((r, f), MXU)
    return pl.pallas_call(
        body, name=name, grid=(r // tr, nf), in_specs=[ca(tr), cg(tr), ca(3), cg(3), ca(1), cg(1)], out_specs=(ca(tr), ca(tr), ca(tr)),
        out_shape=(jax.ShapeDtypeStruct((r, f), MXU), half, half), compiler_params=_cp(("parallel", "parallel")))(u, u, cw, cw, cb, cb)


def ffn_mid_bwd(cfg, dact, u, ca, cg, cw, *, name):
    r, f2 = u.shape
    f = f2 // 2
    tr = cfg["rc"]
    nf = f // FFN_COLS

    def body(da_ref, us_ref, ca_ref, cg_ref, ws_ref, du_ref, dcw_ref, dcb_ref):
        i = pl.program_id(1)
        is_a = pl.program_id(0) < nf
        start, end = _seg_masks(cfg, tr, i)

        @pl.when(i == 0)
        def _():
            dcw_ref[...] = jnp.zeros_like(dcw_ref)
            dcb_ref[...] = jnp.zeros_like(dcb_ref)

        def finish(dc):
            us, ws = us_ref[...].astype(F32), ws_ref[...]
            dn, dp = _next(dc, end), _prev(dc, start)
            du_ref[...] = (ws[1:2] * dc + ws[0:1] * dn + ws[2:3] * dp).astype(MXU)
            dcw_ref[...] += jnp.concatenate([jnp.sum(dn * us, axis=0, keepdims=True), jnp.sum(dc * us, axis=0, keepdims=True),
                                             jnp.sum(dp * us, axis=0, keepdims=True)], axis=0)
            dcb_ref[...] += jnp.sum(dc, axis=0, keepdims=True)

        @pl.when(is_a)
        def _():
            cs = ca_ref[...].astype(F32)
            sg = jax.nn.sigmoid(cs)
            finish(da_ref[...].astype(F32) * cg_ref[...].astype(F32) * (sg * (1.0 + cs * (1.0 - sg))))

        @pl.when(jnp.logical_not(is_a))
        def _():
            finish(da_ref[...].astype(F32) * _silu(ca_ref[...].astype(F32)))

    cs_ = lambda rows: pl.BlockSpec((rows, FFN_COLS), lambda j, i: (i if rows == tr else 0, j))
    hf = pl.BlockSpec((tr, FFN_COLS), lambda j, i: (i, j % nf))
    gate = pl.BlockSpec((tr, FFN_COLS), lambda j, i: (jnp.where(j < nf, i, 0), jnp.where(j < nf, j, 0)))
    return pl.pallas_call(
        body, name=name, grid=(2 * nf, r // tr), in_specs=[hf, cs_(tr), hf, gate, cs_(3)], out_specs=(cs_(tr), cs_(3), cs_(1)),
        out_shape=(jax.ShapeDtypeStruct((r, f2), MXU), jax.ShapeDtypeStruct((3, f2), F32), jax.ShapeDtypeStruct((1, f2), F32)),
        compiler_params=_cp(("parallel", "arbitrary")))(dact, u, ca, cg, cw)


def hg_lb_fwd(e0, e1, *, name):
    def body(a, b, o):
        o[...] = _hg_lower_bound(a[...], b[...])

    return pl.pallas_call(body, name=name, out_shape=jax.ShapeDtypeStruct(e0.shape, F32))(e0, e1)


def hg_lb_bwd(e0, e1, dlb, *, name):
    def body(a, b, g, oa, ob):
        _, vjp = jax.vjp(_hg_lower_bound, a[...], b[...])
        oa[...], ob[...] = vjp(g[...])

    s = jax.ShapeDtypeStruct(e0.shape, F32)
    return pl.pallas_call(body, name=name, out_shape=(s, s))(e0, e1, dlb)


def _adamw(w, g, m, v):
    m = ADAM_B1 * m + (1.0 - ADAM_B1) * g
    v = ADAM_B2 * v + (1.0 - ADAM_B2) * jnp.square(g)
    m_hat = m / (1.0 - ADAM_B1 ** ADAM_STEP)
    v_hat = v / (1.0 - ADAM_B2 ** ADAM_STEP)
    delta = -ADAM_LR * (m_hat / (jnp.sqrt(v_hat) + ADAM_EPS) + ADAM_WD * w)
    return delta, m, v


def _as2d(a):
    if a.ndim >= 2 and a.shape[-1] % 128 == 0:
        return a.reshape(-1, a.shape[-1])
    return a.reshape(-1, 128) if a.size % 128 == 0 else a.reshape(1, -1)


def adamw(w, g, m, v, *, name):
    w2 = _as2d(w)
    outs = rowmap(_adamw, [w2, _as2d(g), _as2d(m), _as2d(v)], [], [(w2.shape[1], F32)] * 3, name=name)
    return tuple(o.reshape(w.shape) for o in outs)


HBM_SPEC = pl.BlockSpec(memory_space=pltpu.HBM)


def _place():
    mx, my, mc = lax.axis_index("x"), lax.axis_index("y"), lax.axis_index("c")
    others = [(1 - mx, my), (mx, 1 - my), (1 - mx, 1 - my)]
    return mx, my, mc, others


def chip_allgather(x, *, name):
    def body(x_ref, o_ref, send_sems, recv_sems, local_sem):
        mx, my, mc, others = _place()
        me = 2 * mx + my
        mine = pltpu.make_async_copy(x_ref, o_ref.at[me], local_sem)
        mine.start()
        sends = [pltpu.make_async_remote_copy(src_ref=x_ref, dst_ref=o_ref.at[me], send_sem=send_sems.at[j], recv_sem=recv_sems.at[j],
                                              device_id=(px, py, mc), device_id_type=MESH) for j, (px, py) in enumerate(others)]
        for cp in sends:
            cp.start()
        for j, (px, py) in enumerate(others):
            pltpu.make_async_remote_copy(src_ref=x_ref, dst_ref=o_ref.at[2 * px + py], send_sem=send_sems.at[j], recv_sem=recv_sems.at[j],
                                         device_id=(px, py, mc), device_id_type=MESH).wait_recv()
        for cp in sends:
            cp.wait_send()
        mine.wait()

    return pl.pallas_call(
        body, name=name, out_shape=jax.ShapeDtypeStruct((4,) + x.shape, x.dtype), in_specs=[HBM_SPEC], out_specs=HBM_SPEC,
        scratch_shapes=[pltpu.SemaphoreType.DMA((3,)), pltpu.SemaphoreType.DMA((3,)), pltpu.SemaphoreType.DMA])(x)


def _win(ref, axis, start, size):
    idx = [slice(None)] * len(ref.shape)
    idx[axis] = pl.ds(start, size)
    return ref.at[tuple(idx)]


def _half_axis(shape, ax):
    if shape[0] == 2:
        return 0
    return 2 if ax == 1 else 1


def _cut(shape, axis, parts):
    return shape[:axis] + (shape[axis] // parts,) + shape[axis + 1:]


def _hbm_call(body, arrays, out_shapes, sems, name):
    n_in = len(arrays)
    return pl.pallas_call(body, name=name, out_shape=tuple(out_shapes), in_specs=[HBM_SPEC] * n_in, out_specs=tuple([HBM_SPEC] * len(out_shapes)),
                          scratch_shapes=sems)(*arrays)


def place_shard(shard, ax, chip, dtype, *, name):
    l, r, c = shard.shape
    tr = _row_tile(r, c)
    per_block = (l, r // tr, 1)[ax]

    def omap(li, ri, cref):
        idx = [li, ri, 0]
        idx[ax] = idx[ax] + cref[0] * per_block
        return tuple(idx)

    def body(c_ref, s_ref, o_ref):
        o_ref[...] = s_ref[...].astype(dtype)

    full = shard.shape[:ax] + (4 * shard.shape[ax],) + shard.shape[ax + 1:]
    return pl.pallas_call(
        body, name=name, out_shape=jax.ShapeDtypeStruct(full, dtype),
        grid_spec=pltpu.PrefetchScalarGridSpec(
            num_scalar_prefetch=1, grid=(l, r // tr),
            in_specs=[pl.BlockSpec((1, tr, c), lambda li, ri, cref: (li, ri, 0))], out_specs=pl.BlockSpec((1, tr, c), omap)),
        compiler_params=_cp(("parallel", "parallel")))(chip, shard)


def gather_placed(arrays, axes, haxes, *, name):
    n = len(arrays)

    def body(*refs):
        ins, outs = refs[:n], refs[n:2 * n]
        send_sems, recv_sems = refs[2 * n:]
        mx, my, mc, others = _place()
        me = 2 * mx + my

        def part(ref, i, chip):
            sz, hs = arrays[i].shape[axes[i]] // 4, arrays[i].shape[haxes[i]] // 2
            return _win(_win(ref, axes[i], chip * sz, sz), haxes[i], mc * hs, hs)

        sends = []
        for i in range(n):
            for j, (px, py) in enumerate(others):
                rc = pltpu.make_async_remote_copy(src_ref=part(ins[i], i, me), dst_ref=part(outs[i], i, me), send_sem=send_sems.at[i, j],
                                                  recv_sem=recv_sems.at[i, j], device_id=(px, py, mc), device_id_type=MESH)
                rc.start()
                sends.append(rc)
        for i in range(n):
            for j, (px, py) in enumerate(others):
                pltpu.make_async_remote_copy(src_ref=part(ins[i], i, me), dst_ref=part(outs[i], i, 2 * px + py), send_sem=send_sems.at[i, j],
                                             recv_sem=recv_sems.at[i, j], device_id=(px, py, mc), device_id_type=MESH).wait_recv()
        for rc in sends:
            rc.wait_send()

    return pl.pallas_call(
        body, name=name, out_shape=tuple(jax.ShapeDtypeStruct(a_.shape, a_.dtype) for a_ in arrays), in_specs=[HBM_SPEC] * n,
        out_specs=tuple([HBM_SPEC] * n), input_output_aliases={i: i for i in range(n)},
        scratch_shapes=[pltpu.SemaphoreType.DMA((n, 3)), pltpu.SemaphoreType.DMA((n, 3))])(*arrays)


SEM_SPEC = pl.BlockSpec(memory_space=pltpu.SEMAPHORE)
SPLIT_COPY = pltpu.CompilerParams(has_side_effects=pltpu.SideEffectType.DATAFLOW_SIDE_EFFECTING)


def _gather_part(ref, shape, ax, hax, chip, core):
    sz, hs = shape[ax] // 4, shape[hax] // 2
    return _win(_win(ref, ax, chip * sz, sz), hax, core * hs, hs)


def gather_placed_start(arrays, axes, haxes, after, *, name):
    n = len(arrays)

    m = 3 * n

    def body(*refs):
        ins, send_sems, recv_sems = refs[:n], refs[n + 1:n + 1 + m], refs[n + 1 + m:n + 1 + 2 * m]
        token = refs[2 * n + 1 + 2 * m]
        mx, my, mc, others = _place()
        me = 2 * mx + my
        for i in range(n):
            for j, (px, py) in enumerate(others):
                part = _gather_part(ins[i], arrays[i].shape, axes[i], haxes[i], me, mc)
                pltpu.make_async_remote_copy(src_ref=part, dst_ref=part, send_sem=send_sems[3 * i + j], recv_sem=recv_sems[3 * i + j],
                                             device_id=(px, py, mc), device_id_type=MESH).start()
        token[...] = jnp.zeros_like(token)

    hbm = [pltpu.with_memory_space_constraint(a_, pltpu.HBM) for a_ in arrays]
    out = pl.pallas_call(
        body, name=name,
        out_shape=tuple([pltpu.SemaphoreType.DMA(())] * (2 * m)) + tuple(pltpu.HBM(a_.shape, a_.dtype) for a_ in arrays)
        + (jax.ShapeDtypeStruct((8, 128), F32),),
        in_specs=[HBM_SPEC] * n + [pl.BlockSpec(memory_space=pl.ANY)],
        out_specs=tuple([SEM_SPEC] * (2 * m)) + tuple([HBM_SPEC] * n) + (pl.BlockSpec(memory_space=pltpu.VMEM),),
        input_output_aliases={i: 2 * m + i for i in range(n)}, compiler_params=SPLIT_COPY)(*hbm, after)
    return list(out[:m]), list(out[m:2 * m]), list(out[2 * m:2 * m + n]), out[2 * m + n]


def gather_placed_wait(arrays, send_sems, recv_sems, axes, haxes, after, *, name):
    n = len(arrays)

    m = 3 * n

    def body(*refs):
        ins, send_refs, recv_refs = refs[:n], refs[n:n + m], refs[n + m:n + 2 * m]
        mx, my, mc, others = _place()
        me = 2 * mx + my
        for i in range(n):
            for j, (px, py) in enumerate(others):
                cp = pltpu.make_async_remote_copy(
                    src_ref=_gather_part(ins[i], arrays[i].shape, axes[i], haxes[i], me, mc),
                    dst_ref=_gather_part(ins[i], arrays[i].shape, axes[i], haxes[i], 2 * px + py, mc),
                    send_sem=send_refs[3 * i + j], recv_sem=recv_refs[3 * i + j], device_id=(px, py, mc), device_id_type=MESH)
                cp.wait_send()
                cp.wait_recv()

    out = pl.pallas_call(
        body, name=name, out_shape=tuple(pltpu.HBM(a_.shape, a_.dtype) for a_ in arrays),
        in_specs=[HBM_SPEC] * n + [SEM_SPEC] * (2 * m) + [pl.BlockSpec(memory_space=pl.ANY)], out_specs=tuple([HBM_SPEC] * n),
        input_output_aliases={i: i for i in range(n)}, compiler_params=SPLIT_COPY)(*arrays, *send_sems, *recv_sems, after)
    return list(out)


def pair_swap_halves(arrays, haxes, *, name):
    n = len(arrays)

    def body(*refs):
        ins, outs = refs[:n], refs[n:2 * n]
        send_sems, recv_sems = refs[2 * n:]
        mx, my, mc, _ = _place()
        cps = []
        for i in range(n):
            hs = arrays[i].shape[haxes[i]] // 2
            cp = pltpu.make_async_remote_copy(src_ref=_win(ins[i], haxes[i], (1 - mc) * hs, hs), dst_ref=outs[i], send_sem=send_sems.at[i],
                                              recv_sem=recv_sems.at[i], device_id=(mx, my, 1 - mc), device_id_type=MESH)
            cp.start()
            cps.append(cp)
        for cp in cps:
            cp.wait()

    outs = [jax.ShapeDtypeStruct(_cut(a_.shape, h_, 2), a_.dtype) for a_, h_ in zip(arrays, haxes)]
    return _hbm_call(body, arrays, outs, [pltpu.SemaphoreType.DMA((n,)), pltpu.SemaphoreType.DMA((n,))], name)


def add_own_half(g, t, hax, core, *, out_dtype, name):
    l, r, c = t.shape
    tr = _row_tile(r, c)
    per_half = (l, r // tr, 1)[hax]

    def imap(li, ri, cref):
        idx = [li, ri, 0]
        idx[hax] = idx[hax] + cref[0] * per_half
        return tuple(idx)

    def body(c_ref, g_ref, t_ref, o_ref):
        o_ref[...] = (g_ref[...] + t_ref[...]).astype(out_dtype)

    return pl.pallas_call(
        body, name=name, out_shape=jax.ShapeDtypeStruct(t.shape, out_dtype),
        grid_spec=pltpu.PrefetchScalarGridSpec(
            num_scalar_prefetch=1, grid=(l, r // tr),
            in_specs=[pl.BlockSpec((1, tr, c), imap), pl.BlockSpec((1, tr, c), lambda li, ri, cref: (li, ri, 0))],
            out_specs=pl.BlockSpec((1, tr, c), lambda li, ri, cref: (li, ri, 0))),
        compiler_params=_cp(("parallel", "parallel")))(core, g, t)


def exchange_blocks(arrays, axes, *, name):
    n = len(arrays)

    def body(*refs):
        ins, outs = refs[:n], refs[n:2 * n]
        send_sems, recv_sems, local_sems = refs[2 * n:]
        mx, my, mc, others = _place()
        me = 2 * mx + my
        waits = []
        for i in range(n):
            sz = arrays[i].shape[axes[i]] // 4
            cp = pltpu.make_async_copy(_win(ins[i], axes[i], me * sz, sz), outs[i].at[me], local_sems.at[i])
            cp.start()
            waits.append(cp.wait)
            for j, (px, py) in enumerate(others):
                rc = pltpu.make_async_remote_copy(src_ref=_win(ins[i], axes[i], (2 * px + py) * sz, sz), dst_ref=outs[i].at[me],
                                                  send_sem=send_sems.at[i, j], recv_sem=recv_sems.at[i, j], device_id=(px, py, mc),
                                                  device_id_type=MESH)
                rc.start()
                waits.append(rc.wait_send)
        for i in range(n):
            sz = arrays[i].shape[axes[i]] // 4
            for j, (px, py) in enumerate(others):
                pltpu.make_async_remote_copy(src_ref=_win(ins[i], axes[i], me * sz, sz), dst_ref=outs[i].at[2 * px + py],
                                             send_sem=send_sems.at[i, j], recv_sem=recv_sems.at[i, j], device_id=(px, py, mc),
                                             device_id_type=MESH).wait_recv()
        for w_ in waits:
            w_()

    outs = [jax.ShapeDtypeStruct((4,) + _cut(a_.shape, ax, 4), a_.dtype) for a_, ax in zip(arrays, axes)]
    return _hbm_call(body, arrays, outs, [pltpu.SemaphoreType.DMA((n, 3)), pltpu.SemaphoreType.DMA((n, 3)), pltpu.SemaphoreType.DMA((n,))], name)


def exchange_blocks_start(arrays, axes, *, name):
    n = len(arrays)
    lands = [lax.empty((4,) + _cut(a_.shape, ax, 4), a_.dtype) for a_, ax in zip(arrays, axes)]

    def body(*refs):
        ins, lnd = refs[:n], refs[n:2 * n]
        send_sems, recv_sems = refs[2 * n:6 * n], refs[6 * n:9 * n]
        token = refs[11 * n]
        mx, my, mc, others = _place()
        me = 2 * mx + my
        for i in range(n):
            sz = arrays[i].shape[axes[i]] // 4
            pltpu.make_async_copy(_win(ins[i], axes[i], me * sz, sz), lnd[i].at[me], send_sems[4 * i + 3]).start()
            for j, (px, py) in enumerate(others):
                pltpu.make_async_remote_copy(src_ref=_win(ins[i], axes[i], (2 * px + py) * sz, sz), dst_ref=lnd[i].at[me],
                                             send_sem=send_sems[4 * i + j], recv_sem=recv_sems[3 * i + j], device_id=(px, py, mc),
                                             device_id_type=MESH).start()
        token[...] = jnp.zeros_like(token)

    hbm = [pltpu.with_memory_space_constraint(a_, pltpu.HBM) for a_ in arrays + lands]
    out = pl.pallas_call(
        body, name=name,
        out_shape=tuple([pltpu.SemaphoreType.DMA(())] * (7 * n)) + tuple(pltpu.HBM(a_.shape, a_.dtype) for a_ in arrays + lands)
        + (jax.ShapeDtypeStruct((8, 128), F32),),
        in_specs=[HBM_SPEC] * (2 * n),
        out_specs=tuple([SEM_SPEC] * (7 * n)) + tuple([HBM_SPEC] * (2 * n)) + (pl.BlockSpec(memory_space=pltpu.VMEM),),
        input_output_aliases={i: 7 * n + i for i in range(2 * n)}, compiler_params=SPLIT_COPY)(*hbm)
    return list(out[:7 * n]), list(out[7 * n:8 * n]), list(out[8 * n:9 * n]), out[9 * n]


def exchange_blocks_wait(sems, arrays, lands, axes, after, *, name):
    n = len(arrays)

    def body(*refs):
        ins, lnd = refs[:n], refs[n:2 * n]
        send_sems, recv_sems = refs[2 * n:6 * n], refs[6 * n:9 * n]
        mx, my, mc, others = _place()
        me = 2 * mx + my
        for i in range(n):
            sz = arrays[i].shape[axes[i]] // 4
            mine = _win(ins[i], axes[i], me * sz, sz)
            pltpu.make_async_copy(mine, lnd[i].at[me], send_sems[4 * i + 3]).wait()
            for j, (px, py) in enumerate(others):
                cp = pltpu.make_async_remote_copy(src_ref=mine, dst_ref=lnd[i].at[2 * px + py], send_sem=send_sems[4 * i + j],
                                                  recv_sem=recv_sems[3 * i + j], device_id=(px, py, mc), device_id_type=MESH)
                cp.wait_send()
                cp.wait_recv()

    out = pl.pallas_call(
        body, name=name, out_shape=tuple(pltpu.HBM(a_.shape, a_.dtype) for a_ in arrays + lands),
        in_specs=[HBM_SPEC] * (2 * n) + [SEM_SPEC] * (7 * n) + [pl.BlockSpec(memory_space=pl.ANY)] * len(after),
        out_specs=tuple([HBM_SPEC] * (2 * n)), input_output_aliases={i: i for i in range(2 * n)}, compiler_params=SPLIT_COPY)(
            *arrays, *lands, *sems, *after)
    return list(out[n:])


def sum_blocks(e, hax, core, *, name):
    _, l, r, c = e.shape
    tr = _row_tile(r, c)
    per_half = (l, r // tr, 1)[hax]

    def omap(li, ri, cref):
        idx = [li, ri, 0]
        idx[hax] = idx[hax] + cref[0] * per_half
        return tuple(idx)

    def body(c_ref, e_ref, o_ref):
        v = e_ref[...].astype(F32)
        o_ref[...] = ((v[0] + v[1]) + v[2]) + v[3]

    full = (l, r, c)[:hax] + (2 * (l, r, c)[hax],) + (l, r, c)[hax + 1:]
    return pl.pallas_call(
        body, name=name, out_shape=jax.ShapeDtypeStruct(full, F32),
        grid_spec=pltpu.PrefetchScalarGridSpec(
            num_scalar_prefetch=1, grid=(l, r // tr),
            in_specs=[pl.BlockSpec((4, 1, tr, c), lambda li, ri, cref: (0, li, ri, 0))], out_specs=pl.BlockSpec((1, tr, c), omap)),
        compiler_params=_cp(("parallel", "parallel")))(core, e)


def pair_fill_halves(arrays, haxes, *, name):
    n = len(arrays)

    def body(*refs):
        ins, outs = refs[:n], refs[n:2 * n]
        send_sems, recv_sems = refs[2 * n:]
        mx, my, mc, _ = _place()
        cps = []
        for i in range(n):
            hs = arrays[i].shape[haxes[i]] // 2
            mine = _win(ins[i], haxes[i], mc * hs, hs)
            cp = pltpu.make_async_remote_copy(src_ref=mine, dst_ref=_win(outs[i], haxes[i], mc * hs, hs), send_sem=send_sems.at[i],
                                              recv_sem=recv_sems.at[i], device_id=(mx, my, 1 - mc), device_id_type=MESH)
            cp.start()
            cps.append(cp)
        for i in range(n):
            hs = arrays[i].shape[haxes[i]] // 2
            pltpu.make_async_remote_copy(src_ref=_win(ins[i], haxes[i], mc * hs, hs), dst_ref=_win(outs[i], haxes[i], (1 - mc) * hs, hs),
                                         send_sem=send_sems.at[i], recv_sem=recv_sems.at[i], device_id=(mx, my, 1 - mc),
                                         device_id_type=MESH).wait_recv()
        for cp in cps:
            cp.wait_send()

    return pl.pallas_call(
        body, name=name, out_shape=tuple(jax.ShapeDtypeStruct(a_.shape, a_.dtype) for a_ in arrays), in_specs=[HBM_SPEC] * n,
        out_specs=tuple([HBM_SPEC] * n), input_output_aliases={i: i for i in range(n)},
        scratch_shapes=[pltpu.SemaphoreType.DMA((n,)), pltpu.SemaphoreType.DMA((n,))])(*arrays)


WEIGHTS = ['c_ctx', 'w_mod', 'b_mod', 'norm1_w', 'norm2_w', 'final_norm_w', 's5_w_in', 's5_lam_re', 's5_lam_im', 's5_log_step', 's5_b_re', 's5_b_im', 's5_c_re', 's5_c_im', 's5_d', 's5_w_glu', 's5_w_out', 'hg_w_in', 'hg_lower_bounds', 'hg_gnorm_w', 'hg_w_out', 'ffn_w_up', 'ffn_conv_w', 'ffn_conv_b', 'ffn_w_down']
INPUTS = ['x', 'c', 'ctx', 'c_ctx', 'w_mod', 'b_mod', 'norm1_w', 'norm2_w', 'final_norm_w', 's5_w_in', 's5_lam_re', 's5_lam_im', 's5_log_step', 's5_b_re', 's5_b_im', 's5_c_re', 's5_c_im', 's5_d', 's5_w_glu', 's5_w_out', 'hg_w_in', 'hg_lower_bounds', 'hg_gnorm_w', 'hg_w_out', 'ffn_w_up', 'ffn_conv_w', 'ffn_conv_b', 'ffn_w_down', 'loss_target', 'm_c_ctx', 'm_w_mod', 'm_b_mod', 'm_norm1_w', 'm_norm2_w', 'm_final_norm_w', 'm_s5_w_in', 'm_s5_lam_re', 'm_s5_lam_im', 'm_s5_log_step', 'm_s5_b_re', 'm_s5_b_im', 'm_s5_c_re', 'm_s5_c_im', 'm_s5_d', 'm_s5_w_glu', 'm_s5_w_out', 'm_hg_w_in', 'm_hg_lower_bounds', 'm_hg_gnorm_w', 'm_hg_w_out', 'm_ffn_w_up', 'm_ffn_conv_w', 'm_ffn_conv_b', 'm_ffn_w_down', 'v_c_ctx', 'v_w_mod', 'v_b_mod', 'v_norm1_w', 'v_norm2_w', 'v_final_norm_w', 'v_s5_w_in', 'v_s5_lam_re', 'v_s5_lam_im', 'v_s5_log_step', 'v_s5_b_re', 'v_s5_b_im', 'v_s5_c_re', 'v_s5_c_im', 'v_s5_d', 'v_s5_w_glu', 'v_s5_w_out', 'v_hg_w_in', 'v_hg_lower_bounds', 'v_hg_gnorm_w', 'v_hg_w_out', 'v_ffn_w_up', 'v_ffn_conv_w', 'v_ffn_conv_b', 'v_ffn_w_down']
SHARD_AXIS = {"w_mod": 2, "s5_w_in": 1, "s5_w_glu": 1, "s5_w_out": 1, "hg_w_in": 2, "hg_lower_bounds": 2, "hg_w_out": 1,
              "ffn_w_up": 2, "ffn_conv_w": 2, "ffn_w_down": 1}
GATHER_F32 = ("hg_lower_bounds", "ffn_conv_w")
PACK_W = 1024
GRAD_WIRE = jnp.bfloat16


def _reduce_start(items, core, tag):
    names, arrays, axes = [n for n, _, _ in items], [g_ for _, g_, _ in items], [ax for _, _, ax in items]
    haxes = [_half_axis(g_.shape, ax) for g_, ax in zip(arrays, axes)]
    t = pair_swap_halves(arrays, haxes, name="grad_pair_swap_" + tag)
    h = [add_own_half(g_, t_, hx, core, out_dtype=GRAD_WIRE, name="grad_pair_add_" + n) for g_, t_, hx, n in zip(arrays, t, haxes, names)]
    sems, h, lands, token = exchange_blocks_start(h, axes, name="grad_exchange_start_" + tag)
    return (names, sems, h, lands, axes, haxes), token


def _reduce_finish(state, core, after, tag):
    names, sems, h, lands, axes, haxes = state
    e = exchange_blocks_wait(sems, h, lands, axes, list(after), name="grad_exchange_wait_" + tag)
    s = [sum_blocks(e_, hx, core, name="grad_chip_sum_" + n) for e_, hx, n in zip(e, haxes, names)]
    return dict(zip(names, pair_fill_halves(s, haxes, name="grad_pair_fill_" + tag)))


def _pack_small(grads, small):
    flat = jnp.concatenate([grads[n].reshape(-1) for n in small])
    pad = (-flat.shape[0]) % (64 * PACK_W)
    return jnp.pad(flat, (0, pad)).reshape(1, -1, PACK_W)


def _unpack_small(a, block, small):
    sm = chip_allgather(block[0], name="allgather_small_grads").reshape(-1)
    out, off = {}, 0
    for n in small:
        out[n] = sm[off:off + math.prod(a[n].shape)].reshape(a[n].shape)
        off += math.prod(a[n].shape)
    return out


def _blockdiag_b(bb, kb):
    gl = S5_KIN // S5_GROUP
    x = bb.reshape(kb, gl, S5_GROUP, S5_STATE)
    return (x[:, :, :, None, :] * jnp.eye(gl, dtype=bb.dtype)[None, :, None, :, None]).reshape(kb, S5_KIN, S5_KST)


def _blockdiag_c(cc, kb):
    gl = S5_KIN // S5_GROUP
    x = cc.reshape(kb, gl, S5_GROUP, S5_STATE).transpose(0, 1, 3, 2)
    return (x[:, :, :, None, :] * jnp.eye(gl, dtype=cc.dtype)[None, :, None, :, None]).reshape(kb, S5_KST, S5_KIN)


def _diag_b(m, kb):
    gl = S5_KIN // S5_GROUP
    x = m.reshape(kb, gl, S5_GROUP, gl, S5_STATE)
    return jnp.stack([x[:, i, :, i, :] for i in range(gl)], axis=1).reshape(kb * gl, S5_GROUP, S5_STATE)


def _diag_c(m, kb):
    gl = S5_KIN // S5_GROUP
    x = m.reshape(kb, gl, S5_STATE, gl, S5_GROUP)
    return jnp.stack([x[:, i, :, i, :] for i in range(gl)], axis=1).transpose(0, 1, 3, 2).reshape(kb * gl, S5_GROUP, S5_STATE)


def kernel(x, c, ctx, c_ctx, w_mod, b_mod, norm1_w, norm2_w, final_norm_w, s5_w_in, s5_lam_re, s5_lam_im, s5_log_step, s5_b_re, s5_b_im, s5_c_re, s5_c_im, s5_d, s5_w_glu, s5_w_out, hg_w_in, hg_lower_bounds, hg_gnorm_w, hg_w_out, ffn_w_up, ffn_conv_w, ffn_conv_b, ffn_w_down, loss_target, m_c_ctx, m_w_mod, m_b_mod, m_norm1_w, m_norm2_w, m_final_norm_w, m_s5_w_in, m_s5_lam_re, m_s5_lam_im, m_s5_log_step, m_s5_b_re, m_s5_b_im, m_s5_c_re, m_s5_c_im, m_s5_d, m_s5_w_glu, m_s5_w_out, m_hg_w_in, m_hg_lower_bounds, m_hg_gnorm_w, m_hg_w_out, m_ffn_w_up, m_ffn_conv_w, m_ffn_conv_b, m_ffn_w_down, v_c_ctx, v_w_mod, v_b_mod, v_norm1_w, v_norm2_w, v_final_norm_w, v_s5_w_in, v_s5_lam_re, v_s5_lam_im, v_s5_log_step, v_s5_b_re, v_s5_b_im, v_s5_c_re, v_s5_c_im, v_s5_d, v_s5_w_glu, v_s5_w_out, v_hg_w_in, v_hg_lower_bounds, v_hg_gnorm_w, v_hg_w_out, v_ffn_w_up, v_ffn_conv_w, v_ffn_conv_b, v_ffn_w_down):
    a = dict(zip(INPUTS, (x, c, ctx, c_ctx, w_mod, b_mod, norm1_w, norm2_w, final_norm_w, s5_w_in, s5_lam_re, s5_lam_im, s5_log_step, s5_b_re, s5_b_im, s5_c_re, s5_c_im, s5_d, s5_w_glu, s5_w_out, hg_w_in, hg_lower_bounds, hg_gnorm_w, hg_w_out, ffn_w_up, ffn_conv_w, ffn_conv_b, ffn_w_down, loss_target, m_c_ctx, m_w_mod, m_b_mod, m_norm1_w, m_norm2_w, m_final_norm_w, m_s5_w_in, m_s5_lam_re, m_s5_lam_im, m_s5_log_step, m_s5_b_re, m_s5_b_im, m_s5_c_re, m_s5_c_im, m_s5_d, m_s5_w_glu, m_s5_w_out, m_hg_w_in, m_hg_lower_bounds, m_hg_gnorm_w, m_hg_w_out, m_ffn_w_up, m_ffn_conv_w, m_ffn_conv_b, m_ffn_w_down, v_c_ctx, v_w_mod, v_b_mod, v_norm1_w, v_norm2_w, v_final_norm_w, v_s5_w_in, v_s5_lam_re, v_s5_lam_im, v_s5_log_step, v_s5_b_re, v_s5_b_im, v_s5_c_re, v_s5_c_im, v_s5_d, v_s5_w_glu, v_s5_w_out, v_hg_w_in, v_hg_lower_bounds, v_hg_gnorm_w, v_hg_w_out, v_ffn_w_up, v_ffn_conv_w, v_ffn_conv_b, v_ffn_w_down)))
    nb, seq, d = x.shape
    assert nb == NB
    rc = nb * ctx.shape[1]
    cfg = {"rc": rc}
    f = a["ffn_w_down"].shape[1] * 4
    core = lax.axis_index("c").astype(jnp.int32).reshape(1)

    w = {n: a[n] for n in WEIGHTS if n not in SHARD_AXIS}
    chip = (2 * lax.axis_index("x") + lax.axis_index("y")).astype(jnp.int32).reshape(1)
    groups = {
        "now": [("w_mod0", a["w_mod"][0:1]), ("s5_w_in", a["s5_w_in"]), ("hg_lower_bounds", a["hg_lower_bounds"]), ("ffn_conv_w", a["ffn_conv_w"])],
        "mid": [("s5_w_glu", a["s5_w_glu"]), ("s5_w_out", a["s5_w_out"]), ("ffn_w_up0", a["ffn_w_up"][0:1]), ("ffn_w_down0", a["ffn_w_down"][0:1])],
        "later": [("w_mod1", a["w_mod"][1:2]), ("hg_w_in", a["hg_w_in"]), ("hg_w_out", a["hg_w_out"]), ("ffn_w_up1", a["ffn_w_up"][1:2]),
                  ("ffn_w_down1", a["ffn_w_down"][1:2])]}
    shard_axis = lambda n: SHARD_AXIS[n.rstrip("01")]
    placed = {g: [place_shard(s_, shard_axis(n), chip, F32 if n in GATHER_F32 else MXU, name="place_" + n) for n, s_ in it] for g, it in groups.items()}
    axes = {g: [shard_axis(n) for n, _ in it] for g, it in groups.items()}
    haxes = {g: [_half_axis(p_.shape, ax) for p_, ax in zip(placed[g], axes[g])] for g in groups}
    fly_now = gather_placed_start(placed["now"], axes["now"], haxes["now"], chip, name="allgather_now_start")
    fly_mid = gather_placed_start(placed["mid"], axes["mid"], haxes["mid"], fly_now[3], name="allgather_mid_start")
    fly_later = gather_placed_start(placed["later"], axes["later"], haxes["later"], fly_mid[3], name="allgather_later_start")

    def land(fly, g, after):
        send_, recv_, flying, _ = fly
        landed = gather_placed_wait(flying, send_, recv_, axes[g], haxes[g], after, name=f"allgather_{g}_wait")
        w.update(dict(zip([n for n, _ in groups[g]], pair_fill_halves(landed, haxes[g], name=f"allgather_{g}_pair_fill"))))

    tmaj = lambda t: t.transpose(1, 0, 2).reshape(-1, t.shape[-1])
    zero = fly_later[3][0:1, 0:1]
    x0 = jnp.concatenate([tmaj(ctx), tmaj(x)], axis=0)
    tgt = tmaj(a["loss_target"])
    land(fly_now, "now", x0)
    c16 = jnp.concatenate([jnp.broadcast_to(c_ctx[None], (8, d)), c, c], axis=0) + zero
    mt0, scb = mod_fwd(c16, w["w_mod0"][0], w["b_mod"][0][None], name="mod_fwd0")
    mt = [mt0, None]
    n1, n2 = w["norm1_w"], w["norm2_w"]
    w["w_mod"], w["ffn_w_up"], w["ffn_w_down"] = [w["w_mod0"][0], None], [None, None], [None, None]

    def ffn_fwd(l, h):
        u = mm(h, w["ffn_w_up"][l], out_dtype=MXU, name=f"ffn_up{l}")
        act, ca, cg = ffn_mid_fwd(cfg, u, w["ffn_conv_w"][l], w["ffn_conv_b"][l][None], name=f"ffn_mid{l}")
        return (u, ca, cg), act, mm(act, w["ffn_w_down"][l], name=f"ffn_down{l}")

    def ffn_bwd(l, dfo, kept, act, h, zero=0.0):
        dact = mm(dfo, w["ffn_w_down"][l], tb=True, out_dtype=MXU, name=f"ffn_down_dx{l}")
        dwd = mm(act, dfo, ta=True, name=f"ffn_down_dw{l}")
        du, dcw, dcb = ffn_mid_bwd(cfg, dact, *kept, w["ffn_conv_w"][l] + zero, name=f"ffn_mid_bwd{l}")
        dh = mm(du, w["ffn_w_up"][l], tb=True, name=f"ffn_up_dx{l}")
        dwu = mm(h, du, ta=True, name=f"ffn_up_dw{l}")
        return dh, dwu, dcw, dcb[0], dwd

    g_, p_ = d // S5_GROUP, S5_STATE
    ns, kb = g_ * p_, d // S5_KIN
    s5p = (w["s5_lam_re"][0].reshape(2 * g_, p_), w["s5_lam_im"][0].reshape(2 * g_, p_), w["s5_log_step"][0].reshape(2 * g_, 1),
           w["s5_b_re"][0].transpose(0, 1, 3, 2).reshape(2 * g_, S5_GROUP, p_), w["s5_b_im"][0].transpose(0, 1, 3, 2).reshape(2 * g_, S5_GROUP, p_))
    ar, ai, bbr, bbi = s5_disc_fwd(*s5p, name="s5_disc")
    dsk = w["s5_d"]
    _, h1 = node_fwd(cfg, x0, None, None, 0, n1[0:1], mt[0], 0, name="node0a")
    u0 = mm(h1, w["s5_w_in"][0], name="s5_in")
    s5s, ys = [], []
    for dd in range(2):
        sl = slice(dd * g_, (dd + 1) * g_)
        a_r, a_i = ar[sl].reshape(1, ns), ai[sl].reshape(1, ns)
        a2 = (a_r * a_r - a_i * a_i, 2.0 * a_r * a_i)
        b_r, b_i = _blockdiag_b(bbr[sl], kb), _blockdiag_b(bbi[sl], kb)
        c_r, c_i = _blockdiag_c(w["s5_c_re"][0, dd], kb), _blockdiag_c(w["s5_c_im"][0, dd], kb)
        ak, ai_k = a_r.reshape(kb, 1, S5_KST), a_i.reshape(kb, 1, S5_KST)
        ab = (ak * b_r - ai_k * b_i, ak * b_i + ai_k * b_r)
        akc, aic = ak.reshape(kb, S5_KST, 1), ai_k.reshape(kb, S5_KST, 1)
        c2 = (akc * c_r - aic * c_i, akc * c_i + aic * c_r)
        bf = lambda t_: t_.astype(MXU)
        sre, sim, ere, eim, y_ = s5_scan_fwd(cfg, u0, a2[0], a2[1], bf(b_r), bf(b_i), bf(ab[0]), bf(ab[1]), bf(c_r), bf(c_i), rev=dd == 1,
                                             name=f"s5_scan{dd}")
        s5s.append((sre, sim, ere, eim, a2[0], a2[1], bf(b_r), bf(b_i), bf(c_r), bf(c_i), bf(c2[0]), bf(c2[1])))
        ys.append(y_)

    def glu_a(u, y0, y1, ds):
        yp = (ds * u + y0) + y1
        return yp, _gelu(yp)

    ypre, zgb = rowmap(glu_a, [u0, ys[0], ys[1]], [dsk], [(d, F32), (d, MXU)], name="s5_glu_a")
    land(fly_mid, "mid", zgb)
    w["ffn_w_up"][0], w["ffn_w_down"][0] = w["ffn_w_up0"][0], w["ffn_w_down0"][0]
    tg = mm(zgb, w["s5_w_glu"][0], name="s5_glu")
    (z2,) = rowmap(lambda yp, t: _gelu(yp) * jax.nn.sigmoid(t), [ypre, tg], [], [(d, MXU)], name="s5_glu_b")
    y1a = mm(z2, w["s5_w_out"][0], name="s5_out")
    x1a, h2a = node_fwd(cfg, x0, y1a, mt[0], 2, n2[0:1], mt[0], 3, name="node0b")
    ufa, acta, foa = ffn_fwd(0, h2a)

    land(fly_later, "later", foa)
    w["w_mod"][1], w["ffn_w_up"][1], w["ffn_w_down"][1] = w["w_mod1"][0], w["ffn_w_up1"][0], w["ffn_w_down1"][0]
    mt[1], _ = mod_fwd(c16, w["w_mod"][1], w["b_mod"][1][None], name="mod_fwd1")
    x2a, h1b = node_fwd(cfg, x1a, foa, mt[0], 5, n1[1:2], mt[1], 0, name="node1a")
    z = mm(h1b, w["hg_w_in"][0], name="hg_in")
    e0, e1 = w["hg_lower_bounds"][:, 0, :], w["hg_lower_bounds"][:, 1, :]
    lb = hg_lb_fwd(e0, e1, name="hg_lb")
    gw = w["hg_gnorm_w"]
    o0, sin0 = hg_scan_fwd(cfg, z, lb[0:1], d_dir=0, name="hg_scan0")
    o1, sin1 = hg_scan_fwd(cfg, z, lb[1:2], d_dir=1, name="hg_scan1")
    onb = hg_read_fwd(o0, o1, z, gw, name="hg_read")
    y1b = mm(onb, w["hg_w_out"][0], name="hg_out")
    x1b, h2b = node_fwd(cfg, x2a, y1b, mt[1], 2, n2[1:2], mt[1], 3, name="node1b")
    ufb, actb, fob = ffn_fwd(1, h2b)
    loss_p, dx2b, dfob, dg2_1, dfnw = final_node(cfg, x1b, fob, mt[1], 5, w["final_norm_w"][None], tgt, name="final_node")

    gr = {}
    dh2b, dwu1, dcw1, dcb1, dwd1 = ffn_bwd(1, dfob, ufb, actb, h2b)
    dx1b, dy1b, dn2_1, dsh2_1, dsc2_1, dg1_1 = node_bwd(cfg, dx2b, dh2b, x1b, y1b, mt[1], 2, n2[1:2], mt[1], 3, name="node1b_bwd")
    don = mm(dy1b, w["hg_w_out"][0], tb=True, name="hg_out_dx")
    gr["hg_w_out"] = mm(onb, dy1b, ta=True, name="hg_out_dw")[None]
    do_, dgate_, dgw = hg_read_bwd(don, o0, o1, z, gw, name="hg_read_bwd")
    dq, dv, dxf, dlb0 = hg_scan_bwd(cfg, do_, z, lb[0:1], sin0, None, None, d_dir=0, name="hg_scan_bwd0")
    dq, dv, dxb, dlb1 = hg_scan_bwd(cfg, do_, z, lb[1:2], sin1, dq, dv, d_dir=1, name="hg_scan_bwd1")
    dz = [dq, dv, dxf, dxb, dgate_]
    dh1b = mm_cat_nt(dz, w["hg_w_in"][0], name="hg_in_dx")
    gr["hg_w_in"] = mm_tn_cat(h1b, dz, name="hg_in_dw")[None]
    de0, de1 = hg_lb_bwd(e0, e1, jnp.concatenate([dlb0, dlb1], axis=0), name="hg_lb_bwd")
    gr["hg_lower_bounds"] = jnp.stack([de0, de1], axis=1)
    gr["hg_gnorm_w"] = dgw
    dx2a, dfoa, dn1_1, dsh1_1, dsc1_1, dg2_0 = node_bwd(cfg, dx1b, dh1b, x2a, foa, mt[0], 5, n1[1:2], mt[1], 0, name="node1a_bwd")
    dmt1 = jnp.concatenate([dsh1_1, dsc1_1, dg1_1, dsh2_1, dsc2_1, dg2_1], axis=1)
    red1, tok1 = _reduce_start([("hg_w_in", gr["hg_w_in"], 2), ("hg_w_out", gr["hg_w_out"], 1), ("ffn_w_up1", dwu1[None], 2),
                                ("ffn_w_down1", dwd1[None], 1), ("w_mod1", mm(scb, dmt1, ta=True, name="mod_dw1")[None], 2)], core, "layer1")

    dh2a, dwu0, dcw0, dcb0, dwd0 = ffn_bwd(0, dfoa, ufa, acta, h2a, zero=tok1[0:1, 0:1])
    red2, tok2 = _reduce_start([("ffn_w_up0", dwu0[None], 2), ("ffn_w_down0", dwd0[None], 1)], core, "ffn0")
    dx1a, dy1a, dn2_0, dsh2_0, dsc2_0, dg1_0 = node_bwd(cfg, dx2a, dh2a, x1a, y1a, mt[0], 2, n2[0:1] + tok2[0:1, 0:1], mt[0], 3,
                                                        name="node0b_bwd")
    dz2 = mm(dy1a, w["s5_w_out"][0], tb=True, name="s5_out_dx")
    gr["s5_w_out"] = mm(z2, dy1a, ta=True, name="s5_out_dw")[None]

    def glu_b_bwd(dz2_, yp, t):
        zg, sg = _gelu(yp), jax.nn.sigmoid(t)
        return dz2_ * zg * sg * (1.0 - sg), dz2_ * sg

    dtg, dzg_dir = rowmap(glu_b_bwd, [dz2, ypre, tg], [], [(d, MXU), (d, F32)], name="s5_glu_b_bwd")
    dzg_mm = mm(dtg, w["s5_w_glu"][0], tb=True, name="s5_glu_dx")
    gr["s5_w_glu"] = mm(zgb, dtg, ta=True, name="s5_glu_dw")[None]

    def glu_a_bwd(dzd, dzm, yp, u, ds):
        _, vjp = jax.vjp(_gelu, yp)
        (dy,) = vjp(dzd + dzm)
        return dy, dy * ds, jnp.sum(dy * u, axis=0, keepdims=True)

    dyb, du, ddsk = rowmap(glu_a_bwd, [dzg_dir, dzg_mm, ypre, u0], [dsk], [(d, MXU), (d, F32)], [(1, d)], name="s5_glu_a_bwd")
    gr["s5_d"] = ddsk
    dar, dai, dbr, dbi, dcr, dci = [], [], [], [], [], []
    for dd in range(2):
        sre, sim, ere, eim = s5s[dd][:4]
        du, gre, gim, da_r, da_i = s5_scan_bwd(cfg, dyb, *s5s[dd], du, rev=dd == 1, name=f"s5_scan_bwd{dd}")
        dar.append(colsum(da_r, name=f"s5_da_re{dd}").reshape(g_, p_))
        dai.append(colsum(da_i, name=f"s5_da_im{dd}").reshape(g_, p_))
        dbr.append(_diag_b(blockdiag_tn(u0, gre, S5_KIN, S5_KST, name=f"s5_db_re{dd}"), kb))
        dbi.append(_diag_b(blockdiag_tn(u0, gim, S5_KIN, S5_KST, name=f"s5_db_im{dd}"), kb))
        dcr.append(_diag_c(blockdiag_tn(sre.reshape(-1, ns), dyb, S5_KST, S5_KIN, name=f"s5_dc_re{dd}"), kb))
        dci.append(_diag_c(blockdiag_tn(sim.reshape(-1, ns), dyb, S5_KST, S5_KIN, scale=-1.0, name=f"s5_dc_im{dd}"), kb))
    cat = lambda l_: jnp.concatenate(l_, axis=0)
    dlr, dli, dls, dbre, dbim = s5_disc_bwd(*s5p, cat(dar), cat(dai), cat(dbr), cat(dbi), name="s5_disc_bwd")
    gr["s5_lam_re"], gr["s5_lam_im"] = dlr.reshape(1, 2, g_, p_), dli.reshape(1, 2, g_, p_)
    gr["s5_log_step"] = dls.reshape(1, 2, g_)
    gr["s5_b_re"] = dbre.reshape(1, 2, g_, S5_GROUP, p_).transpose(0, 1, 2, 4, 3)
    gr["s5_b_im"] = dbim.reshape(1, 2, g_, S5_GROUP, p_).transpose(0, 1, 2, 4, 3)
    gr["s5_c_re"], gr["s5_c_im"] = jnp.stack(dcr)[None], jnp.stack(dci)[None]
    dh1 = mm(du, w["s5_w_in"][0], tb=True, name="s5_in_dx")
    gr["s5_w_in"] = mm(h1, du, ta=True, name="s5_in_dw")[None]
    dx0, _, dn1_0, dsh1_0, dsc1_0, _ = node_bwd(cfg, dx1a, dh1, x0, None, None, 0, n1[0:1], mt[0], 0, name="node0a_bwd")

    dmt = [jnp.concatenate([dsh1_0, dsc1_0, dg1_0, dsh2_0, dsc2_0, dg2_0], axis=1), dmt1]
    gr["b_mod"] = jnp.concatenate([colsum(dmt[l], name=f"mod_db{l}") for l in range(2)], axis=0)
    dsc16 = [mm(dmt[l], w["w_mod"][l], tb=True, name=f"mod_dx{l}") for l in range(2)]
    gr["c_ctx"] = cctx_grad(c16, dsc16, name="c_ctx_grad")[0]
    gr["norm1_w"] = jnp.concatenate([dn1_0, dn1_1], axis=0)
    gr["norm2_w"] = jnp.concatenate([dn2_0, dn2_1], axis=0)
    gr["final_norm_w"] = dfnw[0]
    gr["ffn_conv_w"], gr["ffn_conv_b"] = jnp.stack([dcw0, dcw1]), jnp.stack([dcb0, dcb1])

    last = [(n, gr[n], SHARD_AXIS[n]) for n in ("s5_w_in", "s5_w_glu", "s5_w_out", "hg_lower_bounds", "ffn_conv_w")]
    last.append(("w_mod0", mm(scb, dmt[0], ta=True, name="mod_dw0")[None], 2))
    small = [n for n in WEIGHTS if n not in SHARD_AXIS]
    last.append(("small", _pack_small(gr, small), 1))
    red3, tok3 = _reduce_start(last, core, "last")
    red = _reduce_finish(red1, core, [tok3], "layer1")
    red.update(_reduce_finish(red2, core, [tok3], "ffn0"))
    red["ffn_w_up"] = jnp.concatenate([red["ffn_w_up0"], red["ffn_w_up1"]], axis=0)
    red["ffn_w_down"] = jnp.concatenate([red["ffn_w_down0"], red["ffn_w_down1"]], axis=0)
    early = ("hg_w_in", "hg_w_out", "ffn_w_up", "ffn_w_down")
    upd = {n: adamw(a[n], red[n], a["m_" + n], a["v_" + n], name="adamw_" + n) for n in early}
    grad_x = dx0[rc:].reshape(seq, nb, d).transpose(1, 0, 2)
    red.update(_reduce_finish(red3, core, [upd[n][0] for n in early] + [grad_x], "last"))
    red.update(_unpack_small(a, red["small"], small))
    red["w_mod"] = jnp.concatenate([red["w_mod0"], red["w_mod1"]], axis=0)
    loss = lax.psum(loss_p[0, 0], ("x", "y", "c"))
    upd.update({n: adamw(a[n], red[n], a["m_" + n], a["v_" + n], name="adamw_" + n) for n in WEIGHTS if n not in early})
    return (loss, grad_x, *[red[n] for n in WEIGHTS], *[upd[n][0] for n in WEIGHTS], *[upd[n][1] for n in WEIGHTS],
            *[upd[n][2] for n in WEIGHTS])
```

```python
import functools
import math

import jax
import jax.numpy as jnp
from jax import lax
from jax.experimental import pallas as pl
from jax.experimental.pallas import tpu as pltpu

F32 = jnp.float32
BF = jnp.bfloat16
MXU = jnp.bfloat16

NORM_EPS = 1e-6
GRID_W = 64
N_MOD = 6
S5_GROUP = 16
S5_STATE = 64
S5_LAM_RE_MAX = -1e-4
S5_KIN = 256
S5_KST = S5_KIN // S5_GROUP * S5_STATE
HEAD = 128
CHUNK_ROWS = 128
N_PROJ = 5
NB = 4
ADAM_LR, ADAM_B1, ADAM_B2, ADAM_EPS, ADAM_WD, ADAM_STEP = 0.001, 0.9, 0.999, 1e-08, 0.01, 10
VMEM_LIMIT = 56 * 1024 * 1024
MESH = pl.DeviceIdType.MESH


def _tile(n, cap):
    if n <= cap:
        return n
    best = None
    for t in range(128, cap + 1, 128):
        if n % t == 0:
            best = t
    assert best is not None, (n, cap)
    return best


def _row_tile(r, width=1024):
    cap = max(8, (512 * 1024) // max(width, 1))
    return next((t for t in (512, 256, 128, 64, 32, 16, 8) if t <= cap and r % t == 0), r)


def _cp(sem):
    return pltpu.CompilerParams(dimension_semantics=sem, vmem_limit_bytes=VMEM_LIMIT)


def _dot(a, b, ca=1, cb=0):
    return lax.dot_general(a.astype(MXU), b.astype(MXU), (((ca,), (cb,)), ((), ())), preferred_element_type=F32)


def _dot3(m, x):
    hi = x.astype(MXU)
    lo = (x - hi.astype(F32)).astype(MXU)
    return _dot(m, hi) + _dot(m, lo)


def mm(a, b, *, ta=False, tb=False, out_dtype=F32, name):
    (kd, m) = a.shape if ta else a.shape[::-1]
    (n, kd2) = b.shape if tb else b.shape[::-1]
    assert kd == kd2, (a.shape, b.shape, ta, tb)
    tm, tn, tk = _tile(m, 1024), _tile(n, 1536), _tile(kd, 1024)
    nk = kd // tk

    def body(a_ref, b_ref, o_ref, acc_ref):
        k = pl.program_id(2)

        @pl.when(k == 0)
        def _():
            acc_ref[...] = jnp.zeros_like(acc_ref)

        acc_ref[...] += _dot(a_ref[...], b_ref[...], 0 if ta else 1, 1 if tb else 0)

        @pl.when(k == nk - 1)
        def _():
            o_ref[...] = acc_ref[...].astype(out_dtype)

    a_spec = pl.BlockSpec((tk, tm), lambda i, j, k: (k, i)) if ta else pl.BlockSpec((tm, tk), lambda i, j, k: (i, k))
    b_spec = pl.BlockSpec((tn, tk), lambda i, j, k: (j, k)) if tb else pl.BlockSpec((tk, tn), lambda i, j, k: (k, j))
    return pl.pallas_call(
        body, name=name, grid=(m // tm, n // tn, nk), in_specs=[a_spec, b_spec],
        out_specs=pl.BlockSpec((tm, tn), lambda i, j, k: (i, j)), out_shape=jax.ShapeDtypeStruct((m, n), out_dtype),
        scratch_shapes=[pltpu.VMEM((tm, tn), F32)], compiler_params=_cp(("parallel", "parallel", "arbitrary")))(a, b)


def mm_cat_nt(parts, b, *, name):
    m, wd = parts[0].shape
    n = b.shape[0]
    np_ = len(parts)
    tm, tn = _tile(m, 1024), _tile(n, 1024)

    def body(*refs):
        b_ref, o_ref, acc_ref = refs[np_], refs[np_ + 1], refs[np_ + 2]
        k = pl.program_id(2)

        @pl.when(k == 0)
        def _():
            acc_ref[...] = jnp.zeros_like(acc_ref)

        for p in range(np_):
            @pl.when(k == p)
            def _(p=p):
                acc_ref[...] += _dot(refs[p][...], b_ref[...], 1, 1)

        @pl.when(k == np_ - 1)
        def _():
            o_ref[...] = acc_ref[...]

    return pl.pallas_call(
        body, name=name, grid=(m // tm, n // tn, np_),
        in_specs=[pl.BlockSpec((tm, wd), lambda i, j, k: (i, 0))] * np_ + [pl.BlockSpec((tn, wd), lambda i, j, k: (j, k))],
        out_specs=pl.BlockSpec((tm, tn), lambda i, j, k: (i, j)), out_shape=jax.ShapeDtypeStruct((m, n), F32),
        scratch_shapes=[pltpu.VMEM((tm, tn), F32)], compiler_params=_cp(("parallel", "parallel", "arbitrary")))(*parts, b)


def mm_tn_cat(a, parts, *, name):
    kd, m = a.shape
    wd = parts[0].shape[1]
    np_ = len(parts)
    tm, tk = _tile(m, 1024), _tile(kd, 1024)
    nk = kd // tk

    def body(*refs):
        a_ref, o_ref, acc_ref = refs[0], refs[np_ + 1], refs[np_ + 2]
        j, k = pl.program_id(1), pl.program_id(2)

        @pl.when(k == 0)
        def _():
            acc_ref[...] = jnp.zeros_like(acc_ref)

        for p in range(np_):
            @pl.when(j == p)
            def _(p=p):
                acc_ref[...] += _dot(a_ref[...], refs[1 + p][...], 0, 0)

        @pl.when(k == nk - 1)
        def _():
            o_ref[...] = acc_ref[...]

    part_spec = lambda p: pl.BlockSpec((tk, wd), lambda i, j, k: (jnp.where(j == p, k, 0), 0))
    return pl.pallas_call(
        body, name=name, grid=(m // tm, np_, nk), in_specs=[pl.BlockSpec((tk, tm), lambda i, j, k: (k, i))] + [part_spec(p) for p in range(np_)],
        out_specs=pl.BlockSpec((tm, wd), lambda i, j, k: (i, j)), out_shape=jax.ShapeDtypeStruct((m, np_ * wd), F32),
        scratch_shapes=[pltpu.VMEM((tm, wd), F32)], compiler_params=_cp(("parallel", "parallel", "arbitrary")))(a, *parts)


def blockdiag_tn(a, b, wa, wb, *, scale=1.0, name):
    rows = a.shape[0]
    kb = a.shape[1] // wa
    tr = _tile(rows, 1024)
    nr = rows // tr

    def body(a_ref, b_ref, o_ref):
        i = pl.program_id(1)

        @pl.when(i == 0)
        def _():
            o_ref[...] = jnp.zeros_like(o_ref)

        o_ref[0] += scale * _dot(a_ref[...], b_ref[...], 0, 0)

    return pl.pallas_call(
        body, name=name, grid=(kb, nr),
        in_specs=[pl.BlockSpec((tr, wa), lambda k, i: (i, k)), pl.BlockSpec((tr, wb), lambda k, i: (i, k))],
        out_specs=pl.BlockSpec((1, wa, wb), lambda k, i: (k, 0, 0)), out_shape=jax.ShapeDtypeStruct((kb, wa, wb), F32),
        compiler_params=_cp(("parallel", "arbitrary")))(a, b)


def _pat(v, p, op):
    tm, d = v.shape
    return op(v.reshape(tm // 8, 8, d), p[None]).reshape(tm, d)


def _norm_mod(x, nw, shift, scale):
    y = x * lax.rsqrt(jnp.mean(x * x, axis=-1, keepdims=True) + NORM_EPS) * nw
    return _pat(_pat(y, 1.0 + scale, jnp.multiply), shift, jnp.add)


def _mt_spec(d, nct):
    return pl.BlockSpec((8, N_MOD * d), lambda i: (jnp.where(i < nct, 0, 1), 0))


def _acc_spec(d, nct):
    return pl.BlockSpec((8, d), lambda i: (jnp.where(i < nct, 0, 1), 0))


def _rows(cfg):
    tm = min(512, cfg["rc"])
    return tm, cfg["rc"] // tm


def node_fwd(cfg, xp, y, mtg, gi, nw, mtn, si, *, name):
    r, d = xp.shape
    tm, nct = _rows(cfg)
    row = pl.BlockSpec((tm, d), lambda i: (i, 0))
    vec = pl.BlockSpec((1, d), lambda i: (0, 0))

    def body(*refs):
        if y is None:
            xp_ref, nw_ref, mtn_ref, h_ref = refs
            x = xp_ref[...]
        else:
            xp_ref, y_ref, mtg_ref, nw_ref, mtn_ref, xn_ref, h_ref = refs
            x = xp_ref[...] + _pat(y_ref[...], mtg_ref[:, gi * d:(gi + 1) * d], jnp.multiply)
            xn_ref[...] = x
        h_ref[...] = _norm_mod(x, nw_ref[...], mtn_ref[:, si * d:(si + 1) * d], mtn_ref[:, (si + 1) * d:(si + 2) * d]).astype(MXU)

    h_shape = jax.ShapeDtypeStruct((r, d), MXU)
    if y is None:
        h = pl.pallas_call(body, name=name, grid=(r // tm,), in_specs=[row, vec, _mt_spec(d, nct)], out_specs=row,
                           out_shape=h_shape, compiler_params=_cp(("parallel",)))(xp, nw, mtn)
        return xp, h
    return pl.pallas_call(body, name=name, grid=(r // tm,), in_specs=[row, row, _mt_spec(d, nct), vec, _mt_spec(d, nct)],
                          out_specs=(row, row), out_shape=(jax.ShapeDtypeStruct((r, d), F32), h_shape),
                          compiler_params=_cp(("parallel",)))(xp, y, mtg, nw, mtn)


def node_bwd(cfg, dxres, dh, xn, y, mtg, gi, nw, mtn, si, *, name):
    r, d = xn.shape
    tm, nct = _rows(cfg)
    row = pl.BlockSpec((tm, d), lambda i: (i, 0))
    vec = pl.BlockSpec((1, d), lambda i: (0, 0))
    has_y = y is not None

    def body(*refs):
        if has_y:
            dxres_ref, dh_ref, xn_ref, y_ref, mtg_ref, nw_ref, mtn_ref, dxn_ref, dy_ref, dnw_ref, dsh_ref, dsc_ref, dg_ref = refs
        else:
            dxres_ref, dh_ref, xn_ref, nw_ref, mtn_ref, dxn_ref, dnw_ref, dsh_ref, dsc_ref = refs
        i = pl.program_id(0)
        _, vjp = jax.vjp(_norm_mod, xn_ref[...], nw_ref[...], mtn_ref[:, si * d:(si + 1) * d], mtn_ref[:, (si + 1) * d:(si + 2) * d])
        dx, dnw, dsh, dsc = vjp(dh_ref[...])
        dx = dx + dxres_ref[...]
        dxn_ref[...] = dx

        @pl.when(i == 0)
        def _():
            dnw_ref[...] = jnp.zeros_like(dnw_ref)

        @pl.when((i == 0) | (i == nct))
        def _():
            dsh_ref[...] = jnp.zeros_like(dsh_ref)
            dsc_ref[...] = jnp.zeros_like(dsc_ref)
            if has_y:
                dg_ref[...] = jnp.zeros_like(dg_ref)

        dnw_ref[...] += dnw
        dsh_ref[...] += dsh
        dsc_ref[...] += dsc
        if has_y:
            dy_ref[...] = _pat(dx, mtg_ref[:, gi * d:(gi + 1) * d], jnp.multiply).astype(MXU)
            dg_ref[...] += jnp.sum((dx * y_ref[...]).reshape(tm // 8, 8, d), axis=0)

    acc = jax.ShapeDtypeStruct((16, d), F32)
    xs = jax.ShapeDtypeStruct((r, d), F32)
    if has_y:
        return pl.pallas_call(
            body, name=name, grid=(r // tm,), in_specs=[row, row, row, row, _mt_spec(d, nct), vec, _mt_spec(d, nct)],
            out_specs=(row, row, vec, _acc_spec(d, nct), _acc_spec(d, nct), _acc_spec(d, nct)),
            out_shape=(xs, jax.ShapeDtypeStruct((r, d), MXU), jax.ShapeDtypeStruct((1, d), F32), acc, acc, acc),
            compiler_params=_cp(("arbitrary",)))(dxres, dh, xn, y, mtg, nw, mtn)
    dxn, dnw, dsh, dsc = pl.pallas_call(
        body, name=name, grid=(r // tm,), in_specs=[row, row, row, vec, _mt_spec(d, nct)],
        out_specs=(row, vec, _acc_spec(d, nct), _acc_spec(d, nct)),
        out_shape=(xs, jax.ShapeDtypeStruct((1, d), F32), acc, acc), compiler_params=_cp(("arbitrary",)))(dxres, dh, xn, nw, mtn)
    return dxn, None, dnw, dsh, dsc, None


def final_node(cfg, xp, y, mtg, gi, fnw, tgt, *, name):
    r, d = xp.shape
    tm, nct = _rows(cfg)
    row = pl.BlockSpec((tm, d), lambda i: (i, 0))
    vec = pl.BlockSpec((1, d), lambda i: (0, 0))

    def norm(x, w):
        return x * lax.rsqrt(jnp.mean(x * x, axis=-1, keepdims=True) + NORM_EPS) * w

    def body(xp_ref, y_ref, mtg_ref, fnw_ref, tgt_ref, loss_ref, dx_ref, dy_ref, dg_ref, dfnw_ref):
        i = pl.program_id(0)
        g = mtg_ref[:, gi * d:(gi + 1) * d]
        x = xp_ref[...] + _pat(y_ref[...], g, jnp.multiply)
        out, vjp = jax.vjp(norm, x, fnw_ref[...])
        lat = i >= nct
        err = jnp.where(lat, out - tgt_ref[...], 0.0)
        dx, dfnw = vjp(err * (1.0 / d))

        @pl.when(i == 0)
        def _():
            loss_ref[...] = jnp.zeros_like(loss_ref)
            dfnw_ref[...] = jnp.zeros_like(dfnw_ref)

        @pl.when((i == 0) | (i == nct))
        def _():
            dg_ref[...] = jnp.zeros_like(dg_ref)

        loss_ref[...] += jnp.full(loss_ref.shape, 0.5 / d * jnp.sum(err * err), F32)
        dfnw_ref[...] += dfnw
        dx_ref[...] = dx
        dy_ref[...] = _pat(dx, g, jnp.multiply).astype(MXU)
        dg_ref[...] += jnp.sum((dx * y_ref[...]).reshape(tm // 8, 8, d), axis=0)

    return pl.pallas_call(
        body, name=name, grid=(r // tm,),
        in_specs=[row, row, _mt_spec(d, nct), vec, pl.BlockSpec((tm, d), lambda i: (jnp.maximum(i - nct, 0), 0))],
        out_specs=(pl.BlockSpec((8, 128), lambda i: (0, 0)), row, row, _acc_spec(d, nct), vec),
        out_shape=(jax.ShapeDtypeStruct((8, 128), F32), jax.ShapeDtypeStruct((r, d), F32), jax.ShapeDtypeStruct((r, d), MXU),
                   jax.ShapeDtypeStruct((16, d), F32), jax.ShapeDtypeStruct((1, d), F32)),
        compiler_params=_cp(("arbitrary",)))(xp, y, mtg, fnw, tgt)


def _silu(x):
    return x * jax.nn.sigmoid(x)


def mod_fwd(c16, w, b, *, name):
    d, n = w.shape
    tn = _tile(n, 1536)

    def body(c_ref, w_ref, b_ref, o_ref, s_ref):
        s = _silu(c_ref[...])
        s_ref[...] = s.astype(MXU)
        o_ref[...] = _dot(s, w_ref[...]) + b_ref[...]

    return pl.pallas_call(
        body, name=name, grid=(n // tn,),
        in_specs=[pl.BlockSpec((16, d), lambda j: (0, 0)), pl.BlockSpec((d, tn), lambda j: (0, j)), pl.BlockSpec((1, tn), lambda j: (0, j))],
        out_specs=(pl.BlockSpec((16, tn), lambda j: (0, j)), pl.BlockSpec((16, d), lambda j: (0, 0))),
        out_shape=(jax.ShapeDtypeStruct((16, n), F32), jax.ShapeDtypeStruct((16, d), MXU)),
        compiler_params=_cp(("arbitrary",)))(c16, w, b)


def colsum(x, *, name):
    def body(x_ref, o_ref):
        o_ref[...] = jnp.sum(x_ref[...], axis=0, keepdims=True)

    return pl.pallas_call(body, name=name, out_shape=jax.ShapeDtypeStruct((1, x.shape[1]), F32))(x)


def cctx_grad(c16, ds_list, *, name):
    def body(c_ref, *refs):
        o_ref = refs[-1]
        ds = refs[0][...]
        for r_ in refs[1:-1]:
            ds = ds + r_[...]
        _, vjp = jax.vjp(_silu, c_ref[...])
        (dc,) = vjp(ds)
        o_ref[...] = jnp.sum(dc[0:8], axis=0, keepdims=True)

    return pl.pallas_call(body, name=name, out_shape=jax.ShapeDtypeStruct((1, c16.shape[1]), F32))(c16, *ds_list)


def _s5_disc(lam_re, lam_im, log_step, b_re, b_im):
    lr = jnp.minimum(lam_re, S5_LAM_RE_MAX)
    li = lam_im
    dt = jnp.exp(log_step)
    mag = jnp.exp(lr * dt)
    abar_r = mag * jnp.cos(li * dt)
    abar_i = mag * jnp.sin(li * dt)
    den = lr * lr + li * li
    nr = abar_r - 1.0
    coef_r = (nr * lr + abar_i * li) / den
    coef_i = (abar_i * lr - nr * li) / den
    bbar_r = coef_r[:, None, :] * b_re - coef_i[:, None, :] * b_im
    bbar_i = coef_r[:, None, :] * b_im + coef_i[:, None, :] * b_re
    return abar_r, abar_i, bbar_r, bbar_i


def s5_disc_fwd(lam_re, lam_im, log_step, b_re, b_im, *, name):
    def body(lr, li, ls, br, bi, ar_o, ai_o, br_o, bi_o):
        ar_o[...], ai_o[...], br_o[...], bi_o[...] = _s5_disc(lr[...], li[...], ls[...], br[...], bi[...])

    s2, s3 = jax.ShapeDtypeStruct(lam_re.shape, F32), jax.ShapeDtypeStruct(b_re.shape, F32)
    return pl.pallas_call(body, name=name, out_shape=(s2, s2, s3, s3))(lam_re, lam_im, log_step, b_re, b_im)


def s5_disc_bwd(lam_re, lam_im, log_step, b_re, b_im, d_ar, d_ai, d_br, d_bi, *, name):
    def body(lr, li, ls, br, bi, dar, dai, dbr, dbi, o_lr, o_li, o_ls, o_br, o_bi):
        _, vjp = jax.vjp(_s5_disc, lr[...], li[...], ls[...], br[...], bi[...])
        o_lr[...], o_li[...], o_ls[...], o_br[...], o_bi[...] = vjp((dar[...], dai[...], dbr[...], dbi[...]))

    s2, s3 = jax.ShapeDtypeStruct(lam_re.shape, F32), jax.ShapeDtypeStruct(b_re.shape, F32)
    return pl.pallas_call(body, name=name, out_shape=(s2, s2, jax.ShapeDtypeStruct(log_step.shape, F32), s3, s3))(
        lam_re, lam_im, log_step, b_re, b_im, d_ar, d_ai, d_br, d_bi)


S5_LANES = 512


def _chunk_order(k, ncc, nch, rev):
    if not rev:
        return k
    return jnp.where(k < ncc, ncc - 1 - k, nch - 1 - (k - ncc))


def _cmul(ar, ai, xr, xi):
    return ar * xr - ai * xi, ar * xi + ai * xr


S5_FWD_ROWS = 256
S5_BWD_ROWS = 256


def _const_spec(a):
    return pl.BlockSpec(a.shape, lambda k: (0,) * a.ndim, pipeline_mode=pl.Buffered(1))


def _shift_steps(x, edge_tile, back):
    n = x.shape[0]
    row = lax.broadcasted_iota(jnp.int32, (8, x.shape[1]), 0)
    edge = pltpu.roll(edge_tile, 4, 0)
    if back:
        y = pltpu.roll(x, 4, 0)
        return jnp.concatenate([jnp.where(row < 4, edge, y[0:8]), y[8:]], axis=0)
    y = pltpu.roll(x, n - 4, 0)
    return jnp.concatenate([y[:n - 8], jnp.where(row >= 4, edge, y[n - 8:])], axis=0)


def s5_scan_fwd(cfg, u, a2_re, a2_im, bre, bim, abre, abim, cre, cim, *, rev, name):
    r, d = u.shape
    ns = a2_re.shape[1]
    kb = d // S5_KIN
    tcr = S5_FWD_ROWS
    n8 = tcr // 8
    q = S5_FWD_ROWS // S5_BWD_ROWS
    seg = n8 // q
    nch, ncc = r // tcr, cfg["rc"] // tcr
    lw = min(S5_LANES, ns)

    def body(u_ref, ar_ref, ai_ref, bre_ref, bim_ref, abre_ref, abim_ref, cre_ref, cim_ref, sre_ref, sim_ref, ere_ref, eim_ref, y_ref,
             st_re, st_im, u_edge):
        @pl.when(pl.program_id(0) == 0)
        def _():
            st_re[...] = jnp.zeros_like(st_re)
            st_im[...] = jnp.zeros_like(st_im)
            u_edge[...] = jnp.zeros_like(u_edge)

        u_ = u_ref[...]
        ub = u_.astype(MXU)
        upb = _shift_steps(u_, u_edge[...], back=not rev).astype(MXU)
        u_edge[...] = u_[0:8] if rev else u_[tcr - 8:tcr]
        for j in range(kb):
            uj, upj = ub[:, j * S5_KIN:(j + 1) * S5_KIN], upb[:, j * S5_KIN:(j + 1) * S5_KIN]
            sre_ref[:, :, j * S5_KST:(j + 1) * S5_KST] = (_dot(uj, bre_ref[j]) + _dot(upj, abre_ref[j])).reshape(n8, 8, S5_KST)
            sim_ref[:, :, j * S5_KST:(j + 1) * S5_KST] = (_dot(uj, bim_ref[j]) + _dot(upj, abim_ref[j])).reshape(n8, 8, S5_KST)
        for c in range(ns // lw):
            sl = slice(c * lw, (c + 1) * lw)
            ar = jnp.broadcast_to(ar_ref[:, sl], (8, lw))
            ai = jnp.broadcast_to(ai_ref[:, sl], (8, lw))

            def step(i, carry, sl=sl, ar=ar, ai=ai):
                sr, si = carry
                ii = n8 - 1 - i if rev else i
                pr, pi = _cmul(ar, ai, sr, si)
                sr, si = pr + sre_ref[ii, :, sl], pi + sim_ref[ii, :, sl]
                sre_ref[ii, :, sl] = sr
                sim_ref[ii, :, sl] = si
                return sr, si

            sr, si = st_re[:, sl], st_im[:, sl]
            for s_ in range(q):
                at = q - 1 - s_ if rev else s_
                ere_ref[at, :, sl] = sr
                eim_ref[at, :, sl] = si
                sr, si = lax.fori_loop(s_ * seg, (s_ + 1) * seg, step, (sr, si))
            st_re[:, sl] = sr
            st_im[:, sl] = si
        for j in range(kb):
            sr = sre_ref[:, :, j * S5_KST:(j + 1) * S5_KST].reshape(tcr, S5_KST)
            si = sim_ref[:, :, j * S5_KST:(j + 1) * S5_KST].reshape(tcr, S5_KST)
            y_ref[:, j * S5_KIN:(j + 1) * S5_KIN] = _dot(sr, cre_ref[j]) - _dot(si, cim_ref[j])

    cidx = functools.partial(_chunk_order, ncc=ncc, nch=nch, rev=rev)
    full = _const_spec
    st = pl.BlockSpec((n8, 8, ns), lambda k: (cidx(k), 0, 0))
    en = pl.BlockSpec((q, 8, ns), lambda k: (cidx(k), 0, 0))
    return pl.pallas_call(
        body, name=name, grid=(nch,),
        in_specs=[pl.BlockSpec((tcr, d), lambda k: (cidx(k), 0)), full(a2_re), full(a2_im), full(bre), full(bim), full(abre), full(abim),
                  full(cre), full(cim)],
        out_specs=(st, st, en, en, pl.BlockSpec((tcr, d), lambda k: (cidx(k), 0))),
        out_shape=(jax.ShapeDtypeStruct((r // 8, 8, ns), F32),) * 2 + (jax.ShapeDtypeStruct((q * nch, 8, ns), F32),) * 2
        + (jax.ShapeDtypeStruct((r, d), F32),),
        scratch_shapes=[pltpu.VMEM((8, ns), F32), pltpu.VMEM((8, ns), F32), pltpu.VMEM((8, d), F32)],
        compiler_params=_cp(("arbitrary",)))(u, a2_re, a2_im, bre, bim, abre, abim, cre, cim)


def s5_scan_bwd(cfg, dyb, sre, sim, ere, eim, a2_re, a2_im, bre, bim, cre, cim, c2re, c2im, du_in, *, rev, name):
    r, d = dyb.shape
    ns = a2_re.shape[1]
    kb = d // S5_KIN
    tcr = S5_BWD_ROWS
    n8 = tcr // 8
    nch, ncc = r // tcr, cfg["rc"] // tcr
    lw = min(S5_LANES, ns)

    def body(dy_ref, sre_ref, sim_ref, ere_ref, eim_ref, ar_ref, ai_ref, bre_ref, bim_ref, cre_ref, cim_ref, c2re_ref, c2im_ref, duin_ref,
             du_ref, gre_ref, gim_ref, dar_ref, dai_ref, g_re, g_im, gc_re, gc_im, dy_edge):
        k = pl.program_id(0)

        @pl.when(k == 0)
        def _():
            gc_re[...] = jnp.zeros_like(gc_re)
            gc_im[...] = jnp.zeros_like(gc_im)
            dar_ref[...] = jnp.zeros_like(dar_ref)
            dai_ref[...] = jnp.zeros_like(dai_ref)
            dy_edge[...] = jnp.zeros_like(dy_edge)

        dy32 = dy_ref[...].astype(F32)
        dy = dy32.astype(MXU)
        dyn = _shift_steps(dy32, dy_edge[...], back=rev).astype(MXU)
        dy_edge[...] = dy32[tcr - 8:tcr] if rev else dy32[0:8]
        for j in range(kb):
            dyj, dynj = dy[:, j * S5_KIN:(j + 1) * S5_KIN], dyn[:, j * S5_KIN:(j + 1) * S5_KIN]
            g_re[:, :, j * S5_KST:(j + 1) * S5_KST] = (_dot(dyj, cre_ref[j], 1, 1) + _dot(dynj, c2re_ref[j], 1, 1)).reshape(n8, 8, S5_KST)
            g_im[:, :, j * S5_KST:(j + 1) * S5_KST] = -(_dot(dyj, cim_ref[j], 1, 1) + _dot(dynj, c2im_ref[j], 1, 1)).reshape(n8, 8, S5_KST)
        first = lax.broadcasted_iota(jnp.int32, (8, lw), 0) < 4
        if rev:
            first = jnp.logical_not(first)
        for c in range(ns // lw):
            sl = slice(c * lw, (c + 1) * lw)
            ar = jnp.broadcast_to(ar_ref[:, sl], (8, lw))
            nai = -jnp.broadcast_to(ai_ref[:, sl], (8, lw))

            def step(i, carry, sl=sl, ar=ar, nai=nai):
                gr, gi, accr, acci = carry
                ii = i if rev else n8 - 1 - i
                pr, pi = _cmul(ar, nai, gr, gi)
                outr, outi = pr + g_re[ii, :, sl], pi + g_im[ii, :, sl]
                g_re[ii, :, sl] = outr
                g_im[ii, :, sl] = outi
                pv = jnp.clip(ii + 1 if rev else ii - 1, 0, n8 - 1)
                at_entry = (ii == n8 - 1) if rev else (ii == 0)
                pvr = jnp.where(at_entry, ere_ref[0, :, sl], sre_ref[pv, :, sl])
                pvi = jnp.where(at_entry, eim_ref[0, :, sl], sim_ref[pv, :, sl])
                spr = pltpu.roll(jnp.where(first, sre_ref[ii, :, sl], pvr), 4, 0)
                spi = pltpu.roll(jnp.where(first, sim_ref[ii, :, sl], pvi), 4, 0)
                accr = accr + outr * spr + outi * spi
                acci = acci + outi * spr - outr * spi
                return outr, outi, accr, acci

            gr, gi, accr, acci = lax.fori_loop(0, n8, step, (gc_re[:, sl], gc_im[:, sl], dar_ref[:, sl], dai_ref[:, sl]))
            gc_re[:, sl] = gr
            gc_im[:, sl] = gi
            dar_ref[:, sl] = accr
            dai_ref[:, sl] = acci
        for j in range(kb):
            gr = g_re[:, :, j * S5_KST:(j + 1) * S5_KST].reshape(tcr, S5_KST)
            gi = g_im[:, :, j * S5_KST:(j + 1) * S5_KST].reshape(tcr, S5_KST)
            gre_ref[:, j * S5_KST:(j + 1) * S5_KST] = gr.astype(MXU)
            gim_ref[:, j * S5_KST:(j + 1) * S5_KST] = gi.astype(MXU)
            du_ref[:, j * S5_KIN:(j + 1) * S5_KIN] = (duin_ref[:, j * S5_KIN:(j + 1) * S5_KIN]
                                                     + _dot(gr, bre_ref[j], 1, 1) + _dot(gi, bim_ref[j], 1, 1))

    def cidx(k):
        return _chunk_order(nch - 1 - k, ncc, nch, rev)

    full = _const_spec
    st = pl.BlockSpec((n8, 8, ns), lambda k: (cidx(k), 0, 0))
    en = pl.BlockSpec((1, 8, ns), lambda k: (cidx(k), 0, 0))
    rowd = pl.BlockSpec((tcr, d), lambda k: (cidx(k), 0))
    rown = pl.BlockSpec((tcr, ns), lambda k: (cidx(k), 0))
    acc = pl.BlockSpec((8, ns), lambda k: (0, 0))
    return pl.pallas_call(
        body, name=name, grid=(nch,),
        in_specs=[rowd, st, st, en, en, full(a2_re), full(a2_im), full(bre), full(bim), full(cre), full(cim), full(c2re), full(c2im), rowd],
        out_specs=(rowd, rown, rown, acc, acc),
        out_shape=(jax.ShapeDtypeStruct((r, d), F32), jax.ShapeDtypeStruct((r, ns), MXU), jax.ShapeDtypeStruct((r, ns), MXU),
                   jax.ShapeDtypeStruct((8, ns), F32), jax.ShapeDtypeStruct((8, ns), F32)),
        scratch_shapes=[pltpu.VMEM((n8, 8, ns), F32), pltpu.VMEM((n8, 8, ns), F32), pltpu.VMEM((8, ns), F32), pltpu.VMEM((8, ns), F32),
                        pltpu.VMEM((8, d), F32)],
        compiler_params=_cp(("arbitrary",)))(dyb, sre, sim, ere, eim, a2_re, a2_im, bre, bim, cre, cim, c2re, c2im, du_in)


def rowmap(fn, rows_in, vecs_in, outs, accs=(), *, name):
    r = rows_in[0].shape[0]
    tm = _row_tile(r, max(a.shape[1] for a in rows_in))
    nr, nv, no = len(rows_in), len(vecs_in), len(outs)

    def body(*refs):
        ins = [x[...] for x in refs[:nr + nv]]
        res = fn(*ins)
        if not isinstance(res, (tuple, list)):
            res = (res,)
        out_refs = refs[nr + nv:]
        for o_ref, v in zip(out_refs[:no], res[:no]):
            o_ref[...] = v.astype(o_ref.dtype)
        if accs:
            @pl.when(pl.program_id(0) == 0)
            def _():
                for a_ref in out_refs[no:]:
                    a_ref[...] = jnp.zeros_like(a_ref)
            for a_ref, v in zip(out_refs[no:], res[no:]):
                a_ref[...] += v

    in_specs = [pl.BlockSpec((tm, a.shape[1]), lambda i: (i, 0)) for a in rows_in]
    in_specs += [pl.BlockSpec(v.shape, lambda i, n=v.ndim: (0,) * n) for v in vecs_in]
    out_specs = [pl.BlockSpec((tm, w), lambda i: (i, 0)) for w, _ in outs] + [pl.BlockSpec(s, lambda i, n=len(s): (0,) * n) for s in accs]
    out_shape = [jax.ShapeDtypeStruct((r, w), dt) for w, dt in outs] + [jax.ShapeDtypeStruct(s, F32) for s in accs]
    res = pl.pallas_call(body, name=name, grid=(r // tm,), in_specs=in_specs, out_specs=tuple(out_specs), out_shape=tuple(out_shape),
                         compiler_params=_cp(("arbitrary",) if accs else ("parallel",)))(*rows_in, *vecs_in)
    return res


def _gelu(x):
    return jax.nn.gelu(x, approximate=True)


def _hg_lower_bound(e0, e1):
    m = jnp.maximum(e0, e1)
    a, b = jnp.exp(e0 - m), jnp.exp(e1 - m)
    return b / (a + b)


def _hg_gates(x, lb):
    logf = jnp.log(lb + (1.0 - lb) * jax.nn.sigmoid(x))
    return logf, (1.0 - lb) * jax.nn.sigmoid(-x)


def _hg_masks(rev):
    n = CHUNK_ROWS
    rr = lax.broadcasted_iota(jnp.int32, (n, n), 0)
    ss = lax.broadcasted_iota(jnp.int32, (n, n), 1)
    same = (rr % NB) == (ss % NB)
    causal = same & ((ss >= rr) if rev else (ss <= rr))
    anti = same & ((ss <= rr) if rev else (ss >= rr))
    end0 = 0 if rev else n - NB
    pick_end = ss == (end0 + rr % NB)
    return same, causal, anti, pick_end, end0


def _hg_expand(x):
    ex = lax.broadcasted_iota(jnp.int32, x.shape, 0) % NB
    return jnp.concatenate([jnp.where(ex == b, x, 0.0) for b in range(NB)], axis=1)


def _hg_fold(xe):
    kk = xe.shape[1] // NB
    ex = lax.broadcasted_iota(jnp.int32, (xe.shape[0], kk), 0) % NB
    out = jnp.zeros((xe.shape[0], kk), F32)
    for b in range(NB):
        out = out + jnp.where(ex == b, xe[:, b * kk:(b + 1) * kk], 0.0)
    return out


def _hg_chunk(q, v, x, lb, masks):
    same, causal, anti, pick_end, end0 = masks
    logf, kk = _hg_gates(x, lb)
    b = _dot3(causal.astype(MXU), logf)
    bend_t = _dot3(pick_end.astype(MXU), b)
    bend_flat = jnp.concatenate([b[end0 + i:end0 + i + 1] for i in range(NB)], axis=1)
    eb = jnp.exp(b)
    enb = jnp.exp(-b)
    ee = jnp.exp(bend_t - b)
    qd, kd, ke = q * eb, kk * enb, kk * ee
    att = jnp.where(causal, _dot(qd, kd, 1, 1), 0.0)
    decay = jnp.exp(bend_flat)
    return dict(same=same, causal=causal, anti=anti, logf=logf, kk=kk, b=b, eb=eb, enb=enb, ee=ee, qd=qd, kd=kd, ke=ke, att=att,
                decay=decay, qde=_hg_expand(qd), kee=_hg_expand(ke))


def _hg_chunk_order(cfg, r):
    nch, ncc = r // CHUNK_ROWS, cfg["rc"] // CHUNK_ROWS
    return nch, ncc


def hg_scan_fwd(cfg, z, lb, *, d_dir, name):
    r = z.shape[0]
    d = z.shape[1] // N_PROJ
    nh = d // HEAD
    rev = d_dir == 1
    nch, ncc = _hg_chunk_order(cfg, r)
    n = CHUNK_ROWS

    def body(q_ref, v_ref, x_ref, lb_ref, o_ref, sin_ref, stk):
        @pl.when(pl.program_id(0) == 0)
        def _():
            stk[...] = jnp.zeros_like(stk)

        masks = _hg_masks(rev)
        for h in range(nh):
            sl = slice(h * HEAD, (h + 1) * HEAD)
            s0 = stk[h]
            sin_ref[0, h] = s0
            v = v_ref[:, sl]
            c = _hg_chunk(q_ref[:, sl], v, x_ref[:, sl], lb_ref[:, sl], masks)
            o_ref[:, sl] = _dot(c["att"], v) + _dot(c["qde"], s0, 1, 1)
            stk[h] = s0 * c["decay"] + _dot(v, c["kee"], 0, 0)

    def cidx(k):
        return _chunk_order(k, ncc, nch, rev)

    blk = lambda p: pl.BlockSpec((n, d), lambda k: (cidx(k), p))
    return pl.pallas_call(
        body, name=name, grid=(nch,),
        in_specs=[blk(0), blk(1), blk(2 + d_dir), pl.BlockSpec((1, d), lambda k: (0, 0))],
        out_specs=(blk(0), pl.BlockSpec((1, nh, HEAD, NB * HEAD), lambda k: (cidx(k), 0, 0, 0))),
        out_shape=(jax.ShapeDtypeStruct((r, d), F32), jax.ShapeDtypeStruct((nch, nh, HEAD, NB * HEAD), F32)),
        scratch_shapes=[pltpu.VMEM((nh, HEAD, NB * HEAD), F32)], compiler_params=_cp(("arbitrary",)))(z, z, z, lb)


def hg_scan_bwd(cfg, do, z, lb, sin, dq_in, dv_in, *, d_dir, name):
    r = z.shape[0]
    d = z.shape[1] // N_PROJ
    nh = d // HEAD
    rev = d_dir == 1
    nch, ncc = _hg_chunk_order(cfg, r)
    n = CHUNK_ROWS
    has_in = dq_in is not None

    def body(*refs):
        if has_in:
            do_ref, q_ref, v_ref, x_ref, lb_ref, sin_ref, dqi_ref, dvi_ref, dq_ref, dv_ref, dx_ref, dlb_ref, dstk = refs
        else:
            do_ref, q_ref, v_ref, x_ref, lb_ref, sin_ref, dq_ref, dv_ref, dx_ref, dlb_ref, dstk = refs
        @pl.when(pl.program_id(0) == 0)
        def _():
            dstk[...] = jnp.zeros_like(dstk)
            dlb_ref[...] = jnp.zeros_like(dlb_ref)

        masks = _hg_masks(rev)
        ex = lax.broadcasted_iota(jnp.int32, (n, HEAD), 0) % NB
        for h in range(nh):
            sl = slice(h * HEAD, (h + 1) * HEAD)
            do_, q, v, x, lb_, s0, ds1 = do_ref[:, sl], q_ref[:, sl], v_ref[:, sl], x_ref[:, sl], lb_ref[:, sl], sin_ref[0, h], dstk[h]
            c = _hg_chunk(q, v, x, lb_, masks)
            datt = jnp.where(c["causal"], _dot(do_, v, 1, 1), 0.0)
            dv = _dot(c["att"], do_, 0, 0) + _dot(c["kee"], ds1, 1, 1)
            dqd = _dot(datt, c["kd"]) + _hg_fold(_dot(do_, s0))
            dkd = _dot(datt, c["qd"], 0, 0)
            dke = _hg_fold(_dot(v, ds1))
            dbend_flat = jnp.sum(ds1 * s0, axis=0, keepdims=True) * c["decay"]
            dstk[h] = _dot(do_, c["qde"], 0, 0) + ds1 * c["decay"]
            dq = dqd * c["eb"]
            dk = dkd * c["enb"] + dke * c["ee"]
            db = dqd * c["qd"] - dkd * c["kd"] - dke * c["ke"]
            dbend_rows = jnp.zeros((n, HEAD), F32)
            for b in range(NB):
                dbend_rows = dbend_rows + jnp.where(ex == b, dbend_flat[:, b * HEAD:(b + 1) * HEAD], 0.0)
            dlogf = _dot3(c["anti"].astype(MXU), db) + _dot3(c["same"].astype(MXU), dke * c["ke"]) + dbend_rows
            _, vjp = jax.vjp(_hg_gates, x, lb_)
            dx, dlb = vjp((dlogf, dk))
            if has_in:
                dq = dq + dqi_ref[:, sl]
                dv = dv + dvi_ref[:, sl]
            dq_ref[:, sl] = dq.astype(dq_ref.dtype)
            dv_ref[:, sl] = dv.astype(dv_ref.dtype)
            dx_ref[:, sl] = dx.astype(dx_ref.dtype)
            dlb_ref[:, sl] += dlb

    def cidx(k):
        return _chunk_order(nch - 1 - k, ncc, nch, rev)

    blk = lambda p: pl.BlockSpec((n, d), lambda k: (cidx(k), p))
    vec = pl.BlockSpec((1, d), lambda k: (0, 0))
    in_specs = [blk(0), blk(0), blk(1), blk(2 + d_dir), vec, pl.BlockSpec((1, nh, HEAD, NB * HEAD), lambda k: (cidx(k), 0, 0, 0))]
    args = [do, z, z, z, lb, sin]
    if has_in:
        in_specs += [blk(0), blk(0)]
        args += [dq_in, dv_in]
    rd = jax.ShapeDtypeStruct((r, d), MXU if has_in else F32)
    return pl.pallas_call(
        body, name=name, grid=(nch,), in_specs=in_specs, out_specs=(blk(0), blk(0), blk(0), vec),
        out_shape=(rd, rd, jax.ShapeDtypeStruct((r, d), MXU), jax.ShapeDtypeStruct((1, d), F32)),
        scratch_shapes=[pltpu.VMEM((nh, HEAD, NB * HEAD), F32)], compiler_params=_cp(("arbitrary",)))(*args)


def _hg_read(o, g, gw):
    on = o * lax.rsqrt(jnp.mean(o * o, axis=-1, keepdims=True) + NORM_EPS) * gw
    return on * jax.nn.sigmoid(g)


def hg_read_fwd(of, ob, z, gw, *, name):
    r, d = of.shape
    nh = d // HEAD
    tm = _row_tile(r)

    def body(of_ref, ob_ref, g_ref, gw_ref, o_ref):
        for h in range(nh):
            sl = slice(h * HEAD, (h + 1) * HEAD)
            o_ref[:, sl] = _hg_read(of_ref[:, sl] + ob_ref[:, sl], g_ref[:, sl], gw_ref[...]).astype(MXU)

    blk = pl.BlockSpec((tm, d), lambda i: (i, 0))
    return pl.pallas_call(
        body, name=name, grid=(r // tm,),
        in_specs=[blk, blk, pl.BlockSpec((tm, d), lambda i: (i, N_PROJ - 1)), pl.BlockSpec((1, HEAD), lambda i: (0, 0))],
        out_specs=blk, out_shape=jax.ShapeDtypeStruct((r, d), MXU), compiler_params=_cp(("parallel",)))(of, ob, z, gw)


def hg_read_bwd(don, of, ob, z, gw, *, name):
    r, d = of.shape
    nh = d // HEAD
    tm = _row_tile(r)

    def body(don_ref, of_ref, ob_ref, g_ref, gw_ref, do_ref, dg_ref, dgw_ref):
        @pl.when(pl.program_id(0) == 0)
        def _():
            dgw_ref[...] = jnp.zeros_like(dgw_ref)

        for h in range(nh):
            sl = slice(h * HEAD, (h + 1) * HEAD)
            _, vjp = jax.vjp(_hg_read, of_ref[:, sl] + ob_ref[:, sl], g_ref[:, sl], gw_ref[...])
            do, dg, dgw = vjp(don_ref[:, sl])
            do_ref[:, sl] = do.astype(MXU)
            dg_ref[:, sl] = dg.astype(MXU)
            dgw_ref[...] += dgw

    blk = pl.BlockSpec((tm, d), lambda i: (i, 0))
    vec = pl.BlockSpec((1, HEAD), lambda i: (0, 0))
    rd = jax.ShapeDtypeStruct((r, d), MXU)
    return pl.pallas_call(
        body, name=name, grid=(r // tm,),
        in_specs=[blk, blk, blk, pl.BlockSpec((tm, d), lambda i: (i, N_PROJ - 1)), vec],
        out_specs=(blk, blk, vec), out_shape=(rd, rd, jax.ShapeDtypeStruct((1, HEAD), F32)),
        compiler_params=_cp(("arbitrary",)))(don, of, ob, z, gw)


FFN_COLS = 256


def _seg_masks(cfg, tr, i):
    t = lax.broadcasted_iota(jnp.int32, (tr, FFN_COLS), 0) // NB
    ctx_steps = cfg["rc"] // NB
    pos = jnp.where(i == 0, t % ctx_steps, t % GRID_W)
    last = jnp.where(i == 0, ctx_steps - 1, GRID_W - 1)
    return pos == 0, pos == last


def _prev(x, start):
    return jnp.where(start, 0.0, pltpu.roll(x, NB, 0))


def _next(x, end):
    return jnp.where(end, 0.0, pltpu.roll(x, x.shape[0] - NB, 0))


def _conv3(u, w, b, start, end):
    return ((b + _prev(u, start) * w[0:1]) + u * w[1:2]) + _next(u, end) * w[2:3]


def ffn_mid_fwd(cfg, u, cw, cb, *, name):
    r, f2 = u.shape
    f = f2 // 2
    tr = cfg["rc"]
    nf = f // FFN_COLS

    def body(ua_ref, ug_ref, wa_ref, wg_ref, ba_ref, bg_ref, o_ref, ca_ref, cg_ref):
        start, end = _seg_masks(cfg, tr, pl.program_id(0))
        a = _conv3(ua_ref[...].astype(F32), wa_ref[...], ba_ref[...], start, end)
        g = _conv3(ug_ref[...].astype(F32), wg_ref[...], bg_ref[...], start, end)
        ca_ref[...] = a.astype(MXU)
        cg_ref[...] = g.astype(MXU)
        o_ref[...] = (_silu(a) * g).astype(MXU)

    ca = lambda rows: pl.BlockSpec((rows, FFN_COLS), lambda i, j: (i if rows == tr else 0, j))
    cg = lambda rows: pl.BlockSpec((rows, FFN_COLS), lambda i, j: (i if rows == tr else 0, j + nf))
    half = jax.ShapeDtypeStruct((r, f), MXU)
    return pl.pallas_call(
        body, name=name, grid=(r // tr, nf), in_specs=[ca(tr), cg(tr), ca(3), cg(3), ca(1), cg(1)], out_specs=(ca(tr), ca(tr), ca(tr)),
        out_shape=(jax.ShapeDtypeStruct((r, f), MXU), half, half), compiler_params=_cp(("parallel", "parallel")))(u, u, cw, cw, cb, cb)


def ffn_mid_bwd(cfg, dact, u, ca, cg, cw, *, name):
    r, f2 = u.shape
    f = f2 // 2
    tr = cfg["rc"]
    nf = f // FFN_COLS

    def body(da_ref, us_ref, ca_ref, cg_ref, ws_ref, du_ref, dcw_ref, dcb_ref):
        i = pl.program_id(1)
        is_a = pl.program_id(0) < nf
        start, end = _seg_masks(cfg, tr, i)

        @pl.when(i == 0)
        def _():
            dcw_ref[...] = jnp.zeros_like(dcw_ref)
            dcb_ref[...] = jnp.zeros_like(dcb_ref)

        def finish(dc):
            us, ws = us_ref[...].astype(F32), ws_ref[...]
            dn, dp = _next(dc, end), _prev(dc, start)
            du_ref[...] = (ws[1:2] * dc + ws[0:1] * dn + ws[2:3] * dp).astype(MXU)
            dcw_ref[...] += jnp.concatenate([jnp.sum(dn * us, axis=0, keepdims=True), jnp.sum(dc * us, axis=0, keepdims=True),
                                             jnp.sum(dp * us, axis=0, keepdims=True)], axis=0)
            dcb_ref[...] += jnp.sum(dc, axis=0, keepdims=True)

        @pl.when(is_a)
        def _():
            cs = ca_ref[...].astype(F32)
            sg = jax.nn.sigmoid(cs)
            finish(da_ref[...].astype(F32) * cg_ref[...].astype(F32) * (sg * (1.0 + cs * (1.0 - sg))))

        @pl.when(jnp.logical_not(is_a))
        def _():
            finish(da_ref[...].astype(F32) * _silu(ca_ref[...].astype(F32)))

    cs_ = lambda rows: pl.BlockSpec((rows, FFN_COLS), lambda j, i: (i if rows == tr else 0, j))
    hf = pl.BlockSpec((tr, FFN_COLS), lambda j, i: (i, j % nf))
    gate = pl.BlockSpec((tr, FFN_COLS), lambda j, i: (jnp.where(j < nf, i, 0), jnp.where(j < nf, j, 0)))
    return pl.pallas_call(
        body, name=name, grid=(2 * nf, r // tr), in_specs=[hf, cs_(tr), hf, gate, cs_(3)], out_specs=(cs_(tr), cs_(3), cs_(1)),
        out_shape=(jax.ShapeDtypeStruct((r, f2), MXU), jax.ShapeDtypeStruct((3, f2), F32), jax.ShapeDtypeStruct((1, f2), F32)),
        compiler_params=_cp(("parallel", "arbitrary")))(dact, u, ca, cg, cw)


def hg_lb_fwd(e0, e1, *, name):
    def body(a, b, o):
        o[...] = _hg_lower_bound(a[...], b[...])

    return pl.pallas_call(body, name=name, out_shape=jax.ShapeDtypeStruct(e0.shape, F32))(e0, e1)


def hg_lb_bwd(e0, e1, dlb, *, name):
    def body(a, b, g, oa, ob):
        _, vjp = jax.vjp(_hg_lower_bound, a[...], b[...])
        oa[...], ob[...] = vjp(g[...])

    s = jax.ShapeDtypeStruct(e0.shape, F32)
    return pl.pallas_call(body, name=name, out_shape=(s, s))(e0, e1, dlb)


def _adamw(w, g, m, v):
    m = ADAM_B1 * m + (1.0 - ADAM_B1) * g
    v = ADAM_B2 * v + (1.0 - ADAM_B2) * jnp.square(g)
    m_hat = m / (1.0 - ADAM_B1 ** ADAM_STEP)
    v_hat = v / (1.0 - ADAM_B2 ** ADAM_STEP)
    delta = -ADAM_LR * (m_hat / (jnp.sqrt(v_hat) + ADAM_EPS) + ADAM_WD * w)
    return delta, m, v


def _as2d(a):
    if a.ndim >= 2 and a.shape[-1] % 128 == 0:
        return a.reshape(-1, a.shape[-1])
    return a.reshape(-1, 128) if a.size % 128 == 0 else a.reshape(1, -1)


def adamw(w, g, m, v, *, name):
    w2 = _as2d(w)
    outs = rowmap(_adamw, [w2, _as2d(g), _as2d(m), _as2d(v)], [], [(w2.shape[1], F32)] * 3, name=name)
    return tuple(o.reshape(w.shape) for o in outs)


HBM_SPEC = pl.BlockSpec(memory_space=pltpu.HBM)


def _place():
    mx, my, mc = lax.axis_index("x"), lax.axis_index("y"), lax.axis_index("c")
    others = [(1 - mx, my), (mx, 1 - my), (1 - mx, 1 - my)]
    return mx, my, mc, others


def chip_allgather(x, *, name):
    def body(x_ref, o_ref, send_sems, recv_sems, local_sem):
        mx, my, mc, others = _place()
        me = 2 * mx + my
        mine = pltpu.make_async_copy(x_ref, o_ref.at[me], local_sem)
        mine.start()
        sends = [pltpu.make_async_remote_copy(src_ref=x_ref, dst_ref=o_ref.at[me], send_sem=send_sems.at[j], recv_sem=recv_sems.at[j],
                                              device_id=(px, py, mc), device_id_type=MESH) for j, (px, py) in enumerate(others)]
        for cp in sends:
            cp.start()
        for j, (px, py) in enumerate(others):
            pltpu.make_async_remote_copy(src_ref=x_ref, dst_ref=o_ref.at[2 * px + py], send_sem=send_sems.at[j], recv_sem=recv_sems.at[j],
                                         device_id=(px, py, mc), device_id_type=MESH).wait_recv()
        for cp in sends:
            cp.wait_send()
        mine.wait()

    return pl.pallas_call(
        body, name=name, out_shape=jax.ShapeDtypeStruct((4,) + x.shape, x.dtype), in_specs=[HBM_SPEC], out_specs=HBM_SPEC,
        scratch_shapes=[pltpu.SemaphoreType.DMA((3,)), pltpu.SemaphoreType.DMA((3,)), pltpu.SemaphoreType.DMA])(x)


def _win(ref, axis, start, size):
    idx = [slice(None)] * len(ref.shape)
    idx[axis] = pl.ds(start, size)
    return ref.at[tuple(idx)]


def _half_axis(shape, ax):
    if shape[0] == 2:
        return 0
    return 2 if ax == 1 else 1


def _cut(shape, axis, parts):
    return shape[:axis] + (shape[axis] // parts,) + shape[axis + 1:]


def _hbm_call(body, arrays, out_shapes, sems, name):
    n_in = len(arrays)
    return pl.pallas_call(body, name=name, out_shape=tuple(out_shapes), in_specs=[HBM_SPEC] * n_in, out_specs=tuple([HBM_SPEC] * len(out_shapes)),
                          scratch_shapes=sems)(*arrays)


def place_shard(shard, ax, chip, dtype, *, name):
    l, r, c = shard.shape
    tr = _row_tile(r, c)
    per_block = (l, r // tr, 1)[ax]

    def omap(li, ri, cref):
        idx = [li, ri, 0]
        idx[ax] = idx[ax] + cref[0] * per_block
        return tuple(idx)

    def body(c_ref, s_ref, o_ref):
        o_ref[...] = s_ref[...].astype(dtype)

    full = shard.shape[:ax] + (4 * shard.shape[ax],) + shard.shape[ax + 1:]
    return pl.pallas_call(
        body, name=name, out_shape=jax.ShapeDtypeStruct(full, dtype),
        grid_spec=pltpu.PrefetchScalarGridSpec(
            num_scalar_prefetch=1, grid=(l, r // tr),
            in_specs=[pl.BlockSpec((1, tr, c), lambda li, ri, cref: (li, ri, 0))], out_specs=pl.BlockSpec((1, tr, c), omap)),
        compiler_params=_cp(("parallel", "parallel")))(chip, shard)


SEM_SPEC = pl.BlockSpec(memory_space=pltpu.SEMAPHORE)
SPLIT_COPY = pltpu.CompilerParams(has_side_effects=pltpu.SideEffectType.DATAFLOW_SIDE_EFFECTING)


def _gather_part(ref, shape, ax, hax, chip, core):
    sz, hs = shape[ax] // 4, shape[hax] // 2
    return _win(_win(ref, ax, chip * sz, sz), hax, core * hs, hs)


def gather_placed_start(arrays, axes, haxes, after, *, name):
    n = len(arrays)

    m = 3 * n

    def body(*refs):
        ins, send_sems, recv_sems = refs[:n], refs[n + 1:n + 1 + m], refs[n + 1 + m:n + 1 + 2 * m]
        token = refs[2 * n + 1 + 2 * m]
        mx, my, mc, others = _place()
        me = 2 * mx + my
        for i in range(n):
            for j, (px, py) in enumerate(others):
                part = _gather_part(ins[i], arrays[i].shape, axes[i], haxes[i], me, mc)
                pltpu.make_async_remote_copy(src_ref=part, dst_ref=part, send_sem=send_sems[3 * i + j], recv_sem=recv_sems[3 * i + j],
                                             device_id=(px, py, mc), device_id_type=MESH).start()
        token[...] = jnp.zeros_like(token)

    hbm = [pltpu.with_memory_space_constraint(a_, pltpu.HBM) for a_ in arrays]
    out = pl.pallas_call(
        body, name=name,
        out_shape=tuple([pltpu.SemaphoreType.DMA(())] * (2 * m)) + tuple(pltpu.HBM(a_.shape, a_.dtype) for a_ in arrays)
        + (jax.ShapeDtypeStruct((8, 128), F32),),
        in_specs=[HBM_SPEC] * n + [pl.BlockSpec(memory_space=pl.ANY)],
        out_specs=tuple([SEM_SPEC] * (2 * m)) + tuple([HBM_SPEC] * n) + (pl.BlockSpec(memory_space=pltpu.VMEM),),
        input_output_aliases={i: 2 * m + i for i in range(n)}, compiler_params=SPLIT_COPY)(*hbm, after)
    return list(out[:m]), list(out[m:2 * m]), list(out[2 * m:2 * m + n]), out[2 * m + n]


def gather_placed_wait(arrays, send_sems, recv_sems, axes, haxes, after, *, name):
    n = len(arrays)

    m = 3 * n

    def body(*refs):
        ins, send_refs, recv_refs = refs[:n], refs[n:n + m], refs[n + m:n + 2 * m]
        mx, my, mc, others = _place()
        me = 2 * mx + my
        for i in range(n):
            for j, (px, py) in enumerate(others):
                cp = pltpu.make_async_remote_copy(
                    src_ref=_gather_part(ins[i], arrays[i].shape, axes[i], haxes[i], me, mc),
                    dst_ref=_gather_part(ins[i], arrays[i].shape, axes[i], haxes[i], 2 * px + py, mc),
                    send_sem=send_refs[3 * i + j], recv_sem=recv_refs[3 * i + j], device_id=(px, py, mc), device_id_type=MESH)
                cp.wait_send()
                cp.wait_recv()

    out = pl.pallas_call(
        body, name=name, out_shape=tuple(pltpu.HBM(a_.shape, a_.dtype) for a_ in arrays),
        in_specs=[HBM_SPEC] * n + [SEM_SPEC] * (2 * m) + [pl.BlockSpec(memory_space=pl.ANY)], out_specs=tuple([HBM_SPEC] * n),
        input_output_aliases={i: i for i in range(n)}, compiler_params=SPLIT_COPY)(*arrays, *send_sems, *recv_sems, after)
    return list(out)


def pair_swap_halves(arrays, haxes, *, name):
    n = len(arrays)

    def body(*refs):
        ins, outs = refs[:n], refs[n:2 * n]
        send_sems, recv_sems = refs[2 * n:]
        mx, my, mc, _ = _place()
        cps = []
        for i in range(n):
            hs = arrays[i].shape[haxes[i]] // 2
            cp = pltpu.make_async_remote_copy(src_ref=_win(ins[i], haxes[i], (1 - mc) * hs, hs), dst_ref=outs[i], send_sem=send_sems.at[i],
                                              recv_sem=recv_sems.at[i], device_id=(mx, my, 1 - mc), device_id_type=MESH)
            cp.start()
            cps.append(cp)
        for cp in cps:
            cp.wait()

    outs = [jax.ShapeDtypeStruct(_cut(a_.shape, h_, 2), a_.dtype) for a_, h_ in zip(arrays, haxes)]
    return _hbm_call(body, arrays, outs, [pltpu.SemaphoreType.DMA((n,)), pltpu.SemaphoreType.DMA((n,))], name)


def add_own_half(g, t, hax, core, *, out_dtype, name):
    l, r, c = t.shape
    tr = _row_tile(r, c)
    per_half = (l, r // tr, 1)[hax]

    def imap(li, ri, cref):
        idx = [li, ri, 0]
        idx[hax] = idx[hax] + cref[0] * per_half
        return tuple(idx)

    def body(c_ref, g_ref, t_ref, o_ref):
        o_ref[...] = (g_ref[...] + t_ref[...]).astype(out_dtype)

    return pl.pallas_call(
        body, name=name, out_shape=jax.ShapeDtypeStruct(t.shape, out_dtype),
        grid_spec=pltpu.PrefetchScalarGridSpec(
            num_scalar_prefetch=1, grid=(l, r // tr),
            in_specs=[pl.BlockSpec((1, tr, c), imap), pl.BlockSpec((1, tr, c), lambda li, ri, cref: (li, ri, 0))],
            out_specs=pl.BlockSpec((1, tr, c), lambda li, ri, cref: (li, ri, 0))),
        compiler_params=_cp(("parallel", "parallel")))(core, g, t)


def exchange_blocks_start(arrays, axes, *, name):
    n = len(arrays)
    lands = [lax.empty((4,) + _cut(a_.shape, ax, 4), a_.dtype) for a_, ax in zip(arrays, axes)]

    def body(*refs):
        ins, lnd = refs[:n], refs[n:2 * n]
        send_sems, recv_sems = refs[2 * n:6 * n], refs[6 * n:9 * n]
        token = refs[11 * n]
        mx, my, mc, others = _place()
        me = 2 * mx + my
        for i in range(n):
            sz = arrays[i].shape[axes[i]] // 4
            pltpu.make_async_copy(_win(ins[i], axes[i], me * sz, sz), lnd[i].at[me], send_sems[4 * i + 3]).start()
            for j, (px, py) in enumerate(others):
                pltpu.make_async_remote_copy(src_ref=_win(ins[i], axes[i], (2 * px + py) * sz, sz), dst_ref=lnd[i].at[me],
                                             send_sem=send_sems[4 * i + j], recv_sem=recv_sems[3 * i + j], device_id=(px, py, mc),
                                             device_id_type=MESH).start()
        token[...] = jnp.zeros_like(token)

    hbm = [pltpu.with_memory_space_constraint(a_, pltpu.HBM) for a_ in arrays + lands]
    out = pl.pallas_call(
        body, name=name,
        out_shape=tuple([pltpu.SemaphoreType.DMA(())] * (7 * n)) + tuple(pltpu.HBM(a_.shape, a_.dtype) for a_ in arrays + lands)
        + (jax.ShapeDtypeStruct((8, 128), F32),),
        in_specs=[HBM_SPEC] * (2 * n),
        out_specs=tuple([SEM_SPEC] * (7 * n)) + tuple([HBM_SPEC] * (2 * n)) + (pl.BlockSpec(memory_space=pltpu.VMEM),),
        input_output_aliases={i: 7 * n + i for i in range(2 * n)}, compiler_params=SPLIT_COPY)(*hbm)
    return list(out[:7 * n]), list(out[7 * n:8 * n]), list(out[8 * n:9 * n]), out[9 * n]


def exchange_blocks_wait(sems, arrays, lands, axes, after, *, name):
    n = len(arrays)

    def body(*refs):
        ins, lnd = refs[:n], refs[n:2 * n]
        send_sems, recv_sems = refs[2 * n:6 * n], refs[6 * n:9 * n]
        mx, my, mc, others = _place()
        me = 2 * mx + my
        for i in range(n):
            sz = arrays[i].shape[axes[i]] // 4
            mine = _win(ins[i], axes[i], me * sz, sz)
            pltpu.make_async_copy(mine, lnd[i].at[me], send_sems[4 * i + 3]).wait()
            for j, (px, py) in enumerate(others):
                cp = pltpu.make_async_remote_copy(src_ref=mine, dst_ref=lnd[i].at[2 * px + py], send_sem=send_sems[4 * i + j],
                                                  recv_sem=recv_sems[3 * i + j], device_id=(px, py, mc), device_id_type=MESH)
                cp.wait_send()
                cp.wait_recv()

    out = pl.pallas_call(
        body, name=name, out_shape=tuple(pltpu.HBM(a_.shape, a_.dtype) for a_ in arrays + lands),
        in_specs=[HBM_SPEC] * (2 * n) + [SEM_SPEC] * (7 * n) + [pl.BlockSpec(memory_space=pl.ANY)] * len(after),
        out_specs=tuple([HBM_SPEC] * (2 * n)), input_output_aliases={i: i for i in range(2 * n)}, compiler_params=SPLIT_COPY)(
            *arrays, *lands, *sems, *after)
    return list(out[n:])


def sum_blocks(e, hax, core, *, name):
    _, l, r, c = e.shape
    tr = _row_tile(r, c)
    per_half = (l, r // tr, 1)[hax]

    def omap(li, ri, cref):
        idx = [li, ri, 0]
        idx[hax] = idx[hax] + cref[0] * per_half
        return tuple(idx)

    def body(c_ref, e_ref, o_ref):
        v = e_ref[...].astype(F32)
        o_ref[...] = ((v[0] + v[1]) + v[2]) + v[3]

    full = (l, r, c)[:hax] + (2 * (l, r, c)[hax],) + (l, r, c)[hax + 1:]
    return pl.pallas_call(
        body, name=name, out_shape=jax.ShapeDtypeStruct(full, F32),
        grid_spec=pltpu.PrefetchScalarGridSpec(
            num_scalar_prefetch=1, grid=(l, r // tr),
            in_specs=[pl.BlockSpec((4, 1, tr, c), lambda li, ri, cref: (0, li, ri, 0))], out_specs=pl.BlockSpec((1, tr, c), omap)),
        compiler_params=_cp(("parallel", "parallel")))(core, e)


def pair_fill_halves(arrays, haxes, *, name):
    n = len(arrays)

    def body(*refs):
        ins, outs = refs[:n], refs[n:2 * n]
        send_sems, recv_sems = refs[2 * n:]
        mx, my, mc, _ = _place()
        cps = []
        for i in range(n):
            hs = arrays[i].shape[haxes[i]] // 2
            mine = _win(ins[i], haxes[i], mc * hs, hs)
            cp = pltpu.make_async_remote_copy(src_ref=mine, dst_ref=_win(outs[i], haxes[i], mc * hs, hs), send_sem=send_sems.at[i],
                                              recv_sem=recv_sems.at[i], device_id=(mx, my, 1 - mc), device_id_type=MESH)
            cp.start()
            cps.append(cp)
        for i in range(n):
            hs = arrays[i].shape[haxes[i]] // 2
            pltpu.make_async_remote_copy(src_ref=_win(ins[i], haxes[i], mc * hs, hs), dst_ref=_win(outs[i], haxes[i], (1 - mc) * hs, hs),
                                         send_sem=send_sems.at[i], recv_sem=recv_sems.at[i], device_id=(mx, my, 1 - mc),
                                         device_id_type=MESH).wait_recv()
        for cp in cps:
            cp.wait_send()

    return pl.pallas_call(
        body, name=name, out_shape=tuple(jax.ShapeDtypeStruct(a_.shape, a_.dtype) for a_ in arrays), in_specs=[HBM_SPEC] * n,
        out_specs=tuple([HBM_SPEC] * n), input_output_aliases={i: i for i in range(n)},
        scratch_shapes=[pltpu.SemaphoreType.DMA((n,)), pltpu.SemaphoreType.DMA((n,))])(*arrays)


WEIGHTS = ['c_ctx', 'w_mod', 'b_mod', 'norm1_w', 'norm2_w', 'final_norm_w', 's5_w_in', 's5_lam_re', 's5_lam_im', 's5_log_step', 's5_b_re', 's5_b_im', 's5_c_re', 's5_c_im', 's5_d', 's5_w_glu', 's5_w_out', 'hg_w_in', 'hg_lower_bounds', 'hg_gnorm_w', 'hg_w_out', 'ffn_w_up', 'ffn_conv_w', 'ffn_conv_b', 'ffn_w_down']
INPUTS = ['x', 'c', 'ctx', 'c_ctx', 'w_mod', 'b_mod', 'norm1_w', 'norm2_w', 'final_norm_w', 's5_w_in', 's5_lam_re', 's5_lam_im', 's5_log_step', 's5_b_re', 's5_b_im', 's5_c_re', 's5_c_im', 's5_d', 's5_w_glu', 's5_w_out', 'hg_w_in', 'hg_lower_bounds', 'hg_gnorm_w', 'hg_w_out', 'ffn_w_up', 'ffn_conv_w', 'ffn_conv_b', 'ffn_w_down', 'loss_target', 'm_c_ctx', 'm_w_mod', 'm_b_mod', 'm_norm1_w', 'm_norm2_w', 'm_final_norm_w', 'm_s5_w_in', 'm_s5_lam_re', 'm_s5_lam_im', 'm_s5_log_step', 'm_s5_b_re', 'm_s5_b_im', 'm_s5_c_re', 'm_s5_c_im', 'm_s5_d', 'm_s5_w_glu', 'm_s5_w_out', 'm_hg_w_in', 'm_hg_lower_bounds', 'm_hg_gnorm_w', 'm_hg_w_out', 'm_ffn_w_up', 'm_ffn_conv_w', 'm_ffn_conv_b', 'm_ffn_w_down', 'v_c_ctx', 'v_w_mod', 'v_b_mod', 'v_norm1_w', 'v_norm2_w', 'v_final_norm_w', 'v_s5_w_in', 'v_s5_lam_re', 'v_s5_lam_im', 'v_s5_log_step', 'v_s5_b_re', 'v_s5_b_im', 'v_s5_c_re', 'v_s5_c_im', 'v_s5_d', 'v_s5_w_glu', 'v_s5_w_out', 'v_hg_w_in', 'v_hg_lower_bounds', 'v_hg_gnorm_w', 'v_hg_w_out', 'v_ffn_w_up', 'v_ffn_conv_w', 'v_ffn_conv_b', 'v_ffn_w_down']
SHARD_AXIS = {"w_mod": 2, "s5_w_in": 1, "s5_w_glu": 1, "s5_w_out": 1, "hg_w_in": 2, "hg_lower_bounds": 2, "hg_w_out": 1,
              "ffn_w_up": 2, "ffn_conv_w": 2, "ffn_w_down": 1}
GATHER_F32 = ("hg_lower_bounds", "ffn_conv_w")
PACK_W = 1024
GRAD_WIRE = jnp.bfloat16


def _reduce_start(items, core, tag):
    names, arrays, axes = [n for n, _, _ in items], [g_ for _, g_, _ in items], [ax for _, _, ax in items]
    haxes = [_half_axis(g_.shape, ax) for g_, ax in zip(arrays, axes)]
    t = pair_swap_halves(arrays, haxes, name="grad_pair_swap_" + tag)
    h = [add_own_half(g_, t_, hx, core, out_dtype=GRAD_WIRE, name="grad_pair_add_" + n) for g_, t_, hx, n in zip(arrays, t, haxes, names)]
    sems, h, lands, token = exchange_blocks_start(h, axes, name="grad_exchange_start_" + tag)
    return (names, sems, h, lands, axes, haxes), token


def _reduce_finish(state, core, after, tag):
    names, sems, h, lands, axes, haxes = state
    e = exchange_blocks_wait(sems, h, lands, axes, list(after), name="grad_exchange_wait_" + tag)
    s = [sum_blocks(e_, hx, core, name="grad_chip_sum_" + n) for e_, hx, n in zip(e, haxes, names)]
    return dict(zip(names, pair_fill_halves(s, haxes, name="grad_pair_fill_" + tag)))


def _pack_small(grads, small):
    flat = jnp.concatenate([grads[n].reshape(-1) for n in small])
    pad = (-flat.shape[0]) % (64 * PACK_W)
    return jnp.pad(flat, (0, pad)).reshape(1, -1, PACK_W)


def _unpack_small(a, block, small):
    sm = chip_allgather(block[0], name="allgather_small_grads").reshape(-1)
    out, off = {}, 0
    for n in small:
        out[n] = sm[off:off + math.prod(a[n].shape)].reshape(a[n].shape)
        off += math.prod(a[n].shape)
    return out


def _blockdiag_b(bb, kb):
    gl = S5_KIN // S5_GROUP
    x = bb.reshape(kb, gl, S5_GROUP, S5_STATE)
    return (x[:, :, :, None, :] * jnp.eye(gl, dtype=bb.dtype)[None, :, None, :, None]).reshape(kb, S5_KIN, S5_KST)


def _blockdiag_c(cc, kb):
    gl = S5_KIN // S5_GROUP
    x = cc.reshape(kb, gl, S5_GROUP, S5_STATE).transpose(0, 1, 3, 2)
    return (x[:, :, :, None, :] * jnp.eye(gl, dtype=cc.dtype)[None, :, None, :, None]).reshape(kb, S5_KST, S5_KIN)


def _diag_b(m, kb):
    gl = S5_KIN // S5_GROUP
    x = m.reshape(kb, gl, S5_GROUP, gl, S5_STATE)
    return jnp.stack([x[:, i, :, i, :] for i in range(gl)], axis=1).reshape(kb * gl, S5_GROUP, S5_STATE)


def _diag_c(m, kb):
    gl = S5_KIN // S5_GROUP
    x = m.reshape(kb, gl, S5_STATE, gl, S5_GROUP)
    return jnp.stack([x[:, i, :, i, :] for i in range(gl)], axis=1).transpose(0, 1, 3, 2).reshape(kb * gl, S5_GROUP, S5_STATE)


def kernel(x, c, ctx, c_ctx, w_mod, b_mod, norm1_w, norm2_w, final_norm_w, s5_w_in, s5_lam_re, s5_lam_im, s5_log_step, s5_b_re, s5_b_im, s5_c_re, s5_c_im, s5_d, s5_w_glu, s5_w_out, hg_w_in, hg_lower_bounds, hg_gnorm_w, hg_w_out, ffn_w_up, ffn_conv_w, ffn_conv_b, ffn_w_down, loss_target, m_c_ctx, m_w_mod, m_b_mod, m_norm1_w, m_norm2_w, m_final_norm_w, m_s5_w_in, m_s5_lam_re, m_s5_lam_im, m_s5_log_step, m_s5_b_re, m_s5_b_im, m_s5_c_re, m_s5_c_im, m_s5_d, m_s5_w_glu, m_s5_w_out, m_hg_w_in, m_hg_lower_bounds, m_hg_gnorm_w, m_hg_w_out, m_ffn_w_up, m_ffn_conv_w, m_ffn_conv_b, m_ffn_w_down, v_c_ctx, v_w_mod, v_b_mod, v_norm1_w, v_norm2_w, v_final_norm_w, v_s5_w_in, v_s5_lam_re, v_s5_lam_im, v_s5_log_step, v_s5_b_re, v_s5_b_im, v_s5_c_re, v_s5_c_im, v_s5_d, v_s5_w_glu, v_s5_w_out, v_hg_w_in, v_hg_lower_bounds, v_hg_gnorm_w, v_hg_w_out, v_ffn_w_up, v_ffn_conv_w, v_ffn_conv_b, v_ffn_w_down):
    a = dict(zip(INPUTS, (x, c, ctx, c_ctx, w_mod, b_mod, norm1_w, norm2_w, final_norm_w, s5_w_in, s5_lam_re, s5_lam_im, s5_log_step, s5_b_re, s5_b_im, s5_c_re, s5_c_im, s5_d, s5_w_glu, s5_w_out, hg_w_in, hg_lower_bounds, hg_gnorm_w, hg_w_out, ffn_w_up, ffn_conv_w, ffn_conv_b, ffn_w_down, loss_target, m_c_ctx, m_w_mod, m_b_mod, m_norm1_w, m_norm2_w, m_final_norm_w, m_s5_w_in, m_s5_lam_re, m_s5_lam_im, m_s5_log_step, m_s5_b_re, m_s5_b_im, m_s5_c_re, m_s5_c_im, m_s5_d, m_s5_w_glu, m_s5_w_out, m_hg_w_in, m_hg_lower_bounds, m_hg_gnorm_w, m_hg_w_out, m_ffn_w_up, m_ffn_conv_w, m_ffn_conv_b, m_ffn_w_down, v_c_ctx, v_w_mod, v_b_mod, v_norm1_w, v_norm2_w, v_final_norm_w, v_s5_w_in, v_s5_lam_re, v_s5_lam_im, v_s5_log_step, v_s5_b_re, v_s5_b_im, v_s5_c_re, v_s5_c_im, v_s5_d, v_s5_w_glu, v_s5_w_out, v_hg_w_in, v_hg_lower_bounds, v_hg_gnorm_w, v_hg_w_out, v_ffn_w_up, v_ffn_conv_w, v_ffn_conv_b, v_ffn_w_down)))
    nb, seq, d = x.shape
    assert nb == NB
    rc = nb * ctx.shape[1]
    cfg = {"rc": rc}
    f = a["ffn_w_down"].shape[1] * 4
    core = lax.axis_index("c").astype(jnp.int32).reshape(1)

    w = {n: a[n] for n in WEIGHTS if n not in SHARD_AXIS}
    chip = (2 * lax.axis_index("x") + lax.axis_index("y")).astype(jnp.int32).reshape(1)
    groups = {
        "now": [("w_mod0", a["w_mod"][0:1]), ("s5_w_in", a["s5_w_in"]), ("hg_lower_bounds", a["hg_lower_bounds"]), ("ffn_conv_w", a["ffn_conv_w"])],
        "mid": [("s5_w_glu", a["s5_w_glu"]), ("s5_w_out", a["s5_w_out"]), ("ffn_w_up0", a["ffn_w_up"][0:1]), ("ffn_w_down0", a["ffn_w_down"][0:1])],
        "later": [("w_mod1", a["w_mod"][1:2]), ("hg_w_in", a["hg_w_in"]), ("hg_w_out", a["hg_w_out"]), ("ffn_w_up1", a["ffn_w_up"][1:2]),
                  ("ffn_w_down1", a["ffn_w_down"][1:2])]}
    shard_axis = lambda n: SHARD_AXIS[n.rstrip("01")]
    placed = {g: [place_shard(s_, shard_axis(n), chip, F32 if n in GATHER_F32 else MXU, name="place_" + n) for n, s_ in it] for g, it in groups.items()}
    axes = {g: [shard_axis(n) for n, _ in it] for g, it in groups.items()}
    haxes = {g: [_half_axis(p_.shape, ax) for p_, ax in zip(placed[g], axes[g])] for g in groups}
    fly_now = gather_placed_start(placed["now"], axes["now"], haxes["now"], chip, name="allgather_now_start")
    fly_mid = gather_placed_start(placed["mid"], axes["mid"], haxes["mid"], fly_now[3], name="allgather_mid_start")
    fly_later = gather_placed_start(placed["later"], axes["later"], haxes["later"], fly_mid[3], name="allgather_later_start")

    def land(fly, g, after):
        send_, recv_, flying, _ = fly
        landed = gather_placed_wait(flying, send_, recv_, axes[g], haxes[g], after, name=f"allgather_{g}_wait")
        w.update(dict(zip([n for n, _ in groups[g]], pair_fill_halves(landed, haxes[g], name=f"allgather_{g}_pair_fill"))))

    tmaj = lambda t: t.transpose(1, 0, 2).reshape(-1, t.shape[-1])
    zero = fly_later[3][0:1, 0:1]
    x0 = jnp.concatenate([tmaj(ctx), tmaj(x)], axis=0)
    tgt = tmaj(a["loss_target"])
    land(fly_now, "now", x0)
    c16 = jnp.concatenate([jnp.broadcast_to(c_ctx[None], (8, d)), c, c], axis=0) + zero
    mt0, scb = mod_fwd(c16, w["w_mod0"][0], w["b_mod"][0][None], name="mod_fwd0")
    mt = [mt0, None]
    n1, n2 = w["norm1_w"], w["norm2_w"]
    w["w_mod"], w["ffn_w_up"], w["ffn_w_down"] = [w["w_mod0"][0], None], [None, None], [None, None]

    def ffn_fwd(l, h):
        u = mm(h, w["ffn_w_up"][l], out_dtype=MXU, name=f"ffn_up{l}")
        act, ca, cg = ffn_mid_fwd(cfg, u, w["ffn_conv_w"][l], w["ffn_conv_b"][l][None], name=f"ffn_mid{l}")
        return (u, ca, cg), act, mm(act, w["ffn_w_down"][l], name=f"ffn_down{l}")

    def ffn_bwd(l, dfo, kept, act, h, zero=0.0):
        dact = mm(dfo, w["ffn_w_down"][l], tb=True, out_dtype=MXU, name=f"ffn_down_dx{l}")
        dwd = mm(act, dfo, ta=True, name=f"ffn_down_dw{l}")
        du, dcw, dcb = ffn_mid_bwd(cfg, dact, *kept, w["ffn_conv_w"][l] + zero, name=f"ffn_mid_bwd{l}")
        dh = mm(du, w["ffn_w_up"][l], tb=True, name=f"ffn_up_dx{l}")
        dwu = mm(h, du, ta=True, name=f"ffn_up_dw{l}")
        return dh, dwu, dcw, dcb[0], dwd

    g_, p_ = d // S5_GROUP, S5_STATE
    ns, kb = g_ * p_, d // S5_KIN
    s5p = (w["s5_lam_re"][0].reshape(2 * g_, p_), w["s5_lam_im"][0].reshape(2 * g_, p_), w["s5_log_step"][0].reshape(2 * g_, 1),
           w["s5_b_re"][0].transpose(0, 1, 3, 2).reshape(2 * g_, S5_GROUP, p_), w["s5_b_im"][0].transpose(0, 1, 3, 2).reshape(2 * g_, S5_GROUP, p_))
    ar, ai, bbr, bbi = s5_disc_fwd(*s5p, name="s5_disc")
    dsk = w["s5_d"]
    _, h1 = node_fwd(cfg, x0, None, None, 0, n1[0:1], mt[0], 0, name="node0a")
    u0 = mm(h1, w["s5_w_in"][0], name="s5_in")
    s5s, ys = [], []
    for dd in range(2):
        sl = slice(dd * g_, (dd + 1) * g_)
        a_r, a_i = ar[sl].reshape(1, ns), ai[sl].reshape(1, ns)
        a2 = (a_r * a_r - a_i * a_i, 2.0 * a_r * a_i)
        b_r, b_i = _blockdiag_b(bbr[sl], kb), _blockdiag_b(bbi[sl], kb)
        c_r, c_i = _blockdiag_c(w["s5_c_re"][0, dd], kb), _blockdiag_c(w["s5_c_im"][0, dd], kb)
        ak, ai_k = a_r.reshape(kb, 1, S5_KST), a_i.reshape(kb, 1, S5_KST)
        ab = (ak * b_r - ai_k * b_i, ak * b_i + ai_k * b_r)
        akc, aic = ak.reshape(kb, S5_KST, 1), ai_k.reshape(kb, S5_KST, 1)
        c2 = (akc * c_r - aic * c_i, akc * c_i + aic * c_r)
        bf = lambda t_: t_.astype(MXU)
        sre, sim, ere, eim, y_ = s5_scan_fwd(cfg, u0, a2[0], a2[1], bf(b_r), bf(b_i), bf(ab[0]), bf(ab[1]), bf(c_r), bf(c_i), rev=dd == 1,
                                             name=f"s5_scan{dd}")
        s5s.append((sre, sim, ere, eim, a2[0], a2[1], bf(b_r), bf(b_i), bf(c_r), bf(c_i), bf(c2[0]), bf(c2[1])))
        ys.append(y_)

    def glu_a(u, y0, y1, ds):
        yp = (ds * u + y0) + y1
        return yp, _gelu(yp)

    ypre, zgb = rowmap(glu_a, [u0, ys[0], ys[1]], [dsk], [(d, F32), (d, MXU)], name="s5_glu_a")
    land(fly_mid, "mid", zgb)
    w["ffn_w_up"][0], w["ffn_w_down"][0] = w["ffn_w_up0"][0], w["ffn_w_down0"][0]
    tg = mm(zgb, w["s5_w_glu"][0], name="s5_glu")
    (z2,) = rowmap(lambda yp, t: _gelu(yp) * jax.nn.sigmoid(t), [ypre, tg], [], [(d, MXU)], name="s5_glu_b")
    y1a = mm(z2, w["s5_w_out"][0], name="s5_out")
    x1a, h2a = node_fwd(cfg, x0, y1a, mt[0], 2, n2[0:1], mt[0], 3, name="node0b")
    ufa, acta, foa = ffn_fwd(0, h2a)

    land(fly_later, "later", foa)
    w["w_mod"][1], w["ffn_w_up"][1], w["ffn_w_down"][1] = w["w_mod1"][0], w["ffn_w_up1"][0], w["ffn_w_down1"][0]
    mt[1], _ = mod_fwd(c16, w["w_mod"][1], w["b_mod"][1][None], name="mod_fwd1")
    x2a, h1b = node_fwd(cfg, x1a, foa, mt[0], 5, n1[1:2], mt[1], 0, name="node1a")
    z = mm(h1b, w["hg_w_in"][0], name="hg_in")
    e0, e1 = w["hg_lower_bounds"][:, 0, :], w["hg_lower_bounds"][:, 1, :]
    lb = hg_lb_fwd(e0, e1, name="hg_lb")
    gw = w["hg_gnorm_w"]
    o0, sin0 = hg_scan_fwd(cfg, z, lb[0:1], d_dir=0, name="hg_scan0")
    o1, sin1 = hg_scan_fwd(cfg, z, lb[1:2], d_dir=1, name="hg_scan1")
    onb = hg_read_fwd(o0, o1, z, gw, name="hg_read")
    y1b = mm(onb, w["hg_w_out"][0], name="hg_out")
    x1b, h2b = node_fwd(cfg, x2a, y1b, mt[1], 2, n2[1:2], mt[1], 3, name="node1b")
    ufb, actb, fob = ffn_fwd(1, h2b)
    loss_p, dx2b, dfob, dg2_1, dfnw = final_node(cfg, x1b, fob, mt[1], 5, w["final_norm_w"][None], tgt, name="final_node")

    gr = {}
    dh2b, dwu1, dcw1, dcb1, dwd1 = ffn_bwd(1, dfob, ufb, actb, h2b)
    dx1b, dy1b, dn2_1, dsh2_1, dsc2_1, dg1_1 = node_bwd(cfg, dx2b, dh2b, x1b, y1b, mt[1], 2, n2[1:2], mt[1], 3, name="node1b_bwd")
    don = mm(dy1b, w["hg_w_out"][0], tb=True, name="hg_out_dx")
    gr["hg_w_out"] = mm(onb, dy1b, ta=True, name="hg_out_dw")[None]
    do_, dgate_, dgw = hg_read_bwd(don, o0, o1, z, gw, name="hg_read_bwd")
    dq, dv, dxf, dlb0 = hg_scan_bwd(cfg, do_, z, lb[0:1], sin0, None, None, d_dir=0, name="hg_scan_bwd0")
    dq, dv, dxb, dlb1 = hg_scan_bwd(cfg, do_, z, lb[1:2], sin1, dq, dv, d_dir=1, name="hg_scan_bwd1")
    dz = [dq, dv, dxf, dxb, dgate_]
    dh1b = mm_cat_nt(dz, w["hg_w_in"][0], name="hg_in_dx")
    gr["hg_w_in"] = mm_tn_cat(h1b, dz, name="hg_in_dw")[None]
    de0, de1 = hg_lb_bwd(e0, e1, jnp.concatenate([dlb0, dlb1], axis=0), name="hg_lb_bwd")
    gr["hg_lower_bounds"] = jnp.stack([de0, de1], axis=1)
    gr["hg_gnorm_w"] = dgw
    dx2a, dfoa, dn1_1, dsh1_1, dsc1_1, dg2_0 = node_bwd(cfg, dx1b, dh1b, x2a, foa, mt[0], 5, n1[1:2], mt[1], 0, name="node1a_bwd")
    dmt1 = jnp.concatenate([dsh1_1, dsc1_1, dg1_1, dsh2_1, dsc2_1, dg2_1], axis=1)
    red1, tok1 = _reduce_start([("hg_w_in", gr["hg_w_in"], 2), ("hg_w_out", gr["hg_w_out"], 1), ("ffn_w_up1", dwu1[None], 2),
                                ("ffn_w_down1", dwd1[None], 1), ("w_mod1", mm(scb, dmt1, ta=True, name="mod_dw1")[None], 2)], core, "layer1")

    dh2a, dwu0, dcw0, dcb0, dwd0 = ffn_bwd(0, dfoa, ufa, acta, h2a, zero=tok1[0:1, 0:1])
    red2, tok2 = _reduce_start([("ffn_w_up0", dwu0[None], 2), ("ffn_w_down0", dwd0[None], 1)], core, "ffn0")
    dx1a, dy1a, dn2_0, dsh2_0, dsc2_0, dg1_0 = node_bwd(cfg, dx2a, dh2a, x1a, y1a, mt[0], 2, n2[0:1] + tok2[0:1, 0:1], mt[0], 3,
                                                        name="node0b_bwd")
    dz2 = mm(dy1a, w["s5_w_out"][0], tb=True, name="s5_out_dx")
    gr["s5_w_out"] = mm(z2, dy1a, ta=True, name="s5_out_dw")[None]

    def glu_b_bwd(dz2_, yp, t):
        zg, sg = _gelu(yp), jax.nn.sigmoid(t)
        return dz2_ * zg * sg * (1.0 - sg), dz2_ * sg

    dtg, dzg_dir = rowmap(glu_b_bwd, [dz2, ypre, tg], [], [(d, MXU), (d, F32)], name="s5_glu_b_bwd")
    dzg_mm = mm(dtg, w["s5_w_glu"][0], tb=True, name="s5_glu_dx")
    gr["s5_w_glu"] = mm(zgb, dtg, ta=True, name="s5_glu_dw")[None]

    def glu_a_bwd(dzd, dzm, yp, u, ds):
        _, vjp = jax.vjp(_gelu, yp)
        (dy,) = vjp(dzd + dzm)
        return dy, dy * ds, jnp.sum(dy * u, axis=0, keepdims=True)

    dyb, du, ddsk = rowmap(glu_a_bwd, [dzg_dir, dzg_mm, ypre, u0], [dsk], [(d, MXU), (d, F32)], [(1, d)], name="s5_glu_a_bwd")
    gr["s5_d"] = ddsk
    dar, dai, dbr, dbi, dcr, dci = [], [], [], [], [], []
    for dd in range(2):
        sre, sim, ere, eim = s5s[dd][:4]
        du, gre, gim, da_r, da_i = s5_scan_bwd(cfg, dyb, *s5s[dd], du, rev=dd == 1, name=f"s5_scan_bwd{dd}")
        dar.append(colsum(da_r, name=f"s5_da_re{dd}").reshape(g_, p_))
        dai.append(colsum(da_i, name=f"s5_da_im{dd}").reshape(g_, p_))
        dbr.append(_diag_b(blockdiag_tn(u0, gre, S5_KIN, S5_KST, name=f"s5_db_re{dd}"), kb))
        dbi.append(_diag_b(blockdiag_tn(u0, gim, S5_KIN, S5_KST, name=f"s5_db_im{dd}"), kb))
        dcr.append(_diag_c(blockdiag_tn(sre.reshape(-1, ns), dyb, S5_KST, S5_KIN, name=f"s5_dc_re{dd}"), kb))
        dci.append(_diag_c(blockdiag_tn(sim.reshape(-1, ns), dyb, S5_KST, S5_KIN, scale=-1.0, name=f"s5_dc_im{dd}"), kb))
    cat = lambda l_: jnp.concatenate(l_, axis=0)
    dlr, dli, dls, dbre, dbim = s5_disc_bwd(*s5p, cat(dar), cat(dai), cat(dbr), cat(dbi), name="s5_disc_bwd")
    gr["s5_lam_re"], gr["s5_lam_im"] = dlr.reshape(1, 2, g_, p_), dli.reshape(1, 2, g_, p_)
    gr["s5_log_step"] = dls.reshape(1, 2, g_)
    gr["s5_b_re"] = dbre.reshape(1, 2, g_, S5_GROUP, p_).transpose(0, 1, 2, 4, 3)
    gr["s5_b_im"] = dbim.reshape(1, 2, g_, S5_GROUP, p_).transpose(0, 1, 2, 4, 3)
    gr["s5_c_re"], gr["s5_c_im"] = jnp.stack(dcr)[None], jnp.stack(dci)[None]
    dh1 = mm(du, w["s5_w_in"][0], tb=True, name="s5_in_dx")
    gr["s5_w_in"] = mm(h1, du, ta=True, name="s5_in_dw")[None]
    dx0, _, dn1_0, dsh1_0, dsc1_0, _ = node_bwd(cfg, dx1a, dh1, x0, None, None, 0, n1[0:1], mt[0], 0, name="node0a_bwd")

    dmt = [jnp.concatenate([dsh1_0, dsc1_0, dg1_0, dsh2_0, dsc2_0, dg2_0], axis=1), dmt1]
    gr["b_mod"] = jnp.concatenate([colsum(dmt[l], name=f"mod_db{l}") for l in range(2)], axis=0)
    dsc16 = [mm(dmt[l], w["w_mod"][l], tb=True, name=f"mod_dx{l}") for l in range(2)]
    gr["c_ctx"] = cctx_grad(c16, dsc16, name="c_ctx_grad")[0]
    gr["norm1_w"] = jnp.concatenate([dn1_0, dn1_1], axis=0)
    gr["norm2_w"] = jnp.concatenate([dn2_0, dn2_1], axis=0)
    gr["final_norm_w"] = dfnw[0]
    gr["ffn_conv_w"], gr["ffn_conv_b"] = jnp.stack([dcw0, dcw1]), jnp.stack([dcb0, dcb1])

    last = [(n, gr[n], SHARD_AXIS[n]) for n in ("s5_w_in", "s5_w_glu", "s5_w_out", "hg_lower_bounds", "ffn_conv_w")]
    last.append(("w_mod0", mm(scb, dmt[0], ta=True, name="mod_dw0")[None], 2))
    small = [n for n in WEIGHTS if n not in SHARD_AXIS]
    last.append(("small", _pack_small(gr, small), 1))
    red3, tok3 = _reduce_start(last, core, "last")
    red = _reduce_finish(red1, core, [tok3], "layer1")
    red.update(_reduce_finish(red2, core, [tok3], "ffn0"))
    red["ffn_w_up"] = jnp.concatenate([red["ffn_w_up0"], red["ffn_w_up1"]], axis=0)
    red["ffn_w_down"] = jnp.concatenate([red["ffn_w_down0"], red["ffn_w_down1"]], axis=0)
    early = ("hg_w_in", "hg_w_out", "ffn_w_up", "ffn_w_down")
    upd = {n: adamw(a[n], red[n], a["m_" + n], a["v_" + n], name="adamw_" + n) for n in early}
    grad_x = dx0[rc:].reshape(seq, nb, d).transpose(1, 0, 2)
    red.update(_reduce_finish(red3, core, [upd[n][0] for n in early] + [grad_x], "last"))
    red.update(_unpack_small(a, red["small"], small))
    red["w_mod"] = jnp.concatenate([red["w_mod0"], red["w_mod1"]], axis=0)
    loss = lax.psum(loss_p[0, 0], ("x", "y", "c"))
    upd.update({n: adamw(a[n], red[n], a["m_" + n], a["v_" + n], name="adamw_" + n) for n in WEIGHTS if n not in early})
    return (loss, grad_x, *[red[n] for n in WEIGHTS], *[upd[n][0] for n in WEIGHTS], *[upd[n][1] for n in WEIGHTS],
            *[upd[n][2] for n in WEIGHTS])
```

```python
import functools
import math

import jax
import jax.numpy as jnp
from jax import lax
from jax.experimental import pallas as pl
from jax.experimental.pallas import tpu as pltpu

F32 = jnp.float32
BF = jnp.bfloat16
MXU = jnp.bfloat16

NORM_EPS = 1e-6
GRID_W = 64
N_MOD = 6
S5_GROUP = 16
S5_STATE = 64
S5_LAM_RE_MAX = -1e-4
S5_KIN = 256
S5_KST = S5_KIN // S5_GROUP * S5_STATE
HEAD = 128
CHUNK_ROWS = 128
N_PROJ = 5
NB = 4
ADAM_LR, ADAM_B1, ADAM_B2, ADAM_EPS, ADAM_WD, ADAM_STEP = 0.001, 0.9, 0.999, 1e-08, 0.01, 10
VMEM_LIMIT = 56 * 1024 * 1024
MESH = pl.DeviceIdType.MESH


def _tile(n, cap):
    if n <= cap:
        return n
    best = None
    for t in range(128, cap + 1, 128):
        if n % t == 0:
            best = t
    assert best is not None, (n, cap)
    return best


def _row_tile(r, width=1024):
    cap = max(8, (512 * 1024) // max(width, 1))
    return next((t for t in (512, 256, 128, 64, 32, 16, 8) if t <= cap and r % t == 0), r)


def _cp(sem):
    return pltpu.CompilerParams(dimension_semantics=sem, vmem_limit_bytes=VMEM_LIMIT)


def _dot(a, b, ca=1, cb=0):
    return lax.dot_general(a.astype(MXU), b.astype(MXU), (((ca,), (cb,)), ((), ())), preferred_element_type=F32)


def _dot3(m, x):
    hi = x.astype(MXU)
    lo = (x - hi.astype(F32)).astype(MXU)
    return _dot(m, hi) + _dot(m, lo)


def mm(a, b, *, ta=False, tb=False, out_dtype=F32, name):
    (kd, m) = a.shape if ta else a.shape[::-1]
    (n, kd2) = b.shape if tb else b.shape[::-1]
    assert kd == kd2, (a.shape, b.shape, ta, tb)
    tm, tn, tk = _tile(m, 1024), _tile(n, 1536), _tile(kd, 1024)
    nk = kd // tk

    def body(a_ref, b_ref, o_ref, acc_ref):
        k = pl.program_id(2)

        @pl.when(k == 0)
        def _():
            acc_ref[...] = jnp.zeros_like(acc_ref)

        acc_ref[...] += _dot(a_ref[...], b_ref[...], 0 if ta else 1, 1 if tb else 0)

        @pl.when(k == nk - 1)
        def _():
            o_ref[...] = acc_ref[...].astype(out_dtype)

    a_spec = pl.BlockSpec((tk, tm), lambda i, j, k: (k, i)) if ta else pl.BlockSpec((tm, tk), lambda i, j, k: (i, k))
    b_spec = pl.BlockSpec((tn, tk), lambda i, j, k: (j, k)) if tb else pl.BlockSpec((tk, tn), lambda i, j, k: (k, j))
    return pl.pallas_call(
        body, name=name, grid=(m // tm, n // tn, nk), in_specs=[a_spec, b_spec],
        out_specs=pl.BlockSpec((tm, tn), lambda i, j, k: (i, j)), out_shape=jax.ShapeDtypeStruct((m, n), out_dtype),
        scratch_shapes=[pltpu.VMEM((tm, tn), F32)], compiler_params=_cp(("parallel", "parallel", "arbitrary")))(a, b)


def mm_cat_nt(parts, b, *, name):
    m, wd = parts[0].shape
    n = b.shape[0]
    np_ = len(parts)
    tm, tn = _tile(m, 1024), _tile(n, 1024)

    def body(*refs):
        b_ref, o_ref, acc_ref = refs[np_], refs[np_ + 1], refs[np_ + 2]
        k = pl.program_id(2)

        @pl.when(k == 0)
        def _():
            acc_ref[...] = jnp.zeros_like(acc_ref)

        for p in range(np_):
            @pl.when(k == p)
            def _(p=p):
                acc_ref[...] += _dot(refs[p][...], b_ref[...], 1, 1)

        @pl.when(k == np_ - 1)
        def _():
            o_ref[...] = acc_ref[...]

    return pl.pallas_call(
        body, name=name, grid=(m // tm, n // tn, np_),
        in_specs=[pl.BlockSpec((tm, wd), lambda i, j, k: (i, 0))] * np_ + [pl.BlockSpec((tn, wd), lambda i, j, k: (j, k))],
        out_specs=pl.BlockSpec((tm, tn), lambda i, j, k: (i, j)), out_shape=jax.ShapeDtypeStruct((m, n), F32),
        scratch_shapes=[pltpu.VMEM((tm, tn), F32)], compiler_params=_cp(("parallel", "parallel", "arbitrary")))(*parts, b)


def mm_tn_cat(a, parts, *, name):
    kd, m = a.shape
    wd = parts[0].shape[1]
    np_ = len(parts)
    tm, tk = _tile(m, 1024), _tile(kd, 1024)
    nk = kd // tk

    def body(*refs):
        a_ref, o_ref, acc_ref = refs[0], refs[np_ + 1], refs[np_ + 2]
        j, k = pl.program_id(1), pl.program_id(2)

        @pl.when(k == 0)
        def _():
            acc_ref[...] = jnp.zeros_like(acc_ref)

        for p in range(np_):
            @pl.when(j == p)
            def _(p=p):
                acc_ref[...] += _dot(a_ref[...], refs[1 + p][...], 0, 0)

        @pl.when(k == nk - 1)
        def _():
            o_ref[...] = acc_ref[...]

    part_spec = lambda p: pl.BlockSpec((tk, wd), lambda i, j, k: (jnp.where(j == p, k, 0), 0))
    return pl.pallas_call(
        body, name=name, grid=(m // tm, np_, nk), in_specs=[pl.BlockSpec((tk, tm), lambda i, j, k: (k, i))] + [part_spec(p) for p in range(np_)],
        out_specs=pl.BlockSpec((tm, wd), lambda i, j, k: (i, j)), out_shape=jax.ShapeDtypeStruct((m, np_ * wd), F32),
        scratch_shapes=[pltpu.VMEM((tm, wd), F32)], compiler_params=_cp(("parallel", "parallel", "arbitrary")))(a, *parts)


def blockdiag_tn(a, b, wa, wb, *, scale=1.0, name):
    rows = a.shape[0]
    kb = a.shape[1] // wa
    tr = _tile(rows, 1024)
    nr = rows // tr

    def body(a_ref, b_ref, o_ref):
        i = pl.program_id(1)

        @pl.when(i == 0)
        def _():
            o_ref[...] = jnp.zeros_like(o_ref)

        o_ref[0] += scale * _dot(a_ref[...], b_ref[...], 0, 0)

    return pl.pallas_call(
        body, name=name, grid=(kb, nr),
        in_specs=[pl.BlockSpec((tr, wa), lambda k, i: (i, k)), pl.BlockSpec((tr, wb), lambda k, i: (i, k))],
        out_specs=pl.BlockSpec((1, wa, wb), lambda k, i: (k, 0, 0)), out_shape=jax.ShapeDtypeStruct((kb, wa, wb), F32),
        compiler_params=_cp(("parallel", "arbitrary")))(a, b)


def _pat(v, p, op):
    tm, d = v.shape
    return op(v.reshape(tm // 8, 8, d), p[None]).reshape(tm, d)


def _norm_mod(x, nw, shift, scale):
    y = x * lax.rsqrt(jnp.mean(x * x, axis=-1, keepdims=True) + NORM_EPS) * nw
    return _pat(_pat(y, 1.0 + scale, jnp.multiply), shift, jnp.add)


def _mt_spec(d, nct):
    return pl.BlockSpec((8, N_MOD * d), lambda i: (jnp.where(i < nct, 0, 1), 0))


def _acc_spec(d, nct):
    return pl.BlockSpec((8, d), lambda i: (jnp.where(i < nct, 0, 1), 0))


def _rows(cfg):
    tm = min(512, cfg["rc"])
    return tm, cfg["rc"] // tm


def node_fwd(cfg, xp, y, mtg, gi, nw, mtn, si, *, name):
    r, d = xp.shape
    tm, nct = _rows(cfg)
    row = pl.BlockSpec((tm, d), lambda i: (i, 0))
    vec = pl.BlockSpec((1, d), lambda i: (0, 0))

    def body(*refs):
        if y is None:
            xp_ref, nw_ref, mtn_ref, h_ref = refs
            x = xp_ref[...]
        else:
            xp_ref, y_ref, mtg_ref, nw_ref, mtn_ref, xn_ref, h_ref = refs
            x = xp_ref[...] + _pat(y_ref[...], mtg_ref[:, gi * d:(gi + 1) * d], jnp.multiply)
            xn_ref[...] = x
        h_ref[...] = _norm_mod(x, nw_ref[...], mtn_ref[:, si * d:(si + 1) * d], mtn_ref[:, (si + 1) * d:(si + 2) * d]).astype(MXU)

    h_shape = jax.ShapeDtypeStruct((r, d), MXU)
    if y is None:
        h = pl.pallas_call(body, name=name, grid=(r // tm,), in_specs=[row, vec, _mt_spec(d, nct)], out_specs=row,
                           out_shape=h_shape, compiler_params=_cp(("parallel",)))(xp, nw, mtn)
        return xp, h
    return pl.pallas_call(body, name=name, grid=(r // tm,), in_specs=[row, row, _mt_spec(d, nct), vec, _mt_spec(d, nct)],
                          out_specs=(row, row), out_shape=(jax.ShapeDtypeStruct((r, d), F32), h_shape),
                          compiler_params=_cp(("parallel",)))(xp, y, mtg, nw, mtn)


def node_bwd(cfg, dxres, dh, xn, y, mtg, gi, nw, mtn, si, *, name):
    r, d = xn.shape
    tm, nct = _rows(cfg)
    row = pl.BlockSpec((tm, d), lambda i: (i, 0))
    vec = pl.BlockSpec((1, d), lambda i: (0, 0))
    has_y = y is not None

    def body(*refs):
        if has_y:
            dxres_ref, dh_ref, xn_ref, y_ref, mtg_ref, nw_ref, mtn_ref, dxn_ref, dy_ref, dnw_ref, dsh_ref, dsc_ref, dg_ref = refs
        else:
            dxres_ref, dh_ref, xn_ref, nw_ref, mtn_ref, dxn_ref, dnw_ref, dsh_ref, dsc_ref = refs
        i = pl.program_id(0)
        _, vjp = jax.vjp(_norm_mod, xn_ref[...], nw_ref[...], mtn_ref[:, si * d:(si + 1) * d], mtn_ref[:, (si + 1) * d:(si + 2) * d])
        dx, dnw, dsh, dsc = vjp(dh_ref[...])
        dx = dx + dxres_ref[...]
        dxn_ref[...] = dx

        @pl.when(i == 0)
        def _():
            dnw_ref[...] = jnp.zeros_like(dnw_ref)

        @pl.when((i == 0) | (i == nct))
        def _():
            dsh_ref[...] = jnp.zeros_like(dsh_ref)
            dsc_ref[...] = jnp.zeros_like(dsc_ref)
            if has_y:
                dg_ref[...] = jnp.zeros_like(dg_ref)

        dnw_ref[...] += dnw
        dsh_ref[...] += dsh
        dsc_ref[...] += dsc
        if has_y:
            dy_ref[...] = _pat(dx, mtg_ref[:, gi * d:(gi + 1) * d], jnp.multiply).astype(MXU)
            dg_ref[...] += jnp.sum((dx * y_ref[...]).reshape(tm // 8, 8, d), axis=0)

    acc = jax.ShapeDtypeStruct((16, d), F32)
    xs = jax.ShapeDtypeStruct((r, d), F32)
    if has_y:
        return pl.pallas_call(
            body, name=name, grid=(r // tm,), in_specs=[row, row, row, row, _mt_spec(d, nct), vec, _mt_spec(d, nct)],
            out_specs=(row, row, vec, _acc_spec(d, nct), _acc_spec(d, nct), _acc_spec(d, nct)),
            out_shape=(xs, jax.ShapeDtypeStruct((r, d), MXU), jax.ShapeDtypeStruct((1, d), F32), acc, acc, acc),
            compiler_params=_cp(("arbitrary",)))(dxres, dh, xn, y, mtg, nw, mtn)
    dxn, dnw, dsh, dsc = pl.pallas_call(
        body, name=name, grid=(r // tm,), in_specs=[row, row, row, vec, _mt_spec(d, nct)],
        out_specs=(row, vec, _acc_spec(d, nct), _acc_spec(d, nct)),
        out_shape=(xs, jax.ShapeDtypeStruct((1, d), F32), acc, acc), compiler_params=_cp(("arbitrary",)))(dxres, dh, xn, nw, mtn)
    return dxn, None, dnw, dsh, dsc, None


def final_node(cfg, xp, y, mtg, gi, fnw, tgt, *, name):
    r, d = xp.shape
    tm, nct = _rows(cfg)
    row = pl.BlockSpec((tm, d), lambda i: (i, 0))
    vec = pl.BlockSpec((1, d), lambda i: (0, 0))

    def norm(x, w):
        return x * lax.rsqrt(jnp.mean(x * x, axis=-1, keepdims=True) + NORM_EPS) * w

    def body(xp_ref, y_ref, mtg_ref, fnw_ref, tgt_ref, loss_ref, dx_ref, dy_ref, dg_ref, dfnw_ref):
        i = pl.program_id(0)
        g = mtg_ref[:, gi * d:(gi + 1) * d]
        x = xp_ref[...] + _pat(y_ref[...], g, jnp.multiply)
        out, vjp = jax.vjp(norm, x, fnw_ref[...])
        lat = i >= nct
        err = jnp.where(lat, out - tgt_ref[...], 0.0)
        dx, dfnw = vjp(err * (1.0 / d))

        @pl.when(i == 0)
        def _():
            loss_ref[...] = jnp.zeros_like(loss_ref)
            dfnw_ref[...] = jnp.zeros_like(dfnw_ref)

        @pl.when((i == 0) | (i == nct))
        def _():
            dg_ref[...] = jnp.zeros_like(dg_ref)

        loss_ref[...] += jnp.full(loss_ref.shape, 0.5 / d * jnp.sum(err * err), F32)
        dfnw_ref[...] += dfnw
        dx_ref[...] = dx
        dy_ref[...] = _pat(dx, g, jnp.multiply).astype(MXU)
        dg_ref[...] += jnp.sum((dx * y_ref[...]).reshape(tm // 8, 8, d), axis=0)

    return pl.pallas_call(
        body, name=name, grid=(r // tm,),
        in_specs=[row, row, _mt_spec(d, nct), vec, pl.BlockSpec((tm, d), lambda i: (jnp.maximum(i - nct, 0), 0))],
        out_specs=(pl.BlockSpec((8, 128), lambda i: (0, 0)), row, row, _acc_spec(d, nct), vec),
        out_shape=(jax.ShapeDtypeStruct((8, 128), F32), jax.ShapeDtypeStruct((r, d), F32), jax.ShapeDtypeStruct((r, d), MXU),
                   jax.ShapeDtypeStruct((16, d), F32), jax.ShapeDtypeStruct((1, d), F32)),
        compiler_params=_cp(("arbitrary",)))(xp, y, mtg, fnw, tgt)


def _silu(x):
    return x * jax.nn.sigmoid(x)


def mod_fwd(c16, w, b, *, name):
    d, n = w.shape
    tn = _tile(n, 1536)

    def body(c_ref, w_ref, b_ref, o_ref, s_ref):
        s = _silu(c_ref[...])
        s_ref[...] = s.astype(MXU)
        o_ref[...] = _dot(s, w_ref[...]) + b_ref[...]

    return pl.pallas_call(
        body, name=name, grid=(n // tn,),
        in_specs=[pl.BlockSpec((16, d), lambda j: (0, 0)), pl.BlockSpec((d, tn), lambda j: (0, j)), pl.BlockSpec((1, tn), lambda j: (0, j))],
        out_specs=(pl.BlockSpec((16, tn), lambda j: (0, j)), pl.BlockSpec((16, d), lambda j: (0, 0))),
        out_shape=(jax.ShapeDtypeStruct((16, n), F32), jax.ShapeDtypeStruct((16, d), MXU)),
        compiler_params=_cp(("arbitrary",)))(c16, w, b)


def colsum(x, *, name):
    def body(x_ref, o_ref):
        o_ref[...] = jnp.sum(x_ref[...], axis=0, keepdims=True)

    return pl.pallas_call(body, name=name, out_shape=jax.ShapeDtypeStruct((1, x.shape[1]), F32))(x)


def cctx_grad(c16, ds_list, *, name):
    def body(c_ref, *refs):
        o_ref = refs[-1]
        ds = refs[0][...]
        for r_ in refs[1:-1]:
            ds = ds + r_[...]
        _, vjp = jax.vjp(_silu, c_ref[...])
        (dc,) = vjp(ds)
        o_ref[...] = jnp.sum(dc[0:8], axis=0, keepdims=True)

    return pl.pallas_call(body, name=name, out_shape=jax.ShapeDtypeStruct((1, c16.shape[1]), F32))(c16, *ds_list)


def _s5_disc(lam_re, lam_im, log_step, b_re, b_im):
    lr = jnp.minimum(lam_re, S5_LAM_RE_MAX)
    li = lam_im
    dt = jnp.exp(log_step)
    mag = jnp.exp(lr * dt)
    abar_r = mag * jnp.cos(li * dt)
    abar_i = mag * jnp.sin(li * dt)
    den = lr * lr + li * li
    nr = abar_r - 1.0
    coef_r = (nr * lr + abar_i * li) / den
    coef_i = (abar_i * lr - nr * li) / den
    bbar_r = coef_r[:, None, :] * b_re - coef_i[:, None, :] * b_im
    bbar_i = coef_r[:, None, :] * b_im + coef_i[:, None, :] * b_re
    return abar_r, abar_i, bbar_r, bbar_i


def s5_disc_fwd(lam_re, lam_im, log_step, b_re, b_im, *, name):
    def body(lr, li, ls, br, bi, ar_o, ai_o, br_o, bi_o):
        ar_o[...], ai_o[...], br_o[...], bi_o[...] = _s5_disc(lr[...], li[...], ls[...], br[...], bi[...])

    s2, s3 = jax.ShapeDtypeStruct(lam_re.shape, F32), jax.ShapeDtypeStruct(b_re.shape, F32)
    return pl.pallas_call(body, name=name, out_shape=(s2, s2, s3, s3))(lam_re, lam_im, log_step, b_re, b_im)


def s5_disc_bwd(lam_re, lam_im, log_step, b_re, b_im, d_ar, d_ai, d_br, d_bi, *, name):
    def body(lr, li, ls, br, bi, dar, dai, dbr, dbi, o_lr, o_li, o_ls, o_br, o_bi):
        _, vjp = jax.vjp(_s5_disc, lr[...], li[...], ls[...], br[...], bi[...])
        o_lr[...], o_li[...], o_ls[...], o_br[...], o_bi[...] = vjp((dar[...], dai[...], dbr[...], dbi[...]))

    s2, s3 = jax.ShapeDtypeStruct(lam_re.shape, F32), jax.ShapeDtypeStruct(b_re.shape, F32)
    return pl.pallas_call(body, name=name, out_shape=(s2, s2, jax.ShapeDtypeStruct(log_step.shape, F32), s3, s3))(
        lam_re, lam_im, log_step, b_re, b_im, d_ar, d_ai, d_br, d_bi)


S5_LANES = 1024


def _chunk_order(k, ncc, nch, rev):
    if not rev:
        return k
    return jnp.where(k < ncc, ncc - 1 - k, nch - 1 - (k - ncc))


def _cmul(ar, ai, xr, xi):
    return ar * xr - ai * xi, ar * xi + ai * xr


S5_FWD_ROWS = 256
S5_BWD_ROWS = 256


def _const_spec(a):
    return pl.BlockSpec(a.shape, lambda k: (0,) * a.ndim, pipeline_mode=pl.Buffered(1))


def _shift_steps(x, edge_tile, back):
    n = x.shape[0]
    row = lax.broadcasted_iota(jnp.int32, (8, x.shape[1]), 0)
    edge = pltpu.roll(edge_tile, 4, 0)
    if back:
        y = pltpu.roll(x, 4, 0)
        return jnp.concatenate([jnp.where(row < 4, edge, y[0:8]), y[8:]], axis=0)
    y = pltpu.roll(x, n - 4, 0)
    return jnp.concatenate([y[:n - 8], jnp.where(row >= 4, edge, y[n - 8:])], axis=0)


def s5_scan_fwd(cfg, u, a2_re, a2_im, bre, bim, abre, abim, cre, cim, *, rev, name):
    r, d = u.shape
    ns = a2_re.shape[1]
    kb = d // S5_KIN
    tcr = S5_FWD_ROWS
    n8 = tcr // 8
    q = S5_FWD_ROWS // S5_BWD_ROWS
    seg = n8 // q
    nch, ncc = r // tcr, cfg["rc"] // tcr
    lw = min(S5_LANES, ns)

    def body(u_ref, ar_ref, ai_ref, bre_ref, bim_ref, abre_ref, abim_ref, cre_ref, cim_ref, sre_ref, sim_ref, ere_ref, eim_ref, y_ref,
             st_re, st_im, u_edge):
        @pl.when(pl.program_id(0) == 0)
        def _():
            st_re[...] = jnp.zeros_like(st_re)
            st_im[...] = jnp.zeros_like(st_im)
            u_edge[...] = jnp.zeros_like(u_edge)

        u_ = u_ref[...]
        ub = u_.astype(MXU)
        upb = _shift_steps(u_, u_edge[...], back=not rev).astype(MXU)
        u_edge[...] = u_[0:8] if rev else u_[tcr - 8:tcr]
        for j in range(kb):
            uj, upj = ub[:, j * S5_KIN:(j + 1) * S5_KIN], upb[:, j * S5_KIN:(j + 1) * S5_KIN]
            sre_ref[:, :, j * S5_KST:(j + 1) * S5_KST] = (_dot(uj, bre_ref[j]) + _dot(upj, abre_ref[j])).reshape(n8, 8, S5_KST)
            sim_ref[:, :, j * S5_KST:(j + 1) * S5_KST] = (_dot(uj, bim_ref[j]) + _dot(upj, abim_ref[j])).reshape(n8, 8, S5_KST)
        for c in range(ns // lw):
            sl = slice(c * lw, (c + 1) * lw)
            ar = jnp.broadcast_to(ar_ref[:, sl], (8, lw))
            ai = jnp.broadcast_to(ai_ref[:, sl], (8, lw))

            def step(i, carry, sl=sl, ar=ar, ai=ai):
                sr, si = carry
                ii = n8 - 1 - i if rev else i
                pr, pi = _cmul(ar, ai, sr, si)
                sr, si = pr + sre_ref[ii, :, sl], pi + sim_ref[ii, :, sl]
                sre_ref[ii, :, sl] = sr
                sim_ref[ii, :, sl] = si
                return sr, si

            sr, si = st_re[:, sl], st_im[:, sl]
            for s_ in range(q):
                at = q - 1 - s_ if rev else s_
                ere_ref[at, :, sl] = sr
                eim_ref[at, :, sl] = si
                sr, si = lax.fori_loop(s_ * seg, (s_ + 1) * seg, step, (sr, si))
            st_re[:, sl] = sr
            st_im[:, sl] = si
        for j in range(kb):
            sr = sre_ref[:, :, j * S5_KST:(j + 1) * S5_KST].reshape(tcr, S5_KST)
            si = sim_ref[:, :, j * S5_KST:(j + 1) * S5_KST].reshape(tcr, S5_KST)
            y_ref[:, j * S5_KIN:(j + 1) * S5_KIN] = _dot(sr, cre_ref[j]) - _dot(si, cim_ref[j])

    cidx = functools.partial(_chunk_order, ncc=ncc, nch=nch, rev=rev)
    full = _const_spec
    st = pl.BlockSpec((n8, 8, ns), lambda k: (cidx(k), 0, 0))
    en = pl.BlockSpec((q, 8, ns), lambda k: (cidx(k), 0, 0))
    return pl.pallas_call(
        body, name=name, grid=(nch,),
        in_specs=[pl.BlockSpec((tcr, d), lambda k: (cidx(k), 0)), full(a2_re), full(a2_im), full(bre), full(bim), full(abre), full(abim),
                  full(cre), full(cim)],
        out_specs=(st, st, en, en, pl.BlockSpec((tcr, d), lambda k: (cidx(k), 0))),
        out_shape=(jax.ShapeDtypeStruct((r // 8, 8, ns), F32),) * 2 + (jax.ShapeDtypeStruct((q * nch, 8, ns), F32),) * 2
        + (jax.ShapeDtypeStruct((r, d), F32),),
        scratch_shapes=[pltpu.VMEM((8, ns), F32), pltpu.VMEM((8, ns), F32), pltpu.VMEM((8, d), F32)],
        compiler_params=_cp(("arbitrary",)))(u, a2_re, a2_im, bre, bim, abre, abim, cre, cim)


def s5_scan_bwd(cfg, dyb, sre, sim, ere, eim, a2_re, a2_im, bre, bim, cre, cim, c2re, c2im, du_in, *, rev, name):
    r, d = dyb.shape
    ns = a2_re.shape[1]
    kb = d // S5_KIN
    tcr = S5_BWD_ROWS
    n8 = tcr // 8
    nch, ncc = r // tcr, cfg["rc"] // tcr
    lw = min(S5_LANES, ns)

    def body(dy_ref, sre_ref, sim_ref, ere_ref, eim_ref, ar_ref, ai_ref, bre_ref, bim_ref, cre_ref, cim_ref, c2re_ref, c2im_ref, duin_ref,
             du_ref, gre_ref, gim_ref, dar_ref, dai_ref, g_re, g_im, gc_re, gc_im, dy_edge):
        k = pl.program_id(0)

        @pl.when(k == 0)
        def _():
            gc_re[...] = jnp.zeros_like(gc_re)
            gc_im[...] = jnp.zeros_like(gc_im)
            dar_ref[...] = jnp.zeros_like(dar_ref)
            dai_ref[...] = jnp.zeros_like(dai_ref)
            dy_edge[...] = jnp.zeros_like(dy_edge)

        dy32 = dy_ref[...].astype(F32)
        dy = dy32.astype(MXU)
        dyn = _shift_steps(dy32, dy_edge[...], back=rev).astype(MXU)
        dy_edge[...] = dy32[tcr - 8:tcr] if rev else dy32[0:8]
        for j in range(kb):
            dyj, dynj = dy[:, j * S5_KIN:(j + 1) * S5_KIN], dyn[:, j * S5_KIN:(j + 1) * S5_KIN]
            g_re[:, :, j * S5_KST:(j + 1) * S5_KST] = (_dot(dyj, cre_ref[j], 1, 1) + _dot(dynj, c2re_ref[j], 1, 1)).reshape(n8, 8, S5_KST)
            g_im[:, :, j * S5_KST:(j + 1) * S5_KST] = -(_dot(dyj, cim_ref[j], 1, 1) + _dot(dynj, c2im_ref[j], 1, 1)).reshape(n8, 8, S5_KST)
        first = lax.broadcasted_iota(jnp.int32, (8, lw), 0) < 4
        if rev:
            first = jnp.logical_not(first)
        for c in range(ns // lw):
            sl = slice(c * lw, (c + 1) * lw)
            ar = jnp.broadcast_to(ar_ref[:, sl], (8, lw))
            nai = -jnp.broadcast_to(ai_ref[:, sl], (8, lw))

            def step(i, carry, sl=sl, ar=ar, nai=nai):
                gr, gi, accr, acci = carry
                ii = i if rev else n8 - 1 - i
                pr, pi = _cmul(ar, nai, gr, gi)
                outr, outi = pr + g_re[ii, :, sl], pi + g_im[ii, :, sl]
                g_re[ii, :, sl] = outr
                g_im[ii, :, sl] = outi
                pv = jnp.clip(ii + 1 if rev else ii - 1, 0, n8 - 1)
                at_entry = (ii == n8 - 1) if rev else (ii == 0)
                pvr = jnp.where(at_entry, ere_ref[0, :, sl], sre_ref[pv, :, sl])
                pvi = jnp.where(at_entry, eim_ref[0, :, sl], sim_ref[pv, :, sl])
                spr = pltpu.roll(jnp.where(first, sre_ref[ii, :, sl], pvr), 4, 0)
                spi = pltpu.roll(jnp.where(first, sim_ref[ii, :, sl], pvi), 4, 0)
                accr = accr + outr * spr + outi * spi
                acci = acci + outi * spr - outr * spi
                return outr, outi, accr, acci

            gr, gi, accr, acci = lax.fori_loop(0, n8, step, (gc_re[:, sl], gc_im[:, sl], dar_ref[:, sl], dai_ref[:, sl]))
            gc_re[:, sl] = gr
            gc_im[:, sl] = gi
            dar_ref[:, sl] = accr
            dai_ref[:, sl] = acci
        for j in range(kb):
            gr = g_re[:, :, j * S5_KST:(j + 1) * S5_KST].reshape(tcr, S5_KST)
            gi = g_im[:, :, j * S5_KST:(j + 1) * S5_KST].reshape(tcr, S5_KST)
            gre_ref[:, j * S5_KST:(j + 1) * S5_KST] = gr.astype(MXU)
            gim_ref[:, j * S5_KST:(j + 1) * S5_KST] = gi.astype(MXU)
            du_ref[:, j * S5_KIN:(j + 1) * S5_KIN] = (duin_ref[:, j * S5_KIN:(j + 1) * S5_KIN]
                                                     + _dot(gr, bre_ref[j], 1, 1) + _dot(gi, bim_ref[j], 1, 1))

    def cidx(k):
        return _chunk_order(nch - 1 - k, ncc, nch, rev)

    full = _const_spec
    st = pl.BlockSpec((n8, 8, ns), lambda k: (cidx(k), 0, 0))
    en = pl.BlockSpec((1, 8, ns), lambda k: (cidx(k), 0, 0))
    rowd = pl.BlockSpec((tcr, d), lambda k: (cidx(k), 0))
    rown = pl.BlockSpec((tcr, ns), lambda k: (cidx(k), 0))
    acc = pl.BlockSpec((8, ns), lambda k: (0, 0))
    return pl.pallas_call(
        body, name=name, grid=(nch,),
        in_specs=[rowd, st, st, en, en, full(a2_re), full(a2_im), full(bre), full(bim), full(cre), full(cim), full(c2re), full(c2im), rowd],
        out_specs=(rowd, rown, rown, acc, acc),
        out_shape=(jax.ShapeDtypeStruct((r, d), F32), jax.ShapeDtypeStruct((r, ns), MXU), jax.ShapeDtypeStruct((r, ns), MXU),
                   jax.ShapeDtypeStruct((8, ns), F32), jax.ShapeDtypeStruct((8, ns), F32)),
        scratch_shapes=[pltpu.VMEM((n8, 8, ns), F32), pltpu.VMEM((n8, 8, ns), F32), pltpu.VMEM((8, ns), F32), pltpu.VMEM((8, ns), F32),
                        pltpu.VMEM((8, d), F32)],
        compiler_params=_cp(("arbitrary",)))(dyb, sre, sim, ere, eim, a2_re, a2_im, bre, bim, cre, cim, c2re, c2im, du_in)


def rowmap(fn, rows_in, vecs_in, outs, accs=(), *, name):
    r = rows_in[0].shape[0]
    tm = _row_tile(r, max(a.shape[1] for a in rows_in))
    nr, nv, no = len(rows_in), len(vecs_in), len(outs)

    def body(*refs):
        ins = [x[...] for x in refs[:nr + nv]]
        res = fn(*ins)
        if not isinstance(res, (tuple, list)):
            res = (res,)
        out_refs = refs[nr + nv:]
        for o_ref, v in zip(out_refs[:no], res[:no]):
            o_ref[...] = v.astype(o_ref.dtype)
        if accs:
            @pl.when(pl.program_id(0) == 0)
            def _():
                for a_ref in out_refs[no:]:
                    a_ref[...] = jnp.zeros_like(a_ref)
            for a_ref, v in zip(out_refs[no:], res[no:]):
                a_ref[...] += v

    in_specs = [pl.BlockSpec((tm, a.shape[1]), lambda i: (i, 0)) for a in rows_in]
    in_specs += [pl.BlockSpec(v.shape, lambda i, n=v.ndim: (0,) * n) for v in vecs_in]
    out_specs = [pl.BlockSpec((tm, w), lambda i: (i, 0)) for w, _ in outs] + [pl.BlockSpec(s, lambda i, n=len(s): (0,) * n) for s in accs]
    out_shape = [jax.ShapeDtypeStruct((r, w), dt) for w, dt in outs] + [jax.ShapeDtypeStruct(s, F32) for s in accs]
    res = pl.pallas_call(body, name=name, grid=(r // tm,), in_specs=in_specs, out_specs=tuple(out_specs), out_shape=tuple(out_shape),
                         compiler_params=_cp(("arbitrary",) if accs else ("parallel",)))(*rows_in, *vecs_in)
    return res


def _gelu(x):
    return jax.nn.gelu(x, approximate=True)


def _hg_lower_bound(e0, e1):
    m = jnp.maximum(e0, e1)
    a, b = jnp.exp(e0 - m), jnp.exp(e1 - m)
    return b / (a + b)


def _hg_gates(x, lb):
    logf = jnp.log(lb + (1.0 - lb) * jax.nn.sigmoid(x))
    return logf, (1.0 - lb) * jax.nn.sigmoid(-x)


def _hg_masks(rev):
    n = CHUNK_ROWS
    rr = lax.broadcasted_iota(jnp.int32, (n, n), 0)
    ss = lax.broadcasted_iota(jnp.int32, (n, n), 1)
    same = (rr % NB) == (ss % NB)
    causal = same & ((ss >= rr) if rev else (ss <= rr))
    anti = same & ((ss <= rr) if rev else (ss >= rr))
    end0 = 0 if rev else n - NB
    pick_end = ss == (end0 + rr % NB)
    return same, causal, anti, pick_end, end0


def _hg_expand(x):
    ex = lax.broadcasted_iota(jnp.int32, x.shape, 0) % NB
    return jnp.concatenate([jnp.where(ex == b, x, 0.0) for b in range(NB)], axis=1)


def _hg_fold(xe):
    kk = xe.shape[1] // NB
    ex = lax.broadcasted_iota(jnp.int32, (xe.shape[0], kk), 0) % NB
    out = jnp.zeros((xe.shape[0], kk), F32)
    for b in range(NB):
        out = out + jnp.where(ex == b, xe[:, b * kk:(b + 1) * kk], 0.0)
    return out


def _hg_chunk(q, v, x, lb, masks):
    same, causal, anti, pick_end, end0 = masks
    logf, kk = _hg_gates(x, lb)
    b = _dot3(causal.astype(MXU), logf)
    bend_t = _dot3(pick_end.astype(MXU), b)
    bend_flat = jnp.concatenate([b[end0 + i:end0 + i + 1] for i in range(NB)], axis=1)
    eb = jnp.exp(b)
    enb = jnp.exp(-b)
    ee = jnp.exp(bend_t - b)
    qd, kd, ke = q * eb, kk * enb, kk * ee
    att = jnp.where(causal, _dot(qd, kd, 1, 1), 0.0)
    decay = jnp.exp(bend_flat)
    return dict(same=same, causal=causal, anti=anti, logf=logf, kk=kk, b=b, eb=eb, enb=enb, ee=ee, qd=qd, kd=kd, ke=ke, att=att,
                decay=decay, qde=_hg_expand(qd), kee=_hg_expand(ke))


def _hg_chunk_order(cfg, r):
    nch, ncc = r // CHUNK_ROWS, cfg["rc"] // CHUNK_ROWS
    return nch, ncc


def hg_scan_fwd(cfg, z, lb, *, d_dir, name):
    r = z.shape[0]
    d = z.shape[1] // N_PROJ
    nh = d // HEAD
    rev = d_dir == 1
    nch, ncc = _hg_chunk_order(cfg, r)
    n = CHUNK_ROWS

    def body(q_ref, v_ref, x_ref, lb_ref, o_ref, sin_ref, stk):
        @pl.when(pl.program_id(0) == 0)
        def _():
            stk[...] = jnp.zeros_like(stk)

        masks = _hg_masks(rev)
        for h in range(nh):
            sl = slice(h * HEAD, (h + 1) * HEAD)
            s0 = stk[h]
            sin_ref[0, h] = s0
            v = v_ref[:, sl]
            c = _hg_chunk(q_ref[:, sl], v, x_ref[:, sl], lb_ref[:, sl], masks)
            o_ref[:, sl] = _dot(c["att"], v) + _dot(c["qde"], s0, 1, 1)
            stk[h] = s0 * c["decay"] + _dot(v, c["kee"], 0, 0)

    def cidx(k):
        return _chunk_order(k, ncc, nch, rev)

    blk = lambda p: pl.BlockSpec((n, d), lambda k: (cidx(k), p))
    return pl.pallas_call(
        body, name=name, grid=(nch,),
        in_specs=[blk(0), blk(1), blk(2 + d_dir), pl.BlockSpec((1, d), lambda k: (0, 0))],
        out_specs=(blk(0), pl.BlockSpec((1, nh, HEAD, NB * HEAD), lambda k: (cidx(k), 0, 0, 0))),
        out_shape=(jax.ShapeDtypeStruct((r, d), F32), jax.ShapeDtypeStruct((nch, nh, HEAD, NB * HEAD), F32)),
        scratch_shapes=[pltpu.VMEM((nh, HEAD, NB * HEAD), F32)], compiler_params=_cp(("arbitrary",)))(z, z, z, lb)


def hg_scan_bwd(cfg, do, z, lb, sin, dq_in, dv_in, *, d_dir, name):
    r = z.shape[0]
    d = z.shape[1] // N_PROJ
    nh = d // HEAD
    rev = d_dir == 1
    nch, ncc = _hg_chunk_order(cfg, r)
    n = CHUNK_ROWS
    has_in = dq_in is not None

    def body(*refs):
        if has_in:
            do_ref, q_ref, v_ref, x_ref, lb_ref, sin_ref, dqi_ref, dvi_ref, dq_ref, dv_ref, dx_ref, dlb_ref, dstk = refs
        else:
            do_ref, q_ref, v_ref, x_ref, lb_ref, sin_ref, dq_ref, dv_ref, dx_ref, dlb_ref, dstk = refs
        @pl.when(pl.program_id(0) == 0)
        def _():
            dstk[...] = jnp.zeros_like(dstk)
            dlb_ref[...] = jnp.zeros_like(dlb_ref)

        masks = _hg_masks(rev)
        ex = lax.broadcasted_iota(jnp.int32, (n, HEAD), 0) % NB
        for h in range(nh):
            sl = slice(h * HEAD, (h + 1) * HEAD)
            do_, q, v, x, lb_, s0, ds1 = do_ref[:, sl], q_ref[:, sl], v_ref[:, sl], x_ref[:, sl], lb_ref[:, sl], sin_ref[0, h], dstk[h]
            c = _hg_chunk(q, v, x, lb_, masks)
            datt = jnp.where(c["causal"], _dot(do_, v, 1, 1), 0.0)
            dv = _dot(c["att"], do_, 0, 0) + _dot(c["kee"], ds1, 1, 1)
            dqd = _dot(datt, c["kd"]) + _hg_fold(_dot(do_, s0))
            dkd = _dot(datt, c["qd"], 0, 0)
            dke = _hg_fold(_dot(v, ds1))
            dbend_flat = jnp.sum(ds1 * s0, axis=0, keepdims=True) * c["decay"]
            dstk[h] = _dot(do_, c["qde"], 0, 0) + ds1 * c["decay"]
            dq = dqd * c["eb"]
            dk = dkd * c["enb"] + dke * c["ee"]
            db = dqd * c["qd"] - dkd * c["kd"] - dke * c["ke"]
            dbend_rows = jnp.zeros((n, HEAD), F32)
            for b in range(NB):
                dbend_rows = dbend_rows + jnp.where(ex == b, dbend_flat[:, b * HEAD:(b + 1) * HEAD], 0.0)
            dlogf = _dot3(c["anti"].astype(MXU), db) + _dot3(c["same"].astype(MXU), dke * c["ke"]) + dbend_rows
            _, vjp = jax.vjp(_hg_gates, x, lb_)
            dx, dlb = vjp((dlogf, dk))
            if has_in:
                dq = dq + dqi_ref[:, sl]
                dv = dv + dvi_ref[:, sl]
            dq_ref[:, sl] = dq.astype(dq_ref.dtype)
            dv_ref[:, sl] = dv.astype(dv_ref.dtype)
            dx_ref[:, sl] = dx.astype(dx_ref.dtype)
            dlb_ref[:, sl] += dlb

    def cidx(k):
        return _chunk_order(nch - 1 - k, ncc, nch, rev)

    blk = lambda p: pl.BlockSpec((n, d), lambda k: (cidx(k), p))
    vec = pl.BlockSpec((1, d), lambda k: (0, 0))
    in_specs = [blk(0), blk(0), blk(1), blk(2 + d_dir), vec, pl.BlockSpec((1, nh, HEAD, NB * HEAD), lambda k: (cidx(k), 0, 0, 0))]
    args = [do, z, z, z, lb, sin]
    if has_in:
        in_specs += [blk(0), blk(0)]
        args += [dq_in, dv_in]
    rd = jax.ShapeDtypeStruct((r, d), MXU if has_in else F32)
    return pl.pallas_call(
        body, name=name, grid=(nch,), in_specs=in_specs, out_specs=(blk(0), blk(0), blk(0), vec),
        out_shape=(rd, rd, jax.ShapeDtypeStruct((r, d), MXU), jax.ShapeDtypeStruct((1, d), F32)),
        scratch_shapes=[pltpu.VMEM((nh, HEAD, NB * HEAD), F32)], compiler_params=_cp(("arbitrary",)))(*args)


def _hg_read(o, g, gw):
    on = o * lax.rsqrt(jnp.mean(o * o, axis=-1, keepdims=True) + NORM_EPS) * gw
    return on * jax.nn.sigmoid(g)


def hg_read_fwd(of, ob, z, gw, *, name):
    r, d = of.shape
    nh = d // HEAD
    tm = _row_tile(r)

    def body(of_ref, ob_ref, g_ref, gw_ref, o_ref):
        for h in range(nh):
            sl = slice(h * HEAD, (h + 1) * HEAD)
            o_ref[:, sl] = _hg_read(of_ref[:, sl] + ob_ref[:, sl], g_ref[:, sl], gw_ref[...]).astype(MXU)

    blk = pl.BlockSpec((tm, d), lambda i: (i, 0))
    return pl.pallas_call(
        body, name=name, grid=(r // tm,),
        in_specs=[blk, blk, pl.BlockSpec((tm, d), lambda i: (i, N_PROJ - 1)), pl.BlockSpec((1, HEAD), lambda i: (0, 0))],
        out_specs=blk, out_shape=jax.ShapeDtypeStruct((r, d), MXU), compiler_params=_cp(("parallel",)))(of, ob, z, gw)


def hg_read_bwd(don, of, ob, z, gw, *, name):
    r, d = of.shape
    nh = d // HEAD
    tm = _row_tile(r)

    def body(don_ref, of_ref, ob_ref, g_ref, gw_ref, do_ref, dg_ref, dgw_ref):
        @pl.when(pl.program_id(0) == 0)
        def _():
            dgw_ref[...] = jnp.zeros_like(dgw_ref)

        for h in range(nh):
            sl = slice(h * HEAD, (h + 1) * HEAD)
            _, vjp = jax.vjp(_hg_read, of_ref[:, sl] + ob_ref[:, sl], g_ref[:, sl], gw_ref[...])
            do, dg, dgw = vjp(don_ref[:, sl])
            do_ref[:, sl] = do.astype(MXU)
            dg_ref[:, sl] = dg.astype(MXU)
            dgw_ref[...] += dgw

    blk = pl.BlockSpec((tm, d), lambda i: (i, 0))
    vec = pl.BlockSpec((1, HEAD), lambda i: (0, 0))
    rd = jax.ShapeDtypeStruct((r, d), MXU)
    return pl.pallas_call(
        body, name=name, grid=(r // tm,),
        in_specs=[blk, blk, blk, pl.BlockSpec((tm, d), lambda i: (i, N_PROJ - 1)), vec],
        out_specs=(blk, blk, vec), out_shape=(rd, rd, jax.ShapeDtypeStruct((1, HEAD), F32)),
        compiler_params=_cp(("arbitrary",)))(don, of, ob, z, gw)


FFN_COLS = 256


def _seg_masks(cfg, tr, i):
    t = lax.broadcasted_iota(jnp.int32, (tr, FFN_COLS), 0) // NB
    ctx_steps = cfg["rc"] // NB
    pos = jnp.where(i == 0, t % ctx_steps, t % GRID_W)
    last = jnp.where(i == 0, ctx_steps - 1, GRID_W - 1)
    return pos == 0, pos == last


def _prev(x, start):
    return jnp.where(start, 0.0, pltpu.roll(x, NB, 0))


def _next(x, end):
    return jnp.where(end, 0.0, pltpu.roll(x, x.shape[0] - NB, 0))


def _conv3(u, w, b, start, end):
    return ((b + _prev(u, start) * w[0:1]) + u * w[1:2]) + _next(u, end) * w[2:3]


def ffn_mid_fwd(cfg, u, cw, cb, *, name):
    r, f2 = u.shape
    f = f2 // 2
    tr = cfg["rc"]
    nf = f // FFN_COLS

    def body(ua_ref, ug_ref, wa_ref, wg_ref, ba_ref, bg_ref, o_ref, ca_ref, cg_ref):
        start, end = _seg_masks(cfg, tr, pl.program_id(0))
        a = _conv3(ua_ref[...].astype(F32), wa_ref[...], ba_ref[...], start, end)
        g = _conv3(ug_ref[...].astype(F32), wg_ref[...], bg_ref[...], start, end)
        ca_ref[...] = a.astype(MXU)
        cg_ref[...] = g.astype(MXU)
        o_ref[...] = (_silu(a) * g).astype(MXU)

    ca = lambda rows: pl.BlockSpec((rows, FFN_COLS), lambda i, j: (i if rows == tr else 0, j))
    cg = lambda rows: pl.BlockSpec((rows, FFN_COLS), lambda i, j: (i if rows == tr else 0, j + nf))
    half = jax.ShapeDtypeStruct((r, f), MXU)
    return pl.pallas_call(
        body, name=name, grid=(r // tr, nf), in_specs=[ca(tr), cg(tr), ca(3), cg(3), ca(1), cg(1)], out_specs=(ca(tr), ca(tr), ca(tr)),
        out_shape=(jax.ShapeDtypeStruct((r, f), MXU), half, half), compiler_params=_cp(("parallel", "parallel")))(u, u, cw, cw, cb, cb)


def ffn_mid_bwd(cfg, dact, u, ca, cg, cw, *, name):
    r, f2 = u.shape
    f = f2 // 2
    tr = cfg["rc"]
    nf = f // FFN_COLS

    def body(da_ref, us_ref, ca_ref, cg_ref, ws_ref, du_ref, dcw_ref, dcb_ref):
        i = pl.program_id(1)
        is_a = pl.program_id(0) < nf
        start, end = _seg_masks(cfg, tr, i)

        @pl.when(i == 0)
        def _():
            dcw_ref[...] = jnp.zeros_like(dcw_ref)
            dcb_ref[...] = jnp.zeros_like(dcb_ref)

        def finish(dc):
            us, ws = us_ref[...].astype(F32), ws_ref[...]
            dn, dp = _next(dc, end), _prev(dc, start)
            du_ref[...] = (ws[1:2] * dc + ws[0:1] * dn + ws[2:3] * dp).astype(MXU)
            dcw_ref[...] += jnp.concatenate([jnp.sum(dn * us, axis=0, keepdims=True), jnp.sum(dc * us, axis=0, keepdims=True),
                                             jnp.sum(dp * us, axis=0, keepdims=True)], axis=0)
            dcb_ref[...] += jnp.sum(dc, axis=0, keepdims=True)

        @pl.when(is_a)
        def _():
            cs = ca_ref[...].astype(F32)
            sg = jax.nn.sigmoid(cs)
            finish(da_ref[...].astype(F32) * cg_ref[...].astype(F32) * (sg * (1.0 + cs * (1.0 - sg))))

        @pl.when(jnp.logical_not(is_a))
        def _():
            finish(da_ref[...].astype(F32) * _silu(ca_ref[...].astype(F32)))

    cs_ = lambda rows: pl.BlockSpec((rows, FFN_COLS), lambda j, i: (i if rows == tr else 0, j))
    hf = pl.BlockSpec((tr, FFN_COLS), lambda j, i: (i, j % nf))
    gate = pl.BlockSpec((tr, FFN_COLS), lambda j, i: (jnp.where(j < nf, i, 0), jnp.where(j < nf, j, 0)))
    return pl.pallas_call(
        body, name=name, grid=(2 * nf, r // tr), in_specs=[hf, cs_(tr), hf, gate, cs_(3)], out_specs=(cs_(tr), cs_(3), cs_(1)),
        out_shape=(jax.ShapeDtypeStruct((r, f2), MXU), jax.ShapeDtypeStruct((3, f2), F32), jax.ShapeDtypeStruct((1, f2), F32)),
        compiler_params=_cp(("parallel", "arbitrary")))(dact, u, ca, cg, cw)


def hg_lb_fwd(e0, e1, *, name):
    def body(a, b, o):
        o[...] = _hg_lower_bound(a[...], b[...])

    return pl.pallas_call(body, name=name, out_shape=jax.ShapeDtypeStruct(e0.shape, F32))(e0, e1)


def hg_lb_bwd(e0, e1, dlb, *, name):
    def body(a, b, g, oa, ob):
        _, vjp = jax.vjp(_hg_lower_bound, a[...], b[...])
        oa[...], ob[...] = vjp(g[...])

    s = jax.ShapeDtypeStruct(e0.shape, F32)
    return pl.pallas_call(body, name=name, out_shape=(s, s))(e0, e1, dlb)


def _adamw(w, g, m, v):
    m = ADAM_B1 * m + (1.0 - ADAM_B1) * g
    v = ADAM_B2 * v + (1.0 - ADAM_B2) * jnp.square(g)
    m_hat = m / (1.0 - ADAM_B1 ** ADAM_STEP)
    v_hat = v / (1.0 - ADAM_B2 ** ADAM_STEP)
    delta = -ADAM_LR * (m_hat / (jnp.sqrt(v_hat) + ADAM_EPS) + ADAM_WD * w)
    return delta, m, v


def _as2d(a):
    if a.ndim >= 2 and a.shape[-1] % 128 == 0:
        return a.reshape(-1, a.shape[-1])
    return a.reshape(-1, 128) if a.size % 128 == 0 else a.reshape(1, -1)


def adamw(w, g, m, v, *, name):
    w2 = _as2d(w)
    outs = rowmap(_adamw, [w2, _as2d(g), _as2d(m), _as2d(v)], [], [(w2.shape[1], F32)] * 3, name=name)
    return tuple(o.reshape(w.shape) for o in outs)


HBM_SPEC = pl.BlockSpec(memory_space=pltpu.HBM)


def _place():
    mx, my, mc = lax.axis_index("x"), lax.axis_index("y"), lax.axis_index("c")
    others = [(1 - mx, my), (mx, 1 - my), (1 - mx, 1 - my)]
    return mx, my, mc, others


def chip_allgather(x, *, name):
    def body(x_ref, o_ref, send_sems, recv_sems, local_sem):
        mx, my, mc, others = _place()
        me = 2 * mx + my
        mine = pltpu.make_async_copy(x_ref, o_ref.at[me], local_sem)
        mine.start()
        sends = [pltpu.make_async_remote_copy(src_ref=x_ref, dst_ref=o_ref.at[me], send_sem=send_sems.at[j], recv_sem=recv_sems.at[j],
                                              device_id=(px, py, mc), device_id_type=MESH) for j, (px, py) in enumerate(others)]
        for cp in sends:
            cp.start()
        for j, (px, py) in enumerate(others):
            pltpu.make_async_remote_copy(src_ref=x_ref, dst_ref=o_ref.at[2 * px + py], send_sem=send_sems.at[j], recv_sem=recv_sems.at[j],
                                         device_id=(px, py, mc), device_id_type=MESH).wait_recv()
        for cp in sends:
            cp.wait_send()
        mine.wait()

    return pl.pallas_call(
        body, name=name, out_shape=jax.ShapeDtypeStruct((4,) + x.shape, x.dtype), in_specs=[HBM_SPEC], out_specs=HBM_SPEC,
        scratch_shapes=[pltpu.SemaphoreType.DMA((3,)), pltpu.SemaphoreType.DMA((3,)), pltpu.SemaphoreType.DMA])(x)


def _win(ref, axis, start, size):
    idx = [slice(None)] * len(ref.shape)
    idx[axis] = pl.ds(start, size)
    return ref.at[tuple(idx)]


def _half_axis(shape, ax):
    if shape[0] == 2:
        return 0
    return 2 if ax == 1 else 1


def _cut(shape, axis, parts):
    return shape[:axis] + (shape[axis] // parts,) + shape[axis + 1:]


def _hbm_call(body, arrays, out_shapes, sems, name):
    n_in = len(arrays)
    return pl.pallas_call(body, name=name, out_shape=tuple(out_shapes), in_specs=[HBM_SPEC] * n_in, out_specs=tuple([HBM_SPEC] * len(out_shapes)),
                          scratch_shapes=sems)(*arrays)


def place_shard(shard, ax, chip, dtype, *, name):
    l, r, c = shard.shape
    tr = _row_tile(r, c)
    per_block = (l, r // tr, 1)[ax]

    def omap(li, ri, cref):
        idx = [li, ri, 0]
        idx[ax] = idx[ax] + cref[0] * per_block
        return tuple(idx)

    def body(c_ref, s_ref, o_ref):
        o_ref[...] = s_ref[...].astype(dtype)

    full = shard.shape[:ax] + (4 * shard.shape[ax],) + shard.shape[ax + 1:]
    return pl.pallas_call(
        body, name=name, out_shape=jax.ShapeDtypeStruct(full, dtype),
        grid_spec=pltpu.PrefetchScalarGridSpec(
            num_scalar_prefetch=1, grid=(l, r // tr),
            in_specs=[pl.BlockSpec((1, tr, c), lambda li, ri, cref: (li, ri, 0))], out_specs=pl.BlockSpec((1, tr, c), omap)),
        compiler_params=_cp(("parallel", "parallel")))(chip, shard)


SEM_SPEC = pl.BlockSpec(memory_space=pltpu.SEMAPHORE)
SPLIT_COPY = pltpu.CompilerParams(has_side_effects=pltpu.SideEffectType.DATAFLOW_SIDE_EFFECTING)


def _gather_part(ref, shape, ax, hax, chip, core):
    sz, hs = shape[ax] // 4, shape[hax] // 2
    return _win(_win(ref, ax, chip * sz, sz), hax, core * hs, hs)


def gather_placed_start(arrays, axes, haxes, after, *, name):
    n = len(arrays)

    m = 3 * n

    def body(*refs):
        ins, send_sems, recv_sems = refs[:n], refs[n + 1:n + 1 + m], refs[n + 1 + m:n + 1 + 2 * m]
        token = refs[2 * n + 1 + 2 * m]
        mx, my, mc, others = _place()
        me = 2 * mx + my
        for i in range(n):
            for j, (px, py) in enumerate(others):
                part = _gather_part(ins[i], arrays[i].shape, axes[i], haxes[i], me, mc)
                pltpu.make_async_remote_copy(src_ref=part, dst_ref=part, send_sem=send_sems[3 * i + j], recv_sem=recv_sems[3 * i + j],
                                             device_id=(px, py, mc), device_id_type=MESH).start()
        token[...] = jnp.zeros_like(token)

    hbm = [pltpu.with_memory_space_constraint(a_, pltpu.HBM) for a_ in arrays]
    out = pl.pallas_call(
        body, name=name,
        out_shape=tuple([pltpu.SemaphoreType.DMA(())] * (2 * m)) + tuple(pltpu.HBM(a_.shape, a_.dtype) for a_ in arrays)
        + (jax.ShapeDtypeStruct((8, 128), F32),),
        in_specs=[HBM_SPEC] * n + [pl.BlockSpec(memory_space=pl.ANY)],
        out_specs=tuple([SEM_SPEC] * (2 * m)) + tuple([HBM_SPEC] * n) + (pl.BlockSpec(memory_space=pltpu.VMEM),),
        input_output_aliases={i: 2 * m + i for i in range(n)}, compiler_params=SPLIT_COPY)(*hbm, after)
    return list(out[:m]), list(out[m:2 * m]), list(out[2 * m:2 * m + n]), out[2 * m + n]


def gather_placed_wait(arrays, send_sems, recv_sems, axes, haxes, after, *, name):
    n = len(arrays)

    m = 3 * n

    def body(*refs):
        ins, send_refs, recv_refs = refs[:n], refs[n:n + m], refs[n + m:n + 2 * m]
        mx, my, mc, others = _place()
        me = 2 * mx + my
        for i in range(n):
            for j, (px, py) in enumerate(others):
                cp = pltpu.make_async_remote_copy(
                    src_ref=_gather_part(ins[i], arrays[i].shape, axes[i], haxes[i], me, mc),
                    dst_ref=_gather_part(ins[i], arrays[i].shape, axes[i], haxes[i], 2 * px + py, mc),
                    send_sem=send_refs[3 * i + j], recv_sem=recv_refs[3 * i + j], device_id=(px, py, mc), device_id_type=MESH)
                cp.wait_send()
                cp.wait_recv()

    out = pl.pallas_call(
        body, name=name, out_shape=tuple(pltpu.HBM(a_.shape, a_.dtype) for a_ in arrays),
        in_specs=[HBM_SPEC] * n + [SEM_SPEC] * (2 * m) + [pl.BlockSpec(memory_space=pl.ANY)], out_specs=tuple([HBM_SPEC] * n),
        input_output_aliases={i: i for i in range(n)}, compiler_params=SPLIT_COPY)(*arrays, *send_sems, *recv_sems, after)
    return list(out)


def pair_swap_halves(arrays, haxes, *, name):
    n = len(arrays)

    def body(*refs):
        ins, outs = refs[:n], refs[n:2 * n]
        send_sems, recv_sems = refs[2 * n:]
        mx, my, mc, _ = _place()
        cps = []
        for i in range(n):
            hs = arrays[i].shape[haxes[i]] // 2
            cp = pltpu.make_async_remote_copy(src_ref=_win(ins[i], haxes[i], (1 - mc) * hs, hs), dst_ref=outs[i], send_sem=send_sems.at[i],
                                              recv_sem=recv_sems.at[i], device_id=(mx, my, 1 - mc), device_id_type=MESH)
            cp.start()
            cps.append(cp)
        for cp in cps:
            cp.wait()

    outs = [jax.ShapeDtypeStruct(_cut(a_.shape, h_, 2), a_.dtype) for a_, h_ in zip(arrays, haxes)]
    return _hbm_call(body, arrays, outs, [pltpu.SemaphoreType.DMA((n,)), pltpu.SemaphoreType.DMA((n,))], name)


def add_own_half(g, t, hax, core, *, out_dtype, name):
    l, r, c = t.shape
    tr = _row_tile(r, c)
    per_half = (l, r // tr, 1)[hax]

    def imap(li, ri, cref):
        idx = [li, ri, 0]
        idx[hax] = idx[hax] + cref[0] * per_half
        return tuple(idx)

    def body(c_ref, g_ref, t_ref, o_ref):
        o_ref[...] = (g_ref[...] + t_ref[...]).astype(out_dtype)

    return pl.pallas_call(
        body, name=name, out_shape=jax.ShapeDtypeStruct(t.shape, out_dtype),
        grid_spec=pltpu.PrefetchScalarGridSpec(
            num_scalar_prefetch=1, grid=(l, r // tr),
            in_specs=[pl.BlockSpec((1, tr, c), imap), pl.BlockSpec((1, tr, c), lambda li, ri, cref: (li, ri, 0))],
            out_specs=pl.BlockSpec((1, tr, c), lambda li, ri, cref: (li, ri, 0))),
        compiler_params=_cp(("parallel", "parallel")))(core, g, t)


def exchange_blocks_start(arrays, axes, *, name):
    n = len(arrays)
    lands = [lax.empty((4,) + _cut(a_.shape, ax, 4), a_.dtype) for a_, ax in zip(arrays, axes)]

    def body(*refs):
        ins, lnd = refs[:n], refs[n:2 * n]
        send_sems, recv_sems = refs[2 * n:6 * n], refs[6 * n:9 * n]
        token = refs[11 * n]
        mx, my, mc, others = _place()
        me = 2 * mx + my
        for i in range(n):
            sz = arrays[i].shape[axes[i]] // 4
            pltpu.make_async_copy(_win(ins[i], axes[i], me * sz, sz), lnd[i].at[me], send_sems[4 * i + 3]).start()
            for j, (px, py) in enumerate(others):
                pltpu.make_async_remote_copy(src_ref=_win(ins[i], axes[i], (2 * px + py) * sz, sz), dst_ref=lnd[i].at[me],
                                             send_sem=send_sems[4 * i + j], recv_sem=recv_sems[3 * i + j], device_id=(px, py, mc),
                                             device_id_type=MESH).start()
        token[...] = jnp.zeros_like(token)

    hbm = [pltpu.with_memory_space_constraint(a_, pltpu.HBM) for a_ in arrays + lands]
    out = pl.pallas_call(
        body, name=name,
        out_shape=tuple([pltpu.SemaphoreType.DMA(())] * (7 * n)) + tuple(pltpu.HBM(a_.shape, a_.dtype) for a_ in arrays + lands)
        + (jax.ShapeDtypeStruct((8, 128), F32),),
        in_specs=[HBM_SPEC] * (2 * n),
        out_specs=tuple([SEM_SPEC] * (7 * n)) + tuple([HBM_SPEC] * (2 * n)) + (pl.BlockSpec(memory_space=pltpu.VMEM),),
        input_output_aliases={i: 7 * n + i for i in range(2 * n)}, compiler_params=SPLIT_COPY)(*hbm)
    return list(out[:7 * n]), list(out[7 * n:8 * n]), list(out[8 * n:9 * n]), out[9 * n]


def exchange_blocks_wait(sems, arrays, lands, axes, after, *, name):
    n = len(arrays)

    def body(*refs):
        ins, lnd = refs[:n], refs[n:2 * n]
        send_sems, recv_sems = refs[2 * n:6 * n], refs[6 * n:9 * n]
        mx, my, mc, others = _place()
        me = 2 * mx + my
        for i in range(n):
            sz = arrays[i].shape[axes[i]] // 4
            mine = _win(ins[i], axes[i], me * sz, sz)
            pltpu.make_async_copy(mine, lnd[i].at[me], send_sems[4 * i + 3]).wait()
            for j, (px, py) in enumerate(others):
                cp = pltpu.make_async_remote_copy(src_ref=mine, dst_ref=lnd[i].at[2 * px + py], send_sem=send_sems[4 * i + j],
                                                  recv_sem=recv_sems[3 * i + j], device_id=(px, py, mc), device_id_type=MESH)
                cp.wait_send()
                cp.wait_recv()

    out = pl.pallas_call(
        body, name=name, out_shape=tuple(pltpu.HBM(a_.shape, a_.dtype) for a_ in arrays + lands),
        in_specs=[HBM_SPEC] * (2 * n) + [SEM_SPEC] * (7 * n) + [pl.BlockSpec(memory_space=pl.ANY)] * len(after),
        out_specs=tuple([HBM_SPEC] * (2 * n)), input_output_aliases={i: i for i in range(2 * n)}, compiler_params=SPLIT_COPY)(
            *arrays, *lands, *sems, *after)
    return list(out[n:])


def sum_blocks(e, hax, core, *, name):
    _, l, r, c = e.shape
    tr = _row_tile(r, c)
    per_half = (l, r // tr, 1)[hax]

    def omap(li, ri, cref):
        idx = [li, ri, 0]
        idx[hax] = idx[hax] + cref[0] * per_half
        return tuple(idx)

    def body(c_ref, e_ref, o_ref):
        v = e_ref[...].astype(F32)
        o_ref[...] = ((v[0] + v[1]) + v[2]) + v[3]

    full = (l, r, c)[:hax] + (2 * (l, r, c)[hax],) + (l, r, c)[hax + 1:]
    return pl.pallas_call(
        body, name=name, out_shape=jax.ShapeDtypeStruct(full, F32),
        grid_spec=pltpu.PrefetchScalarGridSpec(
            num_scalar_prefetch=1, grid=(l, r // tr),
            in_specs=[pl.BlockSpec((4, 1, tr, c), lambda li, ri, cref: (0, li, ri, 0))], out_specs=pl.BlockSpec((1, tr, c), omap)),
        compiler_params=_cp(("parallel", "parallel")))(core, e)


def pair_fill_halves(arrays, haxes, *, name):
    n = len(arrays)

    def body(*refs):
        ins, outs = refs[:n], refs[n:2 * n]
        send_sems, recv_sems = refs[2 * n:]
        mx, my, mc, _ = _place()
        cps = []
        for i in range(n):
            hs = arrays[i].shape[haxes[i]] // 2
            mine = _win(ins[i], haxes[i], mc * hs, hs)
            cp = pltpu.make_async_remote_copy(src_ref=mine, dst_ref=_win(outs[i], haxes[i], mc * hs, hs), send_sem=send_sems.at[i],
                                              recv_sem=recv_sems.at[i], device_id=(mx, my, 1 - mc), device_id_type=MESH)
            cp.start()
            cps.append(cp)
        for i in range(n):
            hs = arrays[i].shape[haxes[i]] // 2
            pltpu.make_async_remote_copy(src_ref=_win(ins[i], haxes[i], mc * hs, hs), dst_ref=_win(outs[i], haxes[i], (1 - mc) * hs, hs),
                                         send_sem=send_sems.at[i], recv_sem=recv_sems.at[i], device_id=(mx, my, 1 - mc),
                                         device_id_type=MESH).wait_recv()
        for cp in cps:
            cp.wait_send()

    return pl.pallas_call(
        body, name=name, out_shape=tuple(jax.ShapeDtypeStruct(a_.shape, a_.dtype) for a_ in arrays), in_specs=[HBM_SPEC] * n,
        out_specs=tuple([HBM_SPEC] * n), input_output_aliases={i: i for i in range(n)},
        scratch_shapes=[pltpu.SemaphoreType.DMA((n,)), pltpu.SemaphoreType.DMA((n,))])(*arrays)


WEIGHTS = ['c_ctx', 'w_mod', 'b_mod', 'norm1_w', 'norm2_w', 'final_norm_w', 's5_w_in', 's5_lam_re', 's5_lam_im', 's5_log_step', 's5_b_re', 's5_b_im', 's5_c_re', 's5_c_im', 's5_d', 's5_w_glu', 's5_w_out', 'hg_w_in', 'hg_lower_bounds', 'hg_gnorm_w', 'hg_w_out', 'ffn_w_up', 'ffn_conv_w', 'ffn_conv_b', 'ffn_w_down']
INPUTS = ['x', 'c', 'ctx', 'c_ctx', 'w_mod', 'b_mod', 'norm1_w', 'norm2_w', 'final_norm_w', 's5_w_in', 's5_lam_re', 's5_lam_im', 's5_log_step', 's5_b_re', 's5_b_im', 's5_c_re', 's5_c_im', 's5_d', 's5_w_glu', 's5_w_out', 'hg_w_in', 'hg_lower_bounds', 'hg_gnorm_w', 'hg_w_out', 'ffn_w_up', 'ffn_conv_w', 'ffn_conv_b', 'ffn_w_down', 'loss_target', 'm_c_ctx', 'm_w_mod', 'm_b_mod', 'm_norm1_w', 'm_norm2_w', 'm_final_norm_w', 'm_s5_w_in', 'm_s5_lam_re', 'm_s5_lam_im', 'm_s5_log_step', 'm_s5_b_re', 'm_s5_b_im', 'm_s5_c_re', 'm_s5_c_im', 'm_s5_d', 'm_s5_w_glu', 'm_s5_w_out', 'm_hg_w_in', 'm_hg_lower_bounds', 'm_hg_gnorm_w', 'm_hg_w_out', 'm_ffn_w_up', 'm_ffn_conv_w', 'm_ffn_conv_b', 'm_ffn_w_down', 'v_c_ctx', 'v_w_mod', 'v_b_mod', 'v_norm1_w', 'v_norm2_w', 'v_final_norm_w', 'v_s5_w_in', 'v_s5_lam_re', 'v_s5_lam_im', 'v_s5_log_step', 'v_s5_b_re', 'v_s5_b_im', 'v_s5_c_re', 'v_s5_c_im', 'v_s5_d', 'v_s5_w_glu', 'v_s5_w_out', 'v_hg_w_in', 'v_hg_lower_bounds', 'v_hg_gnorm_w', 'v_hg_w_out', 'v_ffn_w_up', 'v_ffn_conv_w', 'v_ffn_conv_b', 'v_ffn_w_down']
SHARD_AXIS = {"w_mod": 2, "s5_w_in": 1, "s5_w_glu": 1, "s5_w_out": 1, "hg_w_in": 2, "hg_lower_bounds": 2, "hg_w_out": 1,
              "ffn_w_up": 2, "ffn_conv_w": 2, "ffn_w_down": 1}
GATHER_F32 = ("hg_lower_bounds", "ffn_conv_w")
PACK_W = 1024
GRAD_WIRE = jnp.bfloat16


def _reduce_start(items, core, tag):
    names, arrays, axes = [n for n, _, _ in items], [g_ for _, g_, _ in items], [ax for _, _, ax in items]
    haxes = [_half_axis(g_.shape, ax) for g_, ax in zip(arrays, axes)]
    t = pair_swap_halves(arrays, haxes, name="grad_pair_swap_" + tag)
    h = [add_own_half(g_, t_, hx, core, out_dtype=GRAD_WIRE, name="grad_pair_add_" + n) for g_, t_, hx, n in zip(arrays, t, haxes, names)]
    sems, h, lands, token = exchange_blocks_start(h, axes, name="grad_exchange_start_" + tag)
    return (names, sems, h, lands, axes, haxes), token


def _reduce_finish(state, core, after, tag):
    names, sems, h, lands, axes, haxes = state
    e = exchange_blocks_wait(sems, h, lands, axes, list(after), name="grad_exchange_wait_" + tag)
    s = [sum_blocks(e_, hx, core, name="grad_chip_sum_" + n) for e_, hx, n in zip(e, haxes, names)]
    return dict(zip(names, pair_fill_halves(s, haxes, name="grad_pair_fill_" + tag)))


def _pack_small(grads, small):
    flat = jnp.concatenate([grads[n].reshape(-1) for n in small])
    pad = (-flat.shape[0]) % (64 * PACK_W)
    return jnp.pad(flat, (0, pad)).reshape(1, -1, PACK_W)


def _unpack_small(a, block, small):
    sm = chip_allgather(block[0], name="allgather_small_grads").reshape(-1)
    out, off = {}, 0
    for n in small:
        out[n] = sm[off:off + math.prod(a[n].shape)].reshape(a[n].shape)
        off += math.prod(a[n].shape)
    return out


def _blockdiag_b(bb, kb):
    gl = S5_KIN // S5_GROUP
    x = bb.reshape(kb, gl, S5_GROUP, S5_STATE)
    return (x[:, :, :, None, :] * jnp.eye(gl, dtype=bb.dtype)[None, :, None, :, None]).reshape(kb, S5_KIN, S5_KST)


def _blockdiag_c(cc, kb):
    gl = S5_KIN // S5_GROUP
    x = cc.reshape(kb, gl, S5_GROUP, S5_STATE).transpose(0, 1, 3, 2)
    return (x[:, :, :, None, :] * jnp.eye(gl, dtype=cc.dtype)[None, :, None, :, None]).reshape(kb, S5_KST, S5_KIN)


def _diag_b(m, kb):
    gl = S5_KIN // S5_GROUP
    x = m.reshape(kb, gl, S5_GROUP, gl, S5_STATE)
    return jnp.stack([x[:, i, :, i, :] for i in range(gl)], axis=1).reshape(kb * gl, S5_GROUP, S5_STATE)


def _diag_c(m, kb):
    gl = S5_KIN // S5_GROUP
    x = m.reshape(kb, gl, S5_STATE, gl, S5_GROUP)
    return jnp.stack([x[:, i, :, i, :] for i in range(gl)], axis=1).transpose(0, 1, 3, 2).reshape(kb * gl, S5_GROUP, S5_STATE)


def kernel(x, c, ctx, c_ctx, w_mod, b_mod, norm1_w, norm2_w, final_norm_w, s5_w_in, s5_lam_re, s5_lam_im, s5_log_step, s5_b_re, s5_b_im, s5_c_re, s5_c_im, s5_d, s5_w_glu, s5_w_out, hg_w_in, hg_lower_bounds, hg_gnorm_w, hg_w_out, ffn_w_up, ffn_conv_w, ffn_conv_b, ffn_w_down, loss_target, m_c_ctx, m_w_mod, m_b_mod, m_norm1_w, m_norm2_w, m_final_norm_w, m_s5_w_in, m_s5_lam_re, m_s5_lam_im, m_s5_log_step, m_s5_b_re, m_s5_b_im, m_s5_c_re, m_s5_c_im, m_s5_d, m_s5_w_glu, m_s5_w_out, m_hg_w_in, m_hg_lower_bounds, m_hg_gnorm_w, m_hg_w_out, m_ffn_w_up, m_ffn_conv_w, m_ffn_conv_b, m_ffn_w_down, v_c_ctx, v_w_mod, v_b_mod, v_norm1_w, v_norm2_w, v_final_norm_w, v_s5_w_in, v_s5_lam_re, v_s5_lam_im, v_s5_log_step, v_s5_b_re, v_s5_b_im, v_s5_c_re, v_s5_c_im, v_s5_d, v_s5_w_glu, v_s5_w_out, v_hg_w_in, v_hg_lower_bounds, v_hg_gnorm_w, v_hg_w_out, v_ffn_w_up, v_ffn_conv_w, v_ffn_conv_b, v_ffn_w_down):
    a = dict(zip(INPUTS, (x, c, ctx, c_ctx, w_mod, b_mod, norm1_w, norm2_w, final_norm_w, s5_w_in, s5_lam_re, s5_lam_im, s5_log_step, s5_b_re, s5_b_im, s5_c_re, s5_c_im, s5_d, s5_w_glu, s5_w_out, hg_w_in, hg_lower_bounds, hg_gnorm_w, hg_w_out, ffn_w_up, ffn_conv_w, ffn_conv_b, ffn_w_down, loss_target, m_c_ctx, m_w_mod, m_b_mod, m_norm1_w, m_norm2_w, m_final_norm_w, m_s5_w_in, m_s5_lam_re, m_s5_lam_im, m_s5_log_step, m_s5_b_re, m_s5_b_im, m_s5_c_re, m_s5_c_im, m_s5_d, m_s5_w_glu, m_s5_w_out, m_hg_w_in, m_hg_lower_bounds, m_hg_gnorm_w, m_hg_w_out, m_ffn_w_up, m_ffn_conv_w, m_ffn_conv_b, m_ffn_w_down, v_c_ctx, v_w_mod, v_b_mod, v_norm1_w, v_norm2_w, v_final_norm_w, v_s5_w_in, v_s5_lam_re, v_s5_lam_im, v_s5_log_step, v_s5_b_re, v_s5_b_im, v_s5_c_re, v_s5_c_im, v_s5_d, v_s5_w_glu, v_s5_w_out, v_hg_w_in, v_hg_lower_bounds, v_hg_gnorm_w, v_hg_w_out, v_ffn_w_up, v_ffn_conv_w, v_ffn_conv_b, v_ffn_w_down)))
    nb, seq, d = x.shape
    assert nb == NB
    rc = nb * ctx.shape[1]
    cfg = {"rc": rc}
    f = a["ffn_w_down"].shape[1] * 4
    core = lax.axis_index("c").astype(jnp.int32).reshape(1)

    w = {n: a[n] for n in WEIGHTS if n not in SHARD_AXIS}
    chip = (2 * lax.axis_index("x") + lax.axis_index("y")).astype(jnp.int32).reshape(1)
    groups = {
        "now": [("w_mod0", a["w_mod"][0:1]), ("s5_w_in", a["s5_w_in"]), ("hg_lower_bounds", a["hg_lower_bounds"]), ("ffn_conv_w", a["ffn_conv_w"])],
        "mid": [("s5_w_glu", a["s5_w_glu"]), ("s5_w_out", a["s5_w_out"]), ("ffn_w_up0", a["ffn_w_up"][0:1]), ("ffn_w_down0", a["ffn_w_down"][0:1])],
        "later": [("w_mod1", a["w_mod"][1:2]), ("hg_w_in", a["hg_w_in"]), ("hg_w_out", a["hg_w_out"]), ("ffn_w_up1", a["ffn_w_up"][1:2]),
                  ("ffn_w_down1", a["ffn_w_down"][1:2])]}
    shard_axis = lambda n: SHARD_AXIS[n.rstrip("01")]
    placed = {g: [place_shard(s_, shard_axis(n), chip, F32 if n in GATHER_F32 else MXU, name="place_" + n) for n, s_ in it] for g, it in groups.items()}
    axes = {g: [shard_axis(n) for n, _ in it] for g, it in groups.items()}
    haxes = {g: [_half_axis(p_.shape, ax) for p_, ax in zip(placed[g], axes[g])] for g in groups}
    fly_now = gather_placed_start(placed["now"], axes["now"], haxes["now"], chip, name="allgather_now_start")
    fly_mid = gather_placed_start(placed["mid"], axes["mid"], haxes["mid"], fly_now[3], name="allgather_mid_start")
    fly_later = gather_placed_start(placed["later"], axes["later"], haxes["later"], fly_mid[3], name="allgather_later_start")

    def land(fly, g, after):
        send_, recv_, flying, _ = fly
        landed = gather_placed_wait(flying, send_, recv_, axes[g], haxes[g], after, name=f"allgather_{g}_wait")
        w.update(dict(zip([n for n, _ in groups[g]], pair_fill_halves(landed, haxes[g], name=f"allgather_{g}_pair_fill"))))

    tmaj = lambda t: t.transpose(1, 0, 2).reshape(-1, t.shape[-1])
    zero = fly_later[3][0:1, 0:1]
    x0 = jnp.concatenate([tmaj(ctx), tmaj(x)], axis=0)
    tgt = tmaj(a["loss_target"])
    land(fly_now, "now", x0)
    c16 = jnp.concatenate([jnp.broadcast_to(c_ctx[None], (8, d)), c, c], axis=0) + zero
    mt0, scb = mod_fwd(c16, w["w_mod0"][0], w["b_mod"][0][None], name="mod_fwd0")
    mt = [mt0, None]
    n1, n2 = w["norm1_w"], w["norm2_w"]
    w["w_mod"], w["ffn_w_up"], w["ffn_w_down"] = [w["w_mod0"][0], None], [None, None], [None, None]

    def ffn_fwd(l, h):
        u = mm(h, w["ffn_w_up"][l], out_dtype=MXU, name=f"ffn_up{l}")
        act, ca, cg = ffn_mid_fwd(cfg, u, w["ffn_conv_w"][l], w["ffn_conv_b"][l][None], name=f"ffn_mid{l}")
        return (u, ca, cg), act, mm(act, w["ffn_w_down"][l], name=f"ffn_down{l}")

    def ffn_bwd(l, dfo, kept, act, h, zero=0.0):
        dact = mm(dfo, w["ffn_w_down"][l], tb=True, out_dtype=MXU, name=f"ffn_down_dx{l}")
        dwd = mm(act, dfo, ta=True, name=f"ffn_down_dw{l}")
        du, dcw, dcb = ffn_mid_bwd(cfg, dact, *kept, w["ffn_conv_w"][l] + zero, name=f"ffn_mid_bwd{l}")
        dh = mm(du, w["ffn_w_up"][l], tb=True, name=f"ffn_up_dx{l}")
        dwu = mm(h, du, ta=True, name=f"ffn_up_dw{l}")
        return dh, dwu, dcw, dcb[0], dwd

    g_, p_ = d // S5_GROUP, S5_STATE
    ns, kb = g_ * p_, d // S5_KIN
    s5p = (w["s5_lam_re"][0].reshape(2 * g_, p_), w["s5_lam_im"][0].reshape(2 * g_, p_), w["s5_log_step"][0].reshape(2 * g_, 1),
           w["s5_b_re"][0].transpose(0, 1, 3, 2).reshape(2 * g_, S5_GROUP, p_), w["s5_b_im"][0].transpose(0, 1, 3, 2).reshape(2 * g_, S5_GROUP, p_))
    ar, ai, bbr, bbi = s5_disc_fwd(*s5p, name="s5_disc")
    dsk = w["s5_d"]
    _, h1 = node_fwd(cfg, x0, None, None, 0, n1[0:1], mt[0], 0, name="node0a")
    u0 = mm(h1, w["s5_w_in"][0], name="s5_in")
    s5s, ys = [], []
    for dd in range(2):
        sl = slice(dd * g_, (dd + 1) * g_)
        a_r, a_i = ar[sl].reshape(1, ns), ai[sl].reshape(1, ns)
        a2 = (a_r * a_r - a_i * a_i, 2.0 * a_r * a_i)
        b_r, b_i = _blockdiag_b(bbr[sl], kb), _blockdiag_b(bbi[sl], kb)
        c_r, c_i = _blockdiag_c(w["s5_c_re"][0, dd], kb), _blockdiag_c(w["s5_c_im"][0, dd], kb)
        ak, ai_k = a_r.reshape(kb, 1, S5_KST), a_i.reshape(kb, 1, S5_KST)
        ab = (ak * b_r - ai_k * b_i, ak * b_i + ai_k * b_r)
        akc, aic = ak.reshape(kb, S5_KST, 1), ai_k.reshape(kb, S5_KST, 1)
        c2 = (akc * c_r - aic * c_i, akc * c_i + aic * c_r)
        bf = lambda t_: t_.astype(MXU)
        sre, sim, ere, eim, y_ = s5_scan_fwd(cfg, u0, a2[0], a2[1], bf(b_r), bf(b_i), bf(ab[0]), bf(ab[1]), bf(c_r), bf(c_i), rev=dd == 1,
                                             name=f"s5_scan{dd}")
        s5s.append((sre, sim, ere, eim, a2[0], a2[1], bf(b_r), bf(b_i), bf(c_r), bf(c_i), bf(c2[0]), bf(c2[1])))
        ys.append(y_)

    def glu_a(u, y0, y1, ds):
        yp = (ds * u + y0) + y1
        return yp, _gelu(yp)

    ypre, zgb = rowmap(glu_a, [u0, ys[0], ys[1]], [dsk], [(d, F32), (d, MXU)], name="s5_glu_a")
    land(fly_mid, "mid", zgb)
    w["ffn_w_up"][0], w["ffn_w_down"][0] = w["ffn_w_up0"][0], w["ffn_w_down0"][0]
    tg = mm(zgb, w["s5_w_glu"][0], name="s5_glu")
    (z2,) = rowmap(lambda yp, t: _gelu(yp) * jax.nn.sigmoid(t), [ypre, tg], [], [(d, MXU)], name="s5_glu_b")
    y1a = mm(z2, w["s5_w_out"][0], name="s5_out")
    x1a, h2a = node_fwd(cfg, x0, y1a, mt[0], 2, n2[0:1], mt[0], 3, name="node0b")
    ufa, acta, foa = ffn_fwd(0, h2a)

    land(fly_later, "later", foa)
    w["w_mod"][1], w["ffn_w_up"][1], w["ffn_w_down"][1] = w["w_mod1"][0], w["ffn_w_up1"][0], w["ffn_w_down1"][0]
    mt[1], _ = mod_fwd(c16, w["w_mod"][1], w["b_mod"][1][None], name="mod_fwd1")
    x2a, h1b = node_fwd(cfg, x1a, foa, mt[0], 5, n1[1:2], mt[1], 0, name="node1a")
    z = mm(h1b, w["hg_w_in"][0], name="hg_in")
    e0, e1 = w["hg_lower_bounds"][:, 0, :], w["hg_lower_bounds"][:, 1, :]
    lb = hg_lb_fwd(e0, e1, name="hg_lb")
    gw = w["hg_gnorm_w"]
    o0, sin0 = hg_scan_fwd(cfg, z, lb[0:1], d_dir=0, name="hg_scan0")
    o1, sin1 = hg_scan_fwd(cfg, z, lb[1:2], d_dir=1, name="hg_scan1")
    onb = hg_read_fwd(o0, o1, z, gw, name="hg_read")
    y1b = mm(onb, w["hg_w_out"][0], name="hg_out")
    x1b, h2b = node_fwd(cfg, x2a, y1b, mt[1], 2, n2[1:2], mt[1], 3, name="node1b")
    ufb, actb, fob = ffn_fwd(1, h2b)
    loss_p, dx2b, dfob, dg2_1, dfnw = final_node(cfg, x1b, fob, mt[1], 5, w["final_norm_w"][None], tgt, name="final_node")

    gr = {}
    dh2b, dwu1, dcw1, dcb1, dwd1 = ffn_bwd(1, dfob, ufb, actb, h2b)
    dx1b, dy1b, dn2_1, dsh2_1, dsc2_1, dg1_1 = node_bwd(cfg, dx2b, dh2b, x1b, y1b, mt[1], 2, n2[1:2], mt[1], 3, name="node1b_bwd")
    don = mm(dy1b, w["hg_w_out"][0], tb=True, name="hg_out_dx")
    gr["hg_w_out"] = mm(onb, dy1b, ta=True, name="hg_out_dw")[None]
    do_, dgate_, dgw = hg_read_bwd(don, o0, o1, z, gw, name="hg_read_bwd")
    dq, dv, dxf, dlb0 = hg_scan_bwd(cfg, do_, z, lb[0:1], sin0, None, None, d_dir=0, name="hg_scan_bwd0")
    dq, dv, dxb, dlb1 = hg_scan_bwd(cfg, do_, z, lb[1:2], sin1, dq, dv, d_dir=1, name="hg_scan_bwd1")
    dz = [dq, dv, dxf, dxb, dgate_]
    dh1b = mm_cat_nt(dz, w["hg_w_in"][0], name="hg_in_dx")
    gr["hg_w_in"] = mm_tn_cat(h1b, dz, name="hg_in_dw")[None]
    de0, de1 = hg_lb_bwd(e0, e1, jnp.concatenate([dlb0, dlb1], axis=0), name="hg_lb_bwd")
    gr["hg_lower_bounds"] = jnp.stack([de0, de1], axis=1)
    gr["hg_gnorm_w"] = dgw
    dx2a, dfoa, dn1_1, dsh1_1, dsc1_1, dg2_0 = node_bwd(cfg, dx1b, dh1b, x2a, foa, mt[0], 5, n1[1:2], mt[1], 0, name="node1a_bwd")
    dmt1 = jnp.concatenate([dsh1_1, dsc1_1, dg1_1, dsh2_1, dsc2_1, dg2_1], axis=1)
    red1, tok1 = _reduce_start([("hg_w_in", gr["hg_w_in"], 2), ("hg_w_out", gr["hg_w_out"], 1), ("ffn_w_up1", dwu1[None], 2),
                                ("ffn_w_down1", dwd1[None], 1), ("w_mod1", mm(scb, dmt1, ta=True, name="mod_dw1")[None], 2)], core, "layer1")

    dh2a, dwu0, dcw0, dcb0, dwd0 = ffn_bwd(0, dfoa, ufa, acta, h2a, zero=tok1[0:1, 0:1])
    red2, tok2 = _reduce_start([("ffn_w_up0", dwu0[None], 2), ("ffn_w_down0", dwd0[None], 1)], core, "ffn0")
    dx1a, dy1a, dn2_0, dsh2_0, dsc2_0, dg1_0 = node_bwd(cfg, dx2a, dh2a, x1a, y1a, mt[0], 2, n2[0:1] + tok2[0:1, 0:1], mt[0], 3,
                                                        name="node0b_bwd")
    dz2 = mm(dy1a, w["s5_w_out"][0], tb=True, name="s5_out_dx")
    gr["s5_w_out"] = mm(z2, dy1a, ta=True, name="s5_out_dw")[None]

    def glu_b_bwd(dz2_, yp, t):
        zg, sg = _gelu(yp), jax.nn.sigmoid(t)
        return dz2_ * zg * sg * (1.0 - sg), dz2_ * sg

    dtg, dzg_dir = rowmap(glu_b_bwd, [dz2, ypre, tg], [], [(d, MXU), (d, F32)], name="s5_glu_b_bwd")
    dzg_mm = mm(dtg, w["s5_w_glu"][0], tb=True, name="s5_glu_dx")
    gr["s5_w_glu"] = mm(zgb, dtg, ta=True, name="s5_glu_dw")[None]

    def glu_a_bwd(dzd, dzm, yp, u, ds):
        _, vjp = jax.vjp(_gelu, yp)
        (dy,) = vjp(dzd + dzm)
        return dy, dy * ds, jnp.sum(dy * u, axis=0, keepdims=True)

    dyb, du, ddsk = rowmap(glu_a_bwd, [dzg_dir, dzg_mm, ypre, u0], [dsk], [(d, MXU), (d, F32)], [(1, d)], name="s5_glu_a_bwd")
    gr["s5_d"] = ddsk
    dar, dai, dbr, dbi, dcr, dci = [], [], [], [], [], []
    for dd in range(2):
        sre, sim, ere, eim = s5s[dd][:4]
        du, gre, gim, da_r, da_i = s5_scan_bwd(cfg, dyb, *s5s[dd], du, rev=dd == 1, name=f"s5_scan_bwd{dd}")
        dar.append(colsum(da_r, name=f"s5_da_re{dd}").reshape(g_, p_))
        dai.append(colsum(da_i, name=f"s5_da_im{dd}").reshape(g_, p_))
        dbr.append(_diag_b(blockdiag_tn(u0, gre, S5_KIN, S5_KST, name=f"s5_db_re{dd}"), kb))
        dbi.append(_diag_b(blockdiag_tn(u0, gim, S5_KIN, S5_KST, name=f"s5_db_im{dd}"), kb))
        dcr.append(_diag_c(blockdiag_tn(sre.reshape(-1, ns), dyb, S5_KST, S5_KIN, name=f"s5_dc_re{dd}"), kb))
        dci.append(_diag_c(blockdiag_tn(sim.reshape(-1, ns), dyb, S5_KST, S5_KIN, scale=-1.0, name=f"s5_dc_im{dd}"), kb))
    cat = lambda l_: jnp.concatenate(l_, axis=0)
    dlr, dli, dls, dbre, dbim = s5_disc_bwd(*s5p, cat(dar), cat(dai), cat(dbr), cat(dbi), name="s5_disc_bwd")
    gr["s5_lam_re"], gr["s5_lam_im"] = dlr.reshape(1, 2, g_, p_), dli.reshape(1, 2, g_, p_)
    gr["s5_log_step"] = dls.reshape(1, 2, g_)
    gr["s5_b_re"] = dbre.reshape(1, 2, g_, S5_GROUP, p_).transpose(0, 1, 2, 4, 3)
    gr["s5_b_im"] = dbim.reshape(1, 2, g_, S5_GROUP, p_).transpose(0, 1, 2, 4, 3)
    gr["s5_c_re"], gr["s5_c_im"] = jnp.stack(dcr)[None], jnp.stack(dci)[None]
    dh1 = mm(du, w["s5_w_in"][0], tb=True, name="s5_in_dx")
    gr["s5_w_in"] = mm(h1, du, ta=True, name="s5_in_dw")[None]
    dx0, _, dn1_0, dsh1_0, dsc1_0, _ = node_bwd(cfg, dx1a, dh1, x0, None, None, 0, n1[0:1], mt[0], 0, name="node0a_bwd")

    dmt = [jnp.concatenate([dsh1_0, dsc1_0, dg1_0, dsh2_0, dsc2_0, dg2_0], axis=1), dmt1]
    gr["b_mod"] = jnp.concatenate([colsum(dmt[l], name=f"mod_db{l}") for l in range(2)], axis=0)
    dsc16 = [mm(dmt[l], w["w_mod"][l], tb=True, name=f"mod_dx{l}") for l in range(2)]
    gr["c_ctx"] = cctx_grad(c16, dsc16, name="c_ctx_grad")[0]
    gr["norm1_w"] = jnp.concatenate([dn1_0, dn1_1], axis=0)
    gr["norm2_w"] = jnp.concatenate([dn2_0, dn2_1], axis=0)
    gr["final_norm_w"] = dfnw[0]
    gr["ffn_conv_w"], gr["ffn_conv_b"] = jnp.stack([dcw0, dcw1]), jnp.stack([dcb0, dcb1])

    last = [(n, gr[n], SHARD_AXIS[n]) for n in ("s5_w_in", "s5_w_glu", "s5_w_out", "hg_lower_bounds", "ffn_conv_w")]
    last.append(("w_mod0", mm(scb, dmt[0], ta=True, name="mod_dw0")[None], 2))
    small = [n for n in WEIGHTS if n not in SHARD_AXIS]
    last.append(("small", _pack_small(gr, small), 1))
    red3, tok3 = _reduce_start(last, core, "last")
    red = _reduce_finish(red1, core, [tok3], "layer1")
    red.update(_reduce_finish(red2, core, [tok3], "ffn0"))
    red["ffn_w_up"] = jnp.concatenate([red["ffn_w_up0"], red["ffn_w_up1"]], axis=0)
    red["ffn_w_down"] = jnp.concatenate([red["ffn_w_down0"], red["ffn_w_down1"]], axis=0)
    early = ("hg_w_in", "hg_w_out", "ffn_w_up", "ffn_w_down")
    upd = {n: adamw(a[n], red[n], a["m_" + n], a["v_" + n], name="adamw_" + n) for n in early}
    grad_x = dx0[rc:].reshape(seq, nb, d).transpose(1, 0, 2)
    red.update(_reduce_finish(red3, core, [upd[n][0] for n in early] + [grad_x], "last"))
    red.update(_unpack_small(a, red["small"], small))
    red["w_mod"] = jnp.concatenate([red["w_mod0"], red["w_mod1"]], axis=0)
    loss = lax.psum(loss_p[0, 0], ("x", "y", "c"))
    upd.update({n: adamw(a[n], red[n], a["m_" + n], a["v_" + n], name="adamw_" + n) for n in WEIGHTS if n not in early})
    return (loss, grad_x, *[red[n] for n in WEIGHTS], *[upd[n][0] for n in WEIGHTS], *[upd[n][1] for n in WEIGHTS],
            *[upd[n][2] for n in WEIGHTS])
```
